```python
import math
import jax, jax.numpy as jnp
from jax import lax
import numpy as np

D_MODEL = 1024
BATCH = 8
SEQ = 2048
DEPTH = 4

CHUNK = 64
N_MIXERS = 3
N_LAYERS_A = len(range(0, DEPTH, N_MIXERS))
N_LAYERS_B = len(range(1, DEPTH, N_MIXERS))
N_LAYERS_C = len(range(2, DEPTH, N_MIXERS))
RMS_EPS = 1e-6
D_FF = 4 * D_MODEL
CONV_W = 4

GDN_DK = 128
GDN_DV = 128
GDN_HEADS = D_MODEL // GDN_DK
GDN_QK = GDN_HEADS * GDN_DK
GDN_V = GDN_HEADS * GDN_DV
GDN_CONV_CH = 2 * GDN_QK + GDN_V
GDN_IN = GDN_CONV_CH + GDN_V + 2 * GDN_HEADS

S5_GROUP = 16
S5_GROUPS = D_MODEL // S5_GROUP
S5_STATE = 64

M2_INNER = 2 * D_MODEL
M2_HEAD_DIM = 64
M2_HEADS = M2_INNER // M2_HEAD_DIM
M2_GROUPS = 8
M2_HPG = M2_HEADS // M2_GROUPS
M2_STATE = 128
M2_BC = M2_GROUPS * M2_STATE
M2_CONV_CH = M2_INNER + 2 * M2_BC
M2_IN = M2_INNER + M2_CONV_CH + M2_HEADS

kernel_name = "hybrid_gdn_s5_ssd_trunk"


def rmsnorm(x, g):
    xf = x.astype(jnp.float32)
    y = xf * lax.rsqrt(jnp.mean(xf * xf, axis=-1, keepdims=True) + RMS_EPS)
    return (y * g.astype(jnp.float32)).astype(x.dtype)


def causal_dwconv(x, w):
    return lax.conv_general_dilated(
        x, w[:, None, :], window_strides=(1,), padding=[(w.shape[0] - 1, 0)],
        dimension_numbers=("NWC", "WIO", "NWC"), feature_group_count=x.shape[-1])


def l2norm(t):
    return t * lax.rsqrt(jnp.sum(t * t, axis=-1, keepdims=True) + 1e-6)


def gated_deltanet(h, w_in, conv_w, a_log, dt_bias, o_norm_g, w_out):
    bsz, L, _ = h.shape
    nc = L // CHUNK
    f32 = jnp.float32
    proj = h @ w_in
    qkv, gate, a_raw, b_raw = jnp.split(
        proj, [GDN_CONV_CH, GDN_CONV_CH + GDN_V, GDN_CONV_CH + GDN_V + GDN_HEADS], axis=-1)
    qkv = jax.nn.silu(causal_dwconv(qkv, conv_w)).astype(f32)
    q, k, v = jnp.split(qkv, [GDN_QK, 2 * GDN_QK], axis=-1)
    q = l2norm(q.reshape(bsz, L, GDN_HEADS, GDN_DK)) * (GDN_DK ** -0.5)
    k = l2norm(k.reshape(bsz, L, GDN_HEADS, GDN_DK))
    v = v.reshape(bsz, L, GDN_HEADS, GDN_DV)
    g = -jnp.exp(a_log.astype(f32)) * jax.nn.softplus(a_raw.astype(f32) + dt_bias.astype(f32))
    beta = jax.nn.sigmoid(b_raw.astype(f32))

    def to_chunks(t):
        return t.reshape(bsz, nc, CHUNK, GDN_HEADS, -1).transpose(0, 3, 1, 2, 4)

    qc, kc, vc = to_chunks(q), to_chunks(k), to_chunks(v)
    gc = g.reshape(bsz, nc, CHUNK, GDN_HEADS).transpose(0, 3, 1, 2)
    bc = beta.reshape(bsz, nc, CHUNK, GDN_HEADS).transpose(0, 3, 1, 2)
    G = jnp.cumsum(gc, axis=-1)
    idx = jnp.arange(CHUNK)
    causal = idx[:, None] >= idx[None, :]
    strict = idx[:, None] > idx[None, :]
    decay = jnp.exp(jnp.where(causal, G[..., :, None] - G[..., None, :], -jnp.inf))
    kk = jnp.einsum('bhcid,bhcjd->bhcij', kc, kc)
    tri = jnp.where(strict, bc[..., :, None] * kk * decay, 0.0) + jnp.eye(CHUNK, dtype=f32)
    rhs = jnp.concatenate([vc * bc[..., None], kc * (bc * jnp.exp(G))[..., None]], axis=-1)
    sol = lax.linalg.triangular_solve(tri, rhs, left_side=True, lower=True, unit_diagonal=True)
    u, w = sol[..., :GDN_DV], sol[..., GDN_DV:]
    qk = jnp.einsum('bhcid,bhcjd->bhcij', qc, kc) * decay
    q_dec = qc * jnp.exp(G)[..., None]
    k_dec = kc * jnp.exp(G[..., -1:] - G)[..., None]
    chunk_decay = jnp.exp(G[..., -1])
    xs = tuple(jnp.moveaxis(t, 2, 0) for t in (u, w, qk, q_dec, k_dec, chunk_decay))

    def step(S, inp):
        u_c, w_c, qk_c, qd_c, kd_c, cd_c = inp
        v_new = u_c - jnp.einsum('bhid,bhde->bhie', w_c, S)
        o = jnp.einsum('bhid,bhde->bhie', qd_c, S) + jnp.einsum('bhij,bhje->bhie', qk_c, v_new)
        S = cd_c[..., None, None] * S + jnp.einsum('bhid,bhie->bhde', kd_c, v_new)
        return S, o

    S0 = jnp.zeros((bsz, GDN_HEADS, GDN_DK, GDN_DV), f32)
    _, o = lax.scan(step, S0, xs)
    o = o.transpose(1, 0, 3, 2, 4).reshape(bsz, L, GDN_HEADS, GDN_DV)
    o = rmsnorm(o, o_norm_g) * jax.nn.silu(gate.astype(f32).reshape(bsz, L, GDN_HEADS, GDN_DV))
    return o.reshape(bsz, L, GDN_V).astype(h.dtype) @ w_out


def s5_mixer(h, w_in, lam_re, lam_im, log_dt, b_re, b_im, c_re, c_im, d_skip, w_out):
    bsz, L, _ = h.shape
    f32 = jnp.float32
    u = (h @ w_in).astype(f32)
    ug = u.reshape(bsz, L, S5_GROUPS, S5_GROUP).transpose(1, 0, 2, 3)
    lam = lax.complex(lam_re.astype(f32), lam_im.astype(f32))
    dt = jnp.exp(log_dt.astype(f32))[:, None]
    lam_bar = jnp.exp(lam * dt)
    b_bar = ((lam_bar - 1.0) / lam)[..., None] * lax.complex(b_re.astype(f32), b_im.astype(f32))
    bu = jnp.einsum('gpk,lbgk->lbgp', b_bar, ug.astype(jnp.complex64))
    a = jnp.broadcast_to(lam_bar[None, None], (L, 1, S5_GROUPS, S5_STATE))

    def combine(e1, e2):
        a1, b1 = e1
        a2, b2 = e2
        return a1 * a2, a2 * b1 + b2

    _, states = lax.associative_scan(combine, (a, bu), axis=0)
    c = lax.complex(c_re.astype(f32), c_im.astype(f32))
    y = jnp.real(jnp.einsum('gkp,lbgp->lbgk', c, states)) \
        + d_skip.astype(f32).reshape(S5_GROUPS, S5_GROUP) * ug
    y = jax.nn.gelu(y.transpose(1, 0, 2, 3).reshape(bsz, L, D_MODEL)).astype(h.dtype)
    ag = y @ w_out
    val, gt = jnp.split(ag, 2, axis=-1)
    return val * jax.nn.sigmoid(gt)


def mamba2_mixer(h, w_in, conv_w, conv_b, dt_bias, a_log, d_skip, norm_g, w_out):
    bsz, L, _ = h.shape
    nc = L // CHUNK
    f32 = jnp.float32
    proj = h @ w_in
    z, xbc, dt_raw = jnp.split(proj, [M2_INNER, M2_INNER + M2_CONV_CH], axis=-1)
    xbc = jax.nn.silu(causal_dwconv(xbc, conv_w) + conv_b).astype(f32)
    xs, Bm, Cm = jnp.split(xbc, [M2_INNER, M2_INNER + M2_BC], axis=-1)
    x = xs.reshape(bsz, L, M2_HEADS, M2_HEAD_DIM)
    dt = jax.nn.softplus(dt_raw.astype(f32) + dt_bias.astype(f32))
    dA = dt * (-jnp.exp(a_log.astype(f32)))
    xdt = (x * dt[..., None]).reshape(bsz, nc, CHUNK, M2_GROUPS, M2_HPG, M2_HEAD_DIM)
    Bc = Bm.reshape(bsz, nc, CHUNK, M2_GROUPS, M2_STATE)
    Cc = Cm.reshape(bsz, nc, CHUNK, M2_GROUPS, M2_STATE)
    cum = jnp.cumsum(dA.reshape(bsz, nc, CHUNK, M2_GROUPS, M2_HPG), axis=2)
    idx = jnp.arange(CHUNK)
    causal = (idx[:, None] >= idx[None, :])[:, :, None, None]
    seg = cum[:, :, :, None] - cum[:, :, None, :]
    Lmat = jnp.exp(jnp.where(causal, seg, -jnp.inf))
    cb = jnp.einsum('bclgn,bcsgn->bclsg', Cc, Bc)
    y_diag = jnp.einsum('bclsgr,bcsgrp->bclgrp', cb[..., None] * Lmat, xdt)
    decay_states = jnp.exp(cum[:, :, -1:] - cum)
    states = jnp.einsum('bclgn,bclgrp->bcgrpn', Bc, xdt * decay_states[..., None])
    chunk_decay = jnp.exp(cum[:, :, -1])

    def step(S, inp):
        cd, st = inp
        return cd[..., None, None] * S + st, S

    S0 = jnp.zeros((bsz, M2_GROUPS, M2_HPG, M2_HEAD_DIM, M2_STATE), f32)
    _, S_prev = lax.scan(step, S0, (jnp.moveaxis(chunk_decay, 1, 0), jnp.moveaxis(states, 1, 0)))
    S_prev = jnp.moveaxis(S_prev, 0, 1)
    y_off = jnp.einsum('bclgn,bcgrpn->bclgrp', Cc, S_prev) * jnp.exp(cum)[..., None]
    y = (y_diag + y_off).reshape(bsz, L, M2_HEADS, M2_HEAD_DIM) + d_skip.astype(f32)[:, None] * x
    y = y.reshape(bsz, L, M2_INNER) * jax.nn.silu(z.astype(f32))
    y = rmsnorm(y.reshape(bsz, L, M2_GROUPS, M2_INNER // M2_GROUPS),
                norm_g.reshape(M2_GROUPS, M2_INNER // M2_GROUPS))
    return y.reshape(bsz, L, M2_INNER).astype(h.dtype) @ w_out


def sq_relu_mlp(h, w1, w2):
    return jnp.square(jax.nn.relu(h @ w1)) @ w2


def _inv_softplus_dt(key, shape):
    dt = jnp.exp(jax.random.uniform(key, shape, minval=math.log(1e-3), maxval=math.log(1e-1)))
    return dt + jnp.log(-jnp.expm1(-dt))


def _fwd_setup_inputs(seed: int = 0) -> dict:
    key = jax.random.key(seed)
    ks = jax.random.split(key, 32)
    nrm = jax.random.normal
    f32 = jnp.float32
    nA, nB, nC = N_LAYERS_A, N_LAYERS_B, N_LAYERS_C
    return {
        "x": nrm(ks[0], (BATCH, SEQ, D_MODEL), f32),
        "norm_mix_g": 1.0 + 0.02 * nrm(ks[1], (DEPTH, D_MODEL), f32),
        "norm_mlp_g": 1.0 + 0.02 * nrm(ks[2], (DEPTH, D_MODEL), f32),
        "mlp_w1": nrm(ks[3], (DEPTH, D_MODEL, D_FF), f32) * D_MODEL ** -0.5,
        "mlp_w2": nrm(ks[4], (DEPTH, D_FF, D_MODEL), f32) * D_FF ** -0.5,
        "gdn_w_in": nrm(ks[5], (nA, D_MODEL, GDN_IN), f32) * D_MODEL ** -0.5,
        "gdn_conv_w": nrm(ks[6], (nA, CONV_W, GDN_CONV_CH), f32) * CONV_W ** -0.5,
        "gdn_a_log": jnp.log(jax.random.uniform(ks[7], (nA, GDN_HEADS), minval=1.0, maxval=16.0)),
        "gdn_dt_bias": _inv_softplus_dt(ks[8], (nA, GDN_HEADS)),
        "gdn_o_norm_g": 1.0 + 0.02 * nrm(ks[9], (nA, GDN_DV), f32),
        "gdn_w_out": nrm(ks[10], (nA, GDN_V, D_MODEL), f32) * GDN_V ** -0.5,
        "s5_w_in": nrm(ks[11], (nB, D_MODEL, D_MODEL), f32) * D_MODEL ** -0.5,
        "s5_lam_re": -0.5 + 0.01 * nrm(ks[12], (nB, S5_GROUPS, S5_STATE), f32),
        "s5_lam_im": math.pi * jnp.arange(S5_STATE, dtype=f32) + 0.01 * nrm(ks[13], (nB, S5_GROUPS, S5_STATE), f32),
        "s5_log_dt": jax.random.uniform(ks[14], (nB, S5_GROUPS), minval=math.log(1e-3), maxval=math.log(1e-1)),
        "s5_b_re": nrm(ks[15], (nB, S5_GROUPS, S5_STATE, S5_GROUP), f32) * (2 * S5_GROUP) ** -0.5,
        "s5_b_im": nrm(ks[16], (nB, S5_GROUPS, S5_STATE, S5_GROUP), f32) * (2 * S5_GROUP) ** -0.5,
        "s5_c_re": nrm(ks[17], (nB, S5_GROUPS, S5_GROUP, S5_STATE), f32) * (2 * S5_STATE) ** -0.5 * 4.0,
        "s5_c_im": nrm(ks[18], (nB, S5_GROUPS, S5_GROUP, S5_STATE), f32) * (2 * S5_STATE) ** -0.5 * 4.0,
        "s5_d": nrm(ks[19], (nB, D_MODEL), f32),
        "s5_w_out": nrm(ks[20], (nB, D_MODEL, 2 * D_MODEL), f32) * D_MODEL ** -0.5,
        "m2_w_in": nrm(ks[21], (nC, D_MODEL, M2_IN), f32) * D_MODEL ** -0.5,
        "m2_conv_w": nrm(ks[22], (nC, CONV_W, M2_CONV_CH), f32) * CONV_W ** -0.5,
        "m2_conv_b": 0.02 * nrm(ks[23], (nC, M2_CONV_CH), f32),
        "m2_dt_bias": _inv_softplus_dt(ks[24], (nC, M2_HEADS)),
        "m2_a_log": jnp.log(jax.random.uniform(ks[25], (nC, M2_HEADS), minval=1.0, maxval=16.0)),
        "m2_d": 1.0 + 0.02 * nrm(ks[26], (nC, M2_HEADS), f32),
        "m2_norm_g": 1.0 + 0.02 * nrm(ks[27], (nC, M2_INNER), f32),
        "m2_w_out": nrm(ks[28], (nC, M2_INNER, D_MODEL), f32) * M2_INNER ** -0.5,
        "final_norm_g": 1.0 + 0.02 * nrm(ks[29], (D_MODEL,), f32),
    }


def _fwd_reference(x, norm_mix_g, norm_mlp_g, mlp_w1, mlp_w2,
              gdn_w_in, gdn_conv_w, gdn_a_log, gdn_dt_bias, gdn_o_norm_g, gdn_w_out,
              s5_w_in, s5_lam_re, s5_lam_im, s5_log_dt, s5_b_re, s5_b_im, s5_c_re, s5_c_im, s5_d, s5_w_out,
              m2_w_in, m2_conv_w, m2_conv_b, m2_dt_bias, m2_a_log, m2_d, m2_norm_g, m2_w_out,
              final_norm_g):
    h = x
    for i in range(DEPTH):
        kind, j = i % N_MIXERS, i // N_MIXERS
        hn = rmsnorm(h, norm_mix_g[i])
        if kind == 0:
            m = gated_deltanet(hn, gdn_w_in[j], gdn_conv_w[j], gdn_a_log[j], gdn_dt_bias[j],
                               gdn_o_norm_g[j], gdn_w_out[j])
        elif kind == 1:
            m = s5_mixer(hn, s5_w_in[j], s5_lam_re[j], s5_lam_im[j], s5_log_dt[j], s5_b_re[j],
                         s5_b_im[j], s5_c_re[j], s5_c_im[j], s5_d[j], s5_w_out[j])
        else:
            m = mamba2_mixer(hn, m2_w_in[j], m2_conv_w[j], m2_conv_b[j], m2_dt_bias[j], m2_a_log[j],
                             m2_d[j], m2_norm_g[j], m2_w_out[j])
        h = h + m.astype(h.dtype)
        h = h + sq_relu_mlp(rmsnorm(h, norm_mlp_g[i]), mlp_w1[i], mlp_w2[i]).astype(h.dtype)
    return rmsnorm(h, final_norm_g)


import jax as _jax
import jax.numpy as _jnp

TWIN_FORMAT = 'train_step'
FWD_PARAMS = ['x', 'norm_mix_g', 'norm_mlp_g', 'mlp_w1', 'mlp_w2', 'gdn_w_in', 'gdn_conv_w', 'gdn_a_log', 'gdn_dt_bias', 'gdn_o_norm_g', 'gdn_w_out', 's5_w_in', 's5_lam_re', 's5_lam_im', 's5_log_dt', 's5_b_re', 's5_b_im', 's5_c_re', 's5_c_im', 's5_d', 's5_w_out', 'm2_w_in', 'm2_conv_w', 'm2_conv_b', 'm2_dt_bias', 'm2_a_log', 'm2_d', 'm2_norm_g', 'm2_w_out', 'final_norm_g']
TWIN_WEIGHTS = ['norm_mix_g', 'norm_mlp_g', 'mlp_w1', 'mlp_w2', 'gdn_w_in', 'gdn_conv_w', 'gdn_a_log', 'gdn_dt_bias', 'gdn_o_norm_g', 'gdn_w_out', 's5_w_in', 's5_lam_re', 's5_lam_im', 's5_log_dt', 's5_b_re', 's5_b_im', 's5_c_re', 's5_c_im', 's5_d', 's5_w_out', 'm2_w_in', 'm2_conv_w', 'm2_conv_b', 'm2_dt_bias', 'm2_a_log', 'm2_d', 'm2_norm_g', 'm2_w_out', 'final_norm_g']
TWIN_DIFF_INPUT = 'x'
TWIN_INPUTS = ['x', 'norm_mix_g', 'norm_mlp_g', 'mlp_w1', 'mlp_w2', 'gdn_w_in', 'gdn_conv_w', 'gdn_a_log', 'gdn_dt_bias', 'gdn_o_norm_g', 'gdn_w_out', 's5_w_in', 's5_lam_re', 's5_lam_im', 's5_log_dt', 's5_b_re', 's5_b_im', 's5_c_re', 's5_c_im', 's5_d', 's5_w_out', 'm2_w_in', 'm2_conv_w', 'm2_conv_b', 'm2_dt_bias', 'm2_a_log', 'm2_d', 'm2_norm_g', 'm2_w_out', 'final_norm_g', 'loss_target', 'm_norm_mix_g', 'm_norm_mlp_g', 'm_mlp_w1', 'm_mlp_w2', 'm_gdn_w_in', 'm_gdn_conv_w', 'm_gdn_a_log', 'm_gdn_dt_bias', 'm_gdn_o_norm_g', 'm_gdn_w_out', 'm_s5_w_in', 'm_s5_lam_re', 'm_s5_lam_im', 'm_s5_log_dt', 'm_s5_b_re', 'm_s5_b_im', 'm_s5_c_re', 'm_s5_c_im', 'm_s5_d', 'm_s5_w_out', 'm_m2_w_in', 'm_m2_conv_w', 'm_m2_conv_b', 'm_m2_dt_bias', 'm_m2_a_log', 'm_m2_d', 'm_m2_norm_g', 'm_m2_w_out', 'm_final_norm_g', 'v_norm_mix_g', 'v_norm_mlp_g', 'v_mlp_w1', 'v_mlp_w2', 'v_gdn_w_in', 'v_gdn_conv_w', 'v_gdn_a_log', 'v_gdn_dt_bias', 'v_gdn_o_norm_g', 'v_gdn_w_out', 'v_s5_w_in', 'v_s5_lam_re', 'v_s5_lam_im', 'v_s5_log_dt', 'v_s5_b_re', 'v_s5_b_im', 'v_s5_c_re', 'v_s5_c_im', 'v_s5_d', 'v_s5_w_out', 'v_m2_w_in', 'v_m2_conv_w', 'v_m2_conv_b', 'v_m2_dt_bias', 'v_m2_a_log', 'v_m2_d', 'v_m2_norm_g', 'v_m2_w_out', 'v_final_norm_g']
TWIN_OUTPUTS = ['loss', 'grad_x', 'grad_norm_mix_g', 'grad_norm_mlp_g', 'grad_mlp_w1', 'grad_mlp_w2', 'grad_gdn_w_in', 'grad_gdn_conv_w', 'grad_gdn_a_log', 'grad_gdn_dt_bias', 'grad_gdn_o_norm_g', 'grad_gdn_w_out', 'grad_s5_w_in', 'grad_s5_lam_re', 'grad_s5_lam_im', 'grad_s5_log_dt', 'grad_s5_b_re', 'grad_s5_b_im', 'grad_s5_c_re', 'grad_s5_c_im', 'grad_s5_d', 'grad_s5_w_out', 'grad_m2_w_in', 'grad_m2_conv_w', 'grad_m2_conv_b', 'grad_m2_dt_bias', 'grad_m2_a_log', 'grad_m2_d', 'grad_m2_norm_g', 'grad_m2_w_out', 'grad_final_norm_g', 'delta_norm_mix_g', 'delta_norm_mlp_g', 'delta_mlp_w1', 'delta_mlp_w2', 'delta_gdn_w_in', 'delta_gdn_conv_w', 'delta_gdn_a_log', 'delta_gdn_dt_bias', 'delta_gdn_o_norm_g', 'delta_gdn_w_out', 'delta_s5_w_in', 'delta_s5_lam_re', 'delta_s5_lam_im', 'delta_s5_log_dt', 'delta_s5_b_re', 'delta_s5_b_im', 'delta_s5_c_re', 'delta_s5_c_im', 'delta_s5_d', 'delta_s5_w_out', 'delta_m2_w_in', 'delta_m2_conv_w', 'delta_m2_conv_b', 'delta_m2_dt_bias', 'delta_m2_a_log', 'delta_m2_d', 'delta_m2_norm_g', 'delta_m2_w_out', 'delta_final_norm_g', 'new_m_norm_mix_g', 'new_m_norm_mlp_g', 'new_m_mlp_w1', 'new_m_mlp_w2', 'new_m_gdn_w_in', 'new_m_gdn_conv_w', 'new_m_gdn_a_log', 'new_m_gdn_dt_bias', 'new_m_gdn_o_norm_g', 'new_m_gdn_w_out', 'new_m_s5_w_in', 'new_m_s5_lam_re', 'new_m_s5_lam_im', 'new_m_s5_log_dt', 'new_m_s5_b_re', 'new_m_s5_b_im', 'new_m_s5_c_re', 'new_m_s5_c_im', 'new_m_s5_d', 'new_m_s5_w_out', 'new_m_m2_w_in', 'new_m_m2_conv_w', 'new_m_m2_conv_b', 'new_m_m2_dt_bias', 'new_m_m2_a_log', 'new_m_m2_d', 'new_m_m2_norm_g', 'new_m_m2_w_out', 'new_m_final_norm_g', 'new_v_norm_mix_g', 'new_v_norm_mlp_g', 'new_v_mlp_w1', 'new_v_mlp_w2', 'new_v_gdn_w_in', 'new_v_gdn_conv_w', 'new_v_gdn_a_log', 'new_v_gdn_dt_bias', 'new_v_gdn_o_norm_g', 'new_v_gdn_w_out', 'new_v_s5_w_in', 'new_v_s5_lam_re', 'new_v_s5_lam_im', 'new_v_s5_log_dt', 'new_v_s5_b_re', 'new_v_s5_b_im', 'new_v_s5_c_re', 'new_v_s5_c_im', 'new_v_s5_d', 'new_v_s5_w_out', 'new_v_m2_w_in', 'new_v_m2_conv_w', 'new_v_m2_conv_b', 'new_v_m2_dt_bias', 'new_v_m2_a_log', 'new_v_m2_d', 'new_v_m2_norm_g', 'new_v_m2_w_out', 'new_v_final_norm_g']
TWIN_LEAF_KINDS = {'loss': 'loss', 'grad_x': 'grad_x', 'grad_norm_mix_g': 'grad_w', 'grad_norm_mlp_g': 'grad_w', 'grad_mlp_w1': 'grad_w', 'grad_mlp_w2': 'grad_w', 'grad_gdn_w_in': 'grad_w', 'grad_gdn_conv_w': 'grad_w', 'grad_gdn_a_log': 'grad_w', 'grad_gdn_dt_bias': 'grad_w', 'grad_gdn_o_norm_g': 'grad_w', 'grad_gdn_w_out': 'grad_w', 'grad_s5_w_in': 'grad_w', 'grad_s5_lam_re': 'grad_w', 'grad_s5_lam_im': 'grad_w', 'grad_s5_log_dt': 'grad_w', 'grad_s5_b_re': 'grad_w', 'grad_s5_b_im': 'grad_w', 'grad_s5_c_re': 'grad_w', 'grad_s5_c_im': 'grad_w', 'grad_s5_d': 'grad_w', 'grad_s5_w_out': 'grad_w', 'grad_m2_w_in': 'grad_w', 'grad_m2_conv_w': 'grad_w', 'grad_m2_conv_b': 'grad_w', 'grad_m2_dt_bias': 'grad_w', 'grad_m2_a_log': 'grad_w', 'grad_m2_d': 'grad_w', 'grad_m2_norm_g': 'grad_w', 'grad_m2_w_out': 'grad_w', 'grad_final_norm_g': 'grad_w', 'delta_norm_mix_g': 'delta_w', 'delta_norm_mlp_g': 'delta_w', 'delta_mlp_w1': 'delta_w', 'delta_mlp_w2': 'delta_w', 'delta_gdn_w_in': 'delta_w', 'delta_gdn_conv_w': 'delta_w', 'delta_gdn_a_log': 'delta_w', 'delta_gdn_dt_bias': 'delta_w', 'delta_gdn_o_norm_g': 'delta_w', 'delta_gdn_w_out': 'delta_w', 'delta_s5_w_in': 'delta_w', 'delta_s5_lam_re': 'delta_w', 'delta_s5_lam_im': 'delta_w', 'delta_s5_log_dt': 'delta_w', 'delta_s5_b_re': 'delta_w', 'delta_s5_b_im': 'delta_w', 'delta_s5_c_re': 'delta_w', 'delta_s5_c_im': 'delta_w', 'delta_s5_d': 'delta_w', 'delta_s5_w_out': 'delta_w', 'delta_m2_w_in': 'delta_w', 'delta_m2_conv_w': 'delta_w', 'delta_m2_conv_b': 'delta_w', 'delta_m2_dt_bias': 'delta_w', 'delta_m2_a_log': 'delta_w', 'delta_m2_d': 'delta_w', 'delta_m2_norm_g': 'delta_w', 'delta_m2_w_out': 'delta_w', 'delta_final_norm_g': 'delta_w', 'new_m_norm_mix_g': 'new_m', 'new_m_norm_mlp_g': 'new_m', 'new_m_mlp_w1': 'new_m', 'new_m_mlp_w2': 'new_m', 'new_m_gdn_w_in': 'new_m', 'new_m_gdn_conv_w': 'new_m', 'new_m_gdn_a_log': 'new_m', 'new_m_gdn_dt_bias': 'new_m', 'new_m_gdn_o_norm_g': 'new_m', 'new_m_gdn_w_out': 'new_m', 'new_m_s5_w_in': 'new_m', 'new_m_s5_lam_re': 'new_m', 'new_m_s5_lam_im': 'new_m', 'new_m_s5_log_dt': 'new_m', 'new_m_s5_b_re': 'new_m', 'new_m_s5_b_im': 'new_m', 'new_m_s5_c_re': 'new_m', 'new_m_s5_c_im': 'new_m', 'new_m_s5_d': 'new_m', 'new_m_s5_w_out': 'new_m', 'new_m_m2_w_in': 'new_m', 'new_m_m2_conv_w': 'new_m', 'new_m_m2_conv_b': 'new_m', 'new_m_m2_dt_bias': 'new_m', 'new_m_m2_a_log': 'new_m', 'new_m_m2_d': 'new_m', 'new_m_m2_norm_g': 'new_m', 'new_m_m2_w_out': 'new_m', 'new_m_final_norm_g': 'new_m', 'new_v_norm_mix_g': 'new_v', 'new_v_norm_mlp_g': 'new_v', 'new_v_mlp_w1': 'new_v', 'new_v_mlp_w2': 'new_v', 'new_v_gdn_w_in': 'new_v', 'new_v_gdn_conv_w': 'new_v', 'new_v_gdn_a_log': 'new_v', 'new_v_gdn_dt_bias': 'new_v', 'new_v_gdn_o_norm_g': 'new_v', 'new_v_gdn_w_out': 'new_v', 'new_v_s5_w_in': 'new_v', 'new_v_s5_lam_re': 'new_v', 'new_v_s5_lam_im': 'new_v', 'new_v_s5_log_dt': 'new_v', 'new_v_s5_b_re': 'new_v', 'new_v_s5_b_im': 'new_v', 'new_v_s5_c_re': 'new_v', 'new_v_s5_c_im': 'new_v', 'new_v_s5_d': 'new_v', 'new_v_s5_w_out': 'new_v', 'new_v_m2_w_in': 'new_v', 'new_v_m2_conv_w': 'new_v', 'new_v_m2_conv_b': 'new_v', 'new_v_m2_dt_bias': 'new_v', 'new_v_m2_a_log': 'new_v', 'new_v_m2_d': 'new_v', 'new_v_m2_norm_g': 'new_v', 'new_v_m2_w_out': 'new_v', 'new_v_final_norm_g': 'new_v'}


def _forward(args):
    return _fwd_reference(*[args[k] for k in FWD_PARAMS])


def _output_shape():
    out = _jax.eval_shape(lambda: _forward(_fwd_setup_inputs(0)))
    return out.shape, out.dtype

N_MICROBATCH = 1
ADAM_LR = 0.001
ADAM_B1 = 0.9
ADAM_B2 = 0.999
ADAM_EPS = 1e-08
ADAM_WD = 0.01
ADAM_STEP = 10
PER_EXAMPLE_BATCH_AXIS = {'x': 0, 'loss_target': 0}
SHARED_INPUTS = []
_WEIGHT_DTYPES = {'norm_mix_g': _jnp.float32, 'norm_mlp_g': _jnp.float32, 'mlp_w1': _jnp.float32, 'mlp_w2': _jnp.float32, 'gdn_w_in': _jnp.float32, 'gdn_conv_w': _jnp.float32, 'gdn_a_log': _jnp.float32, 'gdn_dt_bias': _jnp.float32, 'gdn_o_norm_g': _jnp.float32, 'gdn_w_out': _jnp.float32, 's5_w_in': _jnp.float32, 's5_lam_re': _jnp.float32, 's5_lam_im': _jnp.float32, 's5_log_dt': _jnp.float32, 's5_b_re': _jnp.float32, 's5_b_im': _jnp.float32, 's5_c_re': _jnp.float32, 's5_c_im': _jnp.float32, 's5_d': _jnp.float32, 's5_w_out': _jnp.float32, 'm2_w_in': _jnp.float32, 'm2_conv_w': _jnp.float32, 'm2_conv_b': _jnp.float32, 'm2_dt_bias': _jnp.float32, 'm2_a_log': _jnp.float32, 'm2_d': _jnp.float32, 'm2_norm_g': _jnp.float32, 'm2_w_out': _jnp.float32, 'final_norm_g': _jnp.float32}
MOMENT_SCALE = {'norm_mix_g': 9.349665e-02, 'norm_mlp_g': 1.003088e-01, 'mlp_w1': 4.970894e-02, 'mlp_w2': 9.313434e-02, 'gdn_w_in': 5.688244e-02, 'gdn_conv_w': 5.434398e-02, 'gdn_a_log': 3.364942e-01, 'gdn_dt_bias': 3.283687e-01, 'gdn_o_norm_g': 2.050481e-01, 'gdn_w_out': 6.830080e-02, 's5_w_in': 3.811451e-02, 's5_lam_re': 1.048132e-02, 's5_lam_im': 1.256014e-02, 's5_log_dt': 4.994287e+00, 's5_b_re': 6.891111e-03, 's5_b_im': 7.062623e-03, 's5_c_re': 3.037019e-03, 's5_c_im': 2.996141e-03, 's5_d': 3.932513e-02, 's5_w_out': 2.742872e-02, 'm2_w_in': 3.751679e-02, 'm2_conv_w': 3.261909e-02, 'm2_conv_b': 4.154263e-02, 'm2_dt_bias': 6.755557e-02, 'm2_a_log': 1.007341e-01, 'm2_d': 2.215989e-01, 'm2_norm_g': 4.430735e-02, 'm2_w_out': 6.264236e-02, 'final_norm_g': 1.634779e+01}


def _to_microbatches(a, axis):
    t = _jnp.moveaxis(a, axis, 0)
    t = t.reshape((N_MICROBATCH, t.shape[0] // N_MICROBATCH) + t.shape[1:])
    return _jnp.moveaxis(t, 1, axis + 1)


def setup_inputs(seed: int = 0) -> dict:
    inp = _fwd_setup_inputs(seed)
    key = _jax.random.fold_in(_jax.random.key(seed), 7919)
    shape, _ = _output_shape()
    out = dict(inp)
    out["loss_target"] = _jax.random.normal(_jax.random.fold_in(key, 0), shape, _jnp.float32)
    for i, name in enumerate(TWIN_WEIGHTS):
        w = inp[name].astype(_jnp.float32)
        if MOMENT_SCALE is None:
            s = _jnp.sqrt(_jnp.mean(_jnp.square(w)) + 1e-30)
        else:
            s = MOMENT_SCALE[name]
        km, kv = _jax.random.split(_jax.random.fold_in(key, i + 1))
        out[name] = w
        out["m_" + name] = s * _jax.random.normal(km, w.shape, _jnp.float32)
        out["v_" + name] = (s * s) * _jax.random.uniform(kv, w.shape, _jnp.float32, 0.5, 1.5)
    if N_MICROBATCH > 1:
        for name, axis in PER_EXAMPLE_BATCH_AXIS.items():
            out[name] = _to_microbatches(out[name], axis)
    return {'x': out['x'], 'norm_mix_g': out['norm_mix_g'], 'norm_mlp_g': out['norm_mlp_g'], 'mlp_w1': out['mlp_w1'], 'mlp_w2': out['mlp_w2'], 'gdn_w_in': out['gdn_w_in'], 'gdn_conv_w': out['gdn_conv_w'], 'gdn_a_log': out['gdn_a_log'], 'gdn_dt_bias': out['gdn_dt_bias'], 'gdn_o_norm_g': out['gdn_o_norm_g'], 'gdn_w_out': out['gdn_w_out'], 's5_w_in': out['s5_w_in'], 's5_lam_re': out['s5_lam_re'], 's5_lam_im': out['s5_lam_im'], 's5_log_dt': out['s5_log_dt'], 's5_b_re': out['s5_b_re'], 's5_b_im': out['s5_b_im'], 's5_c_re': out['s5_c_re'], 's5_c_im': out['s5_c_im'], 's5_d': out['s5_d'], 's5_w_out': out['s5_w_out'], 'm2_w_in': out['m2_w_in'], 'm2_conv_w': out['m2_conv_w'], 'm2_conv_b': out['m2_conv_b'], 'm2_dt_bias': out['m2_dt_bias'], 'm2_a_log': out['m2_a_log'], 'm2_d': out['m2_d'], 'm2_norm_g': out['m2_norm_g'], 'm2_w_out': out['m2_w_out'], 'final_norm_g': out['final_norm_g'], 'loss_target': out['loss_target'], 'm_norm_mix_g': out['m_norm_mix_g'], 'm_norm_mlp_g': out['m_norm_mlp_g'], 'm_mlp_w1': out['m_mlp_w1'], 'm_mlp_w2': out['m_mlp_w2'], 'm_gdn_w_in': out['m_gdn_w_in'], 'm_gdn_conv_w': out['m_gdn_conv_w'], 'm_gdn_a_log': out['m_gdn_a_log'], 'm_gdn_dt_bias': out['m_gdn_dt_bias'], 'm_gdn_o_norm_g': out['m_gdn_o_norm_g'], 'm_gdn_w_out': out['m_gdn_w_out'], 'm_s5_w_in': out['m_s5_w_in'], 'm_s5_lam_re': out['m_s5_lam_re'], 'm_s5_lam_im': out['m_s5_lam_im'], 'm_s5_log_dt': out['m_s5_log_dt'], 'm_s5_b_re': out['m_s5_b_re'], 'm_s5_b_im': out['m_s5_b_im'], 'm_s5_c_re': out['m_s5_c_re'], 'm_s5_c_im': out['m_s5_c_im'], 'm_s5_d': out['m_s5_d'], 'm_s5_w_out': out['m_s5_w_out'], 'm_m2_w_in': out['m_m2_w_in'], 'm_m2_conv_w': out['m_m2_conv_w'], 'm_m2_conv_b': out['m_m2_conv_b'], 'm_m2_dt_bias': out['m_m2_dt_bias'], 'm_m2_a_log': out['m_m2_a_log'], 'm_m2_d': out['m_m2_d'], 'm_m2_norm_g': out['m_m2_norm_g'], 'm_m2_w_out': out['m_m2_w_out'], 'm_final_norm_g': out['m_final_norm_g'], 'v_norm_mix_g': out['v_norm_mix_g'], 'v_norm_mlp_g': out['v_norm_mlp_g'], 'v_mlp_w1': out['v_mlp_w1'], 'v_mlp_w2': out['v_mlp_w2'], 'v_gdn_w_in': out['v_gdn_w_in'], 'v_gdn_conv_w': out['v_gdn_conv_w'], 'v_gdn_a_log': out['v_gdn_a_log'], 'v_gdn_dt_bias': out['v_gdn_dt_bias'], 'v_gdn_o_norm_g': out['v_gdn_o_norm_g'], 'v_gdn_w_out': out['v_gdn_w_out'], 'v_s5_w_in': out['v_s5_w_in'], 'v_s5_lam_re': out['v_s5_lam_re'], 'v_s5_lam_im': out['v_s5_lam_im'], 'v_s5_log_dt': out['v_s5_log_dt'], 'v_s5_b_re': out['v_s5_b_re'], 'v_s5_b_im': out['v_s5_b_im'], 'v_s5_c_re': out['v_s5_c_re'], 'v_s5_c_im': out['v_s5_c_im'], 'v_s5_d': out['v_s5_d'], 'v_s5_w_out': out['v_s5_w_out'], 'v_m2_w_in': out['v_m2_w_in'], 'v_m2_conv_w': out['v_m2_conv_w'], 'v_m2_conv_b': out['v_m2_conv_b'], 'v_m2_dt_bias': out['v_m2_dt_bias'], 'v_m2_a_log': out['v_m2_a_log'], 'v_m2_d': out['v_m2_d'], 'v_m2_norm_g': out['v_m2_norm_g'], 'v_m2_w_out': out['v_m2_w_out'], 'v_final_norm_g': out['v_final_norm_g']}


def _loss(weights, diff, rest, loss_target):
    with _jax.named_scope("forward"):
        args = {**rest, TWIN_DIFF_INPUT: diff, **{k: w.astype(_WEIGHT_DTYPES[k]) for k, w in weights.items()}}
        y = _forward(args)
    with _jax.named_scope("loss_head"):
        err = _jnp.square(y.astype(_jnp.float32) - loss_target)
        return 0.5 * _jnp.sum(_jnp.mean(err, axis=-1)) if err.ndim else 0.5 * err


def _adamw(w, g, m, v):
    m = ADAM_B1 * m + (1.0 - ADAM_B1) * g
    v = ADAM_B2 * v + (1.0 - ADAM_B2) * _jnp.square(g)
    m_hat = m / (1.0 - ADAM_B1 ** ADAM_STEP)
    v_hat = v / (1.0 - ADAM_B2 ** ADAM_STEP)
    delta = -ADAM_LR * (m_hat / (_jnp.sqrt(v_hat) + ADAM_EPS) + ADAM_WD * w)
    return delta, m, v


def reference(x, norm_mix_g, norm_mlp_g, mlp_w1, mlp_w2, gdn_w_in, gdn_conv_w, gdn_a_log, gdn_dt_bias, gdn_o_norm_g, gdn_w_out, s5_w_in, s5_lam_re, s5_lam_im, s5_log_dt, s5_b_re, s5_b_im, s5_c_re, s5_c_im, s5_d, s5_w_out, m2_w_in, m2_conv_w, m2_conv_b, m2_dt_bias, m2_a_log, m2_d, m2_norm_g, m2_w_out, final_norm_g, loss_target, m_norm_mix_g, m_norm_mlp_g, m_mlp_w1, m_mlp_w2, m_gdn_w_in, m_gdn_conv_w, m_gdn_a_log, m_gdn_dt_bias, m_gdn_o_norm_g, m_gdn_w_out, m_s5_w_in, m_s5_lam_re, m_s5_lam_im, m_s5_log_dt, m_s5_b_re, m_s5_b_im, m_s5_c_re, m_s5_c_im, m_s5_d, m_s5_w_out, m_m2_w_in, m_m2_conv_w, m_m2_conv_b, m_m2_dt_bias, m_m2_a_log, m_m2_d, m_m2_norm_g, m_m2_w_out, m_final_norm_g, v_norm_mix_g, v_norm_mlp_g, v_mlp_w1, v_mlp_w2, v_gdn_w_in, v_gdn_conv_w, v_gdn_a_log, v_gdn_dt_bias, v_gdn_o_norm_g, v_gdn_w_out, v_s5_w_in, v_s5_lam_re, v_s5_lam_im, v_s5_log_dt, v_s5_b_re, v_s5_b_im, v_s5_c_re, v_s5_c_im, v_s5_d, v_s5_w_out, v_m2_w_in, v_m2_conv_w, v_m2_conv_b, v_m2_dt_bias, v_m2_a_log, v_m2_d, v_m2_norm_g, v_m2_w_out, v_final_norm_g):
    given = dict(x=x, norm_mix_g=norm_mix_g, norm_mlp_g=norm_mlp_g, mlp_w1=mlp_w1, mlp_w2=mlp_w2, gdn_w_in=gdn_w_in, gdn_conv_w=gdn_conv_w, gdn_a_log=gdn_a_log, gdn_dt_bias=gdn_dt_bias, gdn_o_norm_g=gdn_o_norm_g, gdn_w_out=gdn_w_out, s5_w_in=s5_w_in, s5_lam_re=s5_lam_re, s5_lam_im=s5_lam_im, s5_log_dt=s5_log_dt, s5_b_re=s5_b_re, s5_b_im=s5_b_im, s5_c_re=s5_c_re, s5_c_im=s5_c_im, s5_d=s5_d, s5_w_out=s5_w_out, m2_w_in=m2_w_in, m2_conv_w=m2_conv_w, m2_conv_b=m2_conv_b, m2_dt_bias=m2_dt_bias, m2_a_log=m2_a_log, m2_d=m2_d, m2_norm_g=m2_norm_g, m2_w_out=m2_w_out, final_norm_g=final_norm_g, loss_target=loss_target, m_norm_mix_g=m_norm_mix_g, m_norm_mlp_g=m_norm_mlp_g, m_mlp_w1=m_mlp_w1, m_mlp_w2=m_mlp_w2, m_gdn_w_in=m_gdn_w_in, m_gdn_conv_w=m_gdn_conv_w, m_gdn_a_log=m_gdn_a_log, m_gdn_dt_bias=m_gdn_dt_bias, m_gdn_o_norm_g=m_gdn_o_norm_g, m_gdn_w_out=m_gdn_w_out, m_s5_w_in=m_s5_w_in, m_s5_lam_re=m_s5_lam_re, m_s5_lam_im=m_s5_lam_im, m_s5_log_dt=m_s5_log_dt, m_s5_b_re=m_s5_b_re, m_s5_b_im=m_s5_b_im, m_s5_c_re=m_s5_c_re, m_s5_c_im=m_s5_c_im, m_s5_d=m_s5_d, m_s5_w_out=m_s5_w_out, m_m2_w_in=m_m2_w_in, m_m2_conv_w=m_m2_conv_w, m_m2_conv_b=m_m2_conv_b, m_m2_dt_bias=m_m2_dt_bias, m_m2_a_log=m_m2_a_log, m_m2_d=m_m2_d, m_m2_norm_g=m_m2_norm_g, m_m2_w_out=m_m2_w_out, m_final_norm_g=m_final_norm_g, v_norm_mix_g=v_norm_mix_g, v_norm_mlp_g=v_norm_mlp_g, v_mlp_w1=v_mlp_w1, v_mlp_w2=v_mlp_w2, v_gdn_w_in=v_gdn_w_in, v_gdn_conv_w=v_gdn_conv_w, v_gdn_a_log=v_gdn_a_log, v_gdn_dt_bias=v_gdn_dt_bias, v_gdn_o_norm_g=v_gdn_o_norm_g, v_gdn_w_out=v_gdn_w_out, v_s5_w_in=v_s5_w_in, v_s5_lam_re=v_s5_lam_re, v_s5_lam_im=v_s5_lam_im, v_s5_log_dt=v_s5_log_dt, v_s5_b_re=v_s5_b_re, v_s5_b_im=v_s5_b_im, v_s5_c_re=v_s5_c_re, v_s5_c_im=v_s5_c_im, v_s5_d=v_s5_d, v_s5_w_out=v_s5_w_out, v_m2_w_in=v_m2_w_in, v_m2_conv_w=v_m2_conv_w, v_m2_conv_b=v_m2_conv_b, v_m2_dt_bias=v_m2_dt_bias, v_m2_a_log=v_m2_a_log, v_m2_d=v_m2_d, v_m2_norm_g=v_m2_norm_g, v_m2_w_out=v_m2_w_out, v_final_norm_g=v_final_norm_g)
    weights = {n: given[n] for n in TWIN_WEIGHTS}
    shared = {n: given[n] for n in SHARED_INPUTS}
    per_example = {n: given[n] for n in ['x']}
    grad_fn = _jax.value_and_grad(_loss, argnums=(0, 1))

    def one_microbatch(ex, loss_target):
        ex = dict(ex)
        diff = ex.pop(TWIN_DIFF_INPUT)
        return grad_fn(weights, diff, {**shared, **ex}, loss_target)

    if N_MICROBATCH == 1:
        loss, (grad_w, grad_x) = one_microbatch(per_example, given["loss_target"])
    else:
        def body(carry, xs):
            loss_sum, grad_sum = carry
            l_k, (gw_k, gx_k) = one_microbatch(xs[0], xs[1])
            with _jax.named_scope("update"):
                return (loss_sum + l_k, _jax.tree.map(_jnp.add, grad_sum, gw_k)), gx_k

        init = (_jnp.zeros((), _jnp.float32), _jax.tree.map(_jnp.zeros_like, weights))
        (loss, grad_w), grad_x = _jax.lax.scan(body, init, (per_example, given["loss_target"]))
    with _jax.named_scope("update"):
        delta_w, new_m, new_v = {}, {}, {}
        for n in TWIN_WEIGHTS:
            delta_w[n], new_m[n], new_v[n] = _adamw(weights[n], grad_w[n], given["m_" + n], given["v_" + n])
    return (loss, grad_x, *[grad_w[n] for n in TWIN_WEIGHTS], *[delta_w[n] for n in TWIN_WEIGHTS],
            *[new_m[n] for n in TWIN_WEIGHTS], *[new_v[n] for n in TWIN_WEIGHTS])
```

```python
import functools
import math

import numpy as np
import jax
import jax.numpy as jnp
from jax import lax
from jax.experimental import pallas as pl
from jax.experimental.pallas import tpu as pltpu

F32 = jnp.float32
BF16 = jnp.bfloat16
HIGHEST = lax.Precision.HIGHEST

D_MODEL = 1024
D_FF = 4096
DEPTH = 4
CHUNK = 64
RMS_EPS = 1e-6
CONV_W = 4
GDN_HEADS = 8
GDN_DK = 128
GDN_IN = 4112
GDN_IN_PAD = 4224
S5_GROUPS = 64
S5_STATE = 64
S5_GROUP = 16
S5_BLOCKS = 8
M2_INNER = 2048
M2_HEADS = 32
M2_GROUPS = 8
M2_STATE = 128
M2_CONV_CH = 4096
M2_IN = 6176
M2_IN_PAD = 6272
ADAM_LR, ADAM_B1, ADAM_B2, ADAM_EPS, ADAM_WD, ADAM_STEP = 0.001, 0.9, 0.999, 1e-08, 0.01, 10

VMEM_LIMIT_BYTES = 56 * 1024 * 1024
SUBLANES = 8
LANES = 128


def _params(*sem):
    return pltpu.CompilerParams(dimension_semantics=tuple(sem) if sem else None, vmem_limit_bytes=VMEM_LIMIT_BYTES)


def _dot(a, b, dims=((1,), (0,))):
    return lax.dot_general(a.astype(BF16), b.astype(BF16), (dims, ((), ())), preferred_element_type=F32)


def _nt(a, b):
    return _dot(a, b, ((1,), (1,)))


def _tn(a, b):
    return _dot(a, b, ((0,), (0,)))


def _hi(a, b):
    return lax.dot_general(a, b, (((1,), (0,)), ((), ())), precision=HIGHEST, preferred_element_type=F32)


def _dot3(a, b, dims=((1,), (0,))):
    ah, bh = a.astype(BF16), b.astype(BF16)
    al, bl = (a - ah.astype(F32)).astype(BF16), (b - bh.astype(F32)).astype(BF16)
    f = lambda p, q: lax.dot_general(p, q, (dims, ((), ())), preferred_element_type=F32)
    return f(ah, bh) + (f(ah, bl) + f(al, bh))


def _neumann(x, r, dims):
    r = r + _dot3(x, r, dims)
    for _ in range(5):
        x = _dot3(x, x)
        r = r + _dot3(x, r, dims)
    return r


@jax.custom_vjp
def _unit_lower_solve(a, rhs):
    return _neumann(-a, rhs, ((1,), (0,)))


def _unit_lower_solve_fwd(a, rhs):
    sol = _neumann(-a, rhs, ((1,), (0,)))
    return sol, (a, sol)


def _unit_lower_solve_bwd(res, ct):
    a, sol = res
    d_rhs = _neumann(-a, ct, ((0,), (0,)))
    return -_dot3(d_rhs, sol, ((1,), (1,))), d_rhs


_unit_lower_solve.defvjp(_unit_lower_solve_fwd, _unit_lower_solve_bwd)


def _sigmoid(x):
    return 1.0 / (1.0 + jnp.exp(-x))


def _softplus(x):
    return jnp.maximum(x, 0.0) + jnp.log(1.0 + jnp.exp(-jnp.abs(x)))


def _iota2(shape, axis):
    return lax.broadcasted_iota(jnp.int32, shape, axis)


def _tile(n, cands):
    for c in cands:
        if n % c == 0:
            return c
    return n


def _mm(name, a, b, mode, out_dtypes, epi=None, extras=(), tm=512, tn=None):
    if mode == "nn":
        (M, K), N = a.shape, b.shape[1]
    elif mode == "nt":
        (M, K), N = a.shape, b.shape[0]
    else:
        (K, M), N = a.shape, b.shape[1]
    tm = _tile(M, (tm, 256, 128))
    tn = tn or _tile(N, (512, 384, 896, 256, 128))
    if mode == "nn":
        a_spec, b_spec = pl.BlockSpec((tm, K), lambda i, j: (i, 0)), pl.BlockSpec((K, tn), lambda i, j: (0, j))
        dims = ((1,), (0,))
    elif mode == "nt":
        a_spec, b_spec = pl.BlockSpec((tm, K), lambda i, j: (i, 0)), pl.BlockSpec((tn, K), lambda i, j: (j, 0))
        dims = ((1,), (1,))
    else:
        a_spec, b_spec = pl.BlockSpec((K, tm), lambda i, j: (0, i)), pl.BlockSpec((K, tn), lambda i, j: (0, j))
        dims = ((0,), (0,))
    n_ex = len(extras)

    def body(a_ref, b_ref, *rest):
        acc = _dot(a_ref[...], b_ref[...], dims)
        res = epi(acc, *[e[...] for e in rest[:n_ex]]) if epi is not None else (acc,)
        for o_ref, r in zip(rest[n_ex:], res):
            o_ref[...] = r.astype(o_ref.dtype)

    tile = pl.BlockSpec((tm, tn), lambda i, j: (i, j))
    out = pl.pallas_call(
        body, name=name, grid=(M // tm, N // tn),
        in_specs=[a_spec, b_spec] + [tile] * n_ex,
        out_specs=[tile] * len(out_dtypes),
        out_shape=[jax.ShapeDtypeStruct((M, N), d) for d in out_dtypes],
        compiler_params=_params("parallel", "parallel"),
    )(a, b, *extras)
    return out if len(out_dtypes) > 1 else out[0]


def _rms_fwd(name, h, g):
    L, D = h.shape
    tr = _tile(L, (256, 128))

    def body(h_ref, g_ref, o_ref):
        x = h_ref[...]
        r = lax.rsqrt(jnp.mean(x * x, axis=-1, keepdims=True) + RMS_EPS)
        o_ref[...] = (x * r * g_ref[...]).astype(o_ref.dtype)

    return pl.pallas_call(
        body, name=name, grid=(L // tr,),
        in_specs=[pl.BlockSpec((tr, D), lambda i: (i, 0)), pl.BlockSpec((1, D), lambda i: (0, 0))],
        out_specs=pl.BlockSpec((tr, D), lambda i: (i, 0)),
        out_shape=jax.ShapeDtypeStruct((L, D), BF16),
        compiler_params=_params("parallel"),
    )(h, g.reshape(1, D))


def _rms_bwd(name, h, g, dhn, dres):
    L, D = h.shape
    tr = _tile(L, (256, 128))

    def body(h_ref, g_ref, dhn_ref, dres_ref, dh_ref, dg_ref):
        x = h_ref[...]
        r = lax.rsqrt(jnp.mean(x * x, axis=-1, keepdims=True) + RMS_EPS)
        xh = x * r
        dy = dhn_ref[...]
        dxh = dy * g_ref[...]
        dh_ref[...] = dres_ref[...] + r * (dxh - xh * jnp.mean(dxh * xh, axis=-1, keepdims=True))

        @pl.when(pl.program_id(0) == 0)
        def _():
            dg_ref[...] = jnp.zeros_like(dg_ref)

        dg_ref[...] += jnp.sum(dy * xh, axis=0, keepdims=True)

    row = pl.BlockSpec((tr, D), lambda i: (i, 0))
    vec = pl.BlockSpec((1, D), lambda i: (0, 0))
    return pl.pallas_call(
        body, name=name, grid=(L // tr,),
        in_specs=[row, vec, row, row], out_specs=[row, vec],
        out_shape=[jax.ShapeDtypeStruct((L, D), F32), jax.ShapeDtypeStruct((1, D), F32)],
        compiler_params=_params("arbitrary"),
    )(h, g.reshape(1, D), dhn, dres)


def _loss_head(h, g, target):
    L, D = h.shape
    tr = _tile(L, (256, 128))

    def body(h_ref, g_ref, t_ref, loss_ref, dh_ref, dg_ref):
        x = h_ref[...]
        r = lax.rsqrt(jnp.mean(x * x, axis=-1, keepdims=True) + RMS_EPS)
        xh = x * r
        err = xh * g_ref[...] - t_ref[...]
        dy = err * (1.0 / D)
        dxh = dy * g_ref[...]
        dh_ref[...] = r * (dxh - xh * jnp.mean(dxh * xh, axis=-1, keepdims=True))

        @pl.when(pl.program_id(0) == 0)
        def _():
            dg_ref[...] = jnp.zeros_like(dg_ref)
            loss_ref[...] = jnp.zeros_like(loss_ref)

        dg_ref[...] += jnp.sum(dy * xh, axis=0, keepdims=True)
        loss_ref[...] += (0.5 / D) * jnp.sum(jnp.sum(err * err, axis=-1, keepdims=True), axis=0, keepdims=True)

    row = pl.BlockSpec((tr, D), lambda i: (i, 0))
    vec = pl.BlockSpec((1, D), lambda i: (0, 0))
    return pl.pallas_call(
        body, name="loss_head", grid=(L // tr,),
        in_specs=[row, vec, row], out_specs=[pl.BlockSpec((1, 1), lambda i: (0, 0)), row, vec],
        out_shape=[jax.ShapeDtypeStruct((1, 1), F32), jax.ShapeDtypeStruct((L, D), F32), jax.ShapeDtypeStruct((1, D), F32)],
        compiler_params=_params("arbitrary"),
    )(h, g.reshape(1, D), target)


def _glu_fwd(h, ag):
    L, D = h.shape
    tr = _tile(L, (256, 128))

    def body(h_ref, v_ref, g_ref, o_ref):
        o_ref[...] = h_ref[...] + v_ref[...] * _sigmoid(g_ref[...])

    return pl.pallas_call(
        body, name="s5_glu_fwd", grid=(L // tr,),
        in_specs=[pl.BlockSpec((tr, D), lambda i: (i, 0)), pl.BlockSpec((tr, D), lambda i: (i, 0)),
                  pl.BlockSpec((tr, D), lambda i: (i, 1))],
        out_specs=pl.BlockSpec((tr, D), lambda i: (i, 0)),
        out_shape=jax.ShapeDtypeStruct((L, D), F32),
        compiler_params=_params("parallel"),
    )(h, ag, ag)


def _glu_bwd(dh, ag):
    L, D = dh.shape
    tr = _tile(L, (256, 128))

    def body(dh_ref, v_ref, g_ref, dv_ref, dg_ref):
        s = _sigmoid(g_ref[...])
        d = dh_ref[...]
        dv_ref[...] = d * s
        dg_ref[...] = d * v_ref[...] * s * (1.0 - s)

    dv, dg = pl.pallas_call(
        body, name="s5_glu_bwd", grid=(L // tr,),
        in_specs=[pl.BlockSpec((tr, D), lambda i: (i, 0)), pl.BlockSpec((tr, D), lambda i: (i, 0)),
                  pl.BlockSpec((tr, D), lambda i: (i, 1))],
        out_specs=[pl.BlockSpec((tr, D), lambda i: (i, 0))] * 2,
        out_shape=[jax.ShapeDtypeStruct((L, D), F32)] * 2,
        compiler_params=_params("parallel"),
    )(dh, ag, ag)
    return jnp.concatenate([dv, dg], axis=1)


CONV_ROWS = 128
CONV_COLS = 512


def _shift_rows(cat, s):
    if s == 0:
        return cat[SUBLANES:, :]
    return pltpu.roll(cat, s, axis=0)[SUBLANES:, :]


def _conv_fwd(name, p, col0, w, b):
    L = p.shape[0]
    C = w.shape[1]
    tc = _tile(C, (CONV_COLS, 256))
    cb0 = col0 // tc
    nr = L // CONV_ROWS

    def body(x_ref, w_ref, b_ref, o_ref):
        def step(r, carry):
            r0 = pl.multiple_of(r * CONV_ROWS, CONV_ROWS)
            cur = x_ref[pl.ds(r0, CONV_ROWS), :]
            p0 = pl.multiple_of(jnp.maximum(r0 - SUBLANES, 0), SUBLANES)
            prev = jnp.where(r > 0, x_ref[pl.ds(p0, SUBLANES), :], 0.0)
            cat = jnp.concatenate([prev, cur], axis=0)
            acc = b_ref[...] + w_ref[3:4, :] * cur
            for k in range(CONV_W - 1):
                acc = acc + w_ref[k:k + 1, :] * _shift_rows(cat, CONV_W - 1 - k)
            o_ref[pl.ds(r0, CONV_ROWS), :] = acc * _sigmoid(acc)
            return carry

        lax.fori_loop(0, nr, step, 0)

    return pl.pallas_call(
        body, name=name, grid=(C // tc,),
        in_specs=[pl.BlockSpec((L, tc), lambda j: (0, cb0 + j)), pl.BlockSpec((CONV_W, tc), lambda j: (0, j)),
                  pl.BlockSpec((1, tc), lambda j: (0, j))],
        out_specs=pl.BlockSpec((L, tc), lambda j: (0, j)),
        out_shape=jax.ShapeDtypeStruct((L, C), F32),
        compiler_params=_params("parallel"),
    )(p, w, b)


def _conv_bwd(name, p, col0, w, b, dout):
    L = p.shape[0]
    C = w.shape[1]
    tc = _tile(C, (CONV_COLS, 256))
    cb0 = col0 // tc
    nr = L // CONV_ROWS

    def body(x_ref, w_ref, b_ref, do_ref, dx_ref, dw_ref, db_ref, dpre_ref):
        def step1(r, carry):
            dw0, dw1, dw2, dw3, dbb = carry
            r0 = pl.multiple_of(r * CONV_ROWS, CONV_ROWS)
            cur = x_ref[pl.ds(r0, CONV_ROWS), :]
            p0 = pl.multiple_of(jnp.maximum(r0 - SUBLANES, 0), SUBLANES)
            prev = jnp.where(r > 0, x_ref[pl.ds(p0, SUBLANES), :], 0.0)
            cat = jnp.concatenate([prev, cur], axis=0)
            sh = [_shift_rows(cat, CONV_W - 1 - k) for k in range(CONV_W - 1)] + [cur]
            acc = b_ref[...] + w_ref[3:4, :] * cur
            for k in range(CONV_W - 1):
                acc = acc + w_ref[k:k + 1, :] * sh[k]
            sg = _sigmoid(acc)
            dpre = do_ref[pl.ds(r0, CONV_ROWS), :] * (sg + acc * sg * (1.0 - sg))
            dpre_ref[pl.ds(r0, CONV_ROWS), :] = dpre
            dws = [d + jnp.sum(dpre * s, axis=0, keepdims=True) for d, s in zip((dw0, dw1, dw2, dw3), sh)]
            return (*dws, dbb + jnp.sum(dpre, axis=0, keepdims=True))

        z = jnp.zeros((1, tc), F32)
        dw0, dw1, dw2, dw3, dbb = lax.fori_loop(0, nr, step1, (z, z, z, z, z))
        dw_ref[...] = jnp.concatenate([dw0, dw1, dw2, dw3, z, z, z, z], axis=0)
        db_ref[...] = dbb

        def step2(r, carry):
            r0 = pl.multiple_of(r * CONV_ROWS, CONV_ROWS)
            cur = dpre_ref[pl.ds(r0, CONV_ROWS), :]
            n0 = pl.multiple_of(jnp.minimum(r0 + CONV_ROWS, L - SUBLANES), SUBLANES)
            nxt = jnp.where(r < nr - 1, dpre_ref[pl.ds(n0, SUBLANES), :], 0.0)
            cat = jnp.concatenate([cur, nxt], axis=0)
            acc = w_ref[3:4, :] * cur
            for k in range(CONV_W - 1):
                s = CONV_W - 1 - k
                acc = acc + w_ref[k:k + 1, :] * pltpu.roll(cat, CONV_ROWS + SUBLANES - s, axis=0)[:CONV_ROWS, :]
            dx_ref[pl.ds(r0, CONV_ROWS), :] = acc
            return carry

        lax.fori_loop(0, nr, step2, 0)

    dx, dw, db = pl.pallas_call(
        body, name=name, grid=(C // tc,),
        in_specs=[pl.BlockSpec((L, tc), lambda j: (0, cb0 + j)), pl.BlockSpec((CONV_W, tc), lambda j: (0, j)),
                  pl.BlockSpec((1, tc), lambda j: (0, j)), pl.BlockSpec((L, tc), lambda j: (0, j))],
        out_specs=[pl.BlockSpec((L, tc), lambda j: (0, j)), pl.BlockSpec((SUBLANES, tc), lambda j: (0, j)),
                   pl.BlockSpec((1, tc), lambda j: (0, j))],
        out_shape=[jax.ShapeDtypeStruct((L, C), F32), jax.ShapeDtypeStruct((SUBLANES, C), F32),
                   jax.ShapeDtypeStruct((1, C), F32)],
        scratch_shapes=[pltpu.VMEM((L, tc), F32)],
        compiler_params=_params("parallel"),
    )(p, w, b, dout)
    return dx, dw[:CONV_W], db


def _chunk_consts():
    r, c = _iota2((CHUNK, CHUNK), 0), _iota2((CHUNK, CHUNK), 1)
    causal = r >= c
    return causal, r > c, (r == c).astype(F32), causal.astype(F32), jnp.ones((CHUNK, CHUNK), F32)


def _gdn_chunk(q, k, v, ab, gate, S, alog, dtb, og, ea, eb):
    causal, strict, eye, tril, ones = _chunk_consts()
    g = -jnp.exp(alog) * _softplus(_hi(ab, ea) + dtb)
    beta = _sigmoid(_hi(ab, eb))
    qn = q * lax.rsqrt(jnp.sum(q * q, axis=-1, keepdims=True) + 1e-6) * (GDN_DK ** -0.5)
    kn = k * lax.rsqrt(jnp.sum(k * k, axis=-1, keepdims=True) + 1e-6)
    gc = _hi(tril, g)
    glast = _hi(ones, g)
    gcol = gc[:, :CHUNK]
    grow = _hi(ones, gcol * eye)
    decay = jnp.exp(jnp.where(causal, gcol - grow, -jnp.inf))
    a = jnp.where(strict, beta[:, :CHUNK] * _nt(kn, kn) * decay, 0.0)
    eg = jnp.exp(gc)
    sol = _unit_lower_solve(a, jnp.concatenate([v * beta, kn * (beta * eg)], axis=1))
    u, w = sol[:, :GDN_DK], sol[:, GDN_DK:]
    qk = _nt(qn, kn) * decay
    v_new = u - _dot(w, S)
    o = _dot(qn * eg, S) + _dot(qk, v_new)
    cd = jnp.exp(glast)
    s_new = jnp.concatenate([cd, cd], axis=0) * S + _tn(kn * jnp.exp(glast - gc), v_new)
    on = o * lax.rsqrt(jnp.mean(o * o, axis=-1, keepdims=True) + RMS_EPS) * og
    return on * (gate * _sigmoid(gate)), s_new


def _gdn_specs(nc, rev):
    cm = (lambda c: nc - 1 - c) if rev else (lambda c: c)
    blk = lambda off: pl.BlockSpec((CHUNK, GDN_DK), lambda c, h: (cm(c), off + h))
    ab = pl.BlockSpec((CHUNK, LANES), lambda c, h: (cm(c), (GDN_IN_PAD - LANES) // LANES))
    hv = pl.BlockSpec((None, 1, LANES), lambda c, h: (h, 0, 0))
    og = pl.BlockSpec((1, LANES), lambda c, h: (0, 0))
    em = pl.BlockSpec((None, LANES, LANES), lambda c, h: (h, 0, 0))
    st = pl.BlockSpec((None, None, GDN_DK, GDN_DK), lambda c, h: (cm(c), h, 0, 0))
    return blk, ab, hv, og, em, st


def _gdn_fwd(qc, kc, vc, p, alog_e, dtb_e, og, ea, eb):
    L = qc.shape[0]
    nc = L // CHUNK
    blk, ab, hv, ogs, em, st = _gdn_specs(nc, False)

    def body(q_ref, k_ref, v_ref, gate_ref, ab_ref, al_ref, dt_ref, og_ref, ea_ref, eb_ref, y_ref, sp_ref, s_scr):
        c, h = pl.program_id(0), pl.program_id(1)

        @pl.when(c == 0)
        def _():
            s_scr[h] = jnp.zeros((GDN_DK, GDN_DK), F32)

        S = s_scr[h]
        sp_ref[...] = S
        y, s_new = _gdn_chunk(q_ref[...], k_ref[...], v_ref[...], ab_ref[...], gate_ref[...], S,
                              al_ref[...], dt_ref[...], og_ref[...], ea_ref[...], eb_ref[...])
        y_ref[...] = y
        s_scr[h] = s_new

    return pl.pallas_call(
        body, name="gdn_fwd", grid=(nc, GDN_HEADS),
        in_specs=[blk(0), blk(0), blk(0), blk(3 * GDN_HEADS), ab, hv, hv, ogs, em, em],
        out_specs=[blk(0), st],
        out_shape=[jax.ShapeDtypeStruct((L, D_MODEL), F32), jax.ShapeDtypeStruct((nc, GDN_HEADS, GDN_DK, GDN_DK), F32)],
        scratch_shapes=[pltpu.VMEM((GDN_HEADS, GDN_DK, GDN_DK), F32)],
        compiler_params=_params("arbitrary", "arbitrary"),
    )(qc, kc, vc, p, p, alog_e, dtb_e, og, ea, eb)


def _gdn_bwd(qc, kc, vc, p, alog_e, dtb_e, og, ea, eb, sprev, dy):
    L = qc.shape[0]
    nc = L // CHUNK
    blk, ab, hv, ogs, em, st = _gdn_specs(nc, True)

    def body(q_ref, k_ref, v_ref, gate_ref, ab_ref, al_ref, dt_ref, og_ref, ea_ref, eb_ref, sp_ref, dy_ref,
             dq_ref, dk_ref, dv_ref, dgate_ref, dab_ref, dpar_ref, ds_scr):
        c, h = pl.program_id(0), pl.program_id(1)

        @pl.when(c == 0)
        def _():
            ds_scr[h] = jnp.zeros((GDN_DK, GDN_DK), F32)
            dpar_ref[h] = jnp.zeros((SUBLANES, LANES), F32)

        ea_m, eb_m = ea_ref[...], eb_ref[...]
        f = lambda q, k, v, a_b, gate, S, al, dt, o_g: _gdn_chunk(q, k, v, a_b, gate, S, al, dt, o_g, ea_m, eb_m)
        _, vjp = jax.vjp(f, q_ref[...], k_ref[...], v_ref[...], ab_ref[...], gate_ref[...], sp_ref[...],
                         al_ref[...], dt_ref[...], og_ref[...])
        dq, dk, dv, dab, dgate, ds, dal, ddt, dog = vjp((dy_ref[...], ds_scr[h]))
        dq_ref[...] = dq
        dk_ref[...] = dk
        dv_ref[...] = dv
        dgate_ref[...] = dgate
        ds_scr[h] = ds

        @pl.when(h == 0)
        def _():
            dab_ref[...] = jnp.zeros_like(dab_ref)

        dab_ref[...] += dab
        dpar_ref[h] += jnp.concatenate([dal, ddt, dog, jnp.zeros((SUBLANES - 3, LANES), F32)], axis=0)

    return pl.pallas_call(
        body, name="gdn_bwd", grid=(nc, GDN_HEADS),
        in_specs=[blk(0), blk(0), blk(0), blk(3 * GDN_HEADS), ab, hv, hv, ogs, em, em, st, blk(0)],
        out_specs=[blk(0), blk(0), blk(0), blk(0), pl.BlockSpec((CHUNK, LANES), lambda c, h: (nc - 1 - c, 0)),
                   pl.BlockSpec((GDN_HEADS, SUBLANES, LANES), lambda c, h: (0, 0, 0))],
        out_shape=[jax.ShapeDtypeStruct((L, D_MODEL), F32)] * 4
        + [jax.ShapeDtypeStruct((L, LANES), F32), jax.ShapeDtypeStruct((GDN_HEADS, SUBLANES, LANES), F32)],
        scratch_shapes=[pltpu.VMEM((GDN_HEADS, GDN_DK, GDN_DK), F32)],
        compiler_params=_params("arbitrary", "arbitrary"),
    )(qc, kc, vc, p, p, alog_e, dtb_e, og, ea, eb, sprev, dy)


def _gdn_selectors():
    rows = np.arange(LANES)[None, :, None]
    heads = np.arange(GDN_HEADS)[:, None, None]
    ea = np.broadcast_to(rows == heads, (GDN_HEADS, LANES, LANES)).astype(np.float32)
    eb = np.broadcast_to(rows == heads + GDN_HEADS, (GDN_HEADS, LANES, LANES)).astype(np.float32)
    return jnp.asarray(ea), jnp.asarray(eb)


M2_GW = M2_INNER // M2_GROUPS
M2_HPG = M2_HEADS // M2_GROUPS
M2_HD = M2_INNER // M2_HEADS


def _m2_chunk(x, bm, cm, z, dtr, st, dtb, alog, dsk, ng, e, ecol):
    causal, _, eye, tril, ones = _chunk_consts()
    dt_n = _softplus(dtr + dtb)
    da_n = dt_n * (-jnp.exp(alog))
    cum_n = _hi(tril, da_n)
    tot_n = _hi(ones, da_n)
    dt_w, cum_w, tot_w = _hi(dt_n, e), _hi(cum_n, e), _hi(tot_n, e)
    xdt = x * dt_w
    cb = _nt(cm, bm)
    head = _iota2((CHUNK, M2_GW), 1) // M2_HD
    ydiag = jnp.zeros((CHUNK, M2_GW), F32)
    for r in range(M2_HPG):
        colb = _hi(cum_n, ecol[r])
        rowb = _hi(ones, colb * eye)
        lmat = jnp.exp(jnp.where(causal, colb - rowb, -jnp.inf))
        ydiag = ydiag + jnp.where(head == r, _dot(cb * lmat, xdt), 0.0)
    st_new = _tn(bm, xdt * jnp.exp(tot_w - cum_w))
    cd = jnp.exp(tot_w)
    s_new = jnp.concatenate([cd, cd], axis=0) * st + st_new
    y = ydiag + _dot(cm, st) * jnp.exp(cum_w) + dsk * x
    y = y * (z * _sigmoid(z))
    yn = y * lax.rsqrt(jnp.mean(y * y, axis=-1, keepdims=True) + RMS_EPS) * ng
    return yn, s_new


def _m2_specs(nc, rev):
    cm = (lambda c: nc - 1 - c) if rev else (lambda c: c)
    wide = lambda off: pl.BlockSpec((CHUNK, M2_GW), lambda c, g: (cm(c), off + g))
    nar = lambda off: pl.BlockSpec((CHUNK, LANES), lambda c, g: (cm(c), off + g))
    dts = pl.BlockSpec((CHUNK, LANES), lambda c, g: (cm(c), (M2_IN_PAD - LANES) // LANES))
    v128 = pl.BlockSpec((1, LANES), lambda c, g: (0, 0))
    v256 = pl.BlockSpec((1, M2_GW), lambda c, g: (0, g))
    es = pl.BlockSpec((None, LANES, M2_GW), lambda c, g: (g, 0, 0))
    ecs = pl.BlockSpec((None, M2_HPG, LANES, M2_HD), lambda c, g: (g, 0, 0, 0))
    st = pl.BlockSpec((None, None, M2_STATE, M2_GW), lambda c, g: (cm(c), g, 0, 0))
    return wide, nar, dts, v128, v256, es, ecs, st


def _m2_fwd(xbc, p, dtb, alog, dsk, ng, e, ecol):
    L = xbc.shape[0]
    nc = L // CHUNK
    wide, nar, dts, v128, v256, es, ecs, st = _m2_specs(nc, False)

    def body(x_ref, b_ref, c_ref, z_ref, dt_ref, dtb_ref, al_ref, dsk_ref, ng_ref, e_ref, ec_ref, y_ref, sp_ref, s_scr):
        c, g = pl.program_id(0), pl.program_id(1)

        @pl.when(c == 0)
        def _():
            s_scr[g] = jnp.zeros((M2_STATE, M2_GW), F32)

        S = s_scr[g]
        sp_ref[...] = S
        y, s_new = _m2_chunk(x_ref[...], b_ref[...], c_ref[...], z_ref[...], dt_ref[...], S, dtb_ref[...], al_ref[...],
                             dsk_ref[...], ng_ref[...], e_ref[...], ec_ref[...])
        y_ref[...] = y
        s_scr[g] = s_new

    return pl.pallas_call(
        body, name="m2_fwd", grid=(nc, M2_GROUPS),
        in_specs=[wide(0), nar(2 * M2_GROUPS), nar(3 * M2_GROUPS), wide(0), dts, v128, v128, v256, v256, es, ecs],
        out_specs=[wide(0), st],
        out_shape=[jax.ShapeDtypeStruct((L, M2_INNER), F32), jax.ShapeDtypeStruct((nc, M2_GROUPS, M2_STATE, M2_GW), F32)],
        scratch_shapes=[pltpu.VMEM((M2_GROUPS, M2_STATE, M2_GW), F32)],
        compiler_params=_params("arbitrary", "arbitrary"),
    )(xbc, xbc, xbc, p, p, dtb, alog, dsk, ng, e, ecol)


def _m2_bwd(xbc, p, dtb, alog, dsk, ng, e, ecol, sprev, dy):
    L = xbc.shape[0]
    nc = L // CHUNK
    wide, nar, dts, v128, v256, es, ecs, st = _m2_specs(nc, True)

    def body(x_ref, b_ref, c_ref, z_ref, dt_ref, dtb_ref, al_ref, dsk_ref, ng_ref, e_ref, ec_ref, sp_ref, dy_ref,
             dx_ref, db_ref, dc_ref, dz_ref, ddt_ref, dnar_ref, dwide_ref, ds_scr):
        c, g = pl.program_id(0), pl.program_id(1)

        @pl.when(c == 0)
        def _():
            ds_scr[g] = jnp.zeros((M2_STATE, M2_GW), F32)
            dwide_ref[g] = jnp.zeros((SUBLANES, M2_GW), F32)

        @pl.when(jnp.logical_and(c == 0, g == 0))
        def _():
            dnar_ref[...] = jnp.zeros_like(dnar_ref)

        e_m, ec_m = e_ref[...], ec_ref[...]
        f = lambda x, bm, cm, z, dtr, S, dtb, al, dsk, ng: _m2_chunk(x, bm, cm, z, dtr, S, dtb, al, dsk, ng, e_m, ec_m)
        _, vjp = jax.vjp(f, x_ref[...], b_ref[...], c_ref[...], z_ref[...], dt_ref[...], sp_ref[...], dtb_ref[...],
                         al_ref[...], dsk_ref[...], ng_ref[...])
        dx, db, dc, dz, ddt, ds, ddtb, dal, ddsk, dng = vjp((dy_ref[...], ds_scr[g]))
        dx_ref[...] = dx
        db_ref[...] = db
        dc_ref[...] = dc
        dz_ref[...] = dz
        ds_scr[g] = ds

        @pl.when(g == 0)
        def _():
            ddt_ref[...] = jnp.zeros_like(ddt_ref)

        ddt_ref[...] += ddt
        dnar_ref[...] += jnp.concatenate([ddtb, dal, jnp.zeros((SUBLANES - 2, LANES), F32)], axis=0)
        dwide_ref[g] += jnp.concatenate([ddsk, dng, jnp.zeros((SUBLANES - 2, M2_GW), F32)], axis=0)

    return pl.pallas_call(
        body, name="m2_bwd", grid=(nc, M2_GROUPS),
        in_specs=[wide(0), nar(2 * M2_GROUPS), nar(3 * M2_GROUPS), wide(0), dts, v128, v128, v256, v256, es, ecs, st, wide(0)],
        out_specs=[wide(0), nar(0), nar(0), wide(0), pl.BlockSpec((CHUNK, LANES), lambda c, g: (nc - 1 - c, 0)),
                   pl.BlockSpec((SUBLANES, LANES), lambda c, g: (0, 0)),
                   pl.BlockSpec((M2_GROUPS, SUBLANES, M2_GW), lambda c, g: (0, 0, 0))],
        out_shape=[jax.ShapeDtypeStruct((L, M2_INNER), F32), jax.ShapeDtypeStruct((L, M2_GROUPS * M2_STATE), F32),
                   jax.ShapeDtypeStruct((L, M2_GROUPS * M2_STATE), F32), jax.ShapeDtypeStruct((L, M2_INNER), F32),
                   jax.ShapeDtypeStruct((L, LANES), F32), jax.ShapeDtypeStruct((SUBLANES, LANES), F32),
                   jax.ShapeDtypeStruct((M2_GROUPS, SUBLANES, M2_GW), F32)],
        scratch_shapes=[pltpu.VMEM((M2_GROUPS, M2_STATE, M2_GW), F32)],
        compiler_params=_params("arbitrary", "arbitrary"),
    )(xbc, xbc, xbc, p, p, dtb, alog, dsk, ng, e, ecol, sprev, dy)


def _m2_selectors():
    rows = np.arange(LANES)
    e = np.zeros((M2_GROUPS, LANES, M2_GW), np.float32)
    ecol = np.zeros((M2_GROUPS, M2_HPG, LANES, M2_HD), np.float32)
    for g in range(M2_GROUPS):
        for r in range(M2_HPG):
            e[g, M2_HPG * g + r, M2_HD * r:M2_HD * (r + 1)] = 1.0
            ecol[g, r, M2_HPG * g + r, :] = 1.0
    del rows
    return jnp.asarray(e), jnp.asarray(ecol)


S5_NS = S5_GROUPS * S5_STATE // S5_BLOCKS
S5_ROWS = 256
GELU_C = math.sqrt(2.0 / math.pi)


def _gelu(x):
    return 0.5 * x * (1.0 + jnp.tanh(GELU_C * (x + 0.044715 * x * x * x)))


def _gelu_grad(x):
    t = jnp.tanh(GELU_C * (x + 0.044715 * x * x * x))
    return 0.5 * (1.0 + t) + 0.5 * x * (1.0 - t * t) * GELU_C * (1.0 + 3.0 * 0.044715 * x * x)


def _s5_scan(re_ref, im_ref, pw_re, pw_im, nrows, reverse):
    n = re_ref.shape[1]
    row = _iota2((SUBLANES, n), 0)
    steps = [(d, pw_re[d - 1:d, :], pw_im[d - 1:d, :]) for d in (1, 2, 4)]
    if reverse:
        cw_re = jnp.concatenate([pw_re[SUBLANES - 1 - k:SUBLANES - k, :] for k in range(SUBLANES)], axis=0)
        cw_im = jnp.concatenate([pw_im[SUBLANES - 1 - k:SUBLANES - k, :] for k in range(SUBLANES)], axis=0)
    else:
        cw_re, cw_im = pw_re, pw_im
    edge = 0 if reverse else SUBLANES - 1
    ngroups = nrows // SUBLANES

    def step(i, carry):
        cr, ci = carry
        gi = (ngroups - 1 - i) if reverse else i
        r0 = pl.multiple_of(gi * SUBLANES, SUBLANES)
        xr, xi = re_ref[pl.ds(r0, SUBLANES), :], im_ref[pl.ds(r0, SUBLANES), :]
        for d, pr, pi in steps:
            if reverse:
                sr = jnp.where(row < SUBLANES - d, pltpu.roll(xr, SUBLANES - d, axis=0), 0.0)
                si = jnp.where(row < SUBLANES - d, pltpu.roll(xi, SUBLANES - d, axis=0), 0.0)
            else:
                sr = jnp.where(row >= d, pltpu.roll(xr, d, axis=0), 0.0)
                si = jnp.where(row >= d, pltpu.roll(xi, d, axis=0), 0.0)
            xr, xi = xr + (pr * sr - pi * si), xi + (pr * si + pi * sr)
        xr, xi = xr + (cw_re * cr - cw_im * ci), xi + (cw_re * ci + cw_im * cr)
        re_ref[pl.ds(r0, SUBLANES), :] = xr
        im_ref[pl.ds(r0, SUBLANES), :] = xi
        return (jnp.sum(jnp.where(row == edge, xr, 0.0), axis=0, keepdims=True),
                jnp.sum(jnp.where(row == edge, xi, 0.0), axis=0, keepdims=True))

    z = jnp.zeros((1, n), F32)
    lax.fori_loop(0, ngroups, step, (z, z))


def _s5_project_in(u_ref, bm_ref, re_ref, im_ref, L):
    def step(i, carry):
        r0 = pl.multiple_of(i * S5_ROWS, S5_ROWS)
        bu = _dot(u_ref[pl.ds(r0, S5_ROWS), :], bm_ref[...])
        re_ref[pl.ds(r0, S5_ROWS), :] = bu[:, :S5_NS]
        im_ref[pl.ds(r0, S5_ROWS), :] = bu[:, S5_NS:]
        return carry

    lax.fori_loop(0, L // S5_ROWS, step, 0)


def _s5_specs(L):
    col = pl.BlockSpec((L, LANES), lambda j: (0, j))
    bm = pl.BlockSpec((None, LANES, 2 * S5_NS), lambda j: (j, 0, 0))
    cm = pl.BlockSpec((None, 2 * S5_NS, LANES), lambda j: (j, 0, 0))
    pw = pl.BlockSpec((None, SUBLANES, S5_NS), lambda j: (j, 0, 0))
    vec = pl.BlockSpec((1, LANES), lambda j: (0, j))
    return col, bm, cm, pw, vec


def _s5_fwd(u, bmat, cmat, pw_re, pw_im, dsk):
    L = u.shape[0]
    col, bm, cm, pw, vec = _s5_specs(L)

    def body(u_ref, bm_ref, cm_ref, pr_ref, pi_ref, d_ref, y_ref, re_scr, im_scr):
        _s5_project_in(u_ref, bm_ref, re_scr, im_scr, L)
        _s5_scan(re_scr, im_scr, pr_ref[...], pi_ref[...], L, False)

        def step(i, carry):
            r0 = pl.multiple_of(i * S5_ROWS, S5_ROWS)
            rows = pl.ds(r0, S5_ROWS)
            y = _dot(re_scr[rows, :], cm_ref[:S5_NS, :]) + _dot(im_scr[rows, :], cm_ref[S5_NS:, :]) + d_ref[...] * u_ref[rows, :]
            y_ref[rows, :] = _gelu(y)
            return carry

        lax.fori_loop(0, L // S5_ROWS, step, 0)

    return pl.pallas_call(
        body, name="s5_fwd", grid=(S5_BLOCKS,),
        in_specs=[col, bm, cm, pw, pw, vec], out_specs=col,
        out_shape=jax.ShapeDtypeStruct((L, D_MODEL), F32),
        scratch_shapes=[pltpu.VMEM((L, S5_NS), F32)] * 2,
        compiler_params=_params("parallel"),
    )(u, bmat, cmat, pw_re, pw_im, dsk)


def _s5_bwd(u, bmat, cmat, pw_re, pw_im, dsk, dyg):
    L = u.shape[0]
    col, bm, cm, pw, vec = _s5_specs(L)

    def body(u_ref, bm_ref, cm_ref, pr_ref, pi_ref, d_ref, dy_ref, du_ref, dbm_ref, dcm_ref, dlam_ref, dd_ref,
             re_scr, im_scr, gr_scr, gi_scr, dyp_scr):
        _s5_project_in(u_ref, bm_ref, re_scr, im_scr, L)
        _s5_scan(re_scr, im_scr, pr_ref[...], pi_ref[...], L, False)

        def step(i, carry):
            dcr, dci, dd = carry
            r0 = pl.multiple_of(i * S5_ROWS, S5_ROWS)
            rows = pl.ds(r0, S5_ROWS)
            sr, si, uu = re_scr[rows, :], im_scr[rows, :], u_ref[rows, :]
            y = _dot(sr, cm_ref[:S5_NS, :]) + _dot(si, cm_ref[S5_NS:, :]) + d_ref[...] * uu
            dyp = dy_ref[rows, :] * _gelu_grad(y)
            dyp_scr[rows, :] = dyp
            gr_scr[rows, :] = _nt(dyp, cm_ref[:S5_NS, :])
            gi_scr[rows, :] = _nt(dyp, cm_ref[S5_NS:, :])
            return dcr + _tn(sr, dyp), dci + _tn(si, dyp), dd + jnp.sum(dyp * uu, axis=0, keepdims=True)

        zc = jnp.zeros((S5_NS, LANES), F32)
        dcr, dci, dd = lax.fori_loop(0, L // S5_ROWS, step, (zc, zc, jnp.zeros((1, LANES), F32)))
        dcm_ref[:S5_NS, :] = dcr
        dcm_ref[S5_NS:, :] = dci
        dd_ref[...] = dd

        _s5_scan(gr_scr, gi_scr, pr_ref[...], -pi_ref[...], L, True)

        row = _iota2((SUBLANES, S5_NS), 0)

        def lam_step(i, carry):
            ar, ai, pr, pi = carry
            r0 = pl.multiple_of(i * SUBLANES, SUBLANES)
            rows = pl.ds(r0, SUBLANES)
            sr, si = re_scr[rows, :], im_scr[rows, :]
            spr = jnp.where(row >= 1, pltpu.roll(sr, 1, axis=0), pr)
            spi = jnp.where(row >= 1, pltpu.roll(si, 1, axis=0), pi)
            gr, gi = gr_scr[rows, :], gi_scr[rows, :]
            ar = ar + jnp.sum(spr * gr + spi * gi, axis=0, keepdims=True)
            ai = ai + jnp.sum(spr * gi - spi * gr, axis=0, keepdims=True)
            last = row == SUBLANES - 1
            return (ar, ai, jnp.sum(jnp.where(last, sr, 0.0), axis=0, keepdims=True),
                    jnp.sum(jnp.where(last, si, 0.0), axis=0, keepdims=True))

        z = jnp.zeros((1, S5_NS), F32)
        ar, ai, _, _ = lax.fori_loop(0, L // SUBLANES, lam_step, (z, z, z, z))
        dlam_ref[...] = jnp.concatenate([ar, ai, jnp.zeros((SUBLANES - 2, S5_NS), F32)], axis=0)

        def in_step(i, carry):
            dbr, dbi = carry
            r0 = pl.multiple_of(i * S5_ROWS, S5_ROWS)
            rows = pl.ds(r0, S5_ROWS)
            gr, gi, uu = gr_scr[rows, :], gi_scr[rows, :], u_ref[rows, :]
            du_ref[rows, :] = dyp_scr[rows, :] * d_ref[...] + _nt(gr, bm_ref[:, :S5_NS]) + _nt(gi, bm_ref[:, S5_NS:])
            return dbr + _tn(uu, gr), dbi + _tn(uu, gi)

        zb = jnp.zeros((LANES, S5_NS), F32)
        dbr, dbi = lax.fori_loop(0, L // S5_ROWS, in_step, (zb, zb))
        dbm_ref[:, :S5_NS] = dbr
        dbm_ref[:, S5_NS:] = dbi

    return pl.pallas_call(
        body, name="s5_bwd", grid=(S5_BLOCKS,),
        in_specs=[col, bm, cm, pw, pw, vec, col], out_specs=[col, bm, cm, pw, vec],
        out_shape=[jax.ShapeDtypeStruct((L, D_MODEL), F32), jax.ShapeDtypeStruct((S5_BLOCKS, LANES, 2 * S5_NS), F32),
                   jax.ShapeDtypeStruct((S5_BLOCKS, 2 * S5_NS, LANES), F32),
                   jax.ShapeDtypeStruct((S5_BLOCKS, SUBLANES, S5_NS), F32), jax.ShapeDtypeStruct((1, D_MODEL), F32)],
        scratch_shapes=[pltpu.VMEM((L, S5_NS), F32)] * 4 + [pltpu.VMEM((L, LANES), F32)],
        compiler_params=_params("parallel"),
    )(u, bmat, cmat, pw_re, pw_im, dsk, dyg)


def _s5_discretize(lam_re, lam_im, log_dt, b_re, b_im, e16):
    dt = jnp.exp(log_dt)
    zr, zi = lam_re * dt, lam_im * dt
    mag = jnp.exp(zr)
    lbr, lbi = mag * jnp.cos(zi), mag * jnp.sin(zi)
    den = lam_re * lam_re + lam_im * lam_im
    nr, ni = lbr - 1.0, lbi
    cr = (nr * lam_re + ni * lam_im) / den
    ci = (ni * lam_re - nr * lam_im) / den
    crw, ciw = _hi(cr, e16), _hi(ci, e16)
    return lbr, lbi, crw * b_re - ciw * b_im, crw * b_im + ciw * b_re


def _s5_params_fwd(lam_re, lam_im, log_dt, b_re, b_im, e16):
    def body(lr, li, ld, br, bi, e, o1, o2, o3, o4):
        for o, val in zip((o1, o2, o3, o4), _s5_discretize(lr[...], li[...], ld[...], br[...], bi[...], e[...])):
            o[...] = val

    g, p, n = S5_GROUPS, S5_STATE, S5_STATE * S5_GROUP
    return pl.pallas_call(
        body, name="s5_params_fwd",
        out_shape=[jax.ShapeDtypeStruct((g, p), F32)] * 2 + [jax.ShapeDtypeStruct((g, n), F32)] * 2,
        compiler_params=_params(),
    )(lam_re, lam_im, log_dt, b_re, b_im, e16)


def _s5_params_bwd(lam_re, lam_im, log_dt, b_re, b_im, e16, cts):
    def body(lr, li, ld, br, bi, e, c1, c2, c3, c4, o1, o2, o3, o4, o5):
        e_m = e[...]
        f = lambda a, b, c, d, g: _s5_discretize(a, b, c, d, g, e_m)
        _, vjp = jax.vjp(f, lr[...], li[...], ld[...], br[...], bi[...])
        for o, val in zip((o1, o2, o3, o4, o5), vjp((c1[...], c2[...], c3[...], c4[...]))):
            o[...] = val

    g, p, n = S5_GROUPS, S5_STATE, S5_STATE * S5_GROUP
    return pl.pallas_call(
        body, name="s5_params_bwd",
        out_shape=[jax.ShapeDtypeStruct((g, p), F32)] * 2 + [jax.ShapeDtypeStruct((g, 1), F32)]
        + [jax.ShapeDtypeStruct((g, n), F32)] * 2,
        compiler_params=_params(),
    )(lam_re, lam_im, log_dt, b_re, b_im, e16, *cts)


def _add_residual(acc, h):
    return (acc + h,)


def _mlp_fwd(i, h, g, w1, w2):
    hn = _rms_fwd(f"mlp{i}_norm", h, g)
    r, a = _mm(f"mlp{i}_up", hn, w1, "nn", (BF16, BF16), epi=lambda acc: (jnp.square(jnp.maximum(acc, 0.0)), acc))
    return _mm(f"mlp{i}_down", r, w2, "nn", (F32,), epi=_add_residual, extras=(h,)), (h, hn, r, a)


def _mlp_bwd(i, dh_out, saved, g, w1, w2):
    h, hn, r, a = saved
    dw2 = _mm(f"mlp{i}_dw2", r, dh_out, "tn", (F32,))
    da = _mm(f"mlp{i}_da", dh_out, w2, "nt", (BF16,), epi=lambda acc, aa: (acc * (2.0 * jnp.maximum(aa.astype(F32), 0.0)),),
             extras=(a,))
    dw1 = _mm(f"mlp{i}_dw1", hn, da, "tn", (F32,))
    dhn = _mm(f"mlp{i}_dhn", da, w1, "nt", (F32,))
    dh, dg = _rms_bwd(f"mlp{i}_dnorm", h, g, dhn, dh_out)
    return dh, dg[0], dw1, dw2


def _lanes(v, n):
    return jnp.broadcast_to(v.reshape(n, 1, 1), (n, 1, LANES))


def _gdn_fwd_layer(i, h, g, w_in, conv_w, a_log, dt_bias, o_g, w_out):
    hn = _rms_fwd(f"gdn{i}_norm", h, g)
    p = _mm(f"gdn{i}_in", hn, w_in, "nn", (F32,))
    zb = jnp.zeros((1, D_MODEL), F32)
    qkv = [_conv_fwd(f"gdn{i}_conv{t}", p, t * D_MODEL, conv_w[:, t * D_MODEL:(t + 1) * D_MODEL], zb) for t in range(3)]
    ea, eb = _gdn_selectors()
    y, sprev = _gdn_fwd(*qkv, p, _lanes(a_log, GDN_HEADS), _lanes(dt_bias, GDN_HEADS), o_g.reshape(1, LANES), ea, eb)
    return _mm(f"gdn{i}_out", y, w_out, "nn", (F32,), epi=_add_residual, extras=(h,)), (h, hn, p, qkv, y, sprev)


def _gdn_bwd_layer(i, dh_out, saved, g, w_in, conv_w, a_log, dt_bias, o_g, w_out):
    h, hn, p, qkv, y, sprev = saved
    dy = _mm(f"gdn{i}_dy", dh_out, w_out, "nt", (F32,))
    dw_out = _mm(f"gdn{i}_dwout", y, dh_out, "tn", (F32,))
    ea, eb = _gdn_selectors()
    dq, dk, dv, dgate, dab, dpar = _gdn_bwd(*qkv, p, _lanes(a_log, GDN_HEADS), _lanes(dt_bias, GDN_HEADS),
                                            o_g.reshape(1, LANES), ea, eb, sprev, dy)
    zb = jnp.zeros((1, D_MODEL), F32)
    dpre, dcw = [], []
    for t, d in enumerate((dq, dk, dv)):
        dx, dw, _ = _conv_bwd(f"gdn{i}_dconv{t}", p, t * D_MODEL, conv_w[:, t * D_MODEL:(t + 1) * D_MODEL], zb, d)
        dpre.append(dx)
        dcw.append(dw)
    dp = jnp.concatenate(dpre + [dgate, dab], axis=1)
    dw_in = _mm(f"gdn{i}_dwin", hn, dp, "tn", (F32,))[:, :GDN_IN]
    dhn = _mm(f"gdn{i}_dhn", dp, w_in, "nt", (F32,))
    dh, dg = _rms_bwd(f"gdn{i}_dnorm", h, g, dhn, dh_out)
    grads = dict(w_in=dw_in, conv_w=jnp.concatenate(dcw, axis=1), a_log=jnp.sum(dpar[:, 0, :], axis=-1),
                 dt_bias=jnp.sum(dpar[:, 1, :], axis=-1), o_norm_g=jnp.sum(dpar[:, 2, :], axis=0), w_out=dw_out)
    return dh, dg[0], grads


def _m2_vectors(dt_bias, a_log, d_skip, norm_g):
    pad = lambda v: jnp.pad(v, (0, LANES - M2_HEADS)).reshape(1, LANES)
    return pad(dt_bias), pad(a_log), jnp.repeat(d_skip, M2_HD).reshape(1, M2_INNER), norm_g.reshape(1, M2_INNER)


def _m2_fwd_layer(h, g, w_in, conv_w, conv_b, dt_bias, a_log, d_skip, norm_g, w_out):
    hn = _rms_fwd("m2_norm", h, g)
    p = _mm("m2_in", hn, w_in, "nn", (F32,))
    xbc = _conv_fwd("m2_conv", p, M2_INNER, conv_w, conv_b.reshape(1, M2_CONV_CH))
    e, ecol = _m2_selectors()
    y, sprev = _m2_fwd(xbc, p, *_m2_vectors(dt_bias, a_log, d_skip, norm_g), e, ecol)
    return _mm("m2_out", y, w_out, "nn", (F32,), epi=_add_residual, extras=(h,)), (h, hn, p, xbc, y, sprev)


def _m2_bwd_layer(dh_out, saved, g, w_in, conv_w, conv_b, dt_bias, a_log, d_skip, norm_g, w_out):
    h, hn, p, xbc, y, sprev = saved
    dy = _mm("m2_dy", dh_out, w_out, "nt", (F32,))
    dw_out = _mm("m2_dwout", y, dh_out, "tn", (F32,))
    e, ecol = _m2_selectors()
    dx, db, dc, dz, ddt, dnar, dwide = _m2_bwd(xbc, p, *_m2_vectors(dt_bias, a_log, d_skip, norm_g), e, ecol, sprev, dy)
    dxbc, dcw, dcb = _conv_bwd("m2_dconv", p, M2_INNER, conv_w, conv_b.reshape(1, M2_CONV_CH),
                               jnp.concatenate([dx, db, dc], axis=1))
    dp = jnp.concatenate([dz, dxbc, ddt], axis=1)
    dw_in = _mm("m2_dwin", hn, dp, "tn", (F32,))[:, :M2_IN]
    dhn = _mm("m2_dhn", dp, w_in, "nt", (F32,))
    dh, dg = _rms_bwd("m2_dnorm", h, g, dhn, dh_out)
    grads = dict(w_in=dw_in, conv_w=dcw, conv_b=dcb[0], dt_bias=dnar[0, :M2_HEADS], a_log=dnar[1, :M2_HEADS],
                 d=jnp.sum(dwide[:, 0, :].reshape(M2_HEADS, M2_HD), axis=-1), norm_g=dwide[:, 1, :].reshape(M2_INNER),
                 w_out=dw_out)
    return dh, dg[0], grads


def _s5_selector():
    e16 = np.zeros((S5_STATE, S5_STATE * S5_GROUP), np.float32)
    for p in range(S5_STATE):
        e16[p, p * S5_GROUP:(p + 1) * S5_GROUP] = 1.0
    return jnp.asarray(e16)


def _s5_operands(lbr, lbi, bbr, bbi, c_re, c_im):
    eye = jnp.eye(S5_BLOCKS, dtype=F32)
    gpb = S5_GROUPS // S5_BLOCKS
    bd = lambda t: jnp.einsum("jgpk,gh->jgkhp", t.reshape(S5_BLOCKS, gpb, S5_STATE, S5_GROUP), eye).reshape(S5_BLOCKS, LANES, S5_NS)
    cd = lambda t: jnp.einsum("jgkp,gh->jgphk", t.reshape(S5_BLOCKS, gpb, S5_GROUP, S5_STATE), eye).reshape(S5_BLOCKS, S5_NS, LANES)
    bmat = jnp.concatenate([bd(bbr), bd(bbi)], axis=2).astype(BF16)
    cmat = jnp.concatenate([cd(c_re), -cd(c_im)], axis=1).astype(BF16)
    ar, ai = lbr.reshape(S5_BLOCKS, S5_NS), lbi.reshape(S5_BLOCKS, S5_NS)
    pr, pi = [ar], [ai]
    for _ in range(SUBLANES - 1):
        pr, pi = pr + [pr[-1] * ar - pi[-1] * ai], pi + [pr[-1] * ai + pi[-1] * ar]
    return bmat, cmat, jnp.stack(pr, axis=1), jnp.stack(pi, axis=1)


def _s5_fwd_layer(h, g, w_in, lam_re, lam_im, log_dt, b_re, b_im, c_re, c_im, d_skip, w_out):
    hn = _rms_fwd("s5_norm", h, g)
    u = _mm("s5_in", hn, w_in, "nn", (F32,))
    n = S5_STATE * S5_GROUP
    lbr, lbi, bbr, bbi = _s5_params_fwd(lam_re, lam_im, log_dt.reshape(S5_GROUPS, 1), b_re.reshape(S5_GROUPS, n),
                                        b_im.reshape(S5_GROUPS, n), _s5_selector())
    ops = _s5_operands(lbr, lbi, bbr, bbi, c_re, c_im)
    yg = _s5_fwd(u, *ops, d_skip.reshape(1, D_MODEL))
    ag = _mm("s5_out", yg, w_out, "nn", (F32,))
    return _glu_fwd(h, ag), (h, hn, u, ops, yg, ag)


def _s5_bwd_layer(dh_out, saved, g, w_in, lam_re, lam_im, log_dt, b_re, b_im, c_re, c_im, d_skip, w_out):
    h, hn, u, ops, yg, ag = saved
    dag = _glu_bwd(dh_out, ag)
    dw_out = _mm("s5_dwout", yg, dag, "tn", (F32,))
    dyg = _mm("s5_dyg", dag, w_out, "nt", (F32,))
    du, dbmat, dcmat, dlam, ddsk = _s5_bwd(u, *ops, d_skip.reshape(1, D_MODEL), dyg)
    eye = jnp.eye(S5_BLOCKS, dtype=F32)
    gpb = S5_GROUPS // S5_BLOCKS
    n = S5_STATE * S5_GROUP
    ub = lambda t: jnp.einsum("jgkhp,gh->jgpk", t.reshape(S5_BLOCKS, gpb, S5_GROUP, gpb, S5_STATE), eye).reshape(S5_GROUPS, n)
    uc = lambda t: jnp.einsum("jgphk,gh->jgkp", t.reshape(S5_BLOCKS, gpb, S5_STATE, gpb, S5_GROUP), eye).reshape(c_re.shape)
    cts = (dlam[:, 0, :].reshape(S5_GROUPS, S5_STATE), dlam[:, 1, :].reshape(S5_GROUPS, S5_STATE),
           ub(dbmat[:, :, :S5_NS]), ub(dbmat[:, :, S5_NS:]))
    dlr, dli, dld, dbr, dbi = _s5_params_bwd(lam_re, lam_im, log_dt.reshape(S5_GROUPS, 1), b_re.reshape(S5_GROUPS, n),
                                             b_im.reshape(S5_GROUPS, n), _s5_selector(), cts)
    dw_in = _mm("s5_dwin", hn, du, "tn", (F32,))
    dhn = _mm("s5_dhn", du, w_in, "nt", (F32,))
    dh, dg = _rms_bwd("s5_dnorm", h, g, dhn, dh_out)
    grads = dict(w_in=dw_in, lam_re=dlr, lam_im=dli, log_dt=dld[:, 0], b_re=dbr.reshape(b_re.shape), b_im=dbi.reshape(b_im.shape),
                 c_re=uc(dcmat[:, :S5_NS, :]), c_im=-uc(dcmat[:, S5_NS:, :]), d=ddsk[0], w_out=dw_out)
    return dh, dg[0], grads


MIXER_OF_LAYER = ("gdn", "s5", "m2", "gdn")
MIXER_INDEX = (0, 0, 0, 1)


def _mixer_args(W, i):
    kind, j = MIXER_OF_LAYER[i], MIXER_INDEX[i]
    if kind == "gdn":
        return tuple(W["gdn_" + k][j] for k in ("w_in", "conv_w", "a_log", "dt_bias", "o_norm_g", "w_out"))
    if kind == "s5":
        return tuple(W["s5_" + k][j] for k in ("w_in", "lam_re", "lam_im", "log_dt", "b_re", "b_im", "c_re", "c_im", "d", "w_out"))
    return tuple(W["m2_" + k][j] for k in ("w_in", "conv_w", "conv_b", "dt_bias", "a_log", "d", "norm_g", "w_out"))


def _local_step(x, target, W):
    h = x
    saved = []
    for i in range(DEPTH):
        kind = MIXER_OF_LAYER[i]
        args = _mixer_args(W, i)
        if kind == "gdn":
            h, sm = _gdn_fwd_layer(i, h, W["norm_mix_g"][i], *args)
        elif kind == "s5":
            h, sm = _s5_fwd_layer(h, W["norm_mix_g"][i], *args)
        else:
            h, sm = _m2_fwd_layer(h, W["norm_mix_g"][i], *args)
        h, sp = _mlp_fwd(i, h, W["norm_mlp_g"][i], W["mlp_w1"][i], W["mlp_w2"][i])
        saved.append((sm, sp))
    loss, dh, dgf = _loss_head(h, W["final_norm_g"], target)
    G = {"final_norm_g": dgf[0], "norm_mix_g": [None] * DEPTH, "norm_mlp_g": [None] * DEPTH,
         "mlp_w1": [None] * DEPTH, "mlp_w2": [None] * DEPTH}
    mix = {}
    for i in reversed(range(DEPTH)):
        kind = MIXER_OF_LAYER[i]
        sm, sp = saved[i]
        dh, G["norm_mlp_g"][i], G["mlp_w1"][i], G["mlp_w2"][i] = _mlp_bwd(i, dh, sp, W["norm_mlp_g"][i], W["mlp_w1"][i], W["mlp_w2"][i])
        args = _mixer_args(W, i)
        if kind == "gdn":
            dh, G["norm_mix_g"][i], gm = _gdn_bwd_layer(i, dh, sm, W["norm_mix_g"][i], *args)
        elif kind == "s5":
            dh, G["norm_mix_g"][i], gm = _s5_bwd_layer(dh, sm, W["norm_mix_g"][i], *args)
        else:
            dh, G["norm_mix_g"][i], gm = _m2_bwd_layer(dh, sm, W["norm_mix_g"][i], *args)
        for k, v in gm.items():
            mix.setdefault(kind + "_" + k, {})[MIXER_INDEX[i]] = v
    for k, d in mix.items():
        G[k] = [d[j] for j in sorted(d)]
    return loss, dh, {k: (jnp.stack(v) if isinstance(v, list) else v) for k, v in G.items()}


ADAM_ROWS = 128


def _adamw(name, w, g, m, v):
    R, C = w.shape
    tr = _tile(R, (ADAM_ROWS, SUBLANES))

    def body(w_ref, g_ref, m_ref, v_ref, d_ref, mo_ref, vo_ref):
        gg = g_ref[...]
        mn = ADAM_B1 * m_ref[...] + (1.0 - ADAM_B1) * gg
        vn = ADAM_B2 * v_ref[...] + (1.0 - ADAM_B2) * (gg * gg)
        m_hat = mn / (1.0 - ADAM_B1 ** ADAM_STEP)
        v_hat = vn / (1.0 - ADAM_B2 ** ADAM_STEP)
        d_ref[...] = -ADAM_LR * (m_hat / (jnp.sqrt(v_hat) + ADAM_EPS) + ADAM_WD * w_ref[...])
        mo_ref[...] = mn
        vo_ref[...] = vn

    blk = pl.BlockSpec((tr, C), lambda i: (i, 0))
    return pl.pallas_call(
        body, name=name, grid=(R // tr,), in_specs=[blk] * 4, out_specs=[blk] * 3,
        out_shape=[jax.ShapeDtypeStruct((R, C), F32)] * 3, compiler_params=_params("parallel"),
    )(w, g, m, v)


MESH = pl.DeviceIdType.MESH
ANY = pl.BlockSpec(memory_space=pl.ANY)
N_CHIPS = 4
N_DEV = 8


def _position():
    return lax.axis_index("x"), lax.axis_index("y"), lax.axis_index("c")


def _gather_shards(wp):
    R, C = wp.shape
    half = R // 2

    def body(w_ref, out_ref, send_sems, recv_sems, local_sem):
        x, y, c = _position()
        sibling = (x, y, 1 - c)
        chips = [(1 - x, y), (x, 1 - y), (1 - x, 1 - y)]

        def piece(cx, cy, hc):
            return out_ref.at[2 * cx + cy, pl.ds(hc * half, half), :]

        def copy(k, src, dst, to):
            return pltpu.make_async_remote_copy(src_ref=src, dst_ref=dst, send_sem=send_sems.at[k], recv_sem=recv_sems.at[k],
                                                device_id=to, device_id_type=MESH)

        mine = pltpu.make_async_copy(w_ref, out_ref.at[2 * x + y], local_sem)
        mine.start()
        first = [copy(j, w_ref.at[pl.ds(c * half, half), :], piece(x, y, c), (*chip, c)) for j, chip in enumerate(chips)]
        for cp in first:
            cp.start()
        passed = [copy(3 + j, piece(*chip, c), piece(*chip, c), sibling) for j, chip in enumerate(chips)]
        for j, chip in enumerate(chips):
            copy(j, piece(*chip, c), piece(*chip, c), sibling).wait_recv()
            passed[j].start()
        for j, chip in enumerate(chips):
            copy(3 + j, piece(*chip, 1 - c), piece(*chip, 1 - c), sibling).wait_recv()
        for cp in first + passed:
            cp.wait_send()
        mine.wait()

    return pl.pallas_call(
        body, name="gather_shards", in_specs=[ANY], out_specs=ANY,
        out_shape=jax.ShapeDtypeStruct((N_CHIPS, R, C), wp.dtype),
        scratch_shapes=[pltpu.SemaphoreType.DMA((6,)), pltpu.SemaphoreType.DMA((6,)), pltpu.SemaphoreType.DMA],
    )(wp)


def _exchange_pieces(gp):
    _, R, C = gp.shape
    half = R // 2

    def body(g_ref, out_ref, send_sems, recv_sems, local_sem):
        x, y, c = _position()
        me = 4 * x + 2 * y + c
        mine = pltpu.make_async_copy(g_ref.at[2 * x + y, pl.ds(c * half, half), :], out_ref.at[me], local_sem)
        mine.start()
        copies = []
        for j in range(N_DEV - 1):
            fx, fy, fc = ((j + 1) >> 2) & 1, ((j + 1) >> 1) & 1, (j + 1) & 1
            px, py, pc = x ^ fx, y ^ fy, c ^ fc
            copies.append(pltpu.make_async_remote_copy(
                src_ref=g_ref.at[2 * px + py, pl.ds(pc * half, half), :], dst_ref=out_ref.at[me],
                send_sem=send_sems.at[j], recv_sem=recv_sems.at[j], device_id=(px, py, pc), device_id_type=MESH))
        for cp in copies:
            cp.start()
        for j in range(N_DEV - 1):
            fx, fy, fc = ((j + 1) >> 2) & 1, ((j + 1) >> 1) & 1, (j + 1) & 1
            peer = 4 * (x ^ fx) + 2 * (y ^ fy) + (c ^ fc)
            pltpu.make_async_remote_copy(
                src_ref=out_ref.at[peer], dst_ref=out_ref.at[peer], send_sem=send_sems.at[j], recv_sem=recv_sems.at[j],
                device_id=(x ^ fx, y ^ fy, c ^ fc), device_id_type=MESH).wait_recv()
        for cp in copies:
            cp.wait_send()
        mine.wait()

    return pl.pallas_call(
        body, name="exchange_pieces", in_specs=[ANY], out_specs=ANY,
        out_shape=jax.ShapeDtypeStruct((N_DEV, half, C), gp.dtype),
        scratch_shapes=[pltpu.SemaphoreType.DMA((N_DEV - 1,)), pltpu.SemaphoreType.DMA((N_DEV - 1,)), pltpu.SemaphoreType.DMA],
    )(gp)


def _sum_pieces(name, pieces):
    n, R, C = pieces.shape
    tr = _tile(R, (256, 128, 24, SUBLANES))

    def body(p_ref, o_ref):
        acc = p_ref[0].astype(F32)
        for s in range(1, n):
            acc = acc + p_ref[s].astype(F32)
        o_ref[...] = acc

    return pl.pallas_call(
        body, name=name, grid=(R // tr,),
        in_specs=[pl.BlockSpec((n, tr, C), lambda i: (0, i, 0))], out_specs=pl.BlockSpec((tr, C), lambda i: (i, 0)),
        out_shape=jax.ShapeDtypeStruct((R, C), F32), compiler_params=_params("parallel"),
    )(pieces)


def _swap_halves(s):
    half, C = s.shape

    def body(s_ref, out_ref, send_sem, recv_sem, local_sem):
        x, y, c = _position()
        mine = pltpu.make_async_copy(s_ref, out_ref.at[pl.ds(c * half, half), :], local_sem)
        mine.start()
        cp = pltpu.make_async_remote_copy(src_ref=s_ref, dst_ref=out_ref.at[pl.ds(c * half, half), :], send_sem=send_sem,
                                          recv_sem=recv_sem, device_id=(x, y, 1 - c), device_id_type=MESH)
        cp.start()
        pltpu.make_async_remote_copy(src_ref=s_ref, dst_ref=out_ref.at[pl.ds((1 - c) * half, half), :], send_sem=send_sem,
                                     recv_sem=recv_sem, device_id=(x, y, 1 - c), device_id_type=MESH).wait_recv()
        cp.wait_send()
        mine.wait()

    return pl.pallas_call(
        body, name="swap_halves", in_specs=[ANY], out_specs=ANY,
        out_shape=jax.ShapeDtypeStruct((2 * half, C), s.dtype),
        scratch_shapes=[pltpu.SemaphoreType.DMA, pltpu.SemaphoreType.DMA, pltpu.SemaphoreType.DMA],
    )(s)


def _gather_small(name, blk):
    m_per, n = blk.shape

    def body(x_ref, out_ref, send_sems, recv_sems, local_sem):
        x, y, c = _position()
        me, sibling = (x, y, c), (x, y, 1 - c)
        chips = [(1 - x, y), (x, 1 - y), (1 - x, 1 - y)]

        def rows(px, py, pc):
            return out_ref.at[pl.ds((4 * px + 2 * py + pc) * m_per, m_per), :]

        def copy(k, block, to, src=None):
            return pltpu.make_async_remote_copy(src_ref=rows(*block) if src is None else src, dst_ref=rows(*block),
                                                send_sem=send_sems.at[k], recv_sem=recv_sems.at[k], device_id=to, device_id_type=MESH)

        mine = pltpu.make_async_copy(x_ref, rows(*me), local_sem)
        mine.start()
        first = [copy(0, me, sibling, src=x_ref)] + [copy(1 + j, me, (*chip, c), src=x_ref) for j, chip in enumerate(chips)]
        for cp in first:
            cp.start()
        passed = [copy(4 + j, (*chip, c), sibling) for j, chip in enumerate(chips)]
        for j, chip in enumerate(chips):
            copy(1 + j, (*chip, c), me).wait_recv()
            passed[j].start()
        copy(0, sibling, me).wait_recv()
        for j, chip in enumerate(chips):
            copy(4 + j, (*chip, 1 - c), me).wait_recv()
        for cp in first + passed:
            cp.wait_send()
        mine.wait()

    return pl.pallas_call(
        body, name=name, out_shape=jax.ShapeDtypeStruct((N_DEV * m_per, n), blk.dtype),
        in_specs=[pl.BlockSpec(memory_space=pltpu.VMEM)], out_specs=pl.BlockSpec(memory_space=pltpu.VMEM),
        scratch_shapes=[pltpu.SemaphoreType.DMA((7,)), pltpu.SemaphoreType.DMA((7,)), pltpu.SemaphoreType.DMA],
        compiler_params=pltpu.CompilerParams(vmem_limit_bytes=VMEM_LIMIT_BYTES),
    )(blk)


WEIGHTS = ("norm_mix_g", "norm_mlp_g", "mlp_w1", "mlp_w2", "gdn_w_in", "gdn_conv_w", "gdn_a_log", "gdn_dt_bias", "gdn_o_norm_g",
           "gdn_w_out", "s5_w_in", "s5_lam_re", "s5_lam_im", "s5_log_dt", "s5_b_re", "s5_b_im", "s5_c_re", "s5_c_im", "s5_d",
           "s5_w_out", "m2_w_in", "m2_conv_w", "m2_conv_b", "m2_dt_bias", "m2_a_log", "m2_d", "m2_norm_g", "m2_w_out",
           "final_norm_g")
BIG = {"mlp_w1": 2, "mlp_w2": 1, "gdn_w_in": 2, "gdn_w_out": 1, "s5_w_in": 1, "s5_w_out": 2, "m2_w_in": 2, "m2_w_out": 1}
SMALL_CUT = {"gdn_conv_w": 2, "m2_conv_w": 2, "m2_conv_b": 1, "m2_norm_g": 1}
PACK_COLS = 1024
PACK_ROW_MULTIPLE = 512


def _pack(arrays, cols, row_multiple, dtype):
    flat = jnp.concatenate([a.reshape(-1).astype(dtype) for a in arrays])
    n = -(-flat.shape[0] // (cols * row_multiple)) * cols * row_multiple
    return jnp.pad(flat, (0, n - flat.shape[0])).reshape(-1, cols)


def _unpack(packed, shapes):
    flat = packed.reshape(-1)
    out, off = [], 0
    for shp in shapes:
        n = math.prod(shp)
        out.append(flat[off:off + n].reshape(shp))
        off += n
    return out


def _cut(a, axis, k):
    n = a.shape[axis] // N_CHIPS
    return lax.slice_in_dim(a, k * n, (k + 1) * n, axis=axis)


def kernel(x, norm_mix_g, norm_mlp_g, mlp_w1, mlp_w2, gdn_w_in, gdn_conv_w, gdn_a_log, gdn_dt_bias, gdn_o_norm_g, gdn_w_out, s5_w_in, s5_lam_re, s5_lam_im, s5_log_dt, s5_b_re, s5_b_im, s5_c_re, s5_c_im, s5_d, s5_w_out, m2_w_in, m2_conv_w, m2_conv_b, m2_dt_bias, m2_a_log, m2_d, m2_norm_g, m2_w_out, final_norm_g, loss_target, m_norm_mix_g, m_norm_mlp_g, m_mlp_w1, m_mlp_w2, m_gdn_w_in, m_gdn_conv_w, m_gdn_a_log, m_gdn_dt_bias, m_gdn_o_norm_g, m_gdn_w_out, m_s5_w_in, m_s5_lam_re, m_s5_lam_im, m_s5_log_dt, m_s5_b_re, m_s5_b_im, m_s5_c_re, m_s5_c_im, m_s5_d, m_s5_w_out, m_m2_w_in, m_m2_conv_w, m_m2_conv_b, m_m2_dt_bias, m_m2_a_log, m_m2_d, m_m2_norm_g, m_m2_w_out, m_final_norm_g, v_norm_mix_g, v_norm_mlp_g, v_mlp_w1, v_mlp_w2, v_gdn_w_in, v_gdn_conv_w, v_gdn_a_log, v_gdn_dt_bias, v_gdn_o_norm_g, v_gdn_w_out, v_s5_w_in, v_s5_lam_re, v_s5_lam_im, v_s5_log_dt, v_s5_b_re, v_s5_b_im, v_s5_c_re, v_s5_c_im, v_s5_d, v_s5_w_out, v_m2_w_in, v_m2_conv_w, v_m2_conv_b, v_m2_dt_bias, v_m2_a_log, v_m2_d, v_m2_norm_g, v_m2_w_out, v_final_norm_g):
    given = dict(locals())
    w = {n: given[n] for n in WEIGHTS}
    mom = {n: given["m_" + n] for n in WEIGHTS}
    var = {n: given["v_" + n] for n in WEIGHTS}
    big, small_cut = tuple(BIG), tuple(SMALL_CUT)
    small = tuple(n for n in WEIGHTS if n not in BIG)
    chip = 2 * lax.axis_index("x") + lax.axis_index("y")

    shards = _gather_shards(_pack([w[n] for n in big], PACK_COLS, PACK_ROW_MULTIPLE, BF16))
    per_chip = [_unpack(shards[k], [w[n].shape for n in big]) for k in range(N_CHIPS)]
    W = {n: jnp.concatenate([per_chip[k][i] for k in range(N_CHIPS)], axis=BIG[n]) for i, n in enumerate(big)}
    W["gdn_w_in"] = jnp.pad(W["gdn_w_in"], ((0, 0), (0, 0), (0, GDN_IN_PAD - GDN_IN)))
    W["m2_w_in"] = jnp.pad(W["m2_w_in"], ((0, 0), (0, 0), (0, M2_IN_PAD - M2_IN)))
    cut_blk = _pack([w[n] for n in small_cut], LANES, SUBLANES, F32)
    cut_all = _gather_small("gather_small_params", cut_blk).reshape(N_DEV, *cut_blk.shape)
    per_chip = [_unpack(cut_all[2 * k], [w[n].shape for n in small_cut]) for k in range(N_CHIPS)]
    W.update({n: jnp.concatenate([per_chip[k][i] for k in range(N_CHIPS)], axis=SMALL_CUT[n]) for i, n in enumerate(small_cut)})
    W.update({n: w[n] for n in small if n not in SMALL_CUT})

    loss, grad_x, G = _local_step(x[0], loss_target[0], W)
    loss = lax.psum(loss[0, 0], ("x", "y", "c"))

    gp = jnp.stack([_pack([_cut(G[n], BIG[n], k) for n in big], PACK_COLS, PACK_ROW_MULTIPLE, BF16) for k in range(N_CHIPS)])
    g_shard = _swap_halves(_sum_pieces("sum_grads", _exchange_pieces(gp)))
    grads = dict(zip(big, _unpack(g_shard, [w[n].shape for n in big])))
    sg = _pack([G[n] for n in small], LANES, SUBLANES, F32)
    sg_sum = _sum_pieces("sum_small_grads", _gather_small("gather_small_grads", sg).reshape(N_DEV, *sg.shape))
    for n, g in zip(small, _unpack(sg_sum, [G[n].shape for n in small])):
        if n in SMALL_CUT:
            width = g.shape[SMALL_CUT[n]] // N_CHIPS
            g = lax.dynamic_slice_in_dim(g, chip * width, width, axis=SMALL_CUT[n])
        grads[n] = g.reshape(w[n].shape)

    delta, new_m, new_v = {}, {}, {}
    for n in big:
        as2d = lambda a: a.reshape(-1, a.shape[-1])
        outs = _adamw("adamw_" + n, as2d(w[n]), as2d(grads[n]), as2d(mom[n]), as2d(var[n]))
        delta[n], new_m[n], new_v[n] = (o.reshape(w[n].shape) for o in outs)
    packs = [_pack([t[n] for n in small], LANES, SUBLANES, F32) for t in (w, grads, mom, var)]
    outs = _adamw("adamw_small", *packs)
    for t, o in zip((delta, new_m, new_v), outs):
        t.update(zip(small, _unpack(o, [w[n].shape for n in small])))

    return (loss, grad_x[None], *[grads[n] for n in WEIGHTS], *[delta[n] for n in WEIGHTS], *[new_m[n] for n in WEIGHTS],
            *[new_v[n] for n in WEIGHTS])
```

```python
import functools
import math

import numpy as np
import jax
import jax.numpy as jnp
from jax import lax
from jax.experimental import pallas as pl
from jax.experimental.pallas import tpu as pltpu

F32 = jnp.float32
BF16 = jnp.bfloat16

D_MODEL = 1024
D_FF = 4096
DEPTH = 4
CHUNK = 64
RMS_EPS = 1e-6
CONV_W = 4
GDN_HEADS = 8
GDN_DK = 128
GDN_IN = 4112
GDN_IN_PAD = 4224
S5_GROUPS = 64
S5_STATE = 64
S5_GROUP = 16
S5_BLOCKS = 8
M2_INNER = 2048
M2_HEADS = 32
M2_GROUPS = 8
M2_STATE = 128
M2_CONV_CH = 4096
M2_IN = 6176
M2_IN_PAD = 6272
ADAM_LR, ADAM_B1, ADAM_B2, ADAM_EPS, ADAM_WD, ADAM_STEP = 0.001, 0.9, 0.999, 1e-08, 0.01, 10

VMEM_LIMIT_BYTES = 56 * 1024 * 1024
SUBLANES = 8
LANES = 128


def _params(*sem):
    return pltpu.CompilerParams(dimension_semantics=tuple(sem) if sem else None, vmem_limit_bytes=VMEM_LIMIT_BYTES)


NN, NT, TN = ((1,), (0,)), ((1,), (1,)), ((0,), (0,))
_DOT_TRANSPOSES = {NN: ((NT, "gb"), (TN, "ag")), NT: ((NN, "gb"), (TN, "ga")), TN: ((NT, "bg"), (NN, "ag"))}


def _mxu(a, b, dims):
    return lax.dot_general(a.astype(BF16), b.astype(BF16), (dims, ((), ())), preferred_element_type=F32)


@functools.partial(jax.custom_vjp, nondiff_argnums=(2,))
def _dot(a, b, dims=NN):
    return _mxu(a, b, dims)


def _dot_fwd(a, b, dims):
    return _mxu(a, b, dims), (a, b)


def _dot_bwd(dims, res, g):
    ops = dict(a=res[0], b=res[1], g=g)
    (da_dims, da_ops), (db_dims, db_ops) = _DOT_TRANSPOSES[dims]
    return (_mxu(ops[da_ops[0]], ops[da_ops[1]], da_dims).astype(res[0].dtype),
            _mxu(ops[db_ops[0]], ops[db_ops[1]], db_dims).astype(res[1].dtype))


_dot.defvjp(_dot_fwd, _dot_bwd)


def _nt(a, b):
    return _dot(a, b, NT)


def _tn(a, b):
    return _dot(a, b, TN)


def _split3(x):
    x1 = x.astype(BF16)
    r = x - x1.astype(F32)
    x2 = r.astype(BF16)
    return x1, x2, (r - x2.astype(F32)).astype(BF16)


def _sel_mxu(x, sel, dims, x_first):
    f = (lambda p: lax.dot_general(p, sel.astype(BF16), (dims, ((), ())), preferred_element_type=F32)) if x_first else \
        (lambda p: lax.dot_general(sel.astype(BF16), p, (dims, ((), ())), preferred_element_type=F32))
    x1, x2, x3 = _split3(x)
    return f(x1) + (f(x2) + f(x3))


@jax.custom_vjp
def _pick(x, sel):
    return _sel_mxu(x, sel, NN, True)


def _pick_fwd(x, sel):
    return _sel_mxu(x, sel, NN, True), sel


def _pick_bwd(sel, g):
    return _sel_mxu(g, sel, NT, True), jnp.zeros_like(sel)


_pick.defvjp(_pick_fwd, _pick_bwd)


@jax.custom_vjp
def _accum(sel, x):
    return _sel_mxu(x, sel, NN, False)


def _accum_fwd(sel, x):
    return _sel_mxu(x, sel, NN, False), sel


def _accum_bwd(sel, g):
    return jnp.zeros_like(sel), _sel_mxu(g, sel, TN, False)


_accum.defvjp(_accum_fwd, _accum_bwd)


def _dot3(a, b, dims=NN):
    ah, bh = a.astype(BF16), b.astype(BF16)
    al, bl = (a - ah.astype(F32)).astype(BF16), (b - bh.astype(F32)).astype(BF16)
    f = lambda p, q: lax.dot_general(p, q, (dims, ((), ())), preferred_element_type=F32)
    return f(ah, bh) + (f(ah, bl) + f(al, bh))


def _neumann(x, r, dims):
    r = r + _dot3(x, r, dims)
    for _ in range(5):
        x = _dot3(x, x)
        r = r + _dot3(x, r, dims)
    return r


@jax.custom_vjp
def _unit_lower_solve(a, rhs):
    return _neumann(-a, rhs, NN)


def _unit_lower_solve_fwd(a, rhs):
    sol = _neumann(-a, rhs, NN)
    return sol, (a, sol)


def _unit_lower_solve_bwd(res, ct):
    a, sol = res
    d_rhs = _neumann(-a, ct, TN)
    return -_dot3(d_rhs, sol, NT), d_rhs


_unit_lower_solve.defvjp(_unit_lower_solve_fwd, _unit_lower_solve_bwd)


def _sigmoid(x):
    return 1.0 / (1.0 + jnp.exp(-x))


def _softplus(x):
    return jnp.maximum(x, 0.0) + jnp.log(1.0 + jnp.exp(-jnp.abs(x)))


def _iota2(shape, axis):
    return lax.broadcasted_iota(jnp.int32, shape, axis)


def _tile(n, cands):
    for c in cands:
        if n % c == 0:
            return c
    return n


def _mm(name, a, b, mode, out_dtypes, epi=None, extras=(), tm=512, tn=None):
    if mode == "nn":
        (M, K), N = a.shape, b.shape[1]
    elif mode == "nt":
        (M, K), N = a.shape, b.shape[0]
    else:
        (K, M), N = a.shape, b.shape[1]
    tm = _tile(M, (tm, 256, 128))
    tn = tn or _tile(N, (512, 384, 896, 256, 128))
    if mode == "nn":
        a_spec, b_spec = pl.BlockSpec((tm, K), lambda i, j: (i, 0)), pl.BlockSpec((K, tn), lambda i, j: (0, j))
        dims = NN
    elif mode == "nt":
        a_spec, b_spec = pl.BlockSpec((tm, K), lambda i, j: (i, 0)), pl.BlockSpec((tn, K), lambda i, j: (j, 0))
        dims = NT
    else:
        a_spec, b_spec = pl.BlockSpec((K, tm), lambda i, j: (0, i)), pl.BlockSpec((K, tn), lambda i, j: (0, j))
        dims = TN
    n_ex = len(extras)

    def body(a_ref, b_ref, *rest):
        acc = _mxu(a_ref[...], b_ref[...], dims)
        res = epi(acc, *[e[...] for e in rest[:n_ex]]) if epi is not None else (acc,)
        for o_ref, r in zip(rest[n_ex:], res):
            o_ref[...] = r.astype(o_ref.dtype)

    tile = pl.BlockSpec((tm, tn), lambda i, j: (i, j))
    out = pl.pallas_call(
        body, name=name, grid=(M // tm, N // tn),
        in_specs=[a_spec, b_spec] + [tile] * n_ex,
        out_specs=[tile] * len(out_dtypes),
        out_shape=[jax.ShapeDtypeStruct((M, N), d) for d in out_dtypes],
        compiler_params=_params("parallel", "parallel"),
    )(a, b, *extras)
    return out if len(out_dtypes) > 1 else out[0]


def _rms_fwd(name, h, g):
    L, D = h.shape
    tr = _tile(L, (256, 128))

    def body(h_ref, g_ref, o_ref):
        x = h_ref[...]
        r = lax.rsqrt(jnp.mean(x * x, axis=-1, keepdims=True) + RMS_EPS)
        o_ref[...] = (x * r * g_ref[...]).astype(o_ref.dtype)

    return pl.pallas_call(
        body, name=name, grid=(L // tr,),
        in_specs=[pl.BlockSpec((tr, D), lambda i: (i, 0)), pl.BlockSpec((1, D), lambda i: (0, 0))],
        out_specs=pl.BlockSpec((tr, D), lambda i: (i, 0)),
        out_shape=jax.ShapeDtypeStruct((L, D), BF16),
        compiler_params=_params("parallel"),
    )(h, g.reshape(1, D))


def _rms_bwd(name, h, g, dhn, dres):
    L, D = h.shape
    tr = _tile(L, (256, 128))

    def body(h_ref, g_ref, dhn_ref, dres_ref, dh_ref, dg_ref):
        x = h_ref[...]
        r = lax.rsqrt(jnp.mean(x * x, axis=-1, keepdims=True) + RMS_EPS)
        xh = x * r
        dy = dhn_ref[...]
        dxh = dy * g_ref[...]
        dh_ref[...] = dres_ref[...] + r * (dxh - xh * jnp.mean(dxh * xh, axis=-1, keepdims=True))

        @pl.when(pl.program_id(0) == 0)
        def _():
            dg_ref[...] = jnp.zeros_like(dg_ref)

        dg_ref[...] += jnp.sum(dy * xh, axis=0, keepdims=True)

    row = pl.BlockSpec((tr, D), lambda i: (i, 0))
    vec = pl.BlockSpec((1, D), lambda i: (0, 0))
    return pl.pallas_call(
        body, name=name, grid=(L // tr,),
        in_specs=[row, vec, row, row], out_specs=[row, vec],
        out_shape=[jax.ShapeDtypeStruct((L, D), F32), jax.ShapeDtypeStruct((1, D), F32)],
        compiler_params=_params("arbitrary"),
    )(h, g.reshape(1, D), dhn, dres)


def _loss_head(h, g, target):
    L, D = h.shape
    tr = _tile(L, (256, 128))

    def body(h_ref, g_ref, t_ref, loss_ref, dh_ref, dg_ref):
        x = h_ref[...]
        r = lax.rsqrt(jnp.mean(x * x, axis=-1, keepdims=True) + RMS_EPS)
        xh = x * r
        err = xh * g_ref[...] - t_ref[...]
        dy = err * (1.0 / D)
        dxh = dy * g_ref[...]
        dh_ref[...] = r * (dxh - xh * jnp.mean(dxh * xh, axis=-1, keepdims=True))

        @pl.when(pl.program_id(0) == 0)
        def _():
            dg_ref[...] = jnp.zeros_like(dg_ref)
            loss_ref[...] = jnp.zeros_like(loss_ref)

        dg_ref[...] += jnp.sum(dy * xh, axis=0, keepdims=True)
        loss_ref[...] += (0.5 / D) * jnp.sum(jnp.sum(err * err, axis=-1, keepdims=True), axis=0, keepdims=True)

    row = pl.BlockSpec((tr, D), lambda i: (i, 0))
    vec = pl.BlockSpec((1, D), lambda i: (0, 0))
    return pl.pallas_call(
        body, name="loss_head", grid=(L // tr,),
        in_specs=[row, vec, row], out_specs=[pl.BlockSpec((1, 1), lambda i: (0, 0)), row, vec],
        out_shape=[jax.ShapeDtypeStruct((1, 1), F32), jax.ShapeDtypeStruct((L, D), F32), jax.ShapeDtypeStruct((1, D), F32)],
        compiler_params=_params("arbitrary"),
    )(h, g.reshape(1, D), target)


def _glu_fwd(h, ag):
    L, D = h.shape
    tr = _tile(L, (256, 128))

    def body(h_ref, v_ref, g_ref, o_ref):
        o_ref[...] = h_ref[...] + v_ref[...] * _sigmoid(g_ref[...])

    return pl.pallas_call(
        body, name="s5_glu_fwd", grid=(L // tr,),
        in_specs=[pl.BlockSpec((tr, D), lambda i: (i, 0)), pl.BlockSpec((tr, D), lambda i: (i, 0)),
                  pl.BlockSpec((tr, D), lambda i: (i, 1))],
        out_specs=pl.BlockSpec((tr, D), lambda i: (i, 0)),
        out_shape=jax.ShapeDtypeStruct((L, D), F32),
        compiler_params=_params("parallel"),
    )(h, ag, ag)


def _glu_bwd(dh, ag):
    L, D = dh.shape
    tr = _tile(L, (256, 128))

    def body(dh_ref, v_ref, g_ref, dv_ref, dg_ref):
        s = _sigmoid(g_ref[...])
        d = dh_ref[...]
        dv_ref[...] = d * s
        dg_ref[...] = d * v_ref[...] * s * (1.0 - s)

    dv, dg = pl.pallas_call(
        body, name="s5_glu_bwd", grid=(L // tr,),
        in_specs=[pl.BlockSpec((tr, D), lambda i: (i, 0)), pl.BlockSpec((tr, D), lambda i: (i, 0)),
                  pl.BlockSpec((tr, D), lambda i: (i, 1))],
        out_specs=[pl.BlockSpec((tr, D), lambda i: (i, 0))] * 2,
        out_shape=[jax.ShapeDtypeStruct((L, D), F32)] * 2,
        compiler_params=_params("parallel"),
    )(dh, ag, ag)
    return jnp.concatenate([dv, dg], axis=1)


CONV_ROWS = 128
CONV_COLS = 512


def _shift_rows(cat, s):
    if s == 0:
        return cat[SUBLANES:, :]
    return pltpu.roll(cat, s, axis=0)[SUBLANES:, :]


def _conv_fwd(name, p, col0, w, b):
    L = p.shape[0]
    C = w.shape[1]
    tc = _tile(C, (CONV_COLS, 256))
    cb0 = col0 // tc
    nr = L // CONV_ROWS

    def body(x_ref, w_ref, b_ref, o_ref):
        def step(r, carry):
            r0 = pl.multiple_of(r * CONV_ROWS, CONV_ROWS)
            cur = x_ref[pl.ds(r0, CONV_ROWS), :]
            p0 = pl.multiple_of(jnp.maximum(r0 - SUBLANES, 0), SUBLANES)
            prev = jnp.where(r > 0, x_ref[pl.ds(p0, SUBLANES), :], 0.0)
            cat = jnp.concatenate([prev, cur], axis=0)
            acc = b_ref[...] + w_ref[3:4, :] * cur
            for k in range(CONV_W - 1):
                acc = acc + w_ref[k:k + 1, :] * _shift_rows(cat, CONV_W - 1 - k)
            o_ref[pl.ds(r0, CONV_ROWS), :] = acc * _sigmoid(acc)
            return carry

        lax.fori_loop(0, nr, step, 0)

    return pl.pallas_call(
        body, name=name, grid=(C // tc,),
        in_specs=[pl.BlockSpec((L, tc), lambda j: (0, cb0 + j)), pl.BlockSpec((CONV_W, tc), lambda j: (0, j)),
                  pl.BlockSpec((1, tc), lambda j: (0, j))],
        out_specs=pl.BlockSpec((L, tc), lambda j: (0, j)),
        out_shape=jax.ShapeDtypeStruct((L, C), F32),
        compiler_params=_params("parallel"),
    )(p, w, b)


def _conv_bwd(name, p, col0, w, b, dout):
    L = p.shape[0]
    C = w.shape[1]
    tc = _tile(C, (CONV_COLS, 256))
    cb0 = col0 // tc
    nr = L // CONV_ROWS

    def body(x_ref, w_ref, b_ref, do_ref, dx_ref, dw_ref, db_ref, dpre_ref):
        def step1(r, carry):
            dw0, dw1, dw2, dw3, dbb = carry
            r0 = pl.multiple_of(r * CONV_ROWS, CONV_ROWS)
            cur = x_ref[pl.ds(r0, CONV_ROWS), :]
            p0 = pl.multiple_of(jnp.maximum(r0 - SUBLANES, 0), SUBLANES)
            prev = jnp.where(r > 0, x_ref[pl.ds(p0, SUBLANES), :], 0.0)
            cat = jnp.concatenate([prev, cur], axis=0)
            sh = [_shift_rows(cat, CONV_W - 1 - k) for k in range(CONV_W - 1)] + [cur]
            acc = b_ref[...] + w_ref[3:4, :] * cur
            for k in range(CONV_W - 1):
                acc = acc + w_ref[k:k + 1, :] * sh[k]
            sg = _sigmoid(acc)
            dpre = do_ref[pl.ds(r0, CONV_ROWS), :] * (sg + acc * sg * (1.0 - sg))
            dpre_ref[pl.ds(r0, CONV_ROWS), :] = dpre
            dws = [d + jnp.sum(dpre * s, axis=0, keepdims=True) for d, s in zip((dw0, dw1, dw2, dw3), sh)]
            return (*dws, dbb + jnp.sum(dpre, axis=0, keepdims=True))

        z = jnp.zeros((1, tc), F32)
        dw0, dw1, dw2, dw3, dbb = lax.fori_loop(0, nr, step1, (z, z, z, z, z))
        dw_ref[...] = jnp.concatenate([dw0, dw1, dw2, dw3, z, z, z, z], axis=0)
        db_ref[...] = dbb

        def step2(r, carry):
            r0 = pl.multiple_of(r * CONV_ROWS, CONV_ROWS)
            cur = dpre_ref[pl.ds(r0, CONV_ROWS), :]
            n0 = pl.multiple_of(jnp.minimum(r0 + CONV_ROWS, L - SUBLANES), SUBLANES)
            nxt = jnp.where(r < nr - 1, dpre_ref[pl.ds(n0, SUBLANES), :], 0.0)
            cat = jnp.concatenate([cur, nxt], axis=0)
            acc = w_ref[3:4, :] * cur
            for k in range(CONV_W - 1):
                s = CONV_W - 1 - k
                acc = acc + w_ref[k:k + 1, :] * pltpu.roll(cat, CONV_ROWS + SUBLANES - s, axis=0)[:CONV_ROWS, :]
            dx_ref[pl.ds(r0, CONV_ROWS), :] = acc
            return carry

        lax.fori_loop(0, nr, step2, 0)

    dx, dw, db = pl.pallas_call(
        body, name=name, grid=(C // tc,),
        in_specs=[pl.BlockSpec((L, tc), lambda j: (0, cb0 + j)), pl.BlockSpec((CONV_W, tc), lambda j: (0, j)),
                  pl.BlockSpec((1, tc), lambda j: (0, j)), pl.BlockSpec((L, tc), lambda j: (0, j))],
        out_specs=[pl.BlockSpec((L, tc), lambda j: (0, j)), pl.BlockSpec((SUBLANES, tc), lambda j: (0, j)),
                   pl.BlockSpec((1, tc), lambda j: (0, j))],
        out_shape=[jax.ShapeDtypeStruct((L, C), F32), jax.ShapeDtypeStruct((SUBLANES, C), F32),
                   jax.ShapeDtypeStruct((1, C), F32)],
        scratch_shapes=[pltpu.VMEM((L, tc), F32)],
        compiler_params=_params("parallel"),
    )(p, w, b, dout)
    return dx, dw[:CONV_W], db


def _chunk_consts():
    r, c = _iota2((CHUNK, CHUNK), 0), _iota2((CHUNK, CHUNK), 1)
    causal = r >= c
    return causal, r > c, (r == c).astype(F32), causal.astype(F32), jnp.ones((CHUNK, CHUNK), F32)


def _gdn_chunk(q, k, v, ab, gate, S, alog, dtb, og, ea, eb):
    causal, strict, eye, tril, ones = _chunk_consts()
    g = -jnp.exp(alog) * _softplus(_pick(ab, ea) + dtb)
    beta = _sigmoid(_pick(ab, eb))
    qn = q * lax.rsqrt(jnp.sum(q * q, axis=-1, keepdims=True) + 1e-6) * (GDN_DK ** -0.5)
    kn = k * lax.rsqrt(jnp.sum(k * k, axis=-1, keepdims=True) + 1e-6)
    gc = _accum(tril, g)
    glast = _accum(ones, g)
    gcol = gc[:, :CHUNK]
    grow = _accum(ones, gcol * eye)
    decay = jnp.exp(jnp.where(causal, gcol - grow, -jnp.inf))
    a = jnp.where(strict, beta[:, :CHUNK] * _nt(kn, kn) * decay, 0.0)
    eg = jnp.exp(gc)
    sol = _unit_lower_solve(a, jnp.concatenate([v * beta, kn * (beta * eg)], axis=1))
    u, w = sol[:, :GDN_DK], sol[:, GDN_DK:]
    qk = _nt(qn, kn) * decay
    v_new = u - _dot(w, S)
    o = _dot(qn * eg, S) + _dot(qk, v_new)
    cd = jnp.exp(glast)
    s_new = jnp.concatenate([cd, cd], axis=0) * S + _tn(kn * jnp.exp(glast - gc), v_new)
    on = o * lax.rsqrt(jnp.mean(o * o, axis=-1, keepdims=True) + RMS_EPS) * og
    return on * (gate * _sigmoid(gate)), s_new


GDN_HB = 4


def _gdn_specs(nc, rev):
    cm = (lambda c: nc - 1 - c) if rev else (lambda c: c)
    blk = lambda off: pl.BlockSpec((CHUNK, GDN_HB * GDN_DK), lambda c, h: (cm(c), off // GDN_HB + h))
    ab = pl.BlockSpec((CHUNK, LANES), lambda c, h: (cm(c), (GDN_IN_PAD - LANES) // LANES))
    hv = pl.BlockSpec((GDN_HB, 1, LANES), lambda c, h: (h, 0, 0))
    og = pl.BlockSpec((1, LANES), lambda c, h: (0, 0))
    em = pl.BlockSpec((GDN_HB, LANES, LANES), lambda c, h: (h, 0, 0))
    st = pl.BlockSpec((None, GDN_HB, GDN_DK, GDN_DK), lambda c, h: (cm(c), h, 0, 0))
    return blk, ab, hv, og, em, st


def _gdn_fwd(qc, kc, vc, p, alog_e, dtb_e, og, ea, eb):
    L = qc.shape[0]
    nc = L // CHUNK
    blk, ab, hv, ogs, em, st = _gdn_specs(nc, False)

    def body(q_ref, k_ref, v_ref, gate_ref, ab_ref, al_ref, dt_ref, og_ref, ea_ref, eb_ref, y_ref, sp_ref, s_scr):
        c, h = pl.program_id(0), pl.program_id(1)
        for i in range(GDN_HB):
            hh = h * GDN_HB + i
            ls = slice(i * GDN_DK, (i + 1) * GDN_DK)

            @pl.when(c == 0)
            def _():
                s_scr[hh] = jnp.zeros((GDN_DK, GDN_DK), F32)

            S = s_scr[hh]
            sp_ref[i] = S
            y, s_new = _gdn_chunk(q_ref[:, ls], k_ref[:, ls], v_ref[:, ls], ab_ref[...], gate_ref[:, ls], S,
                                  al_ref[i], dt_ref[i], og_ref[...], ea_ref[i], eb_ref[i])
            y_ref[:, ls] = y
            s_scr[hh] = s_new

    return pl.pallas_call(
        body, name="gdn_fwd", grid=(nc, GDN_HEADS // GDN_HB),
        in_specs=[blk(0), blk(0), blk(0), blk(3 * GDN_HEADS), ab, hv, hv, ogs, em, em],
        out_specs=[blk(0), st],
        out_shape=[jax.ShapeDtypeStruct((L, D_MODEL), F32), jax.ShapeDtypeStruct((nc, GDN_HEADS, GDN_DK, GDN_DK), F32)],
        scratch_shapes=[pltpu.VMEM((GDN_HEADS, GDN_DK, GDN_DK), F32)],
        compiler_params=_params("arbitrary", "arbitrary"),
    )(qc, kc, vc, p, p, alog_e, dtb_e, og, ea, eb)


def _gdn_bwd(qc, kc, vc, p, alog_e, dtb_e, og, ea, eb, sprev, dy):
    L = qc.shape[0]
    nc = L // CHUNK
    blk, ab, hv, ogs, em, st = _gdn_specs(nc, True)

    def body(q_ref, k_ref, v_ref, gate_ref, ab_ref, al_ref, dt_ref, og_ref, ea_ref, eb_ref, sp_ref, dy_ref,
             dq_ref, dk_ref, dv_ref, dgate_ref, dab_ref, dpar_ref, ds_scr):
        c, h = pl.program_id(0), pl.program_id(1)
        dab_sum = jnp.zeros((CHUNK, LANES), F32)
        for i in range(GDN_HB):
            hh = h * GDN_HB + i
            ls = slice(i * GDN_DK, (i + 1) * GDN_DK)

            @pl.when(c == 0)
            def _():
                ds_scr[hh] = jnp.zeros((GDN_DK, GDN_DK), F32)
                dpar_ref[hh] = jnp.zeros((SUBLANES, LANES), F32)

            ea_m, eb_m = ea_ref[i], eb_ref[i]
            f = lambda q, k, v, a_b, gate, S, al, dt, o_g: _gdn_chunk(q, k, v, a_b, gate, S, al, dt, o_g, ea_m, eb_m)
            _, vjp = jax.vjp(f, q_ref[:, ls], k_ref[:, ls], v_ref[:, ls], ab_ref[...], gate_ref[:, ls], sp_ref[i],
                             al_ref[i], dt_ref[i], og_ref[...])
            dq, dk, dv, dab, dgate, ds, dal, ddt, dog = vjp((dy_ref[:, ls], ds_scr[hh]))
            dq_ref[:, ls] = dq
            dk_ref[:, ls] = dk
            dv_ref[:, ls] = dv
            dgate_ref[:, ls] = dgate
            ds_scr[hh] = ds
            dab_sum = dab_sum + dab
            dpar_ref[hh] += jnp.concatenate([dal, ddt, dog, jnp.zeros((SUBLANES - 3, LANES), F32)], axis=0)

        @pl.when(h == 0)
        def _():
            dab_ref[...] = jnp.zeros_like(dab_ref)

        dab_ref[...] += dab_sum

    return pl.pallas_call(
        body, name="gdn_bwd", grid=(nc, GDN_HEADS // GDN_HB),
        in_specs=[blk(0), blk(0), blk(0), blk(3 * GDN_HEADS), ab, hv, hv, ogs, em, em, st, blk(0)],
        out_specs=[blk(0), blk(0), blk(0), blk(0), pl.BlockSpec((CHUNK, LANES), lambda c, h: (nc - 1 - c, 0)),
                   pl.BlockSpec((GDN_HEADS, SUBLANES, LANES), lambda c, h: (0, 0, 0))],
        out_shape=[jax.ShapeDtypeStruct((L, D_MODEL), F32)] * 4
        + [jax.ShapeDtypeStruct((L, LANES), F32), jax.ShapeDtypeStruct((GDN_HEADS, SUBLANES, LANES), F32)],
        scratch_shapes=[pltpu.VMEM((GDN_HEADS, GDN_DK, GDN_DK), F32)],
        compiler_params=_params("arbitrary", "arbitrary"),
    )(qc, kc, vc, p, p, alog_e, dtb_e, og, ea, eb, sprev, dy)


def _gdn_selectors():
    rows = np.arange(LANES)[None, :, None]
    heads = np.arange(GDN_HEADS)[:, None, None]
    ea = np.broadcast_to(rows == heads, (GDN_HEADS, LANES, LANES)).astype(np.float32)
    eb = np.broadcast_to(rows == heads + GDN_HEADS, (GDN_HEADS, LANES, LANES)).astype(np.float32)
    return jnp.asarray(ea), jnp.asarray(eb)


M2_GW = M2_INNER // M2_GROUPS
M2_HPG = M2_HEADS // M2_GROUPS
M2_HD = M2_INNER // M2_HEADS


def _m2_chunk(x, bm, cm, z, dtr, st, dtb, alog, dsk, ng, e, ecol):
    causal, _, eye, tril, ones = _chunk_consts()
    dt_n = _softplus(dtr + dtb)
    da_n = dt_n * (-jnp.exp(alog))
    cum_n = _accum(tril, da_n)
    tot_n = _accum(ones, da_n)
    dt_w, cum_w, tot_w = _pick(dt_n, e), _pick(cum_n, e), _pick(tot_n, e)
    xdt = x * dt_w
    cb = _nt(cm, bm)
    head = _iota2((CHUNK, M2_GW), 1) // M2_HD
    ydiag = jnp.zeros((CHUNK, M2_GW), F32)
    for r in range(M2_HPG):
        colb = _pick(cum_n, ecol[r])
        rowb = _accum(ones, colb * eye)
        lmat = jnp.exp(jnp.where(causal, colb - rowb, -jnp.inf))
        ydiag = ydiag + jnp.where(head == r, _dot(cb * lmat, xdt), 0.0)
    st_new = _tn(bm, xdt * jnp.exp(tot_w - cum_w))
    cd = jnp.exp(tot_w)
    s_new = jnp.concatenate([cd, cd], axis=0) * st + st_new
    y = ydiag + _dot(cm, st) * jnp.exp(cum_w) + dsk * x
    y = y * (z * _sigmoid(z))
    yn = y * lax.rsqrt(jnp.mean(y * y, axis=-1, keepdims=True) + RMS_EPS) * ng
    return yn, s_new


M2_GB = 2


def _m2_specs(nc, rev):
    cm = (lambda c: nc - 1 - c) if rev else (lambda c: c)
    wide = lambda off: pl.BlockSpec((CHUNK, M2_GB * M2_GW), lambda c, g: (cm(c), off // M2_GB + g))
    nar = lambda off: pl.BlockSpec((CHUNK, M2_GB * LANES), lambda c, g: (cm(c), off // M2_GB + g))
    dts = pl.BlockSpec((CHUNK, LANES), lambda c, g: (cm(c), (M2_IN_PAD - LANES) // LANES))
    v128 = pl.BlockSpec((1, LANES), lambda c, g: (0, 0))
    v256 = pl.BlockSpec((1, M2_GB * M2_GW), lambda c, g: (0, g))
    es = pl.BlockSpec((M2_GB, LANES, M2_GW), lambda c, g: (g, 0, 0))
    ecs = pl.BlockSpec((M2_GB, M2_HPG, LANES, M2_HD), lambda c, g: (g, 0, 0, 0))
    st = pl.BlockSpec((None, M2_GB, M2_STATE, M2_GW), lambda c, g: (cm(c), g, 0, 0))
    return wide, nar, dts, v128, v256, es, ecs, st


def _m2_fwd(xbc, p, dtb, alog, dsk, ng, e, ecol):
    L = xbc.shape[0]
    nc = L // CHUNK
    wide, nar, dts, v128, v256, es, ecs, st = _m2_specs(nc, False)

    def body(x_ref, b_ref, c_ref, z_ref, dt_ref, dtb_ref, al_ref, dsk_ref, ng_ref, e_ref, ec_ref, y_ref, sp_ref, s_scr):
        c, g = pl.program_id(0), pl.program_id(1)
        for i in range(M2_GB):
            gg = g * M2_GB + i
            lw = slice(i * M2_GW, (i + 1) * M2_GW)
            ln = slice(i * LANES, (i + 1) * LANES)

            @pl.when(c == 0)
            def _():
                s_scr[gg] = jnp.zeros((M2_STATE, M2_GW), F32)

            S = s_scr[gg]
            sp_ref[i] = S
            y, s_new = _m2_chunk(x_ref[:, lw], b_ref[:, ln], c_ref[:, ln], z_ref[:, lw], dt_ref[...], S, dtb_ref[...],
                                 al_ref[...], dsk_ref[:, lw], ng_ref[:, lw], e_ref[i], ec_ref[i])
            y_ref[:, lw] = y
            s_scr[gg] = s_new

    return pl.pallas_call(
        body, name="m2_fwd", grid=(nc, M2_GROUPS // M2_GB),
        in_specs=[wide(0), nar(2 * M2_GROUPS), nar(3 * M2_GROUPS), wide(0), dts, v128, v128, v256, v256, es, ecs],
        out_specs=[wide(0), st],
        out_shape=[jax.ShapeDtypeStruct((L, M2_INNER), F32), jax.ShapeDtypeStruct((nc, M2_GROUPS, M2_STATE, M2_GW), F32)],
        scratch_shapes=[pltpu.VMEM((M2_GROUPS, M2_STATE, M2_GW), F32)],
        compiler_params=_params("arbitrary", "arbitrary"),
    )(xbc, xbc, xbc, p, p, dtb, alog, dsk, ng, e, ecol)


def _m2_bwd(xbc, p, dtb, alog, dsk, ng, e, ecol, sprev, dy):
    L = xbc.shape[0]
    nc = L // CHUNK
    wide, nar, dts, v128, v256, es, ecs, st = _m2_specs(nc, True)

    def body(x_ref, b_ref, c_ref, z_ref, dt_ref, dtb_ref, al_ref, dsk_ref, ng_ref, e_ref, ec_ref, sp_ref, dy_ref,
             dx_ref, db_ref, dc_ref, dz_ref, ddt_ref, dnar_ref, dwide_ref, ds_scr):
        c, g = pl.program_id(0), pl.program_id(1)

        @pl.when(jnp.logical_and(c == 0, g == 0))
        def _():
            dnar_ref[...] = jnp.zeros_like(dnar_ref)

        ddt_sum = jnp.zeros((CHUNK, LANES), F32)
        dnar_sum = jnp.zeros((SUBLANES, LANES), F32)
        for i in range(M2_GB):
            gg = g * M2_GB + i
            lw = slice(i * M2_GW, (i + 1) * M2_GW)
            ln = slice(i * LANES, (i + 1) * LANES)

            @pl.when(c == 0)
            def _():
                ds_scr[gg] = jnp.zeros((M2_STATE, M2_GW), F32)
                dwide_ref[gg] = jnp.zeros((SUBLANES, M2_GW), F32)

            e_m, ec_m = e_ref[i], ec_ref[i]
            f = lambda x, bm, cm, z, dtr, S, dtb, al, dsk, ng: _m2_chunk(x, bm, cm, z, dtr, S, dtb, al, dsk, ng, e_m, ec_m)
            _, vjp = jax.vjp(f, x_ref[:, lw], b_ref[:, ln], c_ref[:, ln], z_ref[:, lw], dt_ref[...], sp_ref[i], dtb_ref[...],
                             al_ref[...], dsk_ref[:, lw], ng_ref[:, lw])
            dx, db, dc, dz, ddt, ds, ddtb, dal, ddsk, dng = vjp((dy_ref[:, lw], ds_scr[gg]))
            dx_ref[:, lw] = dx
            db_ref[:, ln] = db
            dc_ref[:, ln] = dc
            dz_ref[:, lw] = dz
            ds_scr[gg] = ds
            ddt_sum = ddt_sum + ddt
            dnar_sum = dnar_sum + jnp.concatenate([ddtb, dal, jnp.zeros((SUBLANES - 2, LANES), F32)], axis=0)
            dwide_ref[gg] += jnp.concatenate([ddsk, dng, jnp.zeros((SUBLANES - 2, M2_GW), F32)], axis=0)

        @pl.when(g == 0)
        def _():
            ddt_ref[...] = jnp.zeros_like(ddt_ref)

        ddt_ref[...] += ddt_sum
        dnar_ref[...] += dnar_sum

    return pl.pallas_call(
        body, name="m2_bwd", grid=(nc, M2_GROUPS // M2_GB),
        in_specs=[wide(0), nar(2 * M2_GROUPS), nar(3 * M2_GROUPS), wide(0), dts, v128, v128, v256, v256, es, ecs, st, wide(0)],
        out_specs=[wide(0), nar(0), nar(0), wide(0), pl.BlockSpec((CHUNK, LANES), lambda c, g: (nc - 1 - c, 0)),
                   pl.BlockSpec((SUBLANES, LANES), lambda c, g: (0, 0)),
                   pl.BlockSpec((M2_GROUPS, SUBLANES, M2_GW), lambda c, g: (0, 0, 0))],
        out_shape=[jax.ShapeDtypeStruct((L, M2_INNER), F32), jax.ShapeDtypeStruct((L, M2_GROUPS * M2_STATE), F32),
                   jax.ShapeDtypeStruct((L, M2_GROUPS * M2_STATE), F32), jax.ShapeDtypeStruct((L, M2_INNER), F32),
                   jax.ShapeDtypeStruct((L, LANES), F32), jax.ShapeDtypeStruct((SUBLANES, LANES), F32),
                   jax.ShapeDtypeStruct((M2_GROUPS, SUBLANES, M2_GW), F32)],
        scratch_shapes=[pltpu.VMEM((M2_GROUPS, M2_STATE, M2_GW), F32)],
        compiler_params=_params("arbitrary", "arbitrary"),
    )(xbc, xbc, xbc, p, p, dtb, alog, dsk, ng, e, ecol, sprev, dy)


def _m2_selectors():
    rows = np.arange(LANES)
    e = np.zeros((M2_GROUPS, LANES, M2_GW), np.float32)
    ecol = np.zeros((M2_GROUPS, M2_HPG, LANES, M2_HD), np.float32)
    for g in range(M2_GROUPS):
        for r in range(M2_HPG):
            e[g, M2_HPG * g + r, M2_HD * r:M2_HD * (r + 1)] = 1.0
            ecol[g, r, M2_HPG * g + r, :] = 1.0
    del rows
    return jnp.asarray(e), jnp.asarray(ecol)


S5_NS = S5_GROUPS * S5_STATE // S5_BLOCKS
S5_ROWS = 256
GELU_C = math.sqrt(2.0 / math.pi)


def _gelu(x):
    return 0.5 * x * (1.0 + jnp.tanh(GELU_C * (x + 0.044715 * x * x * x)))


def _gelu_grad(x):
    t = jnp.tanh(GELU_C * (x + 0.044715 * x * x * x))
    return 0.5 * (1.0 + t) + 0.5 * x * (1.0 - t * t) * GELU_C * (1.0 + 3.0 * 0.044715 * x * x)


def _s5_scan(re_ref, im_ref, pw_re, pw_im, nrows, reverse):
    n = re_ref.shape[1]
    row = _iota2((SUBLANES, n), 0)
    steps = [(d, pw_re[d - 1:d, :], pw_im[d - 1:d, :]) for d in (1, 2, 4)]
    if reverse:
        cw_re = jnp.concatenate([pw_re[SUBLANES - 1 - k:SUBLANES - k, :] for k in range(SUBLANES)], axis=0)
        cw_im = jnp.concatenate([pw_im[SUBLANES - 1 - k:SUBLANES - k, :] for k in range(SUBLANES)], axis=0)
    else:
        cw_re, cw_im = pw_re, pw_im
    edge = 0 if reverse else SUBLANES - 1
    ngroups = nrows // SUBLANES

    def step(i, carry):
        cr, ci = carry
        gi = (ngroups - 1 - i) if reverse else i
        r0 = pl.multiple_of(gi * SUBLANES, SUBLANES)
        xr, xi = re_ref[pl.ds(r0, SUBLANES), :], im_ref[pl.ds(r0, SUBLANES), :]
        for d, pr, pi in steps:
            if reverse:
                sr = jnp.where(row < SUBLANES - d, pltpu.roll(xr, SUBLANES - d, axis=0), 0.0)
                si = jnp.where(row < SUBLANES - d, pltpu.roll(xi, SUBLANES - d, axis=0), 0.0)
            else:
                sr = jnp.where(row >= d, pltpu.roll(xr, d, axis=0), 0.0)
                si = jnp.where(row >= d, pltpu.roll(xi, d, axis=0), 0.0)
            xr, xi = xr + (pr * sr - pi * si), xi + (pr * si + pi * sr)
        xr, xi = xr + (cw_re * cr - cw_im * ci), xi + (cw_re * ci + cw_im * cr)
        re_ref[pl.ds(r0, SUBLANES), :] = xr
        im_ref[pl.ds(r0, SUBLANES), :] = xi
        return (jnp.sum(jnp.where(row == edge, xr, 0.0), axis=0, keepdims=True),
                jnp.sum(jnp.where(row == edge, xi, 0.0), axis=0, keepdims=True))

    z = jnp.zeros((1, n), F32)
    lax.fori_loop(0, ngroups, step, (z, z))


def _s5_project_in(u_ref, bm_ref, re_ref, im_ref, L):
    def step(i, carry):
        r0 = pl.multiple_of(i * S5_ROWS, S5_ROWS)
        bu = _dot(u_ref[pl.ds(r0, S5_ROWS), :], bm_ref[...])
        re_ref[pl.ds(r0, S5_ROWS), :] = bu[:, :S5_NS]
        im_ref[pl.ds(r0, S5_ROWS), :] = bu[:, S5_NS:]
        return carry

    lax.fori_loop(0, L // S5_ROWS, step, 0)


def _s5_specs(L):
    col = pl.BlockSpec((L, LANES), lambda j: (0, j))
    bm = pl.BlockSpec((None, LANES, 2 * S5_NS), lambda j: (j, 0, 0))
    cm = pl.BlockSpec((None, 2 * S5_NS, LANES), lambda j: (j, 0, 0))
    pw = pl.BlockSpec((None, SUBLANES, S5_NS), lambda j: (j, 0, 0))
    vec = pl.BlockSpec((1, LANES), lambda j: (0, j))
    return col, bm, cm, pw, vec


def _s5_fwd(u, bmat, cmat, pw_re, pw_im, dsk):
    L = u.shape[0]
    col, bm, cm, pw, vec = _s5_specs(L)

    def body(u_ref, bm_ref, cm_ref, pr_ref, pi_ref, d_ref, y_ref, re_scr, im_scr):
        _s5_project_in(u_ref, bm_ref, re_scr, im_scr, L)
        _s5_scan(re_scr, im_scr, pr_ref[...], pi_ref[...], L, False)

        def step(i, carry):
            r0 = pl.multiple_of(i * S5_ROWS, S5_ROWS)
            rows = pl.ds(r0, S5_ROWS)
            y = _dot(re_scr[rows, :], cm_ref[:S5_NS, :]) + _dot(im_scr[rows, :], cm_ref[S5_NS:, :]) + d_ref[...] * u_ref[rows, :]
            y_ref[rows, :] = _gelu(y)
            return carry

        lax.fori_loop(0, L // S5_ROWS, step, 0)

    return pl.pallas_call(
        body, name="s5_fwd", grid=(S5_BLOCKS,),
        in_specs=[col, bm, cm, pw, pw, vec], out_specs=col,
        out_shape=jax.ShapeDtypeStruct((L, D_MODEL), F32),
        scratch_shapes=[pltpu.VMEM((L, S5_NS), F32)] * 2,
        compiler_params=_params("parallel"),
    )(u, bmat, cmat, pw_re, pw_im, dsk)


def _s5_bwd(u, bmat, cmat, pw_re, pw_im, dsk, dyg):
    L = u.shape[0]
    col, bm, cm, pw, vec = _s5_specs(L)

    def body(u_ref, bm_ref, cm_ref, pr_ref, pi_ref, d_ref, dy_ref, du_ref, dbm_ref, dcm_ref, dlam_ref, dd_ref,
             re_scr, im_scr, gr_scr, gi_scr, dyp_scr):
        _s5_project_in(u_ref, bm_ref, re_scr, im_scr, L)
        _s5_scan(re_scr, im_scr, pr_ref[...], pi_ref[...], L, False)

        def step(i, carry):
            dcr, dci, dd = carry
            r0 = pl.multiple_of(i * S5_ROWS, S5_ROWS)
            rows = pl.ds(r0, S5_ROWS)
            sr, si, uu = re_scr[rows, :], im_scr[rows, :], u_ref[rows, :]
            y = _dot(sr, cm_ref[:S5_NS, :]) + _dot(si, cm_ref[S5_NS:, :]) + d_ref[...] * uu
            dyp = dy_ref[rows, :] * _gelu_grad(y)
            dyp_scr[rows, :] = dyp
            gr_scr[rows, :] = _nt(dyp, cm_ref[:S5_NS, :])
            gi_scr[rows, :] = _nt(dyp, cm_ref[S5_NS:, :])
            return dcr + _tn(sr, dyp), dci + _tn(si, dyp), dd + jnp.sum(dyp * uu, axis=0, keepdims=True)

        zc = jnp.zeros((S5_NS, LANES), F32)
        dcr, dci, dd = lax.fori_loop(0, L // S5_ROWS, step, (zc, zc, jnp.zeros((1, LANES), F32)))
        dcm_ref[:S5_NS, :] = dcr
        dcm_ref[S5_NS:, :] = dci
        dd_ref[...] = dd

        _s5_scan(gr_scr, gi_scr, pr_ref[...], -pi_ref[...], L, True)

        row = _iota2((SUBLANES, S5_NS), 0)

        def lam_step(i, carry):
            ar, ai, pr, pi = carry
            r0 = pl.multiple_of(i * SUBLANES, SUBLANES)
            rows = pl.ds(r0, SUBLANES)
            sr, si = re_scr[rows, :], im_scr[rows, :]
            spr = jnp.where(row >= 1, pltpu.roll(sr, 1, axis=0), pr)
            spi = jnp.where(row >= 1, pltpu.roll(si, 1, axis=0), pi)
            gr, gi = gr_scr[rows, :], gi_scr[rows, :]
            ar = ar + jnp.sum(spr * gr + spi * gi, axis=0, keepdims=True)
            ai = ai + jnp.sum(spr * gi - spi * gr, axis=0, keepdims=True)
            last = row == SUBLANES - 1
            return (ar, ai, jnp.sum(jnp.where(last, sr, 0.0), axis=0, keepdims=True),
                    jnp.sum(jnp.where(last, si, 0.0), axis=0, keepdims=True))

        z = jnp.zeros((1, S5_NS), F32)
        ar, ai, _, _ = lax.fori_loop(0, L // SUBLANES, lam_step, (z, z, z, z))
        dlam_ref[...] = jnp.concatenate([ar, ai, jnp.zeros((SUBLANES - 2, S5_NS), F32)], axis=0)

        def in_step(i, carry):
            dbr, dbi = carry
            r0 = pl.multiple_of(i * S5_ROWS, S5_ROWS)
            rows = pl.ds(r0, S5_ROWS)
            gr, gi, uu = gr_scr[rows, :], gi_scr[rows, :], u_ref[rows, :]
            du_ref[rows, :] = dyp_scr[rows, :] * d_ref[...] + _nt(gr, bm_ref[:, :S5_NS]) + _nt(gi, bm_ref[:, S5_NS:])
            return dbr + _tn(uu, gr), dbi + _tn(uu, gi)

        zb = jnp.zeros((LANES, S5_NS), F32)
        dbr, dbi = lax.fori_loop(0, L // S5_ROWS, in_step, (zb, zb))
        dbm_ref[:, :S5_NS] = dbr
        dbm_ref[:, S5_NS:] = dbi

    return pl.pallas_call(
        body, name="s5_bwd", grid=(S5_BLOCKS,),
        in_specs=[col, bm, cm, pw, pw, vec, col], out_specs=[col, bm, cm, pw, vec],
        out_shape=[jax.ShapeDtypeStruct((L, D_MODEL), F32), jax.ShapeDtypeStruct((S5_BLOCKS, LANES, 2 * S5_NS), F32),
                   jax.ShapeDtypeStruct((S5_BLOCKS, 2 * S5_NS, LANES), F32),
                   jax.ShapeDtypeStruct((S5_BLOCKS, SUBLANES, S5_NS), F32), jax.ShapeDtypeStruct((1, D_MODEL), F32)],
        scratch_shapes=[pltpu.VMEM((L, S5_NS), F32)] * 4 + [pltpu.VMEM((L, LANES), F32)],
        compiler_params=_params("parallel"),
    )(u, bmat, cmat, pw_re, pw_im, dsk, dyg)


def _s5_discretize(lam_re, lam_im, log_dt, b_re, b_im, e16):
    dt = jnp.exp(log_dt)
    zr, zi = lam_re * dt, lam_im * dt
    mag = jnp.exp(zr)
    lbr, lbi = mag * jnp.cos(zi), mag * jnp.sin(zi)
    den = lam_re * lam_re + lam_im * lam_im
    nr, ni = lbr - 1.0, lbi
    cr = (nr * lam_re + ni * lam_im) / den
    ci = (ni * lam_re - nr * lam_im) / den
    crw, ciw = _pick(cr, e16), _pick(ci, e16)
    return lbr, lbi, crw * b_re - ciw * b_im, crw * b_im + ciw * b_re


def _s5_params_fwd(lam_re, lam_im, log_dt, b_re, b_im, e16):
    def body(lr, li, ld, br, bi, e, o1, o2, o3, o4):
        for o, val in zip((o1, o2, o3, o4), _s5_discretize(lr[...], li[...], ld[...], br[...], bi[...], e[...])):
            o[...] = val

    g, p, n = S5_GROUPS, S5_STATE, S5_STATE * S5_GROUP
    return pl.pallas_call(
        body, name="s5_params_fwd",
        out_shape=[jax.ShapeDtypeStruct((g, p), F32)] * 2 + [jax.ShapeDtypeStruct((g, n), F32)] * 2,
        compiler_params=_params(),
    )(lam_re, lam_im, log_dt, b_re, b_im, e16)


def _s5_params_bwd(lam_re, lam_im, log_dt, b_re, b_im, e16, cts):
    def body(lr, li, ld, br, bi, e, c1, c2, c3, c4, o1, o2, o3, o4, o5):
        e_m = e[...]
        f = lambda a, b, c, d, g: _s5_discretize(a, b, c, d, g, e_m)
        _, vjp = jax.vjp(f, lr[...], li[...], ld[...], br[...], bi[...])
        for o, val in zip((o1, o2, o3, o4, o5), vjp((c1[...], c2[...], c3[...], c4[...]))):
            o[...] = val

    g, p, n = S5_GROUPS, S5_STATE, S5_STATE * S5_GROUP
    return pl.pallas_call(
        body, name="s5_params_bwd",
        out_shape=[jax.ShapeDtypeStruct((g, p), F32)] * 2 + [jax.ShapeDtypeStruct((g, 1), F32)]
        + [jax.ShapeDtypeStruct((g, n), F32)] * 2,
        compiler_params=_params(),
    )(lam_re, lam_im, log_dt, b_re, b_im, e16, *cts)


def _add_residual(acc, h):
    return (acc + h,)


def _mlp_fwd(i, h, g, w1, w2):
    hn = _rms_fwd(f"mlp{i}_norm", h, g)
    r, a = _mm(f"mlp{i}_up", hn, w1, "nn", (BF16, BF16), epi=lambda acc: (jnp.square(jnp.maximum(acc, 0.0)), acc))
    return _mm(f"mlp{i}_down", r, w2, "nn", (F32,), epi=_add_residual, extras=(h,)), (h, hn, r, a)


def _mlp_bwd(i, dh_out, saved, g, w1, w2):
    h, hn, r, a = saved
    dw2 = _mm(f"mlp{i}_dw2", r, dh_out, "tn", (F32,))
    da = _mm(f"mlp{i}_da", dh_out, w2, "nt", (BF16,), epi=lambda acc, aa: (acc * (2.0 * jnp.maximum(aa.astype(F32), 0.0)),),
             extras=(a,))
    dw1 = _mm(f"mlp{i}_dw1", hn, da, "tn", (F32,))
    dhn = _mm(f"mlp{i}_dhn", da, w1, "nt", (F32,))
    dh, dg = _rms_bwd(f"mlp{i}_dnorm", h, g, dhn, dh_out)
    return dh, dg[0], dw1, dw2


def _lanes(v, n):
    return jnp.broadcast_to(v.reshape(n, 1, 1), (n, 1, LANES))


def _gdn_fwd_layer(i, h, g, w_in, conv_w, a_log, dt_bias, o_g, w_out):
    hn = _rms_fwd(f"gdn{i}_norm", h, g)
    p = _mm(f"gdn{i}_in", hn, w_in, "nn", (F32,))
    zb = jnp.zeros((1, D_MODEL), F32)
    qkv = [_conv_fwd(f"gdn{i}_conv{t}", p, t * D_MODEL, conv_w[:, t * D_MODEL:(t + 1) * D_MODEL], zb) for t in range(3)]
    ea, eb = _gdn_selectors()
    y, sprev = _gdn_fwd(*qkv, p, _lanes(a_log, GDN_HEADS), _lanes(dt_bias, GDN_HEADS), o_g.reshape(1, LANES), ea, eb)
    return _mm(f"gdn{i}_out", y, w_out, "nn", (F32,), epi=_add_residual, extras=(h,)), (h, hn, p, qkv, y, sprev)


def _gdn_bwd_layer(i, dh_out, saved, g, w_in, conv_w, a_log, dt_bias, o_g, w_out):
    h, hn, p, qkv, y, sprev = saved
    dy = _mm(f"gdn{i}_dy", dh_out, w_out, "nt", (F32,))
    dw_out = _mm(f"gdn{i}_dwout", y, dh_out, "tn", (F32,))
    ea, eb = _gdn_selectors()
    dq, dk, dv, dgate, dab, dpar = _gdn_bwd(*qkv, p, _lanes(a_log, GDN_HEADS), _lanes(dt_bias, GDN_HEADS),
                                            o_g.reshape(1, LANES), ea, eb, sprev, dy)
    zb = jnp.zeros((1, D_MODEL), F32)
    dpre, dcw = [], []
    for t, d in enumerate((dq, dk, dv)):
        dx, dw, _ = _conv_bwd(f"gdn{i}_dconv{t}", p, t * D_MODEL, conv_w[:, t * D_MODEL:(t + 1) * D_MODEL], zb, d)
        dpre.append(dx)
        dcw.append(dw)
    dp = jnp.concatenate(dpre + [dgate, dab], axis=1)
    dw_in = _mm(f"gdn{i}_dwin", hn, dp, "tn", (F32,))[:, :GDN_IN]
    dhn = _mm(f"gdn{i}_dhn", dp, w_in, "nt", (F32,))
    dh, dg = _rms_bwd(f"gdn{i}_dnorm", h, g, dhn, dh_out)
    grads = dict(w_in=dw_in, conv_w=jnp.concatenate(dcw, axis=1), a_log=jnp.sum(dpar[:, 0, :], axis=-1),
                 dt_bias=jnp.sum(dpar[:, 1, :], axis=-1), o_norm_g=jnp.sum(dpar[:, 2, :], axis=0), w_out=dw_out)
    return dh, dg[0], grads


def _m2_vectors(dt_bias, a_log, d_skip, norm_g):
    pad = lambda v: jnp.pad(v, (0, LANES - M2_HEADS)).reshape(1, LANES)
    return pad(dt_bias), pad(a_log), jnp.repeat(d_skip, M2_HD).reshape(1, M2_INNER), norm_g.reshape(1, M2_INNER)


def _m2_fwd_layer(h, g, w_in, conv_w, conv_b, dt_bias, a_log, d_skip, norm_g, w_out):
    hn = _rms_fwd("m2_norm", h, g)
    p = _mm("m2_in", hn, w_in, "nn", (F32,))
    xbc = _conv_fwd("m2_conv", p, M2_INNER, conv_w, conv_b.reshape(1, M2_CONV_CH))
    e, ecol = _m2_selectors()
    y, sprev = _m2_fwd(xbc, p, *_m2_vectors(dt_bias, a_log, d_skip, norm_g), e, ecol)
    return _mm("m2_out", y, w_out, "nn", (F32,), epi=_add_residual, extras=(h,)), (h, hn, p, xbc, y, sprev)


def _m2_bwd_layer(dh_out, saved, g, w_in, conv_w, conv_b, dt_bias, a_log, d_skip, norm_g, w_out):
    h, hn, p, xbc, y, sprev = saved
    dy = _mm("m2_dy", dh_out, w_out, "nt", (F32,))
    dw_out = _mm("m2_dwout", y, dh_out, "tn", (F32,))
    e, ecol = _m2_selectors()
    dx, db, dc, dz, ddt, dnar, dwide = _m2_bwd(xbc, p, *_m2_vectors(dt_bias, a_log, d_skip, norm_g), e, ecol, sprev, dy)
    dxbc, dcw, dcb = _conv_bwd("m2_dconv", p, M2_INNER, conv_w, conv_b.reshape(1, M2_CONV_CH),
                               jnp.concatenate([dx, db, dc], axis=1))
    dp = jnp.concatenate([dz, dxbc, ddt], axis=1)
    dw_in = _mm("m2_dwin", hn, dp, "tn", (F32,))[:, :M2_IN]
    dhn = _mm("m2_dhn", dp, w_in, "nt", (F32,))
    dh, dg = _rms_bwd("m2_dnorm", h, g, dhn, dh_out)
    grads = dict(w_in=dw_in, conv_w=dcw, conv_b=dcb[0], dt_bias=dnar[0, :M2_HEADS], a_log=dnar[1, :M2_HEADS],
                 d=jnp.sum(dwide[:, 0, :].reshape(M2_HEADS, M2_HD), axis=-1), norm_g=dwide[:, 1, :].reshape(M2_INNER),
                 w_out=dw_out)
    return dh, dg[0], grads


def _s5_selector():
    e16 = np.zeros((S5_STATE, S5_STATE * S5_GROUP), np.float32)
    for p in range(S5_STATE):
        e16[p, p * S5_GROUP:(p + 1) * S5_GROUP] = 1.0
    return jnp.asarray(e16)


def _s5_operands(lbr, lbi, bbr, bbi, c_re, c_im):
    eye = jnp.eye(S5_BLOCKS, dtype=F32)
    gpb = S5_GROUPS // S5_BLOCKS
    bd = lambda t: jnp.einsum("jgpk,gh->jgkhp", t.reshape(S5_BLOCKS, gpb, S5_STATE, S5_GROUP), eye).reshape(S5_BLOCKS, LANES, S5_NS)
    cd = lambda t: jnp.einsum("jgkp,gh->jgphk", t.reshape(S5_BLOCKS, gpb, S5_GROUP, S5_STATE), eye).reshape(S5_BLOCKS, S5_NS, LANES)
    bmat = jnp.concatenate([bd(bbr), bd(bbi)], axis=2).astype(BF16)
    cmat = jnp.concatenate([cd(c_re), -cd(c_im)], axis=1).astype(BF16)
    ar, ai = lbr.reshape(S5_BLOCKS, S5_NS), lbi.reshape(S5_BLOCKS, S5_NS)
    pr, pi = [ar], [ai]
    for _ in range(SUBLANES - 1):
        pr, pi = pr + [pr[-1] * ar - pi[-1] * ai], pi + [pr[-1] * ai + pi[-1] * ar]
    return bmat, cmat, jnp.stack(pr, axis=1), jnp.stack(pi, axis=1)


def _s5_fwd_layer(h, g, w_in, lam_re, lam_im, log_dt, b_re, b_im, c_re, c_im, d_skip, w_out):
    hn = _rms_fwd("s5_norm", h, g)
    u = _mm("s5_in", hn, w_in, "nn", (F32,))
    n = S5_STATE * S5_GROUP
    lbr, lbi, bbr, bbi = _s5_params_fwd(lam_re, lam_im, log_dt.reshape(S5_GROUPS, 1), b_re.reshape(S5_GROUPS, n),
                                        b_im.reshape(S5_GROUPS, n), _s5_selector())
    ops = _s5_operands(lbr, lbi, bbr, bbi, c_re, c_im)
    yg = _s5_fwd(u, *ops, d_skip.reshape(1, D_MODEL))
    ag = _mm("s5_out", yg, w_out, "nn", (F32,))
    return _glu_fwd(h, ag), (h, hn, u, ops, yg, ag)


def _s5_bwd_layer(dh_out, saved, g, w_in, lam_re, lam_im, log_dt, b_re, b_im, c_re, c_im, d_skip, w_out):
    h, hn, u, ops, yg, ag = saved
    dag = _glu_bwd(dh_out, ag)
    dw_out = _mm("s5_dwout", yg, dag, "tn", (F32,))
    dyg = _mm("s5_dyg", dag, w_out, "nt", (F32,))
    du, dbmat, dcmat, dlam, ddsk = _s5_bwd(u, *ops, d_skip.reshape(1, D_MODEL), dyg)
    eye = jnp.eye(S5_BLOCKS, dtype=F32)
    gpb = S5_GROUPS // S5_BLOCKS
    n = S5_STATE * S5_GROUP
    ub = lambda t: jnp.einsum("jgkhp,gh->jgpk", t.reshape(S5_BLOCKS, gpb, S5_GROUP, gpb, S5_STATE), eye).reshape(S5_GROUPS, n)
    uc = lambda t: jnp.einsum("jgphk,gh->jgkp", t.reshape(S5_BLOCKS, gpb, S5_STATE, gpb, S5_GROUP), eye).reshape(c_re.shape)
    cts = (dlam[:, 0, :].reshape(S5_GROUPS, S5_STATE), dlam[:, 1, :].reshape(S5_GROUPS, S5_STATE),
           ub(dbmat[:, :, :S5_NS]), ub(dbmat[:, :, S5_NS:]))
    dlr, dli, dld, dbr, dbi = _s5_params_bwd(lam_re, lam_im, log_dt.reshape(S5_GROUPS, 1), b_re.reshape(S5_GROUPS, n),
                                             b_im.reshape(S5_GROUPS, n), _s5_selector(), cts)
    dw_in = _mm("s5_dwin", hn, du, "tn", (F32,))
    dhn = _mm("s5_dhn", du, w_in, "nt", (F32,))
    dh, dg = _rms_bwd("s5_dnorm", h, g, dhn, dh_out)
    grads = dict(w_in=dw_in, lam_re=dlr, lam_im=dli, log_dt=dld[:, 0], b_re=dbr.reshape(b_re.shape), b_im=dbi.reshape(b_im.shape),
                 c_re=uc(dcmat[:, :S5_NS, :]), c_im=-uc(dcmat[:, S5_NS:, :]), d=ddsk[0], w_out=dw_out)
    return dh, dg[0], grads


MIXER_OF_LAYER = ("gdn", "s5", "m2", "gdn")
MIXER_INDEX = (0, 0, 0, 1)


def _mixer_args(W, i):
    kind, j = MIXER_OF_LAYER[i], MIXER_INDEX[i]
    if kind == "gdn":
        return tuple(W["gdn_" + k][j] for k in ("w_in", "conv_w", "a_log", "dt_bias", "o_norm_g", "w_out"))
    if kind == "s5":
        return tuple(W["s5_" + k][j] for k in ("w_in", "lam_re", "lam_im", "log_dt", "b_re", "b_im", "c_re", "c_im", "d", "w_out"))
    return tuple(W["m2_" + k][j] for k in ("w_in", "conv_w", "conv_b", "dt_bias", "a_log", "d", "norm_g", "w_out"))


def _local_step(x, target, W):
    h = x
    saved = []
    for i in range(DEPTH):
        kind = MIXER_OF_LAYER[i]
        args = _mixer_args(W, i)
        if kind == "gdn":
            h, sm = _gdn_fwd_layer(i, h, W["norm_mix_g"][i], *args)
        elif kind == "s5":
            h, sm = _s5_fwd_layer(h, W["norm_mix_g"][i], *args)
        else:
            h, sm = _m2_fwd_layer(h, W["norm_mix_g"][i], *args)
        h, sp = _mlp_fwd(i, h, W["norm_mlp_g"][i], W["mlp_w1"][i], W["mlp_w2"][i])
        saved.append((sm, sp))
    loss, dh, dgf = _loss_head(h, W["final_norm_g"], target)
    G = {"final_norm_g": dgf[0], "norm_mix_g": [None] * DEPTH, "norm_mlp_g": [None] * DEPTH,
         "mlp_w1": [None] * DEPTH, "mlp_w2": [None] * DEPTH}
    mix = {}
    for i in reversed(range(DEPTH)):
        kind = MIXER_OF_LAYER[i]
        sm, sp = saved[i]
        dh, G["norm_mlp_g"][i], G["mlp_w1"][i], G["mlp_w2"][i] = _mlp_bwd(i, dh, sp, W["norm_mlp_g"][i], W["mlp_w1"][i], W["mlp_w2"][i])
        args = _mixer_args(W, i)
        if kind == "gdn":
            dh, G["norm_mix_g"][i], gm = _gdn_bwd_layer(i, dh, sm, W["norm_mix_g"][i], *args)
        elif kind == "s5":
            dh, G["norm_mix_g"][i], gm = _s5_bwd_layer(dh, sm, W["norm_mix_g"][i], *args)
        else:
            dh, G["norm_mix_g"][i], gm = _m2_bwd_layer(dh, sm, W["norm_mix_g"][i], *args)
        for k, v in gm.items():
            mix.setdefault(kind + "_" + k, {})[MIXER_INDEX[i]] = v
    for k, d in mix.items():
        G[k] = [d[j] for j in sorted(d)]
    return loss, dh, {k: (jnp.stack(v) if isinstance(v, list) else v) for k, v in G.items()}


ADAM_ROWS = 128


def _adamw(name, w, g, m, v):
    R, C = w.shape
    tr = _tile(R, (ADAM_ROWS, SUBLANES))

    def body(w_ref, g_ref, m_ref, v_ref, d_ref, mo_ref, vo_ref):
        gg = g_ref[...]
        mn = ADAM_B1 * m_ref[...] + (1.0 - ADAM_B1) * gg
        vn = ADAM_B2 * v_ref[...] + (1.0 - ADAM_B2) * (gg * gg)
        m_hat = mn / (1.0 - ADAM_B1 ** ADAM_STEP)
        v_hat = vn / (1.0 - ADAM_B2 ** ADAM_STEP)
        d_ref[...] = -ADAM_LR * (m_hat / (jnp.sqrt(v_hat) + ADAM_EPS) + ADAM_WD * w_ref[...])
        mo_ref[...] = mn
        vo_ref[...] = vn

    blk = pl.BlockSpec((tr, C), lambda i: (i, 0))
    return pl.pallas_call(
        body, name=name, grid=(R // tr,), in_specs=[blk] * 4, out_specs=[blk] * 3,
        out_shape=[jax.ShapeDtypeStruct((R, C), F32)] * 3, compiler_params=_params("parallel"),
    )(w, g, m, v)


MESH = pl.DeviceIdType.MESH
ANY = pl.BlockSpec(memory_space=pl.ANY)
N_CHIPS = 4
N_DEV = 8


def _position():
    return lax.axis_index("x"), lax.axis_index("y"), lax.axis_index("c")


def _gather_shards(wp):
    R, C = wp.shape
    half = R // 2

    def body(w_ref, out_ref, send_sems, recv_sems):
        x, y, c = _position()
        sibling = (x, y, 1 - c)
        chips = [(1 - x, y), (x, 1 - y), (1 - x, 1 - y)]

        def piece(cx, cy, hc):
            return out_ref.at[2 * cx + cy, pl.ds(hc * half, half), :]

        def copy(k, src, dst, to):
            return pltpu.make_async_remote_copy(src_ref=src, dst_ref=dst, send_sem=send_sems.at[k], recv_sem=recv_sems.at[k],
                                                device_id=to, device_id_type=MESH)

        first = [copy(j, w_ref.at[pl.ds(c * half, half), :], piece(x, y, c), (*chip, c)) for j, chip in enumerate(chips)]
        for cp in first:
            cp.start()
        passed = [copy(3 + j, piece(*chip, c), piece(*chip, c), sibling) for j, chip in enumerate(chips)]
        for j, chip in enumerate(chips):
            copy(j, piece(*chip, c), piece(*chip, c), sibling).wait_recv()
            passed[j].start()
        for j, chip in enumerate(chips):
            copy(3 + j, piece(*chip, 1 - c), piece(*chip, 1 - c), sibling).wait_recv()
        for cp in first + passed:
            cp.wait_send()

    return pl.pallas_call(
        body, name="gather_shards", in_specs=[ANY], out_specs=ANY,
        out_shape=jax.ShapeDtypeStruct((N_CHIPS, R, C), wp.dtype),
        scratch_shapes=[pltpu.SemaphoreType.DMA((6,)), pltpu.SemaphoreType.DMA((6,))],
    )(wp)


def _pair_exchange(gp):
    n, R, C = gp.shape
    half = R // 2

    def body(g_ref, out_ref, send_sems, recv_sems):
        x, y, c = _position()
        copies = [pltpu.make_async_remote_copy(
            src_ref=g_ref.at[k, pl.ds((1 - c) * half, half), :], dst_ref=out_ref.at[k], send_sem=send_sems.at[k],
            recv_sem=recv_sems.at[k], device_id=(x, y, 1 - c), device_id_type=MESH) for k in range(n)]
        for cp in copies:
            cp.start()
        for cp in copies:
            cp.wait()

    return pl.pallas_call(
        body, name="pair_exchange", in_specs=[ANY], out_specs=ANY,
        out_shape=jax.ShapeDtypeStruct((n, half, C), gp.dtype),
        scratch_shapes=[pltpu.SemaphoreType.DMA((n,)), pltpu.SemaphoreType.DMA((n,))],
    )(gp)


SUM_ROWS = 256


def _pair_sum(gp, got, core):
    n, R, C = gp.shape
    half = R // 2
    nb = half // SUM_ROWS

    def body(core_ref, g_ref, r_ref, o_ref):
        o_ref[...] = (g_ref[...].astype(F32) + r_ref[...].astype(F32)).astype(o_ref.dtype)

    return pl.pallas_call(
        body, name="pair_sum",
        grid_spec=pltpu.PrefetchScalarGridSpec(
            num_scalar_prefetch=1, grid=(n, nb),
            in_specs=[pl.BlockSpec((None, SUM_ROWS, C), lambda k, i, core_ref: (k, core_ref[0] * nb + i, 0)),
                      pl.BlockSpec((None, SUM_ROWS, C), lambda k, i, core_ref: (k, i, 0))],
            out_specs=pl.BlockSpec((None, SUM_ROWS, C), lambda k, i, core_ref: (k, i, 0))),
        out_shape=jax.ShapeDtypeStruct((n, half, C), gp.dtype), compiler_params=_params("parallel", "parallel"),
    )(core, gp, got)


def _chip_exchange(t):
    n, H, C = t.shape

    def body(t_ref, out_ref, send_sems, recv_sems):
        x, y, c = _position()
        chips = [(1 - x, y), (x, 1 - y), (1 - x, 1 - y)]
        copies = [pltpu.make_async_remote_copy(
            src_ref=t_ref.at[2 * cx + cy], dst_ref=out_ref.at[2 * x + y], send_sem=send_sems.at[j], recv_sem=recv_sems.at[j],
            device_id=(cx, cy, c), device_id_type=MESH) for j, (cx, cy) in enumerate(chips)]
        for cp in copies:
            cp.start()
        for j, (cx, cy) in enumerate(chips):
            pltpu.make_async_remote_copy(
                src_ref=t_ref.at[2 * cx + cy], dst_ref=out_ref.at[2 * cx + cy], send_sem=send_sems.at[j],
                recv_sem=recv_sems.at[j], device_id=(cx, cy, c), device_id_type=MESH).wait_recv()
        for cp in copies:
            cp.wait_send()

    return pl.pallas_call(
        body, name="chip_exchange", in_specs=[ANY], out_specs=ANY,
        out_shape=jax.ShapeDtypeStruct((n, H, C), t.dtype),
        scratch_shapes=[pltpu.SemaphoreType.DMA((n - 1,)), pltpu.SemaphoreType.DMA((n - 1,))],
    )(t)


def _chip_sum(t, got, ids):
    n, H, C = t.shape
    nb = H // SUM_ROWS

    def body(ids_ref, t_ref, r_ref, o_ref):
        own = t_ref[...].astype(F32)
        acc = jnp.where(ids_ref[0] == 0, own, r_ref[0].astype(F32))
        for k in range(1, n):
            acc = acc + jnp.where(ids_ref[0] == k, own, r_ref[k].astype(F32))
        o_ref[...] = acc

    return pl.pallas_call(
        body, name="chip_sum",
        grid_spec=pltpu.PrefetchScalarGridSpec(
            num_scalar_prefetch=1, grid=(nb,),
            in_specs=[pl.BlockSpec((None, SUM_ROWS, C), lambda i, ids_ref: (ids_ref[0], i, 0)),
                      pl.BlockSpec((n, SUM_ROWS, C), lambda i, ids_ref: (0, i, 0))],
            out_specs=pl.BlockSpec((SUM_ROWS, C), lambda i, ids_ref: (ids_ref[1] * nb + i, 0))),
        out_shape=jax.ShapeDtypeStruct((2 * H, C), F32), compiler_params=_params("parallel"),
    )(ids, t, got)


def _sum_pieces(name, pieces):
    n, R, C = pieces.shape
    tr = _tile(R, (256, 128, SUBLANES))

    def body(p_ref, o_ref):
        acc = p_ref[0].astype(F32)
        for s in range(1, n):
            acc = acc + p_ref[s].astype(F32)
        o_ref[...] = acc

    return pl.pallas_call(
        body, name=name, grid=(R // tr,),
        in_specs=[pl.BlockSpec((n, tr, C), lambda i: (0, i, 0))], out_specs=pl.BlockSpec((tr, C), lambda i: (i, 0)),
        out_shape=jax.ShapeDtypeStruct((R, C), F32), compiler_params=_params("parallel"),
    )(pieces)


def _swap_halves(s):
    R, C = s.shape
    half = R // 2

    def body(s_ref, out_ref, send_sem, recv_sem):
        x, y, c = _position()
        cp = pltpu.make_async_remote_copy(src_ref=s_ref.at[pl.ds(c * half, half), :], dst_ref=out_ref.at[pl.ds(c * half, half), :],
                                          send_sem=send_sem, recv_sem=recv_sem, device_id=(x, y, 1 - c), device_id_type=MESH)
        cp.start()
        pltpu.make_async_remote_copy(src_ref=s_ref.at[pl.ds(c * half, half), :], dst_ref=out_ref.at[pl.ds((1 - c) * half, half), :],
                                     send_sem=send_sem, recv_sem=recv_sem, device_id=(x, y, 1 - c), device_id_type=MESH).wait_recv()
        cp.wait_send()

    return pl.pallas_call(
        body, name="swap_halves", in_specs=[ANY], out_specs=ANY, input_output_aliases={0: 0},
        out_shape=jax.ShapeDtypeStruct((R, C), s.dtype),
        scratch_shapes=[pltpu.SemaphoreType.DMA, pltpu.SemaphoreType.DMA],
    )(s)


def _gather_small(name, blk):
    m_per, n = blk.shape

    def body(x_ref, out_ref, send_sems, recv_sems, local_sem):
        x, y, c = _position()
        me, sibling = (x, y, c), (x, y, 1 - c)
        chips = [(1 - x, y), (x, 1 - y), (1 - x, 1 - y)]

        def rows(px, py, pc):
            return out_ref.at[pl.ds((4 * px + 2 * py + pc) * m_per, m_per), :]

        def copy(k, block, to, src=None):
            return pltpu.make_async_remote_copy(src_ref=rows(*block) if src is None else src, dst_ref=rows(*block),
                                                send_sem=send_sems.at[k], recv_sem=recv_sems.at[k], device_id=to, device_id_type=MESH)

        mine = pltpu.make_async_copy(x_ref, rows(*me), local_sem)
        mine.start()
        first = [copy(0, me, sibling, src=x_ref)] + [copy(1 + j, me, (*chip, c), src=x_ref) for j, chip in enumerate(chips)]
        for cp in first:
            cp.start()
        passed = [copy(4 + j, (*chip, c), sibling) for j, chip in enumerate(chips)]
        for j, chip in enumerate(chips):
            copy(1 + j, (*chip, c), me).wait_recv()
            passed[j].start()
        copy(0, sibling, me).wait_recv()
        for j, chip in enumerate(chips):
            copy(4 + j, (*chip, 1 - c), me).wait_recv()
        for cp in first + passed:
            cp.wait_send()
        mine.wait()

    return pl.pallas_call(
        body, name=name, out_shape=jax.ShapeDtypeStruct((N_DEV * m_per, n), blk.dtype),
        in_specs=[pl.BlockSpec(memory_space=pltpu.VMEM)], out_specs=pl.BlockSpec(memory_space=pltpu.VMEM),
        scratch_shapes=[pltpu.SemaphoreType.DMA((7,)), pltpu.SemaphoreType.DMA((7,)), pltpu.SemaphoreType.DMA],
        compiler_params=pltpu.CompilerParams(vmem_limit_bytes=VMEM_LIMIT_BYTES),
    )(blk)


WEIGHTS = ("norm_mix_g", "norm_mlp_g", "mlp_w1", "mlp_w2", "gdn_w_in", "gdn_conv_w", "gdn_a_log", "gdn_dt_bias", "gdn_o_norm_g",
           "gdn_w_out", "s5_w_in", "s5_lam_re", "s5_lam_im", "s5_log_dt", "s5_b_re", "s5_b_im", "s5_c_re", "s5_c_im", "s5_d",
           "s5_w_out", "m2_w_in", "m2_conv_w", "m2_conv_b", "m2_dt_bias", "m2_a_log", "m2_d", "m2_norm_g", "m2_w_out",
           "final_norm_g")
BIG = {"mlp_w1": 2, "mlp_w2": 1, "gdn_w_in": 2, "gdn_w_out": 1, "s5_w_in": 1, "s5_w_out": 2, "m2_w_in": 2, "m2_w_out": 1}
SMALL_CUT = {"gdn_conv_w": 2, "m2_conv_w": 2, "m2_conv_b": 1, "m2_norm_g": 1}
PACK_COLS = 1024
PACK_ROW_MULTIPLE = 512


def _pack(arrays, cols, row_multiple, dtype):
    flat = jnp.concatenate([a.reshape(-1).astype(dtype) for a in arrays])
    n = -(-flat.shape[0] // (cols * row_multiple)) * cols * row_multiple
    return jnp.pad(flat, (0, n - flat.shape[0])).reshape(-1, cols)


def _unpack(packed, shapes):
    flat = packed.reshape(-1)
    out, off = [], 0
    for shp in shapes:
        n = math.prod(shp)
        out.append(flat[off:off + n].reshape(shp))
        off += n
    return out


def _cut(a, axis, k):
    n = a.shape[axis] // N_CHIPS
    return lax.slice_in_dim(a, k * n, (k + 1) * n, axis=axis)


def kernel(x, norm_mix_g, norm_mlp_g, mlp_w1, mlp_w2, gdn_w_in, gdn_conv_w, gdn_a_log, gdn_dt_bias, gdn_o_norm_g, gdn_w_out, s5_w_in, s5_lam_re, s5_lam_im, s5_log_dt, s5_b_re, s5_b_im, s5_c_re, s5_c_im, s5_d, s5_w_out, m2_w_in, m2_conv_w, m2_conv_b, m2_dt_bias, m2_a_log, m2_d, m2_norm_g, m2_w_out, final_norm_g, loss_target, m_norm_mix_g, m_norm_mlp_g, m_mlp_w1, m_mlp_w2, m_gdn_w_in, m_gdn_conv_w, m_gdn_a_log, m_gdn_dt_bias, m_gdn_o_norm_g, m_gdn_w_out, m_s5_w_in, m_s5_lam_re, m_s5_lam_im, m_s5_log_dt, m_s5_b_re, m_s5_b_im, m_s5_c_re, m_s5_c_im, m_s5_d, m_s5_w_out, m_m2_w_in, m_m2_conv_w, m_m2_conv_b, m_m2_dt_bias, m_m2_a_log, m_m2_d, m_m2_norm_g, m_m2_w_out, m_final_norm_g, v_norm_mix_g, v_norm_mlp_g, v_mlp_w1, v_mlp_w2, v_gdn_w_in, v_gdn_conv_w, v_gdn_a_log, v_gdn_dt_bias, v_gdn_o_norm_g, v_gdn_w_out, v_s5_w_in, v_s5_lam_re, v_s5_lam_im, v_s5_log_dt, v_s5_b_re, v_s5_b_im, v_s5_c_re, v_s5_c_im, v_s5_d, v_s5_w_out, v_m2_w_in, v_m2_conv_w, v_m2_conv_b, v_m2_dt_bias, v_m2_a_log, v_m2_d, v_m2_norm_g, v_m2_w_out, v_final_norm_g):
    given = dict(locals())
    w = {n: given[n] for n in WEIGHTS}
    mom = {n: given["m_" + n] for n in WEIGHTS}
    var = {n: given["v_" + n] for n in WEIGHTS}
    big, small_cut = tuple(BIG), tuple(SMALL_CUT)
    small = tuple(n for n in WEIGHTS if n not in BIG)
    chip = 2 * lax.axis_index("x") + lax.axis_index("y")

    shards = _gather_shards(_pack([w[n] for n in big], PACK_COLS, PACK_ROW_MULTIPLE, BF16))
    own = [w[n].astype(BF16) for n in big]
    per_chip = [[jnp.where(chip == k, o, p) for o, p in zip(own, _unpack(shards[k], [w[n].shape for n in big]))]
                for k in range(N_CHIPS)]
    W = {n: jnp.concatenate([per_chip[k][i] for k in range(N_CHIPS)], axis=BIG[n]) for i, n in enumerate(big)}
    W["gdn_w_in"] = jnp.pad(W["gdn_w_in"], ((0, 0), (0, 0), (0, GDN_IN_PAD - GDN_IN)))
    W["m2_w_in"] = jnp.pad(W["m2_w_in"], ((0, 0), (0, 0), (0, M2_IN_PAD - M2_IN)))
    cut_blk = _pack([w[n] for n in small_cut], LANES, SUBLANES, F32)
    cut_all = _gather_small("gather_small_params", cut_blk).reshape(N_DEV, *cut_blk.shape)
    per_chip = [_unpack(cut_all[2 * k], [w[n].shape for n in small_cut]) for k in range(N_CHIPS)]
    W.update({n: jnp.concatenate([per_chip[k][i] for k in range(N_CHIPS)], axis=SMALL_CUT[n]) for i, n in enumerate(small_cut)})
    W.update({n: w[n] for n in small if n not in SMALL_CUT})

    loss, grad_x, G = _local_step(x[0], loss_target[0], W)
    loss = lax.psum(loss[0, 0], ("x", "y", "c"))

    gp = jnp.stack([_pack([_cut(G[n], BIG[n], k) for n in big], PACK_COLS, PACK_ROW_MULTIPLE, BF16) for k in range(N_CHIPS)])
    core = lax.axis_index("c").astype(jnp.int32)
    pair = _pair_sum(gp, _pair_exchange(gp), core.reshape(1))
    g_shard = _swap_halves(_chip_sum(pair, _chip_exchange(pair), jnp.stack([chip.astype(jnp.int32), core])))
    grads = dict(zip(big, _unpack(g_shard, [w[n].shape for n in big])))
    sg = _pack([G[n] for n in small], LANES, ADAM_ROWS, F32)
    sg_sum = _sum_pieces("sum_small_grads", _gather_small("gather_small_grads", sg).reshape(N_DEV, *sg.shape))
    for n, g in zip(small, _unpack(sg_sum, [G[n].shape for n in small])):
        if n in SMALL_CUT:
            width = g.shape[SMALL_CUT[n]] // N_CHIPS
            g = lax.dynamic_slice_in_dim(g, chip * width, width, axis=SMALL_CUT[n])
        grads[n] = g.reshape(w[n].shape)

    delta, new_m, new_v = {}, {}, {}
    for n in big:
        as2d = lambda a: a.reshape(-1, a.shape[-1])
        outs = _adamw("adamw_" + n, as2d(w[n]), as2d(grads[n]), as2d(mom[n]), as2d(var[n]))
        delta[n], new_m[n], new_v[n] = (o.reshape(w[n].shape) for o in outs)
    packs = [_pack([t[n] for n in small], LANES, ADAM_ROWS, F32) for t in (w, grads, mom, var)]
    outs = _adamw("adamw_small", *packs)
    for t, o in zip((delta, new_m, new_v), outs):
        t.update(zip(small, _unpack(o, [w[n].shape for n in small])))

    return (loss, grad_x[None], *[grads[n] for n in WEIGHTS], *[delta[n] for n in WEIGHTS], *[new_m[n] for n in WEIGHTS],
            *[new_v[n] for n in WEIGHTS])
```

```python
import functools
import math

import numpy as np
import jax
import jax.numpy as jnp
from jax import lax
from jax.experimental import pallas as pl
from jax.experimental.pallas import tpu as pltpu

F32 = jnp.float32
BF16 = jnp.bfloat16

D_MODEL = 1024
D_FF = 4096
DEPTH = 4
CHUNK = 64
RMS_EPS = 1e-6
CONV_W = 4
GDN_HEADS = 8
GDN_DK = 128
GDN_IN = 4112
GDN_IN_PAD = 4224
S5_GROUPS = 64
S5_STATE = 64
S5_GROUP = 16
S5_BLOCKS = 8
M2_INNER = 2048
M2_HEADS = 32
M2_GROUPS = 8
M2_STATE = 128
M2_CONV_CH = 4096
M2_IN = 6176
M2_IN_PAD = 6272
ADAM_LR, ADAM_B1, ADAM_B2, ADAM_EPS, ADAM_WD, ADAM_STEP = 0.001, 0.9, 0.999, 1e-08, 0.01, 10

VMEM_LIMIT_BYTES = 56 * 1024 * 1024
SUBLANES = 8
LANES = 128


def _params(*sem):
    return pltpu.CompilerParams(dimension_semantics=tuple(sem) if sem else None, vmem_limit_bytes=VMEM_LIMIT_BYTES)


NN, NT, TN = ((1,), (0,)), ((1,), (1,)), ((0,), (0,))
_DOT_TRANSPOSES = {NN: ((NT, "gb"), (TN, "ag")), NT: ((NN, "gb"), (TN, "ga")), TN: ((NT, "bg"), (NN, "ag"))}


def _dg(a, b, dims):
    if a.ndim == 3:
        dn = (((dims[0][0] + 1,), (dims[1][0] + 1,)), ((0,), (0,)))
    else:
        dn = (dims, ((), ()))
    return lax.dot_general(a, b, dn, preferred_element_type=F32)


def _mxu(a, b, dims):
    return _dg(a.astype(BF16), b.astype(BF16), dims)


@functools.partial(jax.custom_vjp, nondiff_argnums=(2,))
def _dot(a, b, dims=NN):
    return _mxu(a, b, dims)


def _dot_fwd(a, b, dims):
    return _mxu(a, b, dims), (a, b)


def _dot_bwd(dims, res, g):
    ops = dict(a=res[0], b=res[1], g=g)
    (da_dims, da_ops), (db_dims, db_ops) = _DOT_TRANSPOSES[dims]
    return (_mxu(ops[da_ops[0]], ops[da_ops[1]], da_dims).astype(res[0].dtype),
            _mxu(ops[db_ops[0]], ops[db_ops[1]], db_dims).astype(res[1].dtype))


_dot.defvjp(_dot_fwd, _dot_bwd)


def _nt(a, b):
    return _dot(a, b, NT)


def _tn(a, b):
    return _dot(a, b, TN)


def _split3(x):
    x1 = x.astype(BF16)
    r = x - x1.astype(F32)
    x2 = r.astype(BF16)
    return x1, x2, (r - x2.astype(F32)).astype(BF16)


def _sel_mxu(x, sel, dims, x_first):
    f = (lambda p: _dg(p, sel.astype(BF16), dims)) if x_first else (lambda p: _dg(sel.astype(BF16), p, dims))
    x1, x2, x3 = _split3(x)
    return f(x1) + (f(x2) + f(x3))


@jax.custom_vjp
def _pick(x, sel):
    return _sel_mxu(x, sel, NN, True)


def _pick_fwd(x, sel):
    return _sel_mxu(x, sel, NN, True), sel


def _pick_bwd(sel, g):
    return _sel_mxu(g, sel, NT, True), jnp.zeros_like(sel)


_pick.defvjp(_pick_fwd, _pick_bwd)


@jax.custom_vjp
def _accum(sel, x):
    return _sel_mxu(x, sel, NN, False)


def _accum_fwd(sel, x):
    return _sel_mxu(x, sel, NN, False), sel


def _accum_bwd(sel, g):
    return jnp.zeros_like(sel), _sel_mxu(g, sel, TN, False)


_accum.defvjp(_accum_fwd, _accum_bwd)


def _dot3(a, b, dims=NN):
    ah, bh = a.astype(BF16), b.astype(BF16)
    al, bl = (a - ah.astype(F32)).astype(BF16), (b - bh.astype(F32)).astype(BF16)
    return _dg(ah, bh, dims) + (_dg(ah, bl, dims) + _dg(al, bh, dims))


def _neumann(x, r, dims):
    r = r + _dot3(x, r, dims)
    for _ in range(5):
        x = _dot3(x, x)
        r = r + _dot3(x, r, dims)
    return r


@jax.custom_vjp
def _unit_lower_solve(a, rhs):
    return _neumann(-a, rhs, NN)


def _unit_lower_solve_fwd(a, rhs):
    sol = _neumann(-a, rhs, NN)
    return sol, (a, sol)


def _unit_lower_solve_bwd(res, ct):
    a, sol = res
    d_rhs = _neumann(-a, ct, TN)
    return -_dot3(d_rhs, sol, NT), d_rhs


_unit_lower_solve.defvjp(_unit_lower_solve_fwd, _unit_lower_solve_bwd)


def _sigmoid(x):
    return 1.0 / (1.0 + jnp.exp(-x))


def _softplus(x):
    return jnp.maximum(x, 0.0) + jnp.log(1.0 + jnp.exp(-jnp.abs(x)))


def _iota2(shape, axis):
    return lax.broadcasted_iota(jnp.int32, shape, axis)


def _tile(n, cands):
    for c in cands:
        if n % c == 0:
            return c
    return n


def _mm(name, a, b, mode, out_dtypes, epi=None, extras=(), tm=512, tn=None):
    if mode == "nn":
        (M, K), N = a.shape, b.shape[1]
    elif mode == "nt":
        (M, K), N = a.shape, b.shape[0]
    else:
        (K, M), N = a.shape, b.shape[1]
    tm = _tile(M, (tm, 256, 128))
    tn = tn or _tile(N, (512, 384, 896, 256, 128))
    if mode == "nn":
        a_spec, b_spec = pl.BlockSpec((tm, K), lambda i, j: (i, 0)), pl.BlockSpec((K, tn), lambda i, j: (0, j))
        dims = NN
    elif mode == "nt":
        a_spec, b_spec = pl.BlockSpec((tm, K), lambda i, j: (i, 0)), pl.BlockSpec((tn, K), lambda i, j: (j, 0))
        dims = NT
    else:
        a_spec, b_spec = pl.BlockSpec((K, tm), lambda i, j: (0, i)), pl.BlockSpec((K, tn), lambda i, j: (0, j))
        dims = TN
    n_ex = len(extras)

    def body(a_ref, b_ref, *rest):
        acc = _mxu(a_ref[...], b_ref[...], dims)
        res = epi(acc, *[e[...] for e in rest[:n_ex]]) if epi is not None else (acc,)
        for o_ref, r in zip(rest[n_ex:], res):
            o_ref[...] = r.astype(o_ref.dtype)

    tile = pl.BlockSpec((tm, tn), lambda i, j: (i, j))
    out = pl.pallas_call(
        body, name=name, grid=(M // tm, N // tn),
        in_specs=[a_spec, b_spec] + [tile] * n_ex,
        out_specs=[tile] * len(out_dtypes),
        out_shape=[jax.ShapeDtypeStruct((M, N), d) for d in out_dtypes],
        compiler_params=_params("parallel", "parallel"),
    )(a, b, *extras)
    return out if len(out_dtypes) > 1 else out[0]


def _rms_fwd(name, h, g):
    L, D = h.shape
    tr = _tile(L, (256, 128))

    def body(h_ref, g_ref, o_ref):
        x = h_ref[...]
        r = lax.rsqrt(jnp.mean(x * x, axis=-1, keepdims=True) + RMS_EPS)
        o_ref[...] = (x * r * g_ref[...]).astype(o_ref.dtype)

    return pl.pallas_call(
        body, name=name, grid=(L // tr,),
        in_specs=[pl.BlockSpec((tr, D), lambda i: (i, 0)), pl.BlockSpec((1, D), lambda i: (0, 0))],
        out_specs=pl.BlockSpec((tr, D), lambda i: (i, 0)),
        out_shape=jax.ShapeDtypeStruct((L, D), BF16),
        compiler_params=_params("parallel"),
    )(h, g.reshape(1, D))


def _rms_bwd(name, h, g, dhn, dres):
    L, D = h.shape
    tr = _tile(L, (256, 128))

    def body(h_ref, g_ref, dhn_ref, dres_ref, dh_ref, dg_ref):
        x = h_ref[...]
        r = lax.rsqrt(jnp.mean(x * x, axis=-1, keepdims=True) + RMS_EPS)
        xh = x * r
        dy = dhn_ref[...]
        dxh = dy * g_ref[...]
        dh_ref[...] = dres_ref[...] + r * (dxh - xh * jnp.mean(dxh * xh, axis=-1, keepdims=True))

        @pl.when(pl.program_id(0) == 0)
        def _():
            dg_ref[...] = jnp.zeros_like(dg_ref)

        dg_ref[...] += jnp.sum(dy * xh, axis=0, keepdims=True)

    row = pl.BlockSpec((tr, D), lambda i: (i, 0))
    vec = pl.BlockSpec((1, D), lambda i: (0, 0))
    return pl.pallas_call(
        body, name=name, grid=(L // tr,),
        in_specs=[row, vec, row, row], out_specs=[row, vec],
        out_shape=[jax.ShapeDtypeStruct((L, D), F32), jax.ShapeDtypeStruct((1, D), F32)],
        compiler_params=_params("arbitrary"),
    )(h, g.reshape(1, D), dhn, dres)


def _loss_head(h, g, target):
    L, D = h.shape
    tr = _tile(L, (256, 128))

    def body(h_ref, g_ref, t_ref, loss_ref, dh_ref, dg_ref):
        x = h_ref[...]
        r = lax.rsqrt(jnp.mean(x * x, axis=-1, keepdims=True) + RMS_EPS)
        xh = x * r
        err = xh * g_ref[...] - t_ref[...]
        dy = err * (1.0 / D)
        dxh = dy * g_ref[...]
        dh_ref[...] = r * (dxh - xh * jnp.mean(dxh * xh, axis=-1, keepdims=True))

        @pl.when(pl.program_id(0) == 0)
        def _():
            dg_ref[...] = jnp.zeros_like(dg_ref)
            loss_ref[...] = jnp.zeros_like(loss_ref)

        dg_ref[...] += jnp.sum(dy * xh, axis=0, keepdims=True)
        loss_ref[...] += (0.5 / D) * jnp.sum(jnp.sum(err * err, axis=-1, keepdims=True), axis=0, keepdims=True)

    row = pl.BlockSpec((tr, D), lambda i: (i, 0))
    vec = pl.BlockSpec((1, D), lambda i: (0, 0))
    return pl.pallas_call(
        body, name="loss_head", grid=(L // tr,),
        in_specs=[row, vec, row], out_specs=[pl.BlockSpec((1, 1), lambda i: (0, 0)), row, vec],
        out_shape=[jax.ShapeDtypeStruct((1, 1), F32), jax.ShapeDtypeStruct((L, D), F32), jax.ShapeDtypeStruct((1, D), F32)],
        compiler_params=_params("arbitrary"),
    )(h, g.reshape(1, D), target)


def _glu_fwd(h, ag):
    L, D = h.shape
    tr = _tile(L, (256, 128))

    def body(h_ref, v_ref, g_ref, o_ref):
        o_ref[...] = h_ref[...] + v_ref[...] * _sigmoid(g_ref[...])

    return pl.pallas_call(
        body, name="s5_glu_fwd", grid=(L // tr,),
        in_specs=[pl.BlockSpec((tr, D), lambda i: (i, 0)), pl.BlockSpec((tr, D), lambda i: (i, 0)),
                  pl.BlockSpec((tr, D), lambda i: (i, 1))],
        out_specs=pl.BlockSpec((tr, D), lambda i: (i, 0)),
        out_shape=jax.ShapeDtypeStruct((L, D), F32),
        compiler_params=_params("parallel"),
    )(h, ag, ag)


def _glu_bwd(dh, ag):
    L, D = dh.shape
    tr = _tile(L, (256, 128))

    def body(dh_ref, v_ref, g_ref, dv_ref, dg_ref):
        s = _sigmoid(g_ref[...])
        d = dh_ref[...]
        dv_ref[...] = d * s
        dg_ref[...] = d * v_ref[...] * s * (1.0 - s)

    dv, dg = pl.pallas_call(
        body, name="s5_glu_bwd", grid=(L // tr,),
        in_specs=[pl.BlockSpec((tr, D), lambda i: (i, 0)), pl.BlockSpec((tr, D), lambda i: (i, 0)),
                  pl.BlockSpec((tr, D), lambda i: (i, 1))],
        out_specs=[pl.BlockSpec((tr, D), lambda i: (i, 0))] * 2,
        out_shape=[jax.ShapeDtypeStruct((L, D), F32)] * 2,
        compiler_params=_params("parallel"),
    )(dh, ag, ag)
    return jnp.concatenate([dv, dg], axis=1)


CONV_ROWS = 128
CONV_COLS = 512


def _shift_rows(cat, s):
    if s == 0:
        return cat[SUBLANES:, :]
    return pltpu.roll(cat, s, axis=0)[SUBLANES:, :]


def _conv_fwd(name, p, col0, w, b):
    L = p.shape[0]
    C = w.shape[1]
    tc = _tile(C, (CONV_COLS, 256))
    cb0 = col0 // tc
    nr = L // CONV_ROWS

    def body(x_ref, w_ref, b_ref, o_ref):
        def step(r, carry):
            r0 = pl.multiple_of(r * CONV_ROWS, CONV_ROWS)
            cur = x_ref[pl.ds(r0, CONV_ROWS), :]
            p0 = pl.multiple_of(jnp.maximum(r0 - SUBLANES, 0), SUBLANES)
            prev = jnp.where(r > 0, x_ref[pl.ds(p0, SUBLANES), :], 0.0)
            cat = jnp.concatenate([prev, cur], axis=0)
            acc = b_ref[...] + w_ref[3:4, :] * cur
            for k in range(CONV_W - 1):
                acc = acc + w_ref[k:k + 1, :] * _shift_rows(cat, CONV_W - 1 - k)
            o_ref[pl.ds(r0, CONV_ROWS), :] = acc * _sigmoid(acc)
            return carry

        lax.fori_loop(0, nr, step, 0)

    return pl.pallas_call(
        body, name=name, grid=(C // tc,),
        in_specs=[pl.BlockSpec((L, tc), lambda j: (0, cb0 + j)), pl.BlockSpec((CONV_W, tc), lambda j: (0, j)),
                  pl.BlockSpec((1, tc), lambda j: (0, j))],
        out_specs=pl.BlockSpec((L, tc), lambda j: (0, j)),
        out_shape=jax.ShapeDtypeStruct((L, C), F32),
        compiler_params=_params("parallel"),
    )(p, w, b)


def _conv_bwd(name, p, col0, w, b, dout):
    L = p.shape[0]
    C = w.shape[1]
    tc = _tile(C, (CONV_COLS, 256))
    cb0 = col0 // tc
    nr = L // CONV_ROWS

    def body(x_ref, w_ref, b_ref, do_ref, dx_ref, dw_ref, db_ref, dpre_ref):
        def step1(r, carry):
            dw0, dw1, dw2, dw3, dbb = carry
            r0 = pl.multiple_of(r * CONV_ROWS, CONV_ROWS)
            cur = x_ref[pl.ds(r0, CONV_ROWS), :]
            p0 = pl.multiple_of(jnp.maximum(r0 - SUBLANES, 0), SUBLANES)
            prev = jnp.where(r > 0, x_ref[pl.ds(p0, SUBLANES), :], 0.0)
            cat = jnp.concatenate([prev, cur], axis=0)
            sh = [_shift_rows(cat, CONV_W - 1 - k) for k in range(CONV_W - 1)] + [cur]
            acc = b_ref[...] + w_ref[3:4, :] * cur
            for k in range(CONV_W - 1):
                acc = acc + w_ref[k:k + 1, :] * sh[k]
            sg = _sigmoid(acc)
            dpre = do_ref[pl.ds(r0, CONV_ROWS), :] * (sg + acc * sg * (1.0 - sg))
            dpre_ref[pl.ds(r0, CONV_ROWS), :] = dpre
            dws = [d + jnp.sum(dpre * s, axis=0, keepdims=True) for d, s in zip((dw0, dw1, dw2, dw3), sh)]
            return (*dws, dbb + jnp.sum(dpre, axis=0, keepdims=True))

        z = jnp.zeros((1, tc), F32)
        dw0, dw1, dw2, dw3, dbb = lax.fori_loop(0, nr, step1, (z, z, z, z, z))
        dw_ref[...] = jnp.concatenate([dw0, dw1, dw2, dw3, z, z, z, z], axis=0)
        db_ref[...] = dbb

        def step2(r, carry):
            r0 = pl.multiple_of(r * CONV_ROWS, CONV_ROWS)
            cur = dpre_ref[pl.ds(r0, CONV_ROWS), :]
            n0 = pl.multiple_of(jnp.minimum(r0 + CONV_ROWS, L - SUBLANES), SUBLANES)
            nxt = jnp.where(r < nr - 1, dpre_ref[pl.ds(n0, SUBLANES), :], 0.0)
            cat = jnp.concatenate([cur, nxt], axis=0)
            acc = w_ref[3:4, :] * cur
            for k in range(CONV_W - 1):
                s = CONV_W - 1 - k
                acc = acc + w_ref[k:k + 1, :] * pltpu.roll(cat, CONV_ROWS + SUBLANES - s, axis=0)[:CONV_ROWS, :]
            dx_ref[pl.ds(r0, CONV_ROWS), :] = acc
            return carry

        lax.fori_loop(0, nr, step2, 0)

    dx, dw, db = pl.pallas_call(
        body, name=name, grid=(C // tc,),
        in_specs=[pl.BlockSpec((L, tc), lambda j: (0, cb0 + j)), pl.BlockSpec((CONV_W, tc), lambda j: (0, j)),
                  pl.BlockSpec((1, tc), lambda j: (0, j)), pl.BlockSpec((L, tc), lambda j: (0, j))],
        out_specs=[pl.BlockSpec((L, tc), lambda j: (0, j)), pl.BlockSpec((SUBLANES, tc), lambda j: (0, j)),
                   pl.BlockSpec((1, tc), lambda j: (0, j))],
        out_shape=[jax.ShapeDtypeStruct((L, C), F32), jax.ShapeDtypeStruct((SUBLANES, C), F32),
                   jax.ShapeDtypeStruct((1, C), F32)],
        scratch_shapes=[pltpu.VMEM((L, tc), F32)],
        compiler_params=_params("parallel"),
    )(p, w, b, dout)
    return dx, dw[:CONV_W], db


def _chunk_consts():
    r, c = _iota2((CHUNK, CHUNK), 0), _iota2((CHUNK, CHUNK), 1)
    causal = r >= c
    return causal, r > c, (r == c).astype(F32), causal.astype(F32), jnp.ones((CHUNK, CHUNK), F32)


def _by_lanes(t):
    return jnp.concatenate([t[i] for i in range(t.shape[0])], axis=1)


def _by_batch(t, w):
    return jnp.concatenate([t[None, :, i * w:(i + 1) * w] for i in range(t.shape[1] // w)], axis=0)


def _diag_lanes():
    return (_iota2((CHUNK, LANES), 0) == _iota2((CHUNK, LANES), 1)).astype(F32)


def _gdn_chunk(q, k, v, ab, gate, S, alog, dtb, og, ea, eb):
    causal, strict, _, tril, ones = _chunk_consts()
    logits = _by_batch(_pick(ab, jnp.concatenate([_by_lanes(ea), _by_lanes(eb)], axis=1)), LANES)
    H = q.shape[0]
    g = -jnp.exp(alog) * _softplus(logits[:H] + dtb)
    beta = _sigmoid(logits[H:])
    qn = q * lax.rsqrt(jnp.sum(q * q, axis=-1, keepdims=True) + 1e-6) * (GDN_DK ** -0.5)
    kn = k * lax.rsqrt(jnp.sum(k * k, axis=-1, keepdims=True) + 1e-6)
    g_l = _by_lanes(g)
    gc = _by_batch(_accum(tril, g_l), LANES)
    glast = _by_batch(_accum(ones, g_l), LANES)
    gcol = gc[:, :, :CHUNK]
    grow = _by_batch(_accum(ones, _by_lanes(gc * _diag_lanes())), LANES)[:, :, :CHUNK]
    decay = jnp.exp(jnp.where(causal, gcol - grow, -jnp.inf))
    a = jnp.where(strict, beta[:, :, :CHUNK] * _nt(kn, kn) * decay, 0.0)
    eg = jnp.exp(gc)
    sol = _unit_lower_solve(a, jnp.concatenate([v * beta, kn * (beta * eg)], axis=2))
    u, w = sol[:, :, :GDN_DK], sol[:, :, GDN_DK:]
    qk = _nt(qn, kn) * decay
    v_new = u - _dot(w, S)
    o = _dot(qn * eg, S) + _dot(qk, v_new)
    cd = jnp.exp(glast)
    s_new = jnp.concatenate([cd, cd], axis=1) * S + _tn(kn * jnp.exp(glast - gc), v_new)
    on = o * lax.rsqrt(jnp.mean(o * o, axis=-1, keepdims=True) + RMS_EPS) * og
    return on * (gate * _sigmoid(gate)), s_new


GDN_HB = 8


def _gdn_specs(nc, rev):
    cm = (lambda c: nc - 1 - c) if rev else (lambda c: c)
    blk = lambda off: pl.BlockSpec((CHUNK, GDN_HB * GDN_DK), lambda c, h: (cm(c), off // GDN_HB + h))
    ab = pl.BlockSpec((CHUNK, LANES), lambda c, h: (cm(c), (GDN_IN_PAD - LANES) // LANES))
    hv = pl.BlockSpec((GDN_HB, 1, LANES), lambda c, h: (h, 0, 0))
    og = pl.BlockSpec((1, LANES), lambda c, h: (0, 0))
    em = pl.BlockSpec((GDN_HB, LANES, LANES), lambda c, h: (h, 0, 0))
    st = pl.BlockSpec((None, GDN_HB, GDN_DK, GDN_DK), lambda c, h: (cm(c), h, 0, 0))
    return blk, ab, hv, og, em, st


def _gdn_fwd(qc, kc, vc, p, alog_e, dtb_e, og, ea, eb):
    L = qc.shape[0]
    nc = L // CHUNK
    blk, ab, hv, ogs, em, st = _gdn_specs(nc, False)

    def body(q_ref, k_ref, v_ref, gate_ref, ab_ref, al_ref, dt_ref, og_ref, ea_ref, eb_ref, y_ref, sp_ref, s_scr):
        c, h = pl.program_id(0), pl.program_id(1)
        lanes = [slice(i * GDN_DK, (i + 1) * GDN_DK) for i in range(GDN_HB)]
        heads = pl.ds(h * GDN_HB, GDN_HB)
        stack = lambda ref: jnp.concatenate([ref[:, ls][None] for ls in lanes], axis=0)

        @pl.when(c == 0)
        def _():
            s_scr[heads] = jnp.zeros((GDN_HB, GDN_DK, GDN_DK), F32)

        S = s_scr[heads]
        sp_ref[...] = S
        y, s_new = _gdn_chunk(stack(q_ref), stack(k_ref), stack(v_ref), ab_ref[...], stack(gate_ref), S,
                              al_ref[...], dt_ref[...], og_ref[...], ea_ref[...], eb_ref[...])
        for i, ls in enumerate(lanes):
            y_ref[:, ls] = y[i]
        s_scr[heads] = s_new

    return pl.pallas_call(
        body, name="gdn_fwd", grid=(nc, GDN_HEADS // GDN_HB),
        in_specs=[blk(0), blk(0), blk(0), blk(3 * GDN_HEADS), ab, hv, hv, ogs, em, em],
        out_specs=[blk(0), st],
        out_shape=[jax.ShapeDtypeStruct((L, D_MODEL), F32), jax.ShapeDtypeStruct((nc, GDN_HEADS, GDN_DK, GDN_DK), F32)],
        scratch_shapes=[pltpu.VMEM((GDN_HEADS, GDN_DK, GDN_DK), F32)],
        compiler_params=_params("arbitrary", "arbitrary"),
    )(qc, kc, vc, p, p, alog_e, dtb_e, og, ea, eb)


def _gdn_bwd(qc, kc, vc, p, alog_e, dtb_e, og, ea, eb, sprev, dy):
    L = qc.shape[0]
    nc = L // CHUNK
    blk, ab, hv, ogs, em, st = _gdn_specs(nc, True)

    def body(q_ref, k_ref, v_ref, gate_ref, ab_ref, al_ref, dt_ref, og_ref, ea_ref, eb_ref, sp_ref, dy_ref,
             dq_ref, dk_ref, dv_ref, dgate_ref, dab_ref, dpar_ref, ds_scr):
        c, h = pl.program_id(0), pl.program_id(1)
        lanes = [slice(i * GDN_DK, (i + 1) * GDN_DK) for i in range(GDN_HB)]
        heads = pl.ds(h * GDN_HB, GDN_HB)
        stack = lambda ref: jnp.concatenate([ref[:, ls][None] for ls in lanes], axis=0)

        @pl.when(c == 0)
        def _():
            ds_scr[heads] = jnp.zeros((GDN_HB, GDN_DK, GDN_DK), F32)
            dpar_ref[heads] = jnp.zeros((GDN_HB, SUBLANES, LANES), F32)

        @pl.when(h == 0)
        def _():
            dab_ref[...] = jnp.zeros_like(dab_ref)

        ea_m, eb_m = ea_ref[...], eb_ref[...]
        f = lambda q, k, v, a_b, gate, S, al, dt, o_g: _gdn_chunk(q, k, v, a_b, gate, S, al, dt, o_g, ea_m, eb_m)
        _, vjp = jax.vjp(f, stack(q_ref), stack(k_ref), stack(v_ref), ab_ref[...], stack(gate_ref), sp_ref[...],
                         al_ref[...], dt_ref[...], og_ref[...])
        dq, dk, dv, dab, dgate, ds, dal, ddt, dog = vjp((stack(dy_ref), ds_scr[heads]))
        for i, ls in enumerate(lanes):
            dq_ref[:, ls] = dq[i]
            dk_ref[:, ls] = dk[i]
            dv_ref[:, ls] = dv[i]
            dgate_ref[:, ls] = dgate[i]
        ds_scr[heads] = ds
        dab_ref[...] += dab
        first = _iota2((GDN_HB, 1, LANES), 0) == 0
        dpar_ref[heads] += jnp.concatenate([dal, ddt, jnp.where(first, dog[None], 0.0),
                                            jnp.zeros((GDN_HB, SUBLANES - 3, LANES), F32)], axis=1)

    return pl.pallas_call(
        body, name="gdn_bwd", grid=(nc, GDN_HEADS // GDN_HB),
        in_specs=[blk(0), blk(0), blk(0), blk(3 * GDN_HEADS), ab, hv, hv, ogs, em, em, st, blk(0)],
        out_specs=[blk(0), blk(0), blk(0), blk(0), pl.BlockSpec((CHUNK, LANES), lambda c, h: (nc - 1 - c, 0)),
                   pl.BlockSpec((GDN_HEADS, SUBLANES, LANES), lambda c, h: (0, 0, 0))],
        out_shape=[jax.ShapeDtypeStruct((L, D_MODEL), F32)] * 4
        + [jax.ShapeDtypeStruct((L, LANES), F32), jax.ShapeDtypeStruct((GDN_HEADS, SUBLANES, LANES), F32)],
        scratch_shapes=[pltpu.VMEM((GDN_HEADS, GDN_DK, GDN_DK), F32)],
        compiler_params=_params("arbitrary", "arbitrary"),
    )(qc, kc, vc, p, p, alog_e, dtb_e, og, ea, eb, sprev, dy)


def _gdn_selectors():
    rows = np.arange(LANES)[None, :, None]
    heads = np.arange(GDN_HEADS)[:, None, None]
    ea = np.broadcast_to(rows == heads, (GDN_HEADS, LANES, LANES)).astype(np.float32)
    eb = np.broadcast_to(rows == heads + GDN_HEADS, (GDN_HEADS, LANES, LANES)).astype(np.float32)
    return jnp.asarray(ea), jnp.asarray(eb)


M2_GW = M2_INNER // M2_GROUPS
M2_HPG = M2_HEADS // M2_GROUPS
M2_HD = M2_INNER // M2_HEADS


def _m2_chunk(x, bm, cm, z, dtr, st, dtb, alog, dsk, ng, e, ecol):
    G = x.shape[0]
    causal, _, _, tril, ones = _chunk_consts()
    dt_n = _softplus(dtr + dtb)
    da_n = dt_n * (-jnp.exp(alog))
    cum_n = _accum(tril, da_n)
    tot_n = _accum(ones, da_n)
    wide = _pick(jnp.concatenate([dt_n, cum_n, tot_n], axis=0), e)
    dt_w, cum_w, tot_w = (_by_batch(wide[i * CHUNK:(i + 1) * CHUNK], M2_GW) for i in range(3))
    xdt = x * dt_w
    cb = _nt(cm, bm)
    heads = lambda t: jnp.concatenate([t[i:i + 1] for i in range(G) for _ in range(M2_HPG)], axis=0)
    colb = _by_batch(_pick(cum_n, ecol), LANES)
    rowb = _by_batch(_accum(ones, _by_lanes(colb * _diag_lanes())), LANES)
    lmat = jnp.exp(jnp.where(causal, colb[:, :, :CHUNK] - rowb[:, :, :CHUNK], -jnp.inf))
    yr = _dot(heads(cb) * lmat, heads(xdt))
    head = _iota2((CHUNK, M2_GW), 1) // M2_HD
    ydiag = jnp.concatenate([sum(jnp.where(head == r, yr[i * M2_HPG + r], 0.0) for r in range(M2_HPG))[None] for i in range(G)], axis=0)
    st_new = _tn(bm, xdt * jnp.exp(tot_w - cum_w))
    cd = jnp.exp(tot_w)
    s_new = jnp.concatenate([cd, cd], axis=1) * st + st_new
    y = ydiag + _dot(cm, st) * jnp.exp(cum_w) + dsk * x
    y = y * (z * _sigmoid(z))
    yn = y * lax.rsqrt(jnp.mean(y * y, axis=-1, keepdims=True) + RMS_EPS) * ng
    return yn, s_new


M2_GB = 4


def _m2_specs(nc, rev):
    cm = (lambda c: nc - 1 - c) if rev else (lambda c: c)
    wide = lambda off: pl.BlockSpec((CHUNK, M2_GB * M2_GW), lambda c, g: (cm(c), off // M2_GB + g))
    nar = lambda off: pl.BlockSpec((CHUNK, M2_GB * LANES), lambda c, g: (cm(c), off // M2_GB + g))
    dts = pl.BlockSpec((CHUNK, LANES), lambda c, g: (cm(c), (M2_IN_PAD - LANES) // LANES))
    v128 = pl.BlockSpec((1, LANES), lambda c, g: (0, 0))
    v256 = pl.BlockSpec((1, M2_GB * M2_GW), lambda c, g: (0, g))
    es = pl.BlockSpec((LANES, M2_GB * M2_GW), lambda c, g: (0, g))
    ecs = pl.BlockSpec((LANES, M2_GB * M2_HPG * LANES), lambda c, g: (0, g))
    st = pl.BlockSpec((None, M2_GB, M2_STATE, M2_GW), lambda c, g: (cm(c), g, 0, 0))
    return wide, nar, dts, v128, v256, es, ecs, st


def _m2_fwd(xbc, p, dtb, alog, dsk, ng, e, ecol):
    L = xbc.shape[0]
    nc = L // CHUNK
    wide, nar, dts, v128, v256, es, ecs, st = _m2_specs(nc, False)

    def body(x_ref, b_ref, c_ref, z_ref, dt_ref, dtb_ref, al_ref, dsk_ref, ng_ref, e_ref, ec_ref, y_ref, sp_ref, s_scr):
        c, g = pl.program_id(0), pl.program_id(1)
        wide_l = [slice(i * M2_GW, (i + 1) * M2_GW) for i in range(M2_GB)]
        nar_l = [slice(i * LANES, (i + 1) * LANES) for i in range(M2_GB)]
        groups = pl.ds(g * M2_GB, M2_GB)
        wide_s = lambda ref: jnp.concatenate([ref[:, ls][None] for ls in wide_l], axis=0)
        nar_s = lambda ref: jnp.concatenate([ref[:, ls][None] for ls in nar_l], axis=0)

        @pl.when(c == 0)
        def _():
            s_scr[groups] = jnp.zeros((M2_GB, M2_STATE, M2_GW), F32)

        S = s_scr[groups]
        sp_ref[...] = S
        y, s_new = _m2_chunk(wide_s(x_ref), nar_s(b_ref), nar_s(c_ref), wide_s(z_ref), dt_ref[...], S, dtb_ref[...], al_ref[...],
                             wide_s(dsk_ref), wide_s(ng_ref), e_ref[...], ec_ref[...])
        for i, ls in enumerate(wide_l):
            y_ref[:, ls] = y[i]
        s_scr[groups] = s_new

    return pl.pallas_call(
        body, name="m2_fwd", grid=(nc, M2_GROUPS // M2_GB),
        in_specs=[wide(0), nar(2 * M2_GROUPS), nar(3 * M2_GROUPS), wide(0), dts, v128, v128, v256, v256, es, ecs],
        out_specs=[wide(0), st],
        out_shape=[jax.ShapeDtypeStruct((L, M2_INNER), F32), jax.ShapeDtypeStruct((nc, M2_GROUPS, M2_STATE, M2_GW), F32)],
        scratch_shapes=[pltpu.VMEM((M2_GROUPS, M2_STATE, M2_GW), F32)],
        compiler_params=_params("arbitrary", "arbitrary"),
    )(xbc, xbc, xbc, p, p, dtb, alog, dsk, ng, e, ecol)


def _m2_bwd(xbc, p, dtb, alog, dsk, ng, e, ecol, sprev, dy):
    L = xbc.shape[0]
    nc = L // CHUNK
    wide, nar, dts, v128, v256, es, ecs, st = _m2_specs(nc, True)

    def body(x_ref, b_ref, c_ref, z_ref, dt_ref, dtb_ref, al_ref, dsk_ref, ng_ref, e_ref, ec_ref, sp_ref, dy_ref,
             dx_ref, db_ref, dc_ref, dz_ref, ddt_ref, dnar_ref, dwide_ref, ds_scr):
        c, g = pl.program_id(0), pl.program_id(1)
        wide_l = [slice(i * M2_GW, (i + 1) * M2_GW) for i in range(M2_GB)]
        nar_l = [slice(i * LANES, (i + 1) * LANES) for i in range(M2_GB)]
        groups = pl.ds(g * M2_GB, M2_GB)
        wide_s = lambda ref: jnp.concatenate([ref[:, ls][None] for ls in wide_l], axis=0)
        nar_s = lambda ref: jnp.concatenate([ref[:, ls][None] for ls in nar_l], axis=0)

        @pl.when(jnp.logical_and(c == 0, g == 0))
        def _():
            dnar_ref[...] = jnp.zeros_like(dnar_ref)

        @pl.when(c == 0)
        def _():
            ds_scr[groups] = jnp.zeros((M2_GB, M2_STATE, M2_GW), F32)
            dwide_ref[groups] = jnp.zeros((M2_GB, SUBLANES, M2_GW), F32)

        @pl.when(g == 0)
        def _():
            ddt_ref[...] = jnp.zeros_like(ddt_ref)

        e_m, ec_m = e_ref[...], ec_ref[...]
        f = lambda x, bm, cm, z, dtr, S, dtb, al, dsk, ng: _m2_chunk(x, bm, cm, z, dtr, S, dtb, al, dsk, ng, e_m, ec_m)
        _, vjp = jax.vjp(f, wide_s(x_ref), nar_s(b_ref), nar_s(c_ref), wide_s(z_ref), dt_ref[...], sp_ref[...], dtb_ref[...],
                         al_ref[...], wide_s(dsk_ref), wide_s(ng_ref))
        dx, db, dc, dz, ddt, ds, ddtb, dal, ddsk, dng = vjp((wide_s(dy_ref), ds_scr[groups]))
        for i in range(M2_GB):
            dx_ref[:, wide_l[i]] = dx[i]
            db_ref[:, nar_l[i]] = db[i]
            dc_ref[:, nar_l[i]] = dc[i]
            dz_ref[:, wide_l[i]] = dz[i]
        ds_scr[groups] = ds
        ddt_ref[...] += ddt
        dnar_ref[...] += jnp.concatenate([ddtb, dal, jnp.zeros((SUBLANES - 2, LANES), F32)], axis=0)
        dwide_ref[groups] += jnp.concatenate([ddsk, dng, jnp.zeros((M2_GB, SUBLANES - 2, M2_GW), F32)], axis=1)

    return pl.pallas_call(
        body, name="m2_bwd", grid=(nc, M2_GROUPS // M2_GB),
        in_specs=[wide(0), nar(2 * M2_GROUPS), nar(3 * M2_GROUPS), wide(0), dts, v128, v128, v256, v256, es, ecs, st, wide(0)],
        out_specs=[wide(0), nar(0), nar(0), wide(0), pl.BlockSpec((CHUNK, LANES), lambda c, g: (nc - 1 - c, 0)),
                   pl.BlockSpec((SUBLANES, LANES), lambda c, g: (0, 0)),
                   pl.BlockSpec((M2_GROUPS, SUBLANES, M2_GW), lambda c, g: (0, 0, 0))],
        out_shape=[jax.ShapeDtypeStruct((L, M2_INNER), F32), jax.ShapeDtypeStruct((L, M2_GROUPS * M2_STATE), F32),
                   jax.ShapeDtypeStruct((L, M2_GROUPS * M2_STATE), F32), jax.ShapeDtypeStruct((L, M2_INNER), F32),
                   jax.ShapeDtypeStruct((L, LANES), F32), jax.ShapeDtypeStruct((SUBLANES, LANES), F32),
                   jax.ShapeDtypeStruct((M2_GROUPS, SUBLANES, M2_GW), F32)],
        scratch_shapes=[pltpu.VMEM((M2_GROUPS, M2_STATE, M2_GW), F32)],
        compiler_params=_params("arbitrary", "arbitrary"),
    )(xbc, xbc, xbc, p, p, dtb, alog, dsk, ng, e, ecol, sprev, dy)


def _m2_selectors():
    e = np.zeros((LANES, M2_INNER), np.float32)
    ecol = np.zeros((LANES, M2_HEADS * LANES), np.float32)
    for h in range(M2_HEADS):
        e[h, M2_HD * h:M2_HD * (h + 1)] = 1.0
        ecol[h, LANES * h:LANES * (h + 1)] = 1.0
    return jnp.asarray(e), jnp.asarray(ecol)


S5_NS = S5_GROUPS * S5_STATE // S5_BLOCKS
S5_ROWS = 256
GELU_C = math.sqrt(2.0 / math.pi)


def _gelu(x):
    return 0.5 * x * (1.0 + jnp.tanh(GELU_C * (x + 0.044715 * x * x * x)))


def _gelu_grad(x):
    t = jnp.tanh(GELU_C * (x + 0.044715 * x * x * x))
    return 0.5 * (1.0 + t) + 0.5 * x * (1.0 - t * t) * GELU_C * (1.0 + 3.0 * 0.044715 * x * x)


def _s5_scan(re_ref, im_ref, pw_re, pw_im, nrows, reverse):
    n = re_ref.shape[1]
    row = _iota2((SUBLANES, n), 0)
    steps = [(d, pw_re[d - 1:d, :], pw_im[d - 1:d, :]) for d in (1, 2, 4)]
    if reverse:
        cw_re = jnp.concatenate([pw_re[SUBLANES - 1 - k:SUBLANES - k, :] for k in range(SUBLANES)], axis=0)
        cw_im = jnp.concatenate([pw_im[SUBLANES - 1 - k:SUBLANES - k, :] for k in range(SUBLANES)], axis=0)
    else:
        cw_re, cw_im = pw_re, pw_im
    edge = 0 if reverse else SUBLANES - 1
    ngroups = nrows // SUBLANES

    def step(i, carry):
        cr, ci = carry
        gi = (ngroups - 1 - i) if reverse else i
        r0 = pl.multiple_of(gi * SUBLANES, SUBLANES)
        xr, xi = re_ref[pl.ds(r0, SUBLANES), :], im_ref[pl.ds(r0, SUBLANES), :]
        for d, pr, pi in steps:
            if reverse:
                sr = jnp.where(row < SUBLANES - d, pltpu.roll(xr, SUBLANES - d, axis=0), 0.0)
                si = jnp.where(row < SUBLANES - d, pltpu.roll(xi, SUBLANES - d, axis=0), 0.0)
            else:
                sr = jnp.where(row >= d, pltpu.roll(xr, d, axis=0), 0.0)
                si = jnp.where(row >= d, pltpu.roll(xi, d, axis=0), 0.0)
            xr, xi = xr + (pr * sr - pi * si), xi + (pr * si + pi * sr)
        xr, xi = xr + (cw_re * cr - cw_im * ci), xi + (cw_re * ci + cw_im * cr)
        re_ref[pl.ds(r0, SUBLANES), :] = xr
        im_ref[pl.ds(r0, SUBLANES), :] = xi
        return (jnp.sum(jnp.where(row == edge, xr, 0.0), axis=0, keepdims=True),
                jnp.sum(jnp.where(row == edge, xi, 0.0), axis=0, keepdims=True))

    z = jnp.zeros((1, n), F32)
    lax.fori_loop(0, ngroups, step, (z, z))


def _s5_project_in(u_ref, bm_ref, re_ref, im_ref, L):
    def step(i, carry):
        r0 = pl.multiple_of(i * S5_ROWS, S5_ROWS)
        bu = _dot(u_ref[pl.ds(r0, S5_ROWS), :], bm_ref[...])
        re_ref[pl.ds(r0, S5_ROWS), :] = bu[:, :S5_NS]
        im_ref[pl.ds(r0, S5_ROWS), :] = bu[:, S5_NS:]
        return carry

    lax.fori_loop(0, L // S5_ROWS, step, 0)


def _s5_specs(L):
    col = pl.BlockSpec((L, LANES), lambda j: (0, j))
    bm = pl.BlockSpec((None, LANES, 2 * S5_NS), lambda j: (j, 0, 0))
    cm = pl.BlockSpec((None, 2 * S5_NS, LANES), lambda j: (j, 0, 0))
    pw = pl.BlockSpec((None, SUBLANES, S5_NS), lambda j: (j, 0, 0))
    vec = pl.BlockSpec((1, LANES), lambda j: (0, j))
    return col, bm, cm, pw, vec


def _s5_fwd(u, bmat, cmat, pw_re, pw_im, dsk):
    L = u.shape[0]
    col, bm, cm, pw, vec = _s5_specs(L)

    def body(u_ref, bm_ref, cm_ref, pr_ref, pi_ref, d_ref, y_ref, re_scr, im_scr):
        _s5_project_in(u_ref, bm_ref, re_scr, im_scr, L)
        _s5_scan(re_scr, im_scr, pr_ref[...], pi_ref[...], L, False)

        def step(i, carry):
            r0 = pl.multiple_of(i * S5_ROWS, S5_ROWS)
            rows = pl.ds(r0, S5_ROWS)
            y = _dot(re_scr[rows, :], cm_ref[:S5_NS, :]) + _dot(im_scr[rows, :], cm_ref[S5_NS:, :]) + d_ref[...] * u_ref[rows, :]
            y_ref[rows, :] = _gelu(y)
            return carry

        lax.fori_loop(0, L // S5_ROWS, step, 0)

    return pl.pallas_call(
        body, name="s5_fwd", grid=(S5_BLOCKS,),
        in_specs=[col, bm, cm, pw, pw, vec], out_specs=col,
        out_shape=jax.ShapeDtypeStruct((L, D_MODEL), F32),
        scratch_shapes=[pltpu.VMEM((L, S5_NS), F32)] * 2,
        compiler_params=_params("parallel"),
    )(u, bmat, cmat, pw_re, pw_im, dsk)


def _s5_bwd(u, bmat, cmat, pw_re, pw_im, dsk, dyg):
    L = u.shape[0]
    col, bm, cm, pw, vec = _s5_specs(L)

    def body(u_ref, bm_ref, cm_ref, pr_ref, pi_ref, d_ref, dy_ref, du_ref, dbm_ref, dcm_ref, dlam_ref, dd_ref,
             re_scr, im_scr, gr_scr, gi_scr, dyp_scr):
        _s5_project_in(u_ref, bm_ref, re_scr, im_scr, L)
        _s5_scan(re_scr, im_scr, pr_ref[...], pi_ref[...], L, False)

        def step(i, carry):
            dcr, dci, dd = carry
            r0 = pl.multiple_of(i * S5_ROWS, S5_ROWS)
            rows = pl.ds(r0, S5_ROWS)
            sr, si, uu = re_scr[rows, :], im_scr[rows, :], u_ref[rows, :]
            y = _dot(sr, cm_ref[:S5_NS, :]) + _dot(si, cm_ref[S5_NS:, :]) + d_ref[...] * uu
            dyp = dy_ref[rows, :] * _gelu_grad(y)
            dyp_scr[rows, :] = dyp
            gr_scr[rows, :] = _nt(dyp, cm_ref[:S5_NS, :])
            gi_scr[rows, :] = _nt(dyp, cm_ref[S5_NS:, :])
            return dcr + _tn(sr, dyp), dci + _tn(si, dyp), dd + jnp.sum(dyp * uu, axis=0, keepdims=True)

        zc = jnp.zeros((S5_NS, LANES), F32)
        dcr, dci, dd = lax.fori_loop(0, L // S5_ROWS, step, (zc, zc, jnp.zeros((1, LANES), F32)))
        dcm_ref[:S5_NS, :] = dcr
        dcm_ref[S5_NS:, :] = dci
        dd_ref[...] = dd

        _s5_scan(gr_scr, gi_scr, pr_ref[...], -pi_ref[...], L, True)

        row = _iota2((SUBLANES, S5_NS), 0)

        def lam_step(i, carry):
            ar, ai, pr, pi = carry
            r0 = pl.multiple_of(i * SUBLANES, SUBLANES)
            rows = pl.ds(r0, SUBLANES)
            sr, si = re_scr[rows, :], im_scr[rows, :]
            spr = jnp.where(row >= 1, pltpu.roll(sr, 1, axis=0), pr)
            spi = jnp.where(row >= 1, pltpu.roll(si, 1, axis=0), pi)
            gr, gi = gr_scr[rows, :], gi_scr[rows, :]
            ar = ar + jnp.sum(spr * gr + spi * gi, axis=0, keepdims=True)
            ai = ai + jnp.sum(spr * gi - spi * gr, axis=0, keepdims=True)
            last = row == SUBLANES - 1
            return (ar, ai, jnp.sum(jnp.where(last, sr, 0.0), axis=0, keepdims=True),
                    jnp.sum(jnp.where(last, si, 0.0), axis=0, keepdims=True))

        z = jnp.zeros((1, S5_NS), F32)
        ar, ai, _, _ = lax.fori_loop(0, L // SUBLANES, lam_step, (z, z, z, z))
        dlam_ref[...] = jnp.concatenate([ar, ai, jnp.zeros((SUBLANES - 2, S5_NS), F32)], axis=0)

        def in_step(i, carry):
            dbr, dbi = carry
            r0 = pl.multiple_of(i * S5_ROWS, S5_ROWS)
            rows = pl.ds(r0, S5_ROWS)
            gr, gi, uu = gr_scr[rows, :], gi_scr[rows, :], u_ref[rows, :]
            du_ref[rows, :] = dyp_scr[rows, :] * d_ref[...] + _nt(gr, bm_ref[:, :S5_NS]) + _nt(gi, bm_ref[:, S5_NS:])
            return dbr + _tn(uu, gr), dbi + _tn(uu, gi)

        zb = jnp.zeros((LANES, S5_NS), F32)
        dbr, dbi = lax.fori_loop(0, L // S5_ROWS, in_step, (zb, zb))
        dbm_ref[:, :S5_NS] = dbr
        dbm_ref[:, S5_NS:] = dbi

    return pl.pallas_call(
        body, name="s5_bwd", grid=(S5_BLOCKS,),
        in_specs=[col, bm, cm, pw, pw, vec, col], out_specs=[col, bm, cm, pw, vec],
        out_shape=[jax.ShapeDtypeStruct((L, D_MODEL), F32), jax.ShapeDtypeStruct((S5_BLOCKS, LANES, 2 * S5_NS), F32),
                   jax.ShapeDtypeStruct((S5_BLOCKS, 2 * S5_NS, LANES), F32),
                   jax.ShapeDtypeStruct((S5_BLOCKS, SUBLANES, S5_NS), F32), jax.ShapeDtypeStruct((1, D_MODEL), F32)],
        scratch_shapes=[pltpu.VMEM((L, S5_NS), F32)] * 4 + [pltpu.VMEM((L, LANES), F32)],
        compiler_params=_params("parallel"),
    )(u, bmat, cmat, pw_re, pw_im, dsk, dyg)


def _s5_discretize(lam_re, lam_im, log_dt, b_re, b_im, e16):
    dt = jnp.exp(log_dt)
    zr, zi = lam_re * dt, lam_im * dt
    mag = jnp.exp(zr)
    lbr, lbi = mag * jnp.cos(zi), mag * jnp.sin(zi)
    den = lam_re * lam_re + lam_im * lam_im
    nr, ni = lbr - 1.0, lbi
    cr = (nr * lam_re + ni * lam_im) / den
    ci = (ni * lam_re - nr * lam_im) / den
    crw, ciw = _pick(cr, e16), _pick(ci, e16)
    return lbr, lbi, crw * b_re - ciw * b_im, crw * b_im + ciw * b_re


def _s5_params_fwd(lam_re, lam_im, log_dt, b_re, b_im, e16):
    def body(lr, li, ld, br, bi, e, o1, o2, o3, o4):
        for o, val in zip((o1, o2, o3, o4), _s5_discretize(lr[...], li[...], ld[...], br[...], bi[...], e[...])):
            o[...] = val

    g, p, n = S5_GROUPS, S5_STATE, S5_STATE * S5_GROUP
    return pl.pallas_call(
        body, name="s5_params_fwd",
        out_shape=[jax.ShapeDtypeStruct((g, p), F32)] * 2 + [jax.ShapeDtypeStruct((g, n), F32)] * 2,
        compiler_params=_params(),
    )(lam_re, lam_im, log_dt, b_re, b_im, e16)


def _s5_params_bwd(lam_re, lam_im, log_dt, b_re, b_im, e16, cts):
    def body(lr, li, ld, br, bi, e, c1, c2, c3, c4, o1, o2, o3, o4, o5):
        e_m = e[...]
        f = lambda a, b, c, d, g: _s5_discretize(a, b, c, d, g, e_m)
        _, vjp = jax.vjp(f, lr[...], li[...], ld[...], br[...], bi[...])
        for o, val in zip((o1, o2, o3, o4, o5), vjp((c1[...], c2[...], c3[...], c4[...]))):
            o[...] = val

    g, p, n = S5_GROUPS, S5_STATE, S5_STATE * S5_GROUP
    return pl.pallas_call(
        body, name="s5_params_bwd",
        out_shape=[jax.ShapeDtypeStruct((g, p), F32)] * 2 + [jax.ShapeDtypeStruct((g, 1), F32)]
        + [jax.ShapeDtypeStruct((g, n), F32)] * 2,
        compiler_params=_params(),
    )(lam_re, lam_im, log_dt, b_re, b_im, e16, *cts)


def _add_residual(acc, h):
    return (acc + h,)


def _mlp_fwd(i, h, g, w1, w2):
    hn = _rms_fwd(f"mlp{i}_norm", h, g)
    r, a = _mm(f"mlp{i}_up", hn, w1, "nn", (BF16, BF16), epi=lambda acc: (jnp.square(jnp.maximum(acc, 0.0)), acc))
    return _mm(f"mlp{i}_down", r, w2, "nn", (F32,), epi=_add_residual, extras=(h,)), (h, hn, r, a)


def _mlp_bwd(i, dh_out, saved, g, w1, w2):
    h, hn, r, a = saved
    dw2 = _mm(f"mlp{i}_dw2", r, dh_out, "tn", (F32,))
    da = _mm(f"mlp{i}_da", dh_out, w2, "nt", (BF16,), epi=lambda acc, aa: (acc * (2.0 * jnp.maximum(aa.astype(F32), 0.0)),),
             extras=(a,))
    dw1 = _mm(f"mlp{i}_dw1", hn, da, "tn", (F32,))
    dhn = _mm(f"mlp{i}_dhn", da, w1, "nt", (F32,))
    dh, dg = _rms_bwd(f"mlp{i}_dnorm", h, g, dhn, dh_out)
    return dh, dg[0], dw1, dw2


def _lanes(v, n):
    return jnp.broadcast_to(v.reshape(n, 1, 1), (n, 1, LANES))


def _gdn_fwd_layer(i, h, g, w_in, conv_w, a_log, dt_bias, o_g, w_out):
    hn = _rms_fwd(f"gdn{i}_norm", h, g)
    p = _mm(f"gdn{i}_in", hn, w_in, "nn", (F32,))
    zb = jnp.zeros((1, D_MODEL), F32)
    qkv = [_conv_fwd(f"gdn{i}_conv{t}", p, t * D_MODEL, conv_w[:, t * D_MODEL:(t + 1) * D_MODEL], zb) for t in range(3)]
    ea, eb = _gdn_selectors()
    y, sprev = _gdn_fwd(*qkv, p, _lanes(a_log, GDN_HEADS), _lanes(dt_bias, GDN_HEADS), o_g.reshape(1, LANES), ea, eb)
    return _mm(f"gdn{i}_out", y, w_out, "nn", (F32,), epi=_add_residual, extras=(h,)), (h, hn, p, qkv, y, sprev)


def _gdn_bwd_layer(i, dh_out, saved, g, w_in, conv_w, a_log, dt_bias, o_g, w_out):
    h, hn, p, qkv, y, sprev = saved
    dy = _mm(f"gdn{i}_dy", dh_out, w_out, "nt", (F32,))
    dw_out = _mm(f"gdn{i}_dwout", y, dh_out, "tn", (F32,))
    ea, eb = _gdn_selectors()
    dq, dk, dv, dgate, dab, dpar = _gdn_bwd(*qkv, p, _lanes(a_log, GDN_HEADS), _lanes(dt_bias, GDN_HEADS),
                                            o_g.reshape(1, LANES), ea, eb, sprev, dy)
    zb = jnp.zeros((1, D_MODEL), F32)
    dpre, dcw = [], []
    for t, d in enumerate((dq, dk, dv)):
        dx, dw, _ = _conv_bwd(f"gdn{i}_dconv{t}", p, t * D_MODEL, conv_w[:, t * D_MODEL:(t + 1) * D_MODEL], zb, d)
        dpre.append(dx)
        dcw.append(dw)
    dp = jnp.concatenate(dpre + [dgate, dab], axis=1)
    dw_in = _mm(f"gdn{i}_dwin", hn, dp, "tn", (F32,))[:, :GDN_IN]
    dhn = _mm(f"gdn{i}_dhn", dp, w_in, "nt", (F32,))
    dh, dg = _rms_bwd(f"gdn{i}_dnorm", h, g, dhn, dh_out)
    grads = dict(w_in=dw_in, conv_w=jnp.concatenate(dcw, axis=1), a_log=jnp.sum(dpar[:, 0, :], axis=-1),
                 dt_bias=jnp.sum(dpar[:, 1, :], axis=-1), o_norm_g=jnp.sum(dpar[:, 2, :], axis=0), w_out=dw_out)
    return dh, dg[0], grads


def _m2_vectors(dt_bias, a_log, d_skip, norm_g):
    pad = lambda v: jnp.pad(v, (0, LANES - M2_HEADS)).reshape(1, LANES)
    return pad(dt_bias), pad(a_log), jnp.repeat(d_skip, M2_HD).reshape(1, M2_INNER), norm_g.reshape(1, M2_INNER)


def _m2_fwd_layer(h, g, w_in, conv_w, conv_b, dt_bias, a_log, d_skip, norm_g, w_out):
    hn = _rms_fwd("m2_norm", h, g)
    p = _mm("m2_in", hn, w_in, "nn", (F32,))
    xbc = _conv_fwd("m2_conv", p, M2_INNER, conv_w, conv_b.reshape(1, M2_CONV_CH))
    e, ecol = _m2_selectors()
    y, sprev = _m2_fwd(xbc, p, *_m2_vectors(dt_bias, a_log, d_skip, norm_g), e, ecol)
    return _mm("m2_out", y, w_out, "nn", (F32,), epi=_add_residual, extras=(h,)), (h, hn, p, xbc, y, sprev)


def _m2_bwd_layer(dh_out, saved, g, w_in, conv_w, conv_b, dt_bias, a_log, d_skip, norm_g, w_out):
    h, hn, p, xbc, y, sprev = saved
    dy = _mm("m2_dy", dh_out, w_out, "nt", (F32,))
    dw_out = _mm("m2_dwout", y, dh_out, "tn", (F32,))
    e, ecol = _m2_selectors()
    dx, db, dc, dz, ddt, dnar, dwide = _m2_bwd(xbc, p, *_m2_vectors(dt_bias, a_log, d_skip, norm_g), e, ecol, sprev, dy)
    dxbc, dcw, dcb = _conv_bwd("m2_dconv", p, M2_INNER, conv_w, conv_b.reshape(1, M2_CONV_CH),
                               jnp.concatenate([dx, db, dc], axis=1))
    dp = jnp.concatenate([dz, dxbc, ddt], axis=1)
    dw_in = _mm("m2_dwin", hn, dp, "tn", (F32,))[:, :M2_IN]
    dhn = _mm("m2_dhn", dp, w_in, "nt", (F32,))
    dh, dg = _rms_bwd("m2_dnorm", h, g, dhn, dh_out)
    grads = dict(w_in=dw_in, conv_w=dcw, conv_b=dcb[0], dt_bias=dnar[0, :M2_HEADS], a_log=dnar[1, :M2_HEADS],
                 d=jnp.sum(dwide[:, 0, :].reshape(M2_HEADS, M2_HD), axis=-1), norm_g=dwide[:, 1, :].reshape(M2_INNER),
                 w_out=dw_out)
    return dh, dg[0], grads


def _s5_selector():
    e16 = np.zeros((S5_STATE, S5_STATE * S5_GROUP), np.float32)
    for p in range(S5_STATE):
        e16[p, p * S5_GROUP:(p + 1) * S5_GROUP] = 1.0
    return jnp.asarray(e16)


def _s5_operands(lbr, lbi, bbr, bbi, c_re, c_im):
    eye = jnp.eye(S5_BLOCKS, dtype=F32)
    gpb = S5_GROUPS // S5_BLOCKS
    bd = lambda t: jnp.einsum("jgpk,gh->jgkhp", t.reshape(S5_BLOCKS, gpb, S5_STATE, S5_GROUP), eye).reshape(S5_BLOCKS, LANES, S5_NS)
    cd = lambda t: jnp.einsum("jgkp,gh->jgphk", t.reshape(S5_BLOCKS, gpb, S5_GROUP, S5_STATE), eye).reshape(S5_BLOCKS, S5_NS, LANES)
    bmat = jnp.concatenate([bd(bbr), bd(bbi)], axis=2).astype(BF16)
    cmat = jnp.concatenate([cd(c_re), -cd(c_im)], axis=1).astype(BF16)
    ar, ai = lbr.reshape(S5_BLOCKS, S5_NS), lbi.reshape(S5_BLOCKS, S5_NS)
    pr, pi = [ar], [ai]
    for _ in range(SUBLANES - 1):
        pr, pi = pr + [pr[-1] * ar - pi[-1] * ai], pi + [pr[-1] * ai + pi[-1] * ar]
    return bmat, cmat, jnp.stack(pr, axis=1), jnp.stack(pi, axis=1)


def _s5_fwd_layer(h, g, w_in, lam_re, lam_im, log_dt, b_re, b_im, c_re, c_im, d_skip, w_out):
    hn = _rms_fwd("s5_norm", h, g)
    u = _mm("s5_in", hn, w_in, "nn", (F32,))
    n = S5_STATE * S5_GROUP
    lbr, lbi, bbr, bbi = _s5_params_fwd(lam_re, lam_im, log_dt.reshape(S5_GROUPS, 1), b_re.reshape(S5_GROUPS, n),
                                        b_im.reshape(S5_GROUPS, n), _s5_selector())
    ops = _s5_operands(lbr, lbi, bbr, bbi, c_re, c_im)
    yg = _s5_fwd(u, *ops, d_skip.reshape(1, D_MODEL))
    ag = _mm("s5_out", yg, w_out, "nn", (F32,))
    return _glu_fwd(h, ag), (h, hn, u, ops, yg, ag)


def _s5_bwd_layer(dh_out, saved, g, w_in, lam_re, lam_im, log_dt, b_re, b_im, c_re, c_im, d_skip, w_out):
    h, hn, u, ops, yg, ag = saved
    dag = _glu_bwd(dh_out, ag)
    dw_out = _mm("s5_dwout", yg, dag, "tn", (F32,))
    dyg = _mm("s5_dyg", dag, w_out, "nt", (F32,))
    du, dbmat, dcmat, dlam, ddsk = _s5_bwd(u, *ops, d_skip.reshape(1, D_MODEL), dyg)
    eye = jnp.eye(S5_BLOCKS, dtype=F32)
    gpb = S5_GROUPS // S5_BLOCKS
    n = S5_STATE * S5_GROUP
    ub = lambda t: jnp.einsum("jgkhp,gh->jgpk", t.reshape(S5_BLOCKS, gpb, S5_GROUP, gpb, S5_STATE), eye).reshape(S5_GROUPS, n)
    uc = lambda t: jnp.einsum("jgphk,gh->jgkp", t.reshape(S5_BLOCKS, gpb, S5_STATE, gpb, S5_GROUP), eye).reshape(c_re.shape)
    cts = (dlam[:, 0, :].reshape(S5_GROUPS, S5_STATE), dlam[:, 1, :].reshape(S5_GROUPS, S5_STATE),
           ub(dbmat[:, :, :S5_NS]), ub(dbmat[:, :, S5_NS:]))
    dlr, dli, dld, dbr, dbi = _s5_params_bwd(lam_re, lam_im, log_dt.reshape(S5_GROUPS, 1), b_re.reshape(S5_GROUPS, n),
                                             b_im.reshape(S5_GROUPS, n), _s5_selector(), cts)
    dw_in = _mm("s5_dwin", hn, du, "tn", (F32,))
    dhn = _mm("s5_dhn", du, w_in, "nt", (F32,))
    dh, dg = _rms_bwd("s5_dnorm", h, g, dhn, dh_out)
    grads = dict(w_in=dw_in, lam_re=dlr, lam_im=dli, log_dt=dld[:, 0], b_re=dbr.reshape(b_re.shape), b_im=dbi.reshape(b_im.shape),
                 c_re=uc(dcmat[:, :S5_NS, :]), c_im=-uc(dcmat[:, S5_NS:, :]), d=ddsk[0], w_out=dw_out)
    return dh, dg[0], grads


MIXER_OF_LAYER = ("gdn", "s5", "m2", "gdn")
MIXER_INDEX = (0, 0, 0, 1)


def _mixer_args(W, i):
    kind, j = MIXER_OF_LAYER[i], MIXER_INDEX[i]
    if kind == "gdn":
        return tuple(W["gdn_" + k][j] for k in ("w_in", "conv_w", "a_log", "dt_bias", "o_norm_g", "w_out"))
    if kind == "s5":
        return tuple(W["s5_" + k][j] for k in ("w_in", "lam_re", "lam_im", "log_dt", "b_re", "b_im", "c_re", "c_im", "d", "w_out"))
    return tuple(W["m2_" + k][j] for k in ("w_in", "conv_w", "conv_b", "dt_bias", "a_log", "d", "norm_g", "w_out"))


def _local_step(x, target, W):
    h = x
    saved = []
    for i in range(DEPTH):
        kind = MIXER_OF_LAYER[i]
        args = _mixer_args(W, i)
        if kind == "gdn":
            h, sm = _gdn_fwd_layer(i, h, W["norm_mix_g"][i], *args)
        elif kind == "s5":
            h, sm = _s5_fwd_layer(h, W["norm_mix_g"][i], *args)
        else:
            h, sm = _m2_fwd_layer(h, W["norm_mix_g"][i], *args)
        h, sp = _mlp_fwd(i, h, W["norm_mlp_g"][i], W["mlp_w1"][i], W["mlp_w2"][i])
        saved.append((sm, sp))
    loss, dh, dgf = _loss_head(h, W["final_norm_g"], target)
    G = {"final_norm_g": dgf[0], "norm_mix_g": [None] * DEPTH, "norm_mlp_g": [None] * DEPTH,
         "mlp_w1": [None] * DEPTH, "mlp_w2": [None] * DEPTH}
    mix = {}
    for i in reversed(range(DEPTH)):
        kind = MIXER_OF_LAYER[i]
        sm, sp = saved[i]
        dh, G["norm_mlp_g"][i], G["mlp_w1"][i], G["mlp_w2"][i] = _mlp_bwd(i, dh, sp, W["norm_mlp_g"][i], W["mlp_w1"][i], W["mlp_w2"][i])
        args = _mixer_args(W, i)
        if kind == "gdn":
            dh, G["norm_mix_g"][i], gm = _gdn_bwd_layer(i, dh, sm, W["norm_mix_g"][i], *args)
        elif kind == "s5":
            dh, G["norm_mix_g"][i], gm = _s5_bwd_layer(dh, sm, W["norm_mix_g"][i], *args)
        else:
            dh, G["norm_mix_g"][i], gm = _m2_bwd_layer(dh, sm, W["norm_mix_g"][i], *args)
        for k, v in gm.items():
            mix.setdefault(kind + "_" + k, {})[MIXER_INDEX[i]] = v
    for k, d in mix.items():
        G[k] = [d[j] for j in sorted(d)]
    return loss, dh, {k: (jnp.stack(v) if isinstance(v, list) else v) for k, v in G.items()}


ADAM_ROWS = 128


def _adamw(name, w, g, m, v):
    R, C = w.shape
    tr = _tile(R, (ADAM_ROWS, SUBLANES))

    def body(w_ref, g_ref, m_ref, v_ref, d_ref, mo_ref, vo_ref):
        gg = g_ref[...]
        mn = ADAM_B1 * m_ref[...] + (1.0 - ADAM_B1) * gg
        vn = ADAM_B2 * v_ref[...] + (1.0 - ADAM_B2) * (gg * gg)
        m_hat = mn / (1.0 - ADAM_B1 ** ADAM_STEP)
        v_hat = vn / (1.0 - ADAM_B2 ** ADAM_STEP)
        d_ref[...] = -ADAM_LR * (m_hat / (jnp.sqrt(v_hat) + ADAM_EPS) + ADAM_WD * w_ref[...])
        mo_ref[...] = mn
        vo_ref[...] = vn

    blk = pl.BlockSpec((tr, C), lambda i: (i, 0))
    return pl.pallas_call(
        body, name=name, grid=(R // tr,), in_specs=[blk] * 4, out_specs=[blk] * 3,
        out_shape=[jax.ShapeDtypeStruct((R, C), F32)] * 3, compiler_params=_params("parallel"),
    )(w, g, m, v)


MESH = pl.DeviceIdType.MESH
ANY = pl.BlockSpec(memory_space=pl.ANY)
N_CHIPS = 4
N_DEV = 8


def _position():
    return lax.axis_index("x"), lax.axis_index("y"), lax.axis_index("c")


def _gather_shards(wp):
    R, C = wp.shape
    half = R // 2

    def body(w_ref, out_ref, send_sems, recv_sems):
        x, y, c = _position()
        sibling = (x, y, 1 - c)
        chips = [(1 - x, y), (x, 1 - y), (1 - x, 1 - y)]

        def piece(cx, cy, hc):
            return out_ref.at[2 * cx + cy, pl.ds(hc * half, half), :]

        def copy(k, src, dst, to):
            return pltpu.make_async_remote_copy(src_ref=src, dst_ref=dst, send_sem=send_sems.at[k], recv_sem=recv_sems.at[k],
                                                device_id=to, device_id_type=MESH)

        first = [copy(j, w_ref.at[pl.ds(c * half, half), :], piece(x, y, c), (*chip, c)) for j, chip in enumerate(chips)]
        for cp in first:
            cp.start()
        passed = [copy(3 + j, piece(*chip, c), piece(*chip, c), sibling) for j, chip in enumerate(chips)]
        for j, chip in enumerate(chips):
            copy(j, piece(*chip, c), piece(*chip, c), sibling).wait_recv()
            passed[j].start()
        for j, chip in enumerate(chips):
            copy(3 + j, piece(*chip, 1 - c), piece(*chip, 1 - c), sibling).wait_recv()
        for cp in first + passed:
            cp.wait_send()

    return pl.pallas_call(
        body, name="gather_shards", in_specs=[ANY], out_specs=ANY,
        out_shape=jax.ShapeDtypeStruct((N_CHIPS, R, C), wp.dtype),
        scratch_shapes=[pltpu.SemaphoreType.DMA((6,)), pltpu.SemaphoreType.DMA((6,))],
    )(wp)


def _pair_exchange(gp):
    n, R, C = gp.shape
    half = R // 2

    def body(g_ref, out_ref, send_sems, recv_sems):
        x, y, c = _position()
        copies = [pltpu.make_async_remote_copy(
            src_ref=g_ref.at[k, pl.ds((1 - c) * half, half), :], dst_ref=out_ref.at[k], send_sem=send_sems.at[k],
            recv_sem=recv_sems.at[k], device_id=(x, y, 1 - c), device_id_type=MESH) for k in range(n)]
        for cp in copies:
            cp.start()
        for cp in copies:
            cp.wait()

    return pl.pallas_call(
        body, name="pair_exchange", in_specs=[ANY], out_specs=ANY,
        out_shape=jax.ShapeDtypeStruct((n, half, C), gp.dtype),
        scratch_shapes=[pltpu.SemaphoreType.DMA((n,)), pltpu.SemaphoreType.DMA((n,))],
    )(gp)


SUM_ROWS = 256


def _pair_sum(gp, got, core):
    n, R, C = gp.shape
    half = R // 2
    nb = half // SUM_ROWS

    def body(core_ref, g_ref, r_ref, o_ref):
        o_ref[...] = (g_ref[...].astype(F32) + r_ref[...].astype(F32)).astype(o_ref.dtype)

    return pl.pallas_call(
        body, name="pair_sum",
        grid_spec=pltpu.PrefetchScalarGridSpec(
            num_scalar_prefetch=1, grid=(n, nb),
            in_specs=[pl.BlockSpec((None, SUM_ROWS, C), lambda k, i, core_ref: (k, core_ref[0] * nb + i, 0)),
                      pl.BlockSpec((None, SUM_ROWS, C), lambda k, i, core_ref: (k, i, 0))],
            out_specs=pl.BlockSpec((None, SUM_ROWS, C), lambda k, i, core_ref: (k, i, 0))),
        out_shape=jax.ShapeDtypeStruct((n, half, C), gp.dtype), compiler_params=_params("parallel", "parallel"),
    )(core, gp, got)


def _chip_exchange(t):
    n, H, C = t.shape

    def body(t_ref, out_ref, send_sems, recv_sems):
        x, y, c = _position()
        chips = [(1 - x, y), (x, 1 - y), (1 - x, 1 - y)]
        copies = [pltpu.make_async_remote_copy(
            src_ref=t_ref.at[2 * cx + cy], dst_ref=out_ref.at[2 * x + y], send_sem=send_sems.at[j], recv_sem=recv_sems.at[j],
            device_id=(cx, cy, c), device_id_type=MESH) for j, (cx, cy) in enumerate(chips)]
        for cp in copies:
            cp.start()
        for j, (cx, cy) in enumerate(chips):
            pltpu.make_async_remote_copy(
                src_ref=t_ref.at[2 * cx + cy], dst_ref=out_ref.at[2 * cx + cy], send_sem=send_sems.at[j],
                recv_sem=recv_sems.at[j], device_id=(cx, cy, c), device_id_type=MESH).wait_recv()
        for cp in copies:
            cp.wait_send()

    return pl.pallas_call(
        body, name="chip_exchange", in_specs=[ANY], out_specs=ANY,
        out_shape=jax.ShapeDtypeStruct((n, H, C), t.dtype),
        scratch_shapes=[pltpu.SemaphoreType.DMA((n - 1,)), pltpu.SemaphoreType.DMA((n - 1,))],
    )(t)


def _chip_sum(t, got, ids):
    n, H, C = t.shape
    nb = H // SUM_ROWS

    def body(ids_ref, t_ref, r_ref, o_ref):
        own = t_ref[...].astype(F32)
        acc = jnp.where(ids_ref[0] == 0, own, r_ref[0].astype(F32))
        for k in range(1, n):
            acc = acc + jnp.where(ids_ref[0] == k, own, r_ref[k].astype(F32))
        o_ref[...] = acc

    return pl.pallas_call(
        body, name="chip_sum",
        grid_spec=pltpu.PrefetchScalarGridSpec(
            num_scalar_prefetch=1, grid=(nb,),
            in_specs=[pl.BlockSpec((None, SUM_ROWS, C), lambda i, ids_ref: (ids_ref[0], i, 0)),
                      pl.BlockSpec((n, SUM_ROWS, C), lambda i, ids_ref: (0, i, 0))],
            out_specs=pl.BlockSpec((SUM_ROWS, C), lambda i, ids_ref: (ids_ref[1] * nb + i, 0))),
        out_shape=jax.ShapeDtypeStruct((2 * H, C), F32), compiler_params=_params("parallel"),
    )(ids, t, got)


def _sum_pieces(name, pieces):
    n, R, C = pieces.shape
    tr = _tile(R, (256, 128, SUBLANES))

    def body(p_ref, o_ref):
        acc = p_ref[0].astype(F32)
        for s in range(1, n):
            acc = acc + p_ref[s].astype(F32)
        o_ref[...] = acc

    return pl.pallas_call(
        body, name=name, grid=(R // tr,),
        in_specs=[pl.BlockSpec((n, tr, C), lambda i: (0, i, 0))], out_specs=pl.BlockSpec((tr, C), lambda i: (i, 0)),
        out_shape=jax.ShapeDtypeStruct((R, C), F32), compiler_params=_params("parallel"),
    )(pieces)


def _swap_halves(s):
    R, C = s.shape
    half = R // 2

    def body(s_ref, out_ref, send_sem, recv_sem):
        x, y, c = _position()
        cp = pltpu.make_async_remote_copy(src_ref=s_ref.at[pl.ds(c * half, half), :], dst_ref=out_ref.at[pl.ds(c * half, half), :],
                                          send_sem=send_sem, recv_sem=recv_sem, device_id=(x, y, 1 - c), device_id_type=MESH)
        cp.start()
        pltpu.make_async_remote_copy(src_ref=s_ref.at[pl.ds(c * half, half), :], dst_ref=out_ref.at[pl.ds((1 - c) * half, half), :],
                                     send_sem=send_sem, recv_sem=recv_sem, device_id=(x, y, 1 - c), device_id_type=MESH).wait_recv()
        cp.wait_send()

    return pl.pallas_call(
        body, name="swap_halves", in_specs=[ANY], out_specs=ANY, input_output_aliases={0: 0},
        out_shape=jax.ShapeDtypeStruct((R, C), s.dtype),
        scratch_shapes=[pltpu.SemaphoreType.DMA, pltpu.SemaphoreType.DMA],
    )(s)


def _gather_small(name, blk):
    m_per, n = blk.shape

    def body(x_ref, out_ref, send_sems, recv_sems, local_sem):
        x, y, c = _position()
        me, sibling = (x, y, c), (x, y, 1 - c)
        chips = [(1 - x, y), (x, 1 - y), (1 - x, 1 - y)]

        def rows(px, py, pc):
            return out_ref.at[pl.ds((4 * px + 2 * py + pc) * m_per, m_per), :]

        def copy(k, block, to, src=None):
            return pltpu.make_async_remote_copy(src_ref=rows(*block) if src is None else src, dst_ref=rows(*block),
                                                send_sem=send_sems.at[k], recv_sem=recv_sems.at[k], device_id=to, device_id_type=MESH)

        mine = pltpu.make_async_copy(x_ref, rows(*me), local_sem)
        mine.start()
        first = [copy(0, me, sibling, src=x_ref)] + [copy(1 + j, me, (*chip, c), src=x_ref) for j, chip in enumerate(chips)]
        for cp in first:
            cp.start()
        passed = [copy(4 + j, (*chip, c), sibling) for j, chip in enumerate(chips)]
        for j, chip in enumerate(chips):
            copy(1 + j, (*chip, c), me).wait_recv()
            passed[j].start()
        copy(0, sibling, me).wait_recv()
        for j, chip in enumerate(chips):
            copy(4 + j, (*chip, 1 - c), me).wait_recv()
        for cp in first + passed:
            cp.wait_send()
        mine.wait()

    return pl.pallas_call(
        body, name=name, out_shape=jax.ShapeDtypeStruct((N_DEV * m_per, n), blk.dtype),
        in_specs=[pl.BlockSpec(memory_space=pltpu.VMEM)], out_specs=pl.BlockSpec(memory_space=pltpu.VMEM),
        scratch_shapes=[pltpu.SemaphoreType.DMA((7,)), pltpu.SemaphoreType.DMA((7,)), pltpu.SemaphoreType.DMA],
        compiler_params=pltpu.CompilerParams(vmem_limit_bytes=VMEM_LIMIT_BYTES),
    )(blk)


WEIGHTS = ("norm_mix_g", "norm_mlp_g", "mlp_w1", "mlp_w2", "gdn_w_in", "gdn_conv_w", "gdn_a_log", "gdn_dt_bias", "gdn_o_norm_g",
           "gdn_w_out", "s5_w_in", "s5_lam_re", "s5_lam_im", "s5_log_dt", "s5_b_re", "s5_b_im", "s5_c_re", "s5_c_im", "s5_d",
           "s5_w_out", "m2_w_in", "m2_conv_w", "m2_conv_b", "m2_dt_bias", "m2_a_log", "m2_d", "m2_norm_g", "m2_w_out",
           "final_norm_g")
BIG = {"mlp_w1": 2, "mlp_w2": 1, "gdn_w_in": 2, "gdn_w_out": 1, "s5_w_in": 1, "s5_w_out": 2, "m2_w_in": 2, "m2_w_out": 1}
SMALL_CUT = {"gdn_conv_w": 2, "m2_conv_w": 2, "m2_conv_b": 1, "m2_norm_g": 1}
PACK_COLS = 1024
PACK_ROW_MULTIPLE = 512


def _pack(arrays, cols, row_multiple, dtype):
    flat = jnp.concatenate([a.reshape(-1).astype(dtype) for a in arrays])
    n = -(-flat.shape[0] // (cols * row_multiple)) * cols * row_multiple
    return jnp.pad(flat, (0, n - flat.shape[0])).reshape(-1, cols)


def _unpack(packed, shapes):
    flat = packed.reshape(-1)
    out, off = [], 0
    for shp in shapes:
        n = math.prod(shp)
        out.append(flat[off:off + n].reshape(shp))
        off += n
    return out


def _cut(a, axis, k):
    n = a.shape[axis] // N_CHIPS
    return lax.slice_in_dim(a, k * n, (k + 1) * n, axis=axis)


def kernel(x, norm_mix_g, norm_mlp_g, mlp_w1, mlp_w2, gdn_w_in, gdn_conv_w, gdn_a_log, gdn_dt_bias, gdn_o_norm_g, gdn_w_out, s5_w_in, s5_lam_re, s5_lam_im, s5_log_dt, s5_b_re, s5_b_im, s5_c_re, s5_c_im, s5_d, s5_w_out, m2_w_in, m2_conv_w, m2_conv_b, m2_dt_bias, m2_a_log, m2_d, m2_norm_g, m2_w_out, final_norm_g, loss_target, m_norm_mix_g, m_norm_mlp_g, m_mlp_w1, m_mlp_w2, m_gdn_w_in, m_gdn_conv_w, m_gdn_a_log, m_gdn_dt_bias, m_gdn_o_norm_g, m_gdn_w_out, m_s5_w_in, m_s5_lam_re, m_s5_lam_im, m_s5_log_dt, m_s5_b_re, m_s5_b_im, m_s5_c_re, m_s5_c_im, m_s5_d, m_s5_w_out, m_m2_w_in, m_m2_conv_w, m_m2_conv_b, m_m2_dt_bias, m_m2_a_log, m_m2_d, m_m2_norm_g, m_m2_w_out, m_final_norm_g, v_norm_mix_g, v_norm_mlp_g, v_mlp_w1, v_mlp_w2, v_gdn_w_in, v_gdn_conv_w, v_gdn_a_log, v_gdn_dt_bias, v_gdn_o_norm_g, v_gdn_w_out, v_s5_w_in, v_s5_lam_re, v_s5_lam_im, v_s5_log_dt, v_s5_b_re, v_s5_b_im, v_s5_c_re, v_s5_c_im, v_s5_d, v_s5_w_out, v_m2_w_in, v_m2_conv_w, v_m2_conv_b, v_m2_dt_bias, v_m2_a_log, v_m2_d, v_m2_norm_g, v_m2_w_out, v_final_norm_g):
    given = dict(locals())
    w = {n: given[n] for n in WEIGHTS}
    mom = {n: given["m_" + n] for n in WEIGHTS}
    var = {n: given["v_" + n] for n in WEIGHTS}
    big, small_cut = tuple(BIG), tuple(SMALL_CUT)
    small = tuple(n for n in WEIGHTS if n not in BIG)
    chip = 2 * lax.axis_index("x") + lax.axis_index("y")

    shards = _gather_shards(_pack([w[n] for n in big], PACK_COLS, PACK_ROW_MULTIPLE, BF16))
    own = [w[n].astype(BF16) for n in big]
    per_chip = [[jnp.where(chip == k, o, p) for o, p in zip(own, _unpack(shards[k], [w[n].shape for n in big]))]
                for k in range(N_CHIPS)]
    W = {n: jnp.concatenate([per_chip[k][i] for k in range(N_CHIPS)], axis=BIG[n]) for i, n in enumerate(big)}
    W["gdn_w_in"] = jnp.pad(W["gdn_w_in"], ((0, 0), (0, 0), (0, GDN_IN_PAD - GDN_IN)))
    W["m2_w_in"] = jnp.pad(W["m2_w_in"], ((0, 0), (0, 0), (0, M2_IN_PAD - M2_IN)))
    cut_blk = _pack([w[n] for n in small_cut], LANES, SUBLANES, F32)
    cut_all = _gather_small("gather_small_params", cut_blk).reshape(N_DEV, *cut_blk.shape)
    per_chip = [_unpack(cut_all[2 * k], [w[n].shape for n in small_cut]) for k in range(N_CHIPS)]
    W.update({n: jnp.concatenate([per_chip[k][i] for k in range(N_CHIPS)], axis=SMALL_CUT[n]) for i, n in enumerate(small_cut)})
    W.update({n: w[n] for n in small if n not in SMALL_CUT})

    loss, grad_x, G = _local_step(x[0], loss_target[0], W)
    loss = lax.psum(loss[0, 0], ("x", "y", "c"))

    gp = jnp.stack([_pack([_cut(G[n], BIG[n], k) for n in big], PACK_COLS, PACK_ROW_MULTIPLE, BF16) for k in range(N_CHIPS)])
    core = lax.axis_index("c").astype(jnp.int32)
    pair = _pair_sum(gp, _pair_exchange(gp), core.reshape(1))
    g_shard = _swap_halves(_chip_sum(pair, _chip_exchange(pair), jnp.stack([chip.astype(jnp.int32), core])))
    grads = dict(zip(big, _unpack(g_shard, [w[n].shape for n in big])))
    sg = _pack([G[n] for n in small], LANES, ADAM_ROWS, F32)
    sg_sum = _sum_pieces("sum_small_grads", _gather_small("gather_small_grads", sg).reshape(N_DEV, *sg.shape))
    for n, g in zip(small, _unpack(sg_sum, [G[n].shape for n in small])):
        if n in SMALL_CUT:
            width = g.shape[SMALL_CUT[n]] // N_CHIPS
            g = lax.dynamic_slice_in_dim(g, chip * width, width, axis=SMALL_CUT[n])
        grads[n] = g.reshape(w[n].shape)

    delta, new_m, new_v = {}, {}, {}
    for n in big:
        as2d = lambda a: a.reshape(-1, a.shape[-1])
        outs = _adamw("adamw_" + n, as2d(w[n]), as2d(grads[n]), as2d(mom[n]), as2d(var[n]))
        delta[n], new_m[n], new_v[n] = (o.reshape(w[n].shape) for o in outs)
    packs = [_pack([t[n] for n in small], LANES, ADAM_ROWS, F32) for t in (w, grads, mom, var)]
    outs = _adamw("adamw_small", *packs)
    for t, o in zip((delta, new_m, new_v), outs):
        t.update(zip(small, _unpack(o, [w[n].shape for n in small])))

    return (loss, grad_x[None], *[grads[n] for n in WEIGHTS], *[delta[n] for n in WEIGHTS], *[new_m[n] for n in WEIGHTS],
            *[new_v[n] for n in WEIGHTS])
```

```python
import functools
import math

import numpy as np
import jax
import jax.numpy as jnp
from jax import lax
from jax.experimental import pallas as pl
from jax.experimental.pallas import tpu as pltpu

F32 = jnp.float32
BF16 = jnp.bfloat16

D_MODEL = 1024
D_FF = 4096
DEPTH = 4
CHUNK = 64
RMS_EPS = 1e-6
CONV_W = 4
GDN_HEADS = 8
GDN_DK = 128
GDN_IN = 4112
GDN_IN_PAD = 4224
S5_GROUPS = 64
S5_STATE = 64
S5_GROUP = 16
S5_BLOCKS = 8
M2_INNER = 2048
M2_HEADS = 32
M2_GROUPS = 8
M2_STATE = 128
M2_CONV_CH = 4096
M2_IN = 6176
M2_IN_PAD = 6272
ADAM_LR, ADAM_B1, ADAM_B2, ADAM_EPS, ADAM_WD, ADAM_STEP = 0.001, 0.9, 0.999, 1e-08, 0.01, 10

VMEM_LIMIT_BYTES = 56 * 1024 * 1024
SUBLANES = 8
LANES = 128


def _params(*sem):
    return pltpu.CompilerParams(dimension_semantics=tuple(sem) if sem else None, vmem_limit_bytes=VMEM_LIMIT_BYTES)


NN, NT, TN = ((1,), (0,)), ((1,), (1,)), ((0,), (0,))
_DOT_TRANSPOSES = {NN: ((NT, "gb"), (TN, "ag")), NT: ((NN, "gb"), (TN, "ga")), TN: ((NT, "bg"), (NN, "ag"))}


def _dg(a, b, dims):
    if a.ndim == 3:
        dn = (((dims[0][0] + 1,), (dims[1][0] + 1,)), ((0,), (0,)))
    else:
        dn = (dims, ((), ()))
    return lax.dot_general(a, b, dn, preferred_element_type=F32)


def _mxu(a, b, dims):
    return _dg(a.astype(BF16), b.astype(BF16), dims)


@functools.partial(jax.custom_vjp, nondiff_argnums=(2,))
def _dot(a, b, dims=NN):
    return _mxu(a, b, dims)


def _dot_fwd(a, b, dims):
    return _mxu(a, b, dims), (a, b)


def _dot_bwd(dims, res, g):
    ops = dict(a=res[0], b=res[1], g=g)
    (da_dims, da_ops), (db_dims, db_ops) = _DOT_TRANSPOSES[dims]
    return (_mxu(ops[da_ops[0]], ops[da_ops[1]], da_dims).astype(res[0].dtype),
            _mxu(ops[db_ops[0]], ops[db_ops[1]], db_dims).astype(res[1].dtype))


_dot.defvjp(_dot_fwd, _dot_bwd)


def _nt(a, b):
    return _dot(a, b, NT)


def _tn(a, b):
    return _dot(a, b, TN)


def _split3(x):
    x1 = x.astype(BF16)
    r = x - x1.astype(F32)
    x2 = r.astype(BF16)
    return x1, x2, (r - x2.astype(F32)).astype(BF16)


def _sel_mxu(x, sel, dims, x_first):
    f = (lambda p: _dg(p, sel.astype(BF16), dims)) if x_first else (lambda p: _dg(sel.astype(BF16), p, dims))
    x1, x2, x3 = _split3(x)
    return f(x1) + (f(x2) + f(x3))


@jax.custom_vjp
def _pick(x, sel):
    return _sel_mxu(x, sel, NN, True)


def _pick_fwd(x, sel):
    return _sel_mxu(x, sel, NN, True), sel


def _pick_bwd(sel, g):
    return _sel_mxu(g, sel, NT, True), jnp.zeros_like(sel)


_pick.defvjp(_pick_fwd, _pick_bwd)


@jax.custom_vjp
def _accum(sel, x):
    return _sel_mxu(x, sel, NN, False)


def _accum_fwd(sel, x):
    return _sel_mxu(x, sel, NN, False), sel


def _accum_bwd(sel, g):
    return jnp.zeros_like(sel), _sel_mxu(g, sel, TN, False)


_accum.defvjp(_accum_fwd, _accum_bwd)


def _dot3(a, b, dims=NN):
    ah, bh = a.astype(BF16), b.astype(BF16)
    al, bl = (a - ah.astype(F32)).astype(BF16), (b - bh.astype(F32)).astype(BF16)
    return _dg(ah, bh, dims) + (_dg(ah, bl, dims) + _dg(al, bh, dims))


def _neumann(x, r, dims):
    r = r + _dot3(x, r, dims)
    for _ in range(5):
        x = _dot3(x, x)
        r = r + _dot3(x, r, dims)
    return r


@jax.custom_vjp
def _unit_lower_solve(a, rhs):
    return _neumann(-a, rhs, NN)


def _unit_lower_solve_fwd(a, rhs):
    sol = _neumann(-a, rhs, NN)
    return sol, (a, sol)


def _unit_lower_solve_bwd(res, ct):
    a, sol = res
    d_rhs = _neumann(-a, ct, TN)
    return -_dot3(d_rhs, sol, NT), d_rhs


_unit_lower_solve.defvjp(_unit_lower_solve_fwd, _unit_lower_solve_bwd)


def _sigmoid(x):
    return 1.0 / (1.0 + jnp.exp(-x))


def _softplus(x):
    return jnp.maximum(x, 0.0) + jnp.log(1.0 + jnp.exp(-jnp.abs(x)))


def _iota2(shape, axis):
    return lax.broadcasted_iota(jnp.int32, shape, axis)


def _tile(n, cands):
    for c in cands:
        if n % c == 0:
            return c
    return n


MM_TILE_BYTES = 9 * 1024 * 1024


def _mm(name, a, b, mode, out_dtypes, epi=None, extras=(), tn=None):
    if mode == "nn":
        (M, K), N = a.shape, b.shape[1]
    elif mode == "nt":
        (M, K), N = a.shape, b.shape[0]
    else:
        (K, M), N = a.shape, b.shape[1]
    tn = tn or _tile(N, (512, 384, 896, 256, 128))
    out_bytes = tn * (sum(jnp.dtype(d).itemsize for d in out_dtypes) + sum(e.dtype.itemsize for e in extras))
    fits = lambda t: t * K * a.dtype.itemsize <= MM_TILE_BYTES and t * out_bytes <= MM_TILE_BYTES
    tm = next(t for t in (2048, 1024, 512, 256, 128) if M % t == 0 and (fits(t) or t == 128))
    if mode == "nn":
        a_spec, b_spec = pl.BlockSpec((tm, K), lambda i, j: (i, 0)), pl.BlockSpec((K, tn), lambda i, j: (0, j))
        dims = NN
    elif mode == "nt":
        a_spec, b_spec = pl.BlockSpec((tm, K), lambda i, j: (i, 0)), pl.BlockSpec((tn, K), lambda i, j: (j, 0))
        dims = NT
    else:
        a_spec, b_spec = pl.BlockSpec((K, tm), lambda i, j: (0, i)), pl.BlockSpec((K, tn), lambda i, j: (0, j))
        dims = TN
    n_ex = len(extras)

    def body(a_ref, b_ref, *rest):
        acc = _mxu(a_ref[...], b_ref[...], dims)
        res = epi(acc, *[e[...] for e in rest[:n_ex]]) if epi is not None else (acc,)
        for o_ref, r in zip(rest[n_ex:], res):
            o_ref[...] = r.astype(o_ref.dtype)

    tile = pl.BlockSpec((tm, tn), lambda i, j: (i, j))
    out = pl.pallas_call(
        body, name=name, grid=(M // tm, N // tn),
        in_specs=[a_spec, b_spec] + [tile] * n_ex,
        out_specs=[tile] * len(out_dtypes),
        out_shape=[jax.ShapeDtypeStruct((M, N), d) for d in out_dtypes],
        compiler_params=_params("parallel", "parallel"),
    )(a, b, *extras)
    return out if len(out_dtypes) > 1 else out[0]


def _rms_fwd(name, h, g):
    L, D = h.shape
    tr = _tile(L, (256, 128))

    def body(h_ref, g_ref, o_ref):
        x = h_ref[...]
        r = lax.rsqrt(jnp.mean(x * x, axis=-1, keepdims=True) + RMS_EPS)
        o_ref[...] = (x * r * g_ref[...]).astype(o_ref.dtype)

    return pl.pallas_call(
        body, name=name, grid=(L // tr,),
        in_specs=[pl.BlockSpec((tr, D), lambda i: (i, 0)), pl.BlockSpec((1, D), lambda i: (0, 0))],
        out_specs=pl.BlockSpec((tr, D), lambda i: (i, 0)),
        out_shape=jax.ShapeDtypeStruct((L, D), BF16),
        compiler_params=_params("parallel"),
    )(h, g.reshape(1, D))


def _rms_bwd(name, h, g, dhn, dres):
    L, D = h.shape
    tr = _tile(L, (256, 128))

    def body(h_ref, g_ref, dhn_ref, dres_ref, dh_ref, dg_ref):
        x = h_ref[...]
        r = lax.rsqrt(jnp.mean(x * x, axis=-1, keepdims=True) + RMS_EPS)
        xh = x * r
        dy = dhn_ref[...]
        dxh = dy * g_ref[...]
        dh_ref[...] = dres_ref[...] + r * (dxh - xh * jnp.mean(dxh * xh, axis=-1, keepdims=True))

        @pl.when(pl.program_id(0) == 0)
        def _():
            dg_ref[...] = jnp.zeros_like(dg_ref)

        dg_ref[...] += jnp.sum(dy * xh, axis=0, keepdims=True)

    row = pl.BlockSpec((tr, D), lambda i: (i, 0))
    vec = pl.BlockSpec((1, D), lambda i: (0, 0))
    return pl.pallas_call(
        body, name=name, grid=(L // tr,),
        in_specs=[row, vec, row, row], out_specs=[row, vec],
        out_shape=[jax.ShapeDtypeStruct((L, D), F32), jax.ShapeDtypeStruct((1, D), F32)],
        compiler_params=_params("arbitrary"),
    )(h, g.reshape(1, D), dhn, dres)


def _loss_head(h, g, target):
    L, D = h.shape
    tr = _tile(L, (256, 128))

    def body(h_ref, g_ref, t_ref, loss_ref, dh_ref, dg_ref):
        x = h_ref[...]
        r = lax.rsqrt(jnp.mean(x * x, axis=-1, keepdims=True) + RMS_EPS)
        xh = x * r
        err = xh * g_ref[...] - t_ref[...]
        dy = err * (1.0 / D)
        dxh = dy * g_ref[...]
        dh_ref[...] = r * (dxh - xh * jnp.mean(dxh * xh, axis=-1, keepdims=True))

        @pl.when(pl.program_id(0) == 0)
        def _():
            dg_ref[...] = jnp.zeros_like(dg_ref)
            loss_ref[...] = jnp.zeros_like(loss_ref)

        dg_ref[...] += jnp.sum(dy * xh, axis=0, keepdims=True)
        loss_ref[...] += (0.5 / D) * jnp.sum(jnp.sum(err * err, axis=-1, keepdims=True), axis=0, keepdims=True)

    row = pl.BlockSpec((tr, D), lambda i: (i, 0))
    vec = pl.BlockSpec((1, D), lambda i: (0, 0))
    return pl.pallas_call(
        body, name="loss_head", grid=(L // tr,),
        in_specs=[row, vec, row], out_specs=[pl.BlockSpec((1, 1), lambda i: (0, 0)), row, vec],
        out_shape=[jax.ShapeDtypeStruct((1, 1), F32), jax.ShapeDtypeStruct((L, D), F32), jax.ShapeDtypeStruct((1, D), F32)],
        compiler_params=_params("arbitrary"),
    )(h, g.reshape(1, D), target)


def _glu_fwd(h, ag):
    L, D = h.shape
    tr = _tile(L, (256, 128))

    def body(h_ref, v_ref, g_ref, o_ref):
        o_ref[...] = h_ref[...] + v_ref[...] * _sigmoid(g_ref[...])

    return pl.pallas_call(
        body, name="s5_glu_fwd", grid=(L // tr,),
        in_specs=[pl.BlockSpec((tr, D), lambda i: (i, 0)), pl.BlockSpec((tr, D), lambda i: (i, 0)),
                  pl.BlockSpec((tr, D), lambda i: (i, 1))],
        out_specs=pl.BlockSpec((tr, D), lambda i: (i, 0)),
        out_shape=jax.ShapeDtypeStruct((L, D), F32),
        compiler_params=_params("parallel"),
    )(h, ag, ag)


def _glu_bwd(dh, ag):
    L, D = dh.shape
    tr = _tile(L, (256, 128))

    def body(dh_ref, v_ref, g_ref, dv_ref, dg_ref):
        s = _sigmoid(g_ref[...])
        d = dh_ref[...]
        dv_ref[...] = d * s
        dg_ref[...] = d * v_ref[...] * s * (1.0 - s)

    dv, dg = pl.pallas_call(
        body, name="s5_glu_bwd", grid=(L // tr,),
        in_specs=[pl.BlockSpec((tr, D), lambda i: (i, 0)), pl.BlockSpec((tr, D), lambda i: (i, 0)),
                  pl.BlockSpec((tr, D), lambda i: (i, 1))],
        out_specs=[pl.BlockSpec((tr, D), lambda i: (i, 0))] * 2,
        out_shape=[jax.ShapeDtypeStruct((L, D), F32)] * 2,
        compiler_params=_params("parallel"),
    )(dh, ag, ag)
    return jnp.concatenate([dv, dg], axis=1)


CONV_ROWS = 128
CONV_COLS = 512


def _shift_rows(cat, s):
    if s == 0:
        return cat[SUBLANES:, :]
    return pltpu.roll(cat, s, axis=0)[SUBLANES:, :]


def _conv_fwd(name, p, col0, w, b):
    L = p.shape[0]
    C = w.shape[1]
    tc = _tile(C, (CONV_COLS, 256))
    cb0 = col0 // tc
    nr = L // CONV_ROWS

    def body(x_ref, w_ref, b_ref, o_ref):
        def step(r, carry):
            r0 = pl.multiple_of(r * CONV_ROWS, CONV_ROWS)
            cur = x_ref[pl.ds(r0, CONV_ROWS), :]
            p0 = pl.multiple_of(jnp.maximum(r0 - SUBLANES, 0), SUBLANES)
            prev = jnp.where(r > 0, x_ref[pl.ds(p0, SUBLANES), :], 0.0)
            cat = jnp.concatenate([prev, cur], axis=0)
            acc = b_ref[...] + w_ref[3:4, :] * cur
            for k in range(CONV_W - 1):
                acc = acc + w_ref[k:k + 1, :] * _shift_rows(cat, CONV_W - 1 - k)
            o_ref[pl.ds(r0, CONV_ROWS), :] = acc * _sigmoid(acc)
            return carry

        lax.fori_loop(0, nr, step, 0)

    return pl.pallas_call(
        body, name=name, grid=(C // tc,),
        in_specs=[pl.BlockSpec((L, tc), lambda j: (0, cb0 + j)), pl.BlockSpec((CONV_W, tc), lambda j: (0, j)),
                  pl.BlockSpec((1, tc), lambda j: (0, j))],
        out_specs=pl.BlockSpec((L, tc), lambda j: (0, j)),
        out_shape=jax.ShapeDtypeStruct((L, C), F32),
        compiler_params=_params("parallel"),
    )(p, w, b)


def _conv_bwd(name, p, col0, w, b, dout):
    L = p.shape[0]
    C = w.shape[1]
    tc = _tile(C, (CONV_COLS, 256))
    cb0 = col0 // tc
    nr = L // CONV_ROWS

    def body(x_ref, w_ref, b_ref, do_ref, dx_ref, dw_ref, db_ref, dpre_ref):
        def step1(r, carry):
            dw0, dw1, dw2, dw3, dbb = carry
            r0 = pl.multiple_of(r * CONV_ROWS, CONV_ROWS)
            cur = x_ref[pl.ds(r0, CONV_ROWS), :]
            p0 = pl.multiple_of(jnp.maximum(r0 - SUBLANES, 0), SUBLANES)
            prev = jnp.where(r > 0, x_ref[pl.ds(p0, SUBLANES), :], 0.0)
            cat = jnp.concatenate([prev, cur], axis=0)
            sh = [_shift_rows(cat, CONV_W - 1 - k) for k in range(CONV_W - 1)] + [cur]
            acc = b_ref[...] + w_ref[3:4, :] * cur
            for k in range(CONV_W - 1):
                acc = acc + w_ref[k:k + 1, :] * sh[k]
            sg = _sigmoid(acc)
            dpre = do_ref[pl.ds(r0, CONV_ROWS), :] * (sg + acc * sg * (1.0 - sg))
            dpre_ref[pl.ds(r0, CONV_ROWS), :] = dpre
            dws = [d + jnp.sum(dpre * s, axis=0, keepdims=True) for d, s in zip((dw0, dw1, dw2, dw3), sh)]
            return (*dws, dbb + jnp.sum(dpre, axis=0, keepdims=True))

        z = jnp.zeros((1, tc), F32)
        dw0, dw1, dw2, dw3, dbb = lax.fori_loop(0, nr, step1, (z, z, z, z, z))
        dw_ref[...] = jnp.concatenate([dw0, dw1, dw2, dw3, z, z, z, z], axis=0)
        db_ref[...] = dbb

        def step2(r, carry):
            r0 = pl.multiple_of(r * CONV_ROWS, CONV_ROWS)
            cur = dpre_ref[pl.ds(r0, CONV_ROWS), :]
            n0 = pl.multiple_of(jnp.minimum(r0 + CONV_ROWS, L - SUBLANES), SUBLANES)
            nxt = jnp.where(r < nr - 1, dpre_ref[pl.ds(n0, SUBLANES), :], 0.0)
            cat = jnp.concatenate([cur, nxt], axis=0)
            acc = w_ref[3:4, :] * cur
            for k in range(CONV_W - 1):
                s = CONV_W - 1 - k
                acc = acc + w_ref[k:k + 1, :] * pltpu.roll(cat, CONV_ROWS + SUBLANES - s, axis=0)[:CONV_ROWS, :]
            dx_ref[pl.ds(r0, CONV_ROWS), :] = acc
            return carry

        lax.fori_loop(0, nr, step2, 0)

    dx, dw, db = pl.pallas_call(
        body, name=name, grid=(C // tc,),
        in_specs=[pl.BlockSpec((L, tc), lambda j: (0, cb0 + j)), pl.BlockSpec((CONV_W, tc), lambda j: (0, j)),
                  pl.BlockSpec((1, tc), lambda j: (0, j)), pl.BlockSpec((L, tc), lambda j: (0, j))],
        out_specs=[pl.BlockSpec((L, tc), lambda j: (0, j)), pl.BlockSpec((SUBLANES, tc), lambda j: (0, j)),
                   pl.BlockSpec((1, tc), lambda j: (0, j))],
        out_shape=[jax.ShapeDtypeStruct((L, C), F32), jax.ShapeDtypeStruct((SUBLANES, C), F32),
                   jax.ShapeDtypeStruct((1, C), F32)],
        scratch_shapes=[pltpu.VMEM((L, tc), F32)],
        compiler_params=_params("parallel"),
    )(p, w, b, dout)
    return dx, dw[:CONV_W], db


def _chunk_consts():
    r, c = _iota2((CHUNK, CHUNK), 0), _iota2((CHUNK, CHUNK), 1)
    causal = r >= c
    return causal, r > c, (r == c).astype(F32), causal.astype(F32), jnp.ones((CHUNK, CHUNK), F32)


def _by_lanes(t):
    return jnp.concatenate([t[i] for i in range(t.shape[0])], axis=1)


def _by_batch(t, w):
    return jnp.concatenate([t[None, :, i * w:(i + 1) * w] for i in range(t.shape[1] // w)], axis=0)


def _diag_lanes():
    return (_iota2((CHUNK, LANES), 0) == _iota2((CHUNK, LANES), 1)).astype(F32)


def _gdn_chunk(q, k, v, ab, gate, S, alog, dtb, og, ea, eb):
    causal, strict, _, tril, ones = _chunk_consts()
    logits = _by_batch(_pick(ab, jnp.concatenate([_by_lanes(ea), _by_lanes(eb)], axis=1)), LANES)
    H = q.shape[0]
    g = -jnp.exp(alog) * _softplus(logits[:H] + dtb)
    beta = _sigmoid(logits[H:])
    qn = q * lax.rsqrt(jnp.sum(q * q, axis=-1, keepdims=True) + 1e-6) * (GDN_DK ** -0.5)
    kn = k * lax.rsqrt(jnp.sum(k * k, axis=-1, keepdims=True) + 1e-6)
    g_l = _by_lanes(g)
    gc = _by_batch(_accum(tril, g_l), LANES)
    glast = _by_batch(_accum(ones, g_l), LANES)
    gcol = gc[:, :, :CHUNK]
    grow = _by_batch(_accum(ones, _by_lanes(gc * _diag_lanes())), LANES)[:, :, :CHUNK]
    decay = jnp.exp(jnp.where(causal, gcol - grow, -jnp.inf))
    a = jnp.where(strict, beta[:, :, :CHUNK] * _nt(kn, kn) * decay, 0.0)
    eg = jnp.exp(gc)
    sol = _unit_lower_solve(a, jnp.concatenate([v * beta, kn * (beta * eg)], axis=2))
    u, w = sol[:, :, :GDN_DK], sol[:, :, GDN_DK:]
    qk = _nt(qn, kn) * decay
    v_new = u - _dot(w, S)
    o = _dot(qn * eg, S) + _dot(qk, v_new)
    cd = jnp.exp(glast)
    s_new = jnp.concatenate([cd, cd], axis=1) * S + _tn(kn * jnp.exp(glast - gc), v_new)
    on = o * lax.rsqrt(jnp.mean(o * o, axis=-1, keepdims=True) + RMS_EPS) * og
    return on * (gate * _sigmoid(gate)), s_new


GDN_HB = 8


def _gdn_specs(nc, rev):
    cm = (lambda c: nc - 1 - c) if rev else (lambda c: c)
    blk = lambda off: pl.BlockSpec((CHUNK, GDN_HB * GDN_DK), lambda c, h: (cm(c), off // GDN_HB + h))
    ab = pl.BlockSpec((CHUNK, LANES), lambda c, h: (cm(c), (GDN_IN_PAD - LANES) // LANES))
    hv = pl.BlockSpec((GDN_HB, 1, LANES), lambda c, h: (h, 0, 0))
    og = pl.BlockSpec((1, LANES), lambda c, h: (0, 0))
    em = pl.BlockSpec((GDN_HB, LANES, LANES), lambda c, h: (h, 0, 0))
    st = pl.BlockSpec((None, GDN_HB, GDN_DK, GDN_DK), lambda c, h: (cm(c), h, 0, 0))
    return blk, ab, hv, og, em, st


def _gdn_fwd(qc, kc, vc, p, alog_e, dtb_e, og, ea, eb):
    L = qc.shape[0]
    nc = L // CHUNK
    blk, ab, hv, ogs, em, st = _gdn_specs(nc, False)

    def body(q_ref, k_ref, v_ref, gate_ref, ab_ref, al_ref, dt_ref, og_ref, ea_ref, eb_ref, y_ref, sp_ref, s_scr):
        c, h = pl.program_id(0), pl.program_id(1)
        lanes = [slice(i * GDN_DK, (i + 1) * GDN_DK) for i in range(GDN_HB)]
        heads = pl.ds(h * GDN_HB, GDN_HB)
        stack = lambda ref: jnp.concatenate([ref[:, ls][None] for ls in lanes], axis=0)

        @pl.when(c == 0)
        def _():
            s_scr[heads] = jnp.zeros((GDN_HB, GDN_DK, GDN_DK), F32)

        S = s_scr[heads]
        sp_ref[...] = S
        y, s_new = _gdn_chunk(stack(q_ref), stack(k_ref), stack(v_ref), ab_ref[...], stack(gate_ref), S,
                              al_ref[...], dt_ref[...], og_ref[...], ea_ref[...], eb_ref[...])
        for i, ls in enumerate(lanes):
            y_ref[:, ls] = y[i]
        s_scr[heads] = s_new

    return pl.pallas_call(
        body, name="gdn_fwd", grid=(nc, GDN_HEADS // GDN_HB),
        in_specs=[blk(0), blk(0), blk(0), blk(3 * GDN_HEADS), ab, hv, hv, ogs, em, em],
        out_specs=[blk(0), st],
        out_shape=[jax.ShapeDtypeStruct((L, D_MODEL), F32), jax.ShapeDtypeStruct((nc, GDN_HEADS, GDN_DK, GDN_DK), F32)],
        scratch_shapes=[pltpu.VMEM((GDN_HEADS, GDN_DK, GDN_DK), F32)],
        compiler_params=_params("arbitrary", "arbitrary"),
    )(qc, kc, vc, p, p, alog_e, dtb_e, og, ea, eb)


def _gdn_bwd(qc, kc, vc, p, alog_e, dtb_e, og, ea, eb, sprev, dy):
    L = qc.shape[0]
    nc = L // CHUNK
    blk, ab, hv, ogs, em, st = _gdn_specs(nc, True)

    def body(q_ref, k_ref, v_ref, gate_ref, ab_ref, al_ref, dt_ref, og_ref, ea_ref, eb_ref, sp_ref, dy_ref,
             dq_ref, dk_ref, dv_ref, dgate_ref, dab_ref, dpar_ref, ds_scr):
        c, h = pl.program_id(0), pl.program_id(1)
        lanes = [slice(i * GDN_DK, (i + 1) * GDN_DK) for i in range(GDN_HB)]
        heads = pl.ds(h * GDN_HB, GDN_HB)
        stack = lambda ref: jnp.concatenate([ref[:, ls][None] for ls in lanes], axis=0)

        @pl.when(c == 0)
        def _():
            ds_scr[heads] = jnp.zeros((GDN_HB, GDN_DK, GDN_DK), F32)
            dpar_ref[heads] = jnp.zeros((GDN_HB, SUBLANES, LANES), F32)

        @pl.when(h == 0)
        def _():
            dab_ref[...] = jnp.zeros_like(dab_ref)

        ea_m, eb_m = ea_ref[...], eb_ref[...]
        f = lambda q, k, v, a_b, gate, S, al, dt, o_g: _gdn_chunk(q, k, v, a_b, gate, S, al, dt, o_g, ea_m, eb_m)
        _, vjp = jax.vjp(f, stack(q_ref), stack(k_ref), stack(v_ref), ab_ref[...], stack(gate_ref), sp_ref[...],
                         al_ref[...], dt_ref[...], og_ref[...])
        dq, dk, dv, dab, dgate, ds, dal, ddt, dog = vjp((stack(dy_ref), ds_scr[heads]))
        for i, ls in enumerate(lanes):
            dq_ref[:, ls] = dq[i]
            dk_ref[:, ls] = dk[i]
            dv_ref[:, ls] = dv[i]
            dgate_ref[:, ls] = dgate[i]
        ds_scr[heads] = ds
        dab_ref[...] += dab
        first = _iota2((GDN_HB, 1, LANES), 0) == 0
        dpar_ref[heads] += jnp.concatenate([dal, ddt, jnp.where(first, dog[None], 0.0),
                                            jnp.zeros((GDN_HB, SUBLANES - 3, LANES), F32)], axis=1)

    return pl.pallas_call(
        body, name="gdn_bwd", grid=(nc, GDN_HEADS // GDN_HB),
        in_specs=[blk(0), blk(0), blk(0), blk(3 * GDN_HEADS), ab, hv, hv, ogs, em, em, st, blk(0)],
        out_specs=[blk(0), blk(0), blk(0), blk(0), pl.BlockSpec((CHUNK, LANES), lambda c, h: (nc - 1 - c, 0)),
                   pl.BlockSpec((GDN_HEADS, SUBLANES, LANES), lambda c, h: (0, 0, 0))],
        out_shape=[jax.ShapeDtypeStruct((L, D_MODEL), F32)] * 4
        + [jax.ShapeDtypeStruct((L, LANES), F32), jax.ShapeDtypeStruct((GDN_HEADS, SUBLANES, LANES), F32)],
        scratch_shapes=[pltpu.VMEM((GDN_HEADS, GDN_DK, GDN_DK), F32)],
        compiler_params=_params("arbitrary", "arbitrary"),
    )(qc, kc, vc, p, p, alog_e, dtb_e, og, ea, eb, sprev, dy)


def _gdn_selectors():
    rows = np.arange(LANES)[None, :, None]
    heads = np.arange(GDN_HEADS)[:, None, None]
    ea = np.broadcast_to(rows == heads, (GDN_HEADS, LANES, LANES)).astype(np.float32)
    eb = np.broadcast_to(rows == heads + GDN_HEADS, (GDN_HEADS, LANES, LANES)).astype(np.float32)
    return jnp.asarray(ea), jnp.asarray(eb)


M2_GW = M2_INNER // M2_GROUPS
M2_HPG = M2_HEADS // M2_GROUPS
M2_HD = M2_INNER // M2_HEADS


def _m2_chunk(x, bm, cm, z, dtr, st, dtb, alog, dsk, ng, e, ecol):
    G = x.shape[0]
    causal, _, _, tril, ones = _chunk_consts()
    dt_n = _softplus(dtr + dtb)
    da_n = dt_n * (-jnp.exp(alog))
    cum_n = _accum(tril, da_n)
    tot_n = _accum(ones, da_n)
    wide = _pick(jnp.concatenate([dt_n, cum_n, tot_n], axis=0), e)
    dt_w, cum_w, tot_w = (_by_batch(wide[i * CHUNK:(i + 1) * CHUNK], M2_GW) for i in range(3))
    xdt = x * dt_w
    cb = _nt(cm, bm)
    heads = lambda t: jnp.concatenate([t[i:i + 1] for i in range(G) for _ in range(M2_HPG)], axis=0)
    colb = _by_batch(_pick(cum_n, ecol), LANES)
    rowb = _by_batch(_accum(ones, _by_lanes(colb * _diag_lanes())), LANES)
    lmat = jnp.exp(jnp.where(causal, colb[:, :, :CHUNK] - rowb[:, :, :CHUNK], -jnp.inf))
    yr = _dot(heads(cb) * lmat, heads(xdt))
    head = _iota2((CHUNK, M2_GW), 1) // M2_HD
    ydiag = jnp.concatenate([sum(jnp.where(head == r, yr[i * M2_HPG + r], 0.0) for r in range(M2_HPG))[None] for i in range(G)], axis=0)
    st_new = _tn(bm, xdt * jnp.exp(tot_w - cum_w))
    cd = jnp.exp(tot_w)
    s_new = jnp.concatenate([cd, cd], axis=1) * st + st_new
    y = ydiag + _dot(cm, st) * jnp.exp(cum_w) + dsk * x
    y = y * (z * _sigmoid(z))
    yn = y * lax.rsqrt(jnp.mean(y * y, axis=-1, keepdims=True) + RMS_EPS) * ng
    return yn, s_new


M2_GB = 4


def _m2_specs(nc, rev):
    cm = (lambda c: nc - 1 - c) if rev else (lambda c: c)
    wide = lambda off: pl.BlockSpec((CHUNK, M2_GB * M2_GW), lambda c, g: (cm(c), off // M2_GB + g))
    nar = lambda off: pl.BlockSpec((CHUNK, M2_GB * LANES), lambda c, g: (cm(c), off // M2_GB + g))
    dts = pl.BlockSpec((CHUNK, LANES), lambda c, g: (cm(c), (M2_IN_PAD - LANES) // LANES))
    v128 = pl.BlockSpec((1, LANES), lambda c, g: (0, 0))
    v256 = pl.BlockSpec((1, M2_GB * M2_GW), lambda c, g: (0, g))
    es = pl.BlockSpec((LANES, M2_GB * M2_GW), lambda c, g: (0, g))
    ecs = pl.BlockSpec((LANES, M2_GB * M2_HPG * LANES), lambda c, g: (0, g))
    st = pl.BlockSpec((None, M2_GB, M2_STATE, M2_GW), lambda c, g: (cm(c), g, 0, 0))
    return wide, nar, dts, v128, v256, es, ecs, st


def _m2_fwd(xbc, p, dtb, alog, dsk, ng, e, ecol):
    L = xbc.shape[0]
    nc = L // CHUNK
    wide, nar, dts, v128, v256, es, ecs, st = _m2_specs(nc, False)

    def body(x_ref, b_ref, c_ref, z_ref, dt_ref, dtb_ref, al_ref, dsk_ref, ng_ref, e_ref, ec_ref, y_ref, sp_ref, s_scr):
        c, g = pl.program_id(0), pl.program_id(1)
        wide_l = [slice(i * M2_GW, (i + 1) * M2_GW) for i in range(M2_GB)]
        nar_l = [slice(i * LANES, (i + 1) * LANES) for i in range(M2_GB)]
        groups = pl.ds(g * M2_GB, M2_GB)
        wide_s = lambda ref: jnp.concatenate([ref[:, ls][None] for ls in wide_l], axis=0)
        nar_s = lambda ref: jnp.concatenate([ref[:, ls][None] for ls in nar_l], axis=0)

        @pl.when(c == 0)
        def _():
            s_scr[groups] = jnp.zeros((M2_GB, M2_STATE, M2_GW), F32)

        S = s_scr[groups]
        sp_ref[...] = S
        y, s_new = _m2_chunk(wide_s(x_ref), nar_s(b_ref), nar_s(c_ref), wide_s(z_ref), dt_ref[...], S, dtb_ref[...], al_ref[...],
                             wide_s(dsk_ref), wide_s(ng_ref), e_ref[...], ec_ref[...])
        for i, ls in enumerate(wide_l):
            y_ref[:, ls] = y[i]
        s_scr[groups] = s_new

    return pl.pallas_call(
        body, name="m2_fwd", grid=(nc, M2_GROUPS // M2_GB),
        in_specs=[wide(0), nar(2 * M2_GROUPS), nar(3 * M2_GROUPS), wide(0), dts, v128, v128, v256, v256, es, ecs],
        out_specs=[wide(0), st],
        out_shape=[jax.ShapeDtypeStruct((L, M2_INNER), F32), jax.ShapeDtypeStruct((nc, M2_GROUPS, M2_STATE, M2_GW), F32)],
        scratch_shapes=[pltpu.VMEM((M2_GROUPS, M2_STATE, M2_GW), F32)],
        compiler_params=_params("arbitrary", "arbitrary"),
    )(xbc, xbc, xbc, p, p, dtb, alog, dsk, ng, e, ecol)


def _m2_bwd(xbc, p, dtb, alog, dsk, ng, e, ecol, sprev, dy):
    L = xbc.shape[0]
    nc = L // CHUNK
    wide, nar, dts, v128, v256, es, ecs, st = _m2_specs(nc, True)

    def body(x_ref, b_ref, c_ref, z_ref, dt_ref, dtb_ref, al_ref, dsk_ref, ng_ref, e_ref, ec_ref, sp_ref, dy_ref,
             dx_ref, db_ref, dc_ref, dz_ref, ddt_ref, dnar_ref, dwide_ref, ds_scr):
        c, g = pl.program_id(0), pl.program_id(1)
        wide_l = [slice(i * M2_GW, (i + 1) * M2_GW) for i in range(M2_GB)]
        nar_l = [slice(i * LANES, (i + 1) * LANES) for i in range(M2_GB)]
        groups = pl.ds(g * M2_GB, M2_GB)
        wide_s = lambda ref: jnp.concatenate([ref[:, ls][None] for ls in wide_l], axis=0)
        nar_s = lambda ref: jnp.concatenate([ref[:, ls][None] for ls in nar_l], axis=0)

        @pl.when(jnp.logical_and(c == 0, g == 0))
        def _():
            dnar_ref[...] = jnp.zeros_like(dnar_ref)

        @pl.when(c == 0)
        def _():
            ds_scr[groups] = jnp.zeros((M2_GB, M2_STATE, M2_GW), F32)
            dwide_ref[groups] = jnp.zeros((M2_GB, SUBLANES, M2_GW), F32)

        @pl.when(g == 0)
        def _():
            ddt_ref[...] = jnp.zeros_like(ddt_ref)

        e_m, ec_m = e_ref[...], ec_ref[...]
        f = lambda x, bm, cm, z, dtr, S, dtb, al, dsk, ng: _m2_chunk(x, bm, cm, z, dtr, S, dtb, al, dsk, ng, e_m, ec_m)
        _, vjp = jax.vjp(f, wide_s(x_ref), nar_s(b_ref), nar_s(c_ref), wide_s(z_ref), dt_ref[...], sp_ref[...], dtb_ref[...],
                         al_ref[...], wide_s(dsk_ref), wide_s(ng_ref))
        dx, db, dc, dz, ddt, ds, ddtb, dal, ddsk, dng = vjp((wide_s(dy_ref), ds_scr[groups]))
        for i in range(M2_GB):
            dx_ref[:, wide_l[i]] = dx[i]
            db_ref[:, nar_l[i]] = db[i]
            dc_ref[:, nar_l[i]] = dc[i]
            dz_ref[:, wide_l[i]] = dz[i]
        ds_scr[groups] = ds
        ddt_ref[...] += ddt
        dnar_ref[...] += jnp.concatenate([ddtb, dal, jnp.zeros((SUBLANES - 2, LANES), F32)], axis=0)
        dwide_ref[groups] += jnp.concatenate([ddsk, dng, jnp.zeros((M2_GB, SUBLANES - 2, M2_GW), F32)], axis=1)

    return pl.pallas_call(
        body, name="m2_bwd", grid=(nc, M2_GROUPS // M2_GB),
        in_specs=[wide(0), nar(2 * M2_GROUPS), nar(3 * M2_GROUPS), wide(0), dts, v128, v128, v256, v256, es, ecs, st, wide(0)],
        out_specs=[wide(0), nar(0), nar(0), wide(0), pl.BlockSpec((CHUNK, LANES), lambda c, g: (nc - 1 - c, 0)),
                   pl.BlockSpec((SUBLANES, LANES), lambda c, g: (0, 0)),
                   pl.BlockSpec((M2_GROUPS, SUBLANES, M2_GW), lambda c, g: (0, 0, 0))],
        out_shape=[jax.ShapeDtypeStruct((L, M2_INNER), F32), jax.ShapeDtypeStruct((L, M2_GROUPS * M2_STATE), F32),
                   jax.ShapeDtypeStruct((L, M2_GROUPS * M2_STATE), F32), jax.ShapeDtypeStruct((L, M2_INNER), F32),
                   jax.ShapeDtypeStruct((L, LANES), F32), jax.ShapeDtypeStruct((SUBLANES, LANES), F32),
                   jax.ShapeDtypeStruct((M2_GROUPS, SUBLANES, M2_GW), F32)],
        scratch_shapes=[pltpu.VMEM((M2_GROUPS, M2_STATE, M2_GW), F32)],
        compiler_params=_params("arbitrary", "arbitrary"),
    )(xbc, xbc, xbc, p, p, dtb, alog, dsk, ng, e, ecol, sprev, dy)


def _m2_selectors():
    e = np.zeros((LANES, M2_INNER), np.float32)
    ecol = np.zeros((LANES, M2_HEADS * LANES), np.float32)
    for h in range(M2_HEADS):
        e[h, M2_HD * h:M2_HD * (h + 1)] = 1.0
        ecol[h, LANES * h:LANES * (h + 1)] = 1.0
    return jnp.asarray(e), jnp.asarray(ecol)


S5_NS = S5_GROUPS * S5_STATE // S5_BLOCKS
S5_ROWS = 256
GELU_C = math.sqrt(2.0 / math.pi)


def _gelu(x):
    return 0.5 * x * (1.0 + jnp.tanh(GELU_C * (x + 0.044715 * x * x * x)))


def _gelu_grad(x):
    t = jnp.tanh(GELU_C * (x + 0.044715 * x * x * x))
    return 0.5 * (1.0 + t) + 0.5 * x * (1.0 - t * t) * GELU_C * (1.0 + 3.0 * 0.044715 * x * x)


def _s5_scan(re_ref, im_ref, pw_re, pw_im, nrows, reverse):
    n = re_ref.shape[1]
    row = _iota2((SUBLANES, n), 0)
    steps = [(d, pw_re[d - 1:d, :], pw_im[d - 1:d, :]) for d in (1, 2, 4)]
    if reverse:
        cw_re = jnp.concatenate([pw_re[SUBLANES - 1 - k:SUBLANES - k, :] for k in range(SUBLANES)], axis=0)
        cw_im = jnp.concatenate([pw_im[SUBLANES - 1 - k:SUBLANES - k, :] for k in range(SUBLANES)], axis=0)
    else:
        cw_re, cw_im = pw_re, pw_im
    edge = 0 if reverse else SUBLANES - 1
    ngroups = nrows // SUBLANES

    def step(i, carry):
        cr, ci = carry
        gi = (ngroups - 1 - i) if reverse else i
        r0 = pl.multiple_of(gi * SUBLANES, SUBLANES)
        xr, xi = re_ref[pl.ds(r0, SUBLANES), :], im_ref[pl.ds(r0, SUBLANES), :]
        for d, pr, pi in steps:
            if reverse:
                sr = jnp.where(row < SUBLANES - d, pltpu.roll(xr, SUBLANES - d, axis=0), 0.0)
                si = jnp.where(row < SUBLANES - d, pltpu.roll(xi, SUBLANES - d, axis=0), 0.0)
            else:
                sr = jnp.where(row >= d, pltpu.roll(xr, d, axis=0), 0.0)
                si = jnp.where(row >= d, pltpu.roll(xi, d, axis=0), 0.0)
            xr, xi = xr + (pr * sr - pi * si), xi + (pr * si + pi * sr)
        xr, xi = xr + (cw_re * cr - cw_im * ci), xi + (cw_re * ci + cw_im * cr)
        re_ref[pl.ds(r0, SUBLANES), :] = xr
        im_ref[pl.ds(r0, SUBLANES), :] = xi
        return (jnp.sum(jnp.where(row == edge, xr, 0.0), axis=0, keepdims=True),
                jnp.sum(jnp.where(row == edge, xi, 0.0), axis=0, keepdims=True))

    z = jnp.zeros((1, n), F32)
    lax.fori_loop(0, ngroups, step, (z, z))


def _s5_project_in(u_ref, bm_ref, re_ref, im_ref, L):
    def step(i, carry):
        r0 = pl.multiple_of(i * S5_ROWS, S5_ROWS)
        bu = _dot(u_ref[pl.ds(r0, S5_ROWS), :], bm_ref[...])
        re_ref[pl.ds(r0, S5_ROWS), :] = bu[:, :S5_NS]
        im_ref[pl.ds(r0, S5_ROWS), :] = bu[:, S5_NS:]
        return carry

    lax.fori_loop(0, L // S5_ROWS, step, 0)


def _s5_specs(L):
    col = pl.BlockSpec((L, LANES), lambda j: (0, j))
    bm = pl.BlockSpec((None, LANES, 2 * S5_NS), lambda j: (j, 0, 0))
    cm = pl.BlockSpec((None, 2 * S5_NS, LANES), lambda j: (j, 0, 0))
    pw = pl.BlockSpec((None, SUBLANES, S5_NS), lambda j: (j, 0, 0))
    vec = pl.BlockSpec((1, LANES), lambda j: (0, j))
    return col, bm, cm, pw, vec


def _s5_fwd(u, bmat, cmat, pw_re, pw_im, dsk):
    L = u.shape[0]
    col, bm, cm, pw, vec = _s5_specs(L)

    def body(u_ref, bm_ref, cm_ref, pr_ref, pi_ref, d_ref, y_ref, re_scr, im_scr):
        _s5_project_in(u_ref, bm_ref, re_scr, im_scr, L)
        _s5_scan(re_scr, im_scr, pr_ref[...], pi_ref[...], L, False)

        def step(i, carry):
            r0 = pl.multiple_of(i * S5_ROWS, S5_ROWS)
            rows = pl.ds(r0, S5_ROWS)
            y = _dot(re_scr[rows, :], cm_ref[:S5_NS, :]) + _dot(im_scr[rows, :], cm_ref[S5_NS:, :]) + d_ref[...] * u_ref[rows, :]
            y_ref[rows, :] = _gelu(y)
            return carry

        lax.fori_loop(0, L // S5_ROWS, step, 0)

    return pl.pallas_call(
        body, name="s5_fwd", grid=(S5_BLOCKS,),
        in_specs=[col, bm, cm, pw, pw, vec], out_specs=col,
        out_shape=jax.ShapeDtypeStruct((L, D_MODEL), F32),
        scratch_shapes=[pltpu.VMEM((L, S5_NS), F32)] * 2,
        compiler_params=_params("parallel"),
    )(u, bmat, cmat, pw_re, pw_im, dsk)


def _s5_bwd(u, bmat, cmat, pw_re, pw_im, dsk, dyg):
    L = u.shape[0]
    col, bm, cm, pw, vec = _s5_specs(L)

    def body(u_ref, bm_ref, cm_ref, pr_ref, pi_ref, d_ref, dy_ref, du_ref, dbm_ref, dcm_ref, dlam_ref, dd_ref,
             re_scr, im_scr, gr_scr, gi_scr, dyp_scr):
        _s5_project_in(u_ref, bm_ref, re_scr, im_scr, L)
        _s5_scan(re_scr, im_scr, pr_ref[...], pi_ref[...], L, False)

        def step(i, carry):
            dcr, dci, dd = carry
            r0 = pl.multiple_of(i * S5_ROWS, S5_ROWS)
            rows = pl.ds(r0, S5_ROWS)
            sr, si, uu = re_scr[rows, :], im_scr[rows, :], u_ref[rows, :]
            y = _dot(sr, cm_ref[:S5_NS, :]) + _dot(si, cm_ref[S5_NS:, :]) + d_ref[...] * uu
            dyp = dy_ref[rows, :] * _gelu_grad(y)
            dyp_scr[rows, :] = dyp
            gr_scr[rows, :] = _nt(dyp, cm_ref[:S5_NS, :])
            gi_scr[rows, :] = _nt(dyp, cm_ref[S5_NS:, :])
            return dcr + _tn(sr, dyp), dci + _tn(si, dyp), dd + jnp.sum(dyp * uu, axis=0, keepdims=True)

        zc = jnp.zeros((S5_NS, LANES), F32)
        dcr, dci, dd = lax.fori_loop(0, L // S5_ROWS, step, (zc, zc, jnp.zeros((1, LANES), F32)))
        dcm_ref[:S5_NS, :] = dcr
        dcm_ref[S5_NS:, :] = dci
        dd_ref[...] = dd

        _s5_scan(gr_scr, gi_scr, pr_ref[...], -pi_ref[...], L, True)

        row = _iota2((SUBLANES, S5_NS), 0)

        def lam_step(i, carry):
            ar, ai, pr, pi = carry
            r0 = pl.multiple_of(i * SUBLANES, SUBLANES)
            rows = pl.ds(r0, SUBLANES)
            sr, si = re_scr[rows, :], im_scr[rows, :]
            spr = jnp.where(row >= 1, pltpu.roll(sr, 1, axis=0), pr)
            spi = jnp.where(row >= 1, pltpu.roll(si, 1, axis=0), pi)
            gr, gi = gr_scr[rows, :], gi_scr[rows, :]
            ar = ar + jnp.sum(spr * gr + spi * gi, axis=0, keepdims=True)
            ai = ai + jnp.sum(spr * gi - spi * gr, axis=0, keepdims=True)
            last = row == SUBLANES - 1
            return (ar, ai, jnp.sum(jnp.where(last, sr, 0.0), axis=0, keepdims=True),
                    jnp.sum(jnp.where(last, si, 0.0), axis=0, keepdims=True))

        z = jnp.zeros((1, S5_NS), F32)
        ar, ai, _, _ = lax.fori_loop(0, L // SUBLANES, lam_step, (z, z, z, z))
        dlam_ref[...] = jnp.concatenate([ar, ai, jnp.zeros((SUBLANES - 2, S5_NS), F32)], axis=0)

        def in_step(i, carry):
            dbr, dbi = carry
            r0 = pl.multiple_of(i * S5_ROWS, S5_ROWS)
            rows = pl.ds(r0, S5_ROWS)
            gr, gi, uu = gr_scr[rows, :], gi_scr[rows, :], u_ref[rows, :]
            du_ref[rows, :] = dyp_scr[rows, :] * d_ref[...] + _nt(gr, bm_ref[:, :S5_NS]) + _nt(gi, bm_ref[:, S5_NS:])
            return dbr + _tn(uu, gr), dbi + _tn(uu, gi)

        zb = jnp.zeros((LANES, S5_NS), F32)
        dbr, dbi = lax.fori_loop(0, L // S5_ROWS, in_step, (zb, zb))
        dbm_ref[:, :S5_NS] = dbr
        dbm_ref[:, S5_NS:] = dbi

    return pl.pallas_call(
        body, name="s5_bwd", grid=(S5_BLOCKS,),
        in_specs=[col, bm, cm, pw, pw, vec, col], out_specs=[col, bm, cm, pw, vec],
        out_shape=[jax.ShapeDtypeStruct((L, D_MODEL), F32), jax.ShapeDtypeStruct((S5_BLOCKS, LANES, 2 * S5_NS), F32),
                   jax.ShapeDtypeStruct((S5_BLOCKS, 2 * S5_NS, LANES), F32),
                   jax.ShapeDtypeStruct((S5_BLOCKS, SUBLANES, S5_NS), F32), jax.ShapeDtypeStruct((1, D_MODEL), F32)],
        scratch_shapes=[pltpu.VMEM((L, S5_NS), F32)] * 4 + [pltpu.VMEM((L, LANES), F32)],
        compiler_params=_params("parallel"),
    )(u, bmat, cmat, pw_re, pw_im, dsk, dyg)


def _s5_discretize(lam_re, lam_im, log_dt, b_re, b_im, e16):
    dt = jnp.exp(log_dt)
    zr, zi = lam_re * dt, lam_im * dt
    mag = jnp.exp(zr)
    lbr, lbi = mag * jnp.cos(zi), mag * jnp.sin(zi)
    den = lam_re * lam_re + lam_im * lam_im
    nr, ni = lbr - 1.0, lbi
    cr = (nr * lam_re + ni * lam_im) / den
    ci = (ni * lam_re - nr * lam_im) / den
    crw, ciw = _pick(cr, e16), _pick(ci, e16)
    return lbr, lbi, crw * b_re - ciw * b_im, crw * b_im + ciw * b_re


def _s5_params_fwd(lam_re, lam_im, log_dt, b_re, b_im, e16):
    def body(lr, li, ld, br, bi, e, o1, o2, o3, o4):
        for o, val in zip((o1, o2, o3, o4), _s5_discretize(lr[...], li[...], ld[...], br[...], bi[...], e[...])):
            o[...] = val

    g, p, n = S5_GROUPS, S5_STATE, S5_STATE * S5_GROUP
    return pl.pallas_call(
        body, name="s5_params_fwd",
        out_shape=[jax.ShapeDtypeStruct((g, p), F32)] * 2 + [jax.ShapeDtypeStruct((g, n), F32)] * 2,
        compiler_params=_params(),
    )(lam_re, lam_im, log_dt, b_re, b_im, e16)


def _s5_params_bwd(lam_re, lam_im, log_dt, b_re, b_im, e16, cts):
    def body(lr, li, ld, br, bi, e, c1, c2, c3, c4, o1, o2, o3, o4, o5):
        e_m = e[...]
        f = lambda a, b, c, d, g: _s5_discretize(a, b, c, d, g, e_m)
        _, vjp = jax.vjp(f, lr[...], li[...], ld[...], br[...], bi[...])
        for o, val in zip((o1, o2, o3, o4, o5), vjp((c1[...], c2[...], c3[...], c4[...]))):
            o[...] = val

    g, p, n = S5_GROUPS, S5_STATE, S5_STATE * S5_GROUP
    return pl.pallas_call(
        body, name="s5_params_bwd",
        out_shape=[jax.ShapeDtypeStruct((g, p), F32)] * 2 + [jax.ShapeDtypeStruct((g, 1), F32)]
        + [jax.ShapeDtypeStruct((g, n), F32)] * 2,
        compiler_params=_params(),
    )(lam_re, lam_im, log_dt, b_re, b_im, e16, *cts)


def _add_residual(acc, h):
    return (acc + h,)


def _mlp_fwd(i, h, g, w1, w2):
    hn = _rms_fwd(f"mlp{i}_norm", h, g)
    r = _mm(f"mlp{i}_up", hn, w1, "nn", (BF16,), epi=lambda acc: (jnp.square(jnp.maximum(acc, 0.0)),))
    return _mm(f"mlp{i}_down", r, w2, "nn", (F32,), epi=_add_residual, extras=(h,)), (h, hn, r)


def _mlp_bwd(i, dh_out, saved, g, w1, w2):
    h, hn, r = saved
    dw2 = _mm(f"mlp{i}_dw2", r, dh_out, "tn", (BF16,))
    da = _mm(f"mlp{i}_da", dh_out, w2, "nt", (BF16,), epi=lambda acc, rr: (acc * (2.0 * jnp.sqrt(rr.astype(F32))),), extras=(r,))
    dw1 = _mm(f"mlp{i}_dw1", hn, da, "tn", (BF16,))
    dhn = _mm(f"mlp{i}_dhn", da, w1, "nt", (F32,))
    dh, dg = _rms_bwd(f"mlp{i}_dnorm", h, g, dhn, dh_out)
    return dh, dg[0], dw1, dw2


def _lanes(v, n):
    return jnp.broadcast_to(v.reshape(n, 1, 1), (n, 1, LANES))


def _gdn_fwd_layer(i, h, g, w_in, conv_w, a_log, dt_bias, o_g, w_out):
    hn = _rms_fwd(f"gdn{i}_norm", h, g)
    p = _mm(f"gdn{i}_in", hn, w_in, "nn", (F32,))
    zb = jnp.zeros((1, D_MODEL), F32)
    qkv = [_conv_fwd(f"gdn{i}_conv{t}", p, t * D_MODEL, conv_w[:, t * D_MODEL:(t + 1) * D_MODEL], zb) for t in range(3)]
    ea, eb = _gdn_selectors()
    y, sprev = _gdn_fwd(*qkv, p, _lanes(a_log, GDN_HEADS), _lanes(dt_bias, GDN_HEADS), o_g.reshape(1, LANES), ea, eb)
    return _mm(f"gdn{i}_out", y, w_out, "nn", (F32,), epi=_add_residual, extras=(h,)), (h, hn, p, qkv, y, sprev)


def _gdn_bwd_layer(i, dh_out, saved, g, w_in, conv_w, a_log, dt_bias, o_g, w_out):
    h, hn, p, qkv, y, sprev = saved
    dy = _mm(f"gdn{i}_dy", dh_out, w_out, "nt", (F32,))
    dw_out = _mm(f"gdn{i}_dwout", y, dh_out, "tn", (BF16,))
    ea, eb = _gdn_selectors()
    dq, dk, dv, dgate, dab, dpar = _gdn_bwd(*qkv, p, _lanes(a_log, GDN_HEADS), _lanes(dt_bias, GDN_HEADS),
                                            o_g.reshape(1, LANES), ea, eb, sprev, dy)
    zb = jnp.zeros((1, D_MODEL), F32)
    dpre, dcw = [], []
    for t, d in enumerate((dq, dk, dv)):
        dx, dw, _ = _conv_bwd(f"gdn{i}_dconv{t}", p, t * D_MODEL, conv_w[:, t * D_MODEL:(t + 1) * D_MODEL], zb, d)
        dpre.append(dx)
        dcw.append(dw)
    dp = jnp.concatenate(dpre + [dgate, dab], axis=1)
    dw_in = _mm(f"gdn{i}_dwin", hn, dp, "tn", (BF16,))[:, :GDN_IN]
    dhn = _mm(f"gdn{i}_dhn", dp, w_in, "nt", (F32,))
    dh, dg = _rms_bwd(f"gdn{i}_dnorm", h, g, dhn, dh_out)
    grads = dict(w_in=dw_in, conv_w=jnp.concatenate(dcw, axis=1), a_log=jnp.sum(dpar[:, 0, :], axis=-1),
                 dt_bias=jnp.sum(dpar[:, 1, :], axis=-1), o_norm_g=jnp.sum(dpar[:, 2, :], axis=0), w_out=dw_out)
    return dh, dg[0], grads


def _m2_vectors(dt_bias, a_log, d_skip, norm_g):
    pad = lambda v: jnp.pad(v, (0, LANES - M2_HEADS)).reshape(1, LANES)
    return pad(dt_bias), pad(a_log), jnp.repeat(d_skip, M2_HD).reshape(1, M2_INNER), norm_g.reshape(1, M2_INNER)


def _m2_fwd_layer(h, g, w_in, conv_w, conv_b, dt_bias, a_log, d_skip, norm_g, w_out):
    hn = _rms_fwd("m2_norm", h, g)
    p = _mm("m2_in", hn, w_in, "nn", (F32,))
    xbc = _conv_fwd("m2_conv", p, M2_INNER, conv_w, conv_b.reshape(1, M2_CONV_CH))
    e, ecol = _m2_selectors()
    y, sprev = _m2_fwd(xbc, p, *_m2_vectors(dt_bias, a_log, d_skip, norm_g), e, ecol)
    return _mm("m2_out", y, w_out, "nn", (F32,), epi=_add_residual, extras=(h,)), (h, hn, p, xbc, y, sprev)


def _m2_bwd_layer(dh_out, saved, g, w_in, conv_w, conv_b, dt_bias, a_log, d_skip, norm_g, w_out):
    h, hn, p, xbc, y, sprev = saved
    dy = _mm("m2_dy", dh_out, w_out, "nt", (F32,))
    dw_out = _mm("m2_dwout", y, dh_out, "tn", (BF16,))
    e, ecol = _m2_selectors()
    dx, db, dc, dz, ddt, dnar, dwide = _m2_bwd(xbc, p, *_m2_vectors(dt_bias, a_log, d_skip, norm_g), e, ecol, sprev, dy)
    dxbc, dcw, dcb = _conv_bwd("m2_dconv", p, M2_INNER, conv_w, conv_b.reshape(1, M2_CONV_CH),
                               jnp.concatenate([dx, db, dc], axis=1))
    dp = jnp.concatenate([dz, dxbc, ddt], axis=1)
    dw_in = _mm("m2_dwin", hn, dp, "tn", (BF16,))[:, :M2_IN]
    dhn = _mm("m2_dhn", dp, w_in, "nt", (F32,))
    dh, dg = _rms_bwd("m2_dnorm", h, g, dhn, dh_out)
    grads = dict(w_in=dw_in, conv_w=dcw, conv_b=dcb[0], dt_bias=dnar[0, :M2_HEADS], a_log=dnar[1, :M2_HEADS],
                 d=jnp.sum(dwide[:, 0, :].reshape(M2_HEADS, M2_HD), axis=-1), norm_g=dwide[:, 1, :].reshape(M2_INNER),
                 w_out=dw_out)
    return dh, dg[0], grads


def _s5_selector():
    e16 = np.zeros((S5_STATE, S5_STATE * S5_GROUP), np.float32)
    for p in range(S5_STATE):
        e16[p, p * S5_GROUP:(p + 1) * S5_GROUP] = 1.0
    return jnp.asarray(e16)


def _s5_operands(lbr, lbi, bbr, bbi, c_re, c_im):
    eye = jnp.eye(S5_BLOCKS, dtype=F32)
    gpb = S5_GROUPS // S5_BLOCKS
    bd = lambda t: jnp.einsum("jgpk,gh->jgkhp", t.reshape(S5_BLOCKS, gpb, S5_STATE, S5_GROUP), eye).reshape(S5_BLOCKS, LANES, S5_NS)
    cd = lambda t: jnp.einsum("jgkp,gh->jgphk", t.reshape(S5_BLOCKS, gpb, S5_GROUP, S5_STATE), eye).reshape(S5_BLOCKS, S5_NS, LANES)
    bmat = jnp.concatenate([bd(bbr), bd(bbi)], axis=2).astype(BF16)
    cmat = jnp.concatenate([cd(c_re), -cd(c_im)], axis=1).astype(BF16)
    ar, ai = lbr.reshape(S5_BLOCKS, S5_NS), lbi.reshape(S5_BLOCKS, S5_NS)
    pr, pi = [ar], [ai]
    for _ in range(SUBLANES - 1):
        pr, pi = pr + [pr[-1] * ar - pi[-1] * ai], pi + [pr[-1] * ai + pi[-1] * ar]
    return bmat, cmat, jnp.stack(pr, axis=1), jnp.stack(pi, axis=1)


def _s5_fwd_layer(h, g, w_in, lam_re, lam_im, log_dt, b_re, b_im, c_re, c_im, d_skip, w_out):
    hn = _rms_fwd("s5_norm", h, g)
    u = _mm("s5_in", hn, w_in, "nn", (F32,))
    n = S5_STATE * S5_GROUP
    lbr, lbi, bbr, bbi = _s5_params_fwd(lam_re, lam_im, log_dt.reshape(S5_GROUPS, 1), b_re.reshape(S5_GROUPS, n),
                                        b_im.reshape(S5_GROUPS, n), _s5_selector())
    ops = _s5_operands(lbr, lbi, bbr, bbi, c_re, c_im)
    yg = _s5_fwd(u, *ops, d_skip.reshape(1, D_MODEL))
    ag = _mm("s5_out", yg, w_out, "nn", (F32,))
    return _glu_fwd(h, ag), (h, hn, u, ops, yg, ag)


def _s5_bwd_layer(dh_out, saved, g, w_in, lam_re, lam_im, log_dt, b_re, b_im, c_re, c_im, d_skip, w_out):
    h, hn, u, ops, yg, ag = saved
    dag = _glu_bwd(dh_out, ag)
    dw_out = _mm("s5_dwout", yg, dag, "tn", (BF16,))
    dyg = _mm("s5_dyg", dag, w_out, "nt", (F32,))
    du, dbmat, dcmat, dlam, ddsk = _s5_bwd(u, *ops, d_skip.reshape(1, D_MODEL), dyg)
    eye = jnp.eye(S5_BLOCKS, dtype=F32)
    gpb = S5_GROUPS // S5_BLOCKS
    n = S5_STATE * S5_GROUP
    ub = lambda t: jnp.einsum("jgkhp,gh->jgpk", t.reshape(S5_BLOCKS, gpb, S5_GROUP, gpb, S5_STATE), eye).reshape(S5_GROUPS, n)
    uc = lambda t: jnp.einsum("jgphk,gh->jgkp", t.reshape(S5_BLOCKS, gpb, S5_STATE, gpb, S5_GROUP), eye).reshape(c_re.shape)
    cts = (dlam[:, 0, :].reshape(S5_GROUPS, S5_STATE), dlam[:, 1, :].reshape(S5_GROUPS, S5_STATE),
           ub(dbmat[:, :, :S5_NS]), ub(dbmat[:, :, S5_NS:]))
    dlr, dli, dld, dbr, dbi = _s5_params_bwd(lam_re, lam_im, log_dt.reshape(S5_GROUPS, 1), b_re.reshape(S5_GROUPS, n),
                                             b_im.reshape(S5_GROUPS, n), _s5_selector(), cts)
    dw_in = _mm("s5_dwin", hn, du, "tn", (BF16,))
    dhn = _mm("s5_dhn", du, w_in, "nt", (F32,))
    dh, dg = _rms_bwd("s5_dnorm", h, g, dhn, dh_out)
    grads = dict(w_in=dw_in, lam_re=dlr, lam_im=dli, log_dt=dld[:, 0], b_re=dbr.reshape(b_re.shape), b_im=dbi.reshape(b_im.shape),
                 c_re=uc(dcmat[:, :S5_NS, :]), c_im=-uc(dcmat[:, S5_NS:, :]), d=ddsk[0], w_out=dw_out)
    return dh, dg[0], grads


MIXER_OF_LAYER = ("gdn", "s5", "m2", "gdn")
MIXER_INDEX = (0, 0, 0, 1)


def _mixer_args(W, i):
    kind, j = MIXER_OF_LAYER[i], MIXER_INDEX[i]
    if kind == "gdn":
        return tuple(W["gdn_" + k][j] for k in ("w_in", "conv_w", "a_log", "dt_bias", "o_norm_g", "w_out"))
    if kind == "s5":
        return tuple(W["s5_" + k][j] for k in ("w_in", "lam_re", "lam_im", "log_dt", "b_re", "b_im", "c_re", "c_im", "d", "w_out"))
    return tuple(W["m2_" + k][j] for k in ("w_in", "conv_w", "conv_b", "dt_bias", "a_log", "d", "norm_g", "w_out"))


def _local_step(x, target, W):
    h = x
    saved = []
    for i in range(DEPTH):
        kind = MIXER_OF_LAYER[i]
        args = _mixer_args(W, i)
        if kind == "gdn":
            h, sm = _gdn_fwd_layer(i, h, W["norm_mix_g"][i], *args)
        elif kind == "s5":
            h, sm = _s5_fwd_layer(h, W["norm_mix_g"][i], *args)
        else:
            h, sm = _m2_fwd_layer(h, W["norm_mix_g"][i], *args)
        h, sp = _mlp_fwd(i, h, W["norm_mlp_g"][i], W["mlp_w1"][i], W["mlp_w2"][i])
        saved.append((sm, sp))
    loss, dh, dgf = _loss_head(h, W["final_norm_g"], target)
    G = {"final_norm_g": dgf[0], "norm_mix_g": [None] * DEPTH, "norm_mlp_g": [None] * DEPTH,
         "mlp_w1": [None] * DEPTH, "mlp_w2": [None] * DEPTH}
    mix = {}
    for i in reversed(range(DEPTH)):
        kind = MIXER_OF_LAYER[i]
        sm, sp = saved[i]
        dh, G["norm_mlp_g"][i], G["mlp_w1"][i], G["mlp_w2"][i] = _mlp_bwd(i, dh, sp, W["norm_mlp_g"][i], W["mlp_w1"][i], W["mlp_w2"][i])
        args = _mixer_args(W, i)
        if kind == "gdn":
            dh, G["norm_mix_g"][i], gm = _gdn_bwd_layer(i, dh, sm, W["norm_mix_g"][i], *args)
        elif kind == "s5":
            dh, G["norm_mix_g"][i], gm = _s5_bwd_layer(dh, sm, W["norm_mix_g"][i], *args)
        else:
            dh, G["norm_mix_g"][i], gm = _m2_bwd_layer(dh, sm, W["norm_mix_g"][i], *args)
        for k, v in gm.items():
            mix.setdefault(kind + "_" + k, {})[MIXER_INDEX[i]] = v
    for k, d in mix.items():
        G[k] = [d[j] for j in sorted(d)]
    return loss, dh, {k: (v if k in BIG else jnp.stack(v)) if isinstance(v, list) else v for k, v in G.items()}


ADAM_ROWS = 128


def _adamw(name, w, g, m, v):
    R, C = w.shape
    tr = _tile(R, (ADAM_ROWS, SUBLANES))

    def body(w_ref, g_ref, m_ref, v_ref, d_ref, mo_ref, vo_ref):
        gg = g_ref[...]
        mn = ADAM_B1 * m_ref[...] + (1.0 - ADAM_B1) * gg
        vn = ADAM_B2 * v_ref[...] + (1.0 - ADAM_B2) * (gg * gg)
        m_hat = mn / (1.0 - ADAM_B1 ** ADAM_STEP)
        v_hat = vn / (1.0 - ADAM_B2 ** ADAM_STEP)
        d_ref[...] = -ADAM_LR * (m_hat / (jnp.sqrt(v_hat) + ADAM_EPS) + ADAM_WD * w_ref[...])
        mo_ref[...] = mn
        vo_ref[...] = vn

    blk = pl.BlockSpec((tr, C), lambda i: (i, 0))
    return pl.pallas_call(
        body, name=name, grid=(R // tr,), in_specs=[blk] * 4, out_specs=[blk] * 3,
        out_shape=[jax.ShapeDtypeStruct((R, C), F32)] * 3, compiler_params=_params("parallel"),
    )(w, g, m, v)


MESH = pl.DeviceIdType.MESH
ANY = pl.BlockSpec(memory_space=pl.ANY)
N_CHIPS = 4
N_DEV = 8


def _position():
    return lax.axis_index("x"), lax.axis_index("y"), lax.axis_index("c")


def _gather_shards(wps):
    n = len(wps)

    def body(*refs):
        w_refs, out_refs, (send_sems, recv_sems) = refs[:n], refs[n:2 * n], refs[2 * n:]
        x, y, c = _position()
        sibling = (x, y, 1 - c)
        chips = [(1 - x, y), (x, 1 - y), (1 - x, 1 - y)]
        firsts, passes = [], []
        for t, (w_ref, out_ref) in enumerate(zip(w_refs, out_refs)):
            half = w_ref.shape[0] // 2

            def piece(cx, cy, hc, out_ref=out_ref, half=half):
                return out_ref.at[2 * cx + cy, pl.ds(hc * half, half), :]

            def copy(k, src, dst, to, t=t):
                return pltpu.make_async_remote_copy(src_ref=src, dst_ref=dst, send_sem=send_sems.at[6 * t + k],
                                                    recv_sem=recv_sems.at[6 * t + k], device_id=to, device_id_type=MESH)

            first = [copy(j, w_ref.at[pl.ds(c * half, half), :], piece(x, y, c), (*chip, c)) for j, chip in enumerate(chips)]
            for cp in first:
                cp.start()
            firsts.append((first, piece, copy))
        for first, piece, copy in firsts:
            passed = [copy(3 + j, piece(*chip, c), piece(*chip, c), sibling) for j, chip in enumerate(chips)]
            for j, chip in enumerate(chips):
                copy(j, piece(*chip, c), piece(*chip, c), sibling).wait_recv()
                passed[j].start()
            passes.append(passed)
        for (first, piece, copy), passed in zip(firsts, passes):
            for j, chip in enumerate(chips):
                copy(3 + j, piece(*chip, 1 - c), piece(*chip, 1 - c), sibling).wait_recv()
            for cp in first + passed:
                cp.wait_send()

    return pl.pallas_call(
        body, name="gather_shards", in_specs=[ANY] * n, out_specs=[ANY] * n,
        out_shape=[jax.ShapeDtypeStruct((N_CHIPS, *wp.shape), wp.dtype) for wp in wps],
        scratch_shapes=[pltpu.SemaphoreType.DMA((6 * n,)), pltpu.SemaphoreType.DMA((6 * n,))],
    )(*wps)


def _pair_exchange(gps):
    n = len(gps)

    def body(*refs):
        g_refs, out_refs, (send_sems, recv_sems) = refs[:n], refs[n:2 * n], refs[2 * n:]
        x, y, c = _position()
        copies = []
        for t, (g_ref, out_ref) in enumerate(zip(g_refs, out_refs)):
            half = g_ref.shape[1] // 2
            copies += [pltpu.make_async_remote_copy(
                src_ref=g_ref.at[k, pl.ds((1 - c) * half, half), :], dst_ref=out_ref.at[k], send_sem=send_sems.at[N_CHIPS * t + k],
                recv_sem=recv_sems.at[N_CHIPS * t + k], device_id=(x, y, 1 - c), device_id_type=MESH) for k in range(N_CHIPS)]
        for cp in copies:
            cp.start()
        for cp in copies:
            cp.wait()

    return pl.pallas_call(
        body, name="pair_exchange", in_specs=[ANY] * n, out_specs=[ANY] * n,
        out_shape=[jax.ShapeDtypeStruct((N_CHIPS, gp.shape[1] // 2, gp.shape[2]), gp.dtype) for gp in gps],
        scratch_shapes=[pltpu.SemaphoreType.DMA((N_CHIPS * n,)), pltpu.SemaphoreType.DMA((N_CHIPS * n,))],
    )(*gps)


SUM_ROWS = (256, 128)


def _pair_sum(name, gp, got, core):
    n, R, C = gp.shape
    half = R // 2
    tr = _tile(half, SUM_ROWS)
    nb = half // tr

    def body(core_ref, g_ref, r_ref, o_ref):
        o_ref[...] = (g_ref[...].astype(F32) + r_ref[...].astype(F32)).astype(o_ref.dtype)

    return pl.pallas_call(
        body, name=name,
        grid_spec=pltpu.PrefetchScalarGridSpec(
            num_scalar_prefetch=1, grid=(n, nb),
            in_specs=[pl.BlockSpec((None, tr, C), lambda k, i, core_ref: (k, core_ref[0] * nb + i, 0)),
                      pl.BlockSpec((None, tr, C), lambda k, i, core_ref: (k, i, 0))],
            out_specs=pl.BlockSpec((None, tr, C), lambda k, i, core_ref: (k, i, 0))),
        out_shape=jax.ShapeDtypeStruct((n, half, C), gp.dtype), compiler_params=_params("parallel", "parallel"),
    )(core, gp, got)


def _chip_exchange(ts):
    n = len(ts)

    def body(*refs):
        t_refs, out_refs, (send_sems, recv_sems) = refs[:n], refs[n:2 * n], refs[2 * n:]
        x, y, c = _position()
        chips = [(1 - x, y), (x, 1 - y), (1 - x, 1 - y)]
        copies, waits = [], []
        for t, (t_ref, out_ref) in enumerate(zip(t_refs, out_refs)):
            for j, (cx, cy) in enumerate(chips):
                sems = dict(send_sem=send_sems.at[3 * t + j], recv_sem=recv_sems.at[3 * t + j], device_id=(cx, cy, c),
                            device_id_type=MESH)
                copies.append(pltpu.make_async_remote_copy(src_ref=t_ref.at[2 * cx + cy], dst_ref=out_ref.at[2 * x + y], **sems))
                waits.append(pltpu.make_async_remote_copy(src_ref=t_ref.at[2 * cx + cy], dst_ref=out_ref.at[2 * cx + cy], **sems))
        for cp in copies:
            cp.start()
        for cp in waits:
            cp.wait_recv()
        for cp in copies:
            cp.wait_send()

    return pl.pallas_call(
        body, name="chip_exchange", in_specs=[ANY] * n, out_specs=[ANY] * n,
        out_shape=[jax.ShapeDtypeStruct(t.shape, t.dtype) for t in ts],
        scratch_shapes=[pltpu.SemaphoreType.DMA((3 * n,)), pltpu.SemaphoreType.DMA((3 * n,))],
    )(*ts)


def _chip_sum(name, t, got, ids):
    n, H, C = t.shape
    tr = _tile(H, SUM_ROWS)
    nb = H // tr

    def body(ids_ref, t_ref, r_ref, o_ref):
        own = t_ref[...].astype(F32)
        acc = jnp.where(ids_ref[0] == 0, own, r_ref[0].astype(F32))
        for k in range(1, n):
            acc = acc + jnp.where(ids_ref[0] == k, own, r_ref[k].astype(F32))
        o_ref[...] = acc

    return pl.pallas_call(
        body, name=name,
        grid_spec=pltpu.PrefetchScalarGridSpec(
            num_scalar_prefetch=1, grid=(nb,),
            in_specs=[pl.BlockSpec((None, tr, C), lambda i, ids_ref: (ids_ref[0], i, 0)),
                      pl.BlockSpec((n, tr, C), lambda i, ids_ref: (0, i, 0))],
            out_specs=pl.BlockSpec((tr, C), lambda i, ids_ref: (ids_ref[1] * nb + i, 0))),
        out_shape=jax.ShapeDtypeStruct((2 * H, C), F32), compiler_params=_params("parallel"),
    )(ids, t, got)


def _sum_pieces(name, pieces):
    n, R, C = pieces.shape
    tr = _tile(R, (256, 128, SUBLANES))

    def body(p_ref, o_ref):
        acc = p_ref[0].astype(F32)
        for s in range(1, n):
            acc = acc + p_ref[s].astype(F32)
        o_ref[...] = acc

    return pl.pallas_call(
        body, name=name, grid=(R // tr,),
        in_specs=[pl.BlockSpec((n, tr, C), lambda i: (0, i, 0))], out_specs=pl.BlockSpec((tr, C), lambda i: (i, 0)),
        out_shape=jax.ShapeDtypeStruct((R, C), F32), compiler_params=_params("parallel"),
    )(pieces)


def _swap_halves(ss):
    n = len(ss)

    def body(*refs):
        s_refs, out_refs, (send_sems, recv_sems) = refs[:n], refs[n:2 * n], refs[2 * n:]
        x, y, c = _position()
        copies, waits = [], []
        for t, (s_ref, out_ref) in enumerate(zip(s_refs, out_refs)):
            half = s_ref.shape[0] // 2
            sems = dict(send_sem=send_sems.at[t], recv_sem=recv_sems.at[t], device_id=(x, y, 1 - c), device_id_type=MESH)
            mine = s_ref.at[pl.ds(c * half, half), :]
            copies.append(pltpu.make_async_remote_copy(src_ref=mine, dst_ref=out_ref.at[pl.ds(c * half, half), :], **sems))
            waits.append(pltpu.make_async_remote_copy(src_ref=mine, dst_ref=out_ref.at[pl.ds((1 - c) * half, half), :], **sems))
        for cp in copies:
            cp.start()
        for cp in waits:
            cp.wait_recv()
        for cp in copies:
            cp.wait_send()

    return pl.pallas_call(
        body, name="swap_halves", in_specs=[ANY] * n, out_specs=[ANY] * n, input_output_aliases={i: i for i in range(n)},
        out_shape=[jax.ShapeDtypeStruct(s_.shape, s_.dtype) for s_ in ss],
        scratch_shapes=[pltpu.SemaphoreType.DMA((n,)), pltpu.SemaphoreType.DMA((n,))],
    )(*ss)


def _gather_small(name, blk):
    m_per, n = blk.shape

    def body(x_ref, out_ref, send_sems, recv_sems, local_sem):
        x, y, c = _position()
        me, sibling = (x, y, c), (x, y, 1 - c)
        chips = [(1 - x, y), (x, 1 - y), (1 - x, 1 - y)]

        def rows(px, py, pc):
            return out_ref.at[pl.ds((4 * px + 2 * py + pc) * m_per, m_per), :]

        def copy(k, block, to, src=None):
            return pltpu.make_async_remote_copy(src_ref=rows(*block) if src is None else src, dst_ref=rows(*block),
                                                send_sem=send_sems.at[k], recv_sem=recv_sems.at[k], device_id=to, device_id_type=MESH)

        mine = pltpu.make_async_copy(x_ref, rows(*me), local_sem)
        mine.start()
        first = [copy(0, me, sibling, src=x_ref)] + [copy(1 + j, me, (*chip, c), src=x_ref) for j, chip in enumerate(chips)]
        for cp in first:
            cp.start()
        passed = [copy(4 + j, (*chip, c), sibling) for j, chip in enumerate(chips)]
        for j, chip in enumerate(chips):
            copy(1 + j, (*chip, c), me).wait_recv()
            passed[j].start()
        copy(0, sibling, me).wait_recv()
        for j, chip in enumerate(chips):
            copy(4 + j, (*chip, 1 - c), me).wait_recv()
        for cp in first + passed:
            cp.wait_send()
        mine.wait()

    return pl.pallas_call(
        body, name=name, out_shape=jax.ShapeDtypeStruct((N_DEV * m_per, n), blk.dtype),
        in_specs=[pl.BlockSpec(memory_space=pltpu.VMEM)], out_specs=pl.BlockSpec(memory_space=pltpu.VMEM),
        scratch_shapes=[pltpu.SemaphoreType.DMA((7,)), pltpu.SemaphoreType.DMA((7,)), pltpu.SemaphoreType.DMA],
        compiler_params=pltpu.CompilerParams(vmem_limit_bytes=VMEM_LIMIT_BYTES),
    )(blk)


WEIGHTS = ("norm_mix_g", "norm_mlp_g", "mlp_w1", "mlp_w2", "gdn_w_in", "gdn_conv_w", "gdn_a_log", "gdn_dt_bias", "gdn_o_norm_g",
           "gdn_w_out", "s5_w_in", "s5_lam_re", "s5_lam_im", "s5_log_dt", "s5_b_re", "s5_b_im", "s5_c_re", "s5_c_im", "s5_d",
           "s5_w_out", "m2_w_in", "m2_conv_w", "m2_conv_b", "m2_dt_bias", "m2_a_log", "m2_d", "m2_norm_g", "m2_w_out",
           "final_norm_g")
BIG = {"mlp_w1": 2, "mlp_w2": 1, "gdn_w_in": 2, "gdn_w_out": 1, "s5_w_in": 1, "s5_w_out": 2, "m2_w_in": 2, "m2_w_out": 1}
SMALL_CUT = {"gdn_conv_w": 2, "m2_conv_w": 2, "m2_conv_b": 1, "m2_norm_g": 1}
GROUP_A = ("mlp_w1", "mlp_w2", "gdn_w_out", "s5_w_in", "m2_w_out")
GROUPS = (GROUP_A, ("gdn_w_in",), ("m2_w_in",), ("s5_w_out",))


def _rows2d(a):
    return a.reshape(-1, a.shape[-1])


def _pack(arrays, cols, row_multiple, dtype):
    flat = jnp.concatenate([a.reshape(-1).astype(dtype) for a in arrays])
    n = -(-flat.shape[0] // (cols * row_multiple)) * cols * row_multiple
    return jnp.pad(flat, (0, n - flat.shape[0])).reshape(-1, cols)


def _unpack(packed, shapes):
    flat = packed.reshape(-1)
    out, off = [], 0
    for shp in shapes:
        n = math.prod(shp)
        out.append(flat[off:off + n].reshape(shp))
        off += n
    return out


def _split_rows(buf, shapes):
    out, off = [], 0
    for shp in shapes:
        rows = math.prod(shp[:-1])
        out.append(buf[off:off + rows].reshape(shp))
        off += rows
    return out


def _cut(a, axis, k):
    n = a.shape[axis] // N_CHIPS
    return lax.slice_in_dim(a, k * n, (k + 1) * n, axis=axis)


def kernel(x, norm_mix_g, norm_mlp_g, mlp_w1, mlp_w2, gdn_w_in, gdn_conv_w, gdn_a_log, gdn_dt_bias, gdn_o_norm_g, gdn_w_out, s5_w_in, s5_lam_re, s5_lam_im, s5_log_dt, s5_b_re, s5_b_im, s5_c_re, s5_c_im, s5_d, s5_w_out, m2_w_in, m2_conv_w, m2_conv_b, m2_dt_bias, m2_a_log, m2_d, m2_norm_g, m2_w_out, final_norm_g, loss_target, m_norm_mix_g, m_norm_mlp_g, m_mlp_w1, m_mlp_w2, m_gdn_w_in, m_gdn_conv_w, m_gdn_a_log, m_gdn_dt_bias, m_gdn_o_norm_g, m_gdn_w_out, m_s5_w_in, m_s5_lam_re, m_s5_lam_im, m_s5_log_dt, m_s5_b_re, m_s5_b_im, m_s5_c_re, m_s5_c_im, m_s5_d, m_s5_w_out, m_m2_w_in, m_m2_conv_w, m_m2_conv_b, m_m2_dt_bias, m_m2_a_log, m_m2_d, m_m2_norm_g, m_m2_w_out, m_final_norm_g, v_norm_mix_g, v_norm_mlp_g, v_mlp_w1, v_mlp_w2, v_gdn_w_in, v_gdn_conv_w, v_gdn_a_log, v_gdn_dt_bias, v_gdn_o_norm_g, v_gdn_w_out, v_s5_w_in, v_s5_lam_re, v_s5_lam_im, v_s5_log_dt, v_s5_b_re, v_s5_b_im, v_s5_c_re, v_s5_c_im, v_s5_d, v_s5_w_out, v_m2_w_in, v_m2_conv_w, v_m2_conv_b, v_m2_dt_bias, v_m2_a_log, v_m2_d, v_m2_norm_g, v_m2_w_out, v_final_norm_g):
    given = dict(locals())
    w = {n: given[n] for n in WEIGHTS}
    mom = {n: given["m_" + n] for n in WEIGHTS}
    var = {n: given["v_" + n] for n in WEIGHTS}
    big, small_cut = tuple(BIG), tuple(SMALL_CUT)
    small = tuple(n for n in WEIGHTS if n not in BIG)
    chip = 2 * lax.axis_index("x") + lax.axis_index("y")

    own = [jnp.concatenate([_rows2d(w[n]) for n in grp]).astype(BF16) for grp in GROUPS]
    shards = _gather_shards(own)
    W = {}
    for grp, mine, got in zip(GROUPS, own, shards):
        shapes = [w[n].shape for n in grp]
        per_chip = [_split_rows(jnp.where(chip == k, mine, got[k]), shapes) for k in range(N_CHIPS)]
        W.update({n: jnp.concatenate([per_chip[k][i] for k in range(N_CHIPS)], axis=BIG[n]) for i, n in enumerate(grp)})
    W["gdn_w_in"] = jnp.pad(W["gdn_w_in"], ((0, 0), (0, 0), (0, GDN_IN_PAD - GDN_IN)))
    W["m2_w_in"] = jnp.pad(W["m2_w_in"], ((0, 0), (0, 0), (0, M2_IN_PAD - M2_IN)))
    cut_blk = _pack([w[n] for n in small_cut], LANES, SUBLANES, F32)
    cut_all = _gather_small("gather_small_params", cut_blk).reshape(N_DEV, *cut_blk.shape)
    per_chip = [_unpack(cut_all[2 * k], [w[n].shape for n in small_cut]) for k in range(N_CHIPS)]
    W.update({n: jnp.concatenate([per_chip[k][i] for k in range(N_CHIPS)], axis=SMALL_CUT[n]) for i, n in enumerate(small_cut)})
    W.update({n: w[n] for n in small if n not in SMALL_CUT})

    loss, grad_x, G = _local_step(x[0], loss_target[0], W)
    loss = lax.psum(loss[0, 0], ("x", "y", "c"))

    gps = [jnp.stack([jnp.concatenate([_cut(g, BIG[n] - 1, k) for n in grp for g in G[n]]).astype(BF16) for k in range(N_CHIPS)])
           for grp in GROUPS]
    core = lax.axis_index("c").astype(jnp.int32)
    ids = jnp.stack([chip.astype(jnp.int32), core])
    pairs = [_pair_sum(f"pair_sum{i}", gp, got, core.reshape(1)) for i, (gp, got) in enumerate(zip(gps, _pair_exchange(gps)))]
    sums = [_chip_sum(f"chip_sum{i}", t, got, ids) for i, (t, got) in enumerate(zip(pairs, _chip_exchange(pairs)))]
    grads = {}
    for grp, g_shard in zip(GROUPS, _swap_halves(sums)):
        grads.update(zip(grp, _split_rows(g_shard, [w[n].shape for n in grp])))
    sg = _pack([G[n] for n in small], LANES, ADAM_ROWS, F32)
    sg_sum = _sum_pieces("sum_small_grads", _gather_small("gather_small_grads", sg).reshape(N_DEV, *sg.shape))
    for n, g in zip(small, _unpack(sg_sum, [G[n].shape for n in small])):
        if n in SMALL_CUT:
            width = g.shape[SMALL_CUT[n]] // N_CHIPS
            g = lax.dynamic_slice_in_dim(g, chip * width, width, axis=SMALL_CUT[n])
        grads[n] = g.reshape(w[n].shape)

    delta, new_m, new_v = {}, {}, {}
    for n in big:
        as2d = lambda a: a.reshape(-1, a.shape[-1])
        outs = _adamw("adamw_" + n, as2d(w[n]), as2d(grads[n]), as2d(mom[n]), as2d(var[n]))
        delta[n], new_m[n], new_v[n] = (o.reshape(w[n].shape) for o in outs)
    packs = [_pack([t[n] for n in small], LANES, ADAM_ROWS, F32) for t in (w, grads, mom, var)]
    outs = _adamw("adamw_small", *packs)
    for t, o in zip((delta, new_m, new_v), outs):
        t.update(zip(small, _unpack(o, [w[n].shape for n in small])))

    return (loss, grad_x[None], *[grads[n] for n in WEIGHTS], *[delta[n] for n in WEIGHTS], *[new_m[n] for n in WEIGHTS],
            *[new_v[n] for n in WEIGHTS])
```

```python
import functools
import math

import numpy as np
import jax
import jax.numpy as jnp
from jax import lax
from jax.experimental import pallas as pl
from jax.experimental.pallas import tpu as pltpu
from jax.experimental.pallas import tpu_sc as plsc

F32 = jnp.float32
BF16 = jnp.bfloat16

D_MODEL = 1024
D_FF = 4096
DEPTH = 4
CHUNK = 64
RMS_EPS = 1e-6
CONV_W = 4
GDN_HEADS = 8
GDN_DK = 128
GDN_IN = 4112
GDN_IN_PAD = 4224
S5_GROUPS = 64
S5_STATE = 64
S5_GROUP = 16
S5_BLOCKS = 8
M2_INNER = 2048
M2_HEADS = 32
M2_GROUPS = 8
M2_STATE = 128
M2_CONV_CH = 4096
M2_IN = 6176
M2_IN_PAD = 6272
ADAM_LR, ADAM_B1, ADAM_B2, ADAM_EPS, ADAM_WD, ADAM_STEP = 0.001, 0.9, 0.999, 1e-08, 0.01, 10

VMEM_LIMIT_BYTES = 56 * 1024 * 1024
SUBLANES = 8
LANES = 128


def _params(*sem):
    return pltpu.CompilerParams(dimension_semantics=tuple(sem) if sem else None, vmem_limit_bytes=VMEM_LIMIT_BYTES)


NN, NT, TN = ((1,), (0,)), ((1,), (1,)), ((0,), (0,))
_DOT_TRANSPOSES = {NN: ((NT, "gb"), (TN, "ag")), NT: ((NN, "gb"), (TN, "ga")), TN: ((NT, "bg"), (NN, "ag"))}


def _dg(a, b, dims):
    if a.ndim == 3:
        dn = (((dims[0][0] + 1,), (dims[1][0] + 1,)), ((0,), (0,)))
    else:
        dn = (dims, ((), ()))
    return lax.dot_general(a, b, dn, preferred_element_type=F32)


def _mxu(a, b, dims):
    return _dg(a.astype(BF16), b.astype(BF16), dims)


@functools.partial(jax.custom_vjp, nondiff_argnums=(2,))
def _dot(a, b, dims=NN):
    return _mxu(a, b, dims)


def _dot_fwd(a, b, dims):
    return _mxu(a, b, dims), (a, b)


def _dot_bwd(dims, res, g):
    ops = dict(a=res[0], b=res[1], g=g)
    (da_dims, da_ops), (db_dims, db_ops) = _DOT_TRANSPOSES[dims]
    return (_mxu(ops[da_ops[0]], ops[da_ops[1]], da_dims).astype(res[0].dtype),
            _mxu(ops[db_ops[0]], ops[db_ops[1]], db_dims).astype(res[1].dtype))


_dot.defvjp(_dot_fwd, _dot_bwd)


def _nt(a, b):
    return _dot(a, b, NT)


def _tn(a, b):
    return _dot(a, b, TN)


def _split3(x):
    x1 = x.astype(BF16)
    r = x - x1.astype(F32)
    x2 = r.astype(BF16)
    return x1, x2, (r - x2.astype(F32)).astype(BF16)


def _sel_mxu(x, sel, dims, x_first):
    f = (lambda p: _dg(p, sel.astype(BF16), dims)) if x_first else (lambda p: _dg(sel.astype(BF16), p, dims))
    x1, x2, x3 = _split3(x)
    return f(x1) + (f(x2) + f(x3))


@jax.custom_vjp
def _pick(x, sel):
    return _sel_mxu(x, sel, NN, True)


def _pick_fwd(x, sel):
    return _sel_mxu(x, sel, NN, True), sel


def _pick_bwd(sel, g):
    return _sel_mxu(g, sel, NT, True), jnp.zeros_like(sel)


_pick.defvjp(_pick_fwd, _pick_bwd)


@jax.custom_vjp
def _accum(sel, x):
    return _sel_mxu(x, sel, NN, False)


def _accum_fwd(sel, x):
    return _sel_mxu(x, sel, NN, False), sel


def _accum_bwd(sel, g):
    return jnp.zeros_like(sel), _sel_mxu(g, sel, TN, False)


_accum.defvjp(_accum_fwd, _accum_bwd)


def _dot3(a, b, dims=NN):
    ah, bh = a.astype(BF16), b.astype(BF16)
    al, bl = (a - ah.astype(F32)).astype(BF16), (b - bh.astype(F32)).astype(BF16)
    return _dg(ah, bh, dims) + (_dg(ah, bl, dims) + _dg(al, bh, dims))


def _neumann(x, r, dims):
    r = r + _dot3(x, r, dims)
    for _ in range(5):
        x = _dot3(x, x)
        r = r + _dot3(x, r, dims)
    return r


@jax.custom_vjp
def _unit_lower_solve(a, rhs):
    return _neumann(-a, rhs, NN)


def _unit_lower_solve_fwd(a, rhs):
    sol = _neumann(-a, rhs, NN)
    return sol, (a, sol)


def _unit_lower_solve_bwd(res, ct):
    a, sol = res
    d_rhs = _neumann(-a, ct, TN)
    return -_dot3(d_rhs, sol, NT), d_rhs


_unit_lower_solve.defvjp(_unit_lower_solve_fwd, _unit_lower_solve_bwd)


def _sigmoid(x):
    return 1.0 / (1.0 + jnp.exp(-x))


def _softplus(x):
    return jnp.maximum(x, 0.0) + jnp.log(1.0 + jnp.exp(-jnp.abs(x)))


def _iota2(shape, axis):
    return lax.broadcasted_iota(jnp.int32, shape, axis)


def _tile(n, cands):
    for c in cands:
        if n % c == 0:
            return c
    return n


MM_TILE_BYTES = 9 * 1024 * 1024


def _mm(name, a, b, mode, out_dtypes, epi=None, extras=(), tn=None):
    if mode == "nn":
        (M, K), N = a.shape, b.shape[1]
    elif mode == "nt":
        (M, K), N = a.shape, b.shape[0]
    else:
        (K, M), N = a.shape, b.shape[1]
    tn = tn or _tile(N, (512, 384, 896, 256, 128))
    out_bytes = tn * (sum(jnp.dtype(d).itemsize for d in out_dtypes) + sum(e.dtype.itemsize for e in extras))
    fits = lambda t: t * K * a.dtype.itemsize <= MM_TILE_BYTES and t * out_bytes <= MM_TILE_BYTES
    tm = next(t for t in (2048, 1024, 512, 256, 128) if M % t == 0 and (fits(t) or t == 128))
    if mode == "nn":
        a_spec, b_spec = pl.BlockSpec((tm, K), lambda i, j: (i, 0)), pl.BlockSpec((K, tn), lambda i, j: (0, j))
        dims = NN
    elif mode == "nt":
        a_spec, b_spec = pl.BlockSpec((tm, K), lambda i, j: (i, 0)), pl.BlockSpec((tn, K), lambda i, j: (j, 0))
        dims = NT
    else:
        a_spec, b_spec = pl.BlockSpec((K, tm), lambda i, j: (0, i)), pl.BlockSpec((K, tn), lambda i, j: (0, j))
        dims = TN
    n_ex = len(extras)

    def body(a_ref, b_ref, *rest):
        acc = _mxu(a_ref[...], b_ref[...], dims)
        res = epi(acc, *[e[...] for e in rest[:n_ex]]) if epi is not None else (acc,)
        for o_ref, r in zip(rest[n_ex:], res):
            o_ref[...] = r.astype(o_ref.dtype)

    tile = pl.BlockSpec((tm, tn), lambda i, j: (i, j))
    out = pl.pallas_call(
        body, name=name, grid=(M // tm, N // tn),
        in_specs=[a_spec, b_spec] + [tile] * n_ex,
        out_specs=[tile] * len(out_dtypes),
        out_shape=[jax.ShapeDtypeStruct((M, N), d) for d in out_dtypes],
        compiler_params=_params("parallel", "parallel"),
    )(a, b, *extras)
    return out if len(out_dtypes) > 1 else out[0]


def _rms_fwd(name, h, g):
    L, D = h.shape
    tr = _tile(L, (256, 128))

    def body(h_ref, g_ref, o_ref):
        x = h_ref[...]
        r = lax.rsqrt(jnp.mean(x * x, axis=-1, keepdims=True) + RMS_EPS)
        o_ref[...] = (x * r * g_ref[...]).astype(o_ref.dtype)

    return pl.pallas_call(
        body, name=name, grid=(L // tr,),
        in_specs=[pl.BlockSpec((tr, D), lambda i: (i, 0)), pl.BlockSpec((1, D), lambda i: (0, 0))],
        out_specs=pl.BlockSpec((tr, D), lambda i: (i, 0)),
        out_shape=jax.ShapeDtypeStruct((L, D), BF16),
        compiler_params=_params("parallel"),
    )(h, g.reshape(1, D))


def _rms_bwd(name, h, g, dhn, dres):
    L, D = h.shape
    tr = _tile(L, (256, 128))

    def body(h_ref, g_ref, dhn_ref, dres_ref, dh_ref, dg_ref):
        x = h_ref[...]
        r = lax.rsqrt(jnp.mean(x * x, axis=-1, keepdims=True) + RMS_EPS)
        xh = x * r
        dy = dhn_ref[...]
        dxh = dy * g_ref[...]
        dh_ref[...] = dres_ref[...] + r * (dxh - xh * jnp.mean(dxh * xh, axis=-1, keepdims=True))

        @pl.when(pl.program_id(0) == 0)
        def _():
            dg_ref[...] = jnp.zeros_like(dg_ref)

        dg_ref[...] += jnp.sum(dy * xh, axis=0, keepdims=True)

    row = pl.BlockSpec((tr, D), lambda i: (i, 0))
    vec = pl.BlockSpec((1, D), lambda i: (0, 0))
    return pl.pallas_call(
        body, name=name, grid=(L // tr,),
        in_specs=[row, vec, row, row], out_specs=[row, vec],
        out_shape=[jax.ShapeDtypeStruct((L, D), F32), jax.ShapeDtypeStruct((1, D), F32)],
        compiler_params=_params("arbitrary"),
    )(h, g.reshape(1, D), dhn, dres)


def _loss_head(h, g, target):
    L, D = h.shape
    tr = _tile(L, (256, 128))

    def body(h_ref, g_ref, t_ref, loss_ref, dh_ref, dg_ref):
        x = h_ref[...]
        r = lax.rsqrt(jnp.mean(x * x, axis=-1, keepdims=True) + RMS_EPS)
        xh = x * r
        err = xh * g_ref[...] - t_ref[...]
        dy = err * (1.0 / D)
        dxh = dy * g_ref[...]
        dh_ref[...] = r * (dxh - xh * jnp.mean(dxh * xh, axis=-1, keepdims=True))

        @pl.when(pl.program_id(0) == 0)
        def _():
            dg_ref[...] = jnp.zeros_like(dg_ref)
            loss_ref[...] = jnp.zeros_like(loss_ref)

        dg_ref[...] += jnp.sum(dy * xh, axis=0, keepdims=True)
        loss_ref[...] += (0.5 / D) * jnp.sum(jnp.sum(err * err, axis=-1, keepdims=True), axis=0, keepdims=True)

    row = pl.BlockSpec((tr, D), lambda i: (i, 0))
    vec = pl.BlockSpec((1, D), lambda i: (0, 0))
    return pl.pallas_call(
        body, name="loss_head", grid=(L // tr,),
        in_specs=[row, vec, row], out_specs=[pl.BlockSpec((1, 1), lambda i: (0, 0)), row, vec],
        out_shape=[jax.ShapeDtypeStruct((1, 1), F32), jax.ShapeDtypeStruct((L, D), F32), jax.ShapeDtypeStruct((1, D), F32)],
        compiler_params=_params("arbitrary"),
    )(h, g.reshape(1, D), target)


def _glu_fwd(h, ag):
    L, D = h.shape
    tr = _tile(L, (256, 128))

    def body(h_ref, v_ref, g_ref, o_ref):
        o_ref[...] = h_ref[...] + v_ref[...] * _sigmoid(g_ref[...])

    return pl.pallas_call(
        body, name="s5_glu_fwd", grid=(L // tr,),
        in_specs=[pl.BlockSpec((tr, D), lambda i: (i, 0)), pl.BlockSpec((tr, D), lambda i: (i, 0)),
                  pl.BlockSpec((tr, D), lambda i: (i, 1))],
        out_specs=pl.BlockSpec((tr, D), lambda i: (i, 0)),
        out_shape=jax.ShapeDtypeStruct((L, D), F32),
        compiler_params=_params("parallel"),
    )(h, ag, ag)


def _glu_bwd(dh, ag):
    L, D = dh.shape
    tr = _tile(L, (256, 128))

    def body(dh_ref, v_ref, g_ref, dv_ref, dg_ref):
        s = _sigmoid(g_ref[...])
        d = dh_ref[...]
        dv_ref[...] = d * s
        dg_ref[...] = d * v_ref[...] * s * (1.0 - s)

    dv, dg = pl.pallas_call(
        body, name="s5_glu_bwd", grid=(L // tr,),
        in_specs=[pl.BlockSpec((tr, D), lambda i: (i, 0)), pl.BlockSpec((tr, D), lambda i: (i, 0)),
                  pl.BlockSpec((tr, D), lambda i: (i, 1))],
        out_specs=[pl.BlockSpec((tr, D), lambda i: (i, 0))] * 2,
        out_shape=[jax.ShapeDtypeStruct((L, D), F32)] * 2,
        compiler_params=_params("parallel"),
    )(dh, ag, ag)
    return jnp.concatenate([dv, dg], axis=1)


CONV_ROWS = 128
CONV_COLS = 512


def _shift_rows(cat, s):
    if s == 0:
        return cat[SUBLANES:, :]
    return pltpu.roll(cat, s, axis=0)[SUBLANES:, :]


def _conv_fwd(name, p, col0, w, b):
    L = p.shape[0]
    C = w.shape[1]
    tc = _tile(C, (CONV_COLS, 256))
    cb0 = col0 // tc
    nr = L // CONV_ROWS

    def body(x_ref, w_ref, b_ref, o_ref):
        def step(r, carry):
            r0 = pl.multiple_of(r * CONV_ROWS, CONV_ROWS)
            cur = x_ref[pl.ds(r0, CONV_ROWS), :]
            p0 = pl.multiple_of(jnp.maximum(r0 - SUBLANES, 0), SUBLANES)
            prev = jnp.where(r > 0, x_ref[pl.ds(p0, SUBLANES), :], 0.0)
            cat = jnp.concatenate([prev, cur], axis=0)
            acc = b_ref[...] + w_ref[3:4, :] * cur
            for k in range(CONV_W - 1):
                acc = acc + w_ref[k:k + 1, :] * _shift_rows(cat, CONV_W - 1 - k)
            o_ref[pl.ds(r0, CONV_ROWS), :] = acc * _sigmoid(acc)
            return carry

        lax.fori_loop(0, nr, step, 0)

    return pl.pallas_call(
        body, name=name, grid=(C // tc,),
        in_specs=[pl.BlockSpec((L, tc), lambda j: (0, cb0 + j)), pl.BlockSpec((CONV_W, tc), lambda j: (0, j)),
                  pl.BlockSpec((1, tc), lambda j: (0, j))],
        out_specs=pl.BlockSpec((L, tc), lambda j: (0, j)),
        out_shape=jax.ShapeDtypeStruct((L, C), F32),
        compiler_params=_params("parallel"),
    )(p, w, b)


def _conv_bwd(name, p, col0, w, b, dout):
    L = p.shape[0]
    C = w.shape[1]
    tc = _tile(C, (CONV_COLS, 256))
    cb0 = col0 // tc
    nr = L // CONV_ROWS

    def body(x_ref, w_ref, b_ref, do_ref, dx_ref, dw_ref, db_ref, dpre_ref):
        def step1(r, carry):
            dw0, dw1, dw2, dw3, dbb = carry
            r0 = pl.multiple_of(r * CONV_ROWS, CONV_ROWS)
            cur = x_ref[pl.ds(r0, CONV_ROWS), :]
            p0 = pl.multiple_of(jnp.maximum(r0 - SUBLANES, 0), SUBLANES)
            prev = jnp.where(r > 0, x_ref[pl.ds(p0, SUBLANES), :], 0.0)
            cat = jnp.concatenate([prev, cur], axis=0)
            sh = [_shift_rows(cat, CONV_W - 1 - k) for k in range(CONV_W - 1)] + [cur]
            acc = b_ref[...] + w_ref[3:4, :] * cur
            for k in range(CONV_W - 1):
                acc = acc + w_ref[k:k + 1, :] * sh[k]
            sg = _sigmoid(acc)
            dpre = do_ref[pl.ds(r0, CONV_ROWS), :] * (sg + acc * sg * (1.0 - sg))
            dpre_ref[pl.ds(r0, CONV_ROWS), :] = dpre
            dws = [d + jnp.sum(dpre * s, axis=0, keepdims=True) for d, s in zip((dw0, dw1, dw2, dw3), sh)]
            return (*dws, dbb + jnp.sum(dpre, axis=0, keepdims=True))

        z = jnp.zeros((1, tc), F32)
        dw0, dw1, dw2, dw3, dbb = lax.fori_loop(0, nr, step1, (z, z, z, z, z))
        dw_ref[...] = jnp.concatenate([dw0, dw1, dw2, dw3, z, z, z, z], axis=0)
        db_ref[...] = dbb

        def step2(r, carry):
            r0 = pl.multiple_of(r * CONV_ROWS, CONV_ROWS)
            cur = dpre_ref[pl.ds(r0, CONV_ROWS), :]
            n0 = pl.multiple_of(jnp.minimum(r0 + CONV_ROWS, L - SUBLANES), SUBLANES)
            nxt = jnp.where(r < nr - 1, dpre_ref[pl.ds(n0, SUBLANES), :], 0.0)
            cat = jnp.concatenate([cur, nxt], axis=0)
            acc = w_ref[3:4, :] * cur
            for k in range(CONV_W - 1):
                s = CONV_W - 1 - k
                acc = acc + w_ref[k:k + 1, :] * pltpu.roll(cat, CONV_ROWS + SUBLANES - s, axis=0)[:CONV_ROWS, :]
            dx_ref[pl.ds(r0, CONV_ROWS), :] = acc
            return carry

        lax.fori_loop(0, nr, step2, 0)

    dx, dw, db = pl.pallas_call(
        body, name=name, grid=(C // tc,),
        in_specs=[pl.BlockSpec((L, tc), lambda j: (0, cb0 + j)), pl.BlockSpec((CONV_W, tc), lambda j: (0, j)),
                  pl.BlockSpec((1, tc), lambda j: (0, j)), pl.BlockSpec((L, tc), lambda j: (0, j))],
        out_specs=[pl.BlockSpec((L, tc), lambda j: (0, j)), pl.BlockSpec((SUBLANES, tc), lambda j: (0, j)),
                   pl.BlockSpec((1, tc), lambda j: (0, j))],
        out_shape=[jax.ShapeDtypeStruct((L, C), F32), jax.ShapeDtypeStruct((SUBLANES, C), F32),
                   jax.ShapeDtypeStruct((1, C), F32)],
        scratch_shapes=[pltpu.VMEM((L, tc), F32)],
        compiler_params=_params("parallel"),
    )(p, w, b, dout)
    return dx, dw[:CONV_W], db


def _chunk_consts():
    r, c = _iota2((CHUNK, CHUNK), 0), _iota2((CHUNK, CHUNK), 1)
    causal = r >= c
    return causal, r > c, (r == c).astype(F32), causal.astype(F32), jnp.ones((CHUNK, CHUNK), F32)


def _by_lanes(t):
    return jnp.concatenate([t[i] for i in range(t.shape[0])], axis=1)


def _by_batch(t, w):
    return jnp.concatenate([t[None, :, i * w:(i + 1) * w] for i in range(t.shape[1] // w)], axis=0)


def _diag_lanes():
    return (_iota2((CHUNK, LANES), 0) == _iota2((CHUNK, LANES), 1)).astype(F32)


def _gdn_chunk(q, k, v, ab, gate, S, alog, dtb, og, ea, eb):
    causal, strict, _, tril, ones = _chunk_consts()
    logits = _by_batch(_pick(ab, jnp.concatenate([_by_lanes(ea), _by_lanes(eb)], axis=1)), LANES)
    H = q.shape[0]
    g = -jnp.exp(alog) * _softplus(logits[:H] + dtb)
    beta = _sigmoid(logits[H:])
    qn = q * lax.rsqrt(jnp.sum(q * q, axis=-1, keepdims=True) + 1e-6) * (GDN_DK ** -0.5)
    kn = k * lax.rsqrt(jnp.sum(k * k, axis=-1, keepdims=True) + 1e-6)
    g_l = _by_lanes(g)
    gc = _by_batch(_accum(tril, g_l), LANES)
    glast = _by_batch(_accum(ones, g_l), LANES)
    gcol = gc[:, :, :CHUNK]
    grow = _by_batch(_accum(ones, _by_lanes(gc * _diag_lanes())), LANES)[:, :, :CHUNK]
    decay = jnp.exp(jnp.where(causal, gcol - grow, -jnp.inf))
    a = jnp.where(strict, beta[:, :, :CHUNK] * _nt(kn, kn) * decay, 0.0)
    eg = jnp.exp(gc)
    sol = _unit_lower_solve(a, jnp.concatenate([v * beta, kn * (beta * eg)], axis=2))
    u, w = sol[:, :, :GDN_DK], sol[:, :, GDN_DK:]
    qk = _nt(qn, kn) * decay
    v_new = u - _dot(w, S)
    o = _dot(qn * eg, S) + _dot(qk, v_new)
    cd = jnp.exp(glast)
    s_new = jnp.concatenate([cd, cd], axis=1) * S + _tn(kn * jnp.exp(glast - gc), v_new)
    on = o * lax.rsqrt(jnp.mean(o * o, axis=-1, keepdims=True) + RMS_EPS) * og
    return on * (gate * _sigmoid(gate)), s_new


GDN_HB = 8


def _gdn_specs(nc, rev):
    cm = (lambda c: nc - 1 - c) if rev else (lambda c: c)
    blk = lambda off: pl.BlockSpec((CHUNK, GDN_HB * GDN_DK), lambda c, h: (cm(c), off // GDN_HB + h))
    ab = pl.BlockSpec((CHUNK, LANES), lambda c, h: (cm(c), (GDN_IN_PAD - LANES) // LANES))
    hv = pl.BlockSpec((GDN_HB, 1, LANES), lambda c, h: (h, 0, 0))
    og = pl.BlockSpec((1, LANES), lambda c, h: (0, 0))
    em = pl.BlockSpec((GDN_HB, LANES, LANES), lambda c, h: (h, 0, 0))
    st = pl.BlockSpec((None, GDN_HB, GDN_DK, GDN_DK), lambda c, h: (cm(c), h, 0, 0))
    return blk, ab, hv, og, em, st


def _gdn_fwd(qc, kc, vc, p, alog_e, dtb_e, og, ea, eb):
    L = qc.shape[0]
    nc = L // CHUNK
    blk, ab, hv, ogs, em, st = _gdn_specs(nc, False)

    def body(q_ref, k_ref, v_ref, gate_ref, ab_ref, al_ref, dt_ref, og_ref, ea_ref, eb_ref, y_ref, sp_ref, s_scr):
        c, h = pl.program_id(0), pl.program_id(1)
        lanes = [slice(i * GDN_DK, (i + 1) * GDN_DK) for i in range(GDN_HB)]
        heads = pl.ds(h * GDN_HB, GDN_HB)
        stack = lambda ref: jnp.concatenate([ref[:, ls][None] for ls in lanes], axis=0)

        @pl.when(c == 0)
        def _():
            s_scr[heads] = jnp.zeros((GDN_HB, GDN_DK, GDN_DK), F32)

        S = s_scr[heads]
        sp_ref[...] = S
        y, s_new = _gdn_chunk(stack(q_ref), stack(k_ref), stack(v_ref), ab_ref[...], stack(gate_ref), S,
                              al_ref[...], dt_ref[...], og_ref[...], ea_ref[...], eb_ref[...])
        for i, ls in enumerate(lanes):
            y_ref[:, ls] = y[i]
        s_scr[heads] = s_new

    return pl.pallas_call(
        body, name="gdn_fwd", grid=(nc, GDN_HEADS // GDN_HB),
        in_specs=[blk(0), blk(0), blk(0), blk(3 * GDN_HEADS), ab, hv, hv, ogs, em, em],
        out_specs=[blk(0), st],
        out_shape=[jax.ShapeDtypeStruct((L, D_MODEL), F32), jax.ShapeDtypeStruct((nc, GDN_HEADS, GDN_DK, GDN_DK), F32)],
        scratch_shapes=[pltpu.VMEM((GDN_HEADS, GDN_DK, GDN_DK), F32)],
        compiler_params=_params("arbitrary", "arbitrary"),
    )(qc, kc, vc, p, p, alog_e, dtb_e, og, ea, eb)


def _gdn_bwd(qc, kc, vc, p, alog_e, dtb_e, og, ea, eb, sprev, dy):
    L = qc.shape[0]
    nc = L // CHUNK
    blk, ab, hv, ogs, em, st = _gdn_specs(nc, True)

    def body(q_ref, k_ref, v_ref, gate_ref, ab_ref, al_ref, dt_ref, og_ref, ea_ref, eb_ref, sp_ref, dy_ref,
             dq_ref, dk_ref, dv_ref, dgate_ref, dab_ref, dpar_ref, ds_scr):
        c, h = pl.program_id(0), pl.program_id(1)
        lanes = [slice(i * GDN_DK, (i + 1) * GDN_DK) for i in range(GDN_HB)]
        heads = pl.ds(h * GDN_HB, GDN_HB)
        stack = lambda ref: jnp.concatenate([ref[:, ls][None] for ls in lanes], axis=0)

        @pl.when(c == 0)
        def _():
            ds_scr[heads] = jnp.zeros((GDN_HB, GDN_DK, GDN_DK), F32)
            dpar_ref[heads] = jnp.zeros((GDN_HB, SUBLANES, LANES), F32)

        @pl.when(h == 0)
        def _():
            dab_ref[...] = jnp.zeros_like(dab_ref)

        ea_m, eb_m = ea_ref[...], eb_ref[...]
        f = lambda q, k, v, a_b, gate, S, al, dt, o_g: _gdn_chunk(q, k, v, a_b, gate, S, al, dt, o_g, ea_m, eb_m)
        _, vjp = jax.vjp(f, stack(q_ref), stack(k_ref), stack(v_ref), ab_ref[...], stack(gate_ref), sp_ref[...],
                         al_ref[...], dt_ref[...], og_ref[...])
        dq, dk, dv, dab, dgate, ds, dal, ddt, dog = vjp((stack(dy_ref), ds_scr[heads]))
        for i, ls in enumerate(lanes):
            dq_ref[:, ls] = dq[i]
            dk_ref[:, ls] = dk[i]
            dv_ref[:, ls] = dv[i]
            dgate_ref[:, ls] = dgate[i]
        ds_scr[heads] = ds
        dab_ref[...] += dab
        first = _iota2((GDN_HB, 1, LANES), 0) == 0
        dpar_ref[heads] += jnp.concatenate([dal, ddt, jnp.where(first, dog[None], 0.0),
                                            jnp.zeros((GDN_HB, SUBLANES - 3, LANES), F32)], axis=1)

    return pl.pallas_call(
        body, name="gdn_bwd", grid=(nc, GDN_HEADS // GDN_HB),
        in_specs=[blk(0), blk(0), blk(0), blk(3 * GDN_HEADS), ab, hv, hv, ogs, em, em, st, blk(0)],
        out_specs=[blk(0), blk(0), blk(0), blk(0), pl.BlockSpec((CHUNK, LANES), lambda c, h: (nc - 1 - c, 0)),
                   pl.BlockSpec((GDN_HEADS, SUBLANES, LANES), lambda c, h: (0, 0, 0))],
        out_shape=[jax.ShapeDtypeStruct((L, D_MODEL), F32)] * 4
        + [jax.ShapeDtypeStruct((L, LANES), F32), jax.ShapeDtypeStruct((GDN_HEADS, SUBLANES, LANES), F32)],
        scratch_shapes=[pltpu.VMEM((GDN_HEADS, GDN_DK, GDN_DK), F32)],
        compiler_params=_params("arbitrary", "arbitrary"),
    )(qc, kc, vc, p, p, alog_e, dtb_e, og, ea, eb, sprev, dy)


def _gdn_selectors():
    rows = np.arange(LANES)[None, :, None]
    heads = np.arange(GDN_HEADS)[:, None, None]
    ea = np.broadcast_to(rows == heads, (GDN_HEADS, LANES, LANES)).astype(np.float32)
    eb = np.broadcast_to(rows == heads + GDN_HEADS, (GDN_HEADS, LANES, LANES)).astype(np.float32)
    return jnp.asarray(ea), jnp.asarray(eb)


M2_GW = M2_INNER // M2_GROUPS
M2_HPG = M2_HEADS // M2_GROUPS
M2_HD = M2_INNER // M2_HEADS


def _m2_chunk(x, bm, cm, z, dtr, st, dtb, alog, dsk, ng, e, ecol):
    G = x.shape[0]
    causal, _, _, tril, ones = _chunk_consts()
    dt_n = _softplus(dtr + dtb)
    da_n = dt_n * (-jnp.exp(alog))
    cum_n = _accum(tril, da_n)
    tot_n = _accum(ones, da_n)
    wide = _pick(jnp.concatenate([dt_n, cum_n, tot_n], axis=0), e)
    dt_w, cum_w, tot_w = (_by_batch(wide[i * CHUNK:(i + 1) * CHUNK], M2_GW) for i in range(3))
    xdt = x * dt_w
    cb = _nt(cm, bm)
    heads = lambda t: jnp.concatenate([t[i:i + 1] for i in range(G) for _ in range(M2_HPG)], axis=0)
    colb = _by_batch(_pick(cum_n, ecol), LANES)
    rowb = _by_batch(_accum(ones, _by_lanes(colb * _diag_lanes())), LANES)
    lmat = jnp.exp(jnp.where(causal, colb[:, :, :CHUNK] - rowb[:, :, :CHUNK], -jnp.inf))
    yr = _dot(heads(cb) * lmat, heads(xdt))
    head = _iota2((CHUNK, M2_GW), 1) // M2_HD
    ydiag = jnp.concatenate([sum(jnp.where(head == r, yr[i * M2_HPG + r], 0.0) for r in range(M2_HPG))[None] for i in range(G)], axis=0)
    st_new = _tn(bm, xdt * jnp.exp(tot_w - cum_w))
    cd = jnp.exp(tot_w)
    s_new = jnp.concatenate([cd, cd], axis=1) * st + st_new
    y = ydiag + _dot(cm, st) * jnp.exp(cum_w) + dsk * x
    y = y * (z * _sigmoid(z))
    yn = y * lax.rsqrt(jnp.mean(y * y, axis=-1, keepdims=True) + RMS_EPS) * ng
    return yn, s_new


M2_GB = 4


def _m2_specs(nc, rev):
    cm = (lambda c: nc - 1 - c) if rev else (lambda c: c)
    wide = lambda off: pl.BlockSpec((CHUNK, M2_GB * M2_GW), lambda c, g: (cm(c), off // M2_GB + g))
    nar = lambda off: pl.BlockSpec((CHUNK, M2_GB * LANES), lambda c, g: (cm(c), off // M2_GB + g))
    dts = pl.BlockSpec((CHUNK, LANES), lambda c, g: (cm(c), (M2_IN_PAD - LANES) // LANES))
    v128 = pl.BlockSpec((1, LANES), lambda c, g: (0, 0))
    v256 = pl.BlockSpec((1, M2_GB * M2_GW), lambda c, g: (0, g))
    es = pl.BlockSpec((LANES, M2_GB * M2_GW), lambda c, g: (0, g))
    ecs = pl.BlockSpec((LANES, M2_GB * M2_HPG * LANES), lambda c, g: (0, g))
    st = pl.BlockSpec((None, M2_GB, M2_STATE, M2_GW), lambda c, g: (cm(c), g, 0, 0))
    return wide, nar, dts, v128, v256, es, ecs, st


def _m2_fwd(xbc, p, dtb, alog, dsk, ng, e, ecol):
    L = xbc.shape[0]
    nc = L // CHUNK
    wide, nar, dts, v128, v256, es, ecs, st = _m2_specs(nc, False)

    def body(x_ref, b_ref, c_ref, z_ref, dt_ref, dtb_ref, al_ref, dsk_ref, ng_ref, e_ref, ec_ref, y_ref, sp_ref, s_scr):
        c, g = pl.program_id(0), pl.program_id(1)
        wide_l = [slice(i * M2_GW, (i + 1) * M2_GW) for i in range(M2_GB)]
        nar_l = [slice(i * LANES, (i + 1) * LANES) for i in range(M2_GB)]
        groups = pl.ds(g * M2_GB, M2_GB)
        wide_s = lambda ref: jnp.concatenate([ref[:, ls][None] for ls in wide_l], axis=0)
        nar_s = lambda ref: jnp.concatenate([ref[:, ls][None] for ls in nar_l], axis=0)

        @pl.when(c == 0)
        def _():
            s_scr[groups] = jnp.zeros((M2_GB, M2_STATE, M2_GW), F32)

        S = s_scr[groups]
        sp_ref[...] = S
        y, s_new = _m2_chunk(wide_s(x_ref), nar_s(b_ref), nar_s(c_ref), wide_s(z_ref), dt_ref[...], S, dtb_ref[...], al_ref[...],
                             wide_s(dsk_ref), wide_s(ng_ref), e_ref[...], ec_ref[...])
        for i, ls in enumerate(wide_l):
            y_ref[:, ls] = y[i]
        s_scr[groups] = s_new

    return pl.pallas_call(
        body, name="m2_fwd", grid=(nc, M2_GROUPS // M2_GB),
        in_specs=[wide(0), nar(2 * M2_GROUPS), nar(3 * M2_GROUPS), wide(0), dts, v128, v128, v256, v256, es, ecs],
        out_specs=[wide(0), st],
        out_shape=[jax.ShapeDtypeStruct((L, M2_INNER), F32), jax.ShapeDtypeStruct((nc, M2_GROUPS, M2_STATE, M2_GW), F32)],
        scratch_shapes=[pltpu.VMEM((M2_GROUPS, M2_STATE, M2_GW), F32)],
        compiler_params=_params("arbitrary", "arbitrary"),
    )(xbc, xbc, xbc, p, p, dtb, alog, dsk, ng, e, ecol)


def _m2_bwd(xbc, p, dtb, alog, dsk, ng, e, ecol, sprev, dy):
    L = xbc.shape[0]
    nc = L // CHUNK
    wide, nar, dts, v128, v256, es, ecs, st = _m2_specs(nc, True)

    def body(x_ref, b_ref, c_ref, z_ref, dt_ref, dtb_ref, al_ref, dsk_ref, ng_ref, e_ref, ec_ref, sp_ref, dy_ref,
             dx_ref, db_ref, dc_ref, dz_ref, ddt_ref, dnar_ref, dwide_ref, ds_scr):
        c, g = pl.program_id(0), pl.program_id(1)
        wide_l = [slice(i * M2_GW, (i + 1) * M2_GW) for i in range(M2_GB)]
        nar_l = [slice(i * LANES, (i + 1) * LANES) for i in range(M2_GB)]
        groups = pl.ds(g * M2_GB, M2_GB)
        wide_s = lambda ref: jnp.concatenate([ref[:, ls][None] for ls in wide_l], axis=0)
        nar_s = lambda ref: jnp.concatenate([ref[:, ls][None] for ls in nar_l], axis=0)

        @pl.when(jnp.logical_and(c == 0, g == 0))
        def _():
            dnar_ref[...] = jnp.zeros_like(dnar_ref)

        @pl.when(c == 0)
        def _():
            ds_scr[groups] = jnp.zeros((M2_GB, M2_STATE, M2_GW), F32)
            dwide_ref[groups] = jnp.zeros((M2_GB, SUBLANES, M2_GW), F32)

        @pl.when(g == 0)
        def _():
            ddt_ref[...] = jnp.zeros_like(ddt_ref)

        e_m, ec_m = e_ref[...], ec_ref[...]
        f = lambda x, bm, cm, z, dtr, S, dtb, al, dsk, ng: _m2_chunk(x, bm, cm, z, dtr, S, dtb, al, dsk, ng, e_m, ec_m)
        _, vjp = jax.vjp(f, wide_s(x_ref), nar_s(b_ref), nar_s(c_ref), wide_s(z_ref), dt_ref[...], sp_ref[...], dtb_ref[...],
                         al_ref[...], wide_s(dsk_ref), wide_s(ng_ref))
        dx, db, dc, dz, ddt, ds, ddtb, dal, ddsk, dng = vjp((wide_s(dy_ref), ds_scr[groups]))
        for i in range(M2_GB):
            dx_ref[:, wide_l[i]] = dx[i]
            db_ref[:, nar_l[i]] = db[i]
            dc_ref[:, nar_l[i]] = dc[i]
            dz_ref[:, wide_l[i]] = dz[i]
        ds_scr[groups] = ds
        ddt_ref[...] += ddt
        dnar_ref[...] += jnp.concatenate([ddtb, dal, jnp.zeros((SUBLANES - 2, LANES), F32)], axis=0)
        dwide_ref[groups] += jnp.concatenate([ddsk, dng, jnp.zeros((M2_GB, SUBLANES - 2, M2_GW), F32)], axis=1)

    return pl.pallas_call(
        body, name="m2_bwd", grid=(nc, M2_GROUPS // M2_GB),
        in_specs=[wide(0), nar(2 * M2_GROUPS), nar(3 * M2_GROUPS), wide(0), dts, v128, v128, v256, v256, es, ecs, st, wide(0)],
        out_specs=[wide(0), nar(0), nar(0), wide(0), pl.BlockSpec((CHUNK, LANES), lambda c, g: (nc - 1 - c, 0)),
                   pl.BlockSpec((SUBLANES, LANES), lambda c, g: (0, 0)),
                   pl.BlockSpec((M2_GROUPS, SUBLANES, M2_GW), lambda c, g: (0, 0, 0))],
        out_shape=[jax.ShapeDtypeStruct((L, M2_INNER), F32), jax.ShapeDtypeStruct((L, M2_GROUPS * M2_STATE), F32),
                   jax.ShapeDtypeStruct((L, M2_GROUPS * M2_STATE), F32), jax.ShapeDtypeStruct((L, M2_INNER), F32),
                   jax.ShapeDtypeStruct((L, LANES), F32), jax.ShapeDtypeStruct((SUBLANES, LANES), F32),
                   jax.ShapeDtypeStruct((M2_GROUPS, SUBLANES, M2_GW), F32)],
        scratch_shapes=[pltpu.VMEM((M2_GROUPS, M2_STATE, M2_GW), F32)],
        compiler_params=_params("arbitrary", "arbitrary"),
    )(xbc, xbc, xbc, p, p, dtb, alog, dsk, ng, e, ecol, sprev, dy)


def _m2_selectors():
    e = np.zeros((LANES, M2_INNER), np.float32)
    ecol = np.zeros((LANES, M2_HEADS * LANES), np.float32)
    for h in range(M2_HEADS):
        e[h, M2_HD * h:M2_HD * (h + 1)] = 1.0
        ecol[h, LANES * h:LANES * (h + 1)] = 1.0
    return jnp.asarray(e), jnp.asarray(ecol)


S5_NS = S5_GROUPS * S5_STATE // S5_BLOCKS
S5_ROWS = 256
GELU_C = math.sqrt(2.0 / math.pi)


def _gelu(x):
    return 0.5 * x * (1.0 + jnp.tanh(GELU_C * (x + 0.044715 * x * x * x)))


def _gelu_grad(x):
    t = jnp.tanh(GELU_C * (x + 0.044715 * x * x * x))
    return 0.5 * (1.0 + t) + 0.5 * x * (1.0 - t * t) * GELU_C * (1.0 + 3.0 * 0.044715 * x * x)


def _s5_scan(re_ref, im_ref, pw_re, pw_im, nrows, reverse):
    n = re_ref.shape[1]
    row = _iota2((SUBLANES, n), 0)
    steps = [(d, pw_re[d - 1:d, :], pw_im[d - 1:d, :]) for d in (1, 2, 4)]
    if reverse:
        cw_re = jnp.concatenate([pw_re[SUBLANES - 1 - k:SUBLANES - k, :] for k in range(SUBLANES)], axis=0)
        cw_im = jnp.concatenate([pw_im[SUBLANES - 1 - k:SUBLANES - k, :] for k in range(SUBLANES)], axis=0)
    else:
        cw_re, cw_im = pw_re, pw_im
    edge = 0 if reverse else SUBLANES - 1
    ngroups = nrows // SUBLANES

    def step(i, carry):
        cr, ci = carry
        gi = (ngroups - 1 - i) if reverse else i
        r0 = pl.multiple_of(gi * SUBLANES, SUBLANES)
        xr, xi = re_ref[pl.ds(r0, SUBLANES), :], im_ref[pl.ds(r0, SUBLANES), :]
        for d, pr, pi in steps:
            if reverse:
                sr = jnp.where(row < SUBLANES - d, pltpu.roll(xr, SUBLANES - d, axis=0), 0.0)
                si = jnp.where(row < SUBLANES - d, pltpu.roll(xi, SUBLANES - d, axis=0), 0.0)
            else:
                sr = jnp.where(row >= d, pltpu.roll(xr, d, axis=0), 0.0)
                si = jnp.where(row >= d, pltpu.roll(xi, d, axis=0), 0.0)
            xr, xi = xr + (pr * sr - pi * si), xi + (pr * si + pi * sr)
        xr, xi = xr + (cw_re * cr - cw_im * ci), xi + (cw_re * ci + cw_im * cr)
        re_ref[pl.ds(r0, SUBLANES), :] = xr
        im_ref[pl.ds(r0, SUBLANES), :] = xi
        return (jnp.sum(jnp.where(row == edge, xr, 0.0), axis=0, keepdims=True),
                jnp.sum(jnp.where(row == edge, xi, 0.0), axis=0, keepdims=True))

    z = jnp.zeros((1, n), F32)
    lax.fori_loop(0, ngroups, step, (z, z))


def _s5_project_in(u_ref, bm_ref, re_ref, im_ref, L):
    def step(i, carry):
        r0 = pl.multiple_of(i * S5_ROWS, S5_ROWS)
        bu = _dot(u_ref[pl.ds(r0, S5_ROWS), :], bm_ref[...])
        re_ref[pl.ds(r0, S5_ROWS), :] = bu[:, :S5_NS]
        im_ref[pl.ds(r0, S5_ROWS), :] = bu[:, S5_NS:]
        return carry

    lax.fori_loop(0, L // S5_ROWS, step, 0)


def _s5_specs(L):
    col = pl.BlockSpec((L, LANES), lambda j: (0, j))
    bm = pl.BlockSpec((None, LANES, 2 * S5_NS), lambda j: (j, 0, 0))
    cm = pl.BlockSpec((None, 2 * S5_NS, LANES), lambda j: (j, 0, 0))
    pw = pl.BlockSpec((None, SUBLANES, S5_NS), lambda j: (j, 0, 0))
    vec = pl.BlockSpec((1, LANES), lambda j: (0, j))
    return col, bm, cm, pw, vec


def _s5_fwd(u, bmat, cmat, pw_re, pw_im, dsk):
    L = u.shape[0]
    col, bm, cm, pw, vec = _s5_specs(L)

    def body(u_ref, bm_ref, cm_ref, pr_ref, pi_ref, d_ref, y_ref, re_scr, im_scr):
        _s5_project_in(u_ref, bm_ref, re_scr, im_scr, L)
        _s5_scan(re_scr, im_scr, pr_ref[...], pi_ref[...], L, False)

        def step(i, carry):
            r0 = pl.multiple_of(i * S5_ROWS, S5_ROWS)
            rows = pl.ds(r0, S5_ROWS)
            y = _dot(re_scr[rows, :], cm_ref[:S5_NS, :]) + _dot(im_scr[rows, :], cm_ref[S5_NS:, :]) + d_ref[...] * u_ref[rows, :]
            y_ref[rows, :] = _gelu(y)
            return carry

        lax.fori_loop(0, L // S5_ROWS, step, 0)

    return pl.pallas_call(
        body, name="s5_fwd", grid=(S5_BLOCKS,),
        in_specs=[col, bm, cm, pw, pw, vec], out_specs=col,
        out_shape=jax.ShapeDtypeStruct((L, D_MODEL), F32),
        scratch_shapes=[pltpu.VMEM((L, S5_NS), F32)] * 2,
        compiler_params=_params("parallel"),
    )(u, bmat, cmat, pw_re, pw_im, dsk)


def _s5_bwd(u, bmat, cmat, pw_re, pw_im, dsk, dyg):
    L = u.shape[0]
    col, bm, cm, pw, vec = _s5_specs(L)

    def body(u_ref, bm_ref, cm_ref, pr_ref, pi_ref, d_ref, dy_ref, du_ref, dbm_ref, dcm_ref, dlam_ref, dd_ref,
             re_scr, im_scr, gr_scr, gi_scr, dyp_scr):
        _s5_project_in(u_ref, bm_ref, re_scr, im_scr, L)
        _s5_scan(re_scr, im_scr, pr_ref[...], pi_ref[...], L, False)

        def step(i, carry):
            dcr, dci, dd = carry
            r0 = pl.multiple_of(i * S5_ROWS, S5_ROWS)
            rows = pl.ds(r0, S5_ROWS)
            sr, si, uu = re_scr[rows, :], im_scr[rows, :], u_ref[rows, :]
            y = _dot(sr, cm_ref[:S5_NS, :]) + _dot(si, cm_ref[S5_NS:, :]) + d_ref[...] * uu
            dyp = dy_ref[rows, :] * _gelu_grad(y)
            dyp_scr[rows, :] = dyp
            gr_scr[rows, :] = _nt(dyp, cm_ref[:S5_NS, :])
            gi_scr[rows, :] = _nt(dyp, cm_ref[S5_NS:, :])
            return dcr + _tn(sr, dyp), dci + _tn(si, dyp), dd + jnp.sum(dyp * uu, axis=0, keepdims=True)

        zc = jnp.zeros((S5_NS, LANES), F32)
        dcr, dci, dd = lax.fori_loop(0, L // S5_ROWS, step, (zc, zc, jnp.zeros((1, LANES), F32)))
        dcm_ref[:S5_NS, :] = dcr
        dcm_ref[S5_NS:, :] = dci
        dd_ref[...] = dd

        _s5_scan(gr_scr, gi_scr, pr_ref[...], -pi_ref[...], L, True)

        row = _iota2((SUBLANES, S5_NS), 0)

        def lam_step(i, carry):
            ar, ai, pr, pi = carry
            r0 = pl.multiple_of(i * SUBLANES, SUBLANES)
            rows = pl.ds(r0, SUBLANES)
            sr, si = re_scr[rows, :], im_scr[rows, :]
            spr = jnp.where(row >= 1, pltpu.roll(sr, 1, axis=0), pr)
            spi = jnp.where(row >= 1, pltpu.roll(si, 1, axis=0), pi)
            gr, gi = gr_scr[rows, :], gi_scr[rows, :]
            ar = ar + jnp.sum(spr * gr + spi * gi, axis=0, keepdims=True)
            ai = ai + jnp.sum(spr * gi - spi * gr, axis=0, keepdims=True)
            last = row == SUBLANES - 1
            return (ar, ai, jnp.sum(jnp.where(last, sr, 0.0), axis=0, keepdims=True),
                    jnp.sum(jnp.where(last, si, 0.0), axis=0, keepdims=True))

        z = jnp.zeros((1, S5_NS), F32)
        ar, ai, _, _ = lax.fori_loop(0, L // SUBLANES, lam_step, (z, z, z, z))
        dlam_ref[...] = jnp.concatenate([ar, ai, jnp.zeros((SUBLANES - 2, S5_NS), F32)], axis=0)

        def in_step(i, carry):
            dbr, dbi = carry
            r0 = pl.multiple_of(i * S5_ROWS, S5_ROWS)
            rows = pl.ds(r0, S5_ROWS)
            gr, gi, uu = gr_scr[rows, :], gi_scr[rows, :], u_ref[rows, :]
            du_ref[rows, :] = dyp_scr[rows, :] * d_ref[...] + _nt(gr, bm_ref[:, :S5_NS]) + _nt(gi, bm_ref[:, S5_NS:])
            return dbr + _tn(uu, gr), dbi + _tn(uu, gi)

        zb = jnp.zeros((LANES, S5_NS), F32)
        dbr, dbi = lax.fori_loop(0, L // S5_ROWS, in_step, (zb, zb))
        dbm_ref[:, :S5_NS] = dbr
        dbm_ref[:, S5_NS:] = dbi

    return pl.pallas_call(
        body, name="s5_bwd", grid=(S5_BLOCKS,),
        in_specs=[col, bm, cm, pw, pw, vec, col], out_specs=[col, bm, cm, pw, vec],
        out_shape=[jax.ShapeDtypeStruct((L, D_MODEL), F32), jax.ShapeDtypeStruct((S5_BLOCKS, LANES, 2 * S5_NS), F32),
                   jax.ShapeDtypeStruct((S5_BLOCKS, 2 * S5_NS, LANES), F32),
                   jax.ShapeDtypeStruct((S5_BLOCKS, SUBLANES, S5_NS), F32), jax.ShapeDtypeStruct((1, D_MODEL), F32)],
        scratch_shapes=[pltpu.VMEM((L, S5_NS), F32)] * 4 + [pltpu.VMEM((L, LANES), F32)],
        compiler_params=_params("parallel"),
    )(u, bmat, cmat, pw_re, pw_im, dsk, dyg)


def _s5_discretize(lam_re, lam_im, log_dt, b_re, b_im, e16):
    dt = jnp.exp(log_dt)
    zr, zi = lam_re * dt, lam_im * dt
    mag = jnp.exp(zr)
    lbr, lbi = mag * jnp.cos(zi), mag * jnp.sin(zi)
    den = lam_re * lam_re + lam_im * lam_im
    nr, ni = lbr - 1.0, lbi
    cr = (nr * lam_re + ni * lam_im) / den
    ci = (ni * lam_re - nr * lam_im) / den
    crw, ciw = _pick(cr, e16), _pick(ci, e16)
    return lbr, lbi, crw * b_re - ciw * b_im, crw * b_im + ciw * b_re


def _s5_params_fwd(lam_re, lam_im, log_dt, b_re, b_im, e16):
    def body(lr, li, ld, br, bi, e, o1, o2, o3, o4):
        for o, val in zip((o1, o2, o3, o4), _s5_discretize(lr[...], li[...], ld[...], br[...], bi[...], e[...])):
            o[...] = val

    g, p, n = S5_GROUPS, S5_STATE, S5_STATE * S5_GROUP
    return pl.pallas_call(
        body, name="s5_params_fwd",
        out_shape=[jax.ShapeDtypeStruct((g, p), F32)] * 2 + [jax.ShapeDtypeStruct((g, n), F32)] * 2,
        compiler_params=_params(),
    )(lam_re, lam_im, log_dt, b_re, b_im, e16)


def _s5_params_bwd(lam_re, lam_im, log_dt, b_re, b_im, e16, cts):
    def body(lr, li, ld, br, bi, e, c1, c2, c3, c4, o1, o2, o3, o4, o5):
        e_m = e[...]
        f = lambda a, b, c, d, g: _s5_discretize(a, b, c, d, g, e_m)
        _, vjp = jax.vjp(f, lr[...], li[...], ld[...], br[...], bi[...])
        for o, val in zip((o1, o2, o3, o4, o5), vjp((c1[...], c2[...], c3[...], c4[...]))):
            o[...] = val

    g, p, n = S5_GROUPS, S5_STATE, S5_STATE * S5_GROUP
    return pl.pallas_call(
        body, name="s5_params_bwd",
        out_shape=[jax.ShapeDtypeStruct((g, p), F32)] * 2 + [jax.ShapeDtypeStruct((g, 1), F32)]
        + [jax.ShapeDtypeStruct((g, n), F32)] * 2,
        compiler_params=_params(),
    )(lam_re, lam_im, log_dt, b_re, b_im, e16, *cts)


def _add_residual(acc, h):
    return (acc + h,)


def _mlp_fwd(i, h, g, w1, w2):
    hn = _rms_fwd(f"mlp{i}_norm", h, g)
    r = _mm(f"mlp{i}_up", hn, w1, "nn", (BF16,), epi=lambda acc: (jnp.square(jnp.maximum(acc, 0.0)),))
    return _mm(f"mlp{i}_down", r, w2, "nn", (F32,), epi=_add_residual, extras=(h,)), (h, hn, r)


def _mlp_bwd(i, dh_out, saved, g, w1, w2):
    h, hn, r = saved
    dw2 = _mm(f"mlp{i}_dw2", r, dh_out, "tn", (BF16,))
    da = _mm(f"mlp{i}_da", dh_out, w2, "nt", (BF16,), epi=lambda acc, rr: (acc * (2.0 * jnp.sqrt(rr.astype(F32))),), extras=(r,))
    dw1 = _mm(f"mlp{i}_dw1", hn, da, "tn", (BF16,))
    dhn = _mm(f"mlp{i}_dhn", da, w1, "nt", (F32,))
    dh, dg = _rms_bwd(f"mlp{i}_dnorm", h, g, dhn, dh_out)
    return dh, dg[0], dw1, dw2


def _lanes(v, n):
    return jnp.broadcast_to(v.reshape(n, 1, 1), (n, 1, LANES))


def _gdn_fwd_layer(i, h, g, w_in, conv_w, a_log, dt_bias, o_g, w_out):
    hn = _rms_fwd(f"gdn{i}_norm", h, g)
    p = _mm(f"gdn{i}_in", hn, w_in, "nn", (F32,))
    zb = jnp.zeros((1, D_MODEL), F32)
    qkv = [_conv_fwd(f"gdn{i}_conv{t}", p, t * D_MODEL, conv_w[:, t * D_MODEL:(t + 1) * D_MODEL], zb) for t in range(3)]
    ea, eb = _gdn_selectors()
    y, sprev = _gdn_fwd(*qkv, p, _lanes(a_log, GDN_HEADS), _lanes(dt_bias, GDN_HEADS), o_g.reshape(1, LANES), ea, eb)
    return _mm(f"gdn{i}_out", y, w_out, "nn", (F32,), epi=_add_residual, extras=(h,)), (h, hn, p, qkv, y, sprev)


def _gdn_bwd_layer(i, dh_out, saved, g, w_in, conv_w, a_log, dt_bias, o_g, w_out):
    h, hn, p, qkv, y, sprev = saved
    dy = _mm(f"gdn{i}_dy", dh_out, w_out, "nt", (F32,))
    dw_out = _mm(f"gdn{i}_dwout", y, dh_out, "tn", (BF16,))
    ea, eb = _gdn_selectors()
    dq, dk, dv, dgate, dab, dpar = _gdn_bwd(*qkv, p, _lanes(a_log, GDN_HEADS), _lanes(dt_bias, GDN_HEADS),
                                            o_g.reshape(1, LANES), ea, eb, sprev, dy)
    zb = jnp.zeros((1, D_MODEL), F32)
    dpre, dcw = [], []
    for t, d in enumerate((dq, dk, dv)):
        dx, dw, _ = _conv_bwd(f"gdn{i}_dconv{t}", p, t * D_MODEL, conv_w[:, t * D_MODEL:(t + 1) * D_MODEL], zb, d)
        dpre.append(dx)
        dcw.append(dw)
    dp = jnp.concatenate(dpre + [dgate, dab], axis=1)
    dw_in = _mm(f"gdn{i}_dwin", hn, dp, "tn", (BF16,))[:, :GDN_IN]
    dhn = _mm(f"gdn{i}_dhn", dp, w_in, "nt", (F32,))
    dh, dg = _rms_bwd(f"gdn{i}_dnorm", h, g, dhn, dh_out)
    grads = dict(w_in=dw_in, conv_w=jnp.concatenate(dcw, axis=1), a_log=jnp.sum(dpar[:, 0, :], axis=-1),
                 dt_bias=jnp.sum(dpar[:, 1, :], axis=-1), o_norm_g=jnp.sum(dpar[:, 2, :], axis=0), w_out=dw_out)
    return dh, dg[0], grads


def _m2_vectors(dt_bias, a_log, d_skip, norm_g):
    pad = lambda v: jnp.pad(v, (0, LANES - M2_HEADS)).reshape(1, LANES)
    return pad(dt_bias), pad(a_log), jnp.repeat(d_skip, M2_HD).reshape(1, M2_INNER), norm_g.reshape(1, M2_INNER)


def _m2_fwd_layer(h, g, w_in, conv_w, conv_b, dt_bias, a_log, d_skip, norm_g, w_out):
    hn = _rms_fwd("m2_norm", h, g)
    p = _mm("m2_in", hn, w_in, "nn", (F32,))
    xbc = _conv_fwd("m2_conv", p, M2_INNER, conv_w, conv_b.reshape(1, M2_CONV_CH))
    e, ecol = _m2_selectors()
    y, sprev = _m2_fwd(xbc, p, *_m2_vectors(dt_bias, a_log, d_skip, norm_g), e, ecol)
    return _mm("m2_out", y, w_out, "nn", (F32,), epi=_add_residual, extras=(h,)), (h, hn, p, xbc, y, sprev)


def _m2_bwd_layer(dh_out, saved, g, w_in, conv_w, conv_b, dt_bias, a_log, d_skip, norm_g, w_out):
    h, hn, p, xbc, y, sprev = saved
    dy = _mm("m2_dy", dh_out, w_out, "nt", (F32,))
    dw_out = _mm("m2_dwout", y, dh_out, "tn", (BF16,))
    e, ecol = _m2_selectors()
    dx, db, dc, dz, ddt, dnar, dwide = _m2_bwd(xbc, p, *_m2_vectors(dt_bias, a_log, d_skip, norm_g), e, ecol, sprev, dy)
    dxbc, dcw, dcb = _conv_bwd("m2_dconv", p, M2_INNER, conv_w, conv_b.reshape(1, M2_CONV_CH),
                               jnp.concatenate([dx, db, dc], axis=1))
    dp = jnp.concatenate([dz, dxbc, ddt], axis=1)
    dw_in = _mm("m2_dwin", hn, dp, "tn", (BF16,))[:, :M2_IN]
    dhn = _mm("m2_dhn", dp, w_in, "nt", (F32,))
    dh, dg = _rms_bwd("m2_dnorm", h, g, dhn, dh_out)
    grads = dict(w_in=dw_in, conv_w=dcw, conv_b=dcb[0], dt_bias=dnar[0, :M2_HEADS], a_log=dnar[1, :M2_HEADS],
                 d=jnp.sum(dwide[:, 0, :].reshape(M2_HEADS, M2_HD), axis=-1), norm_g=dwide[:, 1, :].reshape(M2_INNER),
                 w_out=dw_out)
    return dh, dg[0], grads


def _s5_selector():
    e16 = np.zeros((S5_STATE, S5_STATE * S5_GROUP), np.float32)
    for p in range(S5_STATE):
        e16[p, p * S5_GROUP:(p + 1) * S5_GROUP] = 1.0
    return jnp.asarray(e16)


def _s5_operands(lbr, lbi, bbr, bbi, c_re, c_im):
    eye = jnp.eye(S5_BLOCKS, dtype=F32)
    gpb = S5_GROUPS // S5_BLOCKS
    bd = lambda t: jnp.einsum("jgpk,gh->jgkhp", t.reshape(S5_BLOCKS, gpb, S5_STATE, S5_GROUP), eye).reshape(S5_BLOCKS, LANES, S5_NS)
    cd = lambda t: jnp.einsum("jgkp,gh->jgphk", t.reshape(S5_BLOCKS, gpb, S5_GROUP, S5_STATE), eye).reshape(S5_BLOCKS, S5_NS, LANES)
    bmat = jnp.concatenate([bd(bbr), bd(bbi)], axis=2).astype(BF16)
    cmat = jnp.concatenate([cd(c_re), -cd(c_im)], axis=1).astype(BF16)
    ar, ai = lbr.reshape(S5_BLOCKS, S5_NS), lbi.reshape(S5_BLOCKS, S5_NS)
    pr, pi = [ar], [ai]
    for _ in range(SUBLANES - 1):
        pr, pi = pr + [pr[-1] * ar - pi[-1] * ai], pi + [pr[-1] * ai + pi[-1] * ar]
    return bmat, cmat, jnp.stack(pr, axis=1), jnp.stack(pi, axis=1)


def _s5_fwd_layer(h, g, w_in, lam_re, lam_im, log_dt, b_re, b_im, c_re, c_im, d_skip, w_out):
    hn = _rms_fwd("s5_norm", h, g)
    u = _mm("s5_in", hn, w_in, "nn", (F32,))
    n = S5_STATE * S5_GROUP
    lbr, lbi, bbr, bbi = _s5_params_fwd(lam_re, lam_im, log_dt.reshape(S5_GROUPS, 1), b_re.reshape(S5_GROUPS, n),
                                        b_im.reshape(S5_GROUPS, n), _s5_selector())
    ops = _s5_operands(lbr, lbi, bbr, bbi, c_re, c_im)
    yg = _s5_fwd(u, *ops, d_skip.reshape(1, D_MODEL))
    ag = _mm("s5_out", yg, w_out, "nn", (F32,))
    return _glu_fwd(h, ag), (h, hn, u, ops, yg, ag)


def _s5_bwd_layer(dh_out, saved, g, w_in, lam_re, lam_im, log_dt, b_re, b_im, c_re, c_im, d_skip, w_out):
    h, hn, u, ops, yg, ag = saved
    dag = _glu_bwd(dh_out, ag)
    dw_out = _mm("s5_dwout", yg, dag, "tn", (BF16,))
    dyg = _mm("s5_dyg", dag, w_out, "nt", (F32,))
    du, dbmat, dcmat, dlam, ddsk = _s5_bwd(u, *ops, d_skip.reshape(1, D_MODEL), dyg)
    eye = jnp.eye(S5_BLOCKS, dtype=F32)
    gpb = S5_GROUPS // S5_BLOCKS
    n = S5_STATE * S5_GROUP
    ub = lambda t: jnp.einsum("jgkhp,gh->jgpk", t.reshape(S5_BLOCKS, gpb, S5_GROUP, gpb, S5_STATE), eye).reshape(S5_GROUPS, n)
    uc = lambda t: jnp.einsum("jgphk,gh->jgkp", t.reshape(S5_BLOCKS, gpb, S5_STATE, gpb, S5_GROUP), eye).reshape(c_re.shape)
    cts = (dlam[:, 0, :].reshape(S5_GROUPS, S5_STATE), dlam[:, 1, :].reshape(S5_GROUPS, S5_STATE),
           ub(dbmat[:, :, :S5_NS]), ub(dbmat[:, :, S5_NS:]))
    dlr, dli, dld, dbr, dbi = _s5_params_bwd(lam_re, lam_im, log_dt.reshape(S5_GROUPS, 1), b_re.reshape(S5_GROUPS, n),
                                             b_im.reshape(S5_GROUPS, n), _s5_selector(), cts)
    dw_in = _mm("s5_dwin", hn, du, "tn", (BF16,))
    dhn = _mm("s5_dhn", du, w_in, "nt", (F32,))
    dh, dg = _rms_bwd("s5_dnorm", h, g, dhn, dh_out)
    grads = dict(w_in=dw_in, lam_re=dlr, lam_im=dli, log_dt=dld[:, 0], b_re=dbr.reshape(b_re.shape), b_im=dbi.reshape(b_im.shape),
                 c_re=uc(dcmat[:, :S5_NS, :]), c_im=-uc(dcmat[:, S5_NS:, :]), d=ddsk[0], w_out=dw_out)
    return dh, dg[0], grads


MIXER_OF_LAYER = ("gdn", "s5", "m2", "gdn")
MIXER_INDEX = (0, 0, 0, 1)


def _mixer_args(W, i):
    kind, j = MIXER_OF_LAYER[i], MIXER_INDEX[i]
    if kind == "gdn":
        return tuple(W["gdn_" + k][j] for k in ("w_in", "conv_w", "a_log", "dt_bias", "o_norm_g", "w_out"))
    if kind == "s5":
        return tuple(W["s5_" + k][j] for k in ("w_in", "lam_re", "lam_im", "log_dt", "b_re", "b_im", "c_re", "c_im", "d", "w_out"))
    return tuple(W["m2_" + k][j] for k in ("w_in", "conv_w", "conv_b", "dt_bias", "a_log", "d", "norm_g", "w_out"))


def _local_step(x, target, W):
    h = x
    saved = []
    for i in range(DEPTH):
        kind = MIXER_OF_LAYER[i]
        args = _mixer_args(W, i)
        if kind == "gdn":
            h, sm = _gdn_fwd_layer(i, h, W["norm_mix_g"][i], *args)
        elif kind == "s5":
            h, sm = _s5_fwd_layer(h, W["norm_mix_g"][i], *args)
        else:
            h, sm = _m2_fwd_layer(h, W["norm_mix_g"][i], *args)
        h, sp = _mlp_fwd(i, h, W["norm_mlp_g"][i], W["mlp_w1"][i], W["mlp_w2"][i])
        saved.append((sm, sp))
    loss, dh, dgf = _loss_head(h, W["final_norm_g"], target)
    G = {"final_norm_g": dgf[0], "norm_mix_g": [None] * DEPTH, "norm_mlp_g": [None] * DEPTH,
         "mlp_w1": [None] * DEPTH, "mlp_w2": [None] * DEPTH}
    mix = {}
    for i in reversed(range(DEPTH)):
        kind = MIXER_OF_LAYER[i]
        sm, sp = saved[i]
        dh, G["norm_mlp_g"][i], G["mlp_w1"][i], G["mlp_w2"][i] = _mlp_bwd(i, dh, sp, W["norm_mlp_g"][i], W["mlp_w1"][i], W["mlp_w2"][i])
        args = _mixer_args(W, i)
        if kind == "gdn":
            dh, G["norm_mix_g"][i], gm = _gdn_bwd_layer(i, dh, sm, W["norm_mix_g"][i], *args)
        elif kind == "s5":
            dh, G["norm_mix_g"][i], gm = _s5_bwd_layer(dh, sm, W["norm_mix_g"][i], *args)
        else:
            dh, G["norm_mix_g"][i], gm = _m2_bwd_layer(dh, sm, W["norm_mix_g"][i], *args)
        for k, v in gm.items():
            mix.setdefault(kind + "_" + k, {})[MIXER_INDEX[i]] = v
    for k, d in mix.items():
        G[k] = [d[j] for j in sorted(d)]
    return loss, dh, {k: (v if k in BIG else jnp.stack(v)) if isinstance(v, list) else v for k, v in G.items()}


ADAM_ROWS = 128


def _adamw(name, w, g, m, v):
    R, C = w.shape
    tr = _tile(R, (ADAM_ROWS, SUBLANES))

    def body(w_ref, g_ref, m_ref, v_ref, d_ref, mo_ref, vo_ref):
        gg = g_ref[...]
        mn = ADAM_B1 * m_ref[...] + (1.0 - ADAM_B1) * gg
        vn = ADAM_B2 * v_ref[...] + (1.0 - ADAM_B2) * (gg * gg)
        m_hat = mn / (1.0 - ADAM_B1 ** ADAM_STEP)
        v_hat = vn / (1.0 - ADAM_B2 ** ADAM_STEP)
        d_ref[...] = -ADAM_LR * (m_hat / (jnp.sqrt(v_hat) + ADAM_EPS) + ADAM_WD * w_ref[...])
        mo_ref[...] = mn
        vo_ref[...] = vn

    blk = pl.BlockSpec((tr, C), lambda i: (i, 0))
    return pl.pallas_call(
        body, name=name, grid=(R // tr,), in_specs=[blk] * 4, out_specs=[blk] * 3,
        out_shape=[jax.ShapeDtypeStruct((R, C), F32)] * 3, compiler_params=_params("parallel"),
    )(w, g, m, v)


MESH = pl.DeviceIdType.MESH
ANY = pl.BlockSpec(memory_space=pl.ANY)
N_CHIPS = 4
N_DEV = 8


def _position():
    return lax.axis_index("x"), lax.axis_index("y"), lax.axis_index("c")


def _gather_body(w_refs, out_refs, send_sems, recv_sems):
    x, y, c = _position()
    sibling = (x, y, 1 - c)
    chips = [(1 - x, y), (x, 1 - y), (1 - x, 1 - y)]
    firsts, passes = [], []
    for t, (w_ref, out_ref) in enumerate(zip(w_refs, out_refs)):
        half = w_ref.shape[0] // 2

        def piece(cx, cy, hc, out_ref=out_ref, half=half):
            return out_ref.at[2 * cx + cy, pl.ds(hc * half, half), :]

        def copy(k, src, dst, to, t=t):
            return pltpu.make_async_remote_copy(src_ref=src, dst_ref=dst, send_sem=send_sems.at[6 * t + k],
                                                recv_sem=recv_sems.at[6 * t + k], device_id=to, device_id_type=MESH)

        first = [copy(j, w_ref.at[pl.ds(c * half, half), :], piece(x, y, c), (*chip, c)) for j, chip in enumerate(chips)]
        for cp in first:
            cp.start()
        firsts.append((first, piece, copy))
    for first, piece, copy in firsts:
        passed = [copy(3 + j, piece(*chip, c), piece(*chip, c), sibling) for j, chip in enumerate(chips)]
        for j, chip in enumerate(chips):
            copy(j, piece(*chip, c), piece(*chip, c), sibling).wait_recv()
            passed[j].start()
        passes.append(passed)
    for (first, piece, copy), passed in zip(firsts, passes):
        for j, chip in enumerate(chips):
            copy(3 + j, piece(*chip, 1 - c), piece(*chip, 1 - c), sibling).wait_recv()
        for cp in first + passed:
            cp.wait_send()


def _gather_shards(wps):
    n = len(wps)

    def body(*refs):
        _gather_body(refs[:n], refs[n:2 * n], *refs[2 * n:])

    return pl.pallas_call(
        body, name="gather_shards", in_specs=[ANY] * n, out_specs=[ANY] * n,
        out_shape=[jax.ShapeDtypeStruct((N_CHIPS, *wp.shape), wp.dtype) for wp in wps],
        scratch_shapes=[pltpu.SemaphoreType.DMA((6 * n,)), pltpu.SemaphoreType.DMA((6 * n,))],
    )(*wps)


GATHER_LATER_ID = 1


def _gather_shards_later(wps):
    n = len(wps)
    w_refs = [jax.new_ref(wp, memory_space=pltpu.MemorySpace.HBM) for wp in wps]
    out_refs = [jax.empty_ref(jax.ShapeDtypeStruct((N_CHIPS, *wp.shape), wp.dtype), memory_space=pltpu.MemorySpace.HBM)
                for wp in wps]

    @pl.kernel(mesh=plsc.ScalarSubcoreMesh(axis_name="sequencer", num_cores=1), name="gather_shards_later",
               scratch_types=(pltpu.SemaphoreType.DMA((6 * n,)), pltpu.SemaphoreType.DMA((6 * n,))),
               compiler_params=pltpu.CompilerParams(collective_id=GATHER_LATER_ID))
    def launch(send_sems, recv_sems):
        x, y, c = _position()
        barrier = pltpu.get_barrier_semaphore()
        for peer in [(x, y, 1 - c), (1 - x, y, c), (x, 1 - y, c), (1 - x, 1 - y, c)]:
            pl.semaphore_signal(barrier, inc=1, device_id=peer, device_id_type=MESH)
        pl.semaphore_wait(barrier, 4)
        _gather_body(w_refs, out_refs, send_sems, recv_sems)

    launch()
    return [r[...] for r in out_refs]


def _pair_exchange(gps):
    n = len(gps)

    def body(*refs):
        g_refs, out_refs, (send_sems, recv_sems) = refs[:n], refs[n:2 * n], refs[2 * n:]
        x, y, c = _position()
        copies = []
        for t, (g_ref, out_ref) in enumerate(zip(g_refs, out_refs)):
            half = g_ref.shape[1] // 2
            copies += [pltpu.make_async_remote_copy(
                src_ref=g_ref.at[k, pl.ds((1 - c) * half, half), :], dst_ref=out_ref.at[k], send_sem=send_sems.at[N_CHIPS * t + k],
                recv_sem=recv_sems.at[N_CHIPS * t + k], device_id=(x, y, 1 - c), device_id_type=MESH) for k in range(N_CHIPS)]
        for cp in copies:
            cp.start()
        for cp in copies:
            cp.wait()

    return pl.pallas_call(
        body, name="pair_exchange", in_specs=[ANY] * n, out_specs=[ANY] * n,
        out_shape=[jax.ShapeDtypeStruct((N_CHIPS, gp.shape[1] // 2, gp.shape[2]), gp.dtype) for gp in gps],
        scratch_shapes=[pltpu.SemaphoreType.DMA((N_CHIPS * n,)), pltpu.SemaphoreType.DMA((N_CHIPS * n,))],
    )(*gps)


SUM_ROWS = (256, 128)


def _pair_sum(name, gp, got, core):
    n, R, C = gp.shape
    half = R // 2
    tr = _tile(half, SUM_ROWS)
    nb = half // tr

    def body(core_ref, g_ref, r_ref, o_ref):
        o_ref[...] = (g_ref[...].astype(F32) + r_ref[...].astype(F32)).astype(o_ref.dtype)

    return pl.pallas_call(
        body, name=name,
        grid_spec=pltpu.PrefetchScalarGridSpec(
            num_scalar_prefetch=1, grid=(n, nb),
            in_specs=[pl.BlockSpec((None, tr, C), lambda k, i, core_ref: (k, core_ref[0] * nb + i, 0)),
                      pl.BlockSpec((None, tr, C), lambda k, i, core_ref: (k, i, 0))],
            out_specs=pl.BlockSpec((None, tr, C), lambda k, i, core_ref: (k, i, 0))),
        out_shape=jax.ShapeDtypeStruct((n, half, C), gp.dtype), compiler_params=_params("parallel", "parallel"),
    )(core, gp, got)


def _chip_exchange(ts):
    n = len(ts)

    def body(*refs):
        t_refs, out_refs, (send_sems, recv_sems) = refs[:n], refs[n:2 * n], refs[2 * n:]
        x, y, c = _position()
        chips = [(1 - x, y), (x, 1 - y), (1 - x, 1 - y)]
        copies, waits = [], []
        for t, (t_ref, out_ref) in enumerate(zip(t_refs, out_refs)):
            for j, (cx, cy) in enumerate(chips):
                sems = dict(send_sem=send_sems.at[3 * t + j], recv_sem=recv_sems.at[3 * t + j], device_id=(cx, cy, c),
                            device_id_type=MESH)
                copies.append(pltpu.make_async_remote_copy(src_ref=t_ref.at[2 * cx + cy], dst_ref=out_ref.at[2 * x + y], **sems))
                waits.append(pltpu.make_async_remote_copy(src_ref=t_ref.at[2 * cx + cy], dst_ref=out_ref.at[2 * cx + cy], **sems))
        for cp in copies:
            cp.start()
        for cp in waits:
            cp.wait_recv()
        for cp in copies:
            cp.wait_send()

    return pl.pallas_call(
        body, name="chip_exchange", in_specs=[ANY] * n, out_specs=[ANY] * n,
        out_shape=[jax.ShapeDtypeStruct(t.shape, t.dtype) for t in ts],
        scratch_shapes=[pltpu.SemaphoreType.DMA((3 * n,)), pltpu.SemaphoreType.DMA((3 * n,))],
    )(*ts)


def _chip_sum(name, t, got, ids):
    n, H, C = t.shape
    tr = _tile(H, SUM_ROWS)
    nb = H // tr

    def body(ids_ref, t_ref, r_ref, o_ref):
        own = t_ref[...].astype(F32)
        acc = jnp.where(ids_ref[0] == 0, own, r_ref[0].astype(F32))
        for k in range(1, n):
            acc = acc + jnp.where(ids_ref[0] == k, own, r_ref[k].astype(F32))
        o_ref[...] = acc

    return pl.pallas_call(
        body, name=name,
        grid_spec=pltpu.PrefetchScalarGridSpec(
            num_scalar_prefetch=1, grid=(nb,),
            in_specs=[pl.BlockSpec((None, tr, C), lambda i, ids_ref: (ids_ref[0], i, 0)),
                      pl.BlockSpec((n, tr, C), lambda i, ids_ref: (0, i, 0))],
            out_specs=pl.BlockSpec((tr, C), lambda i, ids_ref: (ids_ref[1] * nb + i, 0))),
        out_shape=jax.ShapeDtypeStruct((2 * H, C), F32), compiler_params=_params("parallel"),
    )(ids, t, got)


def _sum_pieces(name, pieces):
    n, R, C = pieces.shape
    tr = _tile(R, (256, 128, SUBLANES))

    def body(p_ref, o_ref):
        acc = p_ref[0].astype(F32)
        for s in range(1, n):
            acc = acc + p_ref[s].astype(F32)
        o_ref[...] = acc

    return pl.pallas_call(
        body, name=name, grid=(R // tr,),
        in_specs=[pl.BlockSpec((n, tr, C), lambda i: (0, i, 0))], out_specs=pl.BlockSpec((tr, C), lambda i: (i, 0)),
        out_shape=jax.ShapeDtypeStruct((R, C), F32), compiler_params=_params("parallel"),
    )(pieces)


def _swap_halves(ss):
    n = len(ss)

    def body(*refs):
        s_refs, out_refs, (send_sems, recv_sems) = refs[:n], refs[n:2 * n], refs[2 * n:]
        x, y, c = _position()
        copies, waits = [], []
        for t, (s_ref, out_ref) in enumerate(zip(s_refs, out_refs)):
            half = s_ref.shape[0] // 2
            sems = dict(send_sem=send_sems.at[t], recv_sem=recv_sems.at[t], device_id=(x, y, 1 - c), device_id_type=MESH)
            mine = s_ref.at[pl.ds(c * half, half), :]
            copies.append(pltpu.make_async_remote_copy(src_ref=mine, dst_ref=out_ref.at[pl.ds(c * half, half), :], **sems))
            waits.append(pltpu.make_async_remote_copy(src_ref=mine, dst_ref=out_ref.at[pl.ds((1 - c) * half, half), :], **sems))
        for cp in copies:
            cp.start()
        for cp in waits:
            cp.wait_recv()
        for cp in copies:
            cp.wait_send()

    return pl.pallas_call(
        body, name="swap_halves", in_specs=[ANY] * n, out_specs=[ANY] * n, input_output_aliases={i: i for i in range(n)},
        out_shape=[jax.ShapeDtypeStruct(s_.shape, s_.dtype) for s_ in ss],
        scratch_shapes=[pltpu.SemaphoreType.DMA((n,)), pltpu.SemaphoreType.DMA((n,))],
    )(*ss)


def _gather_small(name, blk):
    m_per, n = blk.shape

    def body(x_ref, out_ref, send_sems, recv_sems, local_sem):
        x, y, c = _position()
        me, sibling = (x, y, c), (x, y, 1 - c)
        chips = [(1 - x, y), (x, 1 - y), (1 - x, 1 - y)]

        def rows(px, py, pc):
            return out_ref.at[pl.ds((4 * px + 2 * py + pc) * m_per, m_per), :]

        def copy(k, block, to, src=None):
            return pltpu.make_async_remote_copy(src_ref=rows(*block) if src is None else src, dst_ref=rows(*block),
                                                send_sem=send_sems.at[k], recv_sem=recv_sems.at[k], device_id=to, device_id_type=MESH)

        mine = pltpu.make_async_copy(x_ref, rows(*me), local_sem)
        mine.start()
        first = [copy(0, me, sibling, src=x_ref)] + [copy(1 + j, me, (*chip, c), src=x_ref) for j, chip in enumerate(chips)]
        for cp in first:
            cp.start()
        passed = [copy(4 + j, (*chip, c), sibling) for j, chip in enumerate(chips)]
        for j, chip in enumerate(chips):
            copy(1 + j, (*chip, c), me).wait_recv()
            passed[j].start()
        copy(0, sibling, me).wait_recv()
        for j, chip in enumerate(chips):
            copy(4 + j, (*chip, 1 - c), me).wait_recv()
        for cp in first + passed:
            cp.wait_send()
        mine.wait()

    return pl.pallas_call(
        body, name=name, out_shape=jax.ShapeDtypeStruct((N_DEV * m_per, n), blk.dtype),
        in_specs=[pl.BlockSpec(memory_space=pltpu.VMEM)], out_specs=pl.BlockSpec(memory_space=pltpu.VMEM),
        scratch_shapes=[pltpu.SemaphoreType.DMA((7,)), pltpu.SemaphoreType.DMA((7,)), pltpu.SemaphoreType.DMA],
        compiler_params=pltpu.CompilerParams(vmem_limit_bytes=VMEM_LIMIT_BYTES),
    )(blk)


WEIGHTS = ("norm_mix_g", "norm_mlp_g", "mlp_w1", "mlp_w2", "gdn_w_in", "gdn_conv_w", "gdn_a_log", "gdn_dt_bias", "gdn_o_norm_g",
           "gdn_w_out", "s5_w_in", "s5_lam_re", "s5_lam_im", "s5_log_dt", "s5_b_re", "s5_b_im", "s5_c_re", "s5_c_im", "s5_d",
           "s5_w_out", "m2_w_in", "m2_conv_w", "m2_conv_b", "m2_dt_bias", "m2_a_log", "m2_d", "m2_norm_g", "m2_w_out",
           "final_norm_g")
BIG = {"mlp_w1": 2, "mlp_w2": 1, "gdn_w_in": 2, "gdn_w_out": 1, "s5_w_in": 1, "s5_w_out": 2, "m2_w_in": 2, "m2_w_out": 1}
SMALL_CUT = {"gdn_conv_w": 2, "m2_conv_w": 2, "m2_conv_b": 1, "m2_norm_g": 1}
GROUP_A = ("mlp_w1", "mlp_w2", "gdn_w_out", "s5_w_in", "m2_w_out")
GROUPS = (GROUP_A, ("gdn_w_in",), ("m2_w_in",), ("s5_w_out",))
WEIGHTS_NOW = ((("mlp_w1", 0), ("mlp_w2", 0), ("gdn_w_out", 0)), (("gdn_w_in", 0),))
WEIGHTS_LATER = ((("mlp_w1", 1), ("mlp_w1", 2), ("mlp_w1", 3), ("mlp_w2", 1), ("mlp_w2", 2), ("mlp_w2", 3), ("gdn_w_out", 1),
                  ("s5_w_in", 0), ("m2_w_out", 0)), (("gdn_w_in", 1),), (("m2_w_in", 0),), (("s5_w_out", 0),))


def _rows2d(a):
    return a.reshape(-1, a.shape[-1])


def _pack(arrays, cols, row_multiple, dtype):
    flat = jnp.concatenate([a.reshape(-1).astype(dtype) for a in arrays])
    n = -(-flat.shape[0] // (cols * row_multiple)) * cols * row_multiple
    return jnp.pad(flat, (0, n - flat.shape[0])).reshape(-1, cols)


def _unpack(packed, shapes):
    flat = packed.reshape(-1)
    out, off = [], 0
    for shp in shapes:
        n = math.prod(shp)
        out.append(flat[off:off + n].reshape(shp))
        off += n
    return out


def _split_rows(buf, shapes):
    out, off = [], 0
    for shp in shapes:
        rows = math.prod(shp[:-1])
        out.append(buf[off:off + rows].reshape(shp))
        off += rows
    return out


def _cut(a, axis, k):
    n = a.shape[axis] // N_CHIPS
    return lax.slice_in_dim(a, k * n, (k + 1) * n, axis=axis)


def kernel(x, norm_mix_g, norm_mlp_g, mlp_w1, mlp_w2, gdn_w_in, gdn_conv_w, gdn_a_log, gdn_dt_bias, gdn_o_norm_g, gdn_w_out, s5_w_in, s5_lam_re, s5_lam_im, s5_log_dt, s5_b_re, s5_b_im, s5_c_re, s5_c_im, s5_d, s5_w_out, m2_w_in, m2_conv_w, m2_conv_b, m2_dt_bias, m2_a_log, m2_d, m2_norm_g, m2_w_out, final_norm_g, loss_target, m_norm_mix_g, m_norm_mlp_g, m_mlp_w1, m_mlp_w2, m_gdn_w_in, m_gdn_conv_w, m_gdn_a_log, m_gdn_dt_bias, m_gdn_o_norm_g, m_gdn_w_out, m_s5_w_in, m_s5_lam_re, m_s5_lam_im, m_s5_log_dt, m_s5_b_re, m_s5_b_im, m_s5_c_re, m_s5_c_im, m_s5_d, m_s5_w_out, m_m2_w_in, m_m2_conv_w, m_m2_conv_b, m_m2_dt_bias, m_m2_a_log, m_m2_d, m_m2_norm_g, m_m2_w_out, m_final_norm_g, v_norm_mix_g, v_norm_mlp_g, v_mlp_w1, v_mlp_w2, v_gdn_w_in, v_gdn_conv_w, v_gdn_a_log, v_gdn_dt_bias, v_gdn_o_norm_g, v_gdn_w_out, v_s5_w_in, v_s5_lam_re, v_s5_lam_im, v_s5_log_dt, v_s5_b_re, v_s5_b_im, v_s5_c_re, v_s5_c_im, v_s5_d, v_s5_w_out, v_m2_w_in, v_m2_conv_w, v_m2_conv_b, v_m2_dt_bias, v_m2_a_log, v_m2_d, v_m2_norm_g, v_m2_w_out, v_final_norm_g):
    given = dict(locals())
    w = {n: given[n] for n in WEIGHTS}
    mom = {n: given["m_" + n] for n in WEIGHTS}
    var = {n: given["v_" + n] for n in WEIGHTS}
    big, small_cut = tuple(BIG), tuple(SMALL_CUT)
    small = tuple(n for n in WEIGHTS if n not in BIG)
    chip = 2 * lax.axis_index("x") + lax.axis_index("y")

    W = {n: [None] * w[n].shape[0] for n in big}

    def take(groups, gather):
        own = [jnp.concatenate([w[n][l] for n, l in grp]).astype(BF16) for grp in groups]
        for grp, mine, got in zip(groups, own, gather(own)):
            shapes = [w[n][l].shape for n, l in grp]
            per_chip = [_split_rows(jnp.where(chip == k, mine, got[k]), shapes) for k in range(N_CHIPS)]
            for i, (n, l) in enumerate(grp):
                W[n][l] = jnp.concatenate([per_chip[k][i] for k in range(N_CHIPS)], axis=BIG[n] - 1)

    take(WEIGHTS_LATER, _gather_shards_later)
    take(WEIGHTS_NOW, _gather_shards)
    W["gdn_w_in"] = [jnp.pad(m, ((0, 0), (0, GDN_IN_PAD - GDN_IN))) for m in W["gdn_w_in"]]
    W["m2_w_in"] = [jnp.pad(m, ((0, 0), (0, M2_IN_PAD - M2_IN))) for m in W["m2_w_in"]]
    cut_blk = _pack([w[n] for n in small_cut], LANES, SUBLANES, F32)
    cut_all = _gather_small("gather_small_params", cut_blk).reshape(N_DEV, *cut_blk.shape)
    per_chip = [_unpack(cut_all[2 * k], [w[n].shape for n in small_cut]) for k in range(N_CHIPS)]
    W.update({n: jnp.concatenate([per_chip[k][i] for k in range(N_CHIPS)], axis=SMALL_CUT[n]) for i, n in enumerate(small_cut)})
    W.update({n: w[n] for n in small if n not in SMALL_CUT})

    loss, grad_x, G = _local_step(x[0], loss_target[0], W)
    loss = lax.psum(loss[0, 0], ("x", "y", "c"))

    gps = [jnp.stack([jnp.concatenate([_cut(g, BIG[n] - 1, k) for n in grp for g in G[n]]).astype(BF16) for k in range(N_CHIPS)])
           for grp in GROUPS]
    core = lax.axis_index("c").astype(jnp.int32)
    ids = jnp.stack([chip.astype(jnp.int32), core])
    pairs = [_pair_sum(f"pair_sum{i}", gp, got, core.reshape(1)) for i, (gp, got) in enumerate(zip(gps, _pair_exchange(gps)))]
    sums = [_chip_sum(f"chip_sum{i}", t, got, ids) for i, (t, got) in enumerate(zip(pairs, _chip_exchange(pairs)))]
    grads = {}
    for grp, g_shard in zip(GROUPS, _swap_halves(sums)):
        grads.update(zip(grp, _split_rows(g_shard, [w[n].shape for n in grp])))
    sg = _pack([G[n] for n in small], LANES, ADAM_ROWS, F32)
    sg_sum = _sum_pieces("sum_small_grads", _gather_small("gather_small_grads", sg).reshape(N_DEV, *sg.shape))
    for n, g in zip(small, _unpack(sg_sum, [G[n].shape for n in small])):
        if n in SMALL_CUT:
            width = g.shape[SMALL_CUT[n]] // N_CHIPS
            g = lax.dynamic_slice_in_dim(g, chip * width, width, axis=SMALL_CUT[n])
        grads[n] = g.reshape(w[n].shape)

    delta, new_m, new_v = {}, {}, {}
    for n in big:
        as2d = lambda a: a.reshape(-1, a.shape[-1])
        outs = _adamw("adamw_" + n, as2d(w[n]), as2d(grads[n]), as2d(mom[n]), as2d(var[n]))
        delta[n], new_m[n], new_v[n] = (o.reshape(w[n].shape) for o in outs)
    packs = [_pack([t[n] for n in small], LANES, ADAM_ROWS, F32) for t in (w, grads, mom, var)]
    outs = _adamw("adamw_small", *packs)
    for t, o in zip((delta, new_m, new_v), outs):
        t.update(zip(small, _unpack(o, [w[n].shape for n in small])))

    return (loss, grad_x[None], *[grads[n] for n in WEIGHTS], *[delta[n] for n in WEIGHTS], *[new_m[n] for n in WEIGHTS],
            *[new_v[n] for n in WEIGHTS])
```

```python
import functools
import math

import numpy as np
import jax
import jax.numpy as jnp
from jax import lax
from jax.experimental import pallas as pl
from jax.experimental.pallas import tpu as pltpu
from jax.experimental.pallas import tpu_sc as plsc

F32 = jnp.float32
BF16 = jnp.bfloat16

D_MODEL = 1024
D_FF = 4096
DEPTH = 4
CHUNK = 64
RMS_EPS = 1e-6
CONV_W = 4
GDN_HEADS = 8
GDN_DK = 128
GDN_IN = 4112
GDN_IN_PAD = 4224
S5_GROUPS = 64
S5_STATE = 64
S5_GROUP = 16
S5_BLOCKS = 8
M2_INNER = 2048
M2_HEADS = 32
M2_GROUPS = 8
M2_STATE = 128
M2_CONV_CH = 4096
M2_IN = 6176
M2_IN_PAD = 6272
ADAM_LR, ADAM_B1, ADAM_B2, ADAM_EPS, ADAM_WD, ADAM_STEP = 0.001, 0.9, 0.999, 1e-08, 0.01, 10

VMEM_LIMIT_BYTES = 56 * 1024 * 1024
SUBLANES = 8
LANES = 128


def _params(*sem):
    return pltpu.CompilerParams(dimension_semantics=tuple(sem) if sem else None, vmem_limit_bytes=VMEM_LIMIT_BYTES)


NN, NT, TN = ((1,), (0,)), ((1,), (1,)), ((0,), (0,))
_DOT_TRANSPOSES = {NN: ((NT, "gb"), (TN, "ag")), NT: ((NN, "gb"), (TN, "ga")), TN: ((NT, "bg"), (NN, "ag"))}


def _dg(a, b, dims):
    if a.ndim == 3:
        dn = (((dims[0][0] + 1,), (dims[1][0] + 1,)), ((0,), (0,)))
    else:
        dn = (dims, ((), ()))
    return lax.dot_general(a, b, dn, preferred_element_type=F32)


def _mxu(a, b, dims):
    return _dg(a.astype(BF16), b.astype(BF16), dims)


@functools.partial(jax.custom_vjp, nondiff_argnums=(2,))
def _dot(a, b, dims=NN):
    return _mxu(a, b, dims)


def _dot_fwd(a, b, dims):
    return _mxu(a, b, dims), (a, b)


def _dot_bwd(dims, res, g):
    ops = dict(a=res[0], b=res[1], g=g)
    (da_dims, da_ops), (db_dims, db_ops) = _DOT_TRANSPOSES[dims]
    return (_mxu(ops[da_ops[0]], ops[da_ops[1]], da_dims).astype(res[0].dtype),
            _mxu(ops[db_ops[0]], ops[db_ops[1]], db_dims).astype(res[1].dtype))


_dot.defvjp(_dot_fwd, _dot_bwd)


def _nt(a, b):
    return _dot(a, b, NT)


def _tn(a, b):
    return _dot(a, b, TN)


def _split3(x):
    x1 = x.astype(BF16)
    r = x - x1.astype(F32)
    x2 = r.astype(BF16)
    return x1, x2, (r - x2.astype(F32)).astype(BF16)


def _sel_mxu(x, sel, dims, x_first):
    f = (lambda p: _dg(p, sel.astype(BF16), dims)) if x_first else (lambda p: _dg(sel.astype(BF16), p, dims))
    x1, x2, x3 = _split3(x)
    return f(x1) + (f(x2) + f(x3))


@jax.custom_vjp
def _pick(x, sel):
    return _sel_mxu(x, sel, NN, True)


def _pick_fwd(x, sel):
    return _sel_mxu(x, sel, NN, True), sel


def _pick_bwd(sel, g):
    return _sel_mxu(g, sel, NT, True), jnp.zeros_like(sel)


_pick.defvjp(_pick_fwd, _pick_bwd)


@jax.custom_vjp
def _accum(sel, x):
    return _sel_mxu(x, sel, NN, False)


def _accum_fwd(sel, x):
    return _sel_mxu(x, sel, NN, False), sel


def _accum_bwd(sel, g):
    return jnp.zeros_like(sel), _sel_mxu(g, sel, TN, False)


_accum.defvjp(_accum_fwd, _accum_bwd)


def _dot3(a, b, dims=NN):
    ah, bh = a.astype(BF16), b.astype(BF16)
    al, bl = (a - ah.astype(F32)).astype(BF16), (b - bh.astype(F32)).astype(BF16)
    return _dg(ah, bh, dims) + (_dg(ah, bl, dims) + _dg(al, bh, dims))


def _neumann(x, r, dims):
    r = r + _dot3(x, r, dims)
    for _ in range(5):
        x = _dot3(x, x)
        r = r + _dot3(x, r, dims)
    return r


@jax.custom_vjp
def _unit_lower_solve(a, rhs):
    return _neumann(-a, rhs, NN)


def _unit_lower_solve_fwd(a, rhs):
    sol = _neumann(-a, rhs, NN)
    return sol, (a, sol)


def _unit_lower_solve_bwd(res, ct):
    a, sol = res
    d_rhs = _neumann(-a, ct, TN)
    return -_dot3(d_rhs, sol, NT), d_rhs


_unit_lower_solve.defvjp(_unit_lower_solve_fwd, _unit_lower_solve_bwd)


def _sigmoid(x):
    return 1.0 / (1.0 + jnp.exp(-x))


def _softplus(x):
    return jnp.maximum(x, 0.0) + jnp.log(1.0 + jnp.exp(-jnp.abs(x)))


def _iota2(shape, axis):
    return lax.broadcasted_iota(jnp.int32, shape, axis)


def _tile(n, cands):
    for c in cands:
        if n % c == 0:
            return c
    return n


MM_TILE_BYTES = 9 * 1024 * 1024


def _mm(name, a, b, mode, out_dtypes, epi=None, extras=(), tn=None):
    if mode == "nn":
        (M, K), N = a.shape, b.shape[1]
    elif mode == "nt":
        (M, K), N = a.shape, b.shape[0]
    else:
        (K, M), N = a.shape, b.shape[1]
    tn = tn or _tile(N, (512, 384, 896, 256, 128))
    out_bytes = tn * (sum(jnp.dtype(d).itemsize for d in out_dtypes) + sum(e.dtype.itemsize for e in extras))
    fits = lambda t: t * K * a.dtype.itemsize <= MM_TILE_BYTES and t * out_bytes <= MM_TILE_BYTES
    tm = next(t for t in (2048, 1024, 512, 256, 128) if M % t == 0 and (fits(t) or t == 128))
    if mode == "nn":
        a_spec, b_spec = pl.BlockSpec((tm, K), lambda i, j: (i, 0)), pl.BlockSpec((K, tn), lambda i, j: (0, j))
        dims = NN
    elif mode == "nt":
        a_spec, b_spec = pl.BlockSpec((tm, K), lambda i, j: (i, 0)), pl.BlockSpec((tn, K), lambda i, j: (j, 0))
        dims = NT
    else:
        a_spec, b_spec = pl.BlockSpec((K, tm), lambda i, j: (0, i)), pl.BlockSpec((K, tn), lambda i, j: (0, j))
        dims = TN
    n_ex = len(extras)

    def body(a_ref, b_ref, *rest):
        acc = _mxu(a_ref[...], b_ref[...], dims)
        res = epi(acc, *[e[...] for e in rest[:n_ex]]) if epi is not None else (acc,)
        for o_ref, r in zip(rest[n_ex:], res):
            o_ref[...] = r.astype(o_ref.dtype)

    tile = pl.BlockSpec((tm, tn), lambda i, j: (i, j))
    out = pl.pallas_call(
        body, name=name, grid=(M // tm, N // tn),
        in_specs=[a_spec, b_spec] + [tile] * n_ex,
        out_specs=[tile] * len(out_dtypes),
        out_shape=[jax.ShapeDtypeStruct((M, N), d) for d in out_dtypes],
        compiler_params=_params("parallel", "parallel"),
    )(a, b, *extras)
    return out if len(out_dtypes) > 1 else out[0]


def _rms_fwd(name, h, g):
    L, D = h.shape
    tr = _tile(L, (256, 128))

    def body(h_ref, g_ref, o_ref):
        x = h_ref[...]
        r = lax.rsqrt(jnp.mean(x * x, axis=-1, keepdims=True) + RMS_EPS)
        o_ref[...] = (x * r * g_ref[...]).astype(o_ref.dtype)

    return pl.pallas_call(
        body, name=name, grid=(L // tr,),
        in_specs=[pl.BlockSpec((tr, D), lambda i: (i, 0)), pl.BlockSpec((1, D), lambda i: (0, 0))],
        out_specs=pl.BlockSpec((tr, D), lambda i: (i, 0)),
        out_shape=jax.ShapeDtypeStruct((L, D), BF16),
        compiler_params=_params("parallel"),
    )(h, g.reshape(1, D))


def _rms_bwd(name, h, g, dhn, dres):
    L, D = h.shape
    tr = _tile(L, (256, 128))

    def body(h_ref, g_ref, dhn_ref, dres_ref, dh_ref, dg_ref):
        x = h_ref[...]
        r = lax.rsqrt(jnp.mean(x * x, axis=-1, keepdims=True) + RMS_EPS)
        xh = x * r
        dy = dhn_ref[...]
        dxh = dy * g_ref[...]
        dh_ref[...] = dres_ref[...] + r * (dxh - xh * jnp.mean(dxh * xh, axis=-1, keepdims=True))

        @pl.when(pl.program_id(0) == 0)
        def _():
            dg_ref[...] = jnp.zeros_like(dg_ref)

        dg_ref[...] += jnp.sum(dy * xh, axis=0, keepdims=True)

    row = pl.BlockSpec((tr, D), lambda i: (i, 0))
    vec = pl.BlockSpec((1, D), lambda i: (0, 0))
    return pl.pallas_call(
        body, name=name, grid=(L // tr,),
        in_specs=[row, vec, row, row], out_specs=[row, vec],
        out_shape=[jax.ShapeDtypeStruct((L, D), F32), jax.ShapeDtypeStruct((1, D), F32)],
        compiler_params=_params("arbitrary"),
    )(h, g.reshape(1, D), dhn, dres)


def _loss_head(h, g, target):
    L, D = h.shape
    tr = _tile(L, (256, 128))

    def body(h_ref, g_ref, t_ref, loss_ref, dh_ref, dg_ref):
        x = h_ref[...]
        r = lax.rsqrt(jnp.mean(x * x, axis=-1, keepdims=True) + RMS_EPS)
        xh = x * r
        err = xh * g_ref[...] - t_ref[...]
        dy = err * (1.0 / D)
        dxh = dy * g_ref[...]
        dh_ref[...] = r * (dxh - xh * jnp.mean(dxh * xh, axis=-1, keepdims=True))

        @pl.when(pl.program_id(0) == 0)
        def _():
            dg_ref[...] = jnp.zeros_like(dg_ref)
            loss_ref[...] = jnp.zeros_like(loss_ref)

        dg_ref[...] += jnp.sum(dy * xh, axis=0, keepdims=True)
        loss_ref[...] += (0.5 / D) * jnp.sum(jnp.sum(err * err, axis=-1, keepdims=True), axis=0, keepdims=True)

    row = pl.BlockSpec((tr, D), lambda i: (i, 0))
    vec = pl.BlockSpec((1, D), lambda i: (0, 0))
    return pl.pallas_call(
        body, name="loss_head", grid=(L // tr,),
        in_specs=[row, vec, row], out_specs=[pl.BlockSpec((1, 1), lambda i: (0, 0)), row, vec],
        out_shape=[jax.ShapeDtypeStruct((1, 1), F32), jax.ShapeDtypeStruct((L, D), F32), jax.ShapeDtypeStruct((1, D), F32)],
        compiler_params=_params("arbitrary"),
    )(h, g.reshape(1, D), target)


def _glu_fwd(h, ag):
    L, D = h.shape
    tr = _tile(L, (256, 128))

    def body(h_ref, v_ref, g_ref, o_ref):
        o_ref[...] = h_ref[...] + v_ref[...] * _sigmoid(g_ref[...])

    return pl.pallas_call(
        body, name="s5_glu_fwd", grid=(L // tr,),
        in_specs=[pl.BlockSpec((tr, D), lambda i: (i, 0)), pl.BlockSpec((tr, D), lambda i: (i, 0)),
                  pl.BlockSpec((tr, D), lambda i: (i, 1))],
        out_specs=pl.BlockSpec((tr, D), lambda i: (i, 0)),
        out_shape=jax.ShapeDtypeStruct((L, D), F32),
        compiler_params=_params("parallel"),
    )(h, ag, ag)


def _glu_bwd(dh, ag):
    L, D = dh.shape
    tr = _tile(L, (256, 128))

    def body(dh_ref, v_ref, g_ref, dv_ref, dg_ref):
        s = _sigmoid(g_ref[...])
        d = dh_ref[...]
        dv_ref[...] = d * s
        dg_ref[...] = d * v_ref[...] * s * (1.0 - s)

    dv, dg = pl.pallas_call(
        body, name="s5_glu_bwd", grid=(L // tr,),
        in_specs=[pl.BlockSpec((tr, D), lambda i: (i, 0)), pl.BlockSpec((tr, D), lambda i: (i, 0)),
                  pl.BlockSpec((tr, D), lambda i: (i, 1))],
        out_specs=[pl.BlockSpec((tr, D), lambda i: (i, 0))] * 2,
        out_shape=[jax.ShapeDtypeStruct((L, D), F32)] * 2,
        compiler_params=_params("parallel"),
    )(dh, ag, ag)
    return jnp.concatenate([dv, dg], axis=1)


CONV_ROWS = 128
CONV_COLS = 512


def _shift_rows(cat, s):
    if s == 0:
        return cat[SUBLANES:, :]
    return pltpu.roll(cat, s, axis=0)[SUBLANES:, :]


def _conv_fwd(name, p, col0, w, b):
    L = p.shape[0]
    C = w.shape[1]
    tc = _tile(C, (CONV_COLS, 256))
    cb0 = col0 // tc
    nr = L // CONV_ROWS

    def body(x_ref, w_ref, b_ref, o_ref):
        def step(r, carry):
            r0 = pl.multiple_of(r * CONV_ROWS, CONV_ROWS)
            cur = x_ref[pl.ds(r0, CONV_ROWS), :]
            p0 = pl.multiple_of(jnp.maximum(r0 - SUBLANES, 0), SUBLANES)
            prev = jnp.where(r > 0, x_ref[pl.ds(p0, SUBLANES), :], 0.0)
            cat = jnp.concatenate([prev, cur], axis=0)
            acc = b_ref[...] + w_ref[3:4, :] * cur
            for k in range(CONV_W - 1):
                acc = acc + w_ref[k:k + 1, :] * _shift_rows(cat, CONV_W - 1 - k)
            o_ref[pl.ds(r0, CONV_ROWS), :] = acc * _sigmoid(acc)
            return carry

        lax.fori_loop(0, nr, step, 0)

    return pl.pallas_call(
        body, name=name, grid=(C // tc,),
        in_specs=[pl.BlockSpec((L, tc), lambda j: (0, cb0 + j)), pl.BlockSpec((CONV_W, tc), lambda j: (0, j)),
                  pl.BlockSpec((1, tc), lambda j: (0, j))],
        out_specs=pl.BlockSpec((L, tc), lambda j: (0, j)),
        out_shape=jax.ShapeDtypeStruct((L, C), F32),
        compiler_params=_params("parallel"),
    )(p, w, b)


def _conv_bwd(name, p, col0, w, b, dout):
    L = p.shape[0]
    C = w.shape[1]
    tc = _tile(C, (CONV_COLS, 256))
    cb0 = col0 // tc
    nr = L // CONV_ROWS

    def body(x_ref, w_ref, b_ref, do_ref, dx_ref, dw_ref, db_ref, dpre_ref):
        def step1(r, carry):
            dw0, dw1, dw2, dw3, dbb = carry
            r0 = pl.multiple_of(r * CONV_ROWS, CONV_ROWS)
            cur = x_ref[pl.ds(r0, CONV_ROWS), :]
            p0 = pl.multiple_of(jnp.maximum(r0 - SUBLANES, 0), SUBLANES)
            prev = jnp.where(r > 0, x_ref[pl.ds(p0, SUBLANES), :], 0.0)
            cat = jnp.concatenate([prev, cur], axis=0)
            sh = [_shift_rows(cat, CONV_W - 1 - k) for k in range(CONV_W - 1)] + [cur]
            acc = b_ref[...] + w_ref[3:4, :] * cur
            for k in range(CONV_W - 1):
                acc = acc + w_ref[k:k + 1, :] * sh[k]
            sg = _sigmoid(acc)
            dpre = do_ref[pl.ds(r0, CONV_ROWS), :] * (sg + acc * sg * (1.0 - sg))
            dpre_ref[pl.ds(r0, CONV_ROWS), :] = dpre
            dws = [d + jnp.sum(dpre * s, axis=0, keepdims=True) for d, s in zip((dw0, dw1, dw2, dw3), sh)]
            return (*dws, dbb + jnp.sum(dpre, axis=0, keepdims=True))

        z = jnp.zeros((1, tc), F32)
        dw0, dw1, dw2, dw3, dbb = lax.fori_loop(0, nr, step1, (z, z, z, z, z))
        dw_ref[...] = jnp.concatenate([dw0, dw1, dw2, dw3, z, z, z, z], axis=0)
        db_ref[...] = dbb

        def step2(r, carry):
            r0 = pl.multiple_of(r * CONV_ROWS, CONV_ROWS)
            cur = dpre_ref[pl.ds(r0, CONV_ROWS), :]
            n0 = pl.multiple_of(jnp.minimum(r0 + CONV_ROWS, L - SUBLANES), SUBLANES)
            nxt = jnp.where(r < nr - 1, dpre_ref[pl.ds(n0, SUBLANES), :], 0.0)
            cat = jnp.concatenate([cur, nxt], axis=0)
            acc = w_ref[3:4, :] * cur
            for k in range(CONV_W - 1):
                s = CONV_W - 1 - k
                acc = acc + w_ref[k:k + 1, :] * pltpu.roll(cat, CONV_ROWS + SUBLANES - s, axis=0)[:CONV_ROWS, :]
            dx_ref[pl.ds(r0, CONV_ROWS), :] = acc
            return carry

        lax.fori_loop(0, nr, step2, 0)

    dx, dw, db = pl.pallas_call(
        body, name=name, grid=(C // tc,),
        in_specs=[pl.BlockSpec((L, tc), lambda j: (0, cb0 + j)), pl.BlockSpec((CONV_W, tc), lambda j: (0, j)),
                  pl.BlockSpec((1, tc), lambda j: (0, j)), pl.BlockSpec((L, tc), lambda j: (0, j))],
        out_specs=[pl.BlockSpec((L, tc), lambda j: (0, j)), pl.BlockSpec((SUBLANES, tc), lambda j: (0, j)),
                   pl.BlockSpec((1, tc), lambda j: (0, j))],
        out_shape=[jax.ShapeDtypeStruct((L, C), F32), jax.ShapeDtypeStruct((SUBLANES, C), F32),
                   jax.ShapeDtypeStruct((1, C), F32)],
        scratch_shapes=[pltpu.VMEM((L, tc), F32)],
        compiler_params=_params("parallel"),
    )(p, w, b, dout)
    return dx, dw[:CONV_W], db


def _chunk_consts():
    r, c = _iota2((CHUNK, CHUNK), 0), _iota2((CHUNK, CHUNK), 1)
    causal = r >= c
    return causal, r > c, (r == c).astype(F32), causal.astype(F32), jnp.ones((CHUNK, CHUNK), F32)


def _by_lanes(t):
    return jnp.concatenate([t[i] for i in range(t.shape[0])], axis=1)


def _by_batch(t, w):
    return jnp.concatenate([t[None, :, i * w:(i + 1) * w] for i in range(t.shape[1] // w)], axis=0)


def _diag_lanes():
    return (_iota2((CHUNK, LANES), 0) == _iota2((CHUNK, LANES), 1)).astype(F32)


def _gdn_chunk(q, k, v, ab, gate, S, alog, dtb, og, ea, eb):
    causal, strict, _, tril, ones = _chunk_consts()
    logits = _by_batch(_pick(ab, jnp.concatenate([_by_lanes(ea), _by_lanes(eb)], axis=1)), LANES)
    H = q.shape[0]
    g = -jnp.exp(alog) * _softplus(logits[:H] + dtb)
    beta = _sigmoid(logits[H:])
    qn = q * lax.rsqrt(jnp.sum(q * q, axis=-1, keepdims=True) + 1e-6) * (GDN_DK ** -0.5)
    kn = k * lax.rsqrt(jnp.sum(k * k, axis=-1, keepdims=True) + 1e-6)
    g_l = _by_lanes(g)
    gc = _by_batch(_accum(tril, g_l), LANES)
    glast = _by_batch(_accum(ones, g_l), LANES)
    gcol = gc[:, :, :CHUNK]
    grow = _by_batch(_accum(ones, _by_lanes(gc * _diag_lanes())), LANES)[:, :, :CHUNK]
    decay = jnp.exp(jnp.where(causal, gcol - grow, -jnp.inf))
    a = jnp.where(strict, beta[:, :, :CHUNK] * _nt(kn, kn) * decay, 0.0)
    eg = jnp.exp(gc)
    sol = _unit_lower_solve(a, jnp.concatenate([v * beta, kn * (beta * eg)], axis=2))
    u, w = sol[:, :, :GDN_DK], sol[:, :, GDN_DK:]
    qk = _nt(qn, kn) * decay
    v_new = u - _dot(w, S)
    o = _dot(qn * eg, S) + _dot(qk, v_new)
    cd = jnp.exp(glast)
    s_new = jnp.concatenate([cd, cd], axis=1) * S + _tn(kn * jnp.exp(glast - gc), v_new)
    on = o * lax.rsqrt(jnp.mean(o * o, axis=-1, keepdims=True) + RMS_EPS) * og
    return on * (gate * _sigmoid(gate)), s_new


GDN_HB = 8


def _gdn_specs(nc, rev):
    cm = (lambda c: nc - 1 - c) if rev else (lambda c: c)
    blk = lambda off: pl.BlockSpec((CHUNK, GDN_HB * GDN_DK), lambda c, h: (cm(c), off // GDN_HB + h))
    ab = pl.BlockSpec((CHUNK, LANES), lambda c, h: (cm(c), (GDN_IN_PAD - LANES) // LANES))
    hv = pl.BlockSpec((GDN_HB, 1, LANES), lambda c, h: (h, 0, 0))
    og = pl.BlockSpec((1, LANES), lambda c, h: (0, 0))
    em = pl.BlockSpec((GDN_HB, LANES, LANES), lambda c, h: (h, 0, 0))
    st = pl.BlockSpec((None, GDN_HB, GDN_DK, GDN_DK), lambda c, h: (cm(c), h, 0, 0))
    return blk, ab, hv, og, em, st


def _gdn_fwd(qc, kc, vc, p, alog_e, dtb_e, og, ea, eb):
    L = qc.shape[0]
    nc = L // CHUNK
    blk, ab, hv, ogs, em, st = _gdn_specs(nc, False)

    def body(q_ref, k_ref, v_ref, gate_ref, ab_ref, al_ref, dt_ref, og_ref, ea_ref, eb_ref, y_ref, sp_ref, s_scr):
        c, h = pl.program_id(0), pl.program_id(1)
        lanes = [slice(i * GDN_DK, (i + 1) * GDN_DK) for i in range(GDN_HB)]
        heads = pl.ds(h * GDN_HB, GDN_HB)
        stack = lambda ref: jnp.concatenate([ref[:, ls][None] for ls in lanes], axis=0)

        @pl.when(c == 0)
        def _():
            s_scr[heads] = jnp.zeros((GDN_HB, GDN_DK, GDN_DK), F32)

        S = s_scr[heads]
        sp_ref[...] = S
        y, s_new = _gdn_chunk(stack(q_ref), stack(k_ref), stack(v_ref), ab_ref[...], stack(gate_ref), S,
                              al_ref[...], dt_ref[...], og_ref[...], ea_ref[...], eb_ref[...])
        for i, ls in enumerate(lanes):
            y_ref[:, ls] = y[i]
        s_scr[heads] = s_new

    return pl.pallas_call(
        body, name="gdn_fwd", grid=(nc, GDN_HEADS // GDN_HB),
        in_specs=[blk(0), blk(0), blk(0), blk(3 * GDN_HEADS), ab, hv, hv, ogs, em, em],
        out_specs=[blk(0), st],
        out_shape=[jax.ShapeDtypeStruct((L, D_MODEL), F32), jax.ShapeDtypeStruct((nc, GDN_HEADS, GDN_DK, GDN_DK), F32)],
        scratch_shapes=[pltpu.VMEM((GDN_HEADS, GDN_DK, GDN_DK), F32)],
        compiler_params=_params("arbitrary", "arbitrary"),
    )(qc, kc, vc, p, p, alog_e, dtb_e, og, ea, eb)


def _gdn_bwd(qc, kc, vc, p, alog_e, dtb_e, og, ea, eb, sprev, dy):
    L = qc.shape[0]
    nc = L // CHUNK
    blk, ab, hv, ogs, em, st = _gdn_specs(nc, True)

    def body(q_ref, k_ref, v_ref, gate_ref, ab_ref, al_ref, dt_ref, og_ref, ea_ref, eb_ref, sp_ref, dy_ref,
             dq_ref, dk_ref, dv_ref, dgate_ref, dab_ref, dpar_ref, ds_scr):
        c, h = pl.program_id(0), pl.program_id(1)
        lanes = [slice(i * GDN_DK, (i + 1) * GDN_DK) for i in range(GDN_HB)]
        heads = pl.ds(h * GDN_HB, GDN_HB)
        stack = lambda ref: jnp.concatenate([ref[:, ls][None] for ls in lanes], axis=0)

        @pl.when(c == 0)
        def _():
            ds_scr[heads] = jnp.zeros((GDN_HB, GDN_DK, GDN_DK), F32)
            dpar_ref[heads] = jnp.zeros((GDN_HB, SUBLANES, LANES), F32)

        @pl.when(h == 0)
        def _():
            dab_ref[...] = jnp.zeros_like(dab_ref)

        ea_m, eb_m = ea_ref[...], eb_ref[...]
        f = lambda q, k, v, a_b, gate, S, al, dt, o_g: _gdn_chunk(q, k, v, a_b, gate, S, al, dt, o_g, ea_m, eb_m)
        _, vjp = jax.vjp(f, stack(q_ref), stack(k_ref), stack(v_ref), ab_ref[...], stack(gate_ref), sp_ref[...],
                         al_ref[...], dt_ref[...], og_ref[...])
        dq, dk, dv, dab, dgate, ds, dal, ddt, dog = vjp((stack(dy_ref), ds_scr[heads]))
        for i, ls in enumerate(lanes):
            dq_ref[:, ls] = dq[i]
            dk_ref[:, ls] = dk[i]
            dv_ref[:, ls] = dv[i]
            dgate_ref[:, ls] = dgate[i]
        ds_scr[heads] = ds
        dab_ref[...] += dab
        first = _iota2((GDN_HB, 1, LANES), 0) == 0
        dpar_ref[heads] += jnp.concatenate([dal, ddt, jnp.where(first, dog[None], 0.0),
                                            jnp.zeros((GDN_HB, SUBLANES - 3, LANES), F32)], axis=1)

    return pl.pallas_call(
        body, name="gdn_bwd", grid=(nc, GDN_HEADS // GDN_HB),
        in_specs=[blk(0), blk(0), blk(0), blk(3 * GDN_HEADS), ab, hv, hv, ogs, em, em, st, blk(0)],
        out_specs=[blk(0), blk(0), blk(0), blk(0), pl.BlockSpec((CHUNK, LANES), lambda c, h: (nc - 1 - c, 0)),
                   pl.BlockSpec((GDN_HEADS, SUBLANES, LANES), lambda c, h: (0, 0, 0))],
        out_shape=[jax.ShapeDtypeStruct((L, D_MODEL), F32)] * 4
        + [jax.ShapeDtypeStruct((L, LANES), F32), jax.ShapeDtypeStruct((GDN_HEADS, SUBLANES, LANES), F32)],
        scratch_shapes=[pltpu.VMEM((GDN_HEADS, GDN_DK, GDN_DK), F32)],
        compiler_params=_params("arbitrary", "arbitrary"),
    )(qc, kc, vc, p, p, alog_e, dtb_e, og, ea, eb, sprev, dy)


def _gdn_selectors():
    rows = np.arange(LANES)[None, :, None]
    heads = np.arange(GDN_HEADS)[:, None, None]
    ea = np.broadcast_to(rows == heads, (GDN_HEADS, LANES, LANES)).astype(np.float32)
    eb = np.broadcast_to(rows == heads + GDN_HEADS, (GDN_HEADS, LANES, LANES)).astype(np.float32)
    return jnp.asarray(ea), jnp.asarray(eb)


M2_GW = M2_INNER // M2_GROUPS
M2_HPG = M2_HEADS // M2_GROUPS
M2_HD = M2_INNER // M2_HEADS


def _m2_chunk(x, bm, cm, z, dtr, st, dtb, alog, dsk, ng, e, ecol):
    G = x.shape[0]
    causal, _, _, tril, ones = _chunk_consts()
    dt_n = _softplus(dtr + dtb)
    da_n = dt_n * (-jnp.exp(alog))
    cum_n = _accum(tril, da_n)
    tot_n = _accum(ones, da_n)
    wide = _pick(jnp.concatenate([dt_n, cum_n, tot_n], axis=0), e)
    dt_w, cum_w, tot_w = (_by_batch(wide[i * CHUNK:(i + 1) * CHUNK], M2_GW) for i in range(3))
    xdt = x * dt_w
    cb = _nt(cm, bm)
    heads = lambda t: jnp.concatenate([t[i:i + 1] for i in range(G) for _ in range(M2_HPG)], axis=0)
    colb = _by_batch(_pick(cum_n, ecol), LANES)
    rowb = _by_batch(_accum(ones, _by_lanes(colb * _diag_lanes())), LANES)
    lmat = jnp.exp(jnp.where(causal, colb[:, :, :CHUNK] - rowb[:, :, :CHUNK], -jnp.inf))
    yr = _dot(heads(cb) * lmat, heads(xdt))
    head = _iota2((CHUNK, M2_GW), 1) // M2_HD
    ydiag = jnp.concatenate([sum(jnp.where(head == r, yr[i * M2_HPG + r], 0.0) for r in range(M2_HPG))[None] for i in range(G)], axis=0)
    st_new = _tn(bm, xdt * jnp.exp(tot_w - cum_w))
    cd = jnp.exp(tot_w)
    s_new = jnp.concatenate([cd, cd], axis=1) * st + st_new
    y = ydiag + _dot(cm, st) * jnp.exp(cum_w) + dsk * x
    y = y * (z * _sigmoid(z))
    yn = y * lax.rsqrt(jnp.mean(y * y, axis=-1, keepdims=True) + RMS_EPS) * ng
    return yn, s_new


M2_GB = 4


def _m2_specs(nc, rev):
    cm = (lambda c: nc - 1 - c) if rev else (lambda c: c)
    wide = lambda off: pl.BlockSpec((CHUNK, M2_GB * M2_GW), lambda c, g: (cm(c), off // M2_GB + g))
    nar = lambda off: pl.BlockSpec((CHUNK, M2_GB * LANES), lambda c, g: (cm(c), off // M2_GB + g))
    dts = pl.BlockSpec((CHUNK, LANES), lambda c, g: (cm(c), (M2_IN_PAD - LANES) // LANES))
    v128 = pl.BlockSpec((1, LANES), lambda c, g: (0, 0))
    v256 = pl.BlockSpec((1, M2_GB * M2_GW), lambda c, g: (0, g))
    es = pl.BlockSpec((LANES, M2_GB * M2_GW), lambda c, g: (0, g))
    ecs = pl.BlockSpec((LANES, M2_GB * M2_HPG * LANES), lambda c, g: (0, g))
    st = pl.BlockSpec((None, M2_GB, M2_STATE, M2_GW), lambda c, g: (cm(c), g, 0, 0))
    return wide, nar, dts, v128, v256, es, ecs, st


def _m2_fwd(xbc, p, dtb, alog, dsk, ng, e, ecol):
    L = xbc.shape[0]
    nc = L // CHUNK
    wide, nar, dts, v128, v256, es, ecs, st = _m2_specs(nc, False)

    def body(x_ref, b_ref, c_ref, z_ref, dt_ref, dtb_ref, al_ref, dsk_ref, ng_ref, e_ref, ec_ref, y_ref, sp_ref, s_scr):
        c, g = pl.program_id(0), pl.program_id(1)
        wide_l = [slice(i * M2_GW, (i + 1) * M2_GW) for i in range(M2_GB)]
        nar_l = [slice(i * LANES, (i + 1) * LANES) for i in range(M2_GB)]
        groups = pl.ds(g * M2_GB, M2_GB)
        wide_s = lambda ref: jnp.concatenate([ref[:, ls][None] for ls in wide_l], axis=0)
        nar_s = lambda ref: jnp.concatenate([ref[:, ls][None] for ls in nar_l], axis=0)

        @pl.when(c == 0)
        def _():
            s_scr[groups] = jnp.zeros((M2_GB, M2_STATE, M2_GW), F32)

        S = s_scr[groups]
        sp_ref[...] = S
        y, s_new = _m2_chunk(wide_s(x_ref), nar_s(b_ref), nar_s(c_ref), wide_s(z_ref), dt_ref[...], S, dtb_ref[...], al_ref[...],
                             wide_s(dsk_ref), wide_s(ng_ref), e_ref[...], ec_ref[...])
        for i, ls in enumerate(wide_l):
            y_ref[:, ls] = y[i]
        s_scr[groups] = s_new

    return pl.pallas_call(
        body, name="m2_fwd", grid=(nc, M2_GROUPS // M2_GB),
        in_specs=[wide(0), nar(2 * M2_GROUPS), nar(3 * M2_GROUPS), wide(0), dts, v128, v128, v256, v256, es, ecs],
        out_specs=[wide(0), st],
        out_shape=[jax.ShapeDtypeStruct((L, M2_INNER), F32), jax.ShapeDtypeStruct((nc, M2_GROUPS, M2_STATE, M2_GW), F32)],
        scratch_shapes=[pltpu.VMEM((M2_GROUPS, M2_STATE, M2_GW), F32)],
        compiler_params=_params("arbitrary", "arbitrary"),
    )(xbc, xbc, xbc, p, p, dtb, alog, dsk, ng, e, ecol)


def _m2_bwd(xbc, p, dtb, alog, dsk, ng, e, ecol, sprev, dy):
    L = xbc.shape[0]
    nc = L // CHUNK
    wide, nar, dts, v128, v256, es, ecs, st = _m2_specs(nc, True)

    def body(x_ref, b_ref, c_ref, z_ref, dt_ref, dtb_ref, al_ref, dsk_ref, ng_ref, e_ref, ec_ref, sp_ref, dy_ref,
             dx_ref, db_ref, dc_ref, dz_ref, ddt_ref, dnar_ref, dwide_ref, ds_scr):
        c, g = pl.program_id(0), pl.program_id(1)
        wide_l = [slice(i * M2_GW, (i + 1) * M2_GW) for i in range(M2_GB)]
        nar_l = [slice(i * LANES, (i + 1) * LANES) for i in range(M2_GB)]
        groups = pl.ds(g * M2_GB, M2_GB)
        wide_s = lambda ref: jnp.concatenate([ref[:, ls][None] for ls in wide_l], axis=0)
        nar_s = lambda ref: jnp.concatenate([ref[:, ls][None] for ls in nar_l], axis=0)

        @pl.when(jnp.logical_and(c == 0, g == 0))
        def _():
            dnar_ref[...] = jnp.zeros_like(dnar_ref)

        @pl.when(c == 0)
        def _():
            ds_scr[groups] = jnp.zeros((M2_GB, M2_STATE, M2_GW), F32)
            dwide_ref[groups] = jnp.zeros((M2_GB, SUBLANES, M2_GW), F32)

        @pl.when(g == 0)
        def _():
            ddt_ref[...] = jnp.zeros_like(ddt_ref)

        e_m, ec_m = e_ref[...], ec_ref[...]
        f = lambda x, bm, cm, z, dtr, S, dtb, al, dsk, ng: _m2_chunk(x, bm, cm, z, dtr, S, dtb, al, dsk, ng, e_m, ec_m)
        _, vjp = jax.vjp(f, wide_s(x_ref), nar_s(b_ref), nar_s(c_ref), wide_s(z_ref), dt_ref[...], sp_ref[...], dtb_ref[...],
                         al_ref[...], wide_s(dsk_ref), wide_s(ng_ref))
        dx, db, dc, dz, ddt, ds, ddtb, dal, ddsk, dng = vjp((wide_s(dy_ref), ds_scr[groups]))
        for i in range(M2_GB):
            dx_ref[:, wide_l[i]] = dx[i]
            db_ref[:, nar_l[i]] = db[i]
            dc_ref[:, nar_l[i]] = dc[i]
            dz_ref[:, wide_l[i]] = dz[i]
        ds_scr[groups] = ds
        ddt_ref[...] += ddt
        dnar_ref[...] += jnp.concatenate([ddtb, dal, jnp.zeros((SUBLANES - 2, LANES), F32)], axis=0)
        dwide_ref[groups] += jnp.concatenate([ddsk, dng, jnp.zeros((M2_GB, SUBLANES - 2, M2_GW), F32)], axis=1)

    return pl.pallas_call(
        body, name="m2_bwd", grid=(nc, M2_GROUPS // M2_GB),
        in_specs=[wide(0), nar(2 * M2_GROUPS), nar(3 * M2_GROUPS), wide(0), dts, v128, v128, v256, v256, es, ecs, st, wide(0)],
        out_specs=[wide(0), nar(0), nar(0), wide(0), pl.BlockSpec((CHUNK, LANES), lambda c, g: (nc - 1 - c, 0)),
                   pl.BlockSpec((SUBLANES, LANES), lambda c, g: (0, 0)),
                   pl.BlockSpec((M2_GROUPS, SUBLANES, M2_GW), lambda c, g: (0, 0, 0))],
        out_shape=[jax.ShapeDtypeStruct((L, M2_INNER), F32), jax.ShapeDtypeStruct((L, M2_GROUPS * M2_STATE), F32),
                   jax.ShapeDtypeStruct((L, M2_GROUPS * M2_STATE), F32), jax.ShapeDtypeStruct((L, M2_INNER), F32),
                   jax.ShapeDtypeStruct((L, LANES), F32), jax.ShapeDtypeStruct((SUBLANES, LANES), F32),
                   jax.ShapeDtypeStruct((M2_GROUPS, SUBLANES, M2_GW), F32)],
        scratch_shapes=[pltpu.VMEM((M2_GROUPS, M2_STATE, M2_GW), F32)],
        compiler_params=_params("arbitrary", "arbitrary"),
    )(xbc, xbc, xbc, p, p, dtb, alog, dsk, ng, e, ecol, sprev, dy)


def _m2_selectors():
    e = np.zeros((LANES, M2_INNER), np.float32)
    ecol = np.zeros((LANES, M2_HEADS * LANES), np.float32)
    for h in range(M2_HEADS):
        e[h, M2_HD * h:M2_HD * (h + 1)] = 1.0
        ecol[h, LANES * h:LANES * (h + 1)] = 1.0
    return jnp.asarray(e), jnp.asarray(ecol)


S5_NS = S5_GROUPS * S5_STATE // S5_BLOCKS
S5_ROWS = 256
GELU_C = math.sqrt(2.0 / math.pi)


def _gelu(x):
    return 0.5 * x * (1.0 + jnp.tanh(GELU_C * (x + 0.044715 * x * x * x)))


def _gelu_grad(x):
    t = jnp.tanh(GELU_C * (x + 0.044715 * x * x * x))
    return 0.5 * (1.0 + t) + 0.5 * x * (1.0 - t * t) * GELU_C * (1.0 + 3.0 * 0.044715 * x * x)


def _s5_scan(re_ref, im_ref, pw_re, pw_im, nrows, reverse):
    n = re_ref.shape[1]
    row = _iota2((SUBLANES, n), 0)
    steps = [(d, pw_re[d - 1:d, :], pw_im[d - 1:d, :]) for d in (1, 2, 4)]
    if reverse:
        cw_re = jnp.concatenate([pw_re[SUBLANES - 1 - k:SUBLANES - k, :] for k in range(SUBLANES)], axis=0)
        cw_im = jnp.concatenate([pw_im[SUBLANES - 1 - k:SUBLANES - k, :] for k in range(SUBLANES)], axis=0)
    else:
        cw_re, cw_im = pw_re, pw_im
    edge = 0 if reverse else SUBLANES - 1
    ngroups = nrows // SUBLANES

    def step(i, carry):
        cr, ci = carry
        gi = (ngroups - 1 - i) if reverse else i
        r0 = pl.multiple_of(gi * SUBLANES, SUBLANES)
        xr, xi = re_ref[pl.ds(r0, SUBLANES), :], im_ref[pl.ds(r0, SUBLANES), :]
        for d, pr, pi in steps:
            if reverse:
                sr = jnp.where(row < SUBLANES - d, pltpu.roll(xr, SUBLANES - d, axis=0), 0.0)
                si = jnp.where(row < SUBLANES - d, pltpu.roll(xi, SUBLANES - d, axis=0), 0.0)
            else:
                sr = jnp.where(row >= d, pltpu.roll(xr, d, axis=0), 0.0)
                si = jnp.where(row >= d, pltpu.roll(xi, d, axis=0), 0.0)
            xr, xi = xr + (pr * sr - pi * si), xi + (pr * si + pi * sr)
        xr, xi = xr + (cw_re * cr - cw_im * ci), xi + (cw_re * ci + cw_im * cr)
        re_ref[pl.ds(r0, SUBLANES), :] = xr
        im_ref[pl.ds(r0, SUBLANES), :] = xi
        return (jnp.sum(jnp.where(row == edge, xr, 0.0), axis=0, keepdims=True),
                jnp.sum(jnp.where(row == edge, xi, 0.0), axis=0, keepdims=True))

    z = jnp.zeros((1, n), F32)
    lax.fori_loop(0, ngroups, step, (z, z))


def _s5_project_in(u_ref, bm_ref, re_ref, im_ref, L):
    def step(i, carry):
        r0 = pl.multiple_of(i * S5_ROWS, S5_ROWS)
        bu = _dot(u_ref[pl.ds(r0, S5_ROWS), :], bm_ref[...])
        re_ref[pl.ds(r0, S5_ROWS), :] = bu[:, :S5_NS]
        im_ref[pl.ds(r0, S5_ROWS), :] = bu[:, S5_NS:]
        return carry

    lax.fori_loop(0, L // S5_ROWS, step, 0)


def _s5_specs(L):
    col = pl.BlockSpec((L, LANES), lambda j: (0, j))
    bm = pl.BlockSpec((None, LANES, 2 * S5_NS), lambda j: (j, 0, 0))
    cm = pl.BlockSpec((None, 2 * S5_NS, LANES), lambda j: (j, 0, 0))
    pw = pl.BlockSpec((None, SUBLANES, S5_NS), lambda j: (j, 0, 0))
    vec = pl.BlockSpec((1, LANES), lambda j: (0, j))
    return col, bm, cm, pw, vec


def _s5_fwd(u, bmat, cmat, pw_re, pw_im, dsk):
    L = u.shape[0]
    col, bm, cm, pw, vec = _s5_specs(L)

    def body(u_ref, bm_ref, cm_ref, pr_ref, pi_ref, d_ref, y_ref, re_scr, im_scr):
        _s5_project_in(u_ref, bm_ref, re_scr, im_scr, L)
        _s5_scan(re_scr, im_scr, pr_ref[...], pi_ref[...], L, False)

        def step(i, carry):
            r0 = pl.multiple_of(i * S5_ROWS, S5_ROWS)
            rows = pl.ds(r0, S5_ROWS)
            y = _dot(re_scr[rows, :], cm_ref[:S5_NS, :]) + _dot(im_scr[rows, :], cm_ref[S5_NS:, :]) + d_ref[...] * u_ref[rows, :]
            y_ref[rows, :] = _gelu(y)
            return carry

        lax.fori_loop(0, L // S5_ROWS, step, 0)

    return pl.pallas_call(
        body, name="s5_fwd", grid=(S5_BLOCKS,),
        in_specs=[col, bm, cm, pw, pw, vec], out_specs=col,
        out_shape=jax.ShapeDtypeStruct((L, D_MODEL), F32),
        scratch_shapes=[pltpu.VMEM((L, S5_NS), F32)] * 2,
        compiler_params=_params("parallel"),
    )(u, bmat, cmat, pw_re, pw_im, dsk)


def _s5_bwd(u, bmat, cmat, pw_re, pw_im, dsk, dyg):
    L = u.shape[0]
    col, bm, cm, pw, vec = _s5_specs(L)

    def body(u_ref, bm_ref, cm_ref, pr_ref, pi_ref, d_ref, dy_ref, du_ref, dbm_ref, dcm_ref, dlam_ref, dd_ref,
             re_scr, im_scr, gr_scr, gi_scr, dyp_scr):
        _s5_project_in(u_ref, bm_ref, re_scr, im_scr, L)
        _s5_scan(re_scr, im_scr, pr_ref[...], pi_ref[...], L, False)

        def step(i, carry):
            dcr, dci, dd = carry
            r0 = pl.multiple_of(i * S5_ROWS, S5_ROWS)
            rows = pl.ds(r0, S5_ROWS)
            sr, si, uu = re_scr[rows, :], im_scr[rows, :], u_ref[rows, :]
            y = _dot(sr, cm_ref[:S5_NS, :]) + _dot(si, cm_ref[S5_NS:, :]) + d_ref[...] * uu
            dyp = dy_ref[rows, :] * _gelu_grad(y)
            dyp_scr[rows, :] = dyp
            gr_scr[rows, :] = _nt(dyp, cm_ref[:S5_NS, :])
            gi_scr[rows, :] = _nt(dyp, cm_ref[S5_NS:, :])
            return dcr + _tn(sr, dyp), dci + _tn(si, dyp), dd + jnp.sum(dyp * uu, axis=0, keepdims=True)

        zc = jnp.zeros((S5_NS, LANES), F32)
        dcr, dci, dd = lax.fori_loop(0, L // S5_ROWS, step, (zc, zc, jnp.zeros((1, LANES), F32)))
        dcm_ref[:S5_NS, :] = dcr
        dcm_ref[S5_NS:, :] = dci
        dd_ref[...] = dd

        _s5_scan(gr_scr, gi_scr, pr_ref[...], -pi_ref[...], L, True)

        row = _iota2((SUBLANES, S5_NS), 0)

        def lam_step(i, carry):
            ar, ai, pr, pi = carry
            r0 = pl.multiple_of(i * SUBLANES, SUBLANES)
            rows = pl.ds(r0, SUBLANES)
            sr, si = re_scr[rows, :], im_scr[rows, :]
            spr = jnp.where(row >= 1, pltpu.roll(sr, 1, axis=0), pr)
            spi = jnp.where(row >= 1, pltpu.roll(si, 1, axis=0), pi)
            gr, gi = gr_scr[rows, :], gi_scr[rows, :]
            ar = ar + jnp.sum(spr * gr + spi * gi, axis=0, keepdims=True)
            ai = ai + jnp.sum(spr * gi - spi * gr, axis=0, keepdims=True)
            last = row == SUBLANES - 1
            return (ar, ai, jnp.sum(jnp.where(last, sr, 0.0), axis=0, keepdims=True),
                    jnp.sum(jnp.where(last, si, 0.0), axis=0, keepdims=True))

        z = jnp.zeros((1, S5_NS), F32)
        ar, ai, _, _ = lax.fori_loop(0, L // SUBLANES, lam_step, (z, z, z, z))
        dlam_ref[...] = jnp.concatenate([ar, ai, jnp.zeros((SUBLANES - 2, S5_NS), F32)], axis=0)

        def in_step(i, carry):
            dbr, dbi = carry
            r0 = pl.multiple_of(i * S5_ROWS, S5_ROWS)
            rows = pl.ds(r0, S5_ROWS)
            gr, gi, uu = gr_scr[rows, :], gi_scr[rows, :], u_ref[rows, :]
            du_ref[rows, :] = dyp_scr[rows, :] * d_ref[...] + _nt(gr, bm_ref[:, :S5_NS]) + _nt(gi, bm_ref[:, S5_NS:])
            return dbr + _tn(uu, gr), dbi + _tn(uu, gi)

        zb = jnp.zeros((LANES, S5_NS), F32)
        dbr, dbi = lax.fori_loop(0, L // S5_ROWS, in_step, (zb, zb))
        dbm_ref[:, :S5_NS] = dbr
        dbm_ref[:, S5_NS:] = dbi

    return pl.pallas_call(
        body, name="s5_bwd", grid=(S5_BLOCKS,),
        in_specs=[col, bm, cm, pw, pw, vec, col], out_specs=[col, bm, cm, pw, vec],
        out_shape=[jax.ShapeDtypeStruct((L, D_MODEL), F32), jax.ShapeDtypeStruct((S5_BLOCKS, LANES, 2 * S5_NS), F32),
                   jax.ShapeDtypeStruct((S5_BLOCKS, 2 * S5_NS, LANES), F32),
                   jax.ShapeDtypeStruct((S5_BLOCKS, SUBLANES, S5_NS), F32), jax.ShapeDtypeStruct((1, D_MODEL), F32)],
        scratch_shapes=[pltpu.VMEM((L, S5_NS), F32)] * 4 + [pltpu.VMEM((L, LANES), F32)],
        compiler_params=_params("parallel"),
    )(u, bmat, cmat, pw_re, pw_im, dsk, dyg)


def _s5_discretize(lam_re, lam_im, log_dt, b_re, b_im, e16):
    dt = jnp.exp(log_dt)
    zr, zi = lam_re * dt, lam_im * dt
    mag = jnp.exp(zr)
    lbr, lbi = mag * jnp.cos(zi), mag * jnp.sin(zi)
    den = lam_re * lam_re + lam_im * lam_im
    nr, ni = lbr - 1.0, lbi
    cr = (nr * lam_re + ni * lam_im) / den
    ci = (ni * lam_re - nr * lam_im) / den
    crw, ciw = _pick(cr, e16), _pick(ci, e16)
    return lbr, lbi, crw * b_re - ciw * b_im, crw * b_im + ciw * b_re


def _s5_params_fwd(lam_re, lam_im, log_dt, b_re, b_im, e16):
    def body(lr, li, ld, br, bi, e, o1, o2, o3, o4):
        for o, val in zip((o1, o2, o3, o4), _s5_discretize(lr[...], li[...], ld[...], br[...], bi[...], e[...])):
            o[...] = val

    g, p, n = S5_GROUPS, S5_STATE, S5_STATE * S5_GROUP
    return pl.pallas_call(
        body, name="s5_params_fwd",
        out_shape=[jax.ShapeDtypeStruct((g, p), F32)] * 2 + [jax.ShapeDtypeStruct((g, n), F32)] * 2,
        compiler_params=_params(),
    )(lam_re, lam_im, log_dt, b_re, b_im, e16)


def _s5_params_bwd(lam_re, lam_im, log_dt, b_re, b_im, e16, cts):
    def body(lr, li, ld, br, bi, e, c1, c2, c3, c4, o1, o2, o3, o4, o5):
        e_m = e[...]
        f = lambda a, b, c, d, g: _s5_discretize(a, b, c, d, g, e_m)
        _, vjp = jax.vjp(f, lr[...], li[...], ld[...], br[...], bi[...])
        for o, val in zip((o1, o2, o3, o4, o5), vjp((c1[...], c2[...], c3[...], c4[...]))):
            o[...] = val

    g, p, n = S5_GROUPS, S5_STATE, S5_STATE * S5_GROUP
    return pl.pallas_call(
        body, name="s5_params_bwd",
        out_shape=[jax.ShapeDtypeStruct((g, p), F32)] * 2 + [jax.ShapeDtypeStruct((g, 1), F32)]
        + [jax.ShapeDtypeStruct((g, n), F32)] * 2,
        compiler_params=_params(),
    )(lam_re, lam_im, log_dt, b_re, b_im, e16, *cts)


def _add_residual(acc, h):
    return (acc + h,)


def _mlp_fwd(i, h, g, w1, w2):
    hn = _rms_fwd(f"mlp{i}_norm", h, g)
    r = _mm(f"mlp{i}_up", hn, w1, "nn", (BF16,), epi=lambda acc: (jnp.square(jnp.maximum(acc, 0.0)),))
    return _mm(f"mlp{i}_down", r, w2, "nn", (F32,), epi=_add_residual, extras=(h,)), (h, hn, r)


def _mlp_bwd(i, dh_out, saved, g, w1, w2):
    h, hn, r = saved
    dw2 = _mm(f"mlp{i}_dw2", r, dh_out, "tn", (BF16,))
    da = _mm(f"mlp{i}_da", dh_out, w2, "nt", (BF16,), epi=lambda acc, rr: (acc * (2.0 * jnp.sqrt(rr.astype(F32))),), extras=(r,))
    dw1 = _mm(f"mlp{i}_dw1", hn, da, "tn", (BF16,))
    dhn = _mm(f"mlp{i}_dhn", da, w1, "nt", (F32,))
    dh, dg = _rms_bwd(f"mlp{i}_dnorm", h, g, dhn, dh_out)
    return dh, dg[0], dw1, dw2


def _lanes(v, n):
    return jnp.broadcast_to(v.reshape(n, 1, 1), (n, 1, LANES))


def _gdn_fwd_layer(i, h, g, w_in, conv_w, a_log, dt_bias, o_g, w_out):
    hn = _rms_fwd(f"gdn{i}_norm", h, g)
    p = _mm(f"gdn{i}_in", hn, w_in, "nn", (F32,))
    zb = jnp.zeros((1, D_MODEL), F32)
    qkv = [_conv_fwd(f"gdn{i}_conv{t}", p, t * D_MODEL, conv_w[:, t * D_MODEL:(t + 1) * D_MODEL], zb) for t in range(3)]
    ea, eb = _gdn_selectors()
    y, sprev = _gdn_fwd(*qkv, p, _lanes(a_log, GDN_HEADS), _lanes(dt_bias, GDN_HEADS), o_g.reshape(1, LANES), ea, eb)
    return _mm(f"gdn{i}_out", y, w_out, "nn", (F32,), epi=_add_residual, extras=(h,)), (h, hn, p, qkv, y, sprev)


def _gdn_bwd_layer(i, dh_out, saved, g, w_in, conv_w, a_log, dt_bias, o_g, w_out):
    h, hn, p, qkv, y, sprev = saved
    dy = _mm(f"gdn{i}_dy", dh_out, w_out, "nt", (F32,))
    dw_out = _mm(f"gdn{i}_dwout", y, dh_out, "tn", (BF16,))
    ea, eb = _gdn_selectors()
    dq, dk, dv, dgate, dab, dpar = _gdn_bwd(*qkv, p, _lanes(a_log, GDN_HEADS), _lanes(dt_bias, GDN_HEADS),
                                            o_g.reshape(1, LANES), ea, eb, sprev, dy)
    zb = jnp.zeros((1, D_MODEL), F32)
    dpre, dcw = [], []
    for t, d in enumerate((dq, dk, dv)):
        dx, dw, _ = _conv_bwd(f"gdn{i}_dconv{t}", p, t * D_MODEL, conv_w[:, t * D_MODEL:(t + 1) * D_MODEL], zb, d)
        dpre.append(dx)
        dcw.append(dw)
    dp = jnp.concatenate(dpre + [dgate, dab], axis=1)
    dw_in = _mm(f"gdn{i}_dwin", hn, dp, "tn", (BF16,))[:, :GDN_IN]
    dhn = _mm(f"gdn{i}_dhn", dp, w_in, "nt", (F32,))
    dh, dg = _rms_bwd(f"gdn{i}_dnorm", h, g, dhn, dh_out)
    grads = dict(w_in=dw_in, conv_w=jnp.concatenate(dcw, axis=1), a_log=jnp.sum(dpar[:, 0, :], axis=-1),
                 dt_bias=jnp.sum(dpar[:, 1, :], axis=-1), o_norm_g=jnp.sum(dpar[:, 2, :], axis=0), w_out=dw_out)
    return dh, dg[0], grads


def _m2_vectors(dt_bias, a_log, d_skip, norm_g):
    pad = lambda v: jnp.pad(v, (0, LANES - M2_HEADS)).reshape(1, LANES)
    return pad(dt_bias), pad(a_log), jnp.repeat(d_skip, M2_HD).reshape(1, M2_INNER), norm_g.reshape(1, M2_INNER)


def _m2_fwd_layer(h, g, w_in, conv_w, conv_b, dt_bias, a_log, d_skip, norm_g, w_out):
    hn = _rms_fwd("m2_norm", h, g)
    p = _mm("m2_in", hn, w_in, "nn", (F32,))
    xbc = _conv_fwd("m2_conv", p, M2_INNER, conv_w, conv_b.reshape(1, M2_CONV_CH))
    e, ecol = _m2_selectors()
    y, sprev = _m2_fwd(xbc, p, *_m2_vectors(dt_bias, a_log, d_skip, norm_g), e, ecol)
    return _mm("m2_out", y, w_out, "nn", (F32,), epi=_add_residual, extras=(h,)), (h, hn, p, xbc, y, sprev)


def _m2_bwd_layer(dh_out, saved, g, w_in, conv_w, conv_b, dt_bias, a_log, d_skip, norm_g, w_out):
    h, hn, p, xbc, y, sprev = saved
    dy = _mm("m2_dy", dh_out, w_out, "nt", (F32,))
    dw_out = _mm("m2_dwout", y, dh_out, "tn", (BF16,))
    e, ecol = _m2_selectors()
    dx, db, dc, dz, ddt, dnar, dwide = _m2_bwd(xbc, p, *_m2_vectors(dt_bias, a_log, d_skip, norm_g), e, ecol, sprev, dy)
    dxbc, dcw, dcb = _conv_bwd("m2_dconv", p, M2_INNER, conv_w, conv_b.reshape(1, M2_CONV_CH),
                               jnp.concatenate([dx, db, dc], axis=1))
    dp = jnp.concatenate([dz, dxbc, ddt], axis=1)
    dw_in = _mm("m2_dwin", hn, dp, "tn", (BF16,))[:, :M2_IN]
    dhn = _mm("m2_dhn", dp, w_in, "nt", (F32,))
    dh, dg = _rms_bwd("m2_dnorm", h, g, dhn, dh_out)
    grads = dict(w_in=dw_in, conv_w=dcw, conv_b=dcb[0], dt_bias=dnar[0, :M2_HEADS], a_log=dnar[1, :M2_HEADS],
                 d=jnp.sum(dwide[:, 0, :].reshape(M2_HEADS, M2_HD), axis=-1), norm_g=dwide[:, 1, :].reshape(M2_INNER),
                 w_out=dw_out)
    return dh, dg[0], grads


def _s5_selector():
    e16 = np.zeros((S5_STATE, S5_STATE * S5_GROUP), np.float32)
    for p in range(S5_STATE):
        e16[p, p * S5_GROUP:(p + 1) * S5_GROUP] = 1.0
    return jnp.asarray(e16)


def _s5_operands(lbr, lbi, bbr, bbi, c_re, c_im):
    eye = jnp.eye(S5_BLOCKS, dtype=F32)
    gpb = S5_GROUPS // S5_BLOCKS
    bd = lambda t: jnp.einsum("jgpk,gh->jgkhp", t.reshape(S5_BLOCKS, gpb, S5_STATE, S5_GROUP), eye).reshape(S5_BLOCKS, LANES, S5_NS)
    cd = lambda t: jnp.einsum("jgkp,gh->jgphk", t.reshape(S5_BLOCKS, gpb, S5_GROUP, S5_STATE), eye).reshape(S5_BLOCKS, S5_NS, LANES)
    bmat = jnp.concatenate([bd(bbr), bd(bbi)], axis=2).astype(BF16)
    cmat = jnp.concatenate([cd(c_re), -cd(c_im)], axis=1).astype(BF16)
    ar, ai = lbr.reshape(S5_BLOCKS, S5_NS), lbi.reshape(S5_BLOCKS, S5_NS)
    pr, pi = [ar], [ai]
    for _ in range(SUBLANES - 1):
        pr, pi = pr + [pr[-1] * ar - pi[-1] * ai], pi + [pr[-1] * ai + pi[-1] * ar]
    return bmat, cmat, jnp.stack(pr, axis=1), jnp.stack(pi, axis=1)


def _s5_fwd_layer(h, g, w_in, lam_re, lam_im, log_dt, b_re, b_im, c_re, c_im, d_skip, w_out):
    hn = _rms_fwd("s5_norm", h, g)
    u = _mm("s5_in", hn, w_in, "nn", (F32,))
    n = S5_STATE * S5_GROUP
    lbr, lbi, bbr, bbi = _s5_params_fwd(lam_re, lam_im, log_dt.reshape(S5_GROUPS, 1), b_re.reshape(S5_GROUPS, n),
                                        b_im.reshape(S5_GROUPS, n), _s5_selector())
    ops = _s5_operands(lbr, lbi, bbr, bbi, c_re, c_im)
    yg = _s5_fwd(u, *ops, d_skip.reshape(1, D_MODEL))
    ag = _mm("s5_out", yg, w_out, "nn", (F32,))
    return _glu_fwd(h, ag), (h, hn, u, ops, yg, ag)


def _s5_bwd_layer(dh_out, saved, g, w_in, lam_re, lam_im, log_dt, b_re, b_im, c_re, c_im, d_skip, w_out):
    h, hn, u, ops, yg, ag = saved
    dag = _glu_bwd(dh_out, ag)
    dw_out = _mm("s5_dwout", yg, dag, "tn", (BF16,))
    dyg = _mm("s5_dyg", dag, w_out, "nt", (F32,))
    du, dbmat, dcmat, dlam, ddsk = _s5_bwd(u, *ops, d_skip.reshape(1, D_MODEL), dyg)
    eye = jnp.eye(S5_BLOCKS, dtype=F32)
    gpb = S5_GROUPS // S5_BLOCKS
    n = S5_STATE * S5_GROUP
    ub = lambda t: jnp.einsum("jgkhp,gh->jgpk", t.reshape(S5_BLOCKS, gpb, S5_GROUP, gpb, S5_STATE), eye).reshape(S5_GROUPS, n)
    uc = lambda t: jnp.einsum("jgphk,gh->jgkp", t.reshape(S5_BLOCKS, gpb, S5_STATE, gpb, S5_GROUP), eye).reshape(c_re.shape)
    cts = (dlam[:, 0, :].reshape(S5_GROUPS, S5_STATE), dlam[:, 1, :].reshape(S5_GROUPS, S5_STATE),
           ub(dbmat[:, :, :S5_NS]), ub(dbmat[:, :, S5_NS:]))
    dlr, dli, dld, dbr, dbi = _s5_params_bwd(lam_re, lam_im, log_dt.reshape(S5_GROUPS, 1), b_re.reshape(S5_GROUPS, n),
                                             b_im.reshape(S5_GROUPS, n), _s5_selector(), cts)
    dw_in = _mm("s5_dwin", hn, du, "tn", (BF16,))
    dhn = _mm("s5_dhn", du, w_in, "nt", (F32,))
    dh, dg = _rms_bwd("s5_dnorm", h, g, dhn, dh_out)
    grads = dict(w_in=dw_in, lam_re=dlr, lam_im=dli, log_dt=dld[:, 0], b_re=dbr.reshape(b_re.shape), b_im=dbi.reshape(b_im.shape),
                 c_re=uc(dcmat[:, :S5_NS, :]), c_im=-uc(dcmat[:, S5_NS:, :]), d=ddsk[0], w_out=dw_out)
    return dh, dg[0], grads


MIXER_OF_LAYER = ("gdn", "s5", "m2", "gdn")
MIXER_INDEX = (0, 0, 0, 1)


def _mixer_args(W, i):
    kind, j = MIXER_OF_LAYER[i], MIXER_INDEX[i]
    if kind == "gdn":
        return tuple(W["gdn_" + k][j] for k in ("w_in", "conv_w", "a_log", "dt_bias", "o_norm_g", "w_out"))
    if kind == "s5":
        return tuple(W["s5_" + k][j] for k in ("w_in", "lam_re", "lam_im", "log_dt", "b_re", "b_im", "c_re", "c_im", "d", "w_out"))
    return tuple(W["m2_" + k][j] for k in ("w_in", "conv_w", "conv_b", "dt_bias", "a_log", "d", "norm_g", "w_out"))


def _local_step(x, target, W, on_layer_grads):
    h = x
    saved = []
    for i in range(DEPTH):
        kind = MIXER_OF_LAYER[i]
        args = _mixer_args(W, i)
        if kind == "gdn":
            h, sm = _gdn_fwd_layer(i, h, W["norm_mix_g"][i], *args)
        elif kind == "s5":
            h, sm = _s5_fwd_layer(h, W["norm_mix_g"][i], *args)
        else:
            h, sm = _m2_fwd_layer(h, W["norm_mix_g"][i], *args)
        h, sp = _mlp_fwd(i, h, W["norm_mlp_g"][i], W["mlp_w1"][i], W["mlp_w2"][i])
        saved.append((sm, sp))
    loss, dh, dgf = _loss_head(h, W["final_norm_g"], target)
    G = {"final_norm_g": dgf[0], "norm_mix_g": [None] * DEPTH, "norm_mlp_g": [None] * DEPTH,
         "mlp_w1": [None] * DEPTH, "mlp_w2": [None] * DEPTH}
    mix = {}
    for i in reversed(range(DEPTH)):
        kind = MIXER_OF_LAYER[i]
        sm, sp = saved[i]
        dh, G["norm_mlp_g"][i], G["mlp_w1"][i], G["mlp_w2"][i] = _mlp_bwd(i, dh, sp, W["norm_mlp_g"][i], W["mlp_w1"][i], W["mlp_w2"][i])
        args = _mixer_args(W, i)
        if kind == "gdn":
            dh, G["norm_mix_g"][i], gm = _gdn_bwd_layer(i, dh, sm, W["norm_mix_g"][i], *args)
        elif kind == "s5":
            dh, G["norm_mix_g"][i], gm = _s5_bwd_layer(dh, sm, W["norm_mix_g"][i], *args)
        else:
            dh, G["norm_mix_g"][i], gm = _m2_bwd_layer(dh, sm, W["norm_mix_g"][i], *args)
        j = MIXER_INDEX[i]
        on_layer_grads(i, {("mlp_w1", i): G["mlp_w1"][i], ("mlp_w2", i): G["mlp_w2"][i],
                           (kind + "_w_in", j): gm["w_in"], (kind + "_w_out", j): gm["w_out"]})
        for k, v in gm.items():
            mix.setdefault(kind + "_" + k, {})[j] = v
    for k, d in mix.items():
        G[k] = [d[j] for j in sorted(d)]
    return loss, dh, {k: jnp.stack(v) if isinstance(v, list) else v for k, v in G.items() if k not in BIG}


ADAM_ROWS = 128


def _adamw(name, w, g, m, v):
    R, C = w.shape
    tr = _tile(R, (ADAM_ROWS, SUBLANES))

    def body(w_ref, g_ref, m_ref, v_ref, d_ref, mo_ref, vo_ref):
        gg = g_ref[...]
        mn = ADAM_B1 * m_ref[...] + (1.0 - ADAM_B1) * gg
        vn = ADAM_B2 * v_ref[...] + (1.0 - ADAM_B2) * (gg * gg)
        m_hat = mn / (1.0 - ADAM_B1 ** ADAM_STEP)
        v_hat = vn / (1.0 - ADAM_B2 ** ADAM_STEP)
        d_ref[...] = -ADAM_LR * (m_hat / (jnp.sqrt(v_hat) + ADAM_EPS) + ADAM_WD * w_ref[...])
        mo_ref[...] = mn
        vo_ref[...] = vn

    blk = pl.BlockSpec((tr, C), lambda i: (i, 0))
    return pl.pallas_call(
        body, name=name, grid=(R // tr,), in_specs=[blk] * 4, out_specs=[blk] * 3,
        out_shape=[jax.ShapeDtypeStruct((R, C), F32)] * 3, compiler_params=_params("parallel"),
    )(w, g, m, v)


MESH = pl.DeviceIdType.MESH
ANY = pl.BlockSpec(memory_space=pl.ANY)
N_CHIPS = 4
N_DEV = 8


def _position():
    return lax.axis_index("x"), lax.axis_index("y"), lax.axis_index("c")


GATHER_IDS = {1: 1, 2: 2, 3: 3}
EXCHANGE_IDS = {0: 4, 1: 5, 2: 6, 3: 7}


def _gather_body(w_refs, out_refs, send_sems, recv_sems):
    x, y, c = _position()
    sibling = (x, y, 1 - c)
    chips = [(1 - x, y), (x, 1 - y), (1 - x, 1 - y)]
    firsts, passes = [], []
    for t, (w_ref, out_ref) in enumerate(zip(w_refs, out_refs)):
        half = w_ref.shape[0] // 2

        def piece(cx, cy, hc, out_ref=out_ref, half=half):
            return out_ref.at[2 * cx + cy, pl.ds(hc * half, half), :]

        def copy(k, src, dst, to, t=t):
            return pltpu.make_async_remote_copy(src_ref=src, dst_ref=dst, send_sem=send_sems.at[6 * t + k],
                                                recv_sem=recv_sems.at[6 * t + k], device_id=to, device_id_type=MESH)

        first = [copy(j, w_ref.at[pl.ds(c * half, half), :], piece(x, y, c), (*chip, c)) for j, chip in enumerate(chips)]
        for cp in first:
            cp.start()
        firsts.append((first, piece, copy))
    for first, piece, copy in firsts:
        passed = [copy(3 + j, piece(*chip, c), piece(*chip, c), sibling) for j, chip in enumerate(chips)]
        for j, chip in enumerate(chips):
            copy(j, piece(*chip, c), piece(*chip, c), sibling).wait_recv()
            passed[j].start()
        passes.append(passed)
    for (first, piece, copy), passed in zip(firsts, passes):
        for j, chip in enumerate(chips):
            copy(3 + j, piece(*chip, 1 - c), piece(*chip, 1 - c), sibling).wait_recv()
        for cp in first + passed:
            cp.wait_send()


def _gather_shards(wps):
    n = len(wps)

    def body(*refs):
        _gather_body(refs[:n], refs[n:2 * n], *refs[2 * n:])

    return pl.pallas_call(
        body, name="gather_shards", in_specs=[ANY] * n, out_specs=[ANY] * n,
        out_shape=[jax.ShapeDtypeStruct((N_CHIPS, *wp.shape), wp.dtype) for wp in wps],
        scratch_shapes=[pltpu.SemaphoreType.DMA((6 * n,)), pltpu.SemaphoreType.DMA((6 * n,))],
    )(*wps)


def _gather_shards_later(wps, layer):
    n = len(wps)
    w_refs = [jax.new_ref(wp, memory_space=pltpu.MemorySpace.HBM) for wp in wps]
    out_refs = [jax.empty_ref(jax.ShapeDtypeStruct((N_CHIPS, *wp.shape), wp.dtype), memory_space=pltpu.MemorySpace.HBM)
                for wp in wps]

    @pl.kernel(mesh=plsc.ScalarSubcoreMesh(axis_name="sequencer", num_cores=1), name=f"gather_shards_later{layer}",
               scratch_types=(pltpu.SemaphoreType.DMA((6 * n,)), pltpu.SemaphoreType.DMA((6 * n,))),
               compiler_params=pltpu.CompilerParams(collective_id=GATHER_IDS[layer]))
    def launch(send_sems, recv_sems):
        x, y, c = _position()
        barrier = pltpu.get_barrier_semaphore()
        for peer in [(x, y, 1 - c), (1 - x, y, c), (x, 1 - y, c), (1 - x, 1 - y, c)]:
            pl.semaphore_signal(barrier, inc=1, device_id=peer, device_id_type=MESH)
        pl.semaphore_wait(barrier, 4)
        _gather_body(w_refs, out_refs, send_sems, recv_sems)

    launch()
    return [r[...] for r in out_refs]


def _pair_exchange(name, gps):
    n = len(gps)

    def body(*refs):
        g_refs, out_refs, (send_sems, recv_sems) = refs[:n], refs[n:2 * n], refs[2 * n:]
        x, y, c = _position()
        copies = []
        for t, (g_ref, out_ref) in enumerate(zip(g_refs, out_refs)):
            half = g_ref.shape[1] // 2
            copies += [pltpu.make_async_remote_copy(
                src_ref=g_ref.at[k, pl.ds((1 - c) * half, half), :], dst_ref=out_ref.at[k], send_sem=send_sems.at[N_CHIPS * t + k],
                recv_sem=recv_sems.at[N_CHIPS * t + k], device_id=(x, y, 1 - c), device_id_type=MESH) for k in range(N_CHIPS)]
        for cp in copies:
            cp.start()
        for cp in copies:
            cp.wait()

    return pl.pallas_call(
        body, name=name, in_specs=[ANY] * n, out_specs=[ANY] * n,
        out_shape=[jax.ShapeDtypeStruct((N_CHIPS, gp.shape[1] // 2, gp.shape[2]), gp.dtype) for gp in gps],
        scratch_shapes=[pltpu.SemaphoreType.DMA((N_CHIPS * n,)), pltpu.SemaphoreType.DMA((N_CHIPS * n,))],
    )(*gps)


SUM_ROWS = (256, 128)


def _pair_sum(name, gp, got, core):
    n, R, C = gp.shape
    half = R // 2
    tr = _tile(half, SUM_ROWS)
    nb = half // tr

    def body(core_ref, g_ref, r_ref, o_ref):
        o_ref[...] = (g_ref[...].astype(F32) + r_ref[...].astype(F32)).astype(o_ref.dtype)

    return pl.pallas_call(
        body, name=name,
        grid_spec=pltpu.PrefetchScalarGridSpec(
            num_scalar_prefetch=1, grid=(n, nb),
            in_specs=[pl.BlockSpec((None, tr, C), lambda k, i, core_ref: (k, core_ref[0] * nb + i, 0)),
                      pl.BlockSpec((None, tr, C), lambda k, i, core_ref: (k, i, 0))],
            out_specs=pl.BlockSpec((None, tr, C), lambda k, i, core_ref: (k, i, 0))),
        out_shape=jax.ShapeDtypeStruct((n, half, C), gp.dtype), compiler_params=_params("parallel", "parallel"),
    )(core, gp, got)


def _chip_exchange_body(t_refs, out_refs, send_sems, recv_sems):
    x, y, c = _position()
    chips = [(1 - x, y), (x, 1 - y), (1 - x, 1 - y)]
    copies, waits = [], []
    for t, (t_ref, out_ref) in enumerate(zip(t_refs, out_refs)):
        for j, (cx, cy) in enumerate(chips):
            sems = dict(send_sem=send_sems.at[3 * t + j], recv_sem=recv_sems.at[3 * t + j], device_id=(cx, cy, c),
                        device_id_type=MESH)
            copies.append(pltpu.make_async_remote_copy(src_ref=t_ref.at[2 * cx + cy], dst_ref=out_ref.at[2 * x + y], **sems))
            waits.append(pltpu.make_async_remote_copy(src_ref=t_ref.at[2 * cx + cy], dst_ref=out_ref.at[2 * cx + cy], **sems))
    for cp in copies:
        cp.start()
    for cp in waits:
        cp.wait_recv()
    for cp in copies:
        cp.wait_send()


def _chip_exchange_later(ts, layer):
    n = len(ts)
    t_refs = [jax.new_ref(t, memory_space=pltpu.MemorySpace.HBM) for t in ts]
    out_refs = [jax.empty_ref(jax.ShapeDtypeStruct(t.shape, t.dtype), memory_space=pltpu.MemorySpace.HBM) for t in ts]

    @pl.kernel(mesh=plsc.ScalarSubcoreMesh(axis_name="sequencer", num_cores=1), name=f"chip_exchange_later{layer}",
               scratch_types=(pltpu.SemaphoreType.DMA((3 * n,)), pltpu.SemaphoreType.DMA((3 * n,))),
               compiler_params=pltpu.CompilerParams(collective_id=EXCHANGE_IDS[layer]))
    def launch(send_sems, recv_sems):
        x, y, c = _position()
        barrier = pltpu.get_barrier_semaphore()
        for peer in [(1 - x, y, c), (x, 1 - y, c), (1 - x, 1 - y, c)]:
            pl.semaphore_signal(barrier, inc=1, device_id=peer, device_id_type=MESH)
        pl.semaphore_wait(barrier, 3)
        _chip_exchange_body(t_refs, out_refs, send_sems, recv_sems)

    launch()
    return [r[...] for r in out_refs]


def _chip_sum(name, t, got, ids):
    n, H, C = t.shape
    tr = _tile(H, SUM_ROWS)
    nb = H // tr

    def body(ids_ref, t_ref, r_ref, o_ref):
        own = t_ref[...].astype(F32)
        acc = jnp.where(ids_ref[0] == 0, own, r_ref[0].astype(F32))
        for k in range(1, n):
            acc = acc + jnp.where(ids_ref[0] == k, own, r_ref[k].astype(F32))
        o_ref[...] = acc

    return pl.pallas_call(
        body, name=name,
        grid_spec=pltpu.PrefetchScalarGridSpec(
            num_scalar_prefetch=1, grid=(nb,),
            in_specs=[pl.BlockSpec((None, tr, C), lambda i, ids_ref: (ids_ref[0], i, 0)),
                      pl.BlockSpec((n, tr, C), lambda i, ids_ref: (0, i, 0))],
            out_specs=pl.BlockSpec((tr, C), lambda i, ids_ref: (ids_ref[1] * nb + i, 0))),
        out_shape=jax.ShapeDtypeStruct((2 * H, C), F32), compiler_params=_params("parallel"),
    )(ids, t, got)


def _sum_pieces(name, pieces):
    n, R, C = pieces.shape
    tr = _tile(R, (256, 128, SUBLANES))

    def body(p_ref, o_ref):
        acc = p_ref[0].astype(F32)
        for s in range(1, n):
            acc = acc + p_ref[s].astype(F32)
        o_ref[...] = acc

    return pl.pallas_call(
        body, name=name, grid=(R // tr,),
        in_specs=[pl.BlockSpec((n, tr, C), lambda i: (0, i, 0))], out_specs=pl.BlockSpec((tr, C), lambda i: (i, 0)),
        out_shape=jax.ShapeDtypeStruct((R, C), F32), compiler_params=_params("parallel"),
    )(pieces)


def _swap_halves(name, ss):
    n = len(ss)

    def body(*refs):
        s_refs, out_refs, (send_sems, recv_sems) = refs[:n], refs[n:2 * n], refs[2 * n:]
        x, y, c = _position()
        copies, waits = [], []
        for t, (s_ref, out_ref) in enumerate(zip(s_refs, out_refs)):
            half = s_ref.shape[0] // 2
            sems = dict(send_sem=send_sems.at[t], recv_sem=recv_sems.at[t], device_id=(x, y, 1 - c), device_id_type=MESH)
            mine = s_ref.at[pl.ds(c * half, half), :]
            copies.append(pltpu.make_async_remote_copy(src_ref=mine, dst_ref=out_ref.at[pl.ds(c * half, half), :], **sems))
            waits.append(pltpu.make_async_remote_copy(src_ref=mine, dst_ref=out_ref.at[pl.ds((1 - c) * half, half), :], **sems))
        for cp in copies:
            cp.start()
        for cp in waits:
            cp.wait_recv()
        for cp in copies:
            cp.wait_send()

    return pl.pallas_call(
        body, name=name, in_specs=[ANY] * n, out_specs=[ANY] * n, input_output_aliases={i: i for i in range(n)},
        out_shape=[jax.ShapeDtypeStruct(s_.shape, s_.dtype) for s_ in ss],
        scratch_shapes=[pltpu.SemaphoreType.DMA((n,)), pltpu.SemaphoreType.DMA((n,))],
    )(*ss)


def _gather_small(name, blk):
    m_per, n = blk.shape

    def body(x_ref, out_ref, send_sems, recv_sems, local_sem):
        x, y, c = _position()
        me, sibling = (x, y, c), (x, y, 1 - c)
        chips = [(1 - x, y), (x, 1 - y), (1 - x, 1 - y)]

        def rows(px, py, pc):
            return out_ref.at[pl.ds((4 * px + 2 * py + pc) * m_per, m_per), :]

        def copy(k, block, to, src=None):
            return pltpu.make_async_remote_copy(src_ref=rows(*block) if src is None else src, dst_ref=rows(*block),
                                                send_sem=send_sems.at[k], recv_sem=recv_sems.at[k], device_id=to, device_id_type=MESH)

        mine = pltpu.make_async_copy(x_ref, rows(*me), local_sem)
        mine.start()
        first = [copy(0, me, sibling, src=x_ref)] + [copy(1 + j, me, (*chip, c), src=x_ref) for j, chip in enumerate(chips)]
        for cp in first:
            cp.start()
        passed = [copy(4 + j, (*chip, c), sibling) for j, chip in enumerate(chips)]
        for j, chip in enumerate(chips):
            copy(1 + j, (*chip, c), me).wait_recv()
            passed[j].start()
        copy(0, sibling, me).wait_recv()
        for j, chip in enumerate(chips):
            copy(4 + j, (*chip, 1 - c), me).wait_recv()
        for cp in first + passed:
            cp.wait_send()
        mine.wait()

    return pl.pallas_call(
        body, name=name, out_shape=jax.ShapeDtypeStruct((N_DEV * m_per, n), blk.dtype),
        in_specs=[pl.BlockSpec(memory_space=pltpu.VMEM)], out_specs=pl.BlockSpec(memory_space=pltpu.VMEM),
        scratch_shapes=[pltpu.SemaphoreType.DMA((7,)), pltpu.SemaphoreType.DMA((7,)), pltpu.SemaphoreType.DMA],
        compiler_params=pltpu.CompilerParams(vmem_limit_bytes=VMEM_LIMIT_BYTES),
    )(blk)


WEIGHTS = ("norm_mix_g", "norm_mlp_g", "mlp_w1", "mlp_w2", "gdn_w_in", "gdn_conv_w", "gdn_a_log", "gdn_dt_bias", "gdn_o_norm_g",
           "gdn_w_out", "s5_w_in", "s5_lam_re", "s5_lam_im", "s5_log_dt", "s5_b_re", "s5_b_im", "s5_c_re", "s5_c_im", "s5_d",
           "s5_w_out", "m2_w_in", "m2_conv_w", "m2_conv_b", "m2_dt_bias", "m2_a_log", "m2_d", "m2_norm_g", "m2_w_out",
           "final_norm_g")
BIG = {"mlp_w1": 2, "mlp_w2": 1, "gdn_w_in": 2, "gdn_w_out": 1, "s5_w_in": 1, "s5_w_out": 2, "m2_w_in": 2, "m2_w_out": 1}
SMALL_CUT = {"gdn_conv_w": 2, "m2_conv_w": 2, "m2_conv_b": 1, "m2_norm_g": 1}
LAYER_ITEMS = (
    ((("mlp_w1", 0), ("mlp_w2", 0), ("gdn_w_out", 0)), (("gdn_w_in", 0),)),
    ((("mlp_w1", 1), ("mlp_w2", 1), ("s5_w_in", 0)), (("s5_w_out", 0),)),
    ((("mlp_w1", 2), ("mlp_w2", 2), ("m2_w_out", 0)), (("m2_w_in", 0),)),
    ((("mlp_w1", 3), ("mlp_w2", 3), ("gdn_w_out", 1)), (("gdn_w_in", 1),)),
)


def _rows2d(a):
    return a.reshape(-1, a.shape[-1])


def _pack(arrays, cols, row_multiple, dtype):
    flat = jnp.concatenate([a.reshape(-1).astype(dtype) for a in arrays])
    n = -(-flat.shape[0] // (cols * row_multiple)) * cols * row_multiple
    return jnp.pad(flat, (0, n - flat.shape[0])).reshape(-1, cols)


def _unpack(packed, shapes):
    flat = packed.reshape(-1)
    out, off = [], 0
    for shp in shapes:
        n = math.prod(shp)
        out.append(flat[off:off + n].reshape(shp))
        off += n
    return out


def _split_rows(buf, shapes):
    out, off = [], 0
    for shp in shapes:
        rows = math.prod(shp[:-1])
        out.append(buf[off:off + rows].reshape(shp))
        off += rows
    return out


def _cut(a, axis, k):
    n = a.shape[axis] // N_CHIPS
    return lax.slice_in_dim(a, k * n, (k + 1) * n, axis=axis)


def kernel(x, norm_mix_g, norm_mlp_g, mlp_w1, mlp_w2, gdn_w_in, gdn_conv_w, gdn_a_log, gdn_dt_bias, gdn_o_norm_g, gdn_w_out, s5_w_in, s5_lam_re, s5_lam_im, s5_log_dt, s5_b_re, s5_b_im, s5_c_re, s5_c_im, s5_d, s5_w_out, m2_w_in, m2_conv_w, m2_conv_b, m2_dt_bias, m2_a_log, m2_d, m2_norm_g, m2_w_out, final_norm_g, loss_target, m_norm_mix_g, m_norm_mlp_g, m_mlp_w1, m_mlp_w2, m_gdn_w_in, m_gdn_conv_w, m_gdn_a_log, m_gdn_dt_bias, m_gdn_o_norm_g, m_gdn_w_out, m_s5_w_in, m_s5_lam_re, m_s5_lam_im, m_s5_log_dt, m_s5_b_re, m_s5_b_im, m_s5_c_re, m_s5_c_im, m_s5_d, m_s5_w_out, m_m2_w_in, m_m2_conv_w, m_m2_conv_b, m_m2_dt_bias, m_m2_a_log, m_m2_d, m_m2_norm_g, m_m2_w_out, m_final_norm_g, v_norm_mix_g, v_norm_mlp_g, v_mlp_w1, v_mlp_w2, v_gdn_w_in, v_gdn_conv_w, v_gdn_a_log, v_gdn_dt_bias, v_gdn_o_norm_g, v_gdn_w_out, v_s5_w_in, v_s5_lam_re, v_s5_lam_im, v_s5_log_dt, v_s5_b_re, v_s5_b_im, v_s5_c_re, v_s5_c_im, v_s5_d, v_s5_w_out, v_m2_w_in, v_m2_conv_w, v_m2_conv_b, v_m2_dt_bias, v_m2_a_log, v_m2_d, v_m2_norm_g, v_m2_w_out, v_final_norm_g):
    given = dict(locals())
    w = {n: given[n] for n in WEIGHTS}
    mom = {n: given["m_" + n] for n in WEIGHTS}
    var = {n: given["v_" + n] for n in WEIGHTS}
    big, small_cut = tuple(BIG), tuple(SMALL_CUT)
    small = tuple(n for n in WEIGHTS if n not in BIG)
    chip = 2 * lax.axis_index("x") + lax.axis_index("y")

    W = {n: [None] * w[n].shape[0] for n in big}
    before = ()
    for layer, groups in enumerate(LAYER_ITEMS):
        own = [jnp.concatenate([w[n][l] for n, l in grp]).astype(BF16) for grp in groups]
        if layer == 0:
            gathered = _gather_shards(own)
        else:
            own, before = lax.optimization_barrier((own, before))
            gathered = _gather_shards_later(own, layer)
        before = gathered
        for grp, mine, got in zip(groups, own, gathered):
            shapes = [w[n][l].shape for n, l in grp]
            per_chip = [_split_rows(jnp.where(chip == k, mine, got[k]), shapes) for k in range(N_CHIPS)]
            for i, (n, l) in enumerate(grp):
                W[n][l] = jnp.concatenate([per_chip[k][i] for k in range(N_CHIPS)], axis=BIG[n] - 1)
    W["gdn_w_in"] = [jnp.pad(m, ((0, 0), (0, GDN_IN_PAD - GDN_IN))) for m in W["gdn_w_in"]]
    W["m2_w_in"] = [jnp.pad(m, ((0, 0), (0, M2_IN_PAD - M2_IN))) for m in W["m2_w_in"]]
    cut_blk = _pack([w[n] for n in small_cut], LANES, SUBLANES, F32)
    cut_all = _gather_small("gather_small_params", cut_blk).reshape(N_DEV, *cut_blk.shape)
    per_chip = [_unpack(cut_all[2 * k], [w[n].shape for n in small_cut]) for k in range(N_CHIPS)]
    W.update({n: jnp.concatenate([per_chip[k][i] for k in range(N_CHIPS)], axis=SMALL_CUT[n]) for i, n in enumerate(small_cut)})
    W.update({n: w[n] for n in small if n not in SMALL_CUT})

    core = lax.axis_index("c").astype(jnp.int32)
    ids = jnp.stack([chip.astype(jnp.int32), core])
    shard_grads = {}

    def reduce_layer(i, dws):
        groups = LAYER_ITEMS[i]
        gps = [jnp.stack([jnp.concatenate([_cut(dws[it], BIG[it[0]] - 1, k) for it in grp]).astype(BF16) for k in range(N_CHIPS)])
               for grp in groups]
        pairs = [_pair_sum(f"pair_sum{i}_{j}", gp, got, core.reshape(1))
                 for j, (gp, got) in enumerate(zip(gps, _pair_exchange(f"pair_exchange{i}", gps)))]
        sums = [_chip_sum(f"chip_sum{i}_{j}", t, got, ids) for j, (t, got) in enumerate(zip(pairs, _chip_exchange_later(pairs, i)))]
        for grp, g_shard in zip(groups, _swap_halves(f"swap_halves{i}", sums)):
            shard_grads.update(zip(grp, _split_rows(g_shard, [w[n][l].shape for n, l in grp])))

    loss, grad_x, G = _local_step(x[0], loss_target[0], W, reduce_layer)
    loss = lax.psum(loss[0, 0], ("x", "y", "c"))
    grads = {n: jnp.stack([shard_grads[n, l] for l in range(w[n].shape[0])]) for n in big}
    sg = _pack([G[n] for n in small], LANES, ADAM_ROWS, F32)
    sg_sum = _sum_pieces("sum_small_grads", _gather_small("gather_small_grads", sg).reshape(N_DEV, *sg.shape))
    for n, g in zip(small, _unpack(sg_sum, [G[n].shape for n in small])):
        if n in SMALL_CUT:
            width = g.shape[SMALL_CUT[n]] // N_CHIPS
            g = lax.dynamic_slice_in_dim(g, chip * width, width, axis=SMALL_CUT[n])
        grads[n] = g.reshape(w[n].shape)

    delta, new_m, new_v = {}, {}, {}
    for n in big:
        as2d = lambda a: a.reshape(-1, a.shape[-1])
        outs = _adamw("adamw_" + n, as2d(w[n]), as2d(grads[n]), as2d(mom[n]), as2d(var[n]))
        delta[n], new_m[n], new_v[n] = (o.reshape(w[n].shape) for o in outs)
    packs = [_pack([t[n] for n in small], LANES, ADAM_ROWS, F32) for t in (w, grads, mom, var)]
    outs = _adamw("adamw_small", *packs)
    for t, o in zip((delta, new_m, new_v), outs):
        t.update(zip(small, _unpack(o, [w[n].shape for n in small])))

    return (loss, grad_x[None], *[grads[n] for n in WEIGHTS], *[delta[n] for n in WEIGHTS], *[new_m[n] for n in WEIGHTS],
            *[new_v[n] for n in WEIGHTS])
```

```python
import functools
import math

import numpy as np
import jax
import jax.numpy as jnp
from jax import lax
from jax.experimental import pallas as pl
from jax.experimental.pallas import tpu as pltpu
from jax.experimental.pallas import tpu_sc as plsc

F32 = jnp.float32
BF16 = jnp.bfloat16

D_MODEL = 1024
D_FF = 4096
DEPTH = 4
CHUNK = 64
RMS_EPS = 1e-6
CONV_W = 4
GDN_HEADS = 8
GDN_DK = 128
GDN_IN = 4112
GDN_IN_PAD = 4224
S5_GROUPS = 64
S5_STATE = 64
S5_GROUP = 16
S5_BLOCKS = 8
M2_INNER = 2048
M2_HEADS = 32
M2_GROUPS = 8
M2_STATE = 128
M2_CONV_CH = 4096
M2_IN = 6176
M2_IN_PAD = 6272
ADAM_LR, ADAM_B1, ADAM_B2, ADAM_EPS, ADAM_WD, ADAM_STEP = 0.001, 0.9, 0.999, 1e-08, 0.01, 10

VMEM_LIMIT_BYTES = 56 * 1024 * 1024
SUBLANES = 8
LANES = 128


def _params(*sem):
    return pltpu.CompilerParams(dimension_semantics=tuple(sem) if sem else None, vmem_limit_bytes=VMEM_LIMIT_BYTES)


NN, NT, TN = ((1,), (0,)), ((1,), (1,)), ((0,), (0,))
_DOT_TRANSPOSES = {NN: ((NT, "gb"), (TN, "ag")), NT: ((NN, "gb"), (TN, "ga")), TN: ((NT, "bg"), (NN, "ag"))}


def _dg(a, b, dims):
    if a.ndim == 3:
        dn = (((dims[0][0] + 1,), (dims[1][0] + 1,)), ((0,), (0,)))
    else:
        dn = (dims, ((), ()))
    return lax.dot_general(a, b, dn, preferred_element_type=F32)


def _mxu(a, b, dims):
    return _dg(a.astype(BF16), b.astype(BF16), dims)


@functools.partial(jax.custom_vjp, nondiff_argnums=(2,))
def _dot(a, b, dims=NN):
    return _mxu(a, b, dims)


def _dot_fwd(a, b, dims):
    return _mxu(a, b, dims), (a, b)


def _dot_bwd(dims, res, g):
    ops = dict(a=res[0], b=res[1], g=g)
    (da_dims, da_ops), (db_dims, db_ops) = _DOT_TRANSPOSES[dims]
    return (_mxu(ops[da_ops[0]], ops[da_ops[1]], da_dims).astype(res[0].dtype),
            _mxu(ops[db_ops[0]], ops[db_ops[1]], db_dims).astype(res[1].dtype))


_dot.defvjp(_dot_fwd, _dot_bwd)


def _nt(a, b):
    return _dot(a, b, NT)


def _tn(a, b):
    return _dot(a, b, TN)


def _split3(x):
    x1 = x.astype(BF16)
    r = x - x1.astype(F32)
    x2 = r.astype(BF16)
    return x1, x2, (r - x2.astype(F32)).astype(BF16)


def _sel_mxu(x, sel, dims, x_first):
    f = (lambda p: _dg(p, sel.astype(BF16), dims)) if x_first else (lambda p: _dg(sel.astype(BF16), p, dims))
    x1, x2, x3 = _split3(x)
    return f(x1) + (f(x2) + f(x3))


@jax.custom_vjp
def _pick(x, sel):
    return _sel_mxu(x, sel, NN, True)


def _pick_fwd(x, sel):
    return _sel_mxu(x, sel, NN, True), sel


def _pick_bwd(sel, g):
    return _sel_mxu(g, sel, NT, True), jnp.zeros_like(sel)


_pick.defvjp(_pick_fwd, _pick_bwd)


@jax.custom_vjp
def _accum(sel, x):
    return _sel_mxu(x, sel, NN, False)


def _accum_fwd(sel, x):
    return _sel_mxu(x, sel, NN, False), sel


def _accum_bwd(sel, g):
    return jnp.zeros_like(sel), _sel_mxu(g, sel, TN, False)


_accum.defvjp(_accum_fwd, _accum_bwd)


def _dot3(a, b, dims=NN):
    ah, bh = a.astype(BF16), b.astype(BF16)
    al, bl = (a - ah.astype(F32)).astype(BF16), (b - bh.astype(F32)).astype(BF16)
    return _dg(ah, bh, dims) + (_dg(ah, bl, dims) + _dg(al, bh, dims))


def _neumann(x, r, dims):
    r = r + _dot3(x, r, dims)
    for _ in range(5):
        x = _dot3(x, x)
        r = r + _dot3(x, r, dims)
    return r


@jax.custom_vjp
def _unit_lower_solve(a, rhs):
    return _neumann(-a, rhs, NN)


def _unit_lower_solve_fwd(a, rhs):
    sol = _neumann(-a, rhs, NN)
    return sol, (a, sol)


def _unit_lower_solve_bwd(res, ct):
    a, sol = res
    d_rhs = _neumann(-a, ct, TN)
    return -_dot3(d_rhs, sol, NT), d_rhs


_unit_lower_solve.defvjp(_unit_lower_solve_fwd, _unit_lower_solve_bwd)


def _sigmoid(x):
    return 1.0 / (1.0 + jnp.exp(-x))


def _softplus(x):
    return jnp.maximum(x, 0.0) + jnp.log(1.0 + jnp.exp(-jnp.abs(x)))


def _iota2(shape, axis):
    return lax.broadcasted_iota(jnp.int32, shape, axis)


def _tile(n, cands):
    for c in cands:
        if n % c == 0:
            return c
    return n


MM_TILE_BYTES = 9 * 1024 * 1024


def _mm(name, a, b, mode, out_dtypes, epi=None, extras=(), tn=None):
    if mode == "nn":
        (M, K), N = a.shape, b.shape[1]
    elif mode == "nt":
        (M, K), N = a.shape, b.shape[0]
    else:
        (K, M), N = a.shape, b.shape[1]
    tn = tn or _tile(N, (512, 384, 896, 256, 128))
    out_bytes = tn * (sum(jnp.dtype(d).itemsize for d in out_dtypes) + sum(e.dtype.itemsize for e in extras))
    fits = lambda t: t * K * a.dtype.itemsize <= MM_TILE_BYTES and t * out_bytes <= MM_TILE_BYTES
    tm = next(t for t in (2048, 1024, 512, 256, 128) if M % t == 0 and (fits(t) or t == 128))
    if mode == "nn":
        a_spec, b_spec = pl.BlockSpec((tm, K), lambda i, j: (i, 0)), pl.BlockSpec((K, tn), lambda i, j: (0, j))
        dims = NN
    elif mode == "nt":
        a_spec, b_spec = pl.BlockSpec((tm, K), lambda i, j: (i, 0)), pl.BlockSpec((tn, K), lambda i, j: (j, 0))
        dims = NT
    else:
        a_spec, b_spec = pl.BlockSpec((K, tm), lambda i, j: (0, i)), pl.BlockSpec((K, tn), lambda i, j: (0, j))
        dims = TN
    n_ex = len(extras)

    def body(a_ref, b_ref, *rest):
        acc = _mxu(a_ref[...], b_ref[...], dims)
        res = epi(acc, *[e[...] for e in rest[:n_ex]]) if epi is not None else (acc,)
        for o_ref, r in zip(rest[n_ex:], res):
            o_ref[...] = r.astype(o_ref.dtype)

    tile = pl.BlockSpec((tm, tn), lambda i, j: (i, j))
    out = pl.pallas_call(
        body, name=name, grid=(M // tm, N // tn),
        in_specs=[a_spec, b_spec] + [tile] * n_ex,
        out_specs=[tile] * len(out_dtypes),
        out_shape=[jax.ShapeDtypeStruct((M, N), d) for d in out_dtypes],
        compiler_params=_params("parallel", "parallel"),
    )(a, b, *extras)
    return out if len(out_dtypes) > 1 else out[0]


def _rms_fwd(name, h, g):
    L, D = h.shape
    tr = _tile(L, (256, 128))

    def body(h_ref, g_ref, o_ref):
        x = h_ref[...]
        r = lax.rsqrt(jnp.mean(x * x, axis=-1, keepdims=True) + RMS_EPS)
        o_ref[...] = (x * r * g_ref[...]).astype(o_ref.dtype)

    return pl.pallas_call(
        body, name=name, grid=(L // tr,),
        in_specs=[pl.BlockSpec((tr, D), lambda i: (i, 0)), pl.BlockSpec((1, D), lambda i: (0, 0))],
        out_specs=pl.BlockSpec((tr, D), lambda i: (i, 0)),
        out_shape=jax.ShapeDtypeStruct((L, D), BF16),
        compiler_params=_params("parallel"),
    )(h, g.reshape(1, D))


def _rms_bwd(name, h, g, dhn, dres):
    L, D = h.shape
    tr = _tile(L, (256, 128))

    def body(h_ref, g_ref, dhn_ref, dres_ref, dh_ref, dg_ref):
        x = h_ref[...]
        r = lax.rsqrt(jnp.mean(x * x, axis=-1, keepdims=True) + RMS_EPS)
        xh = x * r
        dy = dhn_ref[...]
        dxh = dy * g_ref[...]
        dh_ref[...] = dres_ref[...] + r * (dxh - xh * jnp.mean(dxh * xh, axis=-1, keepdims=True))

        @pl.when(pl.program_id(0) == 0)
        def _():
            dg_ref[...] = jnp.zeros_like(dg_ref)

        dg_ref[...] += jnp.sum(dy * xh, axis=0, keepdims=True)

    row = pl.BlockSpec((tr, D), lambda i: (i, 0))
    vec = pl.BlockSpec((1, D), lambda i: (0, 0))
    return pl.pallas_call(
        body, name=name, grid=(L // tr,),
        in_specs=[row, vec, row, row], out_specs=[row, vec],
        out_shape=[jax.ShapeDtypeStruct((L, D), F32), jax.ShapeDtypeStruct((1, D), F32)],
        compiler_params=_params("arbitrary"),
    )(h, g.reshape(1, D), dhn, dres)


def _loss_head(h, g, target):
    L, D = h.shape
    tr = _tile(L, (256, 128))

    def body(h_ref, g_ref, t_ref, loss_ref, dh_ref, dg_ref):
        x = h_ref[...]
        r = lax.rsqrt(jnp.mean(x * x, axis=-1, keepdims=True) + RMS_EPS)
        xh = x * r
        err = xh * g_ref[...] - t_ref[...]
        dy = err * (1.0 / D)
        dxh = dy * g_ref[...]
        dh_ref[...] = r * (dxh - xh * jnp.mean(dxh * xh, axis=-1, keepdims=True))

        @pl.when(pl.program_id(0) == 0)
        def _():
            dg_ref[...] = jnp.zeros_like(dg_ref)
            loss_ref[...] = jnp.zeros_like(loss_ref)

        dg_ref[...] += jnp.sum(dy * xh, axis=0, keepdims=True)
        loss_ref[...] += (0.5 / D) * jnp.sum(jnp.sum(err * err, axis=-1, keepdims=True), axis=0, keepdims=True)

    row = pl.BlockSpec((tr, D), lambda i: (i, 0))
    vec = pl.BlockSpec((1, D), lambda i: (0, 0))
    return pl.pallas_call(
        body, name="loss_head", grid=(L // tr,),
        in_specs=[row, vec, row], out_specs=[pl.BlockSpec((1, 1), lambda i: (0, 0)), row, vec],
        out_shape=[jax.ShapeDtypeStruct((1, 1), F32), jax.ShapeDtypeStruct((L, D), F32), jax.ShapeDtypeStruct((1, D), F32)],
        compiler_params=_params("arbitrary"),
    )(h, g.reshape(1, D), target)


def _glu_fwd(h, ag):
    L, D = h.shape
    tr = _tile(L, (256, 128))

    def body(h_ref, v_ref, g_ref, o_ref):
        o_ref[...] = h_ref[...] + v_ref[...] * _sigmoid(g_ref[...])

    return pl.pallas_call(
        body, name="s5_glu_fwd", grid=(L // tr,),
        in_specs=[pl.BlockSpec((tr, D), lambda i: (i, 0)), pl.BlockSpec((tr, D), lambda i: (i, 0)),
                  pl.BlockSpec((tr, D), lambda i: (i, 1))],
        out_specs=pl.BlockSpec((tr, D), lambda i: (i, 0)),
        out_shape=jax.ShapeDtypeStruct((L, D), F32),
        compiler_params=_params("parallel"),
    )(h, ag, ag)


def _glu_bwd(dh, ag):
    L, D = dh.shape
    tr = _tile(L, (256, 128))

    def body(dh_ref, v_ref, g_ref, dv_ref, dg_ref):
        s = _sigmoid(g_ref[...])
        d = dh_ref[...]
        dv_ref[...] = d * s
        dg_ref[...] = d * v_ref[...] * s * (1.0 - s)

    dv, dg = pl.pallas_call(
        body, name="s5_glu_bwd", grid=(L // tr,),
        in_specs=[pl.BlockSpec((tr, D), lambda i: (i, 0)), pl.BlockSpec((tr, D), lambda i: (i, 0)),
                  pl.BlockSpec((tr, D), lambda i: (i, 1))],
        out_specs=[pl.BlockSpec((tr, D), lambda i: (i, 0))] * 2,
        out_shape=[jax.ShapeDtypeStruct((L, D), F32)] * 2,
        compiler_params=_params("parallel"),
    )(dh, ag, ag)
    return jnp.concatenate([dv, dg], axis=1).astype(BF16)


CONV_ROWS = 128
CONV_COLS = 512


def _shift_rows(cat, s):
    if s == 0:
        return cat[SUBLANES:, :]
    return pltpu.roll(cat, s, axis=0)[SUBLANES:, :]


def _conv_fwd(name, p, col0, w, b):
    L = p.shape[0]
    C = w.shape[1]
    tc = _tile(C, (CONV_COLS, 256))
    cb0 = col0 // tc
    nr = L // CONV_ROWS

    def body(x_ref, w_ref, b_ref, o_ref):
        def step(r, carry):
            r0 = pl.multiple_of(r * CONV_ROWS, CONV_ROWS)
            cur = x_ref[pl.ds(r0, CONV_ROWS), :]
            p0 = pl.multiple_of(jnp.maximum(r0 - SUBLANES, 0), SUBLANES)
            prev = jnp.where(r > 0, x_ref[pl.ds(p0, SUBLANES), :], 0.0)
            cat = jnp.concatenate([prev, cur], axis=0)
            acc = b_ref[...] + w_ref[3:4, :] * cur
            for k in range(CONV_W - 1):
                acc = acc + w_ref[k:k + 1, :] * _shift_rows(cat, CONV_W - 1 - k)
            o_ref[pl.ds(r0, CONV_ROWS), :] = acc * _sigmoid(acc)
            return carry

        lax.fori_loop(0, nr, step, 0)

    return pl.pallas_call(
        body, name=name, grid=(C // tc,),
        in_specs=[pl.BlockSpec((L, tc), lambda j: (0, cb0 + j)), pl.BlockSpec((CONV_W, tc), lambda j: (0, j)),
                  pl.BlockSpec((1, tc), lambda j: (0, j))],
        out_specs=pl.BlockSpec((L, tc), lambda j: (0, j)),
        out_shape=jax.ShapeDtypeStruct((L, C), F32),
        compiler_params=_params("parallel"),
    )(p, w, b)


def _conv_bwd(name, p, col0, w, b, dout):
    L = p.shape[0]
    C = w.shape[1]
    tc = _tile(C, (CONV_COLS, 256))
    cb0 = col0 // tc
    nr = L // CONV_ROWS

    def body(x_ref, w_ref, b_ref, do_ref, dx_ref, dw_ref, db_ref, dpre_ref):
        def step1(r, carry):
            dw0, dw1, dw2, dw3, dbb = carry
            r0 = pl.multiple_of(r * CONV_ROWS, CONV_ROWS)
            cur = x_ref[pl.ds(r0, CONV_ROWS), :]
            p0 = pl.multiple_of(jnp.maximum(r0 - SUBLANES, 0), SUBLANES)
            prev = jnp.where(r > 0, x_ref[pl.ds(p0, SUBLANES), :], 0.0)
            cat = jnp.concatenate([prev, cur], axis=0)
            sh = [_shift_rows(cat, CONV_W - 1 - k) for k in range(CONV_W - 1)] + [cur]
            acc = b_ref[...] + w_ref[3:4, :] * cur
            for k in range(CONV_W - 1):
                acc = acc + w_ref[k:k + 1, :] * sh[k]
            sg = _sigmoid(acc)
            dpre = do_ref[pl.ds(r0, CONV_ROWS), :] * (sg + acc * sg * (1.0 - sg))
            dpre_ref[pl.ds(r0, CONV_ROWS), :] = dpre
            dws = [d + jnp.sum(dpre * s, axis=0, keepdims=True) for d, s in zip((dw0, dw1, dw2, dw3), sh)]
            return (*dws, dbb + jnp.sum(dpre, axis=0, keepdims=True))

        z = jnp.zeros((1, tc), F32)
        dw0, dw1, dw2, dw3, dbb = lax.fori_loop(0, nr, step1, (z, z, z, z, z))
        dw_ref[...] = jnp.concatenate([dw0, dw1, dw2, dw3, z, z, z, z], axis=0)
        db_ref[...] = dbb

        def step2(r, carry):
            r0 = pl.multiple_of(r * CONV_ROWS, CONV_ROWS)
            cur = dpre_ref[pl.ds(r0, CONV_ROWS), :]
            n0 = pl.multiple_of(jnp.minimum(r0 + CONV_ROWS, L - SUBLANES), SUBLANES)
            nxt = jnp.where(r < nr - 1, dpre_ref[pl.ds(n0, SUBLANES), :], 0.0)
            cat = jnp.concatenate([cur, nxt], axis=0)
            acc = w_ref[3:4, :] * cur
            for k in range(CONV_W - 1):
                s = CONV_W - 1 - k
                acc = acc + w_ref[k:k + 1, :] * pltpu.roll(cat, CONV_ROWS + SUBLANES - s, axis=0)[:CONV_ROWS, :]
            dx_ref[pl.ds(r0, CONV_ROWS), :] = acc
            return carry

        lax.fori_loop(0, nr, step2, 0)

    dx, dw, db = pl.pallas_call(
        body, name=name, grid=(C // tc,),
        in_specs=[pl.BlockSpec((L, tc), lambda j: (0, cb0 + j)), pl.BlockSpec((CONV_W, tc), lambda j: (0, j)),
                  pl.BlockSpec((1, tc), lambda j: (0, j)), pl.BlockSpec((L, tc), lambda j: (0, j))],
        out_specs=[pl.BlockSpec((L, tc), lambda j: (0, j)), pl.BlockSpec((SUBLANES, tc), lambda j: (0, j)),
                   pl.BlockSpec((1, tc), lambda j: (0, j))],
        out_shape=[jax.ShapeDtypeStruct((L, C), F32), jax.ShapeDtypeStruct((SUBLANES, C), F32),
                   jax.ShapeDtypeStruct((1, C), F32)],
        scratch_shapes=[pltpu.VMEM((L, tc), F32)],
        compiler_params=_params("parallel"),
    )(p, w, b, dout)
    return dx, dw[:CONV_W], db


def _chunk_consts():
    r, c = _iota2((CHUNK, CHUNK), 0), _iota2((CHUNK, CHUNK), 1)
    causal = r >= c
    return causal, r > c, (r == c).astype(F32), causal.astype(F32), jnp.ones((CHUNK, CHUNK), F32)


def _by_lanes(t):
    return jnp.concatenate([t[i] for i in range(t.shape[0])], axis=1)


def _by_batch(t, w):
    return jnp.concatenate([t[None, :, i * w:(i + 1) * w] for i in range(t.shape[1] // w)], axis=0)


def _diag_lanes():
    return (_iota2((CHUNK, LANES), 0) == _iota2((CHUNK, LANES), 1)).astype(F32)


def _gdn_chunk(q, k, v, ab, gate, S, alog, dtb, og, ea, eb):
    causal, strict, _, tril, ones = _chunk_consts()
    logits = _by_batch(_pick(ab, jnp.concatenate([_by_lanes(ea), _by_lanes(eb)], axis=1)), LANES)
    H = q.shape[0]
    g = -jnp.exp(alog) * _softplus(logits[:H] + dtb)
    beta = _sigmoid(logits[H:])
    qn = q * lax.rsqrt(jnp.sum(q * q, axis=-1, keepdims=True) + 1e-6) * (GDN_DK ** -0.5)
    kn = k * lax.rsqrt(jnp.sum(k * k, axis=-1, keepdims=True) + 1e-6)
    g_l = _by_lanes(g)
    gc = _by_batch(_accum(tril, g_l), LANES)
    glast = _by_batch(_accum(ones, g_l), LANES)
    gcol = gc[:, :, :CHUNK]
    grow = _by_batch(_accum(ones, _by_lanes(gc * _diag_lanes())), LANES)[:, :, :CHUNK]
    decay = jnp.exp(jnp.where(causal, gcol - grow, -jnp.inf))
    a = jnp.where(strict, beta[:, :, :CHUNK] * _nt(kn, kn) * decay, 0.0)
    eg = jnp.exp(gc)
    sol = _unit_lower_solve(a, jnp.concatenate([v * beta, kn * (beta * eg)], axis=2))
    u, w = sol[:, :, :GDN_DK], sol[:, :, GDN_DK:]
    qk = _nt(qn, kn) * decay
    v_new = u - _dot(w, S)
    o = _dot(qn * eg, S) + _dot(qk, v_new)
    cd = jnp.exp(glast)
    s_new = jnp.concatenate([cd, cd], axis=1) * S + _tn(kn * jnp.exp(glast - gc), v_new)
    on = o * lax.rsqrt(jnp.mean(o * o, axis=-1, keepdims=True) + RMS_EPS) * og
    return on * (gate * _sigmoid(gate)), s_new


GDN_HB = 8


def _gdn_specs(nc, rev):
    cm = (lambda c: nc - 1 - c) if rev else (lambda c: c)
    blk = lambda off: pl.BlockSpec((CHUNK, GDN_HB * GDN_DK), lambda c, h: (cm(c), off // GDN_HB + h))
    ab = pl.BlockSpec((CHUNK, LANES), lambda c, h: (cm(c), (GDN_IN_PAD - LANES) // LANES))
    hv = pl.BlockSpec((GDN_HB, 1, LANES), lambda c, h: (h, 0, 0))
    og = pl.BlockSpec((1, LANES), lambda c, h: (0, 0))
    em = pl.BlockSpec((GDN_HB, LANES, LANES), lambda c, h: (h, 0, 0))
    st = pl.BlockSpec((None, GDN_HB, GDN_DK, GDN_DK), lambda c, h: (cm(c), h, 0, 0))
    return blk, ab, hv, og, em, st


def _gdn_fwd(qc, kc, vc, p, alog_e, dtb_e, og, ea, eb):
    L = qc.shape[0]
    nc = L // CHUNK
    blk, ab, hv, ogs, em, st = _gdn_specs(nc, False)

    def body(q_ref, k_ref, v_ref, gate_ref, ab_ref, al_ref, dt_ref, og_ref, ea_ref, eb_ref, y_ref, sp_ref, s_scr):
        c, h = pl.program_id(0), pl.program_id(1)
        lanes = [slice(i * GDN_DK, (i + 1) * GDN_DK) for i in range(GDN_HB)]
        heads = pl.ds(h * GDN_HB, GDN_HB)
        stack = lambda ref: jnp.concatenate([ref[:, ls][None] for ls in lanes], axis=0)

        @pl.when(c == 0)
        def _():
            s_scr[heads] = jnp.zeros((GDN_HB, GDN_DK, GDN_DK), F32)

        S = s_scr[heads]
        sp_ref[...] = S
        y, s_new = _gdn_chunk(stack(q_ref), stack(k_ref), stack(v_ref), ab_ref[...], stack(gate_ref), S,
                              al_ref[...], dt_ref[...], og_ref[...], ea_ref[...], eb_ref[...])
        for i, ls in enumerate(lanes):
            y_ref[:, ls] = y[i]
        s_scr[heads] = s_new

    return pl.pallas_call(
        body, name="gdn_fwd", grid=(nc, GDN_HEADS // GDN_HB),
        in_specs=[blk(0), blk(0), blk(0), blk(3 * GDN_HEADS), ab, hv, hv, ogs, em, em],
        out_specs=[blk(0), st],
        out_shape=[jax.ShapeDtypeStruct((L, D_MODEL), F32), jax.ShapeDtypeStruct((nc, GDN_HEADS, GDN_DK, GDN_DK), F32)],
        scratch_shapes=[pltpu.VMEM((GDN_HEADS, GDN_DK, GDN_DK), F32)],
        compiler_params=_params("arbitrary", "arbitrary"),
    )(qc, kc, vc, p, p, alog_e, dtb_e, og, ea, eb)


def _gdn_bwd(qc, kc, vc, p, alog_e, dtb_e, og, ea, eb, sprev, dy):
    L = qc.shape[0]
    nc = L // CHUNK
    blk, ab, hv, ogs, em, st = _gdn_specs(nc, True)

    def body(q_ref, k_ref, v_ref, gate_ref, ab_ref, al_ref, dt_ref, og_ref, ea_ref, eb_ref, sp_ref, dy_ref,
             dq_ref, dk_ref, dv_ref, dgate_ref, dab_ref, dpar_ref, ds_scr):
        c, h = pl.program_id(0), pl.program_id(1)
        lanes = [slice(i * GDN_DK, (i + 1) * GDN_DK) for i in range(GDN_HB)]
        heads = pl.ds(h * GDN_HB, GDN_HB)
        stack = lambda ref: jnp.concatenate([ref[:, ls][None] for ls in lanes], axis=0)

        @pl.when(c == 0)
        def _():
            ds_scr[heads] = jnp.zeros((GDN_HB, GDN_DK, GDN_DK), F32)
            dpar_ref[heads] = jnp.zeros((GDN_HB, SUBLANES, LANES), F32)

        @pl.when(h == 0)
        def _():
            dab_ref[...] = jnp.zeros_like(dab_ref)

        ea_m, eb_m = ea_ref[...], eb_ref[...]
        f = lambda q, k, v, a_b, gate, S, al, dt, o_g: _gdn_chunk(q, k, v, a_b, gate, S, al, dt, o_g, ea_m, eb_m)
        _, vjp = jax.vjp(f, stack(q_ref), stack(k_ref), stack(v_ref), ab_ref[...], stack(gate_ref), sp_ref[...],
                         al_ref[...], dt_ref[...], og_ref[...])
        dq, dk, dv, dab, dgate, ds, dal, ddt, dog = vjp((stack(dy_ref), ds_scr[heads]))
        for i, ls in enumerate(lanes):
            dq_ref[:, ls] = dq[i]
            dk_ref[:, ls] = dk[i]
            dv_ref[:, ls] = dv[i]
            dgate_ref[:, ls] = dgate[i]
        ds_scr[heads] = ds
        dab_ref[...] += dab
        first = _iota2((GDN_HB, 1, LANES), 0) == 0
        dpar_ref[heads] += jnp.concatenate([dal, ddt, jnp.where(first, dog[None], 0.0),
                                            jnp.zeros((GDN_HB, SUBLANES - 3, LANES), F32)], axis=1)

    return pl.pallas_call(
        body, name="gdn_bwd", grid=(nc, GDN_HEADS // GDN_HB),
        in_specs=[blk(0), blk(0), blk(0), blk(3 * GDN_HEADS), ab, hv, hv, ogs, em, em, st, blk(0)],
        out_specs=[blk(0), blk(0), blk(0), blk(0), pl.BlockSpec((CHUNK, LANES), lambda c, h: (nc - 1 - c, 0)),
                   pl.BlockSpec((GDN_HEADS, SUBLANES, LANES), lambda c, h: (0, 0, 0))],
        out_shape=[jax.ShapeDtypeStruct((L, D_MODEL), F32)] * 4
        + [jax.ShapeDtypeStruct((L, LANES), F32), jax.ShapeDtypeStruct((GDN_HEADS, SUBLANES, LANES), F32)],
        scratch_shapes=[pltpu.VMEM((GDN_HEADS, GDN_DK, GDN_DK), F32)],
        compiler_params=_params("arbitrary", "arbitrary"),
    )(qc, kc, vc, p, p, alog_e, dtb_e, og, ea, eb, sprev, dy)


def _gdn_selectors():
    rows = np.arange(LANES)[None, :, None]
    heads = np.arange(GDN_HEADS)[:, None, None]
    ea = np.broadcast_to(rows == heads, (GDN_HEADS, LANES, LANES)).astype(np.float32)
    eb = np.broadcast_to(rows == heads + GDN_HEADS, (GDN_HEADS, LANES, LANES)).astype(np.float32)
    return jnp.asarray(ea), jnp.asarray(eb)


M2_GW = M2_INNER // M2_GROUPS
M2_HPG = M2_HEADS // M2_GROUPS
M2_HD = M2_INNER // M2_HEADS


def _m2_chunk(x, bm, cm, z, dtr, st, dtb, alog, dsk, ng, e, ecol):
    G = x.shape[0]
    causal, _, _, tril, ones = _chunk_consts()
    dt_n = _softplus(dtr + dtb)
    da_n = dt_n * (-jnp.exp(alog))
    cum_n = _accum(tril, da_n)
    tot_n = _accum(ones, da_n)
    wide = _pick(jnp.concatenate([dt_n, cum_n, tot_n], axis=0), e)
    dt_w, cum_w, tot_w = (_by_batch(wide[i * CHUNK:(i + 1) * CHUNK], M2_GW) for i in range(3))
    xdt = x * dt_w
    cb = _nt(cm, bm)
    heads = lambda t: jnp.concatenate([t[i:i + 1] for i in range(G) for _ in range(M2_HPG)], axis=0)
    colb = _by_batch(_pick(cum_n, ecol), LANES)
    rowb = _by_batch(_accum(ones, _by_lanes(colb * _diag_lanes())), LANES)
    lmat = jnp.exp(jnp.where(causal, colb[:, :, :CHUNK] - rowb[:, :, :CHUNK], -jnp.inf))
    yr = _dot(heads(cb) * lmat, heads(xdt))
    head = _iota2((CHUNK, M2_GW), 1) // M2_HD
    ydiag = jnp.concatenate([sum(jnp.where(head == r, yr[i * M2_HPG + r], 0.0) for r in range(M2_HPG))[None] for i in range(G)], axis=0)
    st_new = _tn(bm, xdt * jnp.exp(tot_w - cum_w))
    cd = jnp.exp(tot_w)
    s_new = jnp.concatenate([cd, cd], axis=1) * st + st_new
    y = ydiag + _dot(cm, st) * jnp.exp(cum_w) + dsk * x
    y = y * (z * _sigmoid(z))
    yn = y * lax.rsqrt(jnp.mean(y * y, axis=-1, keepdims=True) + RMS_EPS) * ng
    return yn, s_new


M2_GB = 4


def _m2_specs(nc, rev):
    cm = (lambda c: nc - 1 - c) if rev else (lambda c: c)
    wide = lambda off: pl.BlockSpec((CHUNK, M2_GB * M2_GW), lambda c, g: (cm(c), off // M2_GB + g))
    nar = lambda off: pl.BlockSpec((CHUNK, M2_GB * LANES), lambda c, g: (cm(c), off // M2_GB + g))
    dts = pl.BlockSpec((CHUNK, LANES), lambda c, g: (cm(c), (M2_IN_PAD - LANES) // LANES))
    v128 = pl.BlockSpec((1, LANES), lambda c, g: (0, 0))
    v256 = pl.BlockSpec((1, M2_GB * M2_GW), lambda c, g: (0, g))
    es = pl.BlockSpec((LANES, M2_GB * M2_GW), lambda c, g: (0, g))
    ecs = pl.BlockSpec((LANES, M2_GB * M2_HPG * LANES), lambda c, g: (0, g))
    st = pl.BlockSpec((None, M2_GB, M2_STATE, M2_GW), lambda c, g: (cm(c), g, 0, 0))
    return wide, nar, dts, v128, v256, es, ecs, st


def _m2_fwd(xbc, p, dtb, alog, dsk, ng, e, ecol):
    L = xbc.shape[0]
    nc = L // CHUNK
    wide, nar, dts, v128, v256, es, ecs, st = _m2_specs(nc, False)

    def body(x_ref, b_ref, c_ref, z_ref, dt_ref, dtb_ref, al_ref, dsk_ref, ng_ref, e_ref, ec_ref, y_ref, sp_ref, s_scr):
        c, g = pl.program_id(0), pl.program_id(1)
        wide_l = [slice(i * M2_GW, (i + 1) * M2_GW) for i in range(M2_GB)]
        nar_l = [slice(i * LANES, (i + 1) * LANES) for i in range(M2_GB)]
        groups = pl.ds(g * M2_GB, M2_GB)
        wide_s = lambda ref: jnp.concatenate([ref[:, ls][None] for ls in wide_l], axis=0)
        nar_s = lambda ref: jnp.concatenate([ref[:, ls][None] for ls in nar_l], axis=0)

        @pl.when(c == 0)
        def _():
            s_scr[groups] = jnp.zeros((M2_GB, M2_STATE, M2_GW), F32)

        S = s_scr[groups]
        sp_ref[...] = S
        y, s_new = _m2_chunk(wide_s(x_ref), nar_s(b_ref), nar_s(c_ref), wide_s(z_ref), dt_ref[...], S, dtb_ref[...], al_ref[...],
                             wide_s(dsk_ref), wide_s(ng_ref), e_ref[...], ec_ref[...])
        for i, ls in enumerate(wide_l):
            y_ref[:, ls] = y[i]
        s_scr[groups] = s_new

    return pl.pallas_call(
        body, name="m2_fwd", grid=(nc, M2_GROUPS // M2_GB),
        in_specs=[wide(0), nar(2 * M2_GROUPS), nar(3 * M2_GROUPS), wide(0), dts, v128, v128, v256, v256, es, ecs],
        out_specs=[wide(0), st],
        out_shape=[jax.ShapeDtypeStruct((L, M2_INNER), F32), jax.ShapeDtypeStruct((nc, M2_GROUPS, M2_STATE, M2_GW), F32)],
        scratch_shapes=[pltpu.VMEM((M2_GROUPS, M2_STATE, M2_GW), F32)],
        compiler_params=_params("arbitrary", "arbitrary"),
    )(xbc, xbc, xbc, p, p, dtb, alog, dsk, ng, e, ecol)


def _m2_bwd(xbc, p, dtb, alog, dsk, ng, e, ecol, sprev, dy):
    L = xbc.shape[0]
    nc = L // CHUNK
    wide, nar, dts, v128, v256, es, ecs, st = _m2_specs(nc, True)

    def body(x_ref, b_ref, c_ref, z_ref, dt_ref, dtb_ref, al_ref, dsk_ref, ng_ref, e_ref, ec_ref, sp_ref, dy_ref,
             dx_ref, db_ref, dc_ref, dz_ref, ddt_ref, dnar_ref, dwide_ref, ds_scr):
        c, g = pl.program_id(0), pl.program_id(1)
        wide_l = [slice(i * M2_GW, (i + 1) * M2_GW) for i in range(M2_GB)]
        nar_l = [slice(i * LANES, (i + 1) * LANES) for i in range(M2_GB)]
        groups = pl.ds(g * M2_GB, M2_GB)
        wide_s = lambda ref: jnp.concatenate([ref[:, ls][None] for ls in wide_l], axis=0)
        nar_s = lambda ref: jnp.concatenate([ref[:, ls][None] for ls in nar_l], axis=0)

        @pl.when(jnp.logical_and(c == 0, g == 0))
        def _():
            dnar_ref[...] = jnp.zeros_like(dnar_ref)

        @pl.when(c == 0)
        def _():
            ds_scr[groups] = jnp.zeros((M2_GB, M2_STATE, M2_GW), F32)
            dwide_ref[groups] = jnp.zeros((M2_GB, SUBLANES, M2_GW), F32)

        @pl.when(g == 0)
        def _():
            ddt_ref[...] = jnp.zeros_like(ddt_ref)

        e_m, ec_m = e_ref[...], ec_ref[...]
        f = lambda x, bm, cm, z, dtr, S, dtb, al, dsk, ng: _m2_chunk(x, bm, cm, z, dtr, S, dtb, al, dsk, ng, e_m, ec_m)
        _, vjp = jax.vjp(f, wide_s(x_ref), nar_s(b_ref), nar_s(c_ref), wide_s(z_ref), dt_ref[...], sp_ref[...], dtb_ref[...],
                         al_ref[...], wide_s(dsk_ref), wide_s(ng_ref))
        dx, db, dc, dz, ddt, ds, ddtb, dal, ddsk, dng = vjp((wide_s(dy_ref), ds_scr[groups]))
        for i in range(M2_GB):
            dx_ref[:, wide_l[i]] = dx[i]
            db_ref[:, nar_l[i]] = db[i]
            dc_ref[:, nar_l[i]] = dc[i]
            dz_ref[:, wide_l[i]] = dz[i]
        ds_scr[groups] = ds
        ddt_ref[...] += ddt
        dnar_ref[...] += jnp.concatenate([ddtb, dal, jnp.zeros((SUBLANES - 2, LANES), F32)], axis=0)
        dwide_ref[groups] += jnp.concatenate([ddsk, dng, jnp.zeros((M2_GB, SUBLANES - 2, M2_GW), F32)], axis=1)

    return pl.pallas_call(
        body, name="m2_bwd", grid=(nc, M2_GROUPS // M2_GB),
        in_specs=[wide(0), nar(2 * M2_GROUPS), nar(3 * M2_GROUPS), wide(0), dts, v128, v128, v256, v256, es, ecs, st, wide(0)],
        out_specs=[wide(0), nar(0), nar(0), wide(0), pl.BlockSpec((CHUNK, LANES), lambda c, g: (nc - 1 - c, 0)),
                   pl.BlockSpec((SUBLANES, LANES), lambda c, g: (0, 0)),
                   pl.BlockSpec((M2_GROUPS, SUBLANES, M2_GW), lambda c, g: (0, 0, 0))],
        out_shape=[jax.ShapeDtypeStruct((L, M2_INNER), F32), jax.ShapeDtypeStruct((L, M2_GROUPS * M2_STATE), F32),
                   jax.ShapeDtypeStruct((L, M2_GROUPS * M2_STATE), F32), jax.ShapeDtypeStruct((L, M2_INNER), F32),
                   jax.ShapeDtypeStruct((L, LANES), F32), jax.ShapeDtypeStruct((SUBLANES, LANES), F32),
                   jax.ShapeDtypeStruct((M2_GROUPS, SUBLANES, M2_GW), F32)],
        scratch_shapes=[pltpu.VMEM((M2_GROUPS, M2_STATE, M2_GW), F32)],
        compiler_params=_params("arbitrary", "arbitrary"),
    )(xbc, xbc, xbc, p, p, dtb, alog, dsk, ng, e, ecol, sprev, dy)


def _m2_selectors():
    e = np.zeros((LANES, M2_INNER), np.float32)
    ecol = np.zeros((LANES, M2_HEADS * LANES), np.float32)
    for h in range(M2_HEADS):
        e[h, M2_HD * h:M2_HD * (h + 1)] = 1.0
        ecol[h, LANES * h:LANES * (h + 1)] = 1.0
    return jnp.asarray(e), jnp.asarray(ecol)


S5_NS = S5_GROUPS * S5_STATE // S5_BLOCKS
S5_ROWS = 256
GELU_C = math.sqrt(2.0 / math.pi)


def _gelu(x):
    return 0.5 * x * (1.0 + jnp.tanh(GELU_C * (x + 0.044715 * x * x * x)))


def _gelu_grad(x):
    t = jnp.tanh(GELU_C * (x + 0.044715 * x * x * x))
    return 0.5 * (1.0 + t) + 0.5 * x * (1.0 - t * t) * GELU_C * (1.0 + 3.0 * 0.044715 * x * x)


def _s5_scan(re_ref, im_ref, pw_re, pw_im, nrows, reverse):
    n = re_ref.shape[1]
    row = _iota2((SUBLANES, n), 0)
    steps = [(d, pw_re[d - 1:d, :], pw_im[d - 1:d, :]) for d in (1, 2, 4)]
    if reverse:
        cw_re = jnp.concatenate([pw_re[SUBLANES - 1 - k:SUBLANES - k, :] for k in range(SUBLANES)], axis=0)
        cw_im = jnp.concatenate([pw_im[SUBLANES - 1 - k:SUBLANES - k, :] for k in range(SUBLANES)], axis=0)
    else:
        cw_re, cw_im = pw_re, pw_im
    edge = 0 if reverse else SUBLANES - 1
    ngroups = nrows // SUBLANES

    def step(i, carry):
        cr, ci = carry
        gi = (ngroups - 1 - i) if reverse else i
        r0 = pl.multiple_of(gi * SUBLANES, SUBLANES)
        xr, xi = re_ref[pl.ds(r0, SUBLANES), :], im_ref[pl.ds(r0, SUBLANES), :]
        for d, pr, pi in steps:
            if reverse:
                sr = jnp.where(row < SUBLANES - d, pltpu.roll(xr, SUBLANES - d, axis=0), 0.0)
                si = jnp.where(row < SUBLANES - d, pltpu.roll(xi, SUBLANES - d, axis=0), 0.0)
            else:
                sr = jnp.where(row >= d, pltpu.roll(xr, d, axis=0), 0.0)
                si = jnp.where(row >= d, pltpu.roll(xi, d, axis=0), 0.0)
            xr, xi = xr + (pr * sr - pi * si), xi + (pr * si + pi * sr)
        xr, xi = xr + (cw_re * cr - cw_im * ci), xi + (cw_re * ci + cw_im * cr)
        re_ref[pl.ds(r0, SUBLANES), :] = xr
        im_ref[pl.ds(r0, SUBLANES), :] = xi
        return (jnp.sum(jnp.where(row == edge, xr, 0.0), axis=0, keepdims=True),
                jnp.sum(jnp.where(row == edge, xi, 0.0), axis=0, keepdims=True))

    z = jnp.zeros((1, n), F32)
    lax.fori_loop(0, ngroups, step, (z, z))


def _s5_project_in(u_ref, bm_ref, re_ref, im_ref, L):
    def step(i, carry):
        r0 = pl.multiple_of(i * S5_ROWS, S5_ROWS)
        bu = _dot(u_ref[pl.ds(r0, S5_ROWS), :], bm_ref[...])
        re_ref[pl.ds(r0, S5_ROWS), :] = bu[:, :S5_NS]
        im_ref[pl.ds(r0, S5_ROWS), :] = bu[:, S5_NS:]
        return carry

    lax.fori_loop(0, L // S5_ROWS, step, 0)


def _s5_specs(L):
    col = pl.BlockSpec((L, LANES), lambda j: (0, j))
    bm = pl.BlockSpec((None, LANES, 2 * S5_NS), lambda j: (j, 0, 0))
    cm = pl.BlockSpec((None, 2 * S5_NS, LANES), lambda j: (j, 0, 0))
    pw = pl.BlockSpec((None, SUBLANES, S5_NS), lambda j: (j, 0, 0))
    vec = pl.BlockSpec((1, LANES), lambda j: (0, j))
    return col, bm, cm, pw, vec


def _s5_fwd(u, bmat, cmat, pw_re, pw_im, dsk):
    L = u.shape[0]
    col, bm, cm, pw, vec = _s5_specs(L)

    def body(u_ref, bm_ref, cm_ref, pr_ref, pi_ref, d_ref, y_ref, re_scr, im_scr):
        _s5_project_in(u_ref, bm_ref, re_scr, im_scr, L)
        _s5_scan(re_scr, im_scr, pr_ref[...], pi_ref[...], L, False)

        def step(i, carry):
            r0 = pl.multiple_of(i * S5_ROWS, S5_ROWS)
            rows = pl.ds(r0, S5_ROWS)
            y = _dot(re_scr[rows, :], cm_ref[:S5_NS, :]) + _dot(im_scr[rows, :], cm_ref[S5_NS:, :]) + d_ref[...] * u_ref[rows, :]
            y_ref[rows, :] = _gelu(y)
            return carry

        lax.fori_loop(0, L // S5_ROWS, step, 0)

    return pl.pallas_call(
        body, name="s5_fwd", grid=(S5_BLOCKS,),
        in_specs=[col, bm, cm, pw, pw, vec], out_specs=col,
        out_shape=jax.ShapeDtypeStruct((L, D_MODEL), F32),
        scratch_shapes=[pltpu.VMEM((L, S5_NS), F32)] * 2,
        compiler_params=_params("parallel"),
    )(u, bmat, cmat, pw_re, pw_im, dsk)


def _s5_bwd(u, bmat, cmat, pw_re, pw_im, dsk, dyg):
    L = u.shape[0]
    col, bm, cm, pw, vec = _s5_specs(L)

    def body(u_ref, bm_ref, cm_ref, pr_ref, pi_ref, d_ref, dy_ref, du_ref, dbm_ref, dcm_ref, dlam_ref, dd_ref,
             re_scr, im_scr, gr_scr, gi_scr, dyp_scr):
        _s5_project_in(u_ref, bm_ref, re_scr, im_scr, L)
        _s5_scan(re_scr, im_scr, pr_ref[...], pi_ref[...], L, False)

        def step(i, carry):
            dcr, dci, dd = carry
            r0 = pl.multiple_of(i * S5_ROWS, S5_ROWS)
            rows = pl.ds(r0, S5_ROWS)
            sr, si, uu = re_scr[rows, :], im_scr[rows, :], u_ref[rows, :]
            y = _dot(sr, cm_ref[:S5_NS, :]) + _dot(si, cm_ref[S5_NS:, :]) + d_ref[...] * uu
            dyp = dy_ref[rows, :] * _gelu_grad(y)
            dyp_scr[rows, :] = dyp
            gr_scr[rows, :] = _nt(dyp, cm_ref[:S5_NS, :])
            gi_scr[rows, :] = _nt(dyp, cm_ref[S5_NS:, :])
            return dcr + _tn(sr, dyp), dci + _tn(si, dyp), dd + jnp.sum(dyp * uu, axis=0, keepdims=True)

        zc = jnp.zeros((S5_NS, LANES), F32)
        dcr, dci, dd = lax.fori_loop(0, L // S5_ROWS, step, (zc, zc, jnp.zeros((1, LANES), F32)))
        dcm_ref[:S5_NS, :] = dcr
        dcm_ref[S5_NS:, :] = dci
        dd_ref[...] = dd

        _s5_scan(gr_scr, gi_scr, pr_ref[...], -pi_ref[...], L, True)

        row = _iota2((SUBLANES, S5_NS), 0)

        def lam_step(i, carry):
            ar, ai, pr, pi = carry
            r0 = pl.multiple_of(i * SUBLANES, SUBLANES)
            rows = pl.ds(r0, SUBLANES)
            sr, si = re_scr[rows, :], im_scr[rows, :]
            spr = jnp.where(row >= 1, pltpu.roll(sr, 1, axis=0), pr)
            spi = jnp.where(row >= 1, pltpu.roll(si, 1, axis=0), pi)
            gr, gi = gr_scr[rows, :], gi_scr[rows, :]
            ar = ar + jnp.sum(spr * gr + spi * gi, axis=0, keepdims=True)
            ai = ai + jnp.sum(spr * gi - spi * gr, axis=0, keepdims=True)
            last = row == SUBLANES - 1
            return (ar, ai, jnp.sum(jnp.where(last, sr, 0.0), axis=0, keepdims=True),
                    jnp.sum(jnp.where(last, si, 0.0), axis=0, keepdims=True))

        z = jnp.zeros((1, S5_NS), F32)
        ar, ai, _, _ = lax.fori_loop(0, L // SUBLANES, lam_step, (z, z, z, z))
        dlam_ref[...] = jnp.concatenate([ar, ai, jnp.zeros((SUBLANES - 2, S5_NS), F32)], axis=0)

        def in_step(i, carry):
            dbr, dbi = carry
            r0 = pl.multiple_of(i * S5_ROWS, S5_ROWS)
            rows = pl.ds(r0, S5_ROWS)
            gr, gi, uu = gr_scr[rows, :], gi_scr[rows, :], u_ref[rows, :]
            du_ref[rows, :] = dyp_scr[rows, :] * d_ref[...] + _nt(gr, bm_ref[:, :S5_NS]) + _nt(gi, bm_ref[:, S5_NS:])
            return dbr + _tn(uu, gr), dbi + _tn(uu, gi)

        zb = jnp.zeros((LANES, S5_NS), F32)
        dbr, dbi = lax.fori_loop(0, L // S5_ROWS, in_step, (zb, zb))
        dbm_ref[:, :S5_NS] = dbr
        dbm_ref[:, S5_NS:] = dbi

    return pl.pallas_call(
        body, name="s5_bwd", grid=(S5_BLOCKS,),
        in_specs=[col, bm, cm, pw, pw, vec, col], out_specs=[col, bm, cm, pw, vec],
        out_shape=[jax.ShapeDtypeStruct((L, D_MODEL), F32), jax.ShapeDtypeStruct((S5_BLOCKS, LANES, 2 * S5_NS), F32),
                   jax.ShapeDtypeStruct((S5_BLOCKS, 2 * S5_NS, LANES), F32),
                   jax.ShapeDtypeStruct((S5_BLOCKS, SUBLANES, S5_NS), F32), jax.ShapeDtypeStruct((1, D_MODEL), F32)],
        scratch_shapes=[pltpu.VMEM((L, S5_NS), F32)] * 4 + [pltpu.VMEM((L, LANES), F32)],
        compiler_params=_params("parallel"),
    )(u, bmat, cmat, pw_re, pw_im, dsk, dyg)


def _s5_discretize(lam_re, lam_im, log_dt, b_re, b_im, e16):
    dt = jnp.exp(log_dt)
    zr, zi = lam_re * dt, lam_im * dt
    mag = jnp.exp(zr)
    lbr, lbi = mag * jnp.cos(zi), mag * jnp.sin(zi)
    den = lam_re * lam_re + lam_im * lam_im
    nr, ni = lbr - 1.0, lbi
    cr = (nr * lam_re + ni * lam_im) / den
    ci = (ni * lam_re - nr * lam_im) / den
    crw, ciw = _pick(cr, e16), _pick(ci, e16)
    return lbr, lbi, crw * b_re - ciw * b_im, crw * b_im + ciw * b_re


def _s5_params_fwd(lam_re, lam_im, log_dt, b_re, b_im, e16):
    def body(lr, li, ld, br, bi, e, o1, o2, o3, o4):
        for o, val in zip((o1, o2, o3, o4), _s5_discretize(lr[...], li[...], ld[...], br[...], bi[...], e[...])):
            o[...] = val

    g, p, n = S5_GROUPS, S5_STATE, S5_STATE * S5_GROUP
    return pl.pallas_call(
        body, name="s5_params_fwd",
        out_shape=[jax.ShapeDtypeStruct((g, p), F32)] * 2 + [jax.ShapeDtypeStruct((g, n), F32)] * 2,
        compiler_params=_params(),
    )(lam_re, lam_im, log_dt, b_re, b_im, e16)


def _s5_params_bwd(lam_re, lam_im, log_dt, b_re, b_im, e16, cts):
    def body(lr, li, ld, br, bi, e, c1, c2, c3, c4, o1, o2, o3, o4, o5):
        e_m = e[...]
        f = lambda a, b, c, d, g: _s5_discretize(a, b, c, d, g, e_m)
        _, vjp = jax.vjp(f, lr[...], li[...], ld[...], br[...], bi[...])
        for o, val in zip((o1, o2, o3, o4, o5), vjp((c1[...], c2[...], c3[...], c4[...]))):
            o[...] = val

    g, p, n = S5_GROUPS, S5_STATE, S5_STATE * S5_GROUP
    return pl.pallas_call(
        body, name="s5_params_bwd",
        out_shape=[jax.ShapeDtypeStruct((g, p), F32)] * 2 + [jax.ShapeDtypeStruct((g, 1), F32)]
        + [jax.ShapeDtypeStruct((g, n), F32)] * 2,
        compiler_params=_params(),
    )(lam_re, lam_im, log_dt, b_re, b_im, e16, *cts)


def _add_residual(acc, h):
    return (acc + h,)


def _mlp_fwd(i, h, g, w1, w2):
    hn = _rms_fwd(f"mlp{i}_norm", h, g)
    r = _mm(f"mlp{i}_up", hn, w1, "nn", (BF16,), epi=lambda acc: (jnp.square(jnp.maximum(acc, 0.0)),))
    return _mm(f"mlp{i}_down", r, w2, "nn", (F32,), epi=_add_residual, extras=(h,)), (h, hn, r)


def _mlp_bwd(i, dh_out, saved, g, w1, w2):
    h, hn, r = saved
    dw2 = _mm(f"mlp{i}_dw2", r, dh_out, "tn", (BF16,))
    da = _mm(f"mlp{i}_da", dh_out, w2, "nt", (BF16,), epi=lambda acc, rr: (acc * (2.0 * jnp.sqrt(rr.astype(F32))),), extras=(r,))
    dw1 = _mm(f"mlp{i}_dw1", hn, da, "tn", (BF16,))
    dhn = _mm(f"mlp{i}_dhn", da, w1, "nt", (F32,))
    dh, dg = _rms_bwd(f"mlp{i}_dnorm", h, g, dhn, dh_out)
    return dh, dg[0], dw1, dw2


def _lanes(v, n):
    return jnp.broadcast_to(v.reshape(n, 1, 1), (n, 1, LANES))


def _gdn_fwd_layer(i, h, g, w_in, conv_w, a_log, dt_bias, o_g, w_out):
    hn = _rms_fwd(f"gdn{i}_norm", h, g)
    p = _mm(f"gdn{i}_in", hn, w_in, "nn", (F32,))
    zb = jnp.zeros((1, D_MODEL), F32)
    qkv = [_conv_fwd(f"gdn{i}_conv{t}", p, t * D_MODEL, conv_w[:, t * D_MODEL:(t + 1) * D_MODEL], zb) for t in range(3)]
    ea, eb = _gdn_selectors()
    y, sprev = _gdn_fwd(*qkv, p, _lanes(a_log, GDN_HEADS), _lanes(dt_bias, GDN_HEADS), o_g.reshape(1, LANES), ea, eb)
    return _mm(f"gdn{i}_out", y, w_out, "nn", (F32,), epi=_add_residual, extras=(h,)), (h, hn, p, qkv, y, sprev)


def _gdn_bwd_layer(i, dh_out, saved, g, w_in, conv_w, a_log, dt_bias, o_g, w_out):
    h, hn, p, qkv, y, sprev = saved
    dy = _mm(f"gdn{i}_dy", dh_out, w_out, "nt", (F32,))
    dw_out = _mm(f"gdn{i}_dwout", y, dh_out, "tn", (BF16,))
    ea, eb = _gdn_selectors()
    dq, dk, dv, dgate, dab, dpar = _gdn_bwd(*qkv, p, _lanes(a_log, GDN_HEADS), _lanes(dt_bias, GDN_HEADS),
                                            o_g.reshape(1, LANES), ea, eb, sprev, dy)
    zb = jnp.zeros((1, D_MODEL), F32)
    dpre, dcw = [], []
    for t, d in enumerate((dq, dk, dv)):
        dx, dw, _ = _conv_bwd(f"gdn{i}_dconv{t}", p, t * D_MODEL, conv_w[:, t * D_MODEL:(t + 1) * D_MODEL], zb, d)
        dpre.append(dx)
        dcw.append(dw)
    dp = jnp.concatenate(dpre + [dgate, dab], axis=1).astype(BF16)
    dw_in = _mm(f"gdn{i}_dwin", hn, dp, "tn", (BF16,))[:, :GDN_IN]
    dhn = _mm(f"gdn{i}_dhn", dp, w_in, "nt", (F32,))
    dh, dg = _rms_bwd(f"gdn{i}_dnorm", h, g, dhn, dh_out)
    grads = dict(w_in=dw_in, conv_w=jnp.concatenate(dcw, axis=1), a_log=jnp.sum(dpar[:, 0, :], axis=-1),
                 dt_bias=jnp.sum(dpar[:, 1, :], axis=-1), o_norm_g=jnp.sum(dpar[:, 2, :], axis=0), w_out=dw_out)
    return dh, dg[0], grads


def _m2_vectors(dt_bias, a_log, d_skip, norm_g):
    pad = lambda v: jnp.pad(v, (0, LANES - M2_HEADS)).reshape(1, LANES)
    return pad(dt_bias), pad(a_log), jnp.repeat(d_skip, M2_HD).reshape(1, M2_INNER), norm_g.reshape(1, M2_INNER)


def _m2_fwd_layer(h, g, w_in, conv_w, conv_b, dt_bias, a_log, d_skip, norm_g, w_out):
    hn = _rms_fwd("m2_norm", h, g)
    p = _mm("m2_in", hn, w_in, "nn", (F32,))
    xbc = _conv_fwd("m2_conv", p, M2_INNER, conv_w, conv_b.reshape(1, M2_CONV_CH))
    e, ecol = _m2_selectors()
    y, sprev = _m2_fwd(xbc, p, *_m2_vectors(dt_bias, a_log, d_skip, norm_g), e, ecol)
    return _mm("m2_out", y, w_out, "nn", (F32,), epi=_add_residual, extras=(h,)), (h, hn, p, xbc, y, sprev)


def _m2_bwd_layer(dh_out, saved, g, w_in, conv_w, conv_b, dt_bias, a_log, d_skip, norm_g, w_out):
    h, hn, p, xbc, y, sprev = saved
    dy = _mm("m2_dy", dh_out, w_out, "nt", (F32,))
    dw_out = _mm("m2_dwout", y, dh_out, "tn", (BF16,))
    e, ecol = _m2_selectors()
    dx, db, dc, dz, ddt, dnar, dwide = _m2_bwd(xbc, p, *_m2_vectors(dt_bias, a_log, d_skip, norm_g), e, ecol, sprev, dy)
    dxbc, dcw, dcb = _conv_bwd("m2_dconv", p, M2_INNER, conv_w, conv_b.reshape(1, M2_CONV_CH),
                               jnp.concatenate([dx, db, dc], axis=1))
    dp = jnp.concatenate([dz, dxbc, ddt], axis=1).astype(BF16)
    dw_in = _mm("m2_dwin", hn, dp, "tn", (BF16,))[:, :M2_IN]
    dhn = _mm("m2_dhn", dp, w_in, "nt", (F32,))
    dh, dg = _rms_bwd("m2_dnorm", h, g, dhn, dh_out)
    grads = dict(w_in=dw_in, conv_w=dcw, conv_b=dcb[0], dt_bias=dnar[0, :M2_HEADS], a_log=dnar[1, :M2_HEADS],
                 d=jnp.sum(dwide[:, 0, :].reshape(M2_HEADS, M2_HD), axis=-1), norm_g=dwide[:, 1, :].reshape(M2_INNER),
                 w_out=dw_out)
    return dh, dg[0], grads


def _s5_selector():
    e16 = np.zeros((S5_STATE, S5_STATE * S5_GROUP), np.float32)
    for p in range(S5_STATE):
        e16[p, p * S5_GROUP:(p + 1) * S5_GROUP] = 1.0
    return jnp.asarray(e16)


def _s5_operands(lbr, lbi, bbr, bbi, c_re, c_im):
    eye = jnp.eye(S5_BLOCKS, dtype=F32)
    gpb = S5_GROUPS // S5_BLOCKS
    bd = lambda t: jnp.einsum("jgpk,gh->jgkhp", t.reshape(S5_BLOCKS, gpb, S5_STATE, S5_GROUP), eye).reshape(S5_BLOCKS, LANES, S5_NS)
    cd = lambda t: jnp.einsum("jgkp,gh->jgphk", t.reshape(S5_BLOCKS, gpb, S5_GROUP, S5_STATE), eye).reshape(S5_BLOCKS, S5_NS, LANES)
    bmat = jnp.concatenate([bd(bbr), bd(bbi)], axis=2).astype(BF16)
    cmat = jnp.concatenate([cd(c_re), -cd(c_im)], axis=1).astype(BF16)
    ar, ai = lbr.reshape(S5_BLOCKS, S5_NS), lbi.reshape(S5_BLOCKS, S5_NS)
    pr, pi = [ar], [ai]
    for _ in range(SUBLANES - 1):
        pr, pi = pr + [pr[-1] * ar - pi[-1] * ai], pi + [pr[-1] * ai + pi[-1] * ar]
    return bmat, cmat, jnp.stack(pr, axis=1), jnp.stack(pi, axis=1)


def _s5_fwd_layer(h, g, w_in, lam_re, lam_im, log_dt, b_re, b_im, c_re, c_im, d_skip, w_out):
    hn = _rms_fwd("s5_norm", h, g)
    u = _mm("s5_in", hn, w_in, "nn", (F32,))
    n = S5_STATE * S5_GROUP
    lbr, lbi, bbr, bbi = _s5_params_fwd(lam_re, lam_im, log_dt.reshape(S5_GROUPS, 1), b_re.reshape(S5_GROUPS, n),
                                        b_im.reshape(S5_GROUPS, n), _s5_selector())
    ops = _s5_operands(lbr, lbi, bbr, bbi, c_re, c_im)
    yg = _s5_fwd(u, *ops, d_skip.reshape(1, D_MODEL))
    ag = _mm("s5_out", yg, w_out, "nn", (F32,))
    return _glu_fwd(h, ag), (h, hn, u, ops, yg, ag)


def _s5_bwd_layer(dh_out, saved, g, w_in, lam_re, lam_im, log_dt, b_re, b_im, c_re, c_im, d_skip, w_out):
    h, hn, u, ops, yg, ag = saved
    dag = _glu_bwd(dh_out, ag)
    dw_out = _mm("s5_dwout", yg, dag, "tn", (BF16,))
    dyg = _mm("s5_dyg", dag, w_out, "nt", (F32,))
    du, dbmat, dcmat, dlam, ddsk = _s5_bwd(u, *ops, d_skip.reshape(1, D_MODEL), dyg)
    eye = jnp.eye(S5_BLOCKS, dtype=F32)
    gpb = S5_GROUPS // S5_BLOCKS
    n = S5_STATE * S5_GROUP
    ub = lambda t: jnp.einsum("jgkhp,gh->jgpk", t.reshape(S5_BLOCKS, gpb, S5_GROUP, gpb, S5_STATE), eye).reshape(S5_GROUPS, n)
    uc = lambda t: jnp.einsum("jgphk,gh->jgkp", t.reshape(S5_BLOCKS, gpb, S5_STATE, gpb, S5_GROUP), eye).reshape(c_re.shape)
    cts = (dlam[:, 0, :].reshape(S5_GROUPS, S5_STATE), dlam[:, 1, :].reshape(S5_GROUPS, S5_STATE),
           ub(dbmat[:, :, :S5_NS]), ub(dbmat[:, :, S5_NS:]))
    dlr, dli, dld, dbr, dbi = _s5_params_bwd(lam_re, lam_im, log_dt.reshape(S5_GROUPS, 1), b_re.reshape(S5_GROUPS, n),
                                             b_im.reshape(S5_GROUPS, n), _s5_selector(), cts)
    dw_in = _mm("s5_dwin", hn, du, "tn", (BF16,))
    dhn = _mm("s5_dhn", du, w_in, "nt", (F32,))
    dh, dg = _rms_bwd("s5_dnorm", h, g, dhn, dh_out)
    grads = dict(w_in=dw_in, lam_re=dlr, lam_im=dli, log_dt=dld[:, 0], b_re=dbr.reshape(b_re.shape), b_im=dbi.reshape(b_im.shape),
                 c_re=uc(dcmat[:, :S5_NS, :]), c_im=-uc(dcmat[:, S5_NS:, :]), d=ddsk[0], w_out=dw_out)
    return dh, dg[0], grads


MIXER_OF_LAYER = ("gdn", "s5", "m2", "gdn")
MIXER_INDEX = (0, 0, 0, 1)


def _mixer_args(W, i):
    kind, j = MIXER_OF_LAYER[i], MIXER_INDEX[i]
    if kind == "gdn":
        return tuple(W["gdn_" + k][j] for k in ("w_in", "conv_w", "a_log", "dt_bias", "o_norm_g", "w_out"))
    if kind == "s5":
        return tuple(W["s5_" + k][j] for k in ("w_in", "lam_re", "lam_im", "log_dt", "b_re", "b_im", "c_re", "c_im", "d", "w_out"))
    return tuple(W["m2_" + k][j] for k in ("w_in", "conv_w", "conv_b", "dt_bias", "a_log", "d", "norm_g", "w_out"))


def _local_step(x, target, W, on_layer_grads):
    h = x
    saved = []
    for i in range(DEPTH):
        kind = MIXER_OF_LAYER[i]
        args = _mixer_args(W, i)
        if kind == "gdn":
            h, sm = _gdn_fwd_layer(i, h, W["norm_mix_g"][i], *args)
        elif kind == "s5":
            h, sm = _s5_fwd_layer(h, W["norm_mix_g"][i], *args)
        else:
            h, sm = _m2_fwd_layer(h, W["norm_mix_g"][i], *args)
        h, sp = _mlp_fwd(i, h, W["norm_mlp_g"][i], W["mlp_w1"][i], W["mlp_w2"][i])
        saved.append((sm, sp))
    loss, dh, dgf = _loss_head(h, W["final_norm_g"], target)
    G = {"final_norm_g": dgf[0], "norm_mix_g": [None] * DEPTH, "norm_mlp_g": [None] * DEPTH,
         "mlp_w1": [None] * DEPTH, "mlp_w2": [None] * DEPTH}
    mix = {}
    for i in reversed(range(DEPTH)):
        kind = MIXER_OF_LAYER[i]
        sm, sp = saved[i]
        dh, G["norm_mlp_g"][i], G["mlp_w1"][i], G["mlp_w2"][i] = _mlp_bwd(i, dh, sp, W["norm_mlp_g"][i], W["mlp_w1"][i], W["mlp_w2"][i])
        args = _mixer_args(W, i)
        if kind == "gdn":
            dh, G["norm_mix_g"][i], gm = _gdn_bwd_layer(i, dh, sm, W["norm_mix_g"][i], *args)
        elif kind == "s5":
            dh, G["norm_mix_g"][i], gm = _s5_bwd_layer(dh, sm, W["norm_mix_g"][i], *args)
        else:
            dh, G["norm_mix_g"][i], gm = _m2_bwd_layer(dh, sm, W["norm_mix_g"][i], *args)
        j = MIXER_INDEX[i]
        on_layer_grads(i, {("mlp_w1", i): G["mlp_w1"][i], ("mlp_w2", i): G["mlp_w2"][i],
                           (kind + "_w_in", j): gm["w_in"], (kind + "_w_out", j): gm["w_out"]})
        for k, v in gm.items():
            mix.setdefault(kind + "_" + k, {})[j] = v
    for k, d in mix.items():
        G[k] = [d[j] for j in sorted(d)]
    return loss, dh, {k: jnp.stack(v) if isinstance(v, list) else v for k, v in G.items() if k not in BIG}


ADAM_ROWS = 128


def _adamw(name, w, g, m, v):
    R, C = w.shape
    tr = _tile(R, (ADAM_ROWS, SUBLANES))

    def body(w_ref, g_ref, m_ref, v_ref, d_ref, mo_ref, vo_ref):
        gg = g_ref[...]
        mn = ADAM_B1 * m_ref[...] + (1.0 - ADAM_B1) * gg
        vn = ADAM_B2 * v_ref[...] + (1.0 - ADAM_B2) * (gg * gg)
        m_hat = mn / (1.0 - ADAM_B1 ** ADAM_STEP)
        v_hat = vn / (1.0 - ADAM_B2 ** ADAM_STEP)
        d_ref[...] = -ADAM_LR * (m_hat / (jnp.sqrt(v_hat) + ADAM_EPS) + ADAM_WD * w_ref[...])
        mo_ref[...] = mn
        vo_ref[...] = vn

    blk = pl.BlockSpec((tr, C), lambda i: (i, 0))
    return pl.pallas_call(
        body, name=name, grid=(R // tr,), in_specs=[blk] * 4, out_specs=[blk] * 3,
        out_shape=[jax.ShapeDtypeStruct((R, C), F32)] * 3, compiler_params=_params("parallel"),
    )(w, g, m, v)


MESH = pl.DeviceIdType.MESH
ANY = pl.BlockSpec(memory_space=pl.ANY)
N_CHIPS = 4
N_DEV = 8


def _position():
    return lax.axis_index("x"), lax.axis_index("y"), lax.axis_index("c")


GATHER_ID = 1
EXCHANGE_IDS = {0: 4, 1: 5, 2: 6, 3: 7}


LINK_SLOWDOWN = 40


def _link_cost(link_bytes):
    return pl.CostEstimate(flops=0, transcendentals=0, bytes_accessed=LINK_SLOWDOWN * link_bytes)


def _gather_body(w_refs, out_refs, send_sems, recv_sems):
    x, y, c = _position()
    sibling = (x, y, 1 - c)
    chips = [(1 - x, y), (x, 1 - y), (1 - x, 1 - y)]
    firsts, passes = [], []
    for t, (w_ref, out_ref) in enumerate(zip(w_refs, out_refs)):
        half = w_ref.shape[0] // 2

        def piece(cx, cy, hc, out_ref=out_ref, half=half):
            return out_ref.at[2 * cx + cy, pl.ds(hc * half, half), :]

        def copy(k, src, dst, to, t=t):
            return pltpu.make_async_remote_copy(src_ref=src, dst_ref=dst, send_sem=send_sems.at[6 * t + k],
                                                recv_sem=recv_sems.at[6 * t + k], device_id=to, device_id_type=MESH)

        first = [copy(j, w_ref.at[pl.ds(c * half, half), :], piece(x, y, c), (*chip, c)) for j, chip in enumerate(chips)]
        for cp in first:
            cp.start()
        firsts.append((first, piece, copy))
    for first, piece, copy in firsts:
        passed = [copy(3 + j, piece(*chip, c), piece(*chip, c), sibling) for j, chip in enumerate(chips)]
        for j, chip in enumerate(chips):
            copy(j, piece(*chip, c), piece(*chip, c), sibling).wait_recv()
            passed[j].start()
        passes.append(passed)
    for (first, piece, copy), passed in zip(firsts, passes):
        for j, chip in enumerate(chips):
            copy(3 + j, piece(*chip, 1 - c), piece(*chip, 1 - c), sibling).wait_recv()
        for cp in first + passed:
            cp.wait_send()


def _gather_shards(wps):
    n = len(wps)

    def body(*refs):
        _gather_body(refs[:n], refs[n:2 * n], *refs[2 * n:])

    return pl.pallas_call(
        body, name="gather_shards", in_specs=[ANY] * n, out_specs=[ANY] * n,
        out_shape=[jax.ShapeDtypeStruct((N_CHIPS, *wp.shape), wp.dtype) for wp in wps],
        scratch_shapes=[pltpu.SemaphoreType.DMA((6 * n,)), pltpu.SemaphoreType.DMA((6 * n,))],
    )(*wps)


def _gather_shards_later(wps):
    n = len(wps)
    w_refs = [jax.new_ref(wp, memory_space=pltpu.MemorySpace.HBM) for wp in wps]
    out_refs = [jax.empty_ref(jax.ShapeDtypeStruct((N_CHIPS, *wp.shape), wp.dtype), memory_space=pltpu.MemorySpace.HBM)
                for wp in wps]

    @pl.kernel(mesh=plsc.ScalarSubcoreMesh(axis_name="sequencer", num_cores=1), name="gather_shards_later",
               scratch_types=(pltpu.SemaphoreType.DMA((6 * n,)), pltpu.SemaphoreType.DMA((6 * n,))),
               cost_estimate=_link_cost(3 * sum(wp.size * wp.dtype.itemsize for wp in wps)),
               compiler_params=pltpu.CompilerParams(collective_id=GATHER_ID))
    def launch(send_sems, recv_sems):
        x, y, c = _position()
        barrier = pltpu.get_barrier_semaphore()
        for peer in [(x, y, 1 - c), (1 - x, y, c), (x, 1 - y, c), (1 - x, 1 - y, c)]:
            pl.semaphore_signal(barrier, inc=1, device_id=peer, device_id_type=MESH)
        pl.semaphore_wait(barrier, 4)
        _gather_body(w_refs, out_refs, send_sems, recv_sems)

    launch()
    return [r[...] for r in out_refs]


def _pair_exchange(name, gps):
    n = len(gps)

    def body(*refs):
        g_refs, out_refs, (send_sems, recv_sems) = refs[:n], refs[n:2 * n], refs[2 * n:]
        x, y, c = _position()
        copies = []
        for t, (g_ref, out_ref) in enumerate(zip(g_refs, out_refs)):
            half = g_ref.shape[1] // 2
            copies += [pltpu.make_async_remote_copy(
                src_ref=g_ref.at[k, pl.ds((1 - c) * half, half), :], dst_ref=out_ref.at[k], send_sem=send_sems.at[N_CHIPS * t + k],
                recv_sem=recv_sems.at[N_CHIPS * t + k], device_id=(x, y, 1 - c), device_id_type=MESH) for k in range(N_CHIPS)]
        for cp in copies:
            cp.start()
        for cp in copies:
            cp.wait()

    return pl.pallas_call(
        body, name=name, in_specs=[ANY] * n, out_specs=[ANY] * n,
        out_shape=[jax.ShapeDtypeStruct((N_CHIPS, gp.shape[1] // 2, gp.shape[2]), gp.dtype) for gp in gps],
        scratch_shapes=[pltpu.SemaphoreType.DMA((N_CHIPS * n,)), pltpu.SemaphoreType.DMA((N_CHIPS * n,))],
    )(*gps)


SUM_ROWS = (256, 128)


def _pair_sum(name, gp, got, core):
    n, R, C = gp.shape
    half = R // 2
    tr = _tile(half, SUM_ROWS)
    nb = half // tr

    def body(core_ref, g_ref, r_ref, o_ref):
        o_ref[...] = (g_ref[...].astype(F32) + r_ref[...].astype(F32)).astype(o_ref.dtype)

    return pl.pallas_call(
        body, name=name,
        grid_spec=pltpu.PrefetchScalarGridSpec(
            num_scalar_prefetch=1, grid=(n, nb),
            in_specs=[pl.BlockSpec((None, tr, C), lambda k, i, core_ref: (k, core_ref[0] * nb + i, 0)),
                      pl.BlockSpec((None, tr, C), lambda k, i, core_ref: (k, i, 0))],
            out_specs=pl.BlockSpec((None, tr, C), lambda k, i, core_ref: (k, i, 0))),
        out_shape=jax.ShapeDtypeStruct((n, half, C), gp.dtype), compiler_params=_params("parallel", "parallel"),
    )(core, gp, got)


def _chip_exchange_body(t_refs, out_refs, send_sems, recv_sems):
    x, y, c = _position()
    chips = [(1 - x, y), (x, 1 - y), (1 - x, 1 - y)]
    copies, waits = [], []
    for t, (t_ref, out_ref) in enumerate(zip(t_refs, out_refs)):
        for j, (cx, cy) in enumerate(chips):
            sems = dict(send_sem=send_sems.at[3 * t + j], recv_sem=recv_sems.at[3 * t + j], device_id=(cx, cy, c),
                        device_id_type=MESH)
            copies.append(pltpu.make_async_remote_copy(src_ref=t_ref.at[2 * cx + cy], dst_ref=out_ref.at[2 * x + y], **sems))
            waits.append(pltpu.make_async_remote_copy(src_ref=t_ref.at[2 * cx + cy], dst_ref=out_ref.at[2 * cx + cy], **sems))
    for cp in copies:
        cp.start()
    for cp in waits:
        cp.wait_recv()
    for cp in copies:
        cp.wait_send()


def _chip_exchange_later(ts, layer):
    n = len(ts)
    t_refs = [jax.new_ref(t, memory_space=pltpu.MemorySpace.HBM) for t in ts]
    out_refs = [jax.empty_ref(jax.ShapeDtypeStruct(t.shape, t.dtype), memory_space=pltpu.MemorySpace.HBM) for t in ts]

    @pl.kernel(mesh=plsc.ScalarSubcoreMesh(axis_name="sequencer", num_cores=1), name=f"chip_exchange_later{layer}",
               scratch_types=(pltpu.SemaphoreType.DMA((3 * n,)), pltpu.SemaphoreType.DMA((3 * n,))),
               cost_estimate=_link_cost(3 * sum(t.size * t.dtype.itemsize for t in ts) // N_CHIPS),
               compiler_params=pltpu.CompilerParams(collective_id=EXCHANGE_IDS[layer]))
    def launch(send_sems, recv_sems):
        x, y, c = _position()
        barrier = pltpu.get_barrier_semaphore()
        for peer in [(1 - x, y, c), (x, 1 - y, c), (1 - x, 1 - y, c)]:
            pl.semaphore_signal(barrier, inc=1, device_id=peer, device_id_type=MESH)
        pl.semaphore_wait(barrier, 3)
        _chip_exchange_body(t_refs, out_refs, send_sems, recv_sems)

    launch()
    return [r[...] for r in out_refs]


def _chip_sum(name, t, got, ids):
    n, H, C = t.shape
    tr = _tile(H, SUM_ROWS)
    nb = H // tr

    def body(ids_ref, t_ref, r_ref, o_ref):
        own = t_ref[...].astype(F32)
        acc = jnp.where(ids_ref[0] == 0, own, r_ref[0].astype(F32))
        for k in range(1, n):
            acc = acc + jnp.where(ids_ref[0] == k, own, r_ref[k].astype(F32))
        o_ref[...] = acc

    return pl.pallas_call(
        body, name=name,
        grid_spec=pltpu.PrefetchScalarGridSpec(
            num_scalar_prefetch=1, grid=(nb,),
            in_specs=[pl.BlockSpec((None, tr, C), lambda i, ids_ref: (ids_ref[0], i, 0)),
                      pl.BlockSpec((n, tr, C), lambda i, ids_ref: (0, i, 0))],
            out_specs=pl.BlockSpec((tr, C), lambda i, ids_ref: (ids_ref[1] * nb + i, 0))),
        out_shape=jax.ShapeDtypeStruct((2 * H, C), F32), compiler_params=_params("parallel"),
    )(ids, t, got)


def _sum_pieces(name, pieces):
    n, R, C = pieces.shape
    tr = _tile(R, (256, 128, SUBLANES))

    def body(p_ref, o_ref):
        acc = p_ref[0].astype(F32)
        for s in range(1, n):
            acc = acc + p_ref[s].astype(F32)
        o_ref[...] = acc

    return pl.pallas_call(
        body, name=name, grid=(R // tr,),
        in_specs=[pl.BlockSpec((n, tr, C), lambda i: (0, i, 0))], out_specs=pl.BlockSpec((tr, C), lambda i: (i, 0)),
        out_shape=jax.ShapeDtypeStruct((R, C), F32), compiler_params=_params("parallel"),
    )(pieces)


def _swap_halves(name, ss):
    n = len(ss)

    def body(*refs):
        s_refs, out_refs, (send_sems, recv_sems) = refs[:n], refs[n:2 * n], refs[2 * n:]
        x, y, c = _position()
        copies, waits = [], []
        for t, (s_ref, out_ref) in enumerate(zip(s_refs, out_refs)):
            half = s_ref.shape[0] // 2
            sems = dict(send_sem=send_sems.at[t], recv_sem=recv_sems.at[t], device_id=(x, y, 1 - c), device_id_type=MESH)
            mine = s_ref.at[pl.ds(c * half, half), :]
            copies.append(pltpu.make_async_remote_copy(src_ref=mine, dst_ref=out_ref.at[pl.ds(c * half, half), :], **sems))
            waits.append(pltpu.make_async_remote_copy(src_ref=mine, dst_ref=out_ref.at[pl.ds((1 - c) * half, half), :], **sems))
        for cp in copies:
            cp.start()
        for cp in waits:
            cp.wait_recv()
        for cp in copies:
            cp.wait_send()

    return pl.pallas_call(
        body, name=name, in_specs=[ANY] * n, out_specs=[ANY] * n, input_output_aliases={i: i for i in range(n)},
        out_shape=[jax.ShapeDtypeStruct(s_.shape, s_.dtype) for s_ in ss],
        scratch_shapes=[pltpu.SemaphoreType.DMA((n,)), pltpu.SemaphoreType.DMA((n,))],
    )(*ss)


def _gather_small(name, blk):
    m_per, n = blk.shape

    def body(x_ref, out_ref, send_sems, recv_sems, local_sem):
        x, y, c = _position()
        me, sibling = (x, y, c), (x, y, 1 - c)
        chips = [(1 - x, y), (x, 1 - y), (1 - x, 1 - y)]

        def rows(px, py, pc):
            return out_ref.at[pl.ds((4 * px + 2 * py + pc) * m_per, m_per), :]

        def copy(k, block, to, src=None):
            return pltpu.make_async_remote_copy(src_ref=rows(*block) if src is None else src, dst_ref=rows(*block),
                                                send_sem=send_sems.at[k], recv_sem=recv_sems.at[k], device_id=to, device_id_type=MESH)

        mine = pltpu.make_async_copy(x_ref, rows(*me), local_sem)
        mine.start()
        first = [copy(0, me, sibling, src=x_ref)] + [copy(1 + j, me, (*chip, c), src=x_ref) for j, chip in enumerate(chips)]
        for cp in first:
            cp.start()
        passed = [copy(4 + j, (*chip, c), sibling) for j, chip in enumerate(chips)]
        for j, chip in enumerate(chips):
            copy(1 + j, (*chip, c), me).wait_recv()
            passed[j].start()
        copy(0, sibling, me).wait_recv()
        for j, chip in enumerate(chips):
            copy(4 + j, (*chip, 1 - c), me).wait_recv()
        for cp in first + passed:
            cp.wait_send()
        mine.wait()

    return pl.pallas_call(
        body, name=name, out_shape=jax.ShapeDtypeStruct((N_DEV * m_per, n), blk.dtype),
        in_specs=[pl.BlockSpec(memory_space=pltpu.VMEM)], out_specs=pl.BlockSpec(memory_space=pltpu.VMEM),
        scratch_shapes=[pltpu.SemaphoreType.DMA((7,)), pltpu.SemaphoreType.DMA((7,)), pltpu.SemaphoreType.DMA],
        compiler_params=pltpu.CompilerParams(vmem_limit_bytes=VMEM_LIMIT_BYTES),
    )(blk)


WEIGHTS = ("norm_mix_g", "norm_mlp_g", "mlp_w1", "mlp_w2", "gdn_w_in", "gdn_conv_w", "gdn_a_log", "gdn_dt_bias", "gdn_o_norm_g",
           "gdn_w_out", "s5_w_in", "s5_lam_re", "s5_lam_im", "s5_log_dt", "s5_b_re", "s5_b_im", "s5_c_re", "s5_c_im", "s5_d",
           "s5_w_out", "m2_w_in", "m2_conv_w", "m2_conv_b", "m2_dt_bias", "m2_a_log", "m2_d", "m2_norm_g", "m2_w_out",
           "final_norm_g")
BIG = {"mlp_w1": 2, "mlp_w2": 1, "gdn_w_in": 2, "gdn_w_out": 1, "s5_w_in": 1, "s5_w_out": 2, "m2_w_in": 2, "m2_w_out": 1}
SMALL_CUT = {"gdn_conv_w": 2, "m2_conv_w": 2, "m2_conv_b": 1, "m2_norm_g": 1}
LAYER_ITEMS = (
    ((("mlp_w1", 0), ("mlp_w2", 0), ("gdn_w_out", 0)), (("gdn_w_in", 0),)),
    ((("mlp_w1", 1), ("mlp_w2", 1), ("s5_w_in", 0)), (("s5_w_out", 0),)),
    ((("mlp_w1", 2), ("mlp_w2", 2), ("m2_w_out", 0)), (("m2_w_in", 0),)),
    ((("mlp_w1", 3), ("mlp_w2", 3), ("gdn_w_out", 1)), (("gdn_w_in", 1),)),
)


def _rows2d(a):
    return a.reshape(-1, a.shape[-1])


def _pack(arrays, cols, row_multiple, dtype):
    flat = jnp.concatenate([a.reshape(-1).astype(dtype) for a in arrays])
    n = -(-flat.shape[0] // (cols * row_multiple)) * cols * row_multiple
    return jnp.pad(flat, (0, n - flat.shape[0])).reshape(-1, cols)


def _unpack(packed, shapes):
    flat = packed.reshape(-1)
    out, off = [], 0
    for shp in shapes:
        n = math.prod(shp)
        out.append(flat[off:off + n].reshape(shp))
        off += n
    return out


def _split_rows(buf, shapes):
    out, off = [], 0
    for shp in shapes:
        rows = math.prod(shp[:-1])
        out.append(buf[off:off + rows].reshape(shp))
        off += rows
    return out


def _cut(a, axis, k):
    n = a.shape[axis] // N_CHIPS
    return lax.slice_in_dim(a, k * n, (k + 1) * n, axis=axis)


def kernel(x, norm_mix_g, norm_mlp_g, mlp_w1, mlp_w2, gdn_w_in, gdn_conv_w, gdn_a_log, gdn_dt_bias, gdn_o_norm_g, gdn_w_out, s5_w_in, s5_lam_re, s5_lam_im, s5_log_dt, s5_b_re, s5_b_im, s5_c_re, s5_c_im, s5_d, s5_w_out, m2_w_in, m2_conv_w, m2_conv_b, m2_dt_bias, m2_a_log, m2_d, m2_norm_g, m2_w_out, final_norm_g, loss_target, m_norm_mix_g, m_norm_mlp_g, m_mlp_w1, m_mlp_w2, m_gdn_w_in, m_gdn_conv_w, m_gdn_a_log, m_gdn_dt_bias, m_gdn_o_norm_g, m_gdn_w_out, m_s5_w_in, m_s5_lam_re, m_s5_lam_im, m_s5_log_dt, m_s5_b_re, m_s5_b_im, m_s5_c_re, m_s5_c_im, m_s5_d, m_s5_w_out, m_m2_w_in, m_m2_conv_w, m_m2_conv_b, m_m2_dt_bias, m_m2_a_log, m_m2_d, m_m2_norm_g, m_m2_w_out, m_final_norm_g, v_norm_mix_g, v_norm_mlp_g, v_mlp_w1, v_mlp_w2, v_gdn_w_in, v_gdn_conv_w, v_gdn_a_log, v_gdn_dt_bias, v_gdn_o_norm_g, v_gdn_w_out, v_s5_w_in, v_s5_lam_re, v_s5_lam_im, v_s5_log_dt, v_s5_b_re, v_s5_b_im, v_s5_c_re, v_s5_c_im, v_s5_d, v_s5_w_out, v_m2_w_in, v_m2_conv_w, v_m2_conv_b, v_m2_dt_bias, v_m2_a_log, v_m2_d, v_m2_norm_g, v_m2_w_out, v_final_norm_g):
    given = dict(locals())
    w = {n: given[n] for n in WEIGHTS}
    mom = {n: given["m_" + n] for n in WEIGHTS}
    var = {n: given["v_" + n] for n in WEIGHTS}
    big, small_cut = tuple(BIG), tuple(SMALL_CUT)
    small = tuple(n for n in WEIGHTS if n not in BIG)
    chip = 2 * lax.axis_index("x") + lax.axis_index("y")

    W = {n: [None] * w[n].shape[0] for n in big}

    def fetch(groups, gather):
        own = [jnp.concatenate([w[n][l] for n, l in grp]).astype(BF16) for grp in groups]
        for grp, mine, got in zip(groups, own, gather(own)):
            shapes = [w[n][l].shape for n, l in grp]
            per_chip = [_split_rows(jnp.where(chip == k, mine, got[k]), shapes) for k in range(N_CHIPS)]
            for i, (n, l) in enumerate(grp):
                m = jnp.concatenate([per_chip[k][i] for k in range(N_CHIPS)], axis=BIG[n] - 1)
                pad = {"gdn_w_in": GDN_IN_PAD - GDN_IN, "m2_w_in": M2_IN_PAD - M2_IN}.get(n, 0)
                W[n][l] = jnp.pad(m, ((0, 0), (0, pad))) if pad else m

    fetch([grp for items in LAYER_ITEMS[1:] for grp in items], _gather_shards_later)
    fetch(LAYER_ITEMS[0], _gather_shards)
    cut_blk = _pack([w[n] for n in small_cut], LANES, SUBLANES, F32)
    cut_all = _gather_small("gather_small_params", cut_blk).reshape(N_DEV, *cut_blk.shape)
    per_chip = [_unpack(cut_all[2 * k], [w[n].shape for n in small_cut]) for k in range(N_CHIPS)]
    W.update({n: jnp.concatenate([per_chip[k][i] for k in range(N_CHIPS)], axis=SMALL_CUT[n]) for i, n in enumerate(small_cut)})
    W.update({n: w[n] for n in small if n not in SMALL_CUT})

    core = lax.axis_index("c").astype(jnp.int32)
    ids = jnp.stack([chip.astype(jnp.int32), core])
    shard_grads = {}

    def reduce_layer(i, dws):
        groups = LAYER_ITEMS[i]
        gps = [jnp.stack([jnp.concatenate([_cut(dws[it], BIG[it[0]] - 1, k) for it in grp]).astype(BF16) for k in range(N_CHIPS)])
               for grp in groups]
        pairs = [_pair_sum(f"pair_sum{i}_{j}", gp, got, core.reshape(1))
                 for j, (gp, got) in enumerate(zip(gps, _pair_exchange(f"pair_exchange{i}", gps)))]
        sums = [_chip_sum(f"chip_sum{i}_{j}", t, got, ids) for j, (t, got) in enumerate(zip(pairs, _chip_exchange_later(pairs, i)))]
        for grp, g_shard in zip(groups, _swap_halves(f"swap_halves{i}", sums)):
            shard_grads.update(zip(grp, _split_rows(g_shard, [w[n][l].shape for n, l in grp])))

    loss, grad_x, G = _local_step(x[0], loss_target[0], W, reduce_layer)
    loss = lax.psum(loss[0, 0], ("x", "y", "c"))
    grads = {n: jnp.stack([shard_grads[n, l] for l in range(w[n].shape[0])]) for n in big}
    sg = _pack([G[n] for n in small], LANES, ADAM_ROWS, F32)
    sg_sum = _sum_pieces("sum_small_grads", _gather_small("gather_small_grads", sg).reshape(N_DEV, *sg.shape))
    for n, g in zip(small, _unpack(sg_sum, [G[n].shape for n in small])):
        if n in SMALL_CUT:
            width = g.shape[SMALL_CUT[n]] // N_CHIPS
            g = lax.dynamic_slice_in_dim(g, chip * width, width, axis=SMALL_CUT[n])
        grads[n] = g.reshape(w[n].shape)

    delta, new_m, new_v = {}, {}, {}
    for n in big:
        as2d = lambda a: a.reshape(-1, a.shape[-1])
        outs = _adamw("adamw_" + n, as2d(w[n]), as2d(grads[n]), as2d(mom[n]), as2d(var[n]))
        delta[n], new_m[n], new_v[n] = (o.reshape(w[n].shape) for o in outs)
    packs = [_pack([t[n] for n in small], LANES, ADAM_ROWS, F32) for t in (w, grads, mom, var)]
    outs = _adamw("adamw_small", *packs)
    for t, o in zip((delta, new_m, new_v), outs):
        t.update(zip(small, _unpack(o, [w[n].shape for n in small])))

    return (loss, grad_x[None], *[grads[n] for n in WEIGHTS], *[delta[n] for n in WEIGHTS], *[new_m[n] for n in WEIGHTS],
            *[new_v[n] for n in WEIGHTS])
```

```python
import functools
import math

import numpy as np
import jax
import jax.numpy as jnp
from jax import lax
from jax.experimental import pallas as pl
from jax.experimental.pallas import tpu as pltpu
from jax.experimental.pallas import tpu_sc as plsc

F32 = jnp.float32
BF16 = jnp.bfloat16

D_MODEL = 1024
D_FF = 4096
DEPTH = 4
CHUNK = 64
RMS_EPS = 1e-6
CONV_W = 4
GDN_HEADS = 8
GDN_DK = 128
GDN_IN = 4112
GDN_IN_PAD = 4224
S5_GROUPS = 64
S5_STATE = 64
S5_GROUP = 16
S5_BLOCKS = 8
M2_INNER = 2048
M2_HEADS = 32
M2_GROUPS = 8
M2_STATE = 128
M2_CONV_CH = 4096
M2_IN = 6176
M2_IN_PAD = 6272
ADAM_LR, ADAM_B1, ADAM_B2, ADAM_EPS, ADAM_WD, ADAM_STEP = 0.001, 0.9, 0.999, 1e-08, 0.01, 10

VMEM_LIMIT_BYTES = 56 * 1024 * 1024
SUBLANES = 8
LANES = 128


def _params(*sem):
    return pltpu.CompilerParams(dimension_semantics=tuple(sem) if sem else None, vmem_limit_bytes=VMEM_LIMIT_BYTES)


NN, NT, TN = ((1,), (0,)), ((1,), (1,)), ((0,), (0,))
_DOT_TRANSPOSES = {NN: ((NT, "gb"), (TN, "ag")), NT: ((NN, "gb"), (TN, "ga")), TN: ((NT, "bg"), (NN, "ag"))}


def _dg(a, b, dims):
    if a.ndim == 3:
        dn = (((dims[0][0] + 1,), (dims[1][0] + 1,)), ((0,), (0,)))
    else:
        dn = (dims, ((), ()))
    return lax.dot_general(a, b, dn, preferred_element_type=F32)


def _mxu(a, b, dims):
    return _dg(a.astype(BF16), b.astype(BF16), dims)


@functools.partial(jax.custom_vjp, nondiff_argnums=(2,))
def _dot(a, b, dims=NN):
    return _mxu(a, b, dims)


def _dot_fwd(a, b, dims):
    return _mxu(a, b, dims), (a, b)


def _dot_bwd(dims, res, g):
    ops = dict(a=res[0], b=res[1], g=g)
    (da_dims, da_ops), (db_dims, db_ops) = _DOT_TRANSPOSES[dims]
    return (_mxu(ops[da_ops[0]], ops[da_ops[1]], da_dims).astype(res[0].dtype),
            _mxu(ops[db_ops[0]], ops[db_ops[1]], db_dims).astype(res[1].dtype))


_dot.defvjp(_dot_fwd, _dot_bwd)


def _nt(a, b):
    return _dot(a, b, NT)


def _tn(a, b):
    return _dot(a, b, TN)


def _split3(x):
    x1 = x.astype(BF16)
    r = x - x1.astype(F32)
    x2 = r.astype(BF16)
    return x1, x2, (r - x2.astype(F32)).astype(BF16)


def _sel_mxu(x, sel, dims, x_first):
    f = (lambda p: _dg(p, sel.astype(BF16), dims)) if x_first else (lambda p: _dg(sel.astype(BF16), p, dims))
    x1, x2, x3 = _split3(x)
    return f(x1) + (f(x2) + f(x3))


@jax.custom_vjp
def _pick(x, sel):
    return _sel_mxu(x, sel, NN, True)


def _pick_fwd(x, sel):
    return _sel_mxu(x, sel, NN, True), sel


def _pick_bwd(sel, g):
    return _sel_mxu(g, sel, NT, True), jnp.zeros_like(sel)


_pick.defvjp(_pick_fwd, _pick_bwd)


@jax.custom_vjp
def _accum(sel, x):
    return _sel_mxu(x, sel, NN, False)


def _accum_fwd(sel, x):
    return _sel_mxu(x, sel, NN, False), sel


def _accum_bwd(sel, g):
    return jnp.zeros_like(sel), _sel_mxu(g, sel, TN, False)


_accum.defvjp(_accum_fwd, _accum_bwd)


def _dot3(a, b, dims=NN):
    ah, bh = a.astype(BF16), b.astype(BF16)
    al, bl = (a - ah.astype(F32)).astype(BF16), (b - bh.astype(F32)).astype(BF16)
    return _dg(ah, bh, dims) + (_dg(ah, bl, dims) + _dg(al, bh, dims))


def _neumann(x, r, dims):
    r = r + _dot3(x, r, dims)
    for _ in range(5):
        x = _dot3(x, x)
        r = r + _dot3(x, r, dims)
    return r


@jax.custom_vjp
def _unit_lower_solve(a, rhs):
    return _neumann(-a, rhs, NN)


def _unit_lower_solve_fwd(a, rhs):
    sol = _neumann(-a, rhs, NN)
    return sol, (a, sol)


def _unit_lower_solve_bwd(res, ct):
    a, sol = res
    d_rhs = _neumann(-a, ct, TN)
    return -_dot3(d_rhs, sol, NT), d_rhs


_unit_lower_solve.defvjp(_unit_lower_solve_fwd, _unit_lower_solve_bwd)


def _sigmoid(x):
    return 1.0 / (1.0 + jnp.exp(-x))


def _softplus(x):
    return jnp.maximum(x, 0.0) + jnp.log(1.0 + jnp.exp(-jnp.abs(x)))


def _iota2(shape, axis):
    return lax.broadcasted_iota(jnp.int32, shape, axis)


def _tile(n, cands):
    for c in cands:
        if n % c == 0:
            return c
    return n


MM_TILE_BYTES = 9 * 1024 * 1024


def _mm(name, a, b, mode, out_dtypes, epi=None, extras=(), tn=None):
    if mode == "nn":
        (M, K), N = a.shape, b.shape[1]
    elif mode == "nt":
        (M, K), N = a.shape, b.shape[0]
    else:
        (K, M), N = a.shape, b.shape[1]
    tn = tn or _tile(N, (512, 384, 896, 256, 128))
    out_bytes = tn * (sum(jnp.dtype(d).itemsize for d in out_dtypes) + sum(e.dtype.itemsize for e in extras))
    fits = lambda t: t * K * a.dtype.itemsize <= MM_TILE_BYTES and t * out_bytes <= MM_TILE_BYTES
    tm = next(t for t in (2048, 1024, 512, 256, 128) if M % t == 0 and (fits(t) or t == 128))
    if mode == "nn":
        a_spec, b_spec = pl.BlockSpec((tm, K), lambda i, j: (i, 0)), pl.BlockSpec((K, tn), lambda i, j: (0, j))
        dims = NN
    elif mode == "nt":
        a_spec, b_spec = pl.BlockSpec((tm, K), lambda i, j: (i, 0)), pl.BlockSpec((tn, K), lambda i, j: (j, 0))
        dims = NT
    else:
        a_spec, b_spec = pl.BlockSpec((K, tm), lambda i, j: (0, i)), pl.BlockSpec((K, tn), lambda i, j: (0, j))
        dims = TN
    n_ex = len(extras)

    def body(a_ref, b_ref, *rest):
        acc = _mxu(a_ref[...], b_ref[...], dims)
        res = epi(acc, *[e[...] for e in rest[:n_ex]]) if epi is not None else (acc,)
        for o_ref, r in zip(rest[n_ex:], res):
            o_ref[...] = r.astype(o_ref.dtype)

    tile = pl.BlockSpec((tm, tn), lambda i, j: (i, j))
    out = pl.pallas_call(
        body, name=name, grid=(M // tm, N // tn),
        in_specs=[a_spec, b_spec] + [tile] * n_ex,
        out_specs=[tile] * len(out_dtypes),
        out_shape=[jax.ShapeDtypeStruct((M, N), d) for d in out_dtypes],
        compiler_params=_params("parallel", "parallel"),
    )(a, b, *extras)
    return out if len(out_dtypes) > 1 else out[0]


def _rms_fwd(name, h, g):
    L, D = h.shape
    tr = _tile(L, (256, 128))

    def body(h_ref, g_ref, o_ref):
        x = h_ref[...]
        r = lax.rsqrt(jnp.mean(x * x, axis=-1, keepdims=True) + RMS_EPS)
        o_ref[...] = (x * r * g_ref[...]).astype(o_ref.dtype)

    return pl.pallas_call(
        body, name=name, grid=(L // tr,),
        in_specs=[pl.BlockSpec((tr, D), lambda i: (i, 0)), pl.BlockSpec((1, D), lambda i: (0, 0))],
        out_specs=pl.BlockSpec((tr, D), lambda i: (i, 0)),
        out_shape=jax.ShapeDtypeStruct((L, D), BF16),
        compiler_params=_params("parallel"),
    )(h, g.reshape(1, D))


def _rms_bwd(name, h, g, dhn, dres):
    L, D = h.shape
    tr = _tile(L, (256, 128))

    def body(h_ref, g_ref, dhn_ref, dres_ref, dh_ref, dg_ref):
        x = h_ref[...]
        r = lax.rsqrt(jnp.mean(x * x, axis=-1, keepdims=True) + RMS_EPS)
        xh = x * r
        dy = dhn_ref[...]
        dxh = dy * g_ref[...]
        dh_ref[...] = dres_ref[...] + r * (dxh - xh * jnp.mean(dxh * xh, axis=-1, keepdims=True))

        @pl.when(pl.program_id(0) == 0)
        def _():
            dg_ref[...] = jnp.zeros_like(dg_ref)

        dg_ref[...] += jnp.sum(dy * xh, axis=0, keepdims=True)

    row = pl.BlockSpec((tr, D), lambda i: (i, 0))
    vec = pl.BlockSpec((1, D), lambda i: (0, 0))
    return pl.pallas_call(
        body, name=name, grid=(L // tr,),
        in_specs=[row, vec, row, row], out_specs=[row, vec],
        out_shape=[jax.ShapeDtypeStruct((L, D), F32), jax.ShapeDtypeStruct((1, D), F32)],
        compiler_params=_params("arbitrary"),
    )(h, g.reshape(1, D), dhn, dres)


def _loss_head(h, g, target):
    L, D = h.shape
    tr = _tile(L, (256, 128))

    def body(h_ref, g_ref, t_ref, loss_ref, dh_ref, dg_ref):
        x = h_ref[...]
        r = lax.rsqrt(jnp.mean(x * x, axis=-1, keepdims=True) + RMS_EPS)
        xh = x * r
        err = xh * g_ref[...] - t_ref[...]
        dy = err * (1.0 / D)
        dxh = dy * g_ref[...]
        dh_ref[...] = r * (dxh - xh * jnp.mean(dxh * xh, axis=-1, keepdims=True))

        @pl.when(pl.program_id(0) == 0)
        def _():
            dg_ref[...] = jnp.zeros_like(dg_ref)
            loss_ref[...] = jnp.zeros_like(loss_ref)

        dg_ref[...] += jnp.sum(dy * xh, axis=0, keepdims=True)
        loss_ref[...] += (0.5 / D) * jnp.sum(jnp.sum(err * err, axis=-1, keepdims=True), axis=0, keepdims=True)

    row = pl.BlockSpec((tr, D), lambda i: (i, 0))
    vec = pl.BlockSpec((1, D), lambda i: (0, 0))
    return pl.pallas_call(
        body, name="loss_head", grid=(L // tr,),
        in_specs=[row, vec, row], out_specs=[pl.BlockSpec((1, 1), lambda i: (0, 0)), row, vec],
        out_shape=[jax.ShapeDtypeStruct((1, 1), F32), jax.ShapeDtypeStruct((L, D), F32), jax.ShapeDtypeStruct((1, D), F32)],
        compiler_params=_params("arbitrary"),
    )(h, g.reshape(1, D), target)


def _glu_fwd(h, ag):
    L, D = h.shape
    tr = _tile(L, (256, 128))

    def body(h_ref, v_ref, g_ref, o_ref):
        o_ref[...] = h_ref[...] + v_ref[...] * _sigmoid(g_ref[...])

    return pl.pallas_call(
        body, name="s5_glu_fwd", grid=(L // tr,),
        in_specs=[pl.BlockSpec((tr, D), lambda i: (i, 0)), pl.BlockSpec((tr, D), lambda i: (i, 0)),
                  pl.BlockSpec((tr, D), lambda i: (i, 1))],
        out_specs=pl.BlockSpec((tr, D), lambda i: (i, 0)),
        out_shape=jax.ShapeDtypeStruct((L, D), F32),
        compiler_params=_params("parallel"),
    )(h, ag, ag)


def _glu_bwd(dh, ag):
    L, D = dh.shape
    tr = _tile(L, (256, 128))

    def body(dh_ref, v_ref, g_ref, dv_ref, dg_ref):
        s = _sigmoid(g_ref[...])
        d = dh_ref[...]
        dv_ref[...] = d * s
        dg_ref[...] = d * v_ref[...] * s * (1.0 - s)

    dv, dg = pl.pallas_call(
        body, name="s5_glu_bwd", grid=(L // tr,),
        in_specs=[pl.BlockSpec((tr, D), lambda i: (i, 0)), pl.BlockSpec((tr, D), lambda i: (i, 0)),
                  pl.BlockSpec((tr, D), lambda i: (i, 1))],
        out_specs=[pl.BlockSpec((tr, D), lambda i: (i, 0))] * 2,
        out_shape=[jax.ShapeDtypeStruct((L, D), F32)] * 2,
        compiler_params=_params("parallel"),
    )(dh, ag, ag)
    return jnp.concatenate([dv, dg], axis=1).astype(BF16)


CONV_ROWS = 128
CONV_COLS = 512


def _shift_rows(cat, s):
    if s == 0:
        return cat[SUBLANES:, :]
    return pltpu.roll(cat, s, axis=0)[SUBLANES:, :]


def _conv_fwd(name, p, col0, w, b):
    L = p.shape[0]
    C = w.shape[1]
    tc = _tile(C, (CONV_COLS, 256))
    cb0 = col0 // tc
    nr = L // CONV_ROWS

    def body(x_ref, w_ref, b_ref, o_ref):
        def step(r, carry):
            r0 = pl.multiple_of(r * CONV_ROWS, CONV_ROWS)
            cur = x_ref[pl.ds(r0, CONV_ROWS), :]
            p0 = pl.multiple_of(jnp.maximum(r0 - SUBLANES, 0), SUBLANES)
            prev = jnp.where(r > 0, x_ref[pl.ds(p0, SUBLANES), :], 0.0)
            cat = jnp.concatenate([prev, cur], axis=0)
            acc = b_ref[...] + w_ref[3:4, :] * cur
            for k in range(CONV_W - 1):
                acc = acc + w_ref[k:k + 1, :] * _shift_rows(cat, CONV_W - 1 - k)
            o_ref[pl.ds(r0, CONV_ROWS), :] = acc * _sigmoid(acc)
            return carry

        lax.fori_loop(0, nr, step, 0)

    return pl.pallas_call(
        body, name=name, grid=(C // tc,),
        in_specs=[pl.BlockSpec((L, tc), lambda j: (0, cb0 + j)), pl.BlockSpec((CONV_W, tc), lambda j: (0, j)),
                  pl.BlockSpec((1, tc), lambda j: (0, j))],
        out_specs=pl.BlockSpec((L, tc), lambda j: (0, j)),
        out_shape=jax.ShapeDtypeStruct((L, C), F32),
        compiler_params=_params("parallel"),
    )(p, w, b)


def _conv_bwd(name, p, col0, w, b, dout):
    L = p.shape[0]
    C = w.shape[1]
    tc = _tile(C, (CONV_COLS, 256))
    cb0 = col0 // tc
    nr = L // CONV_ROWS

    def body(x_ref, w_ref, b_ref, do_ref, dx_ref, dw_ref, db_ref, dpre_ref):
        def step1(r, carry):
            dw0, dw1, dw2, dw3, dbb = carry
            r0 = pl.multiple_of(r * CONV_ROWS, CONV_ROWS)
            cur = x_ref[pl.ds(r0, CONV_ROWS), :]
            p0 = pl.multiple_of(jnp.maximum(r0 - SUBLANES, 0), SUBLANES)
            prev = jnp.where(r > 0, x_ref[pl.ds(p0, SUBLANES), :], 0.0)
            cat = jnp.concatenate([prev, cur], axis=0)
            sh = [_shift_rows(cat, CONV_W - 1 - k) for k in range(CONV_W - 1)] + [cur]
            acc = b_ref[...] + w_ref[3:4, :] * cur
            for k in range(CONV_W - 1):
                acc = acc + w_ref[k:k + 1, :] * sh[k]
            sg = _sigmoid(acc)
            dpre = do_ref[pl.ds(r0, CONV_ROWS), :] * (sg + acc * sg * (1.0 - sg))
            dpre_ref[pl.ds(r0, CONV_ROWS), :] = dpre
            dws = [d + jnp.sum(dpre * s, axis=0, keepdims=True) for d, s in zip((dw0, dw1, dw2, dw3), sh)]
            return (*dws, dbb + jnp.sum(dpre, axis=0, keepdims=True))

        z = jnp.zeros((1, tc), F32)
        dw0, dw1, dw2, dw3, dbb = lax.fori_loop(0, nr, step1, (z, z, z, z, z))
        dw_ref[...] = jnp.concatenate([dw0, dw1, dw2, dw3, z, z, z, z], axis=0)
        db_ref[...] = dbb

        def step2(r, carry):
            r0 = pl.multiple_of(r * CONV_ROWS, CONV_ROWS)
            cur = dpre_ref[pl.ds(r0, CONV_ROWS), :]
            n0 = pl.multiple_of(jnp.minimum(r0 + CONV_ROWS, L - SUBLANES), SUBLANES)
            nxt = jnp.where(r < nr - 1, dpre_ref[pl.ds(n0, SUBLANES), :], 0.0)
            cat = jnp.concatenate([cur, nxt], axis=0)
            acc = w_ref[3:4, :] * cur
            for k in range(CONV_W - 1):
                s = CONV_W - 1 - k
                acc = acc + w_ref[k:k + 1, :] * pltpu.roll(cat, CONV_ROWS + SUBLANES - s, axis=0)[:CONV_ROWS, :]
            dx_ref[pl.ds(r0, CONV_ROWS), :] = acc
            return carry

        lax.fori_loop(0, nr, step2, 0)

    dx, dw, db = pl.pallas_call(
        body, name=name, grid=(C // tc,),
        in_specs=[pl.BlockSpec((L, tc), lambda j: (0, cb0 + j)), pl.BlockSpec((CONV_W, tc), lambda j: (0, j)),
                  pl.BlockSpec((1, tc), lambda j: (0, j)), pl.BlockSpec((L, tc), lambda j: (0, j))],
        out_specs=[pl.BlockSpec((L, tc), lambda j: (0, j)), pl.BlockSpec((SUBLANES, tc), lambda j: (0, j)),
                   pl.BlockSpec((1, tc), lambda j: (0, j))],
        out_shape=[jax.ShapeDtypeStruct((L, C), F32), jax.ShapeDtypeStruct((SUBLANES, C), F32),
                   jax.ShapeDtypeStruct((1, C), F32)],
        scratch_shapes=[pltpu.VMEM((L, tc), F32)],
        compiler_params=_params("parallel"),
    )(p, w, b, dout)
    return dx, dw[:CONV_W], db


def _chunk_consts():
    r, c = _iota2((CHUNK, CHUNK), 0), _iota2((CHUNK, CHUNK), 1)
    causal = r >= c
    return causal, r > c, (r == c).astype(F32), causal.astype(F32), jnp.ones((CHUNK, CHUNK), F32)


def _by_lanes(t):
    return jnp.concatenate([t[i] for i in range(t.shape[0])], axis=1)


def _by_batch(t, w):
    return jnp.concatenate([t[None, :, i * w:(i + 1) * w] for i in range(t.shape[1] // w)], axis=0)


def _diag_lanes():
    return (_iota2((CHUNK, LANES), 0) == _iota2((CHUNK, LANES), 1)).astype(F32)


def _gdn_chunk(q, k, v, ab, gate, S, alog, dtb, og, ea, eb):
    causal, strict, _, tril, ones = _chunk_consts()
    logits = _by_batch(_pick(ab, jnp.concatenate([_by_lanes(ea), _by_lanes(eb)], axis=1)), LANES)
    H = q.shape[0]
    g = -jnp.exp(alog) * _softplus(logits[:H] + dtb)
    beta = _sigmoid(logits[H:])
    qn = q * lax.rsqrt(jnp.sum(q * q, axis=-1, keepdims=True) + 1e-6) * (GDN_DK ** -0.5)
    kn = k * lax.rsqrt(jnp.sum(k * k, axis=-1, keepdims=True) + 1e-6)
    g_l = _by_lanes(g)
    gc = _by_batch(_accum(tril, g_l), LANES)
    glast = _by_batch(_accum(ones, g_l), LANES)
    gcol = gc[:, :, :CHUNK]
    grow = _by_batch(_accum(ones, _by_lanes(gc * _diag_lanes())), LANES)[:, :, :CHUNK]
    decay = jnp.exp(jnp.where(causal, gcol - grow, -jnp.inf))
    a = jnp.where(strict, beta[:, :, :CHUNK] * _nt(kn, kn) * decay, 0.0)
    eg = jnp.exp(gc)
    sol = _unit_lower_solve(a, jnp.concatenate([v * beta, kn * (beta * eg)], axis=2))
    u, w = sol[:, :, :GDN_DK], sol[:, :, GDN_DK:]
    qk = _nt(qn, kn) * decay
    v_new = u - _dot(w, S)
    o = _dot(qn * eg, S) + _dot(qk, v_new)
    cd = jnp.exp(glast)
    s_new = jnp.concatenate([cd, cd], axis=1) * S + _tn(kn * jnp.exp(glast - gc), v_new)
    on = o * lax.rsqrt(jnp.mean(o * o, axis=-1, keepdims=True) + RMS_EPS) * og
    return on * (gate * _sigmoid(gate)), s_new


GDN_HB = 8


def _gdn_specs(nc, rev):
    cm = (lambda c: nc - 1 - c) if rev else (lambda c: c)
    blk = lambda off: pl.BlockSpec((CHUNK, GDN_HB * GDN_DK), lambda c, h: (cm(c), off // GDN_HB + h))
    ab = pl.BlockSpec((CHUNK, LANES), lambda c, h: (cm(c), (GDN_IN_PAD - LANES) // LANES))
    hv = pl.BlockSpec((GDN_HB, 1, LANES), lambda c, h: (h, 0, 0))
    og = pl.BlockSpec((1, LANES), lambda c, h: (0, 0))
    em = pl.BlockSpec((GDN_HB, LANES, LANES), lambda c, h: (h, 0, 0))
    st = pl.BlockSpec((None, GDN_HB, GDN_DK, GDN_DK), lambda c, h: (cm(c), h, 0, 0))
    return blk, ab, hv, og, em, st


def _gdn_fwd(qc, kc, vc, p, alog_e, dtb_e, og, ea, eb):
    L = qc.shape[0]
    nc = L // CHUNK
    blk, ab, hv, ogs, em, st = _gdn_specs(nc, False)

    def body(q_ref, k_ref, v_ref, gate_ref, ab_ref, al_ref, dt_ref, og_ref, ea_ref, eb_ref, y_ref, sp_ref, s_scr):
        c, h = pl.program_id(0), pl.program_id(1)
        lanes = [slice(i * GDN_DK, (i + 1) * GDN_DK) for i in range(GDN_HB)]
        heads = pl.ds(h * GDN_HB, GDN_HB)
        stack = lambda ref: jnp.concatenate([ref[:, ls][None] for ls in lanes], axis=0)

        @pl.when(c == 0)
        def _():
            s_scr[heads] = jnp.zeros((GDN_HB, GDN_DK, GDN_DK), F32)

        S = s_scr[heads]
        sp_ref[...] = S
        y, s_new = _gdn_chunk(stack(q_ref), stack(k_ref), stack(v_ref), ab_ref[...], stack(gate_ref), S,
                              al_ref[...], dt_ref[...], og_ref[...], ea_ref[...], eb_ref[...])
        for i, ls in enumerate(lanes):
            y_ref[:, ls] = y[i]
        s_scr[heads] = s_new

    return pl.pallas_call(
        body, name="gdn_fwd", grid=(nc, GDN_HEADS // GDN_HB),
        in_specs=[blk(0), blk(0), blk(0), blk(3 * GDN_HEADS), ab, hv, hv, ogs, em, em],
        out_specs=[blk(0), st],
        out_shape=[jax.ShapeDtypeStruct((L, D_MODEL), F32), jax.ShapeDtypeStruct((nc, GDN_HEADS, GDN_DK, GDN_DK), F32)],
        scratch_shapes=[pltpu.VMEM((GDN_HEADS, GDN_DK, GDN_DK), F32)],
        compiler_params=_params("arbitrary", "arbitrary"),
    )(qc, kc, vc, p, p, alog_e, dtb_e, og, ea, eb)


def _gdn_bwd(qc, kc, vc, p, alog_e, dtb_e, og, ea, eb, sprev, dy):
    L = qc.shape[0]
    nc = L // CHUNK
    blk, ab, hv, ogs, em, st = _gdn_specs(nc, True)

    def body(q_ref, k_ref, v_ref, gate_ref, ab_ref, al_ref, dt_ref, og_ref, ea_ref, eb_ref, sp_ref, dy_ref,
             dq_ref, dk_ref, dv_ref, dgate_ref, dab_ref, dpar_ref, ds_scr):
        c, h = pl.program_id(0), pl.program_id(1)
        lanes = [slice(i * GDN_DK, (i + 1) * GDN_DK) for i in range(GDN_HB)]
        heads = pl.ds(h * GDN_HB, GDN_HB)
        stack = lambda ref: jnp.concatenate([ref[:, ls][None] for ls in lanes], axis=0)

        @pl.when(c == 0)
        def _():
            ds_scr[heads] = jnp.zeros((GDN_HB, GDN_DK, GDN_DK), F32)
            dpar_ref[heads] = jnp.zeros((GDN_HB, SUBLANES, LANES), F32)

        @pl.when(h == 0)
        def _():
            dab_ref[...] = jnp.zeros_like(dab_ref)

        ea_m, eb_m = ea_ref[...], eb_ref[...]
        f = lambda q, k, v, a_b, gate, S, al, dt, o_g: _gdn_chunk(q, k, v, a_b, gate, S, al, dt, o_g, ea_m, eb_m)
        _, vjp = jax.vjp(f, stack(q_ref), stack(k_ref), stack(v_ref), ab_ref[...], stack(gate_ref), sp_ref[...],
                         al_ref[...], dt_ref[...], og_ref[...])
        dq, dk, dv, dab, dgate, ds, dal, ddt, dog = vjp((stack(dy_ref), ds_scr[heads]))
        for i, ls in enumerate(lanes):
            dq_ref[:, ls] = dq[i]
            dk_ref[:, ls] = dk[i]
            dv_ref[:, ls] = dv[i]
            dgate_ref[:, ls] = dgate[i]
        ds_scr[heads] = ds
        dab_ref[...] += dab
        first = _iota2((GDN_HB, 1, LANES), 0) == 0
        dpar_ref[heads] += jnp.concatenate([dal, ddt, jnp.where(first, dog[None], 0.0),
                                            jnp.zeros((GDN_HB, SUBLANES - 3, LANES), F32)], axis=1)

    return pl.pallas_call(
        body, name="gdn_bwd", grid=(nc, GDN_HEADS // GDN_HB),
        in_specs=[blk(0), blk(0), blk(0), blk(3 * GDN_HEADS), ab, hv, hv, ogs, em, em, st, blk(0)],
        out_specs=[blk(0), blk(0), blk(0), blk(0), pl.BlockSpec((CHUNK, LANES), lambda c, h: (nc - 1 - c, 0)),
                   pl.BlockSpec((GDN_HEADS, SUBLANES, LANES), lambda c, h: (0, 0, 0))],
        out_shape=[jax.ShapeDtypeStruct((L, D_MODEL), F32)] * 4
        + [jax.ShapeDtypeStruct((L, LANES), F32), jax.ShapeDtypeStruct((GDN_HEADS, SUBLANES, LANES), F32)],
        scratch_shapes=[pltpu.VMEM((GDN_HEADS, GDN_DK, GDN_DK), F32)],
        compiler_params=_params("arbitrary", "arbitrary"),
    )(qc, kc, vc, p, p, alog_e, dtb_e, og, ea, eb, sprev, dy)


def _gdn_selectors():
    rows = np.arange(LANES)[None, :, None]
    heads = np.arange(GDN_HEADS)[:, None, None]
    ea = np.broadcast_to(rows == heads, (GDN_HEADS, LANES, LANES)).astype(np.float32)
    eb = np.broadcast_to(rows == heads + GDN_HEADS, (GDN_HEADS, LANES, LANES)).astype(np.float32)
    return jnp.asarray(ea), jnp.asarray(eb)


M2_GW = M2_INNER // M2_GROUPS
M2_HPG = M2_HEADS // M2_GROUPS
M2_HD = M2_INNER // M2_HEADS


def _m2_chunk(x, bm, cm, z, dtr, st, dtb, alog, dsk, ng, e, ecol):
    G = x.shape[0]
    causal, _, _, tril, ones = _chunk_consts()
    dt_n = _softplus(dtr + dtb)
    da_n = dt_n * (-jnp.exp(alog))
    cum_n = _accum(tril, da_n)
    tot_n = _accum(ones, da_n)
    wide = _pick(jnp.concatenate([dt_n, cum_n, tot_n], axis=0), e)
    dt_w, cum_w, tot_w = (_by_batch(wide[i * CHUNK:(i + 1) * CHUNK], M2_GW) for i in range(3))
    xdt = x * dt_w
    cb = _nt(cm, bm)
    heads = lambda t: jnp.concatenate([t[i:i + 1] for i in range(G) for _ in range(M2_HPG)], axis=0)
    colb = _by_batch(_pick(cum_n, ecol), LANES)
    rowb = _by_batch(_accum(ones, _by_lanes(colb * _diag_lanes())), LANES)
    lmat = jnp.exp(jnp.where(causal, colb[:, :, :CHUNK] - rowb[:, :, :CHUNK], -jnp.inf))
    yr = _dot(heads(cb) * lmat, heads(xdt))
    head = _iota2((CHUNK, M2_GW), 1) // M2_HD
    ydiag = jnp.concatenate([sum(jnp.where(head == r, yr[i * M2_HPG + r], 0.0) for r in range(M2_HPG))[None] for i in range(G)], axis=0)
    st_new = _tn(bm, xdt * jnp.exp(tot_w - cum_w))
    cd = jnp.exp(tot_w)
    s_new = jnp.concatenate([cd, cd], axis=1) * st + st_new
    y = ydiag + _dot(cm, st) * jnp.exp(cum_w) + dsk * x
    y = y * (z * _sigmoid(z))
    yn = y * lax.rsqrt(jnp.mean(y * y, axis=-1, keepdims=True) + RMS_EPS) * ng
    return yn, s_new


M2_GB = 4


def _m2_specs(nc, rev):
    cm = (lambda c: nc - 1 - c) if rev else (lambda c: c)
    wide = lambda off: pl.BlockSpec((CHUNK, M2_GB * M2_GW), lambda c, g: (cm(c), off // M2_GB + g))
    nar = lambda off: pl.BlockSpec((CHUNK, M2_GB * LANES), lambda c, g: (cm(c), off // M2_GB + g))
    dts = pl.BlockSpec((CHUNK, LANES), lambda c, g: (cm(c), (M2_IN_PAD - LANES) // LANES))
    v128 = pl.BlockSpec((1, LANES), lambda c, g: (0, 0))
    v256 = pl.BlockSpec((1, M2_GB * M2_GW), lambda c, g: (0, g))
    es = pl.BlockSpec((LANES, M2_GB * M2_GW), lambda c, g: (0, g))
    ecs = pl.BlockSpec((LANES, M2_GB * M2_HPG * LANES), lambda c, g: (0, g))
    st = pl.BlockSpec((None, M2_GB, M2_STATE, M2_GW), lambda c, g: (cm(c), g, 0, 0))
    return wide, nar, dts, v128, v256, es, ecs, st


def _m2_fwd(xbc, p, dtb, alog, dsk, ng, e, ecol):
    L = xbc.shape[0]
    nc = L // CHUNK
    wide, nar, dts, v128, v256, es, ecs, st = _m2_specs(nc, False)

    def body(x_ref, b_ref, c_ref, z_ref, dt_ref, dtb_ref, al_ref, dsk_ref, ng_ref, e_ref, ec_ref, y_ref, sp_ref, s_scr):
        c, g = pl.program_id(0), pl.program_id(1)
        wide_l = [slice(i * M2_GW, (i + 1) * M2_GW) for i in range(M2_GB)]
        nar_l = [slice(i * LANES, (i + 1) * LANES) for i in range(M2_GB)]
        groups = pl.ds(g * M2_GB, M2_GB)
        wide_s = lambda ref: jnp.concatenate([ref[:, ls][None] for ls in wide_l], axis=0)
        nar_s = lambda ref: jnp.concatenate([ref[:, ls][None] for ls in nar_l], axis=0)

        @pl.when(c == 0)
        def _():
            s_scr[groups] = jnp.zeros((M2_GB, M2_STATE, M2_GW), F32)

        S = s_scr[groups]
        sp_ref[...] = S
        y, s_new = _m2_chunk(wide_s(x_ref), nar_s(b_ref), nar_s(c_ref), wide_s(z_ref), dt_ref[...], S, dtb_ref[...], al_ref[...],
                             wide_s(dsk_ref), wide_s(ng_ref), e_ref[...], ec_ref[...])
        for i, ls in enumerate(wide_l):
            y_ref[:, ls] = y[i]
        s_scr[groups] = s_new

    return pl.pallas_call(
        body, name="m2_fwd", grid=(nc, M2_GROUPS // M2_GB),
        in_specs=[wide(0), nar(2 * M2_GROUPS), nar(3 * M2_GROUPS), wide(0), dts, v128, v128, v256, v256, es, ecs],
        out_specs=[wide(0), st],
        out_shape=[jax.ShapeDtypeStruct((L, M2_INNER), F32), jax.ShapeDtypeStruct((nc, M2_GROUPS, M2_STATE, M2_GW), F32)],
        scratch_shapes=[pltpu.VMEM((M2_GROUPS, M2_STATE, M2_GW), F32)],
        compiler_params=_params("arbitrary", "arbitrary"),
    )(xbc, xbc, xbc, p, p, dtb, alog, dsk, ng, e, ecol)


def _m2_bwd(xbc, p, dtb, alog, dsk, ng, e, ecol, sprev, dy):
    L = xbc.shape[0]
    nc = L // CHUNK
    wide, nar, dts, v128, v256, es, ecs, st = _m2_specs(nc, True)

    def body(x_ref, b_ref, c_ref, z_ref, dt_ref, dtb_ref, al_ref, dsk_ref, ng_ref, e_ref, ec_ref, sp_ref, dy_ref,
             dx_ref, db_ref, dc_ref, dz_ref, ddt_ref, dnar_ref, dwide_ref, ds_scr):
        c, g = pl.program_id(0), pl.program_id(1)
        wide_l = [slice(i * M2_GW, (i + 1) * M2_GW) for i in range(M2_GB)]
        nar_l = [slice(i * LANES, (i + 1) * LANES) for i in range(M2_GB)]
        groups = pl.ds(g * M2_GB, M2_GB)
        wide_s = lambda ref: jnp.concatenate([ref[:, ls][None] for ls in wide_l], axis=0)
        nar_s = lambda ref: jnp.concatenate([ref[:, ls][None] for ls in nar_l], axis=0)

        @pl.when(jnp.logical_and(c == 0, g == 0))
        def _():
            dnar_ref[...] = jnp.zeros_like(dnar_ref)

        @pl.when(c == 0)
        def _():
            ds_scr[groups] = jnp.zeros((M2_GB, M2_STATE, M2_GW), F32)
            dwide_ref[groups] = jnp.zeros((M2_GB, SUBLANES, M2_GW), F32)

        @pl.when(g == 0)
        def _():
            ddt_ref[...] = jnp.zeros_like(ddt_ref)

        e_m, ec_m = e_ref[...], ec_ref[...]
        f = lambda x, bm, cm, z, dtr, S, dtb, al, dsk, ng: _m2_chunk(x, bm, cm, z, dtr, S, dtb, al, dsk, ng, e_m, ec_m)
        _, vjp = jax.vjp(f, wide_s(x_ref), nar_s(b_ref), nar_s(c_ref), wide_s(z_ref), dt_ref[...], sp_ref[...], dtb_ref[...],
                         al_ref[...], wide_s(dsk_ref), wide_s(ng_ref))
        dx, db, dc, dz, ddt, ds, ddtb, dal, ddsk, dng = vjp((wide_s(dy_ref), ds_scr[groups]))
        for i in range(M2_GB):
            dx_ref[:, wide_l[i]] = dx[i]
            db_ref[:, nar_l[i]] = db[i]
            dc_ref[:, nar_l[i]] = dc[i]
            dz_ref[:, wide_l[i]] = dz[i]
        ds_scr[groups] = ds
        ddt_ref[...] += ddt
        dnar_ref[...] += jnp.concatenate([ddtb, dal, jnp.zeros((SUBLANES - 2, LANES), F32)], axis=0)
        dwide_ref[groups] += jnp.concatenate([ddsk, dng, jnp.zeros((M2_GB, SUBLANES - 2, M2_GW), F32)], axis=1)

    return pl.pallas_call(
        body, name="m2_bwd", grid=(nc, M2_GROUPS // M2_GB),
        in_specs=[wide(0), nar(2 * M2_GROUPS), nar(3 * M2_GROUPS), wide(0), dts, v128, v128, v256, v256, es, ecs, st, wide(0)],
        out_specs=[wide(0), nar(0), nar(0), wide(0), pl.BlockSpec((CHUNK, LANES), lambda c, g: (nc - 1 - c, 0)),
                   pl.BlockSpec((SUBLANES, LANES), lambda c, g: (0, 0)),
                   pl.BlockSpec((M2_GROUPS, SUBLANES, M2_GW), lambda c, g: (0, 0, 0))],
        out_shape=[jax.ShapeDtypeStruct((L, M2_INNER), F32), jax.ShapeDtypeStruct((L, M2_GROUPS * M2_STATE), F32),
                   jax.ShapeDtypeStruct((L, M2_GROUPS * M2_STATE), F32), jax.ShapeDtypeStruct((L, M2_INNER), F32),
                   jax.ShapeDtypeStruct((L, LANES), F32), jax.ShapeDtypeStruct((SUBLANES, LANES), F32),
                   jax.ShapeDtypeStruct((M2_GROUPS, SUBLANES, M2_GW), F32)],
        scratch_shapes=[pltpu.VMEM((M2_GROUPS, M2_STATE, M2_GW), F32)],
        compiler_params=_params("arbitrary", "arbitrary"),
    )(xbc, xbc, xbc, p, p, dtb, alog, dsk, ng, e, ecol, sprev, dy)


def _m2_selectors():
    e = np.zeros((LANES, M2_INNER), np.float32)
    ecol = np.zeros((LANES, M2_HEADS * LANES), np.float32)
    for h in range(M2_HEADS):
        e[h, M2_HD * h:M2_HD * (h + 1)] = 1.0
        ecol[h, LANES * h:LANES * (h + 1)] = 1.0
    return jnp.asarray(e), jnp.asarray(ecol)


S5_NS = S5_GROUPS * S5_STATE // S5_BLOCKS
S5_ROWS = 256
GELU_C = math.sqrt(2.0 / math.pi)


def _gelu(x):
    return 0.5 * x * (1.0 + jnp.tanh(GELU_C * (x + 0.044715 * x * x * x)))


def _gelu_grad(x):
    t = jnp.tanh(GELU_C * (x + 0.044715 * x * x * x))
    return 0.5 * (1.0 + t) + 0.5 * x * (1.0 - t * t) * GELU_C * (1.0 + 3.0 * 0.044715 * x * x)


def _s5_scan(re_ref, im_ref, pw_re, pw_im, nrows, reverse, states=None):
    n = re_ref.shape[1]
    row = _iota2((SUBLANES, n), 0)
    steps = []
    for d in (1, 2, 4):
        keep = (row < SUBLANES - d) if reverse else (row >= d)
        steps.append(((SUBLANES - d) if reverse else d, jnp.where(keep, pw_re[d - 1:d, :], 0.0), jnp.where(keep, pw_im[d - 1:d, :], 0.0)))
    if reverse:
        cw_re = jnp.concatenate([pw_re[SUBLANES - 1 - k:SUBLANES - k, :] for k in range(SUBLANES)], axis=0)
        cw_im = jnp.concatenate([pw_im[SUBLANES - 1 - k:SUBLANES - k, :] for k in range(SUBLANES)], axis=0)
    else:
        cw_re, cw_im = pw_re, pw_im
    edge = 0 if reverse else SUBLANES - 1
    ngroups = nrows // SUBLANES

    def step(i, carry):
        cr, ci, ar, ai = carry
        gi = (ngroups - 1 - i) if reverse else i
        r0 = pl.multiple_of(gi * SUBLANES, SUBLANES)
        xr, xi = re_ref[pl.ds(r0, SUBLANES), :], im_ref[pl.ds(r0, SUBLANES), :]
        for shift, pr, pi in steps:
            sr, si = pltpu.roll(xr, shift, axis=0), pltpu.roll(xi, shift, axis=0)
            xr, xi = xr + (pr * sr - pi * si), xi + (pr * si + pi * sr)
        xr, xi = xr + (cw_re * cr - cw_im * ci), xi + (cw_re * ci + cw_im * cr)
        re_ref[pl.ds(r0, SUBLANES), :] = xr
        im_ref[pl.ds(r0, SUBLANES), :] = xi
        if states is not None:
            p0 = pl.multiple_of(jnp.maximum(r0 - SUBLANES, 0), SUBLANES)
            live = jnp.where(gi > 0, 1.0, 0.0)
            prev = [jnp.where(row >= 1, pltpu.roll(ref[pl.ds(r0, SUBLANES), :], 1, axis=0),
                              live * pltpu.roll(ref[pl.ds(p0, SUBLANES), :], 1, axis=0)) for ref in states]
            ar, ai = ar + (prev[0] * xr + prev[1] * xi), ai + (prev[0] * xi - prev[1] * xr)
        return (jnp.sum(jnp.where(row == edge, xr, 0.0), axis=0, keepdims=True),
                jnp.sum(jnp.where(row == edge, xi, 0.0), axis=0, keepdims=True), ar, ai)

    z = jnp.zeros((1, n), F32)
    za = jnp.zeros((SUBLANES, n) if states is not None else (1, n), F32)
    _, _, ar, ai = lax.fori_loop(0, ngroups, step, (z, z, za, za))
    return jnp.sum(ar, axis=0, keepdims=True), jnp.sum(ai, axis=0, keepdims=True)


def _s5_project_in(u_ref, bm_ref, re_ref, im_ref, L):
    def step(i, carry):
        r0 = pl.multiple_of(i * S5_ROWS, S5_ROWS)
        bu = _dot(u_ref[pl.ds(r0, S5_ROWS), :], bm_ref[...])
        re_ref[pl.ds(r0, S5_ROWS), :] = bu[:, :S5_NS]
        im_ref[pl.ds(r0, S5_ROWS), :] = bu[:, S5_NS:]
        return carry

    lax.fori_loop(0, L // S5_ROWS, step, 0)


def _s5_specs(L):
    col = pl.BlockSpec((L, LANES), lambda j: (0, j))
    bm = pl.BlockSpec((None, LANES, 2 * S5_NS), lambda j: (j, 0, 0))
    cm = pl.BlockSpec((None, 2 * S5_NS, LANES), lambda j: (j, 0, 0))
    pw = pl.BlockSpec((None, SUBLANES, S5_NS), lambda j: (j, 0, 0))
    vec = pl.BlockSpec((1, LANES), lambda j: (0, j))
    return col, bm, cm, pw, vec


def _s5_fwd(u, bmat, cmat, pw_re, pw_im, dsk):
    L = u.shape[0]
    col, bm, cm, pw, vec = _s5_specs(L)

    def body(u_ref, bm_ref, cm_ref, pr_ref, pi_ref, d_ref, y_ref, re_scr, im_scr):
        _s5_project_in(u_ref, bm_ref, re_scr, im_scr, L)
        _s5_scan(re_scr, im_scr, pr_ref[...], pi_ref[...], L, False)

        def step(i, carry):
            r0 = pl.multiple_of(i * S5_ROWS, S5_ROWS)
            rows = pl.ds(r0, S5_ROWS)
            y = _dot(re_scr[rows, :], cm_ref[:S5_NS, :]) + _dot(im_scr[rows, :], cm_ref[S5_NS:, :]) + d_ref[...] * u_ref[rows, :]
            y_ref[rows, :] = _gelu(y)
            return carry

        lax.fori_loop(0, L // S5_ROWS, step, 0)

    return pl.pallas_call(
        body, name="s5_fwd", grid=(S5_BLOCKS,),
        in_specs=[col, bm, cm, pw, pw, vec], out_specs=col,
        out_shape=jax.ShapeDtypeStruct((L, D_MODEL), F32),
        scratch_shapes=[pltpu.VMEM((L, S5_NS), F32)] * 2,
        compiler_params=_params("parallel"),
    )(u, bmat, cmat, pw_re, pw_im, dsk)


def _s5_bwd(u, bmat, cmat, pw_re, pw_im, dsk, dyg):
    L = u.shape[0]
    col, bm, cm, pw, vec = _s5_specs(L)

    def body(u_ref, bm_ref, cm_ref, pr_ref, pi_ref, d_ref, dy_ref, du_ref, dbm_ref, dcm_ref, dlam_ref, dd_ref,
             re_scr, im_scr, gr_scr, gi_scr, dyp_scr):
        _s5_project_in(u_ref, bm_ref, re_scr, im_scr, L)
        _s5_scan(re_scr, im_scr, pr_ref[...], pi_ref[...], L, False)

        def step(i, carry):
            dcr, dci, dd = carry
            r0 = pl.multiple_of(i * S5_ROWS, S5_ROWS)
            rows = pl.ds(r0, S5_ROWS)
            sr, si, uu = re_scr[rows, :], im_scr[rows, :], u_ref[rows, :]
            y = _dot(sr, cm_ref[:S5_NS, :]) + _dot(si, cm_ref[S5_NS:, :]) + d_ref[...] * uu
            dyp = dy_ref[rows, :] * _gelu_grad(y)
            dyp_scr[rows, :] = dyp
            gr_scr[rows, :] = _nt(dyp, cm_ref[:S5_NS, :])
            gi_scr[rows, :] = _nt(dyp, cm_ref[S5_NS:, :])
            return dcr + _tn(sr, dyp), dci + _tn(si, dyp), dd + jnp.sum(dyp * uu, axis=0, keepdims=True)

        zc = jnp.zeros((S5_NS, LANES), F32)
        dcr, dci, dd = lax.fori_loop(0, L // S5_ROWS, step, (zc, zc, jnp.zeros((1, LANES), F32)))
        dcm_ref[:S5_NS, :] = dcr
        dcm_ref[S5_NS:, :] = dci
        dd_ref[...] = dd

        ar, ai = _s5_scan(gr_scr, gi_scr, pr_ref[...], -pi_ref[...], L, True, states=(re_scr, im_scr))
        dlam_ref[...] = jnp.concatenate([ar, ai, jnp.zeros((SUBLANES - 2, S5_NS), F32)], axis=0)

        def in_step(i, carry):
            dbr, dbi = carry
            r0 = pl.multiple_of(i * S5_ROWS, S5_ROWS)
            rows = pl.ds(r0, S5_ROWS)
            gr, gi, uu = gr_scr[rows, :], gi_scr[rows, :], u_ref[rows, :]
            du_ref[rows, :] = dyp_scr[rows, :] * d_ref[...] + _nt(gr, bm_ref[:, :S5_NS]) + _nt(gi, bm_ref[:, S5_NS:])
            return dbr + _tn(uu, gr), dbi + _tn(uu, gi)

        zb = jnp.zeros((LANES, S5_NS), F32)
        dbr, dbi = lax.fori_loop(0, L // S5_ROWS, in_step, (zb, zb))
        dbm_ref[:, :S5_NS] = dbr
        dbm_ref[:, S5_NS:] = dbi

    return pl.pallas_call(
        body, name="s5_bwd", grid=(S5_BLOCKS,),
        in_specs=[col, bm, cm, pw, pw, vec, col], out_specs=[col, bm, cm, pw, vec],
        out_shape=[jax.ShapeDtypeStruct((L, D_MODEL), F32), jax.ShapeDtypeStruct((S5_BLOCKS, LANES, 2 * S5_NS), F32),
                   jax.ShapeDtypeStruct((S5_BLOCKS, 2 * S5_NS, LANES), F32),
                   jax.ShapeDtypeStruct((S5_BLOCKS, SUBLANES, S5_NS), F32), jax.ShapeDtypeStruct((1, D_MODEL), F32)],
        scratch_shapes=[pltpu.VMEM((L, S5_NS), F32)] * 4 + [pltpu.VMEM((L, LANES), F32)],
        compiler_params=_params("parallel"),
    )(u, bmat, cmat, pw_re, pw_im, dsk, dyg)


def _s5_discretize(lam_re, lam_im, log_dt, b_re, b_im, e16):
    dt = jnp.exp(log_dt)
    zr, zi = lam_re * dt, lam_im * dt
    mag = jnp.exp(zr)
    lbr, lbi = mag * jnp.cos(zi), mag * jnp.sin(zi)
    den = lam_re * lam_re + lam_im * lam_im
    nr, ni = lbr - 1.0, lbi
    cr = (nr * lam_re + ni * lam_im) / den
    ci = (ni * lam_re - nr * lam_im) / den
    crw, ciw = _pick(cr, e16), _pick(ci, e16)
    return lbr, lbi, crw * b_re - ciw * b_im, crw * b_im + ciw * b_re


def _s5_params_fwd(lam_re, lam_im, log_dt, b_re, b_im, e16):
    def body(lr, li, ld, br, bi, e, o1, o2, o3, o4):
        for o, val in zip((o1, o2, o3, o4), _s5_discretize(lr[...], li[...], ld[...], br[...], bi[...], e[...])):
            o[...] = val

    g, p, n = S5_GROUPS, S5_STATE, S5_STATE * S5_GROUP
    return pl.pallas_call(
        body, name="s5_params_fwd",
        out_shape=[jax.ShapeDtypeStruct((g, p), F32)] * 2 + [jax.ShapeDtypeStruct((g, n), F32)] * 2,
        compiler_params=_params(),
    )(lam_re, lam_im, log_dt, b_re, b_im, e16)


def _s5_params_bwd(lam_re, lam_im, log_dt, b_re, b_im, e16, cts):
    def body(lr, li, ld, br, bi, e, c1, c2, c3, c4, o1, o2, o3, o4, o5):
        e_m = e[...]
        f = lambda a, b, c, d, g: _s5_discretize(a, b, c, d, g, e_m)
        _, vjp = jax.vjp(f, lr[...], li[...], ld[...], br[...], bi[...])
        for o, val in zip((o1, o2, o3, o4, o5), vjp((c1[...], c2[...], c3[...], c4[...]))):
            o[...] = val

    g, p, n = S5_GROUPS, S5_STATE, S5_STATE * S5_GROUP
    return pl.pallas_call(
        body, name="s5_params_bwd",
        out_shape=[jax.ShapeDtypeStruct((g, p), F32)] * 2 + [jax.ShapeDtypeStruct((g, 1), F32)]
        + [jax.ShapeDtypeStruct((g, n), F32)] * 2,
        compiler_params=_params(),
    )(lam_re, lam_im, log_dt, b_re, b_im, e16, *cts)


def _add_residual(acc, h):
    return (acc + h,)


def _mlp_fwd(i, h, g, w1, w2):
    hn = _rms_fwd(f"mlp{i}_norm", h, g)
    r = _mm(f"mlp{i}_up", hn, w1, "nn", (BF16,), epi=lambda acc: (jnp.square(jnp.maximum(acc, 0.0)),))
    return _mm(f"mlp{i}_down", r, w2, "nn", (F32,), epi=_add_residual, extras=(h,)), (h, hn, r)


def _mlp_bwd(i, dh_out, saved, g, w1, w2):
    h, hn, r = saved
    dw2 = _mm(f"mlp{i}_dw2", r, dh_out, "tn", (BF16,))
    da = _mm(f"mlp{i}_da", dh_out, w2, "nt", (BF16,), epi=lambda acc, rr: (acc * (2.0 * jnp.sqrt(rr.astype(F32))),), extras=(r,))
    dw1 = _mm(f"mlp{i}_dw1", hn, da, "tn", (BF16,))
    dhn = _mm(f"mlp{i}_dhn", da, w1, "nt", (F32,))
    dh, dg = _rms_bwd(f"mlp{i}_dnorm", h, g, dhn, dh_out)
    return dh, dg[0], dw1, dw2


def _lanes(v, n):
    return jnp.broadcast_to(v.reshape(n, 1, 1), (n, 1, LANES))


def _gdn_fwd_layer(i, h, g, w_in, conv_w, a_log, dt_bias, o_g, w_out):
    hn = _rms_fwd(f"gdn{i}_norm", h, g)
    p = _mm(f"gdn{i}_in", hn, w_in, "nn", (F32,))
    zb = jnp.zeros((1, D_MODEL), F32)
    qkv = [_conv_fwd(f"gdn{i}_conv{t}", p, t * D_MODEL, conv_w[:, t * D_MODEL:(t + 1) * D_MODEL], zb) for t in range(3)]
    ea, eb = _gdn_selectors()
    y, sprev = _gdn_fwd(*qkv, p, _lanes(a_log, GDN_HEADS), _lanes(dt_bias, GDN_HEADS), o_g.reshape(1, LANES), ea, eb)
    return _mm(f"gdn{i}_out", y, w_out, "nn", (F32,), epi=_add_residual, extras=(h,)), (h, hn, p, qkv, y, sprev)


def _gdn_bwd_layer(i, dh_out, saved, g, w_in, conv_w, a_log, dt_bias, o_g, w_out):
    h, hn, p, qkv, y, sprev = saved
    dy = _mm(f"gdn{i}_dy", dh_out, w_out, "nt", (F32,))
    dw_out = _mm(f"gdn{i}_dwout", y, dh_out, "tn", (BF16,))
    ea, eb = _gdn_selectors()
    dq, dk, dv, dgate, dab, dpar = _gdn_bwd(*qkv, p, _lanes(a_log, GDN_HEADS), _lanes(dt_bias, GDN_HEADS),
                                            o_g.reshape(1, LANES), ea, eb, sprev, dy)
    zb = jnp.zeros((1, D_MODEL), F32)
    dpre, dcw = [], []
    for t, d in enumerate((dq, dk, dv)):
        dx, dw, _ = _conv_bwd(f"gdn{i}_dconv{t}", p, t * D_MODEL, conv_w[:, t * D_MODEL:(t + 1) * D_MODEL], zb, d)
        dpre.append(dx)
        dcw.append(dw)
    dp = jnp.concatenate(dpre + [dgate, dab], axis=1).astype(BF16)
    dw_in = _mm(f"gdn{i}_dwin", hn, dp, "tn", (BF16,))[:, :GDN_IN]
    dhn = _mm(f"gdn{i}_dhn", dp, w_in, "nt", (F32,))
    dh, dg = _rms_bwd(f"gdn{i}_dnorm", h, g, dhn, dh_out)
    grads = dict(w_in=dw_in, conv_w=jnp.concatenate(dcw, axis=1), a_log=jnp.sum(dpar[:, 0, :], axis=-1),
                 dt_bias=jnp.sum(dpar[:, 1, :], axis=-1), o_norm_g=jnp.sum(dpar[:, 2, :], axis=0), w_out=dw_out)
    return dh, dg[0], grads


def _m2_vectors(dt_bias, a_log, d_skip, norm_g):
    pad = lambda v: jnp.pad(v, (0, LANES - M2_HEADS)).reshape(1, LANES)
    return pad(dt_bias), pad(a_log), jnp.repeat(d_skip, M2_HD).reshape(1, M2_INNER), norm_g.reshape(1, M2_INNER)


def _m2_fwd_layer(h, g, w_in, conv_w, conv_b, dt_bias, a_log, d_skip, norm_g, w_out):
    hn = _rms_fwd("m2_norm", h, g)
    p = _mm("m2_in", hn, w_in, "nn", (F32,))
    xbc = _conv_fwd("m2_conv", p, M2_INNER, conv_w, conv_b.reshape(1, M2_CONV_CH))
    e, ecol = _m2_selectors()
    y, sprev = _m2_fwd(xbc, p, *_m2_vectors(dt_bias, a_log, d_skip, norm_g), e, ecol)
    return _mm("m2_out", y, w_out, "nn", (F32,), epi=_add_residual, extras=(h,)), (h, hn, p, xbc, y, sprev)


def _m2_bwd_layer(dh_out, saved, g, w_in, conv_w, conv_b, dt_bias, a_log, d_skip, norm_g, w_out):
    h, hn, p, xbc, y, sprev = saved
    dy = _mm("m2_dy", dh_out, w_out, "nt", (F32,))
    dw_out = _mm("m2_dwout", y, dh_out, "tn", (BF16,))
    e, ecol = _m2_selectors()
    dx, db, dc, dz, ddt, dnar, dwide = _m2_bwd(xbc, p, *_m2_vectors(dt_bias, a_log, d_skip, norm_g), e, ecol, sprev, dy)
    dxbc, dcw, dcb = _conv_bwd("m2_dconv", p, M2_INNER, conv_w, conv_b.reshape(1, M2_CONV_CH),
                               jnp.concatenate([dx, db, dc], axis=1))
    dp = jnp.concatenate([dz, dxbc, ddt], axis=1).astype(BF16)
    dw_in = _mm("m2_dwin", hn, dp, "tn", (BF16,))[:, :M2_IN]
    dhn = _mm("m2_dhn", dp, w_in, "nt", (F32,))
    dh, dg = _rms_bwd("m2_dnorm", h, g, dhn, dh_out)
    grads = dict(w_in=dw_in, conv_w=dcw, conv_b=dcb[0], dt_bias=dnar[0, :M2_HEADS], a_log=dnar[1, :M2_HEADS],
                 d=jnp.sum(dwide[:, 0, :].reshape(M2_HEADS, M2_HD), axis=-1), norm_g=dwide[:, 1, :].reshape(M2_INNER),
                 w_out=dw_out)
    return dh, dg[0], grads


def _s5_selector():
    e16 = np.zeros((S5_STATE, S5_STATE * S5_GROUP), np.float32)
    for p in range(S5_STATE):
        e16[p, p * S5_GROUP:(p + 1) * S5_GROUP] = 1.0
    return jnp.asarray(e16)


def _s5_operands(lbr, lbi, bbr, bbi, c_re, c_im):
    eye = jnp.eye(S5_BLOCKS, dtype=F32)
    gpb = S5_GROUPS // S5_BLOCKS
    bd = lambda t: jnp.einsum("jgpk,gh->jgkhp", t.reshape(S5_BLOCKS, gpb, S5_STATE, S5_GROUP), eye).reshape(S5_BLOCKS, LANES, S5_NS)
    cd = lambda t: jnp.einsum("jgkp,gh->jgphk", t.reshape(S5_BLOCKS, gpb, S5_GROUP, S5_STATE), eye).reshape(S5_BLOCKS, S5_NS, LANES)
    bmat = jnp.concatenate([bd(bbr), bd(bbi)], axis=2).astype(BF16)
    cmat = jnp.concatenate([cd(c_re), -cd(c_im)], axis=1).astype(BF16)
    ar, ai = lbr.reshape(S5_BLOCKS, S5_NS), lbi.reshape(S5_BLOCKS, S5_NS)
    pr, pi = [ar], [ai]
    for _ in range(SUBLANES - 1):
        pr, pi = pr + [pr[-1] * ar - pi[-1] * ai], pi + [pr[-1] * ai + pi[-1] * ar]
    return bmat, cmat, jnp.stack(pr, axis=1), jnp.stack(pi, axis=1)


def _s5_fwd_layer(h, g, w_in, lam_re, lam_im, log_dt, b_re, b_im, c_re, c_im, d_skip, w_out):
    hn = _rms_fwd("s5_norm", h, g)
    u = _mm("s5_in", hn, w_in, "nn", (F32,))
    n = S5_STATE * S5_GROUP
    lbr, lbi, bbr, bbi = _s5_params_fwd(lam_re, lam_im, log_dt.reshape(S5_GROUPS, 1), b_re.reshape(S5_GROUPS, n),
                                        b_im.reshape(S5_GROUPS, n), _s5_selector())
    ops = _s5_operands(lbr, lbi, bbr, bbi, c_re, c_im)
    yg = _s5_fwd(u, *ops, d_skip.reshape(1, D_MODEL))
    ag = _mm("s5_out", yg, w_out, "nn", (F32,))
    return _glu_fwd(h, ag), (h, hn, u, ops, yg, ag)


def _s5_bwd_layer(dh_out, saved, g, w_in, lam_re, lam_im, log_dt, b_re, b_im, c_re, c_im, d_skip, w_out):
    h, hn, u, ops, yg, ag = saved
    dag = _glu_bwd(dh_out, ag)
    dw_out = _mm("s5_dwout", yg, dag, "tn", (BF16,))
    dyg = _mm("s5_dyg", dag, w_out, "nt", (F32,))
    du, dbmat, dcmat, dlam, ddsk = _s5_bwd(u, *ops, d_skip.reshape(1, D_MODEL), dyg)
    eye = jnp.eye(S5_BLOCKS, dtype=F32)
    gpb = S5_GROUPS // S5_BLOCKS
    n = S5_STATE * S5_GROUP
    ub = lambda t: jnp.einsum("jgkhp,gh->jgpk", t.reshape(S5_BLOCKS, gpb, S5_GROUP, gpb, S5_STATE), eye).reshape(S5_GROUPS, n)
    uc = lambda t: jnp.einsum("jgphk,gh->jgkp", t.reshape(S5_BLOCKS, gpb, S5_STATE, gpb, S5_GROUP), eye).reshape(c_re.shape)
    cts = (dlam[:, 0, :].reshape(S5_GROUPS, S5_STATE), dlam[:, 1, :].reshape(S5_GROUPS, S5_STATE),
           ub(dbmat[:, :, :S5_NS]), ub(dbmat[:, :, S5_NS:]))
    dlr, dli, dld, dbr, dbi = _s5_params_bwd(lam_re, lam_im, log_dt.reshape(S5_GROUPS, 1), b_re.reshape(S5_GROUPS, n),
                                             b_im.reshape(S5_GROUPS, n), _s5_selector(), cts)
    dw_in = _mm("s5_dwin", hn, du, "tn", (BF16,))
    dhn = _mm("s5_dhn", du, w_in, "nt", (F32,))
    dh, dg = _rms_bwd("s5_dnorm", h, g, dhn, dh_out)
    grads = dict(w_in=dw_in, lam_re=dlr, lam_im=dli, log_dt=dld[:, 0], b_re=dbr.reshape(b_re.shape), b_im=dbi.reshape(b_im.shape),
                 c_re=uc(dcmat[:, :S5_NS, :]), c_im=-uc(dcmat[:, S5_NS:, :]), d=ddsk[0], w_out=dw_out)
    return dh, dg[0], grads


MIXER_OF_LAYER = ("gdn", "s5", "m2", "gdn")
MIXER_INDEX = (0, 0, 0, 1)


def _mixer_args(W, i):
    kind, j = MIXER_OF_LAYER[i], MIXER_INDEX[i]
    if kind == "gdn":
        return tuple(W["gdn_" + k][j] for k in ("w_in", "conv_w", "a_log", "dt_bias", "o_norm_g", "w_out"))
    if kind == "s5":
        return tuple(W["s5_" + k][j] for k in ("w_in", "lam_re", "lam_im", "log_dt", "b_re", "b_im", "c_re", "c_im", "d", "w_out"))
    return tuple(W["m2_" + k][j] for k in ("w_in", "conv_w", "conv_b", "dt_bias", "a_log", "d", "norm_g", "w_out"))


def _local_step(x, target, W, on_layer_grads):
    h = x
    saved = []
    for i in range(DEPTH):
        kind = MIXER_OF_LAYER[i]
        args = _mixer_args(W, i)
        if kind == "gdn":
            h, sm = _gdn_fwd_layer(i, h, W["norm_mix_g"][i], *args)
        elif kind == "s5":
            h, sm = _s5_fwd_layer(h, W["norm_mix_g"][i], *args)
        else:
            h, sm = _m2_fwd_layer(h, W["norm_mix_g"][i], *args)
        h, sp = _mlp_fwd(i, h, W["norm_mlp_g"][i], W["mlp_w1"][i], W["mlp_w2"][i])
        saved.append((sm, sp))
    loss, dh, dgf = _loss_head(h, W["final_norm_g"], target)
    G = {"final_norm_g": dgf[0], "norm_mix_g": [None] * DEPTH, "norm_mlp_g": [None] * DEPTH,
         "mlp_w1": [None] * DEPTH, "mlp_w2": [None] * DEPTH}
    mix = {}
    for i in reversed(range(DEPTH)):
        kind = MIXER_OF_LAYER[i]
        sm, sp = saved[i]
        dh, G["norm_mlp_g"][i], G["mlp_w1"][i], G["mlp_w2"][i] = _mlp_bwd(i, dh, sp, W["norm_mlp_g"][i], W["mlp_w1"][i], W["mlp_w2"][i])
        args = _mixer_args(W, i)
        if kind == "gdn":
            dh, G["norm_mix_g"][i], gm = _gdn_bwd_layer(i, dh, sm, W["norm_mix_g"][i], *args)
        elif kind == "s5":
            dh, G["norm_mix_g"][i], gm = _s5_bwd_layer(dh, sm, W["norm_mix_g"][i], *args)
        else:
            dh, G["norm_mix_g"][i], gm = _m2_bwd_layer(dh, sm, W["norm_mix_g"][i], *args)
        j = MIXER_INDEX[i]
        on_layer_grads(i, {("mlp_w1", i): G["mlp_w1"][i], ("mlp_w2", i): G["mlp_w2"][i],
                           (kind + "_w_in", j): gm["w_in"], (kind + "_w_out", j): gm["w_out"]})
        for k, v in gm.items():
            mix.setdefault(kind + "_" + k, {})[j] = v
    for k, d in mix.items():
        G[k] = [d[j] for j in sorted(d)]
    return loss, dh, {k: jnp.stack(v) if isinstance(v, list) else v for k, v in G.items() if k not in BIG}


ADAM_ROWS = 128
ADAM_COLS = 128


def _adamw(name, w, g, m, v):
    R, C = w.shape
    if R % ADAM_ROWS == 0:
        grid, blk = (R // ADAM_ROWS,), pl.BlockSpec((ADAM_ROWS, C), lambda i: (i, 0))
    else:
        grid, blk = (C // ADAM_COLS,), pl.BlockSpec((R, ADAM_COLS), lambda j: (0, j))

    def body(w_ref, g_ref, m_ref, v_ref, d_ref, mo_ref, vo_ref):
        gg = g_ref[...]
        mn = ADAM_B1 * m_ref[...] + (1.0 - ADAM_B1) * gg
        vn = ADAM_B2 * v_ref[...] + (1.0 - ADAM_B2) * (gg * gg)
        m_hat = mn / (1.0 - ADAM_B1 ** ADAM_STEP)
        v_hat = vn / (1.0 - ADAM_B2 ** ADAM_STEP)
        d_ref[...] = -ADAM_LR * (m_hat / (jnp.sqrt(v_hat) + ADAM_EPS) + ADAM_WD * w_ref[...])
        mo_ref[...] = mn
        vo_ref[...] = vn

    return pl.pallas_call(
        body, name=name, grid=grid, in_specs=[blk] * 4, out_specs=[blk] * 3,
        out_shape=[jax.ShapeDtypeStruct((R, C), F32)] * 3, compiler_params=_params("parallel"),
    )(w, g, m, v)


MESH = pl.DeviceIdType.MESH
ANY = pl.BlockSpec(memory_space=pl.ANY)
N_CHIPS = 4
N_DEV = 8


def _position():
    return lax.axis_index("x"), lax.axis_index("y"), lax.axis_index("c")


GATHER_ID = 1
EXCHANGE_IDS = {0: 4, 1: 5, 2: 6, 3: 7}


LINK_SLOWDOWN = 40


def _link_cost(link_bytes):
    return pl.CostEstimate(flops=0, transcendentals=0, bytes_accessed=LINK_SLOWDOWN * link_bytes)


def _gather_body(w_refs, out_refs, send_sems, recv_sems):
    x, y, c = _position()
    sibling = (x, y, 1 - c)
    chips = [(1 - x, y), (x, 1 - y), (1 - x, 1 - y)]
    firsts, passes = [], []
    for t, (w_ref, out_ref) in enumerate(zip(w_refs, out_refs)):
        half = w_ref.shape[0] // 2

        def piece(cx, cy, hc, out_ref=out_ref, half=half):
            return out_ref.at[2 * cx + cy, pl.ds(hc * half, half), :]

        def copy(k, src, dst, to, t=t):
            return pltpu.make_async_remote_copy(src_ref=src, dst_ref=dst, send_sem=send_sems.at[6 * t + k],
                                                recv_sem=recv_sems.at[6 * t + k], device_id=to, device_id_type=MESH)

        first = [copy(j, w_ref.at[pl.ds(c * half, half), :], piece(x, y, c), (*chip, c)) for j, chip in enumerate(chips)]
        for cp in first:
            cp.start()
        firsts.append((first, piece, copy))
    for first, piece, copy in firsts:
        passed = [copy(3 + j, piece(*chip, c), piece(*chip, c), sibling) for j, chip in enumerate(chips)]
        for j, chip in enumerate(chips):
            copy(j, piece(*chip, c), piece(*chip, c), sibling).wait_recv()
            passed[j].start()
        passes.append(passed)
    for (first, piece, copy), passed in zip(firsts, passes):
        for j, chip in enumerate(chips):
            copy(3 + j, piece(*chip, 1 - c), piece(*chip, 1 - c), sibling).wait_recv()
        for cp in first + passed:
            cp.wait_send()


def _gather_shards(wps):
    n = len(wps)

    def body(*refs):
        _gather_body(refs[:n], refs[n:2 * n], *refs[2 * n:])

    return pl.pallas_call(
        body, name="gather_shards", in_specs=[ANY] * n, out_specs=[ANY] * n,
        out_shape=[jax.ShapeDtypeStruct((N_CHIPS, *wp.shape), wp.dtype) for wp in wps],
        scratch_shapes=[pltpu.SemaphoreType.DMA((6 * n,)), pltpu.SemaphoreType.DMA((6 * n,))],
    )(*wps)


def _gather_shards_later(wps):
    n = len(wps)
    w_refs = [jax.new_ref(wp, memory_space=pltpu.MemorySpace.HBM) for wp in wps]
    out_refs = [jax.empty_ref(jax.ShapeDtypeStruct((N_CHIPS, *wp.shape), wp.dtype), memory_space=pltpu.MemorySpace.HBM)
                for wp in wps]

    @pl.kernel(mesh=plsc.ScalarSubcoreMesh(axis_name="sequencer", num_cores=1), name="gather_shards_later",
               scratch_types=(pltpu.SemaphoreType.DMA((6 * n,)), pltpu.SemaphoreType.DMA((6 * n,))),
               cost_estimate=_link_cost(3 * sum(wp.size * wp.dtype.itemsize for wp in wps)),
               compiler_params=pltpu.CompilerParams(collective_id=GATHER_ID))
    def launch(send_sems, recv_sems):
        x, y, c = _position()
        barrier = pltpu.get_barrier_semaphore()
        for peer in [(x, y, 1 - c), (1 - x, y, c), (x, 1 - y, c), (1 - x, 1 - y, c)]:
            pl.semaphore_signal(barrier, inc=1, device_id=peer, device_id_type=MESH)
        pl.semaphore_wait(barrier, 4)
        _gather_body(w_refs, out_refs, send_sems, recv_sems)

    launch()
    return [r[...] for r in out_refs]


def _pair_exchange(name, gps):
    n = len(gps)

    def body(*refs):
        g_refs, out_refs, (send_sems, recv_sems) = refs[:n], refs[n:2 * n], refs[2 * n:]
        x, y, c = _position()
        copies = []
        for t, (g_ref, out_ref) in enumerate(zip(g_refs, out_refs)):
            half = g_ref.shape[1] // 2
            copies += [pltpu.make_async_remote_copy(
                src_ref=g_ref.at[k, pl.ds((1 - c) * half, half), :], dst_ref=out_ref.at[k], send_sem=send_sems.at[N_CHIPS * t + k],
                recv_sem=recv_sems.at[N_CHIPS * t + k], device_id=(x, y, 1 - c), device_id_type=MESH) for k in range(N_CHIPS)]
        for cp in copies:
            cp.start()
        for cp in copies:
            cp.wait()

    return pl.pallas_call(
        body, name=name, in_specs=[ANY] * n, out_specs=[ANY] * n,
        out_shape=[jax.ShapeDtypeStruct((N_CHIPS, gp.shape[1] // 2, gp.shape[2]), gp.dtype) for gp in gps],
        scratch_shapes=[pltpu.SemaphoreType.DMA((N_CHIPS * n,)), pltpu.SemaphoreType.DMA((N_CHIPS * n,))],
    )(*gps)


SUM_ROWS = (256, 128)


def _pair_sum(name, gp, got, core):
    n, R, C = gp.shape
    half = R // 2
    tr = _tile(half, SUM_ROWS)
    nb = half // tr

    def body(core_ref, g_ref, r_ref, o_ref):
        o_ref[...] = (g_ref[...].astype(F32) + r_ref[...].astype(F32)).astype(o_ref.dtype)

    return pl.pallas_call(
        body, name=name,
        grid_spec=pltpu.PrefetchScalarGridSpec(
            num_scalar_prefetch=1, grid=(n, nb),
            in_specs=[pl.BlockSpec((None, tr, C), lambda k, i, core_ref: (k, core_ref[0] * nb + i, 0)),
                      pl.BlockSpec((None, tr, C), lambda k, i, core_ref: (k, i, 0))],
            out_specs=pl.BlockSpec((None, tr, C), lambda k, i, core_ref: (k, i, 0))),
        out_shape=jax.ShapeDtypeStruct((n, half, C), gp.dtype), compiler_params=_params("parallel", "parallel"),
    )(core, gp, got)


def _chip_exchange_body(t_refs, out_refs, send_sems, recv_sems):
    x, y, c = _position()
    chips = [(1 - x, y), (x, 1 - y), (1 - x, 1 - y)]
    copies, waits = [], []
    for t, (t_ref, out_ref) in enumerate(zip(t_refs, out_refs)):
        for j, (cx, cy) in enumerate(chips):
            sems = dict(send_sem=send_sems.at[3 * t + j], recv_sem=recv_sems.at[3 * t + j], device_id=(cx, cy, c),
                        device_id_type=MESH)
            copies.append(pltpu.make_async_remote_copy(src_ref=t_ref.at[2 * cx + cy], dst_ref=out_ref.at[2 * x + y], **sems))
            waits.append(pltpu.make_async_remote_copy(src_ref=t_ref.at[2 * cx + cy], dst_ref=out_ref.at[2 * cx + cy], **sems))
    for cp in copies:
        cp.start()
    for cp in waits:
        cp.wait_recv()
    for cp in copies:
        cp.wait_send()


def _chip_exchange_later(ts, layer):
    n = len(ts)
    t_refs = [jax.new_ref(t, memory_space=pltpu.MemorySpace.HBM) for t in ts]
    out_refs = [jax.empty_ref(jax.ShapeDtypeStruct(t.shape, t.dtype), memory_space=pltpu.MemorySpace.HBM) for t in ts]

    @pl.kernel(mesh=plsc.ScalarSubcoreMesh(axis_name="sequencer", num_cores=1), name=f"chip_exchange_later{layer}",
               scratch_types=(pltpu.SemaphoreType.DMA((3 * n,)), pltpu.SemaphoreType.DMA((3 * n,))),
               cost_estimate=_link_cost(3 * sum(t.size * t.dtype.itemsize for t in ts) // N_CHIPS),
               compiler_params=pltpu.CompilerParams(collective_id=EXCHANGE_IDS[layer]))
    def launch(send_sems, recv_sems):
        x, y, c = _position()
        barrier = pltpu.get_barrier_semaphore()
        for peer in [(1 - x, y, c), (x, 1 - y, c), (1 - x, 1 - y, c)]:
            pl.semaphore_signal(barrier, inc=1, device_id=peer, device_id_type=MESH)
        pl.semaphore_wait(barrier, 3)
        _chip_exchange_body(t_refs, out_refs, send_sems, recv_sems)

    launch()
    return [r[...] for r in out_refs]


def _chip_sum(name, t, got, ids):
    n, H, C = t.shape
    tr = _tile(H, SUM_ROWS)
    nb = H // tr

    def body(ids_ref, t_ref, r_ref, o_ref):
        own = t_ref[...].astype(F32)
        acc = jnp.where(ids_ref[0] == 0, own, r_ref[0].astype(F32))
        for k in range(1, n):
            acc = acc + jnp.where(ids_ref[0] == k, own, r_ref[k].astype(F32))
        o_ref[...] = acc

    return pl.pallas_call(
        body, name=name,
        grid_spec=pltpu.PrefetchScalarGridSpec(
            num_scalar_prefetch=1, grid=(nb,),
            in_specs=[pl.BlockSpec((None, tr, C), lambda i, ids_ref: (ids_ref[0], i, 0)),
                      pl.BlockSpec((n, tr, C), lambda i, ids_ref: (0, i, 0))],
            out_specs=pl.BlockSpec((tr, C), lambda i, ids_ref: (ids_ref[1] * nb + i, 0))),
        out_shape=jax.ShapeDtypeStruct((2 * H, C), F32), compiler_params=_params("parallel"),
    )(ids, t, got)


def _sum_pieces(name, pieces):
    n, R, C = pieces.shape
    tr = _tile(R, (256, 128, SUBLANES))

    def body(p_ref, o_ref):
        acc = p_ref[0].astype(F32)
        for s in range(1, n):
            acc = acc + p_ref[s].astype(F32)
        o_ref[...] = acc

    return pl.pallas_call(
        body, name=name, grid=(R // tr,),
        in_specs=[pl.BlockSpec((n, tr, C), lambda i: (0, i, 0))], out_specs=pl.BlockSpec((tr, C), lambda i: (i, 0)),
        out_shape=jax.ShapeDtypeStruct((R, C), F32), compiler_params=_params("parallel"),
    )(pieces)


def _swap_halves(name, ss):
    n = len(ss)

    def body(*refs):
        s_refs, out_refs, (send_sems, recv_sems) = refs[:n], refs[n:2 * n], refs[2 * n:]
        x, y, c = _position()
        copies, waits = [], []
        for t, (s_ref, out_ref) in enumerate(zip(s_refs, out_refs)):
            half = s_ref.shape[0] // 2
            sems = dict(send_sem=send_sems.at[t], recv_sem=recv_sems.at[t], device_id=(x, y, 1 - c), device_id_type=MESH)
            mine = s_ref.at[pl.ds(c * half, half), :]
            copies.append(pltpu.make_async_remote_copy(src_ref=mine, dst_ref=out_ref.at[pl.ds(c * half, half), :], **sems))
            waits.append(pltpu.make_async_remote_copy(src_ref=mine, dst_ref=out_ref.at[pl.ds((1 - c) * half, half), :], **sems))
        for cp in copies:
            cp.start()
        for cp in waits:
            cp.wait_recv()
        for cp in copies:
            cp.wait_send()

    return pl.pallas_call(
        body, name=name, in_specs=[ANY] * n, out_specs=[ANY] * n, input_output_aliases={i: i for i in range(n)},
        out_shape=[jax.ShapeDtypeStruct(s_.shape, s_.dtype) for s_ in ss],
        scratch_shapes=[pltpu.SemaphoreType.DMA((n,)), pltpu.SemaphoreType.DMA((n,))],
    )(*ss)


def _gather_small(name, blk):
    m_per, n = blk.shape

    def body(x_ref, out_ref, send_sems, recv_sems, local_sem):
        x, y, c = _position()
        me, sibling = (x, y, c), (x, y, 1 - c)
        chips = [(1 - x, y), (x, 1 - y), (1 - x, 1 - y)]

        def rows(px, py, pc):
            return out_ref.at[pl.ds((4 * px + 2 * py + pc) * m_per, m_per), :]

        def copy(k, block, to, src=None):
            return pltpu.make_async_remote_copy(src_ref=rows(*block) if src is None else src, dst_ref=rows(*block),
                                                send_sem=send_sems.at[k], recv_sem=recv_sems.at[k], device_id=to, device_id_type=MESH)

        mine = pltpu.make_async_copy(x_ref, rows(*me), local_sem)
        mine.start()
        first = [copy(0, me, sibling, src=x_ref)] + [copy(1 + j, me, (*chip, c), src=x_ref) for j, chip in enumerate(chips)]
        for cp in first:
            cp.start()
        passed = [copy(4 + j, (*chip, c), sibling) for j, chip in enumerate(chips)]
        for j, chip in enumerate(chips):
            copy(1 + j, (*chip, c), me).wait_recv()
            passed[j].start()
        copy(0, sibling, me).wait_recv()
        for j, chip in enumerate(chips):
            copy(4 + j, (*chip, 1 - c), me).wait_recv()
        for cp in first + passed:
            cp.wait_send()
        mine.wait()

    return pl.pallas_call(
        body, name=name, out_shape=jax.ShapeDtypeStruct((N_DEV * m_per, n), blk.dtype),
        in_specs=[pl.BlockSpec(memory_space=pltpu.VMEM)], out_specs=pl.BlockSpec(memory_space=pltpu.VMEM),
        scratch_shapes=[pltpu.SemaphoreType.DMA((7,)), pltpu.SemaphoreType.DMA((7,)), pltpu.SemaphoreType.DMA],
        compiler_params=pltpu.CompilerParams(vmem_limit_bytes=VMEM_LIMIT_BYTES),
    )(blk)


WEIGHTS = ("norm_mix_g", "norm_mlp_g", "mlp_w1", "mlp_w2", "gdn_w_in", "gdn_conv_w", "gdn_a_log", "gdn_dt_bias", "gdn_o_norm_g",
           "gdn_w_out", "s5_w_in", "s5_lam_re", "s5_lam_im", "s5_log_dt", "s5_b_re", "s5_b_im", "s5_c_re", "s5_c_im", "s5_d",
           "s5_w_out", "m2_w_in", "m2_conv_w", "m2_conv_b", "m2_dt_bias", "m2_a_log", "m2_d", "m2_norm_g", "m2_w_out",
           "final_norm_g")
BIG = {"mlp_w1": 2, "mlp_w2": 1, "gdn_w_in": 2, "gdn_w_out": 1, "s5_w_in": 1, "s5_w_out": 2, "m2_w_in": 2, "m2_w_out": 1}
SMALL_CUT = {"gdn_conv_w": 2, "m2_conv_w": 2, "m2_conv_b": 1, "m2_norm_g": 1}
ROWS_MINOR = ("m2_w_in",)
LAYER_ITEMS = (
    ((("mlp_w1", 0), ("mlp_w2", 0), ("gdn_w_out", 0)), (("gdn_w_in", 0),)),
    ((("mlp_w1", 1), ("mlp_w2", 1), ("s5_w_in", 0)), (("s5_w_out", 0),)),
    ((("mlp_w1", 2), ("mlp_w2", 2), ("m2_w_out", 0)), (("m2_w_in", 0),)),
    ((("mlp_w1", 3), ("mlp_w2", 3), ("gdn_w_out", 1)), (("gdn_w_in", 1),)),
)


def _rows2d(a):
    return a.reshape(-1, a.shape[-1])


def _pack(arrays, cols, row_multiple, dtype):
    flat = jnp.concatenate([a.reshape(-1).astype(dtype) for a in arrays])
    n = -(-flat.shape[0] // (cols * row_multiple)) * cols * row_multiple
    return jnp.pad(flat, (0, n - flat.shape[0])).reshape(-1, cols)


def _unpack(packed, shapes):
    flat = packed.reshape(-1)
    out, off = [], 0
    for shp in shapes:
        n = math.prod(shp)
        out.append(flat[off:off + n].reshape(shp))
        off += n
    return out


def _split_rows(buf, shapes):
    out, off = [], 0
    for shp in shapes:
        rows = math.prod(shp[:-1])
        out.append(buf[off:off + rows].reshape(shp))
        off += rows
    return out


def _cut(a, axis, k):
    n = a.shape[axis] // N_CHIPS
    return lax.slice_in_dim(a, k * n, (k + 1) * n, axis=axis)


def kernel(x, norm_mix_g, norm_mlp_g, mlp_w1, mlp_w2, gdn_w_in, gdn_conv_w, gdn_a_log, gdn_dt_bias, gdn_o_norm_g, gdn_w_out, s5_w_in, s5_lam_re, s5_lam_im, s5_log_dt, s5_b_re, s5_b_im, s5_c_re, s5_c_im, s5_d, s5_w_out, m2_w_in, m2_conv_w, m2_conv_b, m2_dt_bias, m2_a_log, m2_d, m2_norm_g, m2_w_out, final_norm_g, loss_target, m_norm_mix_g, m_norm_mlp_g, m_mlp_w1, m_mlp_w2, m_gdn_w_in, m_gdn_conv_w, m_gdn_a_log, m_gdn_dt_bias, m_gdn_o_norm_g, m_gdn_w_out, m_s5_w_in, m_s5_lam_re, m_s5_lam_im, m_s5_log_dt, m_s5_b_re, m_s5_b_im, m_s5_c_re, m_s5_c_im, m_s5_d, m_s5_w_out, m_m2_w_in, m_m2_conv_w, m_m2_conv_b, m_m2_dt_bias, m_m2_a_log, m_m2_d, m_m2_norm_g, m_m2_w_out, m_final_norm_g, v_norm_mix_g, v_norm_mlp_g, v_mlp_w1, v_mlp_w2, v_gdn_w_in, v_gdn_conv_w, v_gdn_a_log, v_gdn_dt_bias, v_gdn_o_norm_g, v_gdn_w_out, v_s5_w_in, v_s5_lam_re, v_s5_lam_im, v_s5_log_dt, v_s5_b_re, v_s5_b_im, v_s5_c_re, v_s5_c_im, v_s5_d, v_s5_w_out, v_m2_w_in, v_m2_conv_w, v_m2_conv_b, v_m2_dt_bias, v_m2_a_log, v_m2_d, v_m2_norm_g, v_m2_w_out, v_final_norm_g):
    given = dict(locals())
    w = {n: given[n] for n in WEIGHTS}
    mom = {n: given["m_" + n] for n in WEIGHTS}
    var = {n: given["v_" + n] for n in WEIGHTS}
    big, small_cut = tuple(BIG), tuple(SMALL_CUT)
    small = tuple(n for n in WEIGHTS if n not in BIG)
    chip = 2 * lax.axis_index("x") + lax.axis_index("y")

    W = {n: [None] * w[n].shape[0] for n in big}

    def fetch(groups, gather):
        own = [jnp.concatenate([w[n][l] for n, l in grp]).astype(BF16) for grp in groups]
        for grp, mine, got in zip(groups, own, gather(own)):
            shapes = [w[n][l].shape for n, l in grp]
            per_chip = [_split_rows(jnp.where(chip == k, mine, got[k]), shapes) for k in range(N_CHIPS)]
            for i, (n, l) in enumerate(grp):
                m = jnp.concatenate([per_chip[k][i] for k in range(N_CHIPS)], axis=BIG[n] - 1)
                pad = {"gdn_w_in": GDN_IN_PAD - GDN_IN, "m2_w_in": M2_IN_PAD - M2_IN}.get(n, 0)
                W[n][l] = jnp.pad(m, ((0, 0), (0, pad))) if pad else m

    fetch([grp for items in LAYER_ITEMS[1:] for grp in items], _gather_shards_later)
    fetch(LAYER_ITEMS[0], _gather_shards)
    cut_blk = _pack([w[n] for n in small_cut], LANES, SUBLANES, F32)
    cut_all = _gather_small("gather_small_params", cut_blk).reshape(N_DEV, *cut_blk.shape)
    per_chip = [_unpack(cut_all[2 * k], [w[n].shape for n in small_cut]) for k in range(N_CHIPS)]
    W.update({n: jnp.concatenate([per_chip[k][i] for k in range(N_CHIPS)], axis=SMALL_CUT[n]) for i, n in enumerate(small_cut)})
    W.update({n: w[n] for n in small if n not in SMALL_CUT})

    core = lax.axis_index("c").astype(jnp.int32)
    ids = jnp.stack([chip.astype(jnp.int32), core])
    shard_grads = {}

    def reduce_layer(i, dws):
        groups = LAYER_ITEMS[i]
        gps = [jnp.stack([jnp.concatenate([_cut(dws[it], BIG[it[0]] - 1, k) for it in grp]).astype(BF16) for k in range(N_CHIPS)])
               for grp in groups]
        pairs = [_pair_sum(f"pair_sum{i}_{j}", gp, got, core.reshape(1))
                 for j, (gp, got) in enumerate(zip(gps, _pair_exchange(f"pair_exchange{i}", gps)))]
        sums = [_chip_sum(f"chip_sum{i}_{j}", t, got, ids) for j, (t, got) in enumerate(zip(pairs, _chip_exchange_later(pairs, i)))]
        for grp, g_shard in zip(groups, _swap_halves(f"swap_halves{i}", sums)):
            shard_grads.update(zip(grp, _split_rows(g_shard, [w[n][l].shape for n, l in grp])))

    loss, grad_x, G = _local_step(x[0], loss_target[0], W, reduce_layer)
    loss = lax.psum(loss[0, 0], ("x", "y", "c"))
    grads = {n: jnp.stack([shard_grads[n, l] for l in range(w[n].shape[0])]) for n in big}
    sg = _pack([G[n] for n in small], LANES, ADAM_ROWS, F32)
    sg_sum = _sum_pieces("sum_small_grads", _gather_small("gather_small_grads", sg).reshape(N_DEV, *sg.shape))
    for n, g in zip(small, _unpack(sg_sum, [G[n].shape for n in small])):
        if n in SMALL_CUT:
            width = g.shape[SMALL_CUT[n]] // N_CHIPS
            g = lax.dynamic_slice_in_dim(g, chip * width, width, axis=SMALL_CUT[n])
        grads[n] = g.reshape(w[n].shape)

    delta, new_m, new_v = {}, {}, {}
    for n in big:
        if n in ROWS_MINOR:
            as2d = lambda a: jnp.swapaxes(a, -1, -2).reshape(-1, a.shape[-2])
            back = lambda o: jnp.swapaxes(o.reshape(w[n].shape[0], w[n].shape[2], w[n].shape[1]), -1, -2)
        else:
            as2d = lambda a: a.reshape(-1, a.shape[-1])
            back = lambda o: o.reshape(w[n].shape)
        outs = _adamw("adamw_" + n, as2d(w[n]), as2d(grads[n]), as2d(mom[n]), as2d(var[n]))
        delta[n], new_m[n], new_v[n] = (back(o) for o in outs)
    packs = [_pack([t[n] for n in small], LANES, ADAM_ROWS, F32) for t in (w, grads, mom, var)]
    outs = _adamw("adamw_small", *packs)
    for t, o in zip((delta, new_m, new_v), outs):
        t.update(zip(small, _unpack(o, [w[n].shape for n in small])))

    return (loss, grad_x[None], *[grads[n] for n in WEIGHTS], *[delta[n] for n in WEIGHTS], *[new_m[n] for n in WEIGHTS],
            *[new_v[n] for n in WEIGHTS])
```

```python
import functools
import math

import numpy as np
import jax
import jax.numpy as jnp
from jax import lax
from jax.experimental import pallas as pl
from jax.experimental.pallas import tpu as pltpu
from jax.experimental.pallas import tpu_sc as plsc

F32 = jnp.float32
BF16 = jnp.bfloat16

D_MODEL = 1024
D_FF = 4096
DEPTH = 4
CHUNK = 64
RMS_EPS = 1e-6
CONV_W = 4
GDN_HEADS = 8
GDN_DK = 128
GDN_IN = 4112
GDN_IN_PAD = 4224
S5_GROUPS = 64
S5_STATE = 64
S5_GROUP = 16
S5_BLOCKS = 8
M2_INNER = 2048
M2_HEADS = 32
M2_GROUPS = 8
M2_STATE = 128
M2_CONV_CH = 4096
M2_IN = 6176
M2_IN_PAD = 6272
ADAM_LR, ADAM_B1, ADAM_B2, ADAM_EPS, ADAM_WD, ADAM_STEP = 0.001, 0.9, 0.999, 1e-08, 0.01, 10

VMEM_LIMIT_BYTES = 56 * 1024 * 1024
SUBLANES = 8
LANES = 128


def _params(*sem):
    return pltpu.CompilerParams(dimension_semantics=tuple(sem) if sem else None, vmem_limit_bytes=VMEM_LIMIT_BYTES)


NN, NT, TN = ((1,), (0,)), ((1,), (1,)), ((0,), (0,))
_DOT_TRANSPOSES = {NN: ((NT, "gb"), (TN, "ag")), NT: ((NN, "gb"), (TN, "ga")), TN: ((NT, "bg"), (NN, "ag"))}


def _dg(a, b, dims):
    if a.ndim == 3:
        dn = (((dims[0][0] + 1,), (dims[1][0] + 1,)), ((0,), (0,)))
    else:
        dn = (dims, ((), ()))
    return lax.dot_general(a, b, dn, preferred_element_type=F32)


def _mxu(a, b, dims):
    return _dg(a.astype(BF16), b.astype(BF16), dims)


@functools.partial(jax.custom_vjp, nondiff_argnums=(2,))
def _dot(a, b, dims=NN):
    return _mxu(a, b, dims)


def _dot_fwd(a, b, dims):
    return _mxu(a, b, dims), (a, b)


def _dot_bwd(dims, res, g):
    ops = dict(a=res[0], b=res[1], g=g)
    (da_dims, da_ops), (db_dims, db_ops) = _DOT_TRANSPOSES[dims]
    return (_mxu(ops[da_ops[0]], ops[da_ops[1]], da_dims).astype(res[0].dtype),
            _mxu(ops[db_ops[0]], ops[db_ops[1]], db_dims).astype(res[1].dtype))


_dot.defvjp(_dot_fwd, _dot_bwd)


def _nt(a, b):
    return _dot(a, b, NT)


def _tn(a, b):
    return _dot(a, b, TN)


def _split3(x):
    x1 = x.astype(BF16)
    r = x - x1.astype(F32)
    x2 = r.astype(BF16)
    return x1, x2, (r - x2.astype(F32)).astype(BF16)


def _sel_mxu(x, sel, dims, x_first):
    f = (lambda p: _dg(p, sel.astype(BF16), dims)) if x_first else (lambda p: _dg(sel.astype(BF16), p, dims))
    x1, x2, x3 = _split3(x)
    return f(x1) + (f(x2) + f(x3))


@jax.custom_vjp
def _pick(x, sel):
    return _sel_mxu(x, sel, NN, True)


def _pick_fwd(x, sel):
    return _sel_mxu(x, sel, NN, True), sel


def _pick_bwd(sel, g):
    return _sel_mxu(g, sel, NT, True), jnp.zeros_like(sel)


_pick.defvjp(_pick_fwd, _pick_bwd)


@jax.custom_vjp
def _accum(sel, x):
    return _sel_mxu(x, sel, NN, False)


def _accum_fwd(sel, x):
    return _sel_mxu(x, sel, NN, False), sel


def _accum_bwd(sel, g):
    return jnp.zeros_like(sel), _sel_mxu(g, sel, TN, False)


_accum.defvjp(_accum_fwd, _accum_bwd)


def _dot3(a, b, dims=NN):
    ah, bh = a.astype(BF16), b.astype(BF16)
    al, bl = (a - ah.astype(F32)).astype(BF16), (b - bh.astype(F32)).astype(BF16)
    return _dg(ah, bh, dims) + (_dg(ah, bl, dims) + _dg(al, bh, dims))


def _neumann(x, r, dims):
    r = r + _dot3(x, r, dims)
    for _ in range(5):
        x = _dot3(x, x)
        r = r + _dot3(x, r, dims)
    return r


@jax.custom_vjp
def _unit_lower_solve(a, rhs):
    return _neumann(-a, rhs, NN)


def _unit_lower_solve_fwd(a, rhs):
    sol = _neumann(-a, rhs, NN)
    return sol, (a, sol)


def _unit_lower_solve_bwd(res, ct):
    a, sol = res
    d_rhs = _neumann(-a, ct, TN)
    return -_dot3(d_rhs, sol, NT), d_rhs


_unit_lower_solve.defvjp(_unit_lower_solve_fwd, _unit_lower_solve_bwd)


@jax.custom_vjp
def _unit_lower_solved(a, rhs, sol):
    return sol


def _unit_lower_solved_fwd(a, rhs, sol):
    return sol, (a, sol)


def _unit_lower_solved_bwd(res, ct):
    da, d_rhs = _unit_lower_solve_bwd(res, ct)
    return da, d_rhs, jnp.zeros_like(ct)


_unit_lower_solved.defvjp(_unit_lower_solved_fwd, _unit_lower_solved_bwd)


def _sigmoid(x):
    return 1.0 / (1.0 + jnp.exp(-x))


def _softplus(x):
    return jnp.maximum(x, 0.0) + jnp.log(1.0 + jnp.exp(-jnp.abs(x)))


def _iota2(shape, axis):
    return lax.broadcasted_iota(jnp.int32, shape, axis)


def _tile(n, cands):
    for c in cands:
        if n % c == 0:
            return c
    return n


MM_TILE_BYTES = 9 * 1024 * 1024


def _mm(name, a, b, mode, out_dtypes, epi=None, extras=(), tn=None):
    if mode == "nn":
        (M, K), N = a.shape, b.shape[1]
    elif mode == "nt":
        (M, K), N = a.shape, b.shape[0]
    else:
        (K, M), N = a.shape, b.shape[1]
    tn = tn or _tile(N, (512, 384, 896, 256, 128))
    out_bytes = tn * (sum(jnp.dtype(d).itemsize for d in out_dtypes) + sum(e.dtype.itemsize for e in extras))
    fits = lambda t: t * K * a.dtype.itemsize <= MM_TILE_BYTES and t * out_bytes <= MM_TILE_BYTES
    tm = next(t for t in (2048, 1024, 512, 256, 128) if M % t == 0 and (fits(t) or t == 128))
    if mode == "nn":
        a_spec, b_spec = pl.BlockSpec((tm, K), lambda i, j: (i, 0)), pl.BlockSpec((K, tn), lambda i, j: (0, j))
        dims = NN
    elif mode == "nt":
        a_spec, b_spec = pl.BlockSpec((tm, K), lambda i, j: (i, 0)), pl.BlockSpec((tn, K), lambda i, j: (j, 0))
        dims = NT
    else:
        a_spec, b_spec = pl.BlockSpec((K, tm), lambda i, j: (0, i)), pl.BlockSpec((K, tn), lambda i, j: (0, j))
        dims = TN
    n_ex = len(extras)

    def body(a_ref, b_ref, *rest):
        acc = _mxu(a_ref[...], b_ref[...], dims)
        res = epi(acc, *[e[...] for e in rest[:n_ex]]) if epi is not None else (acc,)
        for o_ref, r in zip(rest[n_ex:], res):
            o_ref[...] = r.astype(o_ref.dtype)

    tile = pl.BlockSpec((tm, tn), lambda i, j: (i, j))
    out = pl.pallas_call(
        body, name=name, grid=(M // tm, N // tn),
        in_specs=[a_spec, b_spec] + [tile] * n_ex,
        out_specs=[tile] * len(out_dtypes),
        out_shape=[jax.ShapeDtypeStruct((M, N), d) for d in out_dtypes],
        compiler_params=_params("parallel", "parallel"),
    )(a, b, *extras)
    return out if len(out_dtypes) > 1 else out[0]


def _rms_fwd(name, h, g):
    L, D = h.shape
    tr = _tile(L, (256, 128))

    def body(h_ref, g_ref, o_ref):
        x = h_ref[...]
        r = lax.rsqrt(jnp.mean(x * x, axis=-1, keepdims=True) + RMS_EPS)
        o_ref[...] = (x * r * g_ref[...]).astype(o_ref.dtype)

    return pl.pallas_call(
        body, name=name, grid=(L // tr,),
        in_specs=[pl.BlockSpec((tr, D), lambda i: (i, 0)), pl.BlockSpec((1, D), lambda i: (0, 0))],
        out_specs=pl.BlockSpec((tr, D), lambda i: (i, 0)),
        out_shape=jax.ShapeDtypeStruct((L, D), BF16),
        compiler_params=_params("parallel"),
    )(h, g.reshape(1, D))


def _rms_bwd(name, h, g, dhn, dres):
    L, D = h.shape
    tr = _tile(L, (256, 128))

    def body(h_ref, g_ref, dhn_ref, dres_ref, dh_ref, dg_ref):
        x = h_ref[...]
        r = lax.rsqrt(jnp.mean(x * x, axis=-1, keepdims=True) + RMS_EPS)
        xh = x * r
        dy = dhn_ref[...]
        dxh = dy * g_ref[...]
        dh_ref[...] = dres_ref[...] + r * (dxh - xh * jnp.mean(dxh * xh, axis=-1, keepdims=True))

        @pl.when(pl.program_id(0) == 0)
        def _():
            dg_ref[...] = jnp.zeros_like(dg_ref)

        dg_ref[...] += jnp.sum(dy * xh, axis=0, keepdims=True)

    row = pl.BlockSpec((tr, D), lambda i: (i, 0))
    vec = pl.BlockSpec((1, D), lambda i: (0, 0))
    return pl.pallas_call(
        body, name=name, grid=(L // tr,),
        in_specs=[row, vec, row, row], out_specs=[row, vec],
        out_shape=[jax.ShapeDtypeStruct((L, D), F32), jax.ShapeDtypeStruct((1, D), F32)],
        compiler_params=_params("arbitrary"),
    )(h, g.reshape(1, D), dhn, dres)


def _loss_head(h, g, target):
    L, D = h.shape
    tr = _tile(L, (256, 128))

    def body(h_ref, g_ref, t_ref, loss_ref, dh_ref, dg_ref):
        x = h_ref[...]
        r = lax.rsqrt(jnp.mean(x * x, axis=-1, keepdims=True) + RMS_EPS)
        xh = x * r
        err = xh * g_ref[...] - t_ref[...]
        dy = err * (1.0 / D)
        dxh = dy * g_ref[...]
        dh_ref[...] = r * (dxh - xh * jnp.mean(dxh * xh, axis=-1, keepdims=True))

        @pl.when(pl.program_id(0) == 0)
        def _():
            dg_ref[...] = jnp.zeros_like(dg_ref)
            loss_ref[...] = jnp.zeros_like(loss_ref)

        dg_ref[...] += jnp.sum(dy * xh, axis=0, keepdims=True)
        loss_ref[...] += (0.5 / D) * jnp.sum(jnp.sum(err * err, axis=-1, keepdims=True), axis=0, keepdims=True)

    row = pl.BlockSpec((tr, D), lambda i: (i, 0))
    vec = pl.BlockSpec((1, D), lambda i: (0, 0))
    return pl.pallas_call(
        body, name="loss_head", grid=(L // tr,),
        in_specs=[row, vec, row], out_specs=[pl.BlockSpec((1, 1), lambda i: (0, 0)), row, vec],
        out_shape=[jax.ShapeDtypeStruct((1, 1), F32), jax.ShapeDtypeStruct((L, D), F32), jax.ShapeDtypeStruct((1, D), F32)],
        compiler_params=_params("arbitrary"),
    )(h, g.reshape(1, D), target)


def _glu_fwd(h, ag):
    L, D = h.shape
    tr = _tile(L, (256, 128))

    def body(h_ref, v_ref, g_ref, o_ref):
        o_ref[...] = h_ref[...] + v_ref[...] * _sigmoid(g_ref[...])

    return pl.pallas_call(
        body, name="s5_glu_fwd", grid=(L // tr,),
        in_specs=[pl.BlockSpec((tr, D), lambda i: (i, 0)), pl.BlockSpec((tr, D), lambda i: (i, 0)),
                  pl.BlockSpec((tr, D), lambda i: (i, 1))],
        out_specs=pl.BlockSpec((tr, D), lambda i: (i, 0)),
        out_shape=jax.ShapeDtypeStruct((L, D), F32),
        compiler_params=_params("parallel"),
    )(h, ag, ag)


def _glu_bwd(dh, ag):
    L, D = dh.shape
    tr = _tile(L, (256, 128))

    def body(dh_ref, v_ref, g_ref, dv_ref, dg_ref):
        s = _sigmoid(g_ref[...])
        d = dh_ref[...]
        dv_ref[...] = d * s
        dg_ref[...] = d * v_ref[...] * s * (1.0 - s)

    dv, dg = pl.pallas_call(
        body, name="s5_glu_bwd", grid=(L // tr,),
        in_specs=[pl.BlockSpec((tr, D), lambda i: (i, 0)), pl.BlockSpec((tr, D), lambda i: (i, 0)),
                  pl.BlockSpec((tr, D), lambda i: (i, 1))],
        out_specs=[pl.BlockSpec((tr, D), lambda i: (i, 0))] * 2,
        out_shape=[jax.ShapeDtypeStruct((L, D), F32)] * 2,
        compiler_params=_params("parallel"),
    )(dh, ag, ag)
    return jnp.concatenate([dv, dg], axis=1).astype(BF16)


CONV_ROWS = 128
CONV_COLS = 512


def _shift_rows(cat, s):
    if s == 0:
        return cat[SUBLANES:, :]
    return pltpu.roll(cat, s, axis=0)[SUBLANES:, :]


def _conv_fwd(name, p, col0, w, b):
    L = p.shape[0]
    C = w.shape[1]
    tc = _tile(C, (CONV_COLS, 256))
    cb0 = col0 // tc
    nr = L // CONV_ROWS

    def body(x_ref, w_ref, b_ref, o_ref):
        def step(r, carry):
            r0 = pl.multiple_of(r * CONV_ROWS, CONV_ROWS)
            cur = x_ref[pl.ds(r0, CONV_ROWS), :]
            p0 = pl.multiple_of(jnp.maximum(r0 - SUBLANES, 0), SUBLANES)
            prev = jnp.where(r > 0, x_ref[pl.ds(p0, SUBLANES), :], 0.0)
            cat = jnp.concatenate([prev, cur], axis=0)
            acc = b_ref[...] + w_ref[3:4, :] * cur
            for k in range(CONV_W - 1):
                acc = acc + w_ref[k:k + 1, :] * _shift_rows(cat, CONV_W - 1 - k)
            o_ref[pl.ds(r0, CONV_ROWS), :] = acc * _sigmoid(acc)
            return carry

        lax.fori_loop(0, nr, step, 0)

    return pl.pallas_call(
        body, name=name, grid=(C // tc,),
        in_specs=[pl.BlockSpec((L, tc), lambda j: (0, cb0 + j)), pl.BlockSpec((CONV_W, tc), lambda j: (0, j)),
                  pl.BlockSpec((1, tc), lambda j: (0, j))],
        out_specs=pl.BlockSpec((L, tc), lambda j: (0, j)),
        out_shape=jax.ShapeDtypeStruct((L, C), F32),
        compiler_params=_params("parallel"),
    )(p, w, b)


def _conv_bwd(name, p, col0, w, b, dout):
    L = p.shape[0]
    C = w.shape[1]
    tc = _tile(C, (CONV_COLS, 256))
    cb0 = col0 // tc
    nr = L // CONV_ROWS

    def body(x_ref, w_ref, b_ref, do_ref, dx_ref, dw_ref, db_ref, dpre_ref):
        def step1(r, carry):
            dw0, dw1, dw2, dw3, dbb = carry
            r0 = pl.multiple_of(r * CONV_ROWS, CONV_ROWS)
            cur = x_ref[pl.ds(r0, CONV_ROWS), :]
            p0 = pl.multiple_of(jnp.maximum(r0 - SUBLANES, 0), SUBLANES)
            prev = jnp.where(r > 0, x_ref[pl.ds(p0, SUBLANES), :], 0.0)
            cat = jnp.concatenate([prev, cur], axis=0)
            sh = [_shift_rows(cat, CONV_W - 1 - k) for k in range(CONV_W - 1)] + [cur]
            acc = b_ref[...] + w_ref[3:4, :] * cur
            for k in range(CONV_W - 1):
                acc = acc + w_ref[k:k + 1, :] * sh[k]
            sg = _sigmoid(acc)
            dpre = do_ref[pl.ds(r0, CONV_ROWS), :] * (sg + acc * sg * (1.0 - sg))
            dpre_ref[pl.ds(r0, CONV_ROWS), :] = dpre
            dws = [d + jnp.sum(dpre * s, axis=0, keepdims=True) for d, s in zip((dw0, dw1, dw2, dw3), sh)]
            return (*dws, dbb + jnp.sum(dpre, axis=0, keepdims=True))

        z = jnp.zeros((1, tc), F32)
        dw0, dw1, dw2, dw3, dbb = lax.fori_loop(0, nr, step1, (z, z, z, z, z))
        dw_ref[...] = jnp.concatenate([dw0, dw1, dw2, dw3, z, z, z, z], axis=0)
        db_ref[...] = dbb

        def step2(r, carry):
            r0 = pl.multiple_of(r * CONV_ROWS, CONV_ROWS)
            cur = dpre_ref[pl.ds(r0, CONV_ROWS), :]
            n0 = pl.multiple_of(jnp.minimum(r0 + CONV_ROWS, L - SUBLANES), SUBLANES)
            nxt = jnp.where(r < nr - 1, dpre_ref[pl.ds(n0, SUBLANES), :], 0.0)
            cat = jnp.concatenate([cur, nxt], axis=0)
            acc = w_ref[3:4, :] * cur
            for k in range(CONV_W - 1):
                s = CONV_W - 1 - k
                acc = acc + w_ref[k:k + 1, :] * pltpu.roll(cat, CONV_ROWS + SUBLANES - s, axis=0)[:CONV_ROWS, :]
            dx_ref[pl.ds(r0, CONV_ROWS), :] = acc
            return carry

        lax.fori_loop(0, nr, step2, 0)

    dx, dw, db = pl.pallas_call(
        body, name=name, grid=(C // tc,),
        in_specs=[pl.BlockSpec((L, tc), lambda j: (0, cb0 + j)), pl.BlockSpec((CONV_W, tc), lambda j: (0, j)),
                  pl.BlockSpec((1, tc), lambda j: (0, j)), pl.BlockSpec((L, tc), lambda j: (0, j))],
        out_specs=[pl.BlockSpec((L, tc), lambda j: (0, j)), pl.BlockSpec((SUBLANES, tc), lambda j: (0, j)),
                   pl.BlockSpec((1, tc), lambda j: (0, j))],
        out_shape=[jax.ShapeDtypeStruct((L, C), F32), jax.ShapeDtypeStruct((SUBLANES, C), F32),
                   jax.ShapeDtypeStruct((1, C), F32)],
        scratch_shapes=[pltpu.VMEM((L, tc), F32)],
        compiler_params=_params("parallel"),
    )(p, w, b, dout)
    return dx, dw[:CONV_W], db


def _chunk_consts():
    r, c = _iota2((CHUNK, CHUNK), 0), _iota2((CHUNK, CHUNK), 1)
    causal = r >= c
    return causal, r > c, (r == c).astype(F32), causal.astype(F32), jnp.ones((CHUNK, CHUNK), F32)


def _by_lanes(t):
    return jnp.concatenate([t[i] for i in range(t.shape[0])], axis=1)


def _by_batch(t, w):
    return jnp.concatenate([t[None, :, i * w:(i + 1) * w] for i in range(t.shape[1] // w)], axis=0)


def _diag_lanes():
    return (_iota2((CHUNK, LANES), 0) == _iota2((CHUNK, LANES), 1)).astype(F32)


def _gdn_chunk(q, k, v, ab, gate, S, alog, dtb, og, ea, eb, sol=None):
    causal, strict, _, tril, ones = _chunk_consts()
    logits = _by_batch(_pick(ab, jnp.concatenate([_by_lanes(ea), _by_lanes(eb)], axis=1)), LANES)
    H = q.shape[0]
    g = -jnp.exp(alog) * _softplus(logits[:H] + dtb)
    beta = _sigmoid(logits[H:])
    qn = q * lax.rsqrt(jnp.sum(q * q, axis=-1, keepdims=True) + 1e-6) * (GDN_DK ** -0.5)
    kn = k * lax.rsqrt(jnp.sum(k * k, axis=-1, keepdims=True) + 1e-6)
    g_l = _by_lanes(g)
    gc = _by_batch(_accum(tril, g_l), LANES)
    glast = _by_batch(_accum(ones, g_l), LANES)
    gcol = gc[:, :, :CHUNK]
    grow = _by_batch(_accum(ones, _by_lanes(gc * _diag_lanes())), LANES)[:, :, :CHUNK]
    decay = jnp.exp(jnp.where(causal, gcol - grow, -jnp.inf))
    a = jnp.where(strict, beta[:, :, :CHUNK] * _nt(kn, kn) * decay, 0.0)
    eg = jnp.exp(gc)
    rhs = jnp.concatenate([v * beta, kn * (beta * eg)], axis=2)
    sol = _unit_lower_solve(a, rhs) if sol is None else _unit_lower_solved(a, rhs, sol)
    u, w = sol[:, :, :GDN_DK], sol[:, :, GDN_DK:]
    qk = _nt(qn, kn) * decay
    v_new = u - _dot(w, S)
    o = _dot(qn * eg, S) + _dot(qk, v_new)
    cd = jnp.exp(glast)
    s_new = jnp.concatenate([cd, cd], axis=1) * S + _tn(kn * jnp.exp(glast - gc), v_new)
    on = o * lax.rsqrt(jnp.mean(o * o, axis=-1, keepdims=True) + RMS_EPS) * og
    return on * (gate * _sigmoid(gate)), s_new, sol


GDN_HB = 8


def _gdn_specs(nc, rev):
    cm = (lambda c: nc - 1 - c) if rev else (lambda c: c)
    blk = lambda off: pl.BlockSpec((CHUNK, GDN_HB * GDN_DK), lambda c, h: (cm(c), off // GDN_HB + h))
    ab = pl.BlockSpec((CHUNK, LANES), lambda c, h: (cm(c), (GDN_IN_PAD - LANES) // LANES))
    hv = pl.BlockSpec((GDN_HB, 1, LANES), lambda c, h: (h, 0, 0))
    og = pl.BlockSpec((1, LANES), lambda c, h: (0, 0))
    em = pl.BlockSpec((GDN_HB, LANES, LANES), lambda c, h: (h, 0, 0))
    st = pl.BlockSpec((None, GDN_HB, GDN_DK, GDN_DK), lambda c, h: (cm(c), h, 0, 0))
    sl = pl.BlockSpec((None, GDN_HB, CHUNK, 2 * GDN_DK), lambda c, h: (cm(c), h, 0, 0))
    return blk, ab, hv, og, em, st, sl


def _gdn_fwd(qc, kc, vc, p, alog_e, dtb_e, og, ea, eb):
    L = qc.shape[0]
    nc = L // CHUNK
    blk, ab, hv, ogs, em, st, sl = _gdn_specs(nc, False)

    def body(q_ref, k_ref, v_ref, gate_ref, ab_ref, al_ref, dt_ref, og_ref, ea_ref, eb_ref, y_ref, sp_ref, sol_ref, s_scr):
        c, h = pl.program_id(0), pl.program_id(1)
        lanes = [slice(i * GDN_DK, (i + 1) * GDN_DK) for i in range(GDN_HB)]
        heads = pl.ds(h * GDN_HB, GDN_HB)
        stack = lambda ref: jnp.concatenate([ref[:, ls][None] for ls in lanes], axis=0)

        @pl.when(c == 0)
        def _():
            s_scr[heads] = jnp.zeros((GDN_HB, GDN_DK, GDN_DK), F32)

        S = s_scr[heads]
        sp_ref[...] = S
        y, s_new, sol = _gdn_chunk(stack(q_ref), stack(k_ref), stack(v_ref), ab_ref[...], stack(gate_ref), S,
                                   al_ref[...], dt_ref[...], og_ref[...], ea_ref[...], eb_ref[...])
        for i, ls in enumerate(lanes):
            y_ref[:, ls] = y[i]
        s_scr[heads] = s_new
        sol_ref[...] = sol

    return pl.pallas_call(
        body, name="gdn_fwd", grid=(nc, GDN_HEADS // GDN_HB),
        in_specs=[blk(0), blk(0), blk(0), blk(3 * GDN_HEADS), ab, hv, hv, ogs, em, em],
        out_specs=[blk(0), st, sl],
        out_shape=[jax.ShapeDtypeStruct((L, D_MODEL), F32), jax.ShapeDtypeStruct((nc, GDN_HEADS, GDN_DK, GDN_DK), F32),
                   jax.ShapeDtypeStruct((nc, GDN_HEADS, CHUNK, 2 * GDN_DK), F32)],
        scratch_shapes=[pltpu.VMEM((GDN_HEADS, GDN_DK, GDN_DK), F32)],
        compiler_params=_params("arbitrary", "arbitrary"),
    )(qc, kc, vc, p, p, alog_e, dtb_e, og, ea, eb)


def _gdn_bwd(qc, kc, vc, p, alog_e, dtb_e, og, ea, eb, sprev, sol, dy):
    L = qc.shape[0]
    nc = L // CHUNK
    blk, ab, hv, ogs, em, st, sl = _gdn_specs(nc, True)

    def body(q_ref, k_ref, v_ref, gate_ref, ab_ref, al_ref, dt_ref, og_ref, ea_ref, eb_ref, sp_ref, sol_ref, dy_ref,
             dq_ref, dk_ref, dv_ref, dgate_ref, dab_ref, dpar_ref, ds_scr):
        c, h = pl.program_id(0), pl.program_id(1)
        lanes = [slice(i * GDN_DK, (i + 1) * GDN_DK) for i in range(GDN_HB)]
        heads = pl.ds(h * GDN_HB, GDN_HB)
        stack = lambda ref: jnp.concatenate([ref[:, ls][None] for ls in lanes], axis=0)

        @pl.when(c == 0)
        def _():
            ds_scr[heads] = jnp.zeros((GDN_HB, GDN_DK, GDN_DK), F32)
            dpar_ref[heads] = jnp.zeros((GDN_HB, SUBLANES, LANES), F32)

        @pl.when(h == 0)
        def _():
            dab_ref[...] = jnp.zeros_like(dab_ref)

        ea_m, eb_m, sol_m = ea_ref[...], eb_ref[...], sol_ref[...]
        f = lambda q, k, v, a_b, gate, S, al, dt, o_g: _gdn_chunk(q, k, v, a_b, gate, S, al, dt, o_g, ea_m, eb_m, sol_m)[:2]
        _, vjp = jax.vjp(f, stack(q_ref), stack(k_ref), stack(v_ref), ab_ref[...], stack(gate_ref), sp_ref[...],
                         al_ref[...], dt_ref[...], og_ref[...])
        dq, dk, dv, dab, dgate, ds, dal, ddt, dog = vjp((stack(dy_ref), ds_scr[heads]))
        for i, ls in enumerate(lanes):
            dq_ref[:, ls] = dq[i]
            dk_ref[:, ls] = dk[i]
            dv_ref[:, ls] = dv[i]
            dgate_ref[:, ls] = dgate[i]
        ds_scr[heads] = ds
        dab_ref[...] += dab
        first = _iota2((GDN_HB, 1, LANES), 0) == 0
        dpar_ref[heads] += jnp.concatenate([dal, ddt, jnp.where(first, dog[None], 0.0),
                                            jnp.zeros((GDN_HB, SUBLANES - 3, LANES), F32)], axis=1)

    return pl.pallas_call(
        body, name="gdn_bwd", grid=(nc, GDN_HEADS // GDN_HB),
        in_specs=[blk(0), blk(0), blk(0), blk(3 * GDN_HEADS), ab, hv, hv, ogs, em, em, st, sl, blk(0)],
        out_specs=[blk(0), blk(0), blk(0), blk(0), pl.BlockSpec((CHUNK, LANES), lambda c, h: (nc - 1 - c, 0)),
                   pl.BlockSpec((GDN_HEADS, SUBLANES, LANES), lambda c, h: (0, 0, 0))],
        out_shape=[jax.ShapeDtypeStruct((L, D_MODEL), F32)] * 4
        + [jax.ShapeDtypeStruct((L, LANES), F32), jax.ShapeDtypeStruct((GDN_HEADS, SUBLANES, LANES), F32)],
        scratch_shapes=[pltpu.VMEM((GDN_HEADS, GDN_DK, GDN_DK), F32)],
        compiler_params=_params("arbitrary", "arbitrary"),
    )(qc, kc, vc, p, p, alog_e, dtb_e, og, ea, eb, sprev, sol, dy)


def _gdn_selectors():
    rows = np.arange(LANES)[None, :, None]
    heads = np.arange(GDN_HEADS)[:, None, None]
    ea = np.broadcast_to(rows == heads, (GDN_HEADS, LANES, LANES)).astype(np.float32)
    eb = np.broadcast_to(rows == heads + GDN_HEADS, (GDN_HEADS, LANES, LANES)).astype(np.float32)
    return jnp.asarray(ea), jnp.asarray(eb)


M2_GW = M2_INNER // M2_GROUPS
M2_HPG = M2_HEADS // M2_GROUPS
M2_HD = M2_INNER // M2_HEADS


def _m2_chunk(x, bm, cm, z, dtr, st, dtb, alog, dsk, ng, e, ecol):
    G = x.shape[0]
    causal, _, _, tril, ones = _chunk_consts()
    dt_n = _softplus(dtr + dtb)
    da_n = dt_n * (-jnp.exp(alog))
    cum_n = _accum(tril, da_n)
    tot_n = _accum(ones, da_n)
    wide = _pick(jnp.concatenate([dt_n, cum_n, tot_n], axis=0), e)
    dt_w, cum_w, tot_w = (_by_batch(wide[i * CHUNK:(i + 1) * CHUNK], M2_GW) for i in range(3))
    xdt = x * dt_w
    cb = _nt(cm, bm)
    heads = lambda t: jnp.concatenate([t[i:i + 1] for i in range(G) for _ in range(M2_HPG)], axis=0)
    colb = _by_batch(_pick(cum_n, ecol), LANES)
    rowb = _by_batch(_accum(ones, _by_lanes(colb * _diag_lanes())), LANES)
    lmat = jnp.exp(jnp.where(causal, colb[:, :, :CHUNK] - rowb[:, :, :CHUNK], -jnp.inf))
    yr = _dot(heads(cb) * lmat, heads(xdt))
    head = _iota2((CHUNK, M2_GW), 1) // M2_HD
    ydiag = jnp.concatenate([sum(jnp.where(head == r, yr[i * M2_HPG + r], 0.0) for r in range(M2_HPG))[None] for i in range(G)], axis=0)
    st_new = _tn(bm, xdt * jnp.exp(tot_w - cum_w))
    cd = jnp.exp(tot_w)
    s_new = jnp.concatenate([cd, cd], axis=1) * st + st_new
    y = ydiag + _dot(cm, st) * jnp.exp(cum_w) + dsk * x
    y = y * (z * _sigmoid(z))
    yn = y * lax.rsqrt(jnp.mean(y * y, axis=-1, keepdims=True) + RMS_EPS) * ng
    return yn, s_new


M2_GB = 4


def _m2_specs(nc, rev):
    cm = (lambda c: nc - 1 - c) if rev else (lambda c: c)
    wide = lambda off: pl.BlockSpec((CHUNK, M2_GB * M2_GW), lambda c, g: (cm(c), off // M2_GB + g))
    nar = lambda off: pl.BlockSpec((CHUNK, M2_GB * LANES), lambda c, g: (cm(c), off // M2_GB + g))
    dts = pl.BlockSpec((CHUNK, LANES), lambda c, g: (cm(c), (M2_IN_PAD - LANES) // LANES))
    v128 = pl.BlockSpec((1, LANES), lambda c, g: (0, 0))
    v256 = pl.BlockSpec((1, M2_GB * M2_GW), lambda c, g: (0, g))
    es = pl.BlockSpec((LANES, M2_GB * M2_GW), lambda c, g: (0, g))
    ecs = pl.BlockSpec((LANES, M2_GB * M2_HPG * LANES), lambda c, g: (0, g))
    st = pl.BlockSpec((None, M2_GB, M2_STATE, M2_GW), lambda c, g: (cm(c), g, 0, 0))
    return wide, nar, dts, v128, v256, es, ecs, st


def _m2_fwd(xbc, p, dtb, alog, dsk, ng, e, ecol):
    L = xbc.shape[0]
    nc = L // CHUNK
    wide, nar, dts, v128, v256, es, ecs, st = _m2_specs(nc, False)

    def body(x_ref, b_ref, c_ref, z_ref, dt_ref, dtb_ref, al_ref, dsk_ref, ng_ref, e_ref, ec_ref, y_ref, sp_ref, s_scr):
        c, g = pl.program_id(0), pl.program_id(1)
        wide_l = [slice(i * M2_GW, (i + 1) * M2_GW) for i in range(M2_GB)]
        nar_l = [slice(i * LANES, (i + 1) * LANES) for i in range(M2_GB)]
        groups = pl.ds(g * M2_GB, M2_GB)
        wide_s = lambda ref: jnp.concatenate([ref[:, ls][None] for ls in wide_l], axis=0)
        nar_s = lambda ref: jnp.concatenate([ref[:, ls][None] for ls in nar_l], axis=0)

        @pl.when(c == 0)
        def _():
            s_scr[groups] = jnp.zeros((M2_GB, M2_STATE, M2_GW), F32)

        S = s_scr[groups]
        sp_ref[...] = S
        y, s_new = _m2_chunk(wide_s(x_ref), nar_s(b_ref), nar_s(c_ref), wide_s(z_ref), dt_ref[...], S, dtb_ref[...], al_ref[...],
                             wide_s(dsk_ref), wide_s(ng_ref), e_ref[...], ec_ref[...])
        for i, ls in enumerate(wide_l):
            y_ref[:, ls] = y[i]
        s_scr[groups] = s_new

    return pl.pallas_call(
        body, name="m2_fwd", grid=(nc, M2_GROUPS // M2_GB),
        in_specs=[wide(0), nar(2 * M2_GROUPS), nar(3 * M2_GROUPS), wide(0), dts, v128, v128, v256, v256, es, ecs],
        out_specs=[wide(0), st],
        out_shape=[jax.ShapeDtypeStruct((L, M2_INNER), F32), jax.ShapeDtypeStruct((nc, M2_GROUPS, M2_STATE, M2_GW), F32)],
        scratch_shapes=[pltpu.VMEM((M2_GROUPS, M2_STATE, M2_GW), F32)],
        compiler_params=_params("arbitrary", "arbitrary"),
    )(xbc, xbc, xbc, p, p, dtb, alog, dsk, ng, e, ecol)


def _m2_bwd(xbc, p, dtb, alog, dsk, ng, e, ecol, sprev, dy):
    L = xbc.shape[0]
    nc = L // CHUNK
    wide, nar, dts, v128, v256, es, ecs, st = _m2_specs(nc, True)

    def body(x_ref, b_ref, c_ref, z_ref, dt_ref, dtb_ref, al_ref, dsk_ref, ng_ref, e_ref, ec_ref, sp_ref, dy_ref,
             dx_ref, db_ref, dc_ref, dz_ref, ddt_ref, dnar_ref, dwide_ref, ds_scr):
        c, g = pl.program_id(0), pl.program_id(1)
        wide_l = [slice(i * M2_GW, (i + 1) * M2_GW) for i in range(M2_GB)]
        nar_l = [slice(i * LANES, (i + 1) * LANES) for i in range(M2_GB)]
        groups = pl.ds(g * M2_GB, M2_GB)
        wide_s = lambda ref: jnp.concatenate([ref[:, ls][None] for ls in wide_l], axis=0)
        nar_s = lambda ref: jnp.concatenate([ref[:, ls][None] for ls in nar_l], axis=0)

        @pl.when(jnp.logical_and(c == 0, g == 0))
        def _():
            dnar_ref[...] = jnp.zeros_like(dnar_ref)

        @pl.when(c == 0)
        def _():
            ds_scr[groups] = jnp.zeros((M2_GB, M2_STATE, M2_GW), F32)
            dwide_ref[groups] = jnp.zeros((M2_GB, SUBLANES, M2_GW), F32)

        @pl.when(g == 0)
        def _():
            ddt_ref[...] = jnp.zeros_like(ddt_ref)

        e_m, ec_m = e_ref[...], ec_ref[...]
        f = lambda x, bm, cm, z, dtr, S, dtb, al, dsk, ng: _m2_chunk(x, bm, cm, z, dtr, S, dtb, al, dsk, ng, e_m, ec_m)
        _, vjp = jax.vjp(f, wide_s(x_ref), nar_s(b_ref), nar_s(c_ref), wide_s(z_ref), dt_ref[...], sp_ref[...], dtb_ref[...],
                         al_ref[...], wide_s(dsk_ref), wide_s(ng_ref))
        dx, db, dc, dz, ddt, ds, ddtb, dal, ddsk, dng = vjp((wide_s(dy_ref), ds_scr[groups]))
        for i in range(M2_GB):
            dx_ref[:, wide_l[i]] = dx[i]
            db_ref[:, nar_l[i]] = db[i]
            dc_ref[:, nar_l[i]] = dc[i]
            dz_ref[:, wide_l[i]] = dz[i]
        ds_scr[groups] = ds
        ddt_ref[...] += ddt
        dnar_ref[...] += jnp.concatenate([ddtb, dal, jnp.zeros((SUBLANES - 2, LANES), F32)], axis=0)
        dwide_ref[groups] += jnp.concatenate([ddsk, dng, jnp.zeros((M2_GB, SUBLANES - 2, M2_GW), F32)], axis=1)

    return pl.pallas_call(
        body, name="m2_bwd", grid=(nc, M2_GROUPS // M2_GB),
        in_specs=[wide(0), nar(2 * M2_GROUPS), nar(3 * M2_GROUPS), wide(0), dts, v128, v128, v256, v256, es, ecs, st, wide(0)],
        out_specs=[wide(0), nar(0), nar(0), wide(0), pl.BlockSpec((CHUNK, LANES), lambda c, g: (nc - 1 - c, 0)),
                   pl.BlockSpec((SUBLANES, LANES), lambda c, g: (0, 0)),
                   pl.BlockSpec((M2_GROUPS, SUBLANES, M2_GW), lambda c, g: (0, 0, 0))],
        out_shape=[jax.ShapeDtypeStruct((L, M2_INNER), F32), jax.ShapeDtypeStruct((L, M2_GROUPS * M2_STATE), F32),
                   jax.ShapeDtypeStruct((L, M2_GROUPS * M2_STATE), F32), jax.ShapeDtypeStruct((L, M2_INNER), F32),
                   jax.ShapeDtypeStruct((L, LANES), F32), jax.ShapeDtypeStruct((SUBLANES, LANES), F32),
                   jax.ShapeDtypeStruct((M2_GROUPS, SUBLANES, M2_GW), F32)],
        scratch_shapes=[pltpu.VMEM((M2_GROUPS, M2_STATE, M2_GW), F32)],
        compiler_params=_params("arbitrary", "arbitrary"),
    )(xbc, xbc, xbc, p, p, dtb, alog, dsk, ng, e, ecol, sprev, dy)


def _m2_selectors():
    e = np.zeros((LANES, M2_INNER), np.float32)
    ecol = np.zeros((LANES, M2_HEADS * LANES), np.float32)
    for h in range(M2_HEADS):
        e[h, M2_HD * h:M2_HD * (h + 1)] = 1.0
        ecol[h, LANES * h:LANES * (h + 1)] = 1.0
    return jnp.asarray(e), jnp.asarray(ecol)


S5_NS = S5_GROUPS * S5_STATE // S5_BLOCKS
S5_ROWS = 256
GELU_C = math.sqrt(2.0 / math.pi)


def _gelu(x):
    return 0.5 * x * (1.0 + jnp.tanh(GELU_C * (x + 0.044715 * x * x * x)))


def _gelu_grad(x):
    t = jnp.tanh(GELU_C * (x + 0.044715 * x * x * x))
    return 0.5 * (1.0 + t) + 0.5 * x * (1.0 - t * t) * GELU_C * (1.0 + 3.0 * 0.044715 * x * x)


def _s5_scan(re_ref, im_ref, pw_re, pw_im, nrows, reverse, states=None):
    n = re_ref.shape[1]
    row = _iota2((SUBLANES, n), 0)
    steps = []
    for d in (1, 2, 4):
        keep = (row < SUBLANES - d) if reverse else (row >= d)
        steps.append(((SUBLANES - d) if reverse else d, jnp.where(keep, pw_re[d - 1:d, :], 0.0), jnp.where(keep, pw_im[d - 1:d, :], 0.0)))
    if reverse:
        cw_re = jnp.concatenate([pw_re[SUBLANES - 1 - k:SUBLANES - k, :] for k in range(SUBLANES)], axis=0)
        cw_im = jnp.concatenate([pw_im[SUBLANES - 1 - k:SUBLANES - k, :] for k in range(SUBLANES)], axis=0)
    else:
        cw_re, cw_im = pw_re, pw_im
    edge = 0 if reverse else SUBLANES - 1
    ngroups = nrows // SUBLANES

    def step(i, carry):
        cr, ci, ar, ai = carry
        gi = (ngroups - 1 - i) if reverse else i
        r0 = pl.multiple_of(gi * SUBLANES, SUBLANES)
        xr, xi = re_ref[pl.ds(r0, SUBLANES), :], im_ref[pl.ds(r0, SUBLANES), :]
        for shift, pr, pi in steps:
            sr, si = pltpu.roll(xr, shift, axis=0), pltpu.roll(xi, shift, axis=0)
            xr, xi = xr + (pr * sr - pi * si), xi + (pr * si + pi * sr)
        xr, xi = xr + (cw_re * cr - cw_im * ci), xi + (cw_re * ci + cw_im * cr)
        re_ref[pl.ds(r0, SUBLANES), :] = xr
        im_ref[pl.ds(r0, SUBLANES), :] = xi
        if states is not None:
            p0 = pl.multiple_of(jnp.maximum(r0 - SUBLANES, 0), SUBLANES)
            live = jnp.where(gi > 0, 1.0, 0.0)
            prev = [jnp.where(row >= 1, pltpu.roll(ref[pl.ds(r0, SUBLANES), :], 1, axis=0),
                              live * pltpu.roll(ref[pl.ds(p0, SUBLANES), :], 1, axis=0)) for ref in states]
            ar, ai = ar + (prev[0] * xr + prev[1] * xi), ai + (prev[0] * xi - prev[1] * xr)
        return (jnp.sum(jnp.where(row == edge, xr, 0.0), axis=0, keepdims=True),
                jnp.sum(jnp.where(row == edge, xi, 0.0), axis=0, keepdims=True), ar, ai)

    z = jnp.zeros((1, n), F32)
    za = jnp.zeros((SUBLANES, n) if states is not None else (1, n), F32)
    _, _, ar, ai = lax.fori_loop(0, ngroups, step, (z, z, za, za))
    return jnp.sum(ar, axis=0, keepdims=True), jnp.sum(ai, axis=0, keepdims=True)


def _s5_project_in(u_ref, bm_ref, re_ref, im_ref, L):
    def step(i, carry):
        r0 = pl.multiple_of(i * S5_ROWS, S5_ROWS)
        bu = _dot(u_ref[pl.ds(r0, S5_ROWS), :], bm_ref[...])
        re_ref[pl.ds(r0, S5_ROWS), :] = bu[:, :S5_NS]
        im_ref[pl.ds(r0, S5_ROWS), :] = bu[:, S5_NS:]
        return carry

    lax.fori_loop(0, L // S5_ROWS, step, 0)


def _s5_specs(L):
    col = pl.BlockSpec((L, LANES), lambda j: (0, j))
    bm = pl.BlockSpec((None, LANES, 2 * S5_NS), lambda j: (j, 0, 0))
    cm = pl.BlockSpec((None, 2 * S5_NS, LANES), lambda j: (j, 0, 0))
    pw = pl.BlockSpec((None, SUBLANES, S5_NS), lambda j: (j, 0, 0))
    vec = pl.BlockSpec((1, LANES), lambda j: (0, j))
    return col, bm, cm, pw, vec


def _s5_fwd(u, bmat, cmat, pw_re, pw_im, dsk):
    L = u.shape[0]
    col, bm, cm, pw, vec = _s5_specs(L)

    def body(u_ref, bm_ref, cm_ref, pr_ref, pi_ref, d_ref, y_ref, re_scr, im_scr):
        _s5_project_in(u_ref, bm_ref, re_scr, im_scr, L)
        _s5_scan(re_scr, im_scr, pr_ref[...], pi_ref[...], L, False)

        def step(i, carry):
            r0 = pl.multiple_of(i * S5_ROWS, S5_ROWS)
            rows = pl.ds(r0, S5_ROWS)
            y = _dot(re_scr[rows, :], cm_ref[:S5_NS, :]) + _dot(im_scr[rows, :], cm_ref[S5_NS:, :]) + d_ref[...] * u_ref[rows, :]
            y_ref[rows, :] = _gelu(y)
            return carry

        lax.fori_loop(0, L // S5_ROWS, step, 0)

    return pl.pallas_call(
        body, name="s5_fwd", grid=(S5_BLOCKS,),
        in_specs=[col, bm, cm, pw, pw, vec], out_specs=col,
        out_shape=jax.ShapeDtypeStruct((L, D_MODEL), F32),
        scratch_shapes=[pltpu.VMEM((L, S5_NS), F32)] * 2,
        compiler_params=_params("parallel"),
    )(u, bmat, cmat, pw_re, pw_im, dsk)


def _s5_bwd(u, bmat, cmat, pw_re, pw_im, dsk, dyg):
    L = u.shape[0]
    col, bm, cm, pw, vec = _s5_specs(L)

    def body(u_ref, bm_ref, cm_ref, pr_ref, pi_ref, d_ref, dy_ref, du_ref, dbm_ref, dcm_ref, dlam_ref, dd_ref,
             re_scr, im_scr, gr_scr, gi_scr, dyp_scr):
        _s5_project_in(u_ref, bm_ref, re_scr, im_scr, L)
        _s5_scan(re_scr, im_scr, pr_ref[...], pi_ref[...], L, False)

        def step(i, carry):
            dcr, dci, dd = carry
            r0 = pl.multiple_of(i * S5_ROWS, S5_ROWS)
            rows = pl.ds(r0, S5_ROWS)
            sr, si, uu = re_scr[rows, :], im_scr[rows, :], u_ref[rows, :]
            y = _dot(sr, cm_ref[:S5_NS, :]) + _dot(si, cm_ref[S5_NS:, :]) + d_ref[...] * uu
            dyp = dy_ref[rows, :] * _gelu_grad(y)
            dyp_scr[rows, :] = dyp
            gr_scr[rows, :] = _nt(dyp, cm_ref[:S5_NS, :])
            gi_scr[rows, :] = _nt(dyp, cm_ref[S5_NS:, :])
            return dcr + _tn(sr, dyp), dci + _tn(si, dyp), dd + jnp.sum(dyp * uu, axis=0, keepdims=True)

        zc = jnp.zeros((S5_NS, LANES), F32)
        dcr, dci, dd = lax.fori_loop(0, L // S5_ROWS, step, (zc, zc, jnp.zeros((1, LANES), F32)))
        dcm_ref[:S5_NS, :] = dcr
        dcm_ref[S5_NS:, :] = dci
        dd_ref[...] = dd

        ar, ai = _s5_scan(gr_scr, gi_scr, pr_ref[...], -pi_ref[...], L, True, states=(re_scr, im_scr))
        dlam_ref[...] = jnp.concatenate([ar, ai, jnp.zeros((SUBLANES - 2, S5_NS), F32)], axis=0)

        def in_step(i, carry):
            dbr, dbi = carry
            r0 = pl.multiple_of(i * S5_ROWS, S5_ROWS)
            rows = pl.ds(r0, S5_ROWS)
            gr, gi, uu = gr_scr[rows, :], gi_scr[rows, :], u_ref[rows, :]
            du_ref[rows, :] = dyp_scr[rows, :] * d_ref[...] + _nt(gr, bm_ref[:, :S5_NS]) + _nt(gi, bm_ref[:, S5_NS:])
            return dbr + _tn(uu, gr), dbi + _tn(uu, gi)

        zb = jnp.zeros((LANES, S5_NS), F32)
        dbr, dbi = lax.fori_loop(0, L // S5_ROWS, in_step, (zb, zb))
        dbm_ref[:, :S5_NS] = dbr
        dbm_ref[:, S5_NS:] = dbi

    return pl.pallas_call(
        body, name="s5_bwd", grid=(S5_BLOCKS,),
        in_specs=[col, bm, cm, pw, pw, vec, col], out_specs=[col, bm, cm, pw, vec],
        out_shape=[jax.ShapeDtypeStruct((L, D_MODEL), F32), jax.ShapeDtypeStruct((S5_BLOCKS, LANES, 2 * S5_NS), F32),
                   jax.ShapeDtypeStruct((S5_BLOCKS, 2 * S5_NS, LANES), F32),
                   jax.ShapeDtypeStruct((S5_BLOCKS, SUBLANES, S5_NS), F32), jax.ShapeDtypeStruct((1, D_MODEL), F32)],
        scratch_shapes=[pltpu.VMEM((L, S5_NS), F32)] * 4 + [pltpu.VMEM((L, LANES), F32)],
        compiler_params=_params("parallel"),
    )(u, bmat, cmat, pw_re, pw_im, dsk, dyg)


def _s5_discretize(lam_re, lam_im, log_dt, b_re, b_im, e16):
    dt = jnp.exp(log_dt)
    zr, zi = lam_re * dt, lam_im * dt
    mag = jnp.exp(zr)
    lbr, lbi = mag * jnp.cos(zi), mag * jnp.sin(zi)
    den = lam_re * lam_re + lam_im * lam_im
    nr, ni = lbr - 1.0, lbi
    cr = (nr * lam_re + ni * lam_im) / den
    ci = (ni * lam_re - nr * lam_im) / den
    crw, ciw = _pick(cr, e16), _pick(ci, e16)
    return lbr, lbi, crw * b_re - ciw * b_im, crw * b_im + ciw * b_re


def _s5_params_fwd(lam_re, lam_im, log_dt, b_re, b_im, e16):
    def body(lr, li, ld, br, bi, e, o1, o2, o3, o4):
        for o, val in zip((o1, o2, o3, o4), _s5_discretize(lr[...], li[...], ld[...], br[...], bi[...], e[...])):
            o[...] = val

    g, p, n = S5_GROUPS, S5_STATE, S5_STATE * S5_GROUP
    return pl.pallas_call(
        body, name="s5_params_fwd",
        out_shape=[jax.ShapeDtypeStruct((g, p), F32)] * 2 + [jax.ShapeDtypeStruct((g, n), F32)] * 2,
        compiler_params=_params(),
    )(lam_re, lam_im, log_dt, b_re, b_im, e16)


def _s5_params_bwd(lam_re, lam_im, log_dt, b_re, b_im, e16, cts):
    def body(lr, li, ld, br, bi, e, c1, c2, c3, c4, o1, o2, o3, o4, o5):
        e_m = e[...]
        f = lambda a, b, c, d, g: _s5_discretize(a, b, c, d, g, e_m)
        _, vjp = jax.vjp(f, lr[...], li[...], ld[...], br[...], bi[...])
        for o, val in zip((o1, o2, o3, o4, o5), vjp((c1[...], c2[...], c3[...], c4[...]))):
            o[...] = val

    g, p, n = S5_GROUPS, S5_STATE, S5_STATE * S5_GROUP
    return pl.pallas_call(
        body, name="s5_params_bwd",
        out_shape=[jax.ShapeDtypeStruct((g, p), F32)] * 2 + [jax.ShapeDtypeStruct((g, 1), F32)]
        + [jax.ShapeDtypeStruct((g, n), F32)] * 2,
        compiler_params=_params(),
    )(lam_re, lam_im, log_dt, b_re, b_im, e16, *cts)


def _add_residual(acc, h):
    return (acc + h,)


def _mlp_fwd(i, h, g, w1, w2):
    hn = _rms_fwd(f"mlp{i}_norm", h, g)
    r = _mm(f"mlp{i}_up", hn, w1, "nn", (BF16,), epi=lambda acc: (jnp.square(jnp.maximum(acc, 0.0)),))
    return _mm(f"mlp{i}_down", r, w2, "nn", (F32,), epi=_add_residual, extras=(h,)), (h, hn, r)


def _mlp_bwd(i, dh_out, saved, g, w1, w2):
    h, hn, r = saved
    dw2 = _mm(f"mlp{i}_dw2", r, dh_out, "tn", (BF16,))
    da = _mm(f"mlp{i}_da", dh_out, w2, "nt", (BF16,), epi=lambda acc, rr: (acc * (2.0 * jnp.sqrt(rr.astype(F32))),), extras=(r,))
    dw1 = _mm(f"mlp{i}_dw1", hn, da, "tn", (BF16,))
    dhn = _mm(f"mlp{i}_dhn", da, w1, "nt", (F32,))
    dh, dg = _rms_bwd(f"mlp{i}_dnorm", h, g, dhn, dh_out)
    return dh, dg[0], dw1, dw2


def _lanes(v, n):
    return jnp.broadcast_to(v.reshape(n, 1, 1), (n, 1, LANES))


def _gdn_fwd_layer(i, h, g, w_in, conv_w, a_log, dt_bias, o_g, w_out):
    hn = _rms_fwd(f"gdn{i}_norm", h, g)
    p = _mm(f"gdn{i}_in", hn, w_in, "nn", (F32,))
    zb = jnp.zeros((1, D_MODEL), F32)
    qkv = [_conv_fwd(f"gdn{i}_conv{t}", p, t * D_MODEL, conv_w[:, t * D_MODEL:(t + 1) * D_MODEL], zb) for t in range(3)]
    ea, eb = _gdn_selectors()
    y, sprev, sol = _gdn_fwd(*qkv, p, _lanes(a_log, GDN_HEADS), _lanes(dt_bias, GDN_HEADS), o_g.reshape(1, LANES), ea, eb)
    return _mm(f"gdn{i}_out", y, w_out, "nn", (F32,), epi=_add_residual, extras=(h,)), (h, hn, p, qkv, y, sprev, sol)


def _gdn_bwd_layer(i, dh_out, saved, g, w_in, conv_w, a_log, dt_bias, o_g, w_out):
    h, hn, p, qkv, y, sprev, sol = saved
    dy = _mm(f"gdn{i}_dy", dh_out, w_out, "nt", (F32,))
    dw_out = _mm(f"gdn{i}_dwout", y, dh_out, "tn", (BF16,))
    ea, eb = _gdn_selectors()
    dq, dk, dv, dgate, dab, dpar = _gdn_bwd(*qkv, p, _lanes(a_log, GDN_HEADS), _lanes(dt_bias, GDN_HEADS),
                                            o_g.reshape(1, LANES), ea, eb, sprev, sol, dy)
    zb = jnp.zeros((1, D_MODEL), F32)
    dpre, dcw = [], []
    for t, d in enumerate((dq, dk, dv)):
        dx, dw, _ = _conv_bwd(f"gdn{i}_dconv{t}", p, t * D_MODEL, conv_w[:, t * D_MODEL:(t + 1) * D_MODEL], zb, d)
        dpre.append(dx)
        dcw.append(dw)
    dp = jnp.concatenate(dpre + [dgate, dab], axis=1).astype(BF16)
    dw_in = _mm(f"gdn{i}_dwin", hn, dp, "tn", (BF16,))[:, :GDN_IN]
    dhn = _mm(f"gdn{i}_dhn", dp, w_in, "nt", (F32,))
    dh, dg = _rms_bwd(f"gdn{i}_dnorm", h, g, dhn, dh_out)
    grads = dict(w_in=dw_in, conv_w=jnp.concatenate(dcw, axis=1), a_log=jnp.sum(dpar[:, 0, :], axis=-1),
                 dt_bias=jnp.sum(dpar[:, 1, :], axis=-1), o_norm_g=jnp.sum(dpar[:, 2, :], axis=0), w_out=dw_out)
    return dh, dg[0], grads


def _m2_vectors(dt_bias, a_log, d_skip, norm_g):
    pad = lambda v: jnp.pad(v, (0, LANES - M2_HEADS)).reshape(1, LANES)
    return pad(dt_bias), pad(a_log), jnp.repeat(d_skip, M2_HD).reshape(1, M2_INNER), norm_g.reshape(1, M2_INNER)


def _m2_fwd_layer(h, g, w_in, conv_w, conv_b, dt_bias, a_log, d_skip, norm_g, w_out):
    hn = _rms_fwd("m2_norm", h, g)
    p = _mm("m2_in", hn, w_in, "nn", (F32,))
    xbc = _conv_fwd("m2_conv", p, M2_INNER, conv_w, conv_b.reshape(1, M2_CONV_CH))
    e, ecol = _m2_selectors()
    y, sprev = _m2_fwd(xbc, p, *_m2_vectors(dt_bias, a_log, d_skip, norm_g), e, ecol)
    return _mm("m2_out", y, w_out, "nn", (F32,), epi=_add_residual, extras=(h,)), (h, hn, p, xbc, y, sprev)


def _m2_bwd_layer(dh_out, saved, g, w_in, conv_w, conv_b, dt_bias, a_log, d_skip, norm_g, w_out):
    h, hn, p, xbc, y, sprev = saved
    dy = _mm("m2_dy", dh_out, w_out, "nt", (F32,))
    dw_out = _mm("m2_dwout", y, dh_out, "tn", (BF16,))
    e, ecol = _m2_selectors()
    dx, db, dc, dz, ddt, dnar, dwide = _m2_bwd(xbc, p, *_m2_vectors(dt_bias, a_log, d_skip, norm_g), e, ecol, sprev, dy)
    dxbc, dcw, dcb = _conv_bwd("m2_dconv", p, M2_INNER, conv_w, conv_b.reshape(1, M2_CONV_CH),
                               jnp.concatenate([dx, db, dc], axis=1))
    dp = jnp.concatenate([dz, dxbc, ddt], axis=1).astype(BF16)
    dw_in = _mm("m2_dwin", hn, dp, "tn", (BF16,))[:, :M2_IN]
    dhn = _mm("m2_dhn", dp, w_in, "nt", (F32,))
    dh, dg = _rms_bwd("m2_dnorm", h, g, dhn, dh_out)
    grads = dict(w_in=dw_in, conv_w=dcw, conv_b=dcb[0], dt_bias=dnar[0, :M2_HEADS], a_log=dnar[1, :M2_HEADS],
                 d=jnp.sum(dwide[:, 0, :].reshape(M2_HEADS, M2_HD), axis=-1), norm_g=dwide[:, 1, :].reshape(M2_INNER),
                 w_out=dw_out)
    return dh, dg[0], grads


def _s5_selector():
    e16 = np.zeros((S5_STATE, S5_STATE * S5_GROUP), np.float32)
    for p in range(S5_STATE):
        e16[p, p * S5_GROUP:(p + 1) * S5_GROUP] = 1.0
    return jnp.asarray(e16)


def _s5_operands(lbr, lbi, bbr, bbi, c_re, c_im):
    eye = jnp.eye(S5_BLOCKS, dtype=F32)
    gpb = S5_GROUPS // S5_BLOCKS
    bd = lambda t: jnp.einsum("jgpk,gh->jgkhp", t.reshape(S5_BLOCKS, gpb, S5_STATE, S5_GROUP), eye).reshape(S5_BLOCKS, LANES, S5_NS)
    cd = lambda t: jnp.einsum("jgkp,gh->jgphk", t.reshape(S5_BLOCKS, gpb, S5_GROUP, S5_STATE), eye).reshape(S5_BLOCKS, S5_NS, LANES)
    bmat = jnp.concatenate([bd(bbr), bd(bbi)], axis=2).astype(BF16)
    cmat = jnp.concatenate([cd(c_re), -cd(c_im)], axis=1).astype(BF16)
    ar, ai = lbr.reshape(S5_BLOCKS, S5_NS), lbi.reshape(S5_BLOCKS, S5_NS)
    pr, pi = [ar], [ai]
    for _ in range(SUBLANES - 1):
        pr, pi = pr + [pr[-1] * ar - pi[-1] * ai], pi + [pr[-1] * ai + pi[-1] * ar]
    return bmat, cmat, jnp.stack(pr, axis=1), jnp.stack(pi, axis=1)


def _s5_fwd_layer(h, g, w_in, lam_re, lam_im, log_dt, b_re, b_im, c_re, c_im, d_skip, w_out):
    hn = _rms_fwd("s5_norm", h, g)
    u = _mm("s5_in", hn, w_in, "nn", (F32,))
    n = S5_STATE * S5_GROUP
    lbr, lbi, bbr, bbi = _s5_params_fwd(lam_re, lam_im, log_dt.reshape(S5_GROUPS, 1), b_re.reshape(S5_GROUPS, n),
                                        b_im.reshape(S5_GROUPS, n), _s5_selector())
    ops = _s5_operands(lbr, lbi, bbr, bbi, c_re, c_im)
    yg = _s5_fwd(u, *ops, d_skip.reshape(1, D_MODEL))
    ag = _mm("s5_out", yg, w_out, "nn", (F32,))
    return _glu_fwd(h, ag), (h, hn, u, ops, yg, ag)


def _s5_bwd_layer(dh_out, saved, g, w_in, lam_re, lam_im, log_dt, b_re, b_im, c_re, c_im, d_skip, w_out):
    h, hn, u, ops, yg, ag = saved
    dag = _glu_bwd(dh_out, ag)
    dw_out = _mm("s5_dwout", yg, dag, "tn", (BF16,))
    dyg = _mm("s5_dyg", dag, w_out, "nt", (F32,))
    du, dbmat, dcmat, dlam, ddsk = _s5_bwd(u, *ops, d_skip.reshape(1, D_MODEL), dyg)
    eye = jnp.eye(S5_BLOCKS, dtype=F32)
    gpb = S5_GROUPS // S5_BLOCKS
    n = S5_STATE * S5_GROUP
    ub = lambda t: jnp.einsum("jgkhp,gh->jgpk", t.reshape(S5_BLOCKS, gpb, S5_GROUP, gpb, S5_STATE), eye).reshape(S5_GROUPS, n)
    uc = lambda t: jnp.einsum("jgphk,gh->jgkp", t.reshape(S5_BLOCKS, gpb, S5_STATE, gpb, S5_GROUP), eye).reshape(c_re.shape)
    cts = (dlam[:, 0, :].reshape(S5_GROUPS, S5_STATE), dlam[:, 1, :].reshape(S5_GROUPS, S5_STATE),
           ub(dbmat[:, :, :S5_NS]), ub(dbmat[:, :, S5_NS:]))
    dlr, dli, dld, dbr, dbi = _s5_params_bwd(lam_re, lam_im, log_dt.reshape(S5_GROUPS, 1), b_re.reshape(S5_GROUPS, n),
                                             b_im.reshape(S5_GROUPS, n), _s5_selector(), cts)
    dw_in = _mm("s5_dwin", hn, du, "tn", (BF16,))
    dhn = _mm("s5_dhn", du, w_in, "nt", (F32,))
    dh, dg = _rms_bwd("s5_dnorm", h, g, dhn, dh_out)
    grads = dict(w_in=dw_in, lam_re=dlr, lam_im=dli, log_dt=dld[:, 0], b_re=dbr.reshape(b_re.shape), b_im=dbi.reshape(b_im.shape),
                 c_re=uc(dcmat[:, :S5_NS, :]), c_im=-uc(dcmat[:, S5_NS:, :]), d=ddsk[0], w_out=dw_out)
    return dh, dg[0], grads


MIXER_OF_LAYER = ("gdn", "s5", "m2", "gdn")
MIXER_INDEX = (0, 0, 0, 1)


def _mixer_args(W, i):
    kind, j = MIXER_OF_LAYER[i], MIXER_INDEX[i]
    if kind == "gdn":
        return tuple(W["gdn_" + k][j] for k in ("w_in", "conv_w", "a_log", "dt_bias", "o_norm_g", "w_out"))
    if kind == "s5":
        return tuple(W["s5_" + k][j] for k in ("w_in", "lam_re", "lam_im", "log_dt", "b_re", "b_im", "c_re", "c_im", "d", "w_out"))
    return tuple(W["m2_" + k][j] for k in ("w_in", "conv_w", "conv_b", "dt_bias", "a_log", "d", "norm_g", "w_out"))


def _local_step(x, target, W, on_layer_grads):
    h = x
    saved = []
    for i in range(DEPTH):
        kind = MIXER_OF_LAYER[i]
        args = _mixer_args(W, i)
        if kind == "gdn":
            h, sm = _gdn_fwd_layer(i, h, W["norm_mix_g"][i], *args)
        elif kind == "s5":
            h, sm = _s5_fwd_layer(h, W["norm_mix_g"][i], *args)
        else:
            h, sm = _m2_fwd_layer(h, W["norm_mix_g"][i], *args)
        h, sp = _mlp_fwd(i, h, W["norm_mlp_g"][i], W["mlp_w1"][i], W["mlp_w2"][i])
        saved.append((sm, sp))
    loss, dh, dgf = _loss_head(h, W["final_norm_g"], target)
    G = {"final_norm_g": dgf[0], "norm_mix_g": [None] * DEPTH, "norm_mlp_g": [None] * DEPTH,
         "mlp_w1": [None] * DEPTH, "mlp_w2": [None] * DEPTH}
    mix = {}
    for i in reversed(range(DEPTH)):
        kind = MIXER_OF_LAYER[i]
        sm, sp = saved[i]
        dh, G["norm_mlp_g"][i], G["mlp_w1"][i], G["mlp_w2"][i] = _mlp_bwd(i, dh, sp, W["norm_mlp_g"][i], W["mlp_w1"][i], W["mlp_w2"][i])
        args = _mixer_args(W, i)
        if kind == "gdn":
            dh, G["norm_mix_g"][i], gm = _gdn_bwd_layer(i, dh, sm, W["norm_mix_g"][i], *args)
        elif kind == "s5":
            dh, G["norm_mix_g"][i], gm = _s5_bwd_layer(dh, sm, W["norm_mix_g"][i], *args)
        else:
            dh, G["norm_mix_g"][i], gm = _m2_bwd_layer(dh, sm, W["norm_mix_g"][i], *args)
        j = MIXER_INDEX[i]
        on_layer_grads(i, {("mlp_w1", i): G["mlp_w1"][i], ("mlp_w2", i): G["mlp_w2"][i],
                           (kind + "_w_in", j): gm["w_in"], (kind + "_w_out", j): gm["w_out"]})
        for k, v in gm.items():
            mix.setdefault(kind + "_" + k, {})[j] = v
    for k, d in mix.items():
        G[k] = [d[j] for j in sorted(d)]
    return loss, dh, {k: jnp.stack(v) if isinstance(v, list) else v for k, v in G.items() if k not in BIG}


ADAM_ROWS = 128
ADAM_COLS = 128


def _adamw(name, w, g, m, v):
    R, C = w.shape
    if R % ADAM_ROWS == 0:
        grid, blk = (R // ADAM_ROWS,), pl.BlockSpec((ADAM_ROWS, C), lambda i: (i, 0))
    else:
        grid, blk = (C // ADAM_COLS,), pl.BlockSpec((R, ADAM_COLS), lambda j: (0, j))

    def body(w_ref, g_ref, m_ref, v_ref, d_ref, mo_ref, vo_ref):
        gg = g_ref[...]
        mn = ADAM_B1 * m_ref[...] + (1.0 - ADAM_B1) * gg
        vn = ADAM_B2 * v_ref[...] + (1.0 - ADAM_B2) * (gg * gg)
        m_hat = mn / (1.0 - ADAM_B1 ** ADAM_STEP)
        v_hat = vn / (1.0 - ADAM_B2 ** ADAM_STEP)
        d_ref[...] = -ADAM_LR * (m_hat / (jnp.sqrt(v_hat) + ADAM_EPS) + ADAM_WD * w_ref[...])
        mo_ref[...] = mn
        vo_ref[...] = vn

    return pl.pallas_call(
        body, name=name, grid=grid, in_specs=[blk] * 4, out_specs=[blk] * 3,
        out_shape=[jax.ShapeDtypeStruct((R, C), F32)] * 3, compiler_params=_params("parallel"),
    )(w, g, m, v)


MESH = pl.DeviceIdType.MESH
ANY = pl.BlockSpec(memory_space=pl.ANY)
N_CHIPS = 4
N_DEV = 8


def _position():
    return lax.axis_index("x"), lax.axis_index("y"), lax.axis_index("c")


GATHER_IDS = {1: 1, 2: 2}
EXCHANGE_IDS = {0: 4, 1: 5, 2: 6, 3: 7}


LINK_SLOWDOWN = 40


def _link_cost(link_bytes):
    return pl.CostEstimate(flops=0, transcendentals=0, bytes_accessed=LINK_SLOWDOWN * link_bytes)


def _gather_body(w_refs, out_refs, send_sems, recv_sems):
    x, y, c = _position()
    sibling = (x, y, 1 - c)
    chips = [(1 - x, y), (x, 1 - y), (1 - x, 1 - y)]
    firsts, passes = [], []
    for t, (w_ref, out_ref) in enumerate(zip(w_refs, out_refs)):
        half = w_ref.shape[0] // 2

        def piece(cx, cy, hc, out_ref=out_ref, half=half):
            return out_ref.at[2 * cx + cy, pl.ds(hc * half, half), :]

        def copy(k, src, dst, to, t=t):
            return pltpu.make_async_remote_copy(src_ref=src, dst_ref=dst, send_sem=send_sems.at[6 * t + k],
                                                recv_sem=recv_sems.at[6 * t + k], device_id=to, device_id_type=MESH)

        first = [copy(j, w_ref.at[pl.ds(c * half, half), :], piece(x, y, c), (*chip, c)) for j, chip in enumerate(chips)]
        for cp in first:
            cp.start()
        firsts.append((first, piece, copy))
    for first, piece, copy in firsts:
        passed = [copy(3 + j, piece(*chip, c), piece(*chip, c), sibling) for j, chip in enumerate(chips)]
        for j, chip in enumerate(chips):
            copy(j, piece(*chip, c), piece(*chip, c), sibling).wait_recv()
            passed[j].start()
        passes.append(passed)
    for (first, piece, copy), passed in zip(firsts, passes):
        for j, chip in enumerate(chips):
            copy(3 + j, piece(*chip, 1 - c), piece(*chip, 1 - c), sibling).wait_recv()
        for cp in first + passed:
            cp.wait_send()


def _gather_shards(wps):
    n = len(wps)

    def body(*refs):
        _gather_body(refs[:n], refs[n:2 * n], *refs[2 * n:])

    return pl.pallas_call(
        body, name="gather_shards", in_specs=[ANY] * n, out_specs=[ANY] * n,
        out_shape=[jax.ShapeDtypeStruct((N_CHIPS, *wp.shape), wp.dtype) for wp in wps],
        scratch_shapes=[pltpu.SemaphoreType.DMA((6 * n,)), pltpu.SemaphoreType.DMA((6 * n,))],
    )(*wps)


def _gather_shards_later(wps, part):
    n = len(wps)
    w_refs = [jax.new_ref(wp, memory_space=pltpu.MemorySpace.HBM) for wp in wps]
    out_refs = [jax.empty_ref(jax.ShapeDtypeStruct((N_CHIPS, *wp.shape), wp.dtype), memory_space=pltpu.MemorySpace.HBM)
                for wp in wps]

    @pl.kernel(mesh=plsc.ScalarSubcoreMesh(axis_name="sequencer", num_cores=1), name=f"gather_shards_later{part}",
               scratch_types=(pltpu.SemaphoreType.DMA((6 * n,)), pltpu.SemaphoreType.DMA((6 * n,))),
               cost_estimate=_link_cost(3 * sum(wp.size * wp.dtype.itemsize for wp in wps)),
               compiler_params=pltpu.CompilerParams(collective_id=GATHER_IDS[part]))
    def launch(send_sems, recv_sems):
        x, y, c = _position()
        barrier = pltpu.get_barrier_semaphore()
        for peer in [(x, y, 1 - c), (1 - x, y, c), (x, 1 - y, c), (1 - x, 1 - y, c)]:
            pl.semaphore_signal(barrier, inc=1, device_id=peer, device_id_type=MESH)
        pl.semaphore_wait(barrier, 4)
        _gather_body(w_refs, out_refs, send_sems, recv_sems)

    launch()
    return [r[...] for r in out_refs]


def _pair_exchange(name, gps):
    n = len(gps)

    def body(*refs):
        g_refs, out_refs, (send_sems, recv_sems) = refs[:n], refs[n:2 * n], refs[2 * n:]
        x, y, c = _position()
        copies = []
        for t, (g_ref, out_ref) in enumerate(zip(g_refs, out_refs)):
            half = g_ref.shape[1] // 2
            copies += [pltpu.make_async_remote_copy(
                src_ref=g_ref.at[k, pl.ds((1 - c) * half, half), :], dst_ref=out_ref.at[k], send_sem=send_sems.at[N_CHIPS * t + k],
                recv_sem=recv_sems.at[N_CHIPS * t + k], device_id=(x, y, 1 - c), device_id_type=MESH) for k in range(N_CHIPS)]
        for cp in copies:
            cp.start()
        for cp in copies:
            cp.wait()

    return pl.pallas_call(
        body, name=name, in_specs=[ANY] * n, out_specs=[ANY] * n,
        out_shape=[jax.ShapeDtypeStruct((N_CHIPS, gp.shape[1] // 2, gp.shape[2]), gp.dtype) for gp in gps],
        scratch_shapes=[pltpu.SemaphoreType.DMA((N_CHIPS * n,)), pltpu.SemaphoreType.DMA((N_CHIPS * n,))],
    )(*gps)


SUM_ROWS = (256, 128)


def _pair_sum(name, gp, got, core):
    n, R, C = gp.shape
    half = R // 2
    tr = _tile(half, SUM_ROWS)
    nb = half // tr

    def body(core_ref, g_ref, r_ref, o_ref):
        o_ref[...] = (g_ref[...].astype(F32) + r_ref[...].astype(F32)).astype(o_ref.dtype)

    return pl.pallas_call(
        body, name=name,
        grid_spec=pltpu.PrefetchScalarGridSpec(
            num_scalar_prefetch=1, grid=(n, nb),
            in_specs=[pl.BlockSpec((None, tr, C), lambda k, i, core_ref: (k, core_ref[0] * nb + i, 0)),
                      pl.BlockSpec((None, tr, C), lambda k, i, core_ref: (k, i, 0))],
            out_specs=pl.BlockSpec((None, tr, C), lambda k, i, core_ref: (k, i, 0))),
        out_shape=jax.ShapeDtypeStruct((n, half, C), gp.dtype), compiler_params=_params("parallel", "parallel"),
    )(core, gp, got)


def _chip_exchange_body(t_refs, out_refs, send_sems, recv_sems):
    x, y, c = _position()
    chips = [(1 - x, y), (x, 1 - y), (1 - x, 1 - y)]
    copies, waits = [], []
    for t, (t_ref, out_ref) in enumerate(zip(t_refs, out_refs)):
        for j, (cx, cy) in enumerate(chips):
            sems = dict(send_sem=send_sems.at[3 * t + j], recv_sem=recv_sems.at[3 * t + j], device_id=(cx, cy, c),
                        device_id_type=MESH)
            copies.append(pltpu.make_async_remote_copy(src_ref=t_ref.at[2 * cx + cy], dst_ref=out_ref.at[2 * x + y], **sems))
            waits.append(pltpu.make_async_remote_copy(src_ref=t_ref.at[2 * cx + cy], dst_ref=out_ref.at[2 * cx + cy], **sems))
    for cp in copies:
        cp.start()
    for cp in waits:
        cp.wait_recv()
    for cp in copies:
        cp.wait_send()


def _chip_exchange_later(ts, layer):
    n = len(ts)
    t_refs = [jax.new_ref(t, memory_space=pltpu.MemorySpace.HBM) for t in ts]
    out_refs = [jax.empty_ref(jax.ShapeDtypeStruct(t.shape, t.dtype), memory_space=pltpu.MemorySpace.HBM) for t in ts]

    @pl.kernel(mesh=plsc.ScalarSubcoreMesh(axis_name="sequencer", num_cores=1), name=f"chip_exchange_later{layer}",
               scratch_types=(pltpu.SemaphoreType.DMA((3 * n,)), pltpu.SemaphoreType.DMA((3 * n,))),
               cost_estimate=_link_cost(3 * sum(t.size * t.dtype.itemsize for t in ts) // N_CHIPS),
               compiler_params=pltpu.CompilerParams(collective_id=EXCHANGE_IDS[layer]))
    def launch(send_sems, recv_sems):
        x, y, c = _position()
        barrier = pltpu.get_barrier_semaphore()
        for peer in [(1 - x, y, c), (x, 1 - y, c), (1 - x, 1 - y, c)]:
            pl.semaphore_signal(barrier, inc=1, device_id=peer, device_id_type=MESH)
        pl.semaphore_wait(barrier, 3)
        _chip_exchange_body(t_refs, out_refs, send_sems, recv_sems)

    launch()
    return [r[...] for r in out_refs]


def _chip_sum(name, t, got, ids):
    n, H, C = t.shape
    tr = _tile(H, SUM_ROWS)
    nb = H // tr

    def body(ids_ref, t_ref, r_ref, o_ref):
        own = t_ref[...].astype(F32)
        acc = jnp.where(ids_ref[0] == 0, own, r_ref[0].astype(F32))
        for k in range(1, n):
            acc = acc + jnp.where(ids_ref[0] == k, own, r_ref[k].astype(F32))
        o_ref[...] = acc

    return pl.pallas_call(
        body, name=name,
        grid_spec=pltpu.PrefetchScalarGridSpec(
            num_scalar_prefetch=1, grid=(nb,),
            in_specs=[pl.BlockSpec((None, tr, C), lambda i, ids_ref: (ids_ref[0], i, 0)),
                      pl.BlockSpec((n, tr, C), lambda i, ids_ref: (0, i, 0))],
            out_specs=pl.BlockSpec((tr, C), lambda i, ids_ref: (ids_ref[1] * nb + i, 0))),
        out_shape=jax.ShapeDtypeStruct((2 * H, C), F32), compiler_params=_params("parallel"),
    )(ids, t, got)


def _sum_pieces(name, pieces):
    n, R, C = pieces.shape
    tr = _tile(R, (256, 128, SUBLANES))

    def body(p_ref, o_ref):
        acc = p_ref[0].astype(F32)
        for s in range(1, n):
            acc = acc + p_ref[s].astype(F32)
        o_ref[...] = acc

    return pl.pallas_call(
        body, name=name, grid=(R // tr,),
        in_specs=[pl.BlockSpec((n, tr, C), lambda i: (0, i, 0))], out_specs=pl.BlockSpec((tr, C), lambda i: (i, 0)),
        out_shape=jax.ShapeDtypeStruct((R, C), F32), compiler_params=_params("parallel"),
    )(pieces)


def _swap_halves(name, ss):
    n = len(ss)

    def body(*refs):
        s_refs, out_refs, (send_sems, recv_sems) = refs[:n], refs[n:2 * n], refs[2 * n:]
        x, y, c = _position()
        copies, waits = [], []
        for t, (s_ref, out_ref) in enumerate(zip(s_refs, out_refs)):
            half = s_ref.shape[0] // 2
            sems = dict(send_sem=send_sems.at[t], recv_sem=recv_sems.at[t], device_id=(x, y, 1 - c), device_id_type=MESH)
            mine = s_ref.at[pl.ds(c * half, half), :]
            copies.append(pltpu.make_async_remote_copy(src_ref=mine, dst_ref=out_ref.at[pl.ds(c * half, half), :], **sems))
            waits.append(pltpu.make_async_remote_copy(src_ref=mine, dst_ref=out_ref.at[pl.ds((1 - c) * half, half), :], **sems))
        for cp in copies:
            cp.start()
        for cp in waits:
            cp.wait_recv()
        for cp in copies:
            cp.wait_send()

    return pl.pallas_call(
        body, name=name, in_specs=[ANY] * n, out_specs=[ANY] * n, input_output_aliases={i: i for i in range(n)},
        out_shape=[jax.ShapeDtypeStruct(s_.shape, s_.dtype) for s_ in ss],
        scratch_shapes=[pltpu.SemaphoreType.DMA((n,)), pltpu.SemaphoreType.DMA((n,))],
    )(*ss)


def _gather_small(name, blk):
    m_per, n = blk.shape

    def body(x_ref, out_ref, send_sems, recv_sems, local_sem):
        x, y, c = _position()
        me, sibling = (x, y, c), (x, y, 1 - c)
        chips = [(1 - x, y), (x, 1 - y), (1 - x, 1 - y)]

        def rows(px, py, pc):
            return out_ref.at[pl.ds((4 * px + 2 * py + pc) * m_per, m_per), :]

        def copy(k, block, to, src=None):
            return pltpu.make_async_remote_copy(src_ref=rows(*block) if src is None else src, dst_ref=rows(*block),
                                                send_sem=send_sems.at[k], recv_sem=recv_sems.at[k], device_id=to, device_id_type=MESH)

        mine = pltpu.make_async_copy(x_ref, rows(*me), local_sem)
        mine.start()
        first = [copy(0, me, sibling, src=x_ref)] + [copy(1 + j, me, (*chip, c), src=x_ref) for j, chip in enumerate(chips)]
        for cp in first:
            cp.start()
        passed = [copy(4 + j, (*chip, c), sibling) for j, chip in enumerate(chips)]
        for j, chip in enumerate(chips):
            copy(1 + j, (*chip, c), me).wait_recv()
            passed[j].start()
        copy(0, sibling, me).wait_recv()
        for j, chip in enumerate(chips):
            copy(4 + j, (*chip, 1 - c), me).wait_recv()
        for cp in first + passed:
            cp.wait_send()
        mine.wait()

    return pl.pallas_call(
        body, name=name, out_shape=jax.ShapeDtypeStruct((N_DEV * m_per, n), blk.dtype),
        in_specs=[pl.BlockSpec(memory_space=pltpu.VMEM)], out_specs=pl.BlockSpec(memory_space=pltpu.VMEM),
        scratch_shapes=[pltpu.SemaphoreType.DMA((7,)), pltpu.SemaphoreType.DMA((7,)), pltpu.SemaphoreType.DMA],
        compiler_params=pltpu.CompilerParams(vmem_limit_bytes=VMEM_LIMIT_BYTES),
    )(blk)


WEIGHTS = ("norm_mix_g", "norm_mlp_g", "mlp_w1", "mlp_w2", "gdn_w_in", "gdn_conv_w", "gdn_a_log", "gdn_dt_bias", "gdn_o_norm_g",
           "gdn_w_out", "s5_w_in", "s5_lam_re", "s5_lam_im", "s5_log_dt", "s5_b_re", "s5_b_im", "s5_c_re", "s5_c_im", "s5_d",
           "s5_w_out", "m2_w_in", "m2_conv_w", "m2_conv_b", "m2_dt_bias", "m2_a_log", "m2_d", "m2_norm_g", "m2_w_out",
           "final_norm_g")
BIG = {"mlp_w1": 2, "mlp_w2": 1, "gdn_w_in": 2, "gdn_w_out": 1, "s5_w_in": 1, "s5_w_out": 2, "m2_w_in": 2, "m2_w_out": 1}
SMALL_CUT = {"gdn_conv_w": 2, "m2_conv_w": 2, "m2_conv_b": 1, "m2_norm_g": 1}
ROWS_MINOR = ("m2_w_in",)
WEIGHT_PARTS = (
    ((("gdn_w_out", 0),), (("gdn_w_in", 0),)),
    ((("mlp_w1", 0), ("mlp_w2", 0), ("mlp_w1", 1), ("mlp_w2", 1), ("s5_w_in", 0)), (("s5_w_out", 0),)),
    ((("mlp_w1", 2), ("mlp_w2", 2), ("m2_w_out", 0), ("mlp_w1", 3), ("mlp_w2", 3), ("gdn_w_out", 1)), (("m2_w_in", 0),),
     (("gdn_w_in", 1),)),
)
LAYER_ITEMS = (
    ((("mlp_w1", 0), ("mlp_w2", 0), ("gdn_w_out", 0)), (("gdn_w_in", 0),)),
    ((("mlp_w1", 1), ("mlp_w2", 1), ("s5_w_in", 0)), (("s5_w_out", 0),)),
    ((("mlp_w1", 2), ("mlp_w2", 2), ("m2_w_out", 0)), (("m2_w_in", 0),)),
    ((("mlp_w1", 3), ("mlp_w2", 3), ("gdn_w_out", 1)), (("gdn_w_in", 1),)),
)


def _rows2d(a):
    return a.reshape(-1, a.shape[-1])


def _pack(arrays, cols, row_multiple, dtype):
    flat = jnp.concatenate([a.reshape(-1).astype(dtype) for a in arrays])
    n = -(-flat.shape[0] // (cols * row_multiple)) * cols * row_multiple
    return jnp.pad(flat, (0, n - flat.shape[0])).reshape(-1, cols)


def _unpack(packed, shapes):
    flat = packed.reshape(-1)
    out, off = [], 0
    for shp in shapes:
        n = math.prod(shp)
        out.append(flat[off:off + n].reshape(shp))
        off += n
    return out


def _split_rows(buf, shapes):
    out, off = [], 0
    for shp in shapes:
        rows = math.prod(shp[:-1])
        out.append(buf[off:off + rows].reshape(shp))
        off += rows
    return out


def _cut(a, axis, k):
    n = a.shape[axis] // N_CHIPS
    return lax.slice_in_dim(a, k * n, (k + 1) * n, axis=axis)


def kernel(x, norm_mix_g, norm_mlp_g, mlp_w1, mlp_w2, gdn_w_in, gdn_conv_w, gdn_a_log, gdn_dt_bias, gdn_o_norm_g, gdn_w_out, s5_w_in, s5_lam_re, s5_lam_im, s5_log_dt, s5_b_re, s5_b_im, s5_c_re, s5_c_im, s5_d, s5_w_out, m2_w_in, m2_conv_w, m2_conv_b, m2_dt_bias, m2_a_log, m2_d, m2_norm_g, m2_w_out, final_norm_g, loss_target, m_norm_mix_g, m_norm_mlp_g, m_mlp_w1, m_mlp_w2, m_gdn_w_in, m_gdn_conv_w, m_gdn_a_log, m_gdn_dt_bias, m_gdn_o_norm_g, m_gdn_w_out, m_s5_w_in, m_s5_lam_re, m_s5_lam_im, m_s5_log_dt, m_s5_b_re, m_s5_b_im, m_s5_c_re, m_s5_c_im, m_s5_d, m_s5_w_out, m_m2_w_in, m_m2_conv_w, m_m2_conv_b, m_m2_dt_bias, m_m2_a_log, m_m2_d, m_m2_norm_g, m_m2_w_out, m_final_norm_g, v_norm_mix_g, v_norm_mlp_g, v_mlp_w1, v_mlp_w2, v_gdn_w_in, v_gdn_conv_w, v_gdn_a_log, v_gdn_dt_bias, v_gdn_o_norm_g, v_gdn_w_out, v_s5_w_in, v_s5_lam_re, v_s5_lam_im, v_s5_log_dt, v_s5_b_re, v_s5_b_im, v_s5_c_re, v_s5_c_im, v_s5_d, v_s5_w_out, v_m2_w_in, v_m2_conv_w, v_m2_conv_b, v_m2_dt_bias, v_m2_a_log, v_m2_d, v_m2_norm_g, v_m2_w_out, v_final_norm_g):
    given = dict(locals())
    w = {n: given[n] for n in WEIGHTS}
    mom = {n: given["m_" + n] for n in WEIGHTS}
    var = {n: given["v_" + n] for n in WEIGHTS}
    big, small_cut = tuple(BIG), tuple(SMALL_CUT)
    small = tuple(n for n in WEIGHTS if n not in BIG)
    chip = 2 * lax.axis_index("x") + lax.axis_index("y")

    W = {n: [None] * w[n].shape[0] for n in big}

    def fetch(groups, gather):
        own = [jnp.concatenate([w[n][l] for n, l in grp]).astype(BF16) for grp in groups]
        for grp, mine, got in zip(groups, own, gather(own)):
            shapes = [w[n][l].shape for n, l in grp]
            per_chip = [_split_rows(jnp.where(chip == k, mine, got[k]), shapes) for k in range(N_CHIPS)]
            for i, (n, l) in enumerate(grp):
                m = jnp.concatenate([per_chip[k][i] for k in range(N_CHIPS)], axis=BIG[n] - 1)
                pad = {"gdn_w_in": GDN_IN_PAD - GDN_IN, "m2_w_in": M2_IN_PAD - M2_IN}.get(n, 0)
                W[n][l] = jnp.pad(m, ((0, 0), (0, pad))) if pad else m

    for part in (1, 2):
        fetch(WEIGHT_PARTS[part], functools.partial(_gather_shards_later, part=part))
    fetch(WEIGHT_PARTS[0], _gather_shards)
    cut_blk = _pack([w[n] for n in small_cut], LANES, SUBLANES, F32)
    cut_all = _gather_small("gather_small_params", cut_blk).reshape(N_DEV, *cut_blk.shape)
    per_chip = [_unpack(cut_all[2 * k], [w[n].shape for n in small_cut]) for k in range(N_CHIPS)]
    W.update({n: jnp.concatenate([per_chip[k][i] for k in range(N_CHIPS)], axis=SMALL_CUT[n]) for i, n in enumerate(small_cut)})
    W.update({n: w[n] for n in small if n not in SMALL_CUT})

    core = lax.axis_index("c").astype(jnp.int32)
    ids = jnp.stack([chip.astype(jnp.int32), core])
    shard_grads = {}

    def reduce_layer(i, dws):
        groups = LAYER_ITEMS[i]
        gps = [jnp.stack([jnp.concatenate([_cut(dws[it], BIG[it[0]] - 1, k) for it in grp]).astype(BF16) for k in range(N_CHIPS)])
               for grp in groups]
        pairs = [_pair_sum(f"pair_sum{i}_{j}", gp, got, core.reshape(1))
                 for j, (gp, got) in enumerate(zip(gps, _pair_exchange(f"pair_exchange{i}", gps)))]
        sums = [_chip_sum(f"chip_sum{i}_{j}", t, got, ids) for j, (t, got) in enumerate(zip(pairs, _chip_exchange_later(pairs, i)))]
        for grp, g_shard in zip(groups, _swap_halves(f"swap_halves{i}", sums)):
            shard_grads.update(zip(grp, _split_rows(g_shard, [w[n][l].shape for n, l in grp])))

    loss, grad_x, G = _local_step(x[0], loss_target[0], W, reduce_layer)
    loss = lax.psum(loss[0, 0], ("x", "y", "c"))
    grads = {n: jnp.stack([shard_grads[n, l] for l in range(w[n].shape[0])]) for n in big}
    sg = _pack([G[n] for n in small], LANES, ADAM_ROWS, F32)
    sg_sum = _sum_pieces("sum_small_grads", _gather_small("gather_small_grads", sg).reshape(N_DEV, *sg.shape))
    for n, g in zip(small, _unpack(sg_sum, [G[n].shape for n in small])):
        if n in SMALL_CUT:
            width = g.shape[SMALL_CUT[n]] // N_CHIPS
            g = lax.dynamic_slice_in_dim(g, chip * width, width, axis=SMALL_CUT[n])
        grads[n] = g.reshape(w[n].shape)

    delta, new_m, new_v = {}, {}, {}
    for n in big:
        if n in ROWS_MINOR:
            as2d = lambda a: jnp.swapaxes(a, -1, -2).reshape(-1, a.shape[-2])
            back = lambda o: jnp.swapaxes(o.reshape(w[n].shape[0], w[n].shape[2], w[n].shape[1]), -1, -2)
        else:
            as2d = lambda a: a.reshape(-1, a.shape[-1])
            back = lambda o: o.reshape(w[n].shape)
        outs = _adamw("adamw_" + n, as2d(w[n]), as2d(grads[n]), as2d(mom[n]), as2d(var[n]))
        delta[n], new_m[n], new_v[n] = (back(o) for o in outs)
    packs = [_pack([t[n] for n in small], LANES, ADAM_ROWS, F32) for t in (w, grads, mom, var)]
    outs = _adamw("adamw_small", *packs)
    for t, o in zip((delta, new_m, new_v), outs):
        t.update(zip(small, _unpack(o, [w[n].shape for n in small])))

    return (loss, grad_x[None], *[grads[n] for n in WEIGHTS], *[delta[n] for n in WEIGHTS], *[new_m[n] for n in WEIGHTS],
            *[new_v[n] for n in WEIGHTS])
```

```python
import functools
import math

import numpy as np
import jax
import jax.numpy as jnp
from jax import lax
from jax.experimental import pallas as pl
from jax.experimental.pallas import tpu as pltpu
from jax.experimental.pallas import tpu_sc as plsc

F32 = jnp.float32
BF16 = jnp.bfloat16

D_MODEL = 1024
D_FF = 4096
DEPTH = 4
CHUNK = 64
RMS_EPS = 1e-6
CONV_W = 4
GDN_HEADS = 8
GDN_DK = 128
GDN_IN = 4112
GDN_IN_PAD = 4224
S5_GROUPS = 64
S5_STATE = 64
S5_GROUP = 16
S5_BLOCKS = 8
M2_INNER = 2048
M2_HEADS = 32
M2_GROUPS = 8
M2_STATE = 128
M2_CONV_CH = 4096
M2_IN = 6176
M2_IN_PAD = 6272
ADAM_LR, ADAM_B1, ADAM_B2, ADAM_EPS, ADAM_WD, ADAM_STEP = 0.001, 0.9, 0.999, 1e-08, 0.01, 10

VMEM_LIMIT_BYTES = 56 * 1024 * 1024
SUBLANES = 8
LANES = 128


def _params(*sem):
    return pltpu.CompilerParams(dimension_semantics=tuple(sem) if sem else None, vmem_limit_bytes=VMEM_LIMIT_BYTES)


NN, NT, TN = ((1,), (0,)), ((1,), (1,)), ((0,), (0,))
_DOT_TRANSPOSES = {NN: ((NT, "gb"), (TN, "ag")), NT: ((NN, "gb"), (TN, "ga")), TN: ((NT, "bg"), (NN, "ag"))}


def _dg(a, b, dims):
    if a.ndim == 3:
        dn = (((dims[0][0] + 1,), (dims[1][0] + 1,)), ((0,), (0,)))
    else:
        dn = (dims, ((), ()))
    return lax.dot_general(a, b, dn, preferred_element_type=F32)


def _mxu(a, b, dims):
    return _dg(a.astype(BF16), b.astype(BF16), dims)


@functools.partial(jax.custom_vjp, nondiff_argnums=(2,))
def _dot(a, b, dims=NN):
    return _mxu(a, b, dims)


def _dot_fwd(a, b, dims):
    return _mxu(a, b, dims), (a, b)


def _dot_bwd(dims, res, g):
    ops = dict(a=res[0], b=res[1], g=g)
    (da_dims, da_ops), (db_dims, db_ops) = _DOT_TRANSPOSES[dims]
    return (_mxu(ops[da_ops[0]], ops[da_ops[1]], da_dims).astype(res[0].dtype),
            _mxu(ops[db_ops[0]], ops[db_ops[1]], db_dims).astype(res[1].dtype))


_dot.defvjp(_dot_fwd, _dot_bwd)


def _nt(a, b):
    return _dot(a, b, NT)


def _tn(a, b):
    return _dot(a, b, TN)


def _split3(x):
    x1 = x.astype(BF16)
    r = x - x1.astype(F32)
    x2 = r.astype(BF16)
    return x1, x2, (r - x2.astype(F32)).astype(BF16)


def _sel_mxu(x, sel, dims, x_first):
    f = (lambda p: _dg(p, sel.astype(BF16), dims)) if x_first else (lambda p: _dg(sel.astype(BF16), p, dims))
    x1, x2, x3 = _split3(x)
    return f(x1) + (f(x2) + f(x3))


@jax.custom_vjp
def _pick(x, sel):
    return _sel_mxu(x, sel, NN, True)


def _pick_fwd(x, sel):
    return _sel_mxu(x, sel, NN, True), sel


def _pick_bwd(sel, g):
    return _sel_mxu(g, sel, NT, True), jnp.zeros_like(sel)


_pick.defvjp(_pick_fwd, _pick_bwd)


@jax.custom_vjp
def _accum(sel, x):
    return _sel_mxu(x, sel, NN, False)


def _accum_fwd(sel, x):
    return _sel_mxu(x, sel, NN, False), sel


def _accum_bwd(sel, g):
    return jnp.zeros_like(sel), _sel_mxu(g, sel, TN, False)


_accum.defvjp(_accum_fwd, _accum_bwd)


def _dot3(a, b, dims=NN):
    ah, bh = a.astype(BF16), b.astype(BF16)
    al, bl = (a - ah.astype(F32)).astype(BF16), (b - bh.astype(F32)).astype(BF16)
    return _dg(ah, bh, dims) + (_dg(ah, bl, dims) + _dg(al, bh, dims))


def _neumann(x, r, dims):
    r = r + _dot3(x, r, dims)
    for _ in range(5):
        x = _dot3(x, x)
        r = r + _dot3(x, r, dims)
    return r


@jax.custom_vjp
def _unit_lower_solve(a, rhs):
    return _neumann(-a, rhs, NN)


def _unit_lower_solve_fwd(a, rhs):
    sol = _neumann(-a, rhs, NN)
    return sol, (a, sol)


def _unit_lower_solve_bwd(res, ct):
    a, sol = res
    d_rhs = _neumann(-a, ct, TN)
    return -_dot3(d_rhs, sol, NT), d_rhs


_unit_lower_solve.defvjp(_unit_lower_solve_fwd, _unit_lower_solve_bwd)


@jax.custom_vjp
def _unit_lower_solved(a, rhs, sol):
    return sol


def _unit_lower_solved_fwd(a, rhs, sol):
    return sol, (a, sol)


def _unit_lower_solved_bwd(res, ct):
    da, d_rhs = _unit_lower_solve_bwd(res, ct)
    return da, d_rhs, jnp.zeros_like(ct)


_unit_lower_solved.defvjp(_unit_lower_solved_fwd, _unit_lower_solved_bwd)


def _sigmoid(x):
    return 1.0 / (1.0 + jnp.exp(-x))


def _softplus(x):
    return jnp.maximum(x, 0.0) + jnp.log(1.0 + jnp.exp(-jnp.abs(x)))


def _iota2(shape, axis):
    return lax.broadcasted_iota(jnp.int32, shape, axis)


def _tile(n, cands):
    for c in cands:
        if n % c == 0:
            return c
    return n


MM_TILE_BYTES = 9 * 1024 * 1024


def _mm(name, a, b, mode, out_dtypes, epi=None, extras=(), tn=None):
    if mode == "nn":
        (M, K), N = a.shape, b.shape[1]
    elif mode == "nt":
        (M, K), N = a.shape, b.shape[0]
    else:
        (K, M), N = a.shape, b.shape[1]
    tn = tn or _tile(N, (512, 384, 896, 256, 128))
    out_bytes = tn * (sum(jnp.dtype(d).itemsize for d in out_dtypes) + sum(e.dtype.itemsize for e in extras))
    fits = lambda t: t * K * a.dtype.itemsize <= MM_TILE_BYTES and t * out_bytes <= MM_TILE_BYTES
    tm = next(t for t in (2048, 1024, 512, 256, 128) if M % t == 0 and (fits(t) or t == 128))
    if mode == "nn":
        a_spec, b_spec = pl.BlockSpec((tm, K), lambda i, j: (i, 0)), pl.BlockSpec((K, tn), lambda i, j: (0, j))
        dims = NN
    elif mode == "nt":
        a_spec, b_spec = pl.BlockSpec((tm, K), lambda i, j: (i, 0)), pl.BlockSpec((tn, K), lambda i, j: (j, 0))
        dims = NT
    else:
        a_spec, b_spec = pl.BlockSpec((K, tm), lambda i, j: (0, i)), pl.BlockSpec((K, tn), lambda i, j: (0, j))
        dims = TN
    n_ex = len(extras)

    def body(a_ref, b_ref, *rest):
        acc = _mxu(a_ref[...], b_ref[...], dims)
        res = epi(acc, *[e[...] for e in rest[:n_ex]]) if epi is not None else (acc,)
        for o_ref, r in zip(rest[n_ex:], res):
            o_ref[...] = r.astype(o_ref.dtype)

    tile = pl.BlockSpec((tm, tn), lambda i, j: (i, j))
    out = pl.pallas_call(
        body, name=name, grid=(M // tm, N // tn),
        in_specs=[a_spec, b_spec] + [tile] * n_ex,
        out_specs=[tile] * len(out_dtypes),
        out_shape=[jax.ShapeDtypeStruct((M, N), d) for d in out_dtypes],
        compiler_params=_params("parallel", "parallel"),
    )(a, b, *extras)
    return out if len(out_dtypes) > 1 else out[0]


def _dw_into(name, a, b, buf, row0, cut_axis):
    (K, M), N = a.shape, b.shape[1]
    ms, ns = (M, N // N_CHIPS) if cut_axis == 1 else (M // N_CHIPS, N)
    assert buf.shape[2] == ns, (buf.shape, ns)
    tm = next(t for t in (1024, 512, 256, 128) if ms % t == 0 and row0 % t == 0)
    tn = _tile(ns, (512, 256, 128))
    rb, cb = ms // tm, ns // tn
    if cut_axis == 1:
        where = lambda i, j: (j // cb, row0 // tm + i, j % cb)
    else:
        where = lambda i, j: (i // rb, row0 // tm + i % rb, j)

    def body(a_ref, b_ref, buf_ref, o_ref):
        o_ref[...] = _mxu(a_ref[...], b_ref[...], TN).astype(o_ref.dtype)

    return pl.pallas_call(
        body, name=name, grid=(M // tm, N // tn),
        in_specs=[pl.BlockSpec((K, tm), lambda i, j: (0, i)), pl.BlockSpec((K, tn), lambda i, j: (0, j)),
                  pl.BlockSpec(memory_space=pl.ANY)],
        out_specs=pl.BlockSpec((None, tm, tn), where), out_shape=jax.ShapeDtypeStruct(buf.shape, buf.dtype),
        input_output_aliases={2: 0}, compiler_params=_params("parallel", "parallel"),
    )(a, b, buf)


def _rms_fwd(name, h, g):
    L, D = h.shape
    tr = _tile(L, (256, 128))

    def body(h_ref, g_ref, o_ref):
        x = h_ref[...]
        r = lax.rsqrt(jnp.mean(x * x, axis=-1, keepdims=True) + RMS_EPS)
        o_ref[...] = (x * r * g_ref[...]).astype(o_ref.dtype)

    return pl.pallas_call(
        body, name=name, grid=(L // tr,),
        in_specs=[pl.BlockSpec((tr, D), lambda i: (i, 0)), pl.BlockSpec((1, D), lambda i: (0, 0))],
        out_specs=pl.BlockSpec((tr, D), lambda i: (i, 0)),
        out_shape=jax.ShapeDtypeStruct((L, D), BF16),
        compiler_params=_params("parallel"),
    )(h, g.reshape(1, D))


def _rms_bwd(name, h, g, dhn, dres):
    L, D = h.shape
    tr = _tile(L, (256, 128))

    def body(h_ref, g_ref, dhn_ref, dres_ref, dh_ref, dg_ref):
        x = h_ref[...]
        r = lax.rsqrt(jnp.mean(x * x, axis=-1, keepdims=True) + RMS_EPS)
        xh = x * r
        dy = dhn_ref[...]
        dxh = dy * g_ref[...]
        dh_ref[...] = dres_ref[...] + r * (dxh - xh * jnp.mean(dxh * xh, axis=-1, keepdims=True))

        @pl.when(pl.program_id(0) == 0)
        def _():
            dg_ref[...] = jnp.zeros_like(dg_ref)

        dg_ref[...] += jnp.sum(dy * xh, axis=0, keepdims=True)

    row = pl.BlockSpec((tr, D), lambda i: (i, 0))
    vec = pl.BlockSpec((1, D), lambda i: (0, 0))
    return pl.pallas_call(
        body, name=name, grid=(L // tr,),
        in_specs=[row, vec, row, row], out_specs=[row, vec],
        out_shape=[jax.ShapeDtypeStruct((L, D), F32), jax.ShapeDtypeStruct((1, D), F32)],
        compiler_params=_params("arbitrary"),
    )(h, g.reshape(1, D), dhn, dres)


def _loss_head(h, g, target):
    L, D = h.shape
    tr = _tile(L, (256, 128))

    def body(h_ref, g_ref, t_ref, loss_ref, dh_ref, dg_ref):
        x = h_ref[...]
        r = lax.rsqrt(jnp.mean(x * x, axis=-1, keepdims=True) + RMS_EPS)
        xh = x * r
        err = xh * g_ref[...] - t_ref[...]
        dy = err * (1.0 / D)
        dxh = dy * g_ref[...]
        dh_ref[...] = r * (dxh - xh * jnp.mean(dxh * xh, axis=-1, keepdims=True))

        @pl.when(pl.program_id(0) == 0)
        def _():
            dg_ref[...] = jnp.zeros_like(dg_ref)
            loss_ref[...] = jnp.zeros_like(loss_ref)

        dg_ref[...] += jnp.sum(dy * xh, axis=0, keepdims=True)
        loss_ref[...] += (0.5 / D) * jnp.sum(jnp.sum(err * err, axis=-1, keepdims=True), axis=0, keepdims=True)

    row = pl.BlockSpec((tr, D), lambda i: (i, 0))
    vec = pl.BlockSpec((1, D), lambda i: (0, 0))
    return pl.pallas_call(
        body, name="loss_head", grid=(L // tr,),
        in_specs=[row, vec, row], out_specs=[pl.BlockSpec((1, 1), lambda i: (0, 0)), row, vec],
        out_shape=[jax.ShapeDtypeStruct((1, 1), F32), jax.ShapeDtypeStruct((L, D), F32), jax.ShapeDtypeStruct((1, D), F32)],
        compiler_params=_params("arbitrary"),
    )(h, g.reshape(1, D), target)


def _glu_fwd(h, ag):
    L, D = h.shape
    tr = _tile(L, (256, 128))

    def body(h_ref, v_ref, g_ref, o_ref):
        o_ref[...] = h_ref[...] + v_ref[...] * _sigmoid(g_ref[...])

    return pl.pallas_call(
        body, name="s5_glu_fwd", grid=(L // tr,),
        in_specs=[pl.BlockSpec((tr, D), lambda i: (i, 0)), pl.BlockSpec((tr, D), lambda i: (i, 0)),
                  pl.BlockSpec((tr, D), lambda i: (i, 1))],
        out_specs=pl.BlockSpec((tr, D), lambda i: (i, 0)),
        out_shape=jax.ShapeDtypeStruct((L, D), F32),
        compiler_params=_params("parallel"),
    )(h, ag, ag)


def _glu_bwd(dh, ag):
    L, D = dh.shape
    tr = _tile(L, (256, 128))

    def body(dh_ref, v_ref, g_ref, dv_ref, dg_ref):
        s = _sigmoid(g_ref[...])
        d = dh_ref[...]
        dv_ref[...] = d * s
        dg_ref[...] = d * v_ref[...] * s * (1.0 - s)

    dv, dg = pl.pallas_call(
        body, name="s5_glu_bwd", grid=(L // tr,),
        in_specs=[pl.BlockSpec((tr, D), lambda i: (i, 0)), pl.BlockSpec((tr, D), lambda i: (i, 0)),
                  pl.BlockSpec((tr, D), lambda i: (i, 1))],
        out_specs=[pl.BlockSpec((tr, D), lambda i: (i, 0))] * 2,
        out_shape=[jax.ShapeDtypeStruct((L, D), F32)] * 2,
        compiler_params=_params("parallel"),
    )(dh, ag, ag)
    return jnp.concatenate([dv, dg], axis=1).astype(BF16)


CONV_ROWS = 128
CONV_COLS = 512


def _shift_rows(cat, s):
    if s == 0:
        return cat[SUBLANES:, :]
    return pltpu.roll(cat, s, axis=0)[SUBLANES:, :]


def _conv_fwd(name, p, col0, w, b):
    L = p.shape[0]
    C = w.shape[1]
    tc = _tile(C, (CONV_COLS, 256))
    cb0 = col0 // tc
    nr = L // CONV_ROWS

    def body(x_ref, w_ref, b_ref, o_ref):
        def step(r, carry):
            r0 = pl.multiple_of(r * CONV_ROWS, CONV_ROWS)
            cur = x_ref[pl.ds(r0, CONV_ROWS), :]
            p0 = pl.multiple_of(jnp.maximum(r0 - SUBLANES, 0), SUBLANES)
            prev = jnp.where(r > 0, x_ref[pl.ds(p0, SUBLANES), :], 0.0)
            cat = jnp.concatenate([prev, cur], axis=0)
            acc = b_ref[...] + w_ref[3:4, :] * cur
            for k in range(CONV_W - 1):
                acc = acc + w_ref[k:k + 1, :] * _shift_rows(cat, CONV_W - 1 - k)
            o_ref[pl.ds(r0, CONV_ROWS), :] = acc * _sigmoid(acc)
            return carry

        lax.fori_loop(0, nr, step, 0)

    return pl.pallas_call(
        body, name=name, grid=(C // tc,),
        in_specs=[pl.BlockSpec((L, tc), lambda j: (0, cb0 + j)), pl.BlockSpec((CONV_W, tc), lambda j: (0, j)),
                  pl.BlockSpec((1, tc), lambda j: (0, j))],
        out_specs=pl.BlockSpec((L, tc), lambda j: (0, j)),
        out_shape=jax.ShapeDtypeStruct((L, C), F32),
        compiler_params=_params("parallel"),
    )(p, w, b)


def _conv_bwd(name, p, col0, w, b, dout):
    L = p.shape[0]
    C = w.shape[1]
    tc = _tile(C, (CONV_COLS, 256))
    cb0 = col0 // tc
    nr = L // CONV_ROWS

    def body(x_ref, w_ref, b_ref, do_ref, dx_ref, dw_ref, db_ref, dpre_ref):
        def step1(r, carry):
            dw0, dw1, dw2, dw3, dbb = carry
            r0 = pl.multiple_of(r * CONV_ROWS, CONV_ROWS)
            cur = x_ref[pl.ds(r0, CONV_ROWS), :]
            p0 = pl.multiple_of(jnp.maximum(r0 - SUBLANES, 0), SUBLANES)
            prev = jnp.where(r > 0, x_ref[pl.ds(p0, SUBLANES), :], 0.0)
            cat = jnp.concatenate([prev, cur], axis=0)
            sh = [_shift_rows(cat, CONV_W - 1 - k) for k in range(CONV_W - 1)] + [cur]
            acc = b_ref[...] + w_ref[3:4, :] * cur
            for k in range(CONV_W - 1):
                acc = acc + w_ref[k:k + 1, :] * sh[k]
            sg = _sigmoid(acc)
            dpre = do_ref[pl.ds(r0, CONV_ROWS), :] * (sg + acc * sg * (1.0 - sg))
            dpre_ref[pl.ds(r0, CONV_ROWS), :] = dpre
            dws = [d + jnp.sum(dpre * s, axis=0, keepdims=True) for d, s in zip((dw0, dw1, dw2, dw3), sh)]
            return (*dws, dbb + jnp.sum(dpre, axis=0, keepdims=True))

        z = jnp.zeros((1, tc), F32)
        dw0, dw1, dw2, dw3, dbb = lax.fori_loop(0, nr, step1, (z, z, z, z, z))
        dw_ref[...] = jnp.concatenate([dw0, dw1, dw2, dw3, z, z, z, z], axis=0)
        db_ref[...] = dbb

        def step2(r, carry):
            r0 = pl.multiple_of(r * CONV_ROWS, CONV_ROWS)
            cur = dpre_ref[pl.ds(r0, CONV_ROWS), :]
            n0 = pl.multiple_of(jnp.minimum(r0 + CONV_ROWS, L - SUBLANES), SUBLANES)
            nxt = jnp.where(r < nr - 1, dpre_ref[pl.ds(n0, SUBLANES), :], 0.0)
            cat = jnp.concatenate([cur, nxt], axis=0)
            acc = w_ref[3:4, :] * cur
            for k in range(CONV_W - 1):
                s = CONV_W - 1 - k
                acc = acc + w_ref[k:k + 1, :] * pltpu.roll(cat, CONV_ROWS + SUBLANES - s, axis=0)[:CONV_ROWS, :]
            dx_ref[pl.ds(r0, CONV_ROWS), :] = acc
            return carry

        lax.fori_loop(0, nr, step2, 0)

    dx, dw, db = pl.pallas_call(
        body, name=name, grid=(C // tc,),
        in_specs=[pl.BlockSpec((L, tc), lambda j: (0, cb0 + j)), pl.BlockSpec((CONV_W, tc), lambda j: (0, j)),
                  pl.BlockSpec((1, tc), lambda j: (0, j)), pl.BlockSpec((L, tc), lambda j: (0, j))],
        out_specs=[pl.BlockSpec((L, tc), lambda j: (0, j)), pl.BlockSpec((SUBLANES, tc), lambda j: (0, j)),
                   pl.BlockSpec((1, tc), lambda j: (0, j))],
        out_shape=[jax.ShapeDtypeStruct((L, C), F32), jax.ShapeDtypeStruct((SUBLANES, C), F32),
                   jax.ShapeDtypeStruct((1, C), F32)],
        scratch_shapes=[pltpu.VMEM((L, tc), F32)],
        compiler_params=_params("parallel"),
    )(p, w, b, dout)
    return dx, dw[:CONV_W], db


def _chunk_consts():
    r, c = _iota2((CHUNK, CHUNK), 0), _iota2((CHUNK, CHUNK), 1)
    causal = r >= c
    return causal, r > c, (r == c).astype(F32), causal.astype(F32), jnp.ones((CHUNK, CHUNK), F32)


def _by_lanes(t):
    return jnp.concatenate([t[i] for i in range(t.shape[0])], axis=1)


def _by_batch(t, w):
    return jnp.concatenate([t[None, :, i * w:(i + 1) * w] for i in range(t.shape[1] // w)], axis=0)


def _diag_lanes():
    return (_iota2((CHUNK, LANES), 0) == _iota2((CHUNK, LANES), 1)).astype(F32)


def _gdn_chunk(q, k, v, ab, gate, S, alog, dtb, og, ea, eb, sol=None):
    causal, strict, _, tril, ones = _chunk_consts()
    logits = _by_batch(_pick(ab, jnp.concatenate([_by_lanes(ea), _by_lanes(eb)], axis=1)), LANES)
    H = q.shape[0]
    g = -jnp.exp(alog) * _softplus(logits[:H] + dtb)
    beta = _sigmoid(logits[H:])
    qn = q * lax.rsqrt(jnp.sum(q * q, axis=-1, keepdims=True) + 1e-6) * (GDN_DK ** -0.5)
    kn = k * lax.rsqrt(jnp.sum(k * k, axis=-1, keepdims=True) + 1e-6)
    g_l = _by_lanes(g)
    gc = _by_batch(_accum(tril, g_l), LANES)
    glast = _by_batch(_accum(ones, g_l), LANES)
    gcol = gc[:, :, :CHUNK]
    grow = _by_batch(_accum(ones, _by_lanes(gc * _diag_lanes())), LANES)[:, :, :CHUNK]
    decay = jnp.exp(jnp.where(causal, gcol - grow, -jnp.inf))
    a = jnp.where(strict, beta[:, :, :CHUNK] * _nt(kn, kn) * decay, 0.0)
    eg = jnp.exp(gc)
    rhs = jnp.concatenate([v * beta, kn * (beta * eg)], axis=2)
    sol = _unit_lower_solve(a, rhs) if sol is None else _unit_lower_solved(a, rhs, sol)
    u, w = sol[:, :, :GDN_DK], sol[:, :, GDN_DK:]
    qk = _nt(qn, kn) * decay
    v_new = u - _dot(w, S)
    o = _dot(qn * eg, S) + _dot(qk, v_new)
    cd = jnp.exp(glast)
    s_new = jnp.concatenate([cd, cd], axis=1) * S + _tn(kn * jnp.exp(glast - gc), v_new)
    on = o * lax.rsqrt(jnp.mean(o * o, axis=-1, keepdims=True) + RMS_EPS) * og
    return on * (gate * _sigmoid(gate)), s_new, sol


GDN_HB = 8


def _gdn_specs(nc, rev):
    cm = (lambda c: nc - 1 - c) if rev else (lambda c: c)
    blk = lambda off: pl.BlockSpec((CHUNK, GDN_HB * GDN_DK), lambda c, h: (cm(c), off // GDN_HB + h))
    ab = pl.BlockSpec((CHUNK, LANES), lambda c, h: (cm(c), (GDN_IN_PAD - LANES) // LANES))
    hv = pl.BlockSpec((GDN_HB, 1, LANES), lambda c, h: (h, 0, 0))
    og = pl.BlockSpec((1, LANES), lambda c, h: (0, 0))
    em = pl.BlockSpec((GDN_HB, LANES, LANES), lambda c, h: (h, 0, 0))
    st = pl.BlockSpec((None, GDN_HB, GDN_DK, GDN_DK), lambda c, h: (cm(c), h, 0, 0))
    sl = pl.BlockSpec((None, GDN_HB, CHUNK, 2 * GDN_DK), lambda c, h: (cm(c), h, 0, 0))
    return blk, ab, hv, og, em, st, sl


def _gdn_fwd(qc, kc, vc, p, alog_e, dtb_e, og, ea, eb):
    L = qc.shape[0]
    nc = L // CHUNK
    blk, ab, hv, ogs, em, st, sl = _gdn_specs(nc, False)

    def body(q_ref, k_ref, v_ref, gate_ref, ab_ref, al_ref, dt_ref, og_ref, ea_ref, eb_ref, y_ref, sp_ref, sol_ref, s_scr):
        c, h = pl.program_id(0), pl.program_id(1)
        lanes = [slice(i * GDN_DK, (i + 1) * GDN_DK) for i in range(GDN_HB)]
        heads = pl.ds(h * GDN_HB, GDN_HB)
        stack = lambda ref: jnp.concatenate([ref[:, ls][None] for ls in lanes], axis=0)

        @pl.when(c == 0)
        def _():
            s_scr[heads] = jnp.zeros((GDN_HB, GDN_DK, GDN_DK), F32)

        S = s_scr[heads]
        sp_ref[...] = S
        y, s_new, sol = _gdn_chunk(stack(q_ref), stack(k_ref), stack(v_ref), ab_ref[...], stack(gate_ref), S,
                                   al_ref[...], dt_ref[...], og_ref[...], ea_ref[...], eb_ref[...])
        for i, ls in enumerate(lanes):
            y_ref[:, ls] = y[i]
        s_scr[heads] = s_new
        sol_ref[...] = sol

    return pl.pallas_call(
        body, name="gdn_fwd", grid=(nc, GDN_HEADS // GDN_HB),
        in_specs=[blk(0), blk(0), blk(0), blk(3 * GDN_HEADS), ab, hv, hv, ogs, em, em],
        out_specs=[blk(0), st, sl],
        out_shape=[jax.ShapeDtypeStruct((L, D_MODEL), F32), jax.ShapeDtypeStruct((nc, GDN_HEADS, GDN_DK, GDN_DK), F32),
                   jax.ShapeDtypeStruct((nc, GDN_HEADS, CHUNK, 2 * GDN_DK), F32)],
        scratch_shapes=[pltpu.VMEM((GDN_HEADS, GDN_DK, GDN_DK), F32)],
        compiler_params=_params("arbitrary", "arbitrary"),
    )(qc, kc, vc, p, p, alog_e, dtb_e, og, ea, eb)


def _gdn_bwd(qc, kc, vc, p, alog_e, dtb_e, og, ea, eb, sprev, sol, dy):
    L = qc.shape[0]
    nc = L // CHUNK
    blk, ab, hv, ogs, em, st, sl = _gdn_specs(nc, True)

    def body(q_ref, k_ref, v_ref, gate_ref, ab_ref, al_ref, dt_ref, og_ref, ea_ref, eb_ref, sp_ref, sol_ref, dy_ref,
             dq_ref, dk_ref, dv_ref, dgate_ref, dab_ref, dpar_ref, ds_scr):
        c, h = pl.program_id(0), pl.program_id(1)
        lanes = [slice(i * GDN_DK, (i + 1) * GDN_DK) for i in range(GDN_HB)]
        heads = pl.ds(h * GDN_HB, GDN_HB)
        stack = lambda ref: jnp.concatenate([ref[:, ls][None] for ls in lanes], axis=0)

        @pl.when(c == 0)
        def _():
            ds_scr[heads] = jnp.zeros((GDN_HB, GDN_DK, GDN_DK), F32)
            dpar_ref[heads] = jnp.zeros((GDN_HB, SUBLANES, LANES), F32)

        @pl.when(h == 0)
        def _():
            dab_ref[...] = jnp.zeros_like(dab_ref)

        ea_m, eb_m, sol_m = ea_ref[...], eb_ref[...], sol_ref[...]
        f = lambda q, k, v, a_b, gate, S, al, dt, o_g: _gdn_chunk(q, k, v, a_b, gate, S, al, dt, o_g, ea_m, eb_m, sol_m)[:2]
        _, vjp = jax.vjp(f, stack(q_ref), stack(k_ref), stack(v_ref), ab_ref[...], stack(gate_ref), sp_ref[...],
                         al_ref[...], dt_ref[...], og_ref[...])
        dq, dk, dv, dab, dgate, ds, dal, ddt, dog = vjp((stack(dy_ref), ds_scr[heads]))
        for i, ls in enumerate(lanes):
            dq_ref[:, ls] = dq[i]
            dk_ref[:, ls] = dk[i]
            dv_ref[:, ls] = dv[i]
            dgate_ref[:, ls] = dgate[i]
        ds_scr[heads] = ds
        dab_ref[...] += dab
        first = _iota2((GDN_HB, 1, LANES), 0) == 0
        dpar_ref[heads] += jnp.concatenate([dal, ddt, jnp.where(first, dog[None], 0.0),
                                            jnp.zeros((GDN_HB, SUBLANES - 3, LANES), F32)], axis=1)

    return pl.pallas_call(
        body, name="gdn_bwd", grid=(nc, GDN_HEADS // GDN_HB),
        in_specs=[blk(0), blk(0), blk(0), blk(3 * GDN_HEADS), ab, hv, hv, ogs, em, em, st, sl, blk(0)],
        out_specs=[blk(0), blk(0), blk(0), blk(0), pl.BlockSpec((CHUNK, LANES), lambda c, h: (nc - 1 - c, 0)),
                   pl.BlockSpec((GDN_HEADS, SUBLANES, LANES), lambda c, h: (0, 0, 0))],
        out_shape=[jax.ShapeDtypeStruct((L, D_MODEL), F32)] * 4
        + [jax.ShapeDtypeStruct((L, LANES), F32), jax.ShapeDtypeStruct((GDN_HEADS, SUBLANES, LANES), F32)],
        scratch_shapes=[pltpu.VMEM((GDN_HEADS, GDN_DK, GDN_DK), F32)],
        compiler_params=_params("arbitrary", "arbitrary"),
    )(qc, kc, vc, p, p, alog_e, dtb_e, og, ea, eb, sprev, sol, dy)


def _gdn_selectors():
    rows = np.arange(LANES)[None, :, None]
    heads = np.arange(GDN_HEADS)[:, None, None]
    ea = np.broadcast_to(rows == heads, (GDN_HEADS, LANES, LANES)).astype(np.float32)
    eb = np.broadcast_to(rows == heads + GDN_HEADS, (GDN_HEADS, LANES, LANES)).astype(np.float32)
    return jnp.asarray(ea), jnp.asarray(eb)


M2_GW = M2_INNER // M2_GROUPS
M2_HPG = M2_HEADS // M2_GROUPS
M2_HD = M2_INNER // M2_HEADS


def _m2_chunk(x, bm, cm, z, dtr, st, dtb, alog, dsk, ng, e, ecol):
    G = x.shape[0]
    causal, _, _, tril, ones = _chunk_consts()
    dt_n = _softplus(dtr + dtb)
    da_n = dt_n * (-jnp.exp(alog))
    cum_n = _accum(tril, da_n)
    tot_n = _accum(ones, da_n)
    wide = _pick(jnp.concatenate([dt_n, cum_n, tot_n], axis=0), e)
    dt_w, cum_w, tot_w = (_by_batch(wide[i * CHUNK:(i + 1) * CHUNK], M2_GW) for i in range(3))
    xdt = x * dt_w
    cb = _nt(cm, bm)
    heads = lambda t: jnp.concatenate([t[i:i + 1] for i in range(G) for _ in range(M2_HPG)], axis=0)
    colb = _by_batch(_pick(cum_n, ecol), LANES)
    rowb = _by_batch(_accum(ones, _by_lanes(colb * _diag_lanes())), LANES)
    lmat = jnp.exp(jnp.where(causal, colb[:, :, :CHUNK] - rowb[:, :, :CHUNK], -jnp.inf))
    yr = _dot(heads(cb) * lmat, heads(xdt))
    head = _iota2((CHUNK, M2_GW), 1) // M2_HD
    ydiag = jnp.concatenate([sum(jnp.where(head == r, yr[i * M2_HPG + r], 0.0) for r in range(M2_HPG))[None] for i in range(G)], axis=0)
    st_new = _tn(bm, xdt * jnp.exp(tot_w - cum_w))
    cd = jnp.exp(tot_w)
    s_new = jnp.concatenate([cd, cd], axis=1) * st + st_new
    y = ydiag + _dot(cm, st) * jnp.exp(cum_w) + dsk * x
    y = y * (z * _sigmoid(z))
    yn = y * lax.rsqrt(jnp.mean(y * y, axis=-1, keepdims=True) + RMS_EPS) * ng
    return yn, s_new


M2_GB = 8


def _m2_specs(nc, rev):
    cm = (lambda c: nc - 1 - c) if rev else (lambda c: c)
    wide = lambda off: pl.BlockSpec((CHUNK, M2_GB * M2_GW), lambda c, g: (cm(c), off // M2_GB + g))
    nar = lambda off: pl.BlockSpec((CHUNK, M2_GB * LANES), lambda c, g: (cm(c), off // M2_GB + g))
    dts = pl.BlockSpec((CHUNK, LANES), lambda c, g: (cm(c), (M2_IN_PAD - LANES) // LANES))
    v128 = pl.BlockSpec((1, LANES), lambda c, g: (0, 0))
    v256 = pl.BlockSpec((1, M2_GB * M2_GW), lambda c, g: (0, g))
    es = pl.BlockSpec((LANES, M2_GB * M2_GW), lambda c, g: (0, g))
    ecs = pl.BlockSpec((LANES, M2_GB * M2_HPG * LANES), lambda c, g: (0, g))
    st = pl.BlockSpec((None, M2_GB, M2_STATE, M2_GW), lambda c, g: (cm(c), g, 0, 0))
    return wide, nar, dts, v128, v256, es, ecs, st


def _m2_fwd(xbc, p, dtb, alog, dsk, ng, e, ecol):
    L = xbc.shape[0]
    nc = L // CHUNK
    wide, nar, dts, v128, v256, es, ecs, st = _m2_specs(nc, False)

    def body(x_ref, b_ref, c_ref, z_ref, dt_ref, dtb_ref, al_ref, dsk_ref, ng_ref, e_ref, ec_ref, y_ref, sp_ref, s_scr):
        c, g = pl.program_id(0), pl.program_id(1)
        wide_l = [slice(i * M2_GW, (i + 1) * M2_GW) for i in range(M2_GB)]
        nar_l = [slice(i * LANES, (i + 1) * LANES) for i in range(M2_GB)]
        groups = pl.ds(g * M2_GB, M2_GB)
        wide_s = lambda ref: jnp.concatenate([ref[:, ls][None] for ls in wide_l], axis=0)
        nar_s = lambda ref: jnp.concatenate([ref[:, ls][None] for ls in nar_l], axis=0)

        @pl.when(c == 0)
        def _():
            s_scr[groups] = jnp.zeros((M2_GB, M2_STATE, M2_GW), F32)

        S = s_scr[groups]
        sp_ref[...] = S
        y, s_new = _m2_chunk(wide_s(x_ref), nar_s(b_ref), nar_s(c_ref), wide_s(z_ref), dt_ref[...], S, dtb_ref[...], al_ref[...],
                             wide_s(dsk_ref), wide_s(ng_ref), e_ref[...], ec_ref[...])
        for i, ls in enumerate(wide_l):
            y_ref[:, ls] = y[i]
        s_scr[groups] = s_new

    return pl.pallas_call(
        body, name="m2_fwd", grid=(nc, M2_GROUPS // M2_GB),
        in_specs=[wide(0), nar(2 * M2_GROUPS), nar(3 * M2_GROUPS), wide(0), dts, v128, v128, v256, v256, es, ecs],
        out_specs=[wide(0), st],
        out_shape=[jax.ShapeDtypeStruct((L, M2_INNER), F32), jax.ShapeDtypeStruct((nc, M2_GROUPS, M2_STATE, M2_GW), F32)],
        scratch_shapes=[pltpu.VMEM((M2_GROUPS, M2_STATE, M2_GW), F32)],
        compiler_params=_params("arbitrary", "arbitrary"),
    )(xbc, xbc, xbc, p, p, dtb, alog, dsk, ng, e, ecol)


def _m2_bwd(xbc, p, dtb, alog, dsk, ng, e, ecol, sprev, dy):
    L = xbc.shape[0]
    nc = L // CHUNK
    wide, nar, dts, v128, v256, es, ecs, st = _m2_specs(nc, True)

    def body(x_ref, b_ref, c_ref, z_ref, dt_ref, dtb_ref, al_ref, dsk_ref, ng_ref, e_ref, ec_ref, sp_ref, dy_ref,
             dx_ref, db_ref, dc_ref, dz_ref, ddt_ref, dnar_ref, dwide_ref, ds_scr):
        c, g = pl.program_id(0), pl.program_id(1)
        wide_l = [slice(i * M2_GW, (i + 1) * M2_GW) for i in range(M2_GB)]
        nar_l = [slice(i * LANES, (i + 1) * LANES) for i in range(M2_GB)]
        groups = pl.ds(g * M2_GB, M2_GB)
        wide_s = lambda ref: jnp.concatenate([ref[:, ls][None] for ls in wide_l], axis=0)
        nar_s = lambda ref: jnp.concatenate([ref[:, ls][None] for ls in nar_l], axis=0)

        @pl.when(jnp.logical_and(c == 0, g == 0))
        def _():
            dnar_ref[...] = jnp.zeros_like(dnar_ref)

        @pl.when(c == 0)
        def _():
            ds_scr[groups] = jnp.zeros((M2_GB, M2_STATE, M2_GW), F32)
            dwide_ref[groups] = jnp.zeros((M2_GB, SUBLANES, M2_GW), F32)

        @pl.when(g == 0)
        def _():
            ddt_ref[...] = jnp.zeros_like(ddt_ref)

        e_m, ec_m = e_ref[...], ec_ref[...]
        f = lambda x, bm, cm, z, dtr, S, dtb, al, dsk, ng: _m2_chunk(x, bm, cm, z, dtr, S, dtb, al, dsk, ng, e_m, ec_m)
        _, vjp = jax.vjp(f, wide_s(x_ref), nar_s(b_ref), nar_s(c_ref), wide_s(z_ref), dt_ref[...], sp_ref[...], dtb_ref[...],
                         al_ref[...], wide_s(dsk_ref), wide_s(ng_ref))
        dx, db, dc, dz, ddt, ds, ddtb, dal, ddsk, dng = vjp((wide_s(dy_ref), ds_scr[groups]))
        for i in range(M2_GB):
            dx_ref[:, wide_l[i]] = dx[i]
            db_ref[:, nar_l[i]] = db[i]
            dc_ref[:, nar_l[i]] = dc[i]
            dz_ref[:, wide_l[i]] = dz[i]
        ds_scr[groups] = ds
        ddt_ref[...] += ddt
        dnar_ref[...] += jnp.concatenate([ddtb, dal, jnp.zeros((SUBLANES - 2, LANES), F32)], axis=0)
        dwide_ref[groups] += jnp.concatenate([ddsk, dng, jnp.zeros((M2_GB, SUBLANES - 2, M2_GW), F32)], axis=1)

    return pl.pallas_call(
        body, name="m2_bwd", grid=(nc, M2_GROUPS // M2_GB),
        in_specs=[wide(0), nar(2 * M2_GROUPS), nar(3 * M2_GROUPS), wide(0), dts, v128, v128, v256, v256, es, ecs, st, wide(0)],
        out_specs=[wide(0), nar(0), nar(0), wide(0), pl.BlockSpec((CHUNK, LANES), lambda c, g: (nc - 1 - c, 0)),
                   pl.BlockSpec((SUBLANES, LANES), lambda c, g: (0, 0)),
                   pl.BlockSpec((M2_GROUPS, SUBLANES, M2_GW), lambda c, g: (0, 0, 0))],
        out_shape=[jax.ShapeDtypeStruct((L, M2_INNER), F32), jax.ShapeDtypeStruct((L, M2_GROUPS * M2_STATE), F32),
                   jax.ShapeDtypeStruct((L, M2_GROUPS * M2_STATE), F32), jax.ShapeDtypeStruct((L, M2_INNER), F32),
                   jax.ShapeDtypeStruct((L, LANES), F32), jax.ShapeDtypeStruct((SUBLANES, LANES), F32),
                   jax.ShapeDtypeStruct((M2_GROUPS, SUBLANES, M2_GW), F32)],
        scratch_shapes=[pltpu.VMEM((M2_GROUPS, M2_STATE, M2_GW), F32)],
        compiler_params=_params("arbitrary", "arbitrary"),
    )(xbc, xbc, xbc, p, p, dtb, alog, dsk, ng, e, ecol, sprev, dy)


def _m2_selectors():
    e = np.zeros((LANES, M2_INNER), np.float32)
    ecol = np.zeros((LANES, M2_HEADS * LANES), np.float32)
    for h in range(M2_HEADS):
        e[h, M2_HD * h:M2_HD * (h + 1)] = 1.0
        ecol[h, LANES * h:LANES * (h + 1)] = 1.0
    return jnp.asarray(e), jnp.asarray(ecol)


S5_NS = S5_GROUPS * S5_STATE // S5_BLOCKS
S5_ROWS = 256
GELU_C = math.sqrt(2.0 / math.pi)


def _gelu(x):
    return 0.5 * x * (1.0 + jnp.tanh(GELU_C * (x + 0.044715 * x * x * x)))


def _gelu_grad(x):
    t = jnp.tanh(GELU_C * (x + 0.044715 * x * x * x))
    return 0.5 * (1.0 + t) + 0.5 * x * (1.0 - t * t) * GELU_C * (1.0 + 3.0 * 0.044715 * x * x)


def _s5_scan(re_ref, im_ref, pw_re, pw_im, nrows, reverse, states=None):
    n = re_ref.shape[1]
    row = _iota2((SUBLANES, n), 0)
    steps = []
    for d in (1, 2, 4):
        keep = (row < SUBLANES - d) if reverse else (row >= d)
        steps.append(((SUBLANES - d) if reverse else d, jnp.where(keep, pw_re[d - 1:d, :], 0.0), jnp.where(keep, pw_im[d - 1:d, :], 0.0)))
    if reverse:
        cw_re = jnp.concatenate([pw_re[SUBLANES - 1 - k:SUBLANES - k, :] for k in range(SUBLANES)], axis=0)
        cw_im = jnp.concatenate([pw_im[SUBLANES - 1 - k:SUBLANES - k, :] for k in range(SUBLANES)], axis=0)
    else:
        cw_re, cw_im = pw_re, pw_im
    edge = 0 if reverse else SUBLANES - 1
    ngroups = nrows // SUBLANES

    def step(i, carry):
        cr, ci, ar, ai = carry
        gi = (ngroups - 1 - i) if reverse else i
        r0 = pl.multiple_of(gi * SUBLANES, SUBLANES)
        xr, xi = re_ref[pl.ds(r0, SUBLANES), :], im_ref[pl.ds(r0, SUBLANES), :]
        for shift, pr, pi in steps:
            sr, si = pltpu.roll(xr, shift, axis=0), pltpu.roll(xi, shift, axis=0)
            xr, xi = xr + (pr * sr - pi * si), xi + (pr * si + pi * sr)
        xr, xi = xr + (cw_re * cr - cw_im * ci), xi + (cw_re * ci + cw_im * cr)
        re_ref[pl.ds(r0, SUBLANES), :] = xr
        im_ref[pl.ds(r0, SUBLANES), :] = xi
        if states is not None:
            p0 = pl.multiple_of(jnp.maximum(r0 - SUBLANES, 0), SUBLANES)
            live = jnp.where(gi > 0, 1.0, 0.0)
            prev = [jnp.where(row >= 1, pltpu.roll(ref[pl.ds(r0, SUBLANES), :], 1, axis=0),
                              live * pltpu.roll(ref[pl.ds(p0, SUBLANES), :], 1, axis=0)) for ref in states]
            ar, ai = ar + (prev[0] * xr + prev[1] * xi), ai + (prev[0] * xi - prev[1] * xr)
        return (jnp.sum(jnp.where(row == edge, xr, 0.0), axis=0, keepdims=True),
                jnp.sum(jnp.where(row == edge, xi, 0.0), axis=0, keepdims=True), ar, ai)

    z = jnp.zeros((1, n), F32)
    za = jnp.zeros((SUBLANES, n) if states is not None else (1, n), F32)
    _, _, ar, ai = lax.fori_loop(0, ngroups, step, (z, z, za, za))
    return jnp.sum(ar, axis=0, keepdims=True), jnp.sum(ai, axis=0, keepdims=True)


def _s5_project_in(u_ref, bm_ref, re_ref, im_ref, L):
    def step(i, carry):
        r0 = pl.multiple_of(i * S5_ROWS, S5_ROWS)
        bu = _dot(u_ref[pl.ds(r0, S5_ROWS), :], bm_ref[...])
        re_ref[pl.ds(r0, S5_ROWS), :] = bu[:, :S5_NS]
        im_ref[pl.ds(r0, S5_ROWS), :] = bu[:, S5_NS:]
        return carry

    lax.fori_loop(0, L // S5_ROWS, step, 0)


def _s5_specs(L):
    col = pl.BlockSpec((L, LANES), lambda j: (0, j))
    bm = pl.BlockSpec((None, LANES, 2 * S5_NS), lambda j: (j, 0, 0))
    cm = pl.BlockSpec((None, 2 * S5_NS, LANES), lambda j: (j, 0, 0))
    pw = pl.BlockSpec((None, SUBLANES, S5_NS), lambda j: (j, 0, 0))
    vec = pl.BlockSpec((1, LANES), lambda j: (0, j))
    return col, bm, cm, pw, vec


def _s5_fwd(u, bmat, cmat, pw_re, pw_im, dsk):
    L = u.shape[0]
    col, bm, cm, pw, vec = _s5_specs(L)

    def body(u_ref, bm_ref, cm_ref, pr_ref, pi_ref, d_ref, y_ref, re_scr, im_scr):
        _s5_project_in(u_ref, bm_ref, re_scr, im_scr, L)
        _s5_scan(re_scr, im_scr, pr_ref[...], pi_ref[...], L, False)

        def step(i, carry):
            r0 = pl.multiple_of(i * S5_ROWS, S5_ROWS)
            rows = pl.ds(r0, S5_ROWS)
            y = _dot(re_scr[rows, :], cm_ref[:S5_NS, :]) + _dot(im_scr[rows, :], cm_ref[S5_NS:, :]) + d_ref[...] * u_ref[rows, :]
            y_ref[rows, :] = _gelu(y)
            return carry

        lax.fori_loop(0, L // S5_ROWS, step, 0)

    return pl.pallas_call(
        body, name="s5_fwd", grid=(S5_BLOCKS,),
        in_specs=[col, bm, cm, pw, pw, vec], out_specs=col,
        out_shape=jax.ShapeDtypeStruct((L, D_MODEL), F32),
        scratch_shapes=[pltpu.VMEM((L, S5_NS), F32)] * 2,
        compiler_params=_params("parallel"),
    )(u, bmat, cmat, pw_re, pw_im, dsk)


def _s5_bwd(u, bmat, cmat, pw_re, pw_im, dsk, dyg):
    L = u.shape[0]
    col, bm, cm, pw, vec = _s5_specs(L)

    def body(u_ref, bm_ref, cm_ref, pr_ref, pi_ref, d_ref, dy_ref, du_ref, dbm_ref, dcm_ref, dlam_ref, dd_ref,
             re_scr, im_scr, gr_scr, gi_scr, dyp_scr):
        _s5_project_in(u_ref, bm_ref, re_scr, im_scr, L)
        _s5_scan(re_scr, im_scr, pr_ref[...], pi_ref[...], L, False)

        def step(i, carry):
            dcr, dci, dd = carry
            r0 = pl.multiple_of(i * S5_ROWS, S5_ROWS)
            rows = pl.ds(r0, S5_ROWS)
            sr, si, uu = re_scr[rows, :], im_scr[rows, :], u_ref[rows, :]
            y = _dot(sr, cm_ref[:S5_NS, :]) + _dot(si, cm_ref[S5_NS:, :]) + d_ref[...] * uu
            dyp = dy_ref[rows, :] * _gelu_grad(y)
            dyp_scr[rows, :] = dyp
            gr_scr[rows, :] = _nt(dyp, cm_ref[:S5_NS, :])
            gi_scr[rows, :] = _nt(dyp, cm_ref[S5_NS:, :])
            return dcr + _tn(sr, dyp), dci + _tn(si, dyp), dd + jnp.sum(dyp * uu, axis=0, keepdims=True)

        zc = jnp.zeros((S5_NS, LANES), F32)
        dcr, dci, dd = lax.fori_loop(0, L // S5_ROWS, step, (zc, zc, jnp.zeros((1, LANES), F32)))
        dcm_ref[:S5_NS, :] = dcr
        dcm_ref[S5_NS:, :] = dci
        dd_ref[...] = dd

        ar, ai = _s5_scan(gr_scr, gi_scr, pr_ref[...], -pi_ref[...], L, True, states=(re_scr, im_scr))
        dlam_ref[...] = jnp.concatenate([ar, ai, jnp.zeros((SUBLANES - 2, S5_NS), F32)], axis=0)

        def in_step(i, carry):
            dbr, dbi = carry
            r0 = pl.multiple_of(i * S5_ROWS, S5_ROWS)
            rows = pl.ds(r0, S5_ROWS)
            gr, gi, uu = gr_scr[rows, :], gi_scr[rows, :], u_ref[rows, :]
            du_ref[rows, :] = dyp_scr[rows, :] * d_ref[...] + _nt(gr, bm_ref[:, :S5_NS]) + _nt(gi, bm_ref[:, S5_NS:])
            return dbr + _tn(uu, gr), dbi + _tn(uu, gi)

        zb = jnp.zeros((LANES, S5_NS), F32)
        dbr, dbi = lax.fori_loop(0, L // S5_ROWS, in_step, (zb, zb))
        dbm_ref[:, :S5_NS] = dbr
        dbm_ref[:, S5_NS:] = dbi

    return pl.pallas_call(
        body, name="s5_bwd", grid=(S5_BLOCKS,),
        in_specs=[col, bm, cm, pw, pw, vec, col], out_specs=[col, bm, cm, pw, vec],
        out_shape=[jax.ShapeDtypeStruct((L, D_MODEL), F32), jax.ShapeDtypeStruct((S5_BLOCKS, LANES, 2 * S5_NS), F32),
                   jax.ShapeDtypeStruct((S5_BLOCKS, 2 * S5_NS, LANES), F32),
                   jax.ShapeDtypeStruct((S5_BLOCKS, SUBLANES, S5_NS), F32), jax.ShapeDtypeStruct((1, D_MODEL), F32)],
        scratch_shapes=[pltpu.VMEM((L, S5_NS), F32)] * 4 + [pltpu.VMEM((L, LANES), F32)],
        compiler_params=_params("parallel"),
    )(u, bmat, cmat, pw_re, pw_im, dsk, dyg)


def _s5_discretize(lam_re, lam_im, log_dt, b_re, b_im, e16):
    dt = jnp.exp(log_dt)
    zr, zi = lam_re * dt, lam_im * dt
    mag = jnp.exp(zr)
    lbr, lbi = mag * jnp.cos(zi), mag * jnp.sin(zi)
    den = lam_re * lam_re + lam_im * lam_im
    nr, ni = lbr - 1.0, lbi
    cr = (nr * lam_re + ni * lam_im) / den
    ci = (ni * lam_re - nr * lam_im) / den
    crw, ciw = _pick(cr, e16), _pick(ci, e16)
    return lbr, lbi, crw * b_re - ciw * b_im, crw * b_im + ciw * b_re


def _s5_params_fwd(lam_re, lam_im, log_dt, b_re, b_im, e16):
    def body(lr, li, ld, br, bi, e, o1, o2, o3, o4):
        for o, val in zip((o1, o2, o3, o4), _s5_discretize(lr[...], li[...], ld[...], br[...], bi[...], e[...])):
            o[...] = val

    g, p, n = S5_GROUPS, S5_STATE, S5_STATE * S5_GROUP
    return pl.pallas_call(
        body, name="s5_params_fwd",
        out_shape=[jax.ShapeDtypeStruct((g, p), F32)] * 2 + [jax.ShapeDtypeStruct((g, n), F32)] * 2,
        compiler_params=_params(),
    )(lam_re, lam_im, log_dt, b_re, b_im, e16)


def _s5_params_bwd(lam_re, lam_im, log_dt, b_re, b_im, e16, cts):
    def body(lr, li, ld, br, bi, e, c1, c2, c3, c4, o1, o2, o3, o4, o5):
        e_m = e[...]
        f = lambda a, b, c, d, g: _s5_discretize(a, b, c, d, g, e_m)
        _, vjp = jax.vjp(f, lr[...], li[...], ld[...], br[...], bi[...])
        for o, val in zip((o1, o2, o3, o4, o5), vjp((c1[...], c2[...], c3[...], c4[...]))):
            o[...] = val

    g, p, n = S5_GROUPS, S5_STATE, S5_STATE * S5_GROUP
    return pl.pallas_call(
        body, name="s5_params_bwd",
        out_shape=[jax.ShapeDtypeStruct((g, p), F32)] * 2 + [jax.ShapeDtypeStruct((g, 1), F32)]
        + [jax.ShapeDtypeStruct((g, n), F32)] * 2,
        compiler_params=_params(),
    )(lam_re, lam_im, log_dt, b_re, b_im, e16, *cts)


def _add_residual(acc, h):
    return (acc + h,)


def _mlp_fwd(i, h, g, w1, w2):
    hn = _rms_fwd(f"mlp{i}_norm", h, g)
    r = _mm(f"mlp{i}_up", hn, w1, "nn", (BF16,), epi=lambda acc: (jnp.square(jnp.maximum(acc, 0.0)),))
    return _mm(f"mlp{i}_down", r, w2, "nn", (F32,), epi=_add_residual, extras=(h,)), (h, hn, r)


def _mlp_bwd(i, dh_out, saved, g, w1, w2, dw):
    h, hn, r = saved
    dw(("mlp_w2", i), f"mlp{i}_dw2", r, dh_out)
    da = _mm(f"mlp{i}_da", dh_out, w2, "nt", (BF16,), epi=lambda acc, rr: (acc * (2.0 * jnp.sqrt(rr.astype(F32))),), extras=(r,))
    dw(("mlp_w1", i), f"mlp{i}_dw1", hn, da)
    dhn = _mm(f"mlp{i}_dhn", da, w1, "nt", (F32,))
    dh, dg = _rms_bwd(f"mlp{i}_dnorm", h, g, dhn, dh_out)
    return dh, dg[0]


def _lanes(v, n):
    return jnp.broadcast_to(v.reshape(n, 1, 1), (n, 1, LANES))


def _gdn_fwd_layer(i, h, g, w_in, conv_w, a_log, dt_bias, o_g, w_out):
    hn = _rms_fwd(f"gdn{i}_norm", h, g)
    p = _mm(f"gdn{i}_in", hn, w_in, "nn", (F32,))
    zb = jnp.zeros((1, D_MODEL), F32)
    qkv = [_conv_fwd(f"gdn{i}_conv{t}", p, t * D_MODEL, conv_w[:, t * D_MODEL:(t + 1) * D_MODEL], zb) for t in range(3)]
    ea, eb = _gdn_selectors()
    y, sprev, sol = _gdn_fwd(*qkv, p, _lanes(a_log, GDN_HEADS), _lanes(dt_bias, GDN_HEADS), o_g.reshape(1, LANES), ea, eb)
    return _mm(f"gdn{i}_out", y, w_out, "nn", (F32,), epi=_add_residual, extras=(h,)), (h, hn, p, qkv, y, sprev, sol)


def _gdn_bwd_layer(i, dh_out, saved, g, w_in, conv_w, a_log, dt_bias, o_g, w_out, dw):
    h, hn, p, qkv, y, sprev, sol = saved
    dy = _mm(f"gdn{i}_dy", dh_out, w_out, "nt", (F32,))
    dw(("gdn_w_out", MIXER_INDEX[i]), f"gdn{i}_dwout", y, dh_out)
    ea, eb = _gdn_selectors()
    dq, dk, dv, dgate, dab, dpar = _gdn_bwd(*qkv, p, _lanes(a_log, GDN_HEADS), _lanes(dt_bias, GDN_HEADS),
                                            o_g.reshape(1, LANES), ea, eb, sprev, sol, dy)
    zb = jnp.zeros((1, D_MODEL), F32)
    dpre, dcw = [], []
    for t, d in enumerate((dq, dk, dv)):
        dx, dwc, _ = _conv_bwd(f"gdn{i}_dconv{t}", p, t * D_MODEL, conv_w[:, t * D_MODEL:(t + 1) * D_MODEL], zb, d)
        dpre.append(dx)
        dcw.append(dwc)
    dp = jnp.concatenate(dpre + [dgate, dab], axis=1).astype(BF16)
    dw(("gdn_w_in", MIXER_INDEX[i]), f"gdn{i}_dwin", hn, dp)
    dhn = _mm(f"gdn{i}_dhn", dp, w_in, "nt", (F32,))
    dh, dg = _rms_bwd(f"gdn{i}_dnorm", h, g, dhn, dh_out)
    grads = dict(conv_w=jnp.concatenate(dcw, axis=1), a_log=jnp.sum(dpar[:, 0, :], axis=-1),
                 dt_bias=jnp.sum(dpar[:, 1, :], axis=-1), o_norm_g=jnp.sum(dpar[:, 2, :], axis=0))
    return dh, dg[0], grads


def _m2_vectors(dt_bias, a_log, d_skip, norm_g):
    pad = lambda v: jnp.pad(v, (0, LANES - M2_HEADS)).reshape(1, LANES)
    return pad(dt_bias), pad(a_log), jnp.repeat(d_skip, M2_HD).reshape(1, M2_INNER), norm_g.reshape(1, M2_INNER)


def _m2_fwd_layer(h, g, w_in, conv_w, conv_b, dt_bias, a_log, d_skip, norm_g, w_out):
    hn = _rms_fwd("m2_norm", h, g)
    p = _mm("m2_in", hn, w_in, "nn", (F32,))
    xbc = _conv_fwd("m2_conv", p, M2_INNER, conv_w, conv_b.reshape(1, M2_CONV_CH))
    e, ecol = _m2_selectors()
    y, sprev = _m2_fwd(xbc, p, *_m2_vectors(dt_bias, a_log, d_skip, norm_g), e, ecol)
    return _mm("m2_out", y, w_out, "nn", (F32,), epi=_add_residual, extras=(h,)), (h, hn, p, xbc, y, sprev)


def _m2_bwd_layer(dh_out, saved, g, w_in, conv_w, conv_b, dt_bias, a_log, d_skip, norm_g, w_out, dw):
    h, hn, p, xbc, y, sprev = saved
    dy = _mm("m2_dy", dh_out, w_out, "nt", (F32,))
    dw(("m2_w_out", 0), "m2_dwout", y, dh_out)
    e, ecol = _m2_selectors()
    dx, db, dc, dz, ddt, dnar, dwide = _m2_bwd(xbc, p, *_m2_vectors(dt_bias, a_log, d_skip, norm_g), e, ecol, sprev, dy)
    dxbc, dcw, dcb = _conv_bwd("m2_dconv", p, M2_INNER, conv_w, conv_b.reshape(1, M2_CONV_CH),
                               jnp.concatenate([dx, db, dc], axis=1))
    dp = jnp.concatenate([dz, dxbc, ddt], axis=1).astype(BF16)
    dw(("m2_w_in", 0), "m2_dwin", hn, dp)
    dhn = _mm("m2_dhn", dp, w_in, "nt", (F32,))
    dh, dg = _rms_bwd("m2_dnorm", h, g, dhn, dh_out)
    grads = dict(conv_w=dcw, conv_b=dcb[0], dt_bias=dnar[0, :M2_HEADS], a_log=dnar[1, :M2_HEADS],
                 d=jnp.sum(dwide[:, 0, :].reshape(M2_HEADS, M2_HD), axis=-1), norm_g=dwide[:, 1, :].reshape(M2_INNER))
    return dh, dg[0], grads


def _s5_selector():
    e16 = np.zeros((S5_STATE, S5_STATE * S5_GROUP), np.float32)
    for p in range(S5_STATE):
        e16[p, p * S5_GROUP:(p + 1) * S5_GROUP] = 1.0
    return jnp.asarray(e16)


def _s5_operands(lbr, lbi, bbr, bbi, c_re, c_im):
    eye = jnp.eye(S5_BLOCKS, dtype=F32)
    gpb = S5_GROUPS // S5_BLOCKS
    bd = lambda t: jnp.einsum("jgpk,gh->jgkhp", t.reshape(S5_BLOCKS, gpb, S5_STATE, S5_GROUP), eye).reshape(S5_BLOCKS, LANES, S5_NS)
    cd = lambda t: jnp.einsum("jgkp,gh->jgphk", t.reshape(S5_BLOCKS, gpb, S5_GROUP, S5_STATE), eye).reshape(S5_BLOCKS, S5_NS, LANES)
    bmat = jnp.concatenate([bd(bbr), bd(bbi)], axis=2).astype(BF16)
    cmat = jnp.concatenate([cd(c_re), -cd(c_im)], axis=1).astype(BF16)
    ar, ai = lbr.reshape(S5_BLOCKS, S5_NS), lbi.reshape(S5_BLOCKS, S5_NS)
    pr, pi = [ar], [ai]
    for _ in range(SUBLANES - 1):
        pr, pi = pr + [pr[-1] * ar - pi[-1] * ai], pi + [pr[-1] * ai + pi[-1] * ar]
    return bmat, cmat, jnp.stack(pr, axis=1), jnp.stack(pi, axis=1)


def _s5_fwd_layer(h, g, w_in, lam_re, lam_im, log_dt, b_re, b_im, c_re, c_im, d_skip, w_out):
    hn = _rms_fwd("s5_norm", h, g)
    u = _mm("s5_in", hn, w_in, "nn", (F32,))
    n = S5_STATE * S5_GROUP
    lbr, lbi, bbr, bbi = _s5_params_fwd(lam_re, lam_im, log_dt.reshape(S5_GROUPS, 1), b_re.reshape(S5_GROUPS, n),
                                        b_im.reshape(S5_GROUPS, n), _s5_selector())
    ops = _s5_operands(lbr, lbi, bbr, bbi, c_re, c_im)
    yg = _s5_fwd(u, *ops, d_skip.reshape(1, D_MODEL))
    ag = _mm("s5_out", yg, w_out, "nn", (F32,))
    return _glu_fwd(h, ag), (h, hn, u, ops, yg, ag)


def _s5_bwd_layer(dh_out, saved, g, w_in, lam_re, lam_im, log_dt, b_re, b_im, c_re, c_im, d_skip, w_out, dw):
    h, hn, u, ops, yg, ag = saved
    dag = _glu_bwd(dh_out, ag)
    dw(("s5_w_out", 0), "s5_dwout", yg, dag)
    dyg = _mm("s5_dyg", dag, w_out, "nt", (F32,))
    du, dbmat, dcmat, dlam, ddsk = _s5_bwd(u, *ops, d_skip.reshape(1, D_MODEL), dyg)
    eye = jnp.eye(S5_BLOCKS, dtype=F32)
    gpb = S5_GROUPS // S5_BLOCKS
    n = S5_STATE * S5_GROUP
    ub = lambda t: jnp.einsum("jgkhp,gh->jgpk", t.reshape(S5_BLOCKS, gpb, S5_GROUP, gpb, S5_STATE), eye).reshape(S5_GROUPS, n)
    uc = lambda t: jnp.einsum("jgphk,gh->jgkp", t.reshape(S5_BLOCKS, gpb, S5_STATE, gpb, S5_GROUP), eye).reshape(c_re.shape)
    cts = (dlam[:, 0, :].reshape(S5_GROUPS, S5_STATE), dlam[:, 1, :].reshape(S5_GROUPS, S5_STATE),
           ub(dbmat[:, :, :S5_NS]), ub(dbmat[:, :, S5_NS:]))
    dlr, dli, dld, dbr, dbi = _s5_params_bwd(lam_re, lam_im, log_dt.reshape(S5_GROUPS, 1), b_re.reshape(S5_GROUPS, n),
                                             b_im.reshape(S5_GROUPS, n), _s5_selector(), cts)
    dw(("s5_w_in", 0), "s5_dwin", hn, du)
    dhn = _mm("s5_dhn", du, w_in, "nt", (F32,))
    dh, dg = _rms_bwd("s5_dnorm", h, g, dhn, dh_out)
    grads = dict(lam_re=dlr, lam_im=dli, log_dt=dld[:, 0], b_re=dbr.reshape(b_re.shape), b_im=dbi.reshape(b_im.shape),
                 c_re=uc(dcmat[:, :S5_NS, :]), c_im=-uc(dcmat[:, S5_NS:, :]), d=ddsk[0])
    return dh, dg[0], grads


MIXER_OF_LAYER = ("gdn", "s5", "m2", "gdn")
MIXER_INDEX = (0, 0, 0, 1)


def _mixer_args(W, i):
    kind, j = MIXER_OF_LAYER[i], MIXER_INDEX[i]
    if kind == "gdn":
        return tuple(W["gdn_" + k][j] for k in ("w_in", "conv_w", "a_log", "dt_bias", "o_norm_g", "w_out"))
    if kind == "s5":
        return tuple(W["s5_" + k][j] for k in ("w_in", "lam_re", "lam_im", "log_dt", "b_re", "b_im", "c_re", "c_im", "d", "w_out"))
    return tuple(W["m2_" + k][j] for k in ("w_in", "conv_w", "conv_b", "dt_bias", "a_log", "d", "norm_g", "w_out"))


def _local_step(x, target, W, dw, on_layer_done):
    h = x
    saved = []
    for i in range(DEPTH):
        kind = MIXER_OF_LAYER[i]
        args = _mixer_args(W, i)
        if kind == "gdn":
            h, sm = _gdn_fwd_layer(i, h, W["norm_mix_g"][i], *args)
        elif kind == "s5":
            h, sm = _s5_fwd_layer(h, W["norm_mix_g"][i], *args)
        else:
            h, sm = _m2_fwd_layer(h, W["norm_mix_g"][i], *args)
        h, sp = _mlp_fwd(i, h, W["norm_mlp_g"][i], W["mlp_w1"][i], W["mlp_w2"][i])
        saved.append((sm, sp))
    loss, dh, dgf = _loss_head(h, W["final_norm_g"], target)
    G = {"final_norm_g": dgf[0], "norm_mix_g": [None] * DEPTH, "norm_mlp_g": [None] * DEPTH}
    mix = {}
    for i in reversed(range(DEPTH)):
        kind = MIXER_OF_LAYER[i]
        sm, sp = saved[i]
        dh, G["norm_mlp_g"][i] = _mlp_bwd(i, dh, sp, W["norm_mlp_g"][i], W["mlp_w1"][i], W["mlp_w2"][i], dw)
        args = _mixer_args(W, i)
        if kind == "gdn":
            dh, G["norm_mix_g"][i], gm = _gdn_bwd_layer(i, dh, sm, W["norm_mix_g"][i], *args, dw)
        elif kind == "s5":
            dh, G["norm_mix_g"][i], gm = _s5_bwd_layer(dh, sm, W["norm_mix_g"][i], *args, dw)
        else:
            dh, G["norm_mix_g"][i], gm = _m2_bwd_layer(dh, sm, W["norm_mix_g"][i], *args, dw)
        j = MIXER_INDEX[i]
        on_layer_done(i)
        for k, v in gm.items():
            mix.setdefault(kind + "_" + k, {})[j] = v
    for k, d in mix.items():
        G[k] = [d[j] for j in sorted(d)]
    return loss, dh, {k: jnp.stack(v) if isinstance(v, list) else v for k, v in G.items()}


ADAM_ROWS = 128
ADAM_COLS = 128


def _adamw(name, w, g, m, v):
    R, C = w.shape
    if R % ADAM_ROWS == 0:
        grid, blk = (R // ADAM_ROWS,), pl.BlockSpec((ADAM_ROWS, C), lambda i: (i, 0))
    else:
        grid, blk = (C // ADAM_COLS,), pl.BlockSpec((R, ADAM_COLS), lambda j: (0, j))

    def body(w_ref, g_ref, m_ref, v_ref, d_ref, mo_ref, vo_ref):
        gg = g_ref[...]
        mn = ADAM_B1 * m_ref[...] + (1.0 - ADAM_B1) * gg
        vn = ADAM_B2 * v_ref[...] + (1.0 - ADAM_B2) * (gg * gg)
        m_hat = mn / (1.0 - ADAM_B1 ** ADAM_STEP)
        v_hat = vn / (1.0 - ADAM_B2 ** ADAM_STEP)
        d_ref[...] = -ADAM_LR * (m_hat / (jnp.sqrt(v_hat) + ADAM_EPS) + ADAM_WD * w_ref[...])
        mo_ref[...] = mn
        vo_ref[...] = vn

    return pl.pallas_call(
        body, name=name, grid=grid, in_specs=[blk] * 4, out_specs=[blk] * 3,
        out_shape=[jax.ShapeDtypeStruct((R, C), F32)] * 3, compiler_params=_params("parallel"),
    )(w, g, m, v)


MESH = pl.DeviceIdType.MESH
ANY = pl.BlockSpec(memory_space=pl.ANY)
N_CHIPS = 4
N_DEV = 8


def _position():
    return lax.axis_index("x"), lax.axis_index("y"), lax.axis_index("c")


GATHER_IDS = {1: 1, 2: 2}
EXCHANGE_IDS = {0: 4, 1: 5, 2: 6, 3: 7}


LINK_SLOWDOWN = 40


def _link_cost(link_bytes):
    return pl.CostEstimate(flops=0, transcendentals=0, bytes_accessed=LINK_SLOWDOWN * link_bytes)


def _gather_body(w_refs, out_refs, send_sems, recv_sems):
    x, y, c = _position()
    sibling = (x, y, 1 - c)
    chips = [(1 - x, y), (x, 1 - y), (1 - x, 1 - y)]
    firsts, passes = [], []
    for t, (w_ref, out_ref) in enumerate(zip(w_refs, out_refs)):
        half = w_ref.shape[0] // 2

        def piece(cx, cy, hc, out_ref=out_ref, half=half):
            return out_ref.at[2 * cx + cy, pl.ds(hc * half, half), :]

        def copy(k, src, dst, to, t=t):
            return pltpu.make_async_remote_copy(src_ref=src, dst_ref=dst, send_sem=send_sems.at[6 * t + k],
                                                recv_sem=recv_sems.at[6 * t + k], device_id=to, device_id_type=MESH)

        first = [copy(j, w_ref.at[pl.ds(c * half, half), :], piece(x, y, c), (*chip, c)) for j, chip in enumerate(chips)]
        for cp in first:
            cp.start()
        firsts.append((first, piece, copy))
    for first, piece, copy in firsts:
        passed = [copy(3 + j, piece(*chip, c), piece(*chip, c), sibling) for j, chip in enumerate(chips)]
        for j, chip in enumerate(chips):
            copy(j, piece(*chip, c), piece(*chip, c), sibling).wait_recv()
            passed[j].start()
        passes.append(passed)
    for (first, piece, copy), passed in zip(firsts, passes):
        for j, chip in enumerate(chips):
            copy(3 + j, piece(*chip, 1 - c), piece(*chip, 1 - c), sibling).wait_recv()
        for cp in first + passed:
            cp.wait_send()


def _gather_shards(wps):
    n = len(wps)

    def body(*refs):
        _gather_body(refs[:n], refs[n:2 * n], *refs[2 * n:])

    return pl.pallas_call(
        body, name="gather_shards", in_specs=[ANY] * n, out_specs=[ANY] * n,
        out_shape=[jax.ShapeDtypeStruct((N_CHIPS, *wp.shape), wp.dtype) for wp in wps],
        scratch_shapes=[pltpu.SemaphoreType.DMA((6 * n,)), pltpu.SemaphoreType.DMA((6 * n,))],
    )(*wps)


def _gather_shards_later(wps, part):
    n = len(wps)
    w_refs = [jax.new_ref(wp, memory_space=pltpu.MemorySpace.HBM) for wp in wps]
    out_refs = [jax.empty_ref(jax.ShapeDtypeStruct((N_CHIPS, *wp.shape), wp.dtype), memory_space=pltpu.MemorySpace.HBM)
                for wp in wps]

    @pl.kernel(mesh=plsc.ScalarSubcoreMesh(axis_name="sequencer", num_cores=1), name=f"gather_shards_later{part}",
               scratch_types=(pltpu.SemaphoreType.DMA((6 * n,)), pltpu.SemaphoreType.DMA((6 * n,))),
               cost_estimate=_link_cost(3 * sum(wp.size * wp.dtype.itemsize for wp in wps)),
               compiler_params=pltpu.CompilerParams(collective_id=GATHER_IDS[part]))
    def launch(send_sems, recv_sems):
        x, y, c = _position()
        barrier = pltpu.get_barrier_semaphore()
        for peer in [(x, y, 1 - c), (1 - x, y, c), (x, 1 - y, c), (1 - x, 1 - y, c)]:
            pl.semaphore_signal(barrier, inc=1, device_id=peer, device_id_type=MESH)
        pl.semaphore_wait(barrier, 4)
        _gather_body(w_refs, out_refs, send_sems, recv_sems)

    launch()
    return [r[...] for r in out_refs]


def _pair_exchange(name, gps):
    n = len(gps)

    def body(*refs):
        g_refs, out_refs, (send_sems, recv_sems) = refs[:n], refs[n:2 * n], refs[2 * n:]
        x, y, c = _position()
        copies = []
        for t, (g_ref, out_ref) in enumerate(zip(g_refs, out_refs)):
            half = g_ref.shape[1] // 2
            copies += [pltpu.make_async_remote_copy(
                src_ref=g_ref.at[k, pl.ds((1 - c) * half, half), :], dst_ref=out_ref.at[k], send_sem=send_sems.at[N_CHIPS * t + k],
                recv_sem=recv_sems.at[N_CHIPS * t + k], device_id=(x, y, 1 - c), device_id_type=MESH) for k in range(N_CHIPS)]
        for cp in copies:
            cp.start()
        for cp in copies:
            cp.wait()

    return pl.pallas_call(
        body, name=name, in_specs=[ANY] * n, out_specs=[ANY] * n,
        out_shape=[jax.ShapeDtypeStruct((N_CHIPS, gp.shape[1] // 2, gp.shape[2]), gp.dtype) for gp in gps],
        scratch_shapes=[pltpu.SemaphoreType.DMA((N_CHIPS * n,)), pltpu.SemaphoreType.DMA((N_CHIPS * n,))],
    )(*gps)


SUM_ROWS = (1280, 1152, 1024, 512, 256, 128)


def _pair_sum(name, gp, got, core):
    n, R, C = gp.shape
    half = R // 2
    tr = _tile(half, SUM_ROWS)
    nb = half // tr

    def body(core_ref, g_ref, r_ref, o_ref):
        o_ref[...] = (g_ref[...].astype(F32) + r_ref[...].astype(F32)).astype(o_ref.dtype)

    return pl.pallas_call(
        body, name=name,
        grid_spec=pltpu.PrefetchScalarGridSpec(
            num_scalar_prefetch=1, grid=(n, nb),
            in_specs=[pl.BlockSpec((None, tr, C), lambda k, i, core_ref: (k, core_ref[0] * nb + i, 0)),
                      pl.BlockSpec((None, tr, C), lambda k, i, core_ref: (k, i, 0))],
            out_specs=pl.BlockSpec((None, tr, C), lambda k, i, core_ref: (k, i, 0))),
        out_shape=jax.ShapeDtypeStruct((n, half, C), gp.dtype), compiler_params=_params("parallel", "parallel"),
    )(core, gp, got)


def _chip_exchange_body(t_refs, out_refs, send_sems, recv_sems):
    x, y, c = _position()
    chips = [(1 - x, y), (x, 1 - y), (1 - x, 1 - y)]
    copies, waits = [], []
    for t, (t_ref, out_ref) in enumerate(zip(t_refs, out_refs)):
        for j, (cx, cy) in enumerate(chips):
            sems = dict(send_sem=send_sems.at[3 * t + j], recv_sem=recv_sems.at[3 * t + j], device_id=(cx, cy, c),
                        device_id_type=MESH)
            copies.append(pltpu.make_async_remote_copy(src_ref=t_ref.at[2 * cx + cy], dst_ref=out_ref.at[2 * x + y], **sems))
            waits.append(pltpu.make_async_remote_copy(src_ref=t_ref.at[2 * cx + cy], dst_ref=out_ref.at[2 * cx + cy], **sems))
    for cp in copies:
        cp.start()
    for cp in waits:
        cp.wait_recv()
    for cp in copies:
        cp.wait_send()


def _chip_exchange_later(ts, layer):
    n = len(ts)
    t_refs = [jax.new_ref(t, memory_space=pltpu.MemorySpace.HBM) for t in ts]
    out_refs = [jax.empty_ref(jax.ShapeDtypeStruct(t.shape, t.dtype), memory_space=pltpu.MemorySpace.HBM) for t in ts]

    @pl.kernel(mesh=plsc.ScalarSubcoreMesh(axis_name="sequencer", num_cores=1), name=f"chip_exchange_later{layer}",
               scratch_types=(pltpu.SemaphoreType.DMA((3 * n,)), pltpu.SemaphoreType.DMA((3 * n,))),
               cost_estimate=_link_cost(3 * sum(t.size * t.dtype.itemsize for t in ts) // N_CHIPS),
               compiler_params=pltpu.CompilerParams(collective_id=EXCHANGE_IDS[layer]))
    def launch(send_sems, recv_sems):
        x, y, c = _position()
        barrier = pltpu.get_barrier_semaphore()
        for peer in [(1 - x, y, c), (x, 1 - y, c), (1 - x, 1 - y, c)]:
            pl.semaphore_signal(barrier, inc=1, device_id=peer, device_id_type=MESH)
        pl.semaphore_wait(barrier, 3)
        _chip_exchange_body(t_refs, out_refs, send_sems, recv_sems)

    launch()
    return [r[...] for r in out_refs]


def _chip_sum(name, t, got, ids):
    n, H, C = t.shape
    tr = _tile(H, SUM_ROWS)
    nb = H // tr

    def body(ids_ref, t_ref, r_ref, o_ref):
        own = t_ref[...].astype(F32)
        acc = jnp.where(ids_ref[0] == 0, own, r_ref[0].astype(F32))
        for k in range(1, n):
            acc = acc + jnp.where(ids_ref[0] == k, own, r_ref[k].astype(F32))
        o_ref[...] = acc

    return pl.pallas_call(
        body, name=name,
        grid_spec=pltpu.PrefetchScalarGridSpec(
            num_scalar_prefetch=1, grid=(nb,),
            in_specs=[pl.BlockSpec((None, tr, C), lambda i, ids_ref: (ids_ref[0], i, 0)),
                      pl.BlockSpec((n, tr, C), lambda i, ids_ref: (0, i, 0))],
            out_specs=pl.BlockSpec((tr, C), lambda i, ids_ref: (ids_ref[1] * nb + i, 0))),
        out_shape=jax.ShapeDtypeStruct((2 * H, C), F32), compiler_params=_params("parallel"),
    )(ids, t, got)


def _sum_pieces(name, pieces):
    n, R, C = pieces.shape
    tr = _tile(R, (256, 128, SUBLANES))

    def body(p_ref, o_ref):
        acc = p_ref[0].astype(F32)
        for s in range(1, n):
            acc = acc + p_ref[s].astype(F32)
        o_ref[...] = acc

    return pl.pallas_call(
        body, name=name, grid=(R // tr,),
        in_specs=[pl.BlockSpec((n, tr, C), lambda i: (0, i, 0))], out_specs=pl.BlockSpec((tr, C), lambda i: (i, 0)),
        out_shape=jax.ShapeDtypeStruct((R, C), F32), compiler_params=_params("parallel"),
    )(pieces)


def _swap_halves(name, ss):
    n = len(ss)

    def body(*refs):
        s_refs, out_refs, (send_sems, recv_sems) = refs[:n], refs[n:2 * n], refs[2 * n:]
        x, y, c = _position()
        copies, waits = [], []
        for t, (s_ref, out_ref) in enumerate(zip(s_refs, out_refs)):
            half = s_ref.shape[0] // 2
            sems = dict(send_sem=send_sems.at[t], recv_sem=recv_sems.at[t], device_id=(x, y, 1 - c), device_id_type=MESH)
            mine = s_ref.at[pl.ds(c * half, half), :]
            copies.append(pltpu.make_async_remote_copy(src_ref=mine, dst_ref=out_ref.at[pl.ds(c * half, half), :], **sems))
            waits.append(pltpu.make_async_remote_copy(src_ref=mine, dst_ref=out_ref.at[pl.ds((1 - c) * half, half), :], **sems))
        for cp in copies:
            cp.start()
        for cp in waits:
            cp.wait_recv()
        for cp in copies:
            cp.wait_send()

    return pl.pallas_call(
        body, name=name, in_specs=[ANY] * n, out_specs=[ANY] * n, input_output_aliases={i: i for i in range(n)},
        out_shape=[jax.ShapeDtypeStruct(s_.shape, s_.dtype) for s_ in ss],
        scratch_shapes=[pltpu.SemaphoreType.DMA((n,)), pltpu.SemaphoreType.DMA((n,))],
    )(*ss)


def _gather_small(name, blk):
    m_per, n = blk.shape

    def body(x_ref, out_ref, send_sems, recv_sems, local_sem):
        x, y, c = _position()
        me, sibling = (x, y, c), (x, y, 1 - c)
        chips = [(1 - x, y), (x, 1 - y), (1 - x, 1 - y)]

        def rows(px, py, pc):
            return out_ref.at[pl.ds((4 * px + 2 * py + pc) * m_per, m_per), :]

        def copy(k, block, to, src=None):
            return pltpu.make_async_remote_copy(src_ref=rows(*block) if src is None else src, dst_ref=rows(*block),
                                                send_sem=send_sems.at[k], recv_sem=recv_sems.at[k], device_id=to, device_id_type=MESH)

        mine = pltpu.make_async_copy(x_ref, rows(*me), local_sem)
        mine.start()
        first = [copy(0, me, sibling, src=x_ref)] + [copy(1 + j, me, (*chip, c), src=x_ref) for j, chip in enumerate(chips)]
        for cp in first:
            cp.start()
        passed = [copy(4 + j, (*chip, c), sibling) for j, chip in enumerate(chips)]
        for j, chip in enumerate(chips):
            copy(1 + j, (*chip, c), me).wait_recv()
            passed[j].start()
        copy(0, sibling, me).wait_recv()
        for j, chip in enumerate(chips):
            copy(4 + j, (*chip, 1 - c), me).wait_recv()
        for cp in first + passed:
            cp.wait_send()
        mine.wait()

    return pl.pallas_call(
        body, name=name, out_shape=jax.ShapeDtypeStruct((N_DEV * m_per, n), blk.dtype),
        in_specs=[pl.BlockSpec(memory_space=pltpu.VMEM)], out_specs=pl.BlockSpec(memory_space=pltpu.VMEM),
        scratch_shapes=[pltpu.SemaphoreType.DMA((7,)), pltpu.SemaphoreType.DMA((7,)), pltpu.SemaphoreType.DMA],
        compiler_params=pltpu.CompilerParams(vmem_limit_bytes=VMEM_LIMIT_BYTES),
    )(blk)


WEIGHTS = ("norm_mix_g", "norm_mlp_g", "mlp_w1", "mlp_w2", "gdn_w_in", "gdn_conv_w", "gdn_a_log", "gdn_dt_bias", "gdn_o_norm_g",
           "gdn_w_out", "s5_w_in", "s5_lam_re", "s5_lam_im", "s5_log_dt", "s5_b_re", "s5_b_im", "s5_c_re", "s5_c_im", "s5_d",
           "s5_w_out", "m2_w_in", "m2_conv_w", "m2_conv_b", "m2_dt_bias", "m2_a_log", "m2_d", "m2_norm_g", "m2_w_out",
           "final_norm_g")
BIG = {"mlp_w1": 2, "mlp_w2": 1, "gdn_w_in": 2, "gdn_w_out": 1, "s5_w_in": 1, "s5_w_out": 2, "m2_w_in": 2, "m2_w_out": 1}
SMALL_CUT = {"gdn_conv_w": 2, "m2_conv_w": 2, "m2_conv_b": 1, "m2_norm_g": 1}
ROWS_MINOR = ("m2_w_in",)
ODD_WIDTH = {"gdn_w_in": GDN_IN, "m2_w_in": M2_IN}
WEIGHT_PARTS = (
    ((("gdn_w_out", 0),), (("gdn_w_in", 0),)),
    ((("mlp_w1", 0), ("mlp_w2", 0), ("mlp_w1", 1), ("mlp_w2", 1), ("s5_w_in", 0)), (("s5_w_out", 0),)),
    ((("mlp_w1", 2), ("mlp_w2", 2), ("m2_w_out", 0), ("mlp_w1", 3), ("mlp_w2", 3), ("gdn_w_out", 1)), (("m2_w_in", 0),),
     (("gdn_w_in", 1),)),
)
LAYER_ITEMS = (
    ((("mlp_w1", 0), ("mlp_w2", 0), ("gdn_w_out", 0)), (("gdn_w_in", 0),)),
    ((("mlp_w1", 1), ("mlp_w2", 1), ("s5_w_in", 0)), (("s5_w_out", 0),)),
    ((("mlp_w1", 2), ("mlp_w2", 2), ("m2_w_out", 0)), (("m2_w_in", 0),)),
    ((("mlp_w1", 3), ("mlp_w2", 3), ("gdn_w_out", 1)), (("gdn_w_in", 1),)),
)


def _rows2d(a):
    return a.reshape(-1, a.shape[-1])


def _pack(arrays, cols, row_multiple, dtype):
    flat = jnp.concatenate([a.reshape(-1).astype(dtype) for a in arrays])
    n = -(-flat.shape[0] // (cols * row_multiple)) * cols * row_multiple
    return jnp.pad(flat, (0, n - flat.shape[0])).reshape(-1, cols)


def _unpack(packed, shapes):
    flat = packed.reshape(-1)
    out, off = [], 0
    for shp in shapes:
        n = math.prod(shp)
        out.append(flat[off:off + n].reshape(shp))
        off += n
    return out


def _split_rows(buf, shapes):
    out, off = [], 0
    for shp in shapes:
        rows = math.prod(shp[:-1])
        out.append(buf[off:off + rows].reshape(shp))
        off += rows
    return out


def _cut(a, axis, k):
    n = a.shape[axis] // N_CHIPS
    return lax.slice_in_dim(a, k * n, (k + 1) * n, axis=axis)


def kernel(x, norm_mix_g, norm_mlp_g, mlp_w1, mlp_w2, gdn_w_in, gdn_conv_w, gdn_a_log, gdn_dt_bias, gdn_o_norm_g, gdn_w_out, s5_w_in, s5_lam_re, s5_lam_im, s5_log_dt, s5_b_re, s5_b_im, s5_c_re, s5_c_im, s5_d, s5_w_out, m2_w_in, m2_conv_w, m2_conv_b, m2_dt_bias, m2_a_log, m2_d, m2_norm_g, m2_w_out, final_norm_g, loss_target, m_norm_mix_g, m_norm_mlp_g, m_mlp_w1, m_mlp_w2, m_gdn_w_in, m_gdn_conv_w, m_gdn_a_log, m_gdn_dt_bias, m_gdn_o_norm_g, m_gdn_w_out, m_s5_w_in, m_s5_lam_re, m_s5_lam_im, m_s5_log_dt, m_s5_b_re, m_s5_b_im, m_s5_c_re, m_s5_c_im, m_s5_d, m_s5_w_out, m_m2_w_in, m_m2_conv_w, m_m2_conv_b, m_m2_dt_bias, m_m2_a_log, m_m2_d, m_m2_norm_g, m_m2_w_out, m_final_norm_g, v_norm_mix_g, v_norm_mlp_g, v_mlp_w1, v_mlp_w2, v_gdn_w_in, v_gdn_conv_w, v_gdn_a_log, v_gdn_dt_bias, v_gdn_o_norm_g, v_gdn_w_out, v_s5_w_in, v_s5_lam_re, v_s5_lam_im, v_s5_log_dt, v_s5_b_re, v_s5_b_im, v_s5_c_re, v_s5_c_im, v_s5_d, v_s5_w_out, v_m2_w_in, v_m2_conv_w, v_m2_conv_b, v_m2_dt_bias, v_m2_a_log, v_m2_d, v_m2_norm_g, v_m2_w_out, v_final_norm_g):
    given = dict(locals())
    w = {n: given[n] for n in WEIGHTS}
    mom = {n: given["m_" + n] for n in WEIGHTS}
    var = {n: given["v_" + n] for n in WEIGHTS}
    big, small_cut = tuple(BIG), tuple(SMALL_CUT)
    small = tuple(n for n in WEIGHTS if n not in BIG)
    chip = 2 * lax.axis_index("x") + lax.axis_index("y")

    W = {n: [None] * w[n].shape[0] for n in big}

    def fetch(groups, gather):
        own = [jnp.concatenate([w[n][l] for n, l in grp]).astype(BF16) for grp in groups]
        for grp, mine, got in zip(groups, own, gather(own)):
            shapes = [w[n][l].shape for n, l in grp]
            per_chip = [_split_rows(jnp.where(chip == k, mine, got[k]), shapes) for k in range(N_CHIPS)]
            for i, (n, l) in enumerate(grp):
                m = jnp.concatenate([per_chip[k][i] for k in range(N_CHIPS)], axis=BIG[n] - 1)
                pad = {"gdn_w_in": GDN_IN_PAD - GDN_IN, "m2_w_in": M2_IN_PAD - M2_IN}.get(n, 0)
                W[n][l] = jnp.pad(m, ((0, 0), (0, pad))) if pad else m

    for part in (1, 2):
        fetch(WEIGHT_PARTS[part], functools.partial(_gather_shards_later, part=part))
    fetch(WEIGHT_PARTS[0], _gather_shards)
    cut_blk = _pack([w[n] for n in small_cut], LANES, SUBLANES, F32)
    cut_all = _gather_small("gather_small_params", cut_blk).reshape(N_DEV, *cut_blk.shape)
    per_chip = [_unpack(cut_all[2 * k], [w[n].shape for n in small_cut]) for k in range(N_CHIPS)]
    W.update({n: jnp.concatenate([per_chip[k][i] for k in range(N_CHIPS)], axis=SMALL_CUT[n]) for i, n in enumerate(small_cut)})
    W.update({n: w[n] for n in small if n not in SMALL_CUT})

    core = lax.axis_index("c").astype(jnp.int32)
    ids = jnp.stack([chip.astype(jnp.int32), core])
    shard_grads = {}

    place, comm_bufs, odd = {}, {}, {}
    for layer, groups in enumerate(LAYER_ITEMS):
        for j, grp in enumerate(groups):
            row = 0
            for n, l in grp:
                place[n, l] = (layer, j, row)
                row += w[n].shape[1]
            if grp[0][0] not in ODD_WIDTH:
                comm_bufs[layer, j] = lax.empty((N_CHIPS, row, w[grp[0][0]].shape[2]), BF16)

    def dw(item, name, a, b):
        n = item[0]
        layer, j, row0 = place[item]
        if n in ODD_WIDTH:
            full = _mm(name, a, b, "tn", (BF16,))
            odd[layer, j] = jnp.stack([_cut(full[:, :ODD_WIDTH[n]], 1, k) for k in range(N_CHIPS)])
        else:
            comm_bufs[layer, j] = _dw_into(name, a, b, comm_bufs[layer, j], row0, BIG[n] - 1)

    def reduce_layer(i):
        groups = LAYER_ITEMS[i]
        gps = [comm_bufs[i, j] if (i, j) in comm_bufs else odd[i, j] for j in range(len(groups))]
        pairs = [_pair_sum(f"pair_sum{i}_{j}", gp, got, core.reshape(1))
                 for j, (gp, got) in enumerate(zip(gps, _pair_exchange(f"pair_exchange{i}", gps)))]
        sums = [_chip_sum(f"chip_sum{i}_{j}", t, got, ids) for j, (t, got) in enumerate(zip(pairs, _chip_exchange_later(pairs, i)))]
        for grp, g_shard in zip(groups, _swap_halves(f"swap_halves{i}", sums)):
            shard_grads.update(zip(grp, _split_rows(g_shard, [w[n][l].shape for n, l in grp])))

    loss, grad_x, G = _local_step(x[0], loss_target[0], W, dw, reduce_layer)
    loss = lax.psum(loss[0, 0], ("x", "y", "c"))
    grads = {n: jnp.stack([shard_grads[n, l] for l in range(w[n].shape[0])]) for n in big}
    sg = _pack([G[n] for n in small], LANES, ADAM_ROWS, F32)
    sg_sum = _sum_pieces("sum_small_grads", _gather_small("gather_small_grads", sg).reshape(N_DEV, *sg.shape))
    for n, g in zip(small, _unpack(sg_sum, [G[n].shape for n in small])):
        if n in SMALL_CUT:
            width = g.shape[SMALL_CUT[n]] // N_CHIPS
            g = lax.dynamic_slice_in_dim(g, chip * width, width, axis=SMALL_CUT[n])
        grads[n] = g.reshape(w[n].shape)

    delta, new_m, new_v = {}, {}, {}
    for n in big:
        if n in ROWS_MINOR:
            as2d = lambda a: jnp.swapaxes(a, -1, -2).reshape(-1, a.shape[-2])
            back = lambda o: jnp.swapaxes(o.reshape(w[n].shape[0], w[n].shape[2], w[n].shape[1]), -1, -2)
        else:
            as2d = lambda a: a.reshape(-1, a.shape[-1])
            back = lambda o: o.reshape(w[n].shape)
        outs = _adamw("adamw_" + n, as2d(w[n]), as2d(grads[n]), as2d(mom[n]), as2d(var[n]))
        delta[n], new_m[n], new_v[n] = (back(o) for o in outs)
    packs = [_pack([t[n] for n in small], LANES, ADAM_ROWS, F32) for t in (w, grads, mom, var)]
    outs = _adamw("adamw_small", *packs)
    for t, o in zip((delta, new_m, new_v), outs):
        t.update(zip(small, _unpack(o, [w[n].shape for n in small])))

    return (loss, grad_x[None], *[grads[n] for n in WEIGHTS], *[delta[n] for n in WEIGHTS], *[new_m[n] for n in WEIGHTS],
            *[new_v[n] for n in WEIGHTS])
```

```python
import functools
import math

import numpy as np
import jax
import jax.numpy as jnp
from jax import lax
from jax.experimental import pallas as pl
from jax.experimental.pallas import tpu as pltpu
from jax.experimental.pallas import tpu_sc as plsc

F32 = jnp.float32
BF16 = jnp.bfloat16

D_MODEL = 1024
D_FF = 4096
DEPTH = 4
CHUNK = 64
RMS_EPS = 1e-6
CONV_W = 4
GDN_HEADS = 8
GDN_DK = 128
GDN_IN = 4112
GDN_IN_PAD = 4224
S5_GROUPS = 64
S5_STATE = 64
S5_GROUP = 16
S5_BLOCKS = 8
M2_INNER = 2048
M2_HEADS = 32
M2_GROUPS = 8
M2_STATE = 128
M2_CONV_CH = 4096
M2_IN = 6176
M2_IN_PAD = 6272
ADAM_LR, ADAM_B1, ADAM_B2, ADAM_EPS, ADAM_WD, ADAM_STEP = 0.001, 0.9, 0.999, 1e-08, 0.01, 10

VMEM_LIMIT_BYTES = 56 * 1024 * 1024
SUBLANES = 8
LANES = 128


def _params(*sem):
    return pltpu.CompilerParams(dimension_semantics=tuple(sem) if sem else None, vmem_limit_bytes=VMEM_LIMIT_BYTES)


NN, NT, TN = ((1,), (0,)), ((1,), (1,)), ((0,), (0,))
_DOT_TRANSPOSES = {NN: ((NT, "gb"), (TN, "ag")), NT: ((NN, "gb"), (TN, "ga")), TN: ((NT, "bg"), (NN, "ag"))}


def _dg(a, b, dims):
    if a.ndim == 3:
        dn = (((dims[0][0] + 1,), (dims[1][0] + 1,)), ((0,), (0,)))
    else:
        dn = (dims, ((), ()))
    return lax.dot_general(a, b, dn, preferred_element_type=F32)


def _mxu(a, b, dims):
    return _dg(a.astype(BF16), b.astype(BF16), dims)


@functools.partial(jax.custom_vjp, nondiff_argnums=(2,))
def _dot(a, b, dims=NN):
    return _mxu(a, b, dims)


def _dot_fwd(a, b, dims):
    return _mxu(a, b, dims), (a, b)


def _dot_bwd(dims, res, g):
    ops = dict(a=res[0], b=res[1], g=g)
    (da_dims, da_ops), (db_dims, db_ops) = _DOT_TRANSPOSES[dims]
    return (_mxu(ops[da_ops[0]], ops[da_ops[1]], da_dims).astype(res[0].dtype),
            _mxu(ops[db_ops[0]], ops[db_ops[1]], db_dims).astype(res[1].dtype))


_dot.defvjp(_dot_fwd, _dot_bwd)


def _nt(a, b):
    return _dot(a, b, NT)


def _tn(a, b):
    return _dot(a, b, TN)


def _split3(x):
    x1 = x.astype(BF16)
    r = x - x1.astype(F32)
    x2 = r.astype(BF16)
    return x1, x2, (r - x2.astype(F32)).astype(BF16)


def _sel_mxu(x, sel, dims, x_first):
    f = (lambda p: _dg(p, sel.astype(BF16), dims)) if x_first else (lambda p: _dg(sel.astype(BF16), p, dims))
    x1, x2, x3 = _split3(x)
    return f(x1) + (f(x2) + f(x3))


@jax.custom_vjp
def _pick(x, sel):
    return _sel_mxu(x, sel, NN, True)


def _pick_fwd(x, sel):
    return _sel_mxu(x, sel, NN, True), sel


def _pick_bwd(sel, g):
    return _sel_mxu(g, sel, NT, True), jnp.zeros_like(sel)


_pick.defvjp(_pick_fwd, _pick_bwd)


@jax.custom_vjp
def _accum(sel, x):
    return _sel_mxu(x, sel, NN, False)


def _accum_fwd(sel, x):
    return _sel_mxu(x, sel, NN, False), sel


def _accum_bwd(sel, g):
    return jnp.zeros_like(sel), _sel_mxu(g, sel, TN, False)


_accum.defvjp(_accum_fwd, _accum_bwd)


def _dot3(a, b, dims=NN):
    ah, bh = a.astype(BF16), b.astype(BF16)
    al, bl = (a - ah.astype(F32)).astype(BF16), (b - bh.astype(F32)).astype(BF16)
    return _dg(ah, bh, dims) + (_dg(ah, bl, dims) + _dg(al, bh, dims))


def _neumann(x, r, dims):
    r = r + _dot3(x, r, dims)
    for _ in range(5):
        x = _dot3(x, x)
        r = r + _dot3(x, r, dims)
    return r


@jax.custom_vjp
def _unit_lower_solve(a, rhs):
    return _neumann(-a, rhs, NN)


def _unit_lower_solve_fwd(a, rhs):
    sol = _neumann(-a, rhs, NN)
    return sol, (a, sol)


def _unit_lower_solve_bwd(res, ct):
    a, sol = res
    d_rhs = _neumann(-a, ct, TN)
    return -_dot3(d_rhs, sol, NT), d_rhs


_unit_lower_solve.defvjp(_unit_lower_solve_fwd, _unit_lower_solve_bwd)


@jax.custom_vjp
def _unit_lower_solved(a, rhs, sol):
    return sol


def _unit_lower_solved_fwd(a, rhs, sol):
    return sol, (a, sol)


def _unit_lower_solved_bwd(res, ct):
    da, d_rhs = _unit_lower_solve_bwd(res, ct)
    return da, d_rhs, jnp.zeros_like(ct)


_unit_lower_solved.defvjp(_unit_lower_solved_fwd, _unit_lower_solved_bwd)


def _sigmoid(x):
    return 1.0 / (1.0 + jnp.exp(-x))


def _softplus(x):
    return jnp.maximum(x, 0.0) + jnp.log(1.0 + jnp.exp(-jnp.abs(x)))


def _iota2(shape, axis):
    return lax.broadcasted_iota(jnp.int32, shape, axis)


def _tile(n, cands):
    for c in cands:
        if n % c == 0:
            return c
    return n


MM_TILE_BYTES = 9 * 1024 * 1024


def _mm(name, a, b, mode, out_dtypes, epi=None, extras=(), tn=None):
    if mode == "nn":
        (M, K), N = a.shape, b.shape[1]
    elif mode == "nt":
        (M, K), N = a.shape, b.shape[0]
    else:
        (K, M), N = a.shape, b.shape[1]
    tn = tn or _tile(N, (512, 384, 896, 256, 128))
    out_bytes = tn * (sum(jnp.dtype(d).itemsize for d in out_dtypes) + sum(e.dtype.itemsize for e in extras))
    fits = lambda t: t * K * a.dtype.itemsize <= MM_TILE_BYTES and t * out_bytes <= MM_TILE_BYTES
    tm = next(t for t in (2048, 1024, 512, 256, 128) if M % t == 0 and (fits(t) or t == 128))
    if mode == "nn":
        a_spec, b_spec = pl.BlockSpec((tm, K), lambda i, j: (i, 0)), pl.BlockSpec((K, tn), lambda i, j: (0, j))
        dims = NN
    elif mode == "nt":
        a_spec, b_spec = pl.BlockSpec((tm, K), lambda i, j: (i, 0)), pl.BlockSpec((tn, K), lambda i, j: (j, 0))
        dims = NT
    else:
        a_spec, b_spec = pl.BlockSpec((K, tm), lambda i, j: (0, i)), pl.BlockSpec((K, tn), lambda i, j: (0, j))
        dims = TN
    n_ex = len(extras)

    def body(a_ref, b_ref, *rest):
        acc = _mxu(a_ref[...], b_ref[...], dims)
        res = epi(acc, *[e[...] for e in rest[:n_ex]]) if epi is not None else (acc,)
        for o_ref, r in zip(rest[n_ex:], res):
            o_ref[...] = r.astype(o_ref.dtype)

    tile = pl.BlockSpec((tm, tn), lambda i, j: (i, j))
    out = pl.pallas_call(
        body, name=name, grid=(M // tm, N // tn),
        in_specs=[a_spec, b_spec] + [tile] * n_ex,
        out_specs=[tile] * len(out_dtypes),
        out_shape=[jax.ShapeDtypeStruct((M, N), d) for d in out_dtypes],
        compiler_params=_params("parallel", "parallel"),
    )(a, b, *extras)
    return out if len(out_dtypes) > 1 else out[0]


def _dw_into(name, a, b, buf, row0, cut_axis):
    (K, M), N = a.shape, b.shape[1]
    ms, ns = (M, N // N_CHIPS) if cut_axis == 1 else (M // N_CHIPS, N)
    assert buf.shape[2] == ns, (buf.shape, ns)
    tm = next(t for t in (1024, 512, 256, 128) if ms % t == 0 and row0 % t == 0)
    tn = _tile(ns, (512, 256, 128))
    rb, cb = ms // tm, ns // tn
    if cut_axis == 1:
        where = lambda i, j: (j // cb, row0 // tm + i, j % cb)
    else:
        where = lambda i, j: (i // rb, row0 // tm + i % rb, j)

    def body(a_ref, b_ref, buf_ref, o_ref):
        o_ref[...] = _mxu(a_ref[...], b_ref[...], TN).astype(o_ref.dtype)

    return pl.pallas_call(
        body, name=name, grid=(M // tm, N // tn),
        in_specs=[pl.BlockSpec((K, tm), lambda i, j: (0, i)), pl.BlockSpec((K, tn), lambda i, j: (0, j)),
                  pl.BlockSpec(memory_space=pl.ANY)],
        out_specs=pl.BlockSpec((None, tm, tn), where), out_shape=jax.ShapeDtypeStruct(buf.shape, buf.dtype),
        input_output_aliases={2: 0}, compiler_params=_params("parallel", "parallel"),
    )(a, b, buf)


def _rms_fwd(name, h, g):
    L, D = h.shape
    tr = _tile(L, (256, 128))

    def body(h_ref, g_ref, o_ref):
        x = h_ref[...]
        r = lax.rsqrt(jnp.mean(x * x, axis=-1, keepdims=True) + RMS_EPS)
        o_ref[...] = (x * r * g_ref[...]).astype(o_ref.dtype)

    return pl.pallas_call(
        body, name=name, grid=(L // tr,),
        in_specs=[pl.BlockSpec((tr, D), lambda i: (i, 0)), pl.BlockSpec((1, D), lambda i: (0, 0))],
        out_specs=pl.BlockSpec((tr, D), lambda i: (i, 0)),
        out_shape=jax.ShapeDtypeStruct((L, D), BF16),
        compiler_params=_params("parallel"),
    )(h, g.reshape(1, D))


def _rms_bwd(name, h, g, dhn, dres):
    L, D = h.shape
    tr = _tile(L, (256, 128))

    def body(h_ref, g_ref, dhn_ref, dres_ref, dh_ref, dg_ref):
        x = h_ref[...]
        r = lax.rsqrt(jnp.mean(x * x, axis=-1, keepdims=True) + RMS_EPS)
        xh = x * r
        dy = dhn_ref[...]
        dxh = dy * g_ref[...]
        dh_ref[...] = dres_ref[...] + r * (dxh - xh * jnp.mean(dxh * xh, axis=-1, keepdims=True))

        @pl.when(pl.program_id(0) == 0)
        def _():
            dg_ref[...] = jnp.zeros_like(dg_ref)

        dg_ref[...] += jnp.sum(dy * xh, axis=0, keepdims=True)

    row = pl.BlockSpec((tr, D), lambda i: (i, 0))
    vec = pl.BlockSpec((1, D), lambda i: (0, 0))
    return pl.pallas_call(
        body, name=name, grid=(L // tr,),
        in_specs=[row, vec, row, row], out_specs=[row, vec],
        out_shape=[jax.ShapeDtypeStruct((L, D), F32), jax.ShapeDtypeStruct((1, D), F32)],
        compiler_params=_params("arbitrary"),
    )(h, g.reshape(1, D), dhn, dres)


def _loss_head(h, g, target):
    L, D = h.shape
    tr = _tile(L, (256, 128))

    def body(h_ref, g_ref, t_ref, loss_ref, dh_ref, dg_ref):
        x = h_ref[...]
        r = lax.rsqrt(jnp.mean(x * x, axis=-1, keepdims=True) + RMS_EPS)
        xh = x * r
        err = xh * g_ref[...] - t_ref[...]
        dy = err * (1.0 / D)
        dxh = dy * g_ref[...]
        dh_ref[...] = r * (dxh - xh * jnp.mean(dxh * xh, axis=-1, keepdims=True))

        @pl.when(pl.program_id(0) == 0)
        def _():
            dg_ref[...] = jnp.zeros_like(dg_ref)
            loss_ref[...] = jnp.zeros_like(loss_ref)

        dg_ref[...] += jnp.sum(dy * xh, axis=0, keepdims=True)
        loss_ref[...] += (0.5 / D) * jnp.sum(jnp.sum(err * err, axis=-1, keepdims=True), axis=0, keepdims=True)

    row = pl.BlockSpec((tr, D), lambda i: (i, 0))
    vec = pl.BlockSpec((1, D), lambda i: (0, 0))
    return pl.pallas_call(
        body, name="loss_head", grid=(L // tr,),
        in_specs=[row, vec, row], out_specs=[pl.BlockSpec((1, 1), lambda i: (0, 0)), row, vec],
        out_shape=[jax.ShapeDtypeStruct((1, 1), F32), jax.ShapeDtypeStruct((L, D), F32), jax.ShapeDtypeStruct((1, D), F32)],
        compiler_params=_params("arbitrary"),
    )(h, g.reshape(1, D), target)


def _glu_fwd(h, ag):
    L, D = h.shape
    tr = _tile(L, (256, 128))

    def body(h_ref, v_ref, g_ref, o_ref):
        o_ref[...] = h_ref[...] + v_ref[...] * _sigmoid(g_ref[...])

    return pl.pallas_call(
        body, name="s5_glu_fwd", grid=(L // tr,),
        in_specs=[pl.BlockSpec((tr, D), lambda i: (i, 0)), pl.BlockSpec((tr, D), lambda i: (i, 0)),
                  pl.BlockSpec((tr, D), lambda i: (i, 1))],
        out_specs=pl.BlockSpec((tr, D), lambda i: (i, 0)),
        out_shape=jax.ShapeDtypeStruct((L, D), F32),
        compiler_params=_params("parallel"),
    )(h, ag, ag)


def _glu_bwd(dh, ag):
    L, D = dh.shape
    tr = _tile(L, (256, 128))

    def body(dh_ref, v_ref, g_ref, dv_ref, dg_ref):
        s = _sigmoid(g_ref[...])
        d = dh_ref[...]
        dv_ref[...] = d * s
        dg_ref[...] = d * v_ref[...] * s * (1.0 - s)

    dv, dg = pl.pallas_call(
        body, name="s5_glu_bwd", grid=(L // tr,),
        in_specs=[pl.BlockSpec((tr, D), lambda i: (i, 0)), pl.BlockSpec((tr, D), lambda i: (i, 0)),
                  pl.BlockSpec((tr, D), lambda i: (i, 1))],
        out_specs=[pl.BlockSpec((tr, D), lambda i: (i, 0))] * 2,
        out_shape=[jax.ShapeDtypeStruct((L, D), F32)] * 2,
        compiler_params=_params("parallel"),
    )(dh, ag, ag)
    return jnp.concatenate([dv, dg], axis=1).astype(BF16)


CONV_ROWS = 128
CONV_COLS = 512


def _shift_rows(cat, s):
    if s == 0:
        return cat[SUBLANES:, :]
    return pltpu.roll(cat, s, axis=0)[SUBLANES:, :]


def _conv_fwd(name, p, col0, w, b):
    L = p.shape[0]
    C = w.shape[1]
    tc = _tile(C, (CONV_COLS, 256))
    cb0 = col0 // tc
    nr = L // CONV_ROWS

    def body(x_ref, w_ref, b_ref, o_ref):
        def step(r, carry):
            r0 = pl.multiple_of(r * CONV_ROWS, CONV_ROWS)
            cur = x_ref[pl.ds(r0, CONV_ROWS), :]
            p0 = pl.multiple_of(jnp.maximum(r0 - SUBLANES, 0), SUBLANES)
            prev = jnp.where(r > 0, x_ref[pl.ds(p0, SUBLANES), :], 0.0)
            cat = jnp.concatenate([prev, cur], axis=0)
            acc = b_ref[...] + w_ref[3:4, :] * cur
            for k in range(CONV_W - 1):
                acc = acc + w_ref[k:k + 1, :] * _shift_rows(cat, CONV_W - 1 - k)
            o_ref[pl.ds(r0, CONV_ROWS), :] = acc * _sigmoid(acc)
            return carry

        lax.fori_loop(0, nr, step, 0)

    return pl.pallas_call(
        body, name=name, grid=(C // tc,),
        in_specs=[pl.BlockSpec((L, tc), lambda j: (0, cb0 + j)), pl.BlockSpec((CONV_W, tc), lambda j: (0, j)),
                  pl.BlockSpec((1, tc), lambda j: (0, j))],
        out_specs=pl.BlockSpec((L, tc), lambda j: (0, j)),
        out_shape=jax.ShapeDtypeStruct((L, C), F32),
        compiler_params=_params("parallel"),
    )(p, w, b)


def _conv_bwd(name, p, col0, w, b, dout):
    L = p.shape[0]
    C = w.shape[1]
    tc = _tile(C, (CONV_COLS, 256))
    cb0 = col0 // tc
    nr = L // CONV_ROWS

    def body(x_ref, w_ref, b_ref, do_ref, dx_ref, dw_ref, db_ref, dpre_ref):
        def step1(r, carry):
            dw0, dw1, dw2, dw3, dbb = carry
            r0 = pl.multiple_of(r * CONV_ROWS, CONV_ROWS)
            cur = x_ref[pl.ds(r0, CONV_ROWS), :]
            p0 = pl.multiple_of(jnp.maximum(r0 - SUBLANES, 0), SUBLANES)
            prev = jnp.where(r > 0, x_ref[pl.ds(p0, SUBLANES), :], 0.0)
            cat = jnp.concatenate([prev, cur], axis=0)
            sh = [_shift_rows(cat, CONV_W - 1 - k) for k in range(CONV_W - 1)] + [cur]
            acc = b_ref[...] + w_ref[3:4, :] * cur
            for k in range(CONV_W - 1):
                acc = acc + w_ref[k:k + 1, :] * sh[k]
            sg = _sigmoid(acc)
            dpre = do_ref[pl.ds(r0, CONV_ROWS), :] * (sg + acc * sg * (1.0 - sg))
            dpre_ref[pl.ds(r0, CONV_ROWS), :] = dpre
            dws = [d + jnp.sum(dpre * s, axis=0, keepdims=True) for d, s in zip((dw0, dw1, dw2, dw3), sh)]
            return (*dws, dbb + jnp.sum(dpre, axis=0, keepdims=True))

        z = jnp.zeros((1, tc), F32)
        dw0, dw1, dw2, dw3, dbb = lax.fori_loop(0, nr, step1, (z, z, z, z, z))
        dw_ref[...] = jnp.concatenate([dw0, dw1, dw2, dw3, z, z, z, z], axis=0)
        db_ref[...] = dbb

        def step2(r, carry):
            r0 = pl.multiple_of(r * CONV_ROWS, CONV_ROWS)
            cur = dpre_ref[pl.ds(r0, CONV_ROWS), :]
            n0 = pl.multiple_of(jnp.minimum(r0 + CONV_ROWS, L - SUBLANES), SUBLANES)
            nxt = jnp.where(r < nr - 1, dpre_ref[pl.ds(n0, SUBLANES), :], 0.0)
            cat = jnp.concatenate([cur, nxt], axis=0)
            acc = w_ref[3:4, :] * cur
            for k in range(CONV_W - 1):
                s = CONV_W - 1 - k
                acc = acc + w_ref[k:k + 1, :] * pltpu.roll(cat, CONV_ROWS + SUBLANES - s, axis=0)[:CONV_ROWS, :]
            dx_ref[pl.ds(r0, CONV_ROWS), :] = acc
            return carry

        lax.fori_loop(0, nr, step2, 0)

    dx, dw, db = pl.pallas_call(
        body, name=name, grid=(C // tc,),
        in_specs=[pl.BlockSpec((L, tc), lambda j: (0, cb0 + j)), pl.BlockSpec((CONV_W, tc), lambda j: (0, j)),
                  pl.BlockSpec((1, tc), lambda j: (0, j)), pl.BlockSpec((L, tc), lambda j: (0, j))],
        out_specs=[pl.BlockSpec((L, tc), lambda j: (0, j)), pl.BlockSpec((SUBLANES, tc), lambda j: (0, j)),
                   pl.BlockSpec((1, tc), lambda j: (0, j))],
        out_shape=[jax.ShapeDtypeStruct((L, C), F32), jax.ShapeDtypeStruct((SUBLANES, C), F32),
                   jax.ShapeDtypeStruct((1, C), F32)],
        scratch_shapes=[pltpu.VMEM((L, tc), F32)],
        compiler_params=_params("parallel"),
    )(p, w, b, dout)
    return dx, dw[:CONV_W], db


def _chunk_consts():
    r, c = _iota2((CHUNK, CHUNK), 0), _iota2((CHUNK, CHUNK), 1)
    causal = r >= c
    return causal, r > c, (r == c).astype(F32), causal.astype(F32), jnp.ones((CHUNK, CHUNK), F32)


def _by_lanes(t):
    return jnp.concatenate([t[i] for i in range(t.shape[0])], axis=1)


def _by_batch(t, w):
    return jnp.concatenate([t[None, :, i * w:(i + 1) * w] for i in range(t.shape[1] // w)], axis=0)


def _diag_lanes():
    return (_iota2((CHUNK, LANES), 0) == _iota2((CHUNK, LANES), 1)).astype(F32)


def _gdn_chunk(q, k, v, ab, gate, S, alog, dtb, og, ea, eb, sol=None):
    causal, strict, _, tril, ones = _chunk_consts()
    logits = _by_batch(_pick(ab, jnp.concatenate([_by_lanes(ea), _by_lanes(eb)], axis=1)), LANES)
    H = q.shape[0]
    g = -jnp.exp(alog) * _softplus(logits[:H] + dtb)
    beta = _sigmoid(logits[H:])
    qn = q * lax.rsqrt(jnp.sum(q * q, axis=-1, keepdims=True) + 1e-6) * (GDN_DK ** -0.5)
    kn = k * lax.rsqrt(jnp.sum(k * k, axis=-1, keepdims=True) + 1e-6)
    g_l = _by_lanes(g)
    gc = _by_batch(_accum(tril, g_l), LANES)
    glast = _by_batch(_accum(ones, g_l), LANES)
    gcol = gc[:, :, :CHUNK]
    grow = _by_batch(_accum(ones, _by_lanes(gc * _diag_lanes())), LANES)[:, :, :CHUNK]
    decay = jnp.exp(jnp.where(causal, gcol - grow, -jnp.inf))
    a = jnp.where(strict, beta[:, :, :CHUNK] * _nt(kn, kn) * decay, 0.0)
    eg = jnp.exp(gc)
    rhs = jnp.concatenate([v * beta, kn * (beta * eg)], axis=2)
    sol = _unit_lower_solve(a, rhs) if sol is None else _unit_lower_solved(a, rhs, sol)
    u, w = sol[:, :, :GDN_DK], sol[:, :, GDN_DK:]
    qk = _nt(qn, kn) * decay
    v_new = u - _dot(w, S)
    o = _dot(qn * eg, S) + _dot(qk, v_new)
    cd = jnp.exp(glast)
    s_new = jnp.concatenate([cd, cd], axis=1) * S + _tn(kn * jnp.exp(glast - gc), v_new)
    on = o * lax.rsqrt(jnp.mean(o * o, axis=-1, keepdims=True) + RMS_EPS) * og
    return on * (gate * _sigmoid(gate)), s_new, sol


GDN_HB = 8


def _gdn_specs(nc, rev):
    cm = (lambda c: nc - 1 - c) if rev else (lambda c: c)
    blk = lambda off: pl.BlockSpec((CHUNK, GDN_HB * GDN_DK), lambda c, h: (cm(c), off // GDN_HB + h))
    ab = pl.BlockSpec((CHUNK, LANES), lambda c, h: (cm(c), (GDN_IN_PAD - LANES) // LANES))
    hv = pl.BlockSpec((GDN_HB, 1, LANES), lambda c, h: (h, 0, 0))
    og = pl.BlockSpec((1, LANES), lambda c, h: (0, 0))
    em = pl.BlockSpec((GDN_HB, LANES, LANES), lambda c, h: (h, 0, 0))
    st = pl.BlockSpec((None, GDN_HB, GDN_DK, GDN_DK), lambda c, h: (cm(c), h, 0, 0))
    sl = pl.BlockSpec((None, GDN_HB, CHUNK, 2 * GDN_DK), lambda c, h: (cm(c), h, 0, 0))
    return blk, ab, hv, og, em, st, sl


def _gdn_fwd(qc, kc, vc, p, alog_e, dtb_e, og, ea, eb):
    L = qc.shape[0]
    nc = L // CHUNK
    blk, ab, hv, ogs, em, st, sl = _gdn_specs(nc, False)

    def body(q_ref, k_ref, v_ref, gate_ref, ab_ref, al_ref, dt_ref, og_ref, ea_ref, eb_ref, y_ref, sp_ref, sol_ref, s_scr):
        c, h = pl.program_id(0), pl.program_id(1)
        lanes = [slice(i * GDN_DK, (i + 1) * GDN_DK) for i in range(GDN_HB)]
        heads = pl.ds(h * GDN_HB, GDN_HB)
        stack = lambda ref: jnp.concatenate([ref[:, ls][None] for ls in lanes], axis=0)

        @pl.when(c == 0)
        def _():
            s_scr[heads] = jnp.zeros((GDN_HB, GDN_DK, GDN_DK), F32)

        S = s_scr[heads]
        sp_ref[...] = S
        y, s_new, sol = _gdn_chunk(stack(q_ref), stack(k_ref), stack(v_ref), ab_ref[...], stack(gate_ref), S,
                                   al_ref[...], dt_ref[...], og_ref[...], ea_ref[...], eb_ref[...])
        for i, ls in enumerate(lanes):
            y_ref[:, ls] = y[i]
        s_scr[heads] = s_new
        sol_ref[...] = sol

    return pl.pallas_call(
        body, name="gdn_fwd", grid=(nc, GDN_HEADS // GDN_HB),
        in_specs=[blk(0), blk(0), blk(0), blk(3 * GDN_HEADS), ab, hv, hv, ogs, em, em],
        out_specs=[blk(0), st, sl],
        out_shape=[jax.ShapeDtypeStruct((L, D_MODEL), F32), jax.ShapeDtypeStruct((nc, GDN_HEADS, GDN_DK, GDN_DK), F32),
                   jax.ShapeDtypeStruct((nc, GDN_HEADS, CHUNK, 2 * GDN_DK), F32)],
        scratch_shapes=[pltpu.VMEM((GDN_HEADS, GDN_DK, GDN_DK), F32)],
        compiler_params=_params("arbitrary", "arbitrary"),
    )(qc, kc, vc, p, p, alog_e, dtb_e, og, ea, eb)


def _gdn_bwd(qc, kc, vc, p, alog_e, dtb_e, og, ea, eb, sprev, sol, dy):
    L = qc.shape[0]
    nc = L // CHUNK
    blk, ab, hv, ogs, em, st, sl = _gdn_specs(nc, True)

    def body(q_ref, k_ref, v_ref, gate_ref, ab_ref, al_ref, dt_ref, og_ref, ea_ref, eb_ref, sp_ref, sol_ref, dy_ref,
             dq_ref, dk_ref, dv_ref, dgate_ref, dab_ref, dpar_ref, ds_scr):
        c, h = pl.program_id(0), pl.program_id(1)
        lanes = [slice(i * GDN_DK, (i + 1) * GDN_DK) for i in range(GDN_HB)]
        heads = pl.ds(h * GDN_HB, GDN_HB)
        stack = lambda ref: jnp.concatenate([ref[:, ls][None] for ls in lanes], axis=0)

        @pl.when(c == 0)
        def _():
            ds_scr[heads] = jnp.zeros((GDN_HB, GDN_DK, GDN_DK), F32)
            dpar_ref[heads] = jnp.zeros((GDN_HB, SUBLANES, LANES), F32)

        @pl.when(h == 0)
        def _():
            dab_ref[...] = jnp.zeros_like(dab_ref)

        ea_m, eb_m, sol_m = ea_ref[...], eb_ref[...], sol_ref[...]
        f = lambda q, k, v, a_b, gate, S, al, dt, o_g: _gdn_chunk(q, k, v, a_b, gate, S, al, dt, o_g, ea_m, eb_m, sol_m)[:2]
        _, vjp = jax.vjp(f, stack(q_ref), stack(k_ref), stack(v_ref), ab_ref[...], stack(gate_ref), sp_ref[...],
                         al_ref[...], dt_ref[...], og_ref[...])
        dq, dk, dv, dab, dgate, ds, dal, ddt, dog = vjp((stack(dy_ref), ds_scr[heads]))
        for i, ls in enumerate(lanes):
            dq_ref[:, ls] = dq[i]
            dk_ref[:, ls] = dk[i]
            dv_ref[:, ls] = dv[i]
            dgate_ref[:, ls] = dgate[i]
        ds_scr[heads] = ds
        dab_ref[...] += dab
        first = _iota2((GDN_HB, 1, LANES), 0) == 0
        dpar_ref[heads] += jnp.concatenate([dal, ddt, jnp.where(first, dog[None], 0.0),
                                            jnp.zeros((GDN_HB, SUBLANES - 3, LANES), F32)], axis=1)

    return pl.pallas_call(
        body, name="gdn_bwd", grid=(nc, GDN_HEADS // GDN_HB),
        in_specs=[blk(0), blk(0), blk(0), blk(3 * GDN_HEADS), ab, hv, hv, ogs, em, em, st, sl, blk(0)],
        out_specs=[blk(0), blk(0), blk(0), blk(0), pl.BlockSpec((CHUNK, LANES), lambda c, h: (nc - 1 - c, 0)),
                   pl.BlockSpec((GDN_HEADS, SUBLANES, LANES), lambda c, h: (0, 0, 0))],
        out_shape=[jax.ShapeDtypeStruct((L, D_MODEL), F32)] * 4
        + [jax.ShapeDtypeStruct((L, LANES), F32), jax.ShapeDtypeStruct((GDN_HEADS, SUBLANES, LANES), F32)],
        scratch_shapes=[pltpu.VMEM((GDN_HEADS, GDN_DK, GDN_DK), F32)],
        compiler_params=_params("arbitrary", "arbitrary"),
    )(qc, kc, vc, p, p, alog_e, dtb_e, og, ea, eb, sprev, sol, dy)


def _gdn_selectors():
    rows = np.arange(LANES)[None, :, None]
    heads = np.arange(GDN_HEADS)[:, None, None]
    ea = np.broadcast_to(rows == heads, (GDN_HEADS, LANES, LANES)).astype(np.float32)
    eb = np.broadcast_to(rows == heads + GDN_HEADS, (GDN_HEADS, LANES, LANES)).astype(np.float32)
    return jnp.asarray(ea), jnp.asarray(eb)


M2_GW = M2_INNER // M2_GROUPS
M2_HPG = M2_HEADS // M2_GROUPS
M2_HD = M2_INNER // M2_HEADS


def _m2_chunk(x, bm, cm, z, dtr, st, dtb, alog, dsk, ng, e, ecol):
    G = x.shape[0]
    causal, _, _, tril, ones = _chunk_consts()
    dt_n = _softplus(dtr + dtb)
    da_n = dt_n * (-jnp.exp(alog))
    cum_n = _accum(tril, da_n)
    tot_n = _accum(ones, da_n)
    wide = _pick(jnp.concatenate([dt_n, cum_n, tot_n], axis=0), e)
    dt_w, cum_w, tot_w = (_by_batch(wide[i * CHUNK:(i + 1) * CHUNK], M2_GW) for i in range(3))
    xdt = x * dt_w
    cb = _nt(cm, bm)
    heads = lambda t: jnp.concatenate([t[i:i + 1] for i in range(G) for _ in range(M2_HPG)], axis=0)
    colb = _by_batch(_pick(cum_n, ecol), LANES)
    rowb = _by_batch(_accum(ones, _by_lanes(colb * _diag_lanes())), LANES)
    lmat = jnp.exp(jnp.where(causal, colb[:, :, :CHUNK] - rowb[:, :, :CHUNK], -jnp.inf))
    yr = _dot(heads(cb) * lmat, heads(xdt))
    head = _iota2((CHUNK, M2_GW), 1) // M2_HD
    ydiag = jnp.concatenate([sum(jnp.where(head == r, yr[i * M2_HPG + r], 0.0) for r in range(M2_HPG))[None] for i in range(G)], axis=0)
    st_new = _tn(bm, xdt * jnp.exp(tot_w - cum_w))
    cd = jnp.exp(tot_w)
    s_new = jnp.concatenate([cd, cd], axis=1) * st + st_new
    y = ydiag + _dot(cm, st) * jnp.exp(cum_w) + dsk * x
    y = y * (z * _sigmoid(z))
    yn = y * lax.rsqrt(jnp.mean(y * y, axis=-1, keepdims=True) + RMS_EPS) * ng
    return yn, s_new


M2_GB = 8


def _m2_specs(nc, rev):
    cm = (lambda c: nc - 1 - c) if rev else (lambda c: c)
    wide = lambda off: pl.BlockSpec((CHUNK, M2_GB * M2_GW), lambda c, g: (cm(c), off // M2_GB + g))
    nar = lambda off: pl.BlockSpec((CHUNK, M2_GB * LANES), lambda c, g: (cm(c), off // M2_GB + g))
    dts = pl.BlockSpec((CHUNK, LANES), lambda c, g: (cm(c), (M2_IN_PAD - LANES) // LANES))
    v128 = pl.BlockSpec((1, LANES), lambda c, g: (0, 0))
    v256 = pl.BlockSpec((1, M2_GB * M2_GW), lambda c, g: (0, g))
    es = pl.BlockSpec((LANES, M2_GB * M2_GW), lambda c, g: (0, g))
    ecs = pl.BlockSpec((LANES, M2_GB * M2_HPG * LANES), lambda c, g: (0, g))
    st = pl.BlockSpec((None, M2_GB, M2_STATE, M2_GW), lambda c, g: (cm(c), g, 0, 0))
    return wide, nar, dts, v128, v256, es, ecs, st


def _m2_fwd(xbc, p, dtb, alog, dsk, ng, e, ecol):
    L = xbc.shape[0]
    nc = L // CHUNK
    wide, nar, dts, v128, v256, es, ecs, st = _m2_specs(nc, False)

    def body(x_ref, b_ref, c_ref, z_ref, dt_ref, dtb_ref, al_ref, dsk_ref, ng_ref, e_ref, ec_ref, y_ref, sp_ref, s_scr):
        c, g = pl.program_id(0), pl.program_id(1)
        wide_l = [slice(i * M2_GW, (i + 1) * M2_GW) for i in range(M2_GB)]
        nar_l = [slice(i * LANES, (i + 1) * LANES) for i in range(M2_GB)]
        groups = pl.ds(g * M2_GB, M2_GB)
        wide_s = lambda ref: jnp.concatenate([ref[:, ls][None] for ls in wide_l], axis=0)
        nar_s = lambda ref: jnp.concatenate([ref[:, ls][None] for ls in nar_l], axis=0)

        @pl.when(c == 0)
        def _():
            s_scr[groups] = jnp.zeros((M2_GB, M2_STATE, M2_GW), F32)

        S = s_scr[groups]
        sp_ref[...] = S
        y, s_new = _m2_chunk(wide_s(x_ref), nar_s(b_ref), nar_s(c_ref), wide_s(z_ref), dt_ref[...], S, dtb_ref[...], al_ref[...],
                             wide_s(dsk_ref), wide_s(ng_ref), e_ref[...], ec_ref[...])
        for i, ls in enumerate(wide_l):
            y_ref[:, ls] = y[i]
        s_scr[groups] = s_new

    return pl.pallas_call(
        body, name="m2_fwd", grid=(nc, M2_GROUPS // M2_GB),
        in_specs=[wide(0), nar(2 * M2_GROUPS), nar(3 * M2_GROUPS), wide(0), dts, v128, v128, v256, v256, es, ecs],
        out_specs=[wide(0), st],
        out_shape=[jax.ShapeDtypeStruct((L, M2_INNER), F32), jax.ShapeDtypeStruct((nc, M2_GROUPS, M2_STATE, M2_GW), F32)],
        scratch_shapes=[pltpu.VMEM((M2_GROUPS, M2_STATE, M2_GW), F32)],
        compiler_params=_params("arbitrary", "arbitrary"),
    )(xbc, xbc, xbc, p, p, dtb, alog, dsk, ng, e, ecol)


def _m2_bwd(xbc, p, dtb, alog, dsk, ng, e, ecol, sprev, dy):
    L = xbc.shape[0]
    nc = L // CHUNK
    wide, nar, dts, v128, v256, es, ecs, st = _m2_specs(nc, True)

    def body(x_ref, b_ref, c_ref, z_ref, dt_ref, dtb_ref, al_ref, dsk_ref, ng_ref, e_ref, ec_ref, sp_ref, dy_ref,
             dx_ref, db_ref, dc_ref, dz_ref, ddt_ref, dnar_ref, dwide_ref, ds_scr):
        c, g = pl.program_id(0), pl.program_id(1)
        wide_l = [slice(i * M2_GW, (i + 1) * M2_GW) for i in range(M2_GB)]
        nar_l = [slice(i * LANES, (i + 1) * LANES) for i in range(M2_GB)]
        groups = pl.ds(g * M2_GB, M2_GB)
        wide_s = lambda ref: jnp.concatenate([ref[:, ls][None] for ls in wide_l], axis=0)
        nar_s = lambda ref: jnp.concatenate([ref[:, ls][None] for ls in nar_l], axis=0)

        @pl.when(jnp.logical_and(c == 0, g == 0))
        def _():
            dnar_ref[...] = jnp.zeros_like(dnar_ref)

        @pl.when(c == 0)
        def _():
            ds_scr[groups] = jnp.zeros((M2_GB, M2_STATE, M2_GW), F32)
            dwide_ref[groups] = jnp.zeros((M2_GB, SUBLANES, M2_GW), F32)

        @pl.when(g == 0)
        def _():
            ddt_ref[...] = jnp.zeros_like(ddt_ref)

        e_m, ec_m = e_ref[...], ec_ref[...]
        f = lambda x, bm, cm, z, dtr, S, dtb, al, dsk, ng: _m2_chunk(x, bm, cm, z, dtr, S, dtb, al, dsk, ng, e_m, ec_m)
        _, vjp = jax.vjp(f, wide_s(x_ref), nar_s(b_ref), nar_s(c_ref), wide_s(z_ref), dt_ref[...], sp_ref[...], dtb_ref[...],
                         al_ref[...], wide_s(dsk_ref), wide_s(ng_ref))
        dx, db, dc, dz, ddt, ds, ddtb, dal, ddsk, dng = vjp((wide_s(dy_ref), ds_scr[groups]))
        for i in range(M2_GB):
            dx_ref[:, wide_l[i]] = dx[i]
            db_ref[:, nar_l[i]] = db[i]
            dc_ref[:, nar_l[i]] = dc[i]
            dz_ref[:, wide_l[i]] = dz[i]
        ds_scr[groups] = ds
        ddt_ref[...] += ddt
        dnar_ref[...] += jnp.concatenate([ddtb, dal, jnp.zeros((SUBLANES - 2, LANES), F32)], axis=0)
        dwide_ref[groups] += jnp.concatenate([ddsk, dng, jnp.zeros((M2_GB, SUBLANES - 2, M2_GW), F32)], axis=1)

    return pl.pallas_call(
        body, name="m2_bwd", grid=(nc, M2_GROUPS // M2_GB),
        in_specs=[wide(0), nar(2 * M2_GROUPS), nar(3 * M2_GROUPS), wide(0), dts, v128, v128, v256, v256, es, ecs, st, wide(0)],
        out_specs=[wide(0), nar(0), nar(0), wide(0), pl.BlockSpec((CHUNK, LANES), lambda c, g: (nc - 1 - c, 0)),
                   pl.BlockSpec((SUBLANES, LANES), lambda c, g: (0, 0)),
                   pl.BlockSpec((M2_GROUPS, SUBLANES, M2_GW), lambda c, g: (0, 0, 0))],
        out_shape=[jax.ShapeDtypeStruct((L, M2_INNER), F32), jax.ShapeDtypeStruct((L, M2_GROUPS * M2_STATE), F32),
                   jax.ShapeDtypeStruct((L, M2_GROUPS * M2_STATE), F32), jax.ShapeDtypeStruct((L, M2_INNER), F32),
                   jax.ShapeDtypeStruct((L, LANES), F32), jax.ShapeDtypeStruct((SUBLANES, LANES), F32),
                   jax.ShapeDtypeStruct((M2_GROUPS, SUBLANES, M2_GW), F32)],
        scratch_shapes=[pltpu.VMEM((M2_GROUPS, M2_STATE, M2_GW), F32)],
        compiler_params=_params("arbitrary", "arbitrary"),
    )(xbc, xbc, xbc, p, p, dtb, alog, dsk, ng, e, ecol, sprev, dy)


def _m2_selectors():
    e = np.zeros((LANES, M2_INNER), np.float32)
    ecol = np.zeros((LANES, M2_HEADS * LANES), np.float32)
    for h in range(M2_HEADS):
        e[h, M2_HD * h:M2_HD * (h + 1)] = 1.0
        ecol[h, LANES * h:LANES * (h + 1)] = 1.0
    return jnp.asarray(e), jnp.asarray(ecol)


S5_NS = S5_GROUPS * S5_STATE // S5_BLOCKS
S5_ROWS = 256
GELU_C = math.sqrt(2.0 / math.pi)


def _gelu(x):
    return 0.5 * x * (1.0 + jnp.tanh(GELU_C * (x + 0.044715 * x * x * x)))


def _gelu_grad(x):
    t = jnp.tanh(GELU_C * (x + 0.044715 * x * x * x))
    return 0.5 * (1.0 + t) + 0.5 * x * (1.0 - t * t) * GELU_C * (1.0 + 3.0 * 0.044715 * x * x)


def _s5_scan(re_ref, im_ref, pw_re, pw_im, nrows, reverse, states=None):
    n = re_ref.shape[1]
    row = _iota2((SUBLANES, n), 0)
    steps = []
    for d in (1, 2, 4):
        keep = (row < SUBLANES - d) if reverse else (row >= d)
        steps.append(((SUBLANES - d) if reverse else d, jnp.where(keep, pw_re[d - 1:d, :], 0.0), jnp.where(keep, pw_im[d - 1:d, :], 0.0)))
    if reverse:
        cw_re = jnp.concatenate([pw_re[SUBLANES - 1 - k:SUBLANES - k, :] for k in range(SUBLANES)], axis=0)
        cw_im = jnp.concatenate([pw_im[SUBLANES - 1 - k:SUBLANES - k, :] for k in range(SUBLANES)], axis=0)
    else:
        cw_re, cw_im = pw_re, pw_im
    edge = 0 if reverse else SUBLANES - 1
    ngroups = nrows // SUBLANES

    def step(i, carry):
        cr, ci, ar, ai = carry
        gi = (ngroups - 1 - i) if reverse else i
        r0 = pl.multiple_of(gi * SUBLANES, SUBLANES)
        xr, xi = re_ref[pl.ds(r0, SUBLANES), :], im_ref[pl.ds(r0, SUBLANES), :]
        for shift, pr, pi in steps:
            sr, si = pltpu.roll(xr, shift, axis=0), pltpu.roll(xi, shift, axis=0)
            xr, xi = xr + (pr * sr - pi * si), xi + (pr * si + pi * sr)
        xr, xi = xr + (cw_re * cr - cw_im * ci), xi + (cw_re * ci + cw_im * cr)
        re_ref[pl.ds(r0, SUBLANES), :] = xr
        im_ref[pl.ds(r0, SUBLANES), :] = xi
        if states is not None:
            p0 = pl.multiple_of(jnp.maximum(r0 - SUBLANES, 0), SUBLANES)
            live = jnp.where(gi > 0, 1.0, 0.0)
            prev = [jnp.where(row >= 1, pltpu.roll(ref[pl.ds(r0, SUBLANES), :], 1, axis=0),
                              live * pltpu.roll(ref[pl.ds(p0, SUBLANES), :], 1, axis=0)) for ref in states]
            ar, ai = ar + (prev[0] * xr + prev[1] * xi), ai + (prev[0] * xi - prev[1] * xr)
        return (jnp.sum(jnp.where(row == edge, xr, 0.0), axis=0, keepdims=True),
                jnp.sum(jnp.where(row == edge, xi, 0.0), axis=0, keepdims=True), ar, ai)

    z = jnp.zeros((1, n), F32)
    za = jnp.zeros((SUBLANES, n) if states is not None else (1, n), F32)
    _, _, ar, ai = lax.fori_loop(0, ngroups, step, (z, z, za, za))
    return jnp.sum(ar, axis=0, keepdims=True), jnp.sum(ai, axis=0, keepdims=True)


def _s5_project_in(u_ref, bm_ref, re_ref, im_ref, L):
    def step(i, carry):
        r0 = pl.multiple_of(i * S5_ROWS, S5_ROWS)
        bu = _dot(u_ref[pl.ds(r0, S5_ROWS), :], bm_ref[...])
        re_ref[pl.ds(r0, S5_ROWS), :] = bu[:, :S5_NS]
        im_ref[pl.ds(r0, S5_ROWS), :] = bu[:, S5_NS:]
        return carry

    lax.fori_loop(0, L // S5_ROWS, step, 0)


def _s5_specs(L):
    col = pl.BlockSpec((L, LANES), lambda j: (0, j))
    bm = pl.BlockSpec((None, LANES, 2 * S5_NS), lambda j: (j, 0, 0))
    cm = pl.BlockSpec((None, 2 * S5_NS, LANES), lambda j: (j, 0, 0))
    pw = pl.BlockSpec((None, SUBLANES, S5_NS), lambda j: (j, 0, 0))
    vec = pl.BlockSpec((1, LANES), lambda j: (0, j))
    return col, bm, cm, pw, vec


def _s5_fwd(u, bmat, cmat, pw_re, pw_im, dsk):
    L = u.shape[0]
    col, bm, cm, pw, vec = _s5_specs(L)

    def body(u_ref, bm_ref, cm_ref, pr_ref, pi_ref, d_ref, y_ref, re_scr, im_scr):
        _s5_project_in(u_ref, bm_ref, re_scr, im_scr, L)
        _s5_scan(re_scr, im_scr, pr_ref[...], pi_ref[...], L, False)

        def step(i, carry):
            r0 = pl.multiple_of(i * S5_ROWS, S5_ROWS)
            rows = pl.ds(r0, S5_ROWS)
            y = _dot(re_scr[rows, :], cm_ref[:S5_NS, :]) + _dot(im_scr[rows, :], cm_ref[S5_NS:, :]) + d_ref[...] * u_ref[rows, :]
            y_ref[rows, :] = _gelu(y)
            return carry

        lax.fori_loop(0, L // S5_ROWS, step, 0)

    return pl.pallas_call(
        body, name="s5_fwd", grid=(S5_BLOCKS,),
        in_specs=[col, bm, cm, pw, pw, vec], out_specs=col,
        out_shape=jax.ShapeDtypeStruct((L, D_MODEL), F32),
        scratch_shapes=[pltpu.VMEM((L, S5_NS), F32)] * 2,
        compiler_params=_params("parallel"),
    )(u, bmat, cmat, pw_re, pw_im, dsk)


def _s5_bwd(u, bmat, cmat, pw_re, pw_im, dsk, dyg):
    L = u.shape[0]
    col, bm, cm, pw, vec = _s5_specs(L)

    def body(u_ref, bm_ref, cm_ref, pr_ref, pi_ref, d_ref, dy_ref, du_ref, dbm_ref, dcm_ref, dlam_ref, dd_ref,
             re_scr, im_scr, gr_scr, gi_scr, dyp_scr):
        _s5_project_in(u_ref, bm_ref, re_scr, im_scr, L)
        _s5_scan(re_scr, im_scr, pr_ref[...], pi_ref[...], L, False)

        def step(i, carry):
            dcr, dci, dd = carry
            r0 = pl.multiple_of(i * S5_ROWS, S5_ROWS)
            rows = pl.ds(r0, S5_ROWS)
            sr, si, uu = re_scr[rows, :], im_scr[rows, :], u_ref[rows, :]
            y = _dot(sr, cm_ref[:S5_NS, :]) + _dot(si, cm_ref[S5_NS:, :]) + d_ref[...] * uu
            dyp = dy_ref[rows, :] * _gelu_grad(y)
            dyp_scr[rows, :] = dyp
            gr_scr[rows, :] = _nt(dyp, cm_ref[:S5_NS, :])
            gi_scr[rows, :] = _nt(dyp, cm_ref[S5_NS:, :])
            return dcr + _tn(sr, dyp), dci + _tn(si, dyp), dd + jnp.sum(dyp * uu, axis=0, keepdims=True)

        zc = jnp.zeros((S5_NS, LANES), F32)
        dcr, dci, dd = lax.fori_loop(0, L // S5_ROWS, step, (zc, zc, jnp.zeros((1, LANES), F32)))
        dcm_ref[:S5_NS, :] = dcr
        dcm_ref[S5_NS:, :] = dci
        dd_ref[...] = dd

        ar, ai = _s5_scan(gr_scr, gi_scr, pr_ref[...], -pi_ref[...], L, True, states=(re_scr, im_scr))
        dlam_ref[...] = jnp.concatenate([ar, ai, jnp.zeros((SUBLANES - 2, S5_NS), F32)], axis=0)

        def in_step(i, carry):
            dbr, dbi = carry
            r0 = pl.multiple_of(i * S5_ROWS, S5_ROWS)
            rows = pl.ds(r0, S5_ROWS)
            gr, gi, uu = gr_scr[rows, :], gi_scr[rows, :], u_ref[rows, :]
            du_ref[rows, :] = dyp_scr[rows, :] * d_ref[...] + _nt(gr, bm_ref[:, :S5_NS]) + _nt(gi, bm_ref[:, S5_NS:])
            return dbr + _tn(uu, gr), dbi + _tn(uu, gi)

        zb = jnp.zeros((LANES, S5_NS), F32)
        dbr, dbi = lax.fori_loop(0, L // S5_ROWS, in_step, (zb, zb))
        dbm_ref[:, :S5_NS] = dbr
        dbm_ref[:, S5_NS:] = dbi

    return pl.pallas_call(
        body, name="s5_bwd", grid=(S5_BLOCKS,),
        in_specs=[col, bm, cm, pw, pw, vec, col], out_specs=[col, bm, cm, pw, vec],
        out_shape=[jax.ShapeDtypeStruct((L, D_MODEL), F32), jax.ShapeDtypeStruct((S5_BLOCKS, LANES, 2 * S5_NS), F32),
                   jax.ShapeDtypeStruct((S5_BLOCKS, 2 * S5_NS, LANES), F32),
                   jax.ShapeDtypeStruct((S5_BLOCKS, SUBLANES, S5_NS), F32), jax.ShapeDtypeStruct((1, D_MODEL), F32)],
        scratch_shapes=[pltpu.VMEM((L, S5_NS), F32)] * 4 + [pltpu.VMEM((L, LANES), F32)],
        compiler_params=_params("parallel"),
    )(u, bmat, cmat, pw_re, pw_im, dsk, dyg)


def _s5_discretize(lam_re, lam_im, log_dt, b_re, b_im, e16):
    dt = jnp.exp(log_dt)
    zr, zi = lam_re * dt, lam_im * dt
    mag = jnp.exp(zr)
    lbr, lbi = mag * jnp.cos(zi), mag * jnp.sin(zi)
    den = lam_re * lam_re + lam_im * lam_im
    nr, ni = lbr - 1.0, lbi
    cr = (nr * lam_re + ni * lam_im) / den
    ci = (ni * lam_re - nr * lam_im) / den
    crw, ciw = _pick(cr, e16), _pick(ci, e16)
    return lbr, lbi, crw * b_re - ciw * b_im, crw * b_im + ciw * b_re


def _s5_params_fwd(lam_re, lam_im, log_dt, b_re, b_im, e16):
    def body(lr, li, ld, br, bi, e, o1, o2, o3, o4):
        for o, val in zip((o1, o2, o3, o4), _s5_discretize(lr[...], li[...], ld[...], br[...], bi[...], e[...])):
            o[...] = val

    g, p, n = S5_GROUPS, S5_STATE, S5_STATE * S5_GROUP
    return pl.pallas_call(
        body, name="s5_params_fwd",
        out_shape=[jax.ShapeDtypeStruct((g, p), F32)] * 2 + [jax.ShapeDtypeStruct((g, n), F32)] * 2,
        compiler_params=_params(),
    )(lam_re, lam_im, log_dt, b_re, b_im, e16)


def _s5_params_bwd(lam_re, lam_im, log_dt, b_re, b_im, e16, cts):
    def body(lr, li, ld, br, bi, e, c1, c2, c3, c4, o1, o2, o3, o4, o5):
        e_m = e[...]
        f = lambda a, b, c, d, g: _s5_discretize(a, b, c, d, g, e_m)
        _, vjp = jax.vjp(f, lr[...], li[...], ld[...], br[...], bi[...])
        for o, val in zip((o1, o2, o3, o4, o5), vjp((c1[...], c2[...], c3[...], c4[...]))):
            o[...] = val

    g, p, n = S5_GROUPS, S5_STATE, S5_STATE * S5_GROUP
    return pl.pallas_call(
        body, name="s5_params_bwd",
        out_shape=[jax.ShapeDtypeStruct((g, p), F32)] * 2 + [jax.ShapeDtypeStruct((g, 1), F32)]
        + [jax.ShapeDtypeStruct((g, n), F32)] * 2,
        compiler_params=_params(),
    )(lam_re, lam_im, log_dt, b_re, b_im, e16, *cts)


def _add_residual(acc, h):
    return (acc + h,)


def _mlp_fwd(i, h, g, w1, w2):
    hn = _rms_fwd(f"mlp{i}_norm", h, g)
    r = _mm(f"mlp{i}_up", hn, w1, "nn", (BF16,), epi=lambda acc: (jnp.square(jnp.maximum(acc, 0.0)),))
    return _mm(f"mlp{i}_down", r, w2, "nn", (F32,), epi=_add_residual, extras=(h,)), (h, hn, r)


def _mlp_bwd(i, dh_out, saved, g, w1, w2, dw):
    h, hn, r = saved
    dw(("mlp_w2", i), f"mlp{i}_dw2", r, dh_out)
    da = _mm(f"mlp{i}_da", dh_out, w2, "nt", (BF16,), epi=lambda acc, rr: (acc * (2.0 * jnp.sqrt(rr.astype(F32))),), extras=(r,))
    dw(("mlp_w1", i), f"mlp{i}_dw1", hn, da)
    dhn = _mm(f"mlp{i}_dhn", da, w1, "nt", (F32,))
    dh, dg = _rms_bwd(f"mlp{i}_dnorm", h, g, dhn, dh_out)
    return dh, dg[0]


def _lanes(v, n):
    return jnp.broadcast_to(v.reshape(n, 1, 1), (n, 1, LANES))


def _gdn_fwd_layer(i, h, g, w_in, conv_w, a_log, dt_bias, o_g, w_out):
    hn = _rms_fwd(f"gdn{i}_norm", h, g)
    p = _mm(f"gdn{i}_in", hn, w_in, "nn", (F32,))
    zb = jnp.zeros((1, D_MODEL), F32)
    qkv = [_conv_fwd(f"gdn{i}_conv{t}", p, t * D_MODEL, conv_w[:, t * D_MODEL:(t + 1) * D_MODEL], zb) for t in range(3)]
    ea, eb = _gdn_selectors()
    y, sprev, sol = _gdn_fwd(*qkv, p, _lanes(a_log, GDN_HEADS), _lanes(dt_bias, GDN_HEADS), o_g.reshape(1, LANES), ea, eb)
    return _mm(f"gdn{i}_out", y, w_out, "nn", (F32,), epi=_add_residual, extras=(h,)), (h, hn, p, qkv, y, sprev, sol)


def _gdn_bwd_layer(i, dh_out, saved, g, w_in, conv_w, a_log, dt_bias, o_g, w_out, dw):
    h, hn, p, qkv, y, sprev, sol = saved
    dy = _mm(f"gdn{i}_dy", dh_out, w_out, "nt", (F32,))
    dw(("gdn_w_out", MIXER_INDEX[i]), f"gdn{i}_dwout", y, dh_out)
    ea, eb = _gdn_selectors()
    dq, dk, dv, dgate, dab, dpar = _gdn_bwd(*qkv, p, _lanes(a_log, GDN_HEADS), _lanes(dt_bias, GDN_HEADS),
                                            o_g.reshape(1, LANES), ea, eb, sprev, sol, dy)
    zb = jnp.zeros((1, D_MODEL), F32)
    dpre, dcw = [], []
    for t, d in enumerate((dq, dk, dv)):
        dx, dwc, _ = _conv_bwd(f"gdn{i}_dconv{t}", p, t * D_MODEL, conv_w[:, t * D_MODEL:(t + 1) * D_MODEL], zb, d)
        dpre.append(dx)
        dcw.append(dwc)
    dp = jnp.concatenate(dpre + [dgate, dab], axis=1).astype(BF16)
    dw(("gdn_w_in", MIXER_INDEX[i]), f"gdn{i}_dwin", hn, dp)
    dhn = _mm(f"gdn{i}_dhn", dp, w_in, "nt", (F32,))
    dh, dg = _rms_bwd(f"gdn{i}_dnorm", h, g, dhn, dh_out)
    grads = dict(conv_w=jnp.concatenate(dcw, axis=1), a_log=jnp.sum(dpar[:, 0, :], axis=-1),
                 dt_bias=jnp.sum(dpar[:, 1, :], axis=-1), o_norm_g=jnp.sum(dpar[:, 2, :], axis=0))
    return dh, dg[0], grads


def _m2_vectors(dt_bias, a_log, d_skip, norm_g):
    pad = lambda v: jnp.pad(v, (0, LANES - M2_HEADS)).reshape(1, LANES)
    return pad(dt_bias), pad(a_log), jnp.repeat(d_skip, M2_HD).reshape(1, M2_INNER), norm_g.reshape(1, M2_INNER)


def _m2_fwd_layer(h, g, w_in, conv_w, conv_b, dt_bias, a_log, d_skip, norm_g, w_out):
    hn = _rms_fwd("m2_norm", h, g)
    p = _mm("m2_in", hn, w_in, "nn", (F32,))
    xbc = _conv_fwd("m2_conv", p, M2_INNER, conv_w, conv_b.reshape(1, M2_CONV_CH))
    e, ecol = _m2_selectors()
    y, sprev = _m2_fwd(xbc, p, *_m2_vectors(dt_bias, a_log, d_skip, norm_g), e, ecol)
    return _mm("m2_out", y, w_out, "nn", (F32,), epi=_add_residual, extras=(h,)), (h, hn, p, xbc, y, sprev)


def _m2_bwd_layer(dh_out, saved, g, w_in, conv_w, conv_b, dt_bias, a_log, d_skip, norm_g, w_out, dw):
    h, hn, p, xbc, y, sprev = saved
    dy = _mm("m2_dy", dh_out, w_out, "nt", (F32,))
    dw(("m2_w_out", 0), "m2_dwout", y, dh_out)
    e, ecol = _m2_selectors()
    dx, db, dc, dz, ddt, dnar, dwide = _m2_bwd(xbc, p, *_m2_vectors(dt_bias, a_log, d_skip, norm_g), e, ecol, sprev, dy)
    dxbc, dcw, dcb = _conv_bwd("m2_dconv", p, M2_INNER, conv_w, conv_b.reshape(1, M2_CONV_CH),
                               jnp.concatenate([dx, db, dc], axis=1))
    dp = jnp.concatenate([dz, dxbc, ddt], axis=1).astype(BF16)
    dw(("m2_w_in", 0), "m2_dwin", hn, dp)
    dhn = _mm("m2_dhn", dp, w_in, "nt", (F32,))
    dh, dg = _rms_bwd("m2_dnorm", h, g, dhn, dh_out)
    grads = dict(conv_w=dcw, conv_b=dcb[0], dt_bias=dnar[0, :M2_HEADS], a_log=dnar[1, :M2_HEADS],
                 d=jnp.sum(dwide[:, 0, :].reshape(M2_HEADS, M2_HD), axis=-1), norm_g=dwide[:, 1, :].reshape(M2_INNER))
    return dh, dg[0], grads


def _s5_selector():
    e16 = np.zeros((S5_STATE, S5_STATE * S5_GROUP), np.float32)
    for p in range(S5_STATE):
        e16[p, p * S5_GROUP:(p + 1) * S5_GROUP] = 1.0
    return jnp.asarray(e16)


def _s5_operands(lbr, lbi, bbr, bbi, c_re, c_im):
    eye = jnp.eye(S5_BLOCKS, dtype=F32)
    gpb = S5_GROUPS // S5_BLOCKS
    bd = lambda t: jnp.einsum("jgpk,gh->jgkhp", t.reshape(S5_BLOCKS, gpb, S5_STATE, S5_GROUP), eye).reshape(S5_BLOCKS, LANES, S5_NS)
    cd = lambda t: jnp.einsum("jgkp,gh->jgphk", t.reshape(S5_BLOCKS, gpb, S5_GROUP, S5_STATE), eye).reshape(S5_BLOCKS, S5_NS, LANES)
    bmat = jnp.concatenate([bd(bbr), bd(bbi)], axis=2).astype(BF16)
    cmat = jnp.concatenate([cd(c_re), -cd(c_im)], axis=1).astype(BF16)
    ar, ai = lbr.reshape(S5_BLOCKS, S5_NS), lbi.reshape(S5_BLOCKS, S5_NS)
    pr, pi = [ar], [ai]
    for _ in range(SUBLANES - 1):
        pr, pi = pr + [pr[-1] * ar - pi[-1] * ai], pi + [pr[-1] * ai + pi[-1] * ar]
    return bmat, cmat, jnp.stack(pr, axis=1), jnp.stack(pi, axis=1)


def _s5_fwd_layer(h, g, w_in, lam_re, lam_im, log_dt, b_re, b_im, c_re, c_im, d_skip, w_out):
    hn = _rms_fwd("s5_norm", h, g)
    u = _mm("s5_in", hn, w_in, "nn", (F32,))
    n = S5_STATE * S5_GROUP
    lbr, lbi, bbr, bbi = _s5_params_fwd(lam_re, lam_im, log_dt.reshape(S5_GROUPS, 1), b_re.reshape(S5_GROUPS, n),
                                        b_im.reshape(S5_GROUPS, n), _s5_selector())
    ops = _s5_operands(lbr, lbi, bbr, bbi, c_re, c_im)
    yg = _s5_fwd(u, *ops, d_skip.reshape(1, D_MODEL))
    ag = _mm("s5_out", yg, w_out, "nn", (F32,))
    return _glu_fwd(h, ag), (h, hn, u, ops, yg, ag)


def _s5_bwd_layer(dh_out, saved, g, w_in, lam_re, lam_im, log_dt, b_re, b_im, c_re, c_im, d_skip, w_out, dw):
    h, hn, u, ops, yg, ag = saved
    dag = _glu_bwd(dh_out, ag)
    dw(("s5_w_out", 0), "s5_dwout", yg, dag)
    dyg = _mm("s5_dyg", dag, w_out, "nt", (F32,))
    du, dbmat, dcmat, dlam, ddsk = _s5_bwd(u, *ops, d_skip.reshape(1, D_MODEL), dyg)
    eye = jnp.eye(S5_BLOCKS, dtype=F32)
    gpb = S5_GROUPS // S5_BLOCKS
    n = S5_STATE * S5_GROUP
    ub = lambda t: jnp.einsum("jgkhp,gh->jgpk", t.reshape(S5_BLOCKS, gpb, S5_GROUP, gpb, S5_STATE), eye).reshape(S5_GROUPS, n)
    uc = lambda t: jnp.einsum("jgphk,gh->jgkp", t.reshape(S5_BLOCKS, gpb, S5_STATE, gpb, S5_GROUP), eye).reshape(c_re.shape)
    cts = (dlam[:, 0, :].reshape(S5_GROUPS, S5_STATE), dlam[:, 1, :].reshape(S5_GROUPS, S5_STATE),
           ub(dbmat[:, :, :S5_NS]), ub(dbmat[:, :, S5_NS:]))
    dlr, dli, dld, dbr, dbi = _s5_params_bwd(lam_re, lam_im, log_dt.reshape(S5_GROUPS, 1), b_re.reshape(S5_GROUPS, n),
                                             b_im.reshape(S5_GROUPS, n), _s5_selector(), cts)
    dw(("s5_w_in", 0), "s5_dwin", hn, du)
    dhn = _mm("s5_dhn", du, w_in, "nt", (F32,))
    dh, dg = _rms_bwd("s5_dnorm", h, g, dhn, dh_out)
    grads = dict(lam_re=dlr, lam_im=dli, log_dt=dld[:, 0], b_re=dbr.reshape(b_re.shape), b_im=dbi.reshape(b_im.shape),
                 c_re=uc(dcmat[:, :S5_NS, :]), c_im=-uc(dcmat[:, S5_NS:, :]), d=ddsk[0])
    return dh, dg[0], grads


MIXER_OF_LAYER = ("gdn", "s5", "m2", "gdn")
MIXER_INDEX = (0, 0, 0, 1)


def _mixer_args(W, i):
    kind, j = MIXER_OF_LAYER[i], MIXER_INDEX[i]
    if kind == "gdn":
        return tuple(W["gdn_" + k][j] for k in ("w_in", "conv_w", "a_log", "dt_bias", "o_norm_g", "w_out"))
    if kind == "s5":
        return tuple(W["s5_" + k][j] for k in ("w_in", "lam_re", "lam_im", "log_dt", "b_re", "b_im", "c_re", "c_im", "d", "w_out"))
    return tuple(W["m2_" + k][j] for k in ("w_in", "conv_w", "conv_b", "dt_bias", "a_log", "d", "norm_g", "w_out"))


def _local_step(x, target, W, dw, on_layer_done):
    h = x
    saved = []
    for i in range(DEPTH):
        kind = MIXER_OF_LAYER[i]
        args = _mixer_args(W, i)
        if kind == "gdn":
            h, sm = _gdn_fwd_layer(i, h, W["norm_mix_g"][i], *args)
        elif kind == "s5":
            h, sm = _s5_fwd_layer(h, W["norm_mix_g"][i], *args)
        else:
            h, sm = _m2_fwd_layer(h, W["norm_mix_g"][i], *args)
        h, sp = _mlp_fwd(i, h, W["norm_mlp_g"][i], W["mlp_w1"][i], W["mlp_w2"][i])
        saved.append((sm, sp))
    loss, dh, dgf = _loss_head(h, W["final_norm_g"], target)
    G = {"final_norm_g": dgf[0], "norm_mix_g": [None] * DEPTH, "norm_mlp_g": [None] * DEPTH}
    mix = {}
    for i in reversed(range(DEPTH)):
        kind = MIXER_OF_LAYER[i]
        sm, sp = saved[i]
        dh, G["norm_mlp_g"][i] = _mlp_bwd(i, dh, sp, W["norm_mlp_g"][i], W["mlp_w1"][i], W["mlp_w2"][i], dw)
        args = _mixer_args(W, i)
        if kind == "gdn":
            dh, G["norm_mix_g"][i], gm = _gdn_bwd_layer(i, dh, sm, W["norm_mix_g"][i], *args, dw)
        elif kind == "s5":
            dh, G["norm_mix_g"][i], gm = _s5_bwd_layer(dh, sm, W["norm_mix_g"][i], *args, dw)
        else:
            dh, G["norm_mix_g"][i], gm = _m2_bwd_layer(dh, sm, W["norm_mix_g"][i], *args, dw)
        j = MIXER_INDEX[i]
        on_layer_done(i)
        for k, v in gm.items():
            mix.setdefault(kind + "_" + k, {})[j] = v
    for k, d in mix.items():
        G[k] = [d[j] for j in sorted(d)]
    return loss, dh, {k: jnp.stack(v) if isinstance(v, list) else v for k, v in G.items()}


ADAM_ROWS = 128
ADAM_COLS = 128


def _adamw(name, w, g, m, v):
    R, C = w.shape
    if R % ADAM_ROWS == 0:
        grid, blk = (R // ADAM_ROWS,), pl.BlockSpec((ADAM_ROWS, C), lambda i: (i, 0))
    else:
        grid, blk = (C // ADAM_COLS,), pl.BlockSpec((R, ADAM_COLS), lambda j: (0, j))

    def body(w_ref, g_ref, m_ref, v_ref, d_ref, mo_ref, vo_ref):
        gg = g_ref[...]
        mn = ADAM_B1 * m_ref[...] + (1.0 - ADAM_B1) * gg
        vn = ADAM_B2 * v_ref[...] + (1.0 - ADAM_B2) * (gg * gg)
        m_hat = mn / (1.0 - ADAM_B1 ** ADAM_STEP)
        v_hat = vn / (1.0 - ADAM_B2 ** ADAM_STEP)
        d_ref[...] = -ADAM_LR * (m_hat / (jnp.sqrt(v_hat) + ADAM_EPS) + ADAM_WD * w_ref[...])
        mo_ref[...] = mn
        vo_ref[...] = vn

    return pl.pallas_call(
        body, name=name, grid=grid, in_specs=[blk] * 4, out_specs=[blk] * 3,
        out_shape=[jax.ShapeDtypeStruct((R, C), F32)] * 3, compiler_params=_params("parallel"),
    )(w, g, m, v)


MESH = pl.DeviceIdType.MESH
ANY = pl.BlockSpec(memory_space=pl.ANY)
N_CHIPS = 4
N_DEV = 8


def _position():
    return lax.axis_index("x"), lax.axis_index("y"), lax.axis_index("c")


GATHER_IDS = {1: 1, 2: 2}
EXCHANGE_IDS = {0: 4, 1: 5, 2: 6, 3: 7}


LINK_SLOWDOWN = 40


def _link_cost(link_bytes):
    return pl.CostEstimate(flops=0, transcendentals=0, bytes_accessed=LINK_SLOWDOWN * link_bytes)


def _gather_body(w_refs, out_refs, send_sems, recv_sems):
    x, y, c = _position()
    sibling = (x, y, 1 - c)
    chips = [(1 - x, y), (x, 1 - y), (1 - x, 1 - y)]
    firsts, passes = [], []
    for t, (w_ref, out_ref) in enumerate(zip(w_refs, out_refs)):
        half = w_ref.shape[0] // 2

        def piece(cx, cy, hc, out_ref=out_ref, half=half):
            return out_ref.at[2 * cx + cy, pl.ds(hc * half, half), :]

        def copy(k, src, dst, to, t=t):
            return pltpu.make_async_remote_copy(src_ref=src, dst_ref=dst, send_sem=send_sems.at[6 * t + k],
                                                recv_sem=recv_sems.at[6 * t + k], device_id=to, device_id_type=MESH)

        first = [copy(j, w_ref.at[pl.ds(c * half, half), :], piece(x, y, c), (*chip, c)) for j, chip in enumerate(chips)]
        for cp in first:
            cp.start()
        firsts.append((first, piece, copy))
    for first, piece, copy in firsts:
        passed = [copy(3 + j, piece(*chip, c), piece(*chip, c), sibling) for j, chip in enumerate(chips)]
        for j, chip in enumerate(chips):
            copy(j, piece(*chip, c), piece(*chip, c), sibling).wait_recv()
            passed[j].start()
        passes.append(passed)
    for (first, piece, copy), passed in zip(firsts, passes):
        for j, chip in enumerate(chips):
            copy(3 + j, piece(*chip, 1 - c), piece(*chip, 1 - c), sibling).wait_recv()
        for cp in first + passed:
            cp.wait_send()


def _gather_shards(wps):
    n = len(wps)

    def body(*refs):
        _gather_body(refs[:n], refs[n:2 * n], *refs[2 * n:])

    return pl.pallas_call(
        body, name="gather_shards", in_specs=[ANY] * n, out_specs=[ANY] * n,
        out_shape=[jax.ShapeDtypeStruct((N_CHIPS, *wp.shape), wp.dtype) for wp in wps],
        scratch_shapes=[pltpu.SemaphoreType.DMA((6 * n,)), pltpu.SemaphoreType.DMA((6 * n,))],
    )(*wps)


def _gather_shards_later(wps, part):
    n = len(wps)
    w_refs = [jax.new_ref(wp, memory_space=pltpu.MemorySpace.HBM) for wp in wps]
    out_refs = [jax.empty_ref(jax.ShapeDtypeStruct((N_CHIPS, *wp.shape), wp.dtype), memory_space=pltpu.MemorySpace.HBM)
                for wp in wps]

    @pl.kernel(mesh=plsc.ScalarSubcoreMesh(axis_name="sequencer", num_cores=1), name=f"gather_shards_later{part}",
               scratch_types=(pltpu.SemaphoreType.DMA((6 * n,)), pltpu.SemaphoreType.DMA((6 * n,))),
               cost_estimate=_link_cost(3 * sum(wp.size * wp.dtype.itemsize for wp in wps)),
               compiler_params=pltpu.CompilerParams(collective_id=GATHER_IDS[part]))
    def launch(send_sems, recv_sems):
        x, y, c = _position()
        barrier = pltpu.get_barrier_semaphore()
        for peer in [(x, y, 1 - c), (1 - x, y, c), (x, 1 - y, c), (1 - x, 1 - y, c)]:
            pl.semaphore_signal(barrier, inc=1, device_id=peer, device_id_type=MESH)
        pl.semaphore_wait(barrier, 4)
        _gather_body(w_refs, out_refs, send_sems, recv_sems)

    launch()
    return [r[...] for r in out_refs]


def _pair_exchange(name, gps):
    n = len(gps)

    def body(*refs):
        g_refs, out_refs, (send_sems, recv_sems) = refs[:n], refs[n:2 * n], refs[2 * n:]
        x, y, c = _position()
        copies = []
        for t, (g_ref, out_ref) in enumerate(zip(g_refs, out_refs)):
            half = g_ref.shape[1] // 2
            copies += [pltpu.make_async_remote_copy(
                src_ref=g_ref.at[k, pl.ds((1 - c) * half, half), :], dst_ref=out_ref.at[k], send_sem=send_sems.at[N_CHIPS * t + k],
                recv_sem=recv_sems.at[N_CHIPS * t + k], device_id=(x, y, 1 - c), device_id_type=MESH) for k in range(N_CHIPS)]
        for cp in copies:
            cp.start()
        for cp in copies:
            cp.wait()

    return pl.pallas_call(
        body, name=name, in_specs=[ANY] * n, out_specs=[ANY] * n,
        out_shape=[jax.ShapeDtypeStruct((N_CHIPS, gp.shape[1] // 2, gp.shape[2]), gp.dtype) for gp in gps],
        scratch_shapes=[pltpu.SemaphoreType.DMA((N_CHIPS * n,)), pltpu.SemaphoreType.DMA((N_CHIPS * n,))],
    )(*gps)


SUM_ROWS = (1280, 1152, 1024, 512, 256, 128)


def _pair_sum(name, gp, got, core):
    n, R, C = gp.shape
    half = R // 2
    tr = _tile(half, SUM_ROWS)
    nb = half // tr

    def body(core_ref, g_ref, r_ref, o_ref):
        o_ref[...] = (g_ref[...].astype(F32) + r_ref[...].astype(F32)).astype(o_ref.dtype)

    return pl.pallas_call(
        body, name=name,
        grid_spec=pltpu.PrefetchScalarGridSpec(
            num_scalar_prefetch=1, grid=(n, nb),
            in_specs=[pl.BlockSpec((None, tr, C), lambda k, i, core_ref: (k, core_ref[0] * nb + i, 0)),
                      pl.BlockSpec((None, tr, C), lambda k, i, core_ref: (k, i, 0))],
            out_specs=pl.BlockSpec((None, tr, C), lambda k, i, core_ref: (k, i, 0))),
        out_shape=jax.ShapeDtypeStruct((n, half, C), gp.dtype), compiler_params=_params("parallel", "parallel"),
    )(core, gp, got)


def _chip_exchange_body(t_refs, out_refs, send_sems, recv_sems):
    x, y, c = _position()
    chips = [(1 - x, y), (x, 1 - y), (1 - x, 1 - y)]
    copies, waits = [], []
    for t, (t_ref, out_ref) in enumerate(zip(t_refs, out_refs)):
        for j, (cx, cy) in enumerate(chips):
            sems = dict(send_sem=send_sems.at[3 * t + j], recv_sem=recv_sems.at[3 * t + j], device_id=(cx, cy, c),
                        device_id_type=MESH)
            copies.append(pltpu.make_async_remote_copy(src_ref=t_ref.at[2 * cx + cy], dst_ref=out_ref.at[2 * x + y], **sems))
            waits.append(pltpu.make_async_remote_copy(src_ref=t_ref.at[2 * cx + cy], dst_ref=out_ref.at[2 * cx + cy], **sems))
    for cp in copies:
        cp.start()
    for cp in waits:
        cp.wait_recv()
    for cp in copies:
        cp.wait_send()


def _chip_exchange_later(ts, layer):
    n = len(ts)
    t_refs = [jax.new_ref(t, memory_space=pltpu.MemorySpace.HBM) for t in ts]
    out_refs = [jax.empty_ref(jax.ShapeDtypeStruct(t.shape, t.dtype), memory_space=pltpu.MemorySpace.HBM) for t in ts]

    @pl.kernel(mesh=plsc.ScalarSubcoreMesh(axis_name="sequencer", num_cores=1), name=f"chip_exchange_later{layer}",
               scratch_types=(pltpu.SemaphoreType.DMA((3 * n,)), pltpu.SemaphoreType.DMA((3 * n,))),
               cost_estimate=_link_cost(3 * sum(t.size * t.dtype.itemsize for t in ts) // N_CHIPS),
               compiler_params=pltpu.CompilerParams(collective_id=EXCHANGE_IDS[layer]))
    def launch(send_sems, recv_sems):
        x, y, c = _position()
        barrier = pltpu.get_barrier_semaphore()
        for peer in [(1 - x, y, c), (x, 1 - y, c), (1 - x, 1 - y, c)]:
            pl.semaphore_signal(barrier, inc=1, device_id=peer, device_id_type=MESH)
        pl.semaphore_wait(barrier, 3)
        _chip_exchange_body(t_refs, out_refs, send_sems, recv_sems)

    launch()
    return [r[...] for r in out_refs]


def _chip_sum(name, t, got, ids):
    n, H, C = t.shape
    tr = _tile(H, SUM_ROWS)
    nb = H // tr

    def body(ids_ref, t_ref, r_ref, o_ref):
        own = t_ref[...].astype(F32)
        acc = jnp.where(ids_ref[0] == 0, own, r_ref[0].astype(F32))
        for k in range(1, n):
            acc = acc + jnp.where(ids_ref[0] == k, own, r_ref[k].astype(F32))
        o_ref[...] = acc

    return pl.pallas_call(
        body, name=name,
        grid_spec=pltpu.PrefetchScalarGridSpec(
            num_scalar_prefetch=1, grid=(nb,),
            in_specs=[pl.BlockSpec((None, tr, C), lambda i, ids_ref: (ids_ref[0], i, 0)),
                      pl.BlockSpec((n, tr, C), lambda i, ids_ref: (0, i, 0))],
            out_specs=pl.BlockSpec((tr, C), lambda i, ids_ref: (ids_ref[1] * nb + i, 0))),
        out_shape=jax.ShapeDtypeStruct((2 * H, C), F32), compiler_params=_params("parallel"),
    )(ids, t, got)


def _sum_pieces(name, pieces):
    n, R, C = pieces.shape
    tr = _tile(R, (256, 128, SUBLANES))

    def body(p_ref, o_ref):
        acc = p_ref[0].astype(F32)
        for s in range(1, n):
            acc = acc + p_ref[s].astype(F32)
        o_ref[...] = acc

    return pl.pallas_call(
        body, name=name, grid=(R // tr,),
        in_specs=[pl.BlockSpec((n, tr, C), lambda i: (0, i, 0))], out_specs=pl.BlockSpec((tr, C), lambda i: (i, 0)),
        out_shape=jax.ShapeDtypeStruct((R, C), F32), compiler_params=_params("parallel"),
    )(pieces)


def _swap_halves(name, ss):
    n = len(ss)

    def body(*refs):
        s_refs, out_refs, (send_sems, recv_sems) = refs[:n], refs[n:2 * n], refs[2 * n:]
        x, y, c = _position()
        copies, waits = [], []
        for t, (s_ref, out_ref) in enumerate(zip(s_refs, out_refs)):
            half = s_ref.shape[0] // 2
            sems = dict(send_sem=send_sems.at[t], recv_sem=recv_sems.at[t], device_id=(x, y, 1 - c), device_id_type=MESH)
            mine = s_ref.at[pl.ds(c * half, half), :]
            copies.append(pltpu.make_async_remote_copy(src_ref=mine, dst_ref=out_ref.at[pl.ds(c * half, half), :], **sems))
            waits.append(pltpu.make_async_remote_copy(src_ref=mine, dst_ref=out_ref.at[pl.ds((1 - c) * half, half), :], **sems))
        for cp in copies:
            cp.start()
        for cp in waits:
            cp.wait_recv()
        for cp in copies:
            cp.wait_send()

    return pl.pallas_call(
        body, name=name, in_specs=[ANY] * n, out_specs=[ANY] * n, input_output_aliases={i: i for i in range(n)},
        out_shape=[jax.ShapeDtypeStruct(s_.shape, s_.dtype) for s_ in ss],
        scratch_shapes=[pltpu.SemaphoreType.DMA((n,)), pltpu.SemaphoreType.DMA((n,))],
    )(*ss)


def _gather_small(name, blk):
    m_per, n = blk.shape

    def body(x_ref, out_ref, send_sems, recv_sems, local_sem):
        x, y, c = _position()
        me, sibling = (x, y, c), (x, y, 1 - c)
        chips = [(1 - x, y), (x, 1 - y), (1 - x, 1 - y)]

        def rows(px, py, pc):
            return out_ref.at[pl.ds((4 * px + 2 * py + pc) * m_per, m_per), :]

        def copy(k, block, to, src=None):
            return pltpu.make_async_remote_copy(src_ref=rows(*block) if src is None else src, dst_ref=rows(*block),
                                                send_sem=send_sems.at[k], recv_sem=recv_sems.at[k], device_id=to, device_id_type=MESH)

        mine = pltpu.make_async_copy(x_ref, rows(*me), local_sem)
        mine.start()
        first = [copy(0, me, sibling, src=x_ref)] + [copy(1 + j, me, (*chip, c), src=x_ref) for j, chip in enumerate(chips)]
        for cp in first:
            cp.start()
        passed = [copy(4 + j, (*chip, c), sibling) for j, chip in enumerate(chips)]
        for j, chip in enumerate(chips):
            copy(1 + j, (*chip, c), me).wait_recv()
            passed[j].start()
        copy(0, sibling, me).wait_recv()
        for j, chip in enumerate(chips):
            copy(4 + j, (*chip, 1 - c), me).wait_recv()
        for cp in first + passed:
            cp.wait_send()
        mine.wait()

    return pl.pallas_call(
        body, name=name, out_shape=jax.ShapeDtypeStruct((N_DEV * m_per, n), blk.dtype),
        in_specs=[pl.BlockSpec(memory_space=pltpu.VMEM)], out_specs=pl.BlockSpec(memory_space=pltpu.VMEM),
        scratch_shapes=[pltpu.SemaphoreType.DMA((7,)), pltpu.SemaphoreType.DMA((7,)), pltpu.SemaphoreType.DMA],
        compiler_params=pltpu.CompilerParams(vmem_limit_bytes=VMEM_LIMIT_BYTES),
    )(blk)


WEIGHTS = ("norm_mix_g", "norm_mlp_g", "mlp_w1", "mlp_w2", "gdn_w_in", "gdn_conv_w", "gdn_a_log", "gdn_dt_bias", "gdn_o_norm_g",
           "gdn_w_out", "s5_w_in", "s5_lam_re", "s5_lam_im", "s5_log_dt", "s5_b_re", "s5_b_im", "s5_c_re", "s5_c_im", "s5_d",
           "s5_w_out", "m2_w_in", "m2_conv_w", "m2_conv_b", "m2_dt_bias", "m2_a_log", "m2_d", "m2_norm_g", "m2_w_out",
           "final_norm_g")
BIG = {"mlp_w1": 2, "mlp_w2": 1, "gdn_w_in": 2, "gdn_w_out": 1, "s5_w_in": 1, "s5_w_out": 2, "m2_w_in": 2, "m2_w_out": 1}
SMALL_CUT = {"gdn_conv_w": 2, "m2_conv_w": 2, "m2_conv_b": 1, "m2_norm_g": 1}
ROWS_MINOR = ("m2_w_in",)
ODD_WIDTH = {"gdn_w_in": GDN_IN, "m2_w_in": M2_IN}
WEIGHT_PARTS = (
    ((("gdn_w_out", 0),), (("gdn_w_in", 0),)),
    ((("mlp_w1", 0), ("mlp_w2", 0), ("mlp_w1", 1), ("mlp_w2", 1), ("s5_w_in", 0)), (("s5_w_out", 0),)),
    ((("mlp_w1", 2), ("mlp_w2", 2), ("m2_w_out", 0), ("mlp_w1", 3), ("mlp_w2", 3), ("gdn_w_out", 1)), (("m2_w_in", 0),),
     (("gdn_w_in", 1),)),
)
LAYER_ITEMS = (
    ((("mlp_w1", 0), ("mlp_w2", 0), ("gdn_w_out", 0)), (("gdn_w_in", 0),)),
    ((("mlp_w1", 1), ("mlp_w2", 1), ("s5_w_in", 0)), (("s5_w_out", 0),)),
    ((("mlp_w1", 2), ("mlp_w2", 2), ("m2_w_out", 0)), (("m2_w_in", 0),)),
    ((("mlp_w1", 3), ("mlp_w2", 3), ("gdn_w_out", 1)), (("gdn_w_in", 1),)),
)


def _rows2d(a):
    return a.reshape(-1, a.shape[-1])


def _pack(arrays, cols, row_multiple, dtype):
    flat = jnp.concatenate([a.reshape(-1).astype(dtype) for a in arrays])
    n = -(-flat.shape[0] // (cols * row_multiple)) * cols * row_multiple
    return jnp.pad(flat, (0, n - flat.shape[0])).reshape(-1, cols)


def _unpack(packed, shapes):
    flat = packed.reshape(-1)
    out, off = [], 0
    for shp in shapes:
        n = math.prod(shp)
        out.append(flat[off:off + n].reshape(shp))
        off += n
    return out


def _split_rows(buf, shapes):
    out, off = [], 0
    for shp in shapes:
        rows = math.prod(shp[:-1])
        out.append(buf[off:off + rows].reshape(shp))
        off += rows
    return out


def _cut(a, axis, k):
    n = a.shape[axis] // N_CHIPS
    return lax.slice_in_dim(a, k * n, (k + 1) * n, axis=axis)


def kernel(x, norm_mix_g, norm_mlp_g, mlp_w1, mlp_w2, gdn_w_in, gdn_conv_w, gdn_a_log, gdn_dt_bias, gdn_o_norm_g, gdn_w_out, s5_w_in, s5_lam_re, s5_lam_im, s5_log_dt, s5_b_re, s5_b_im, s5_c_re, s5_c_im, s5_d, s5_w_out, m2_w_in, m2_conv_w, m2_conv_b, m2_dt_bias, m2_a_log, m2_d, m2_norm_g, m2_w_out, final_norm_g, loss_target, m_norm_mix_g, m_norm_mlp_g, m_mlp_w1, m_mlp_w2, m_gdn_w_in, m_gdn_conv_w, m_gdn_a_log, m_gdn_dt_bias, m_gdn_o_norm_g, m_gdn_w_out, m_s5_w_in, m_s5_lam_re, m_s5_lam_im, m_s5_log_dt, m_s5_b_re, m_s5_b_im, m_s5_c_re, m_s5_c_im, m_s5_d, m_s5_w_out, m_m2_w_in, m_m2_conv_w, m_m2_conv_b, m_m2_dt_bias, m_m2_a_log, m_m2_d, m_m2_norm_g, m_m2_w_out, m_final_norm_g, v_norm_mix_g, v_norm_mlp_g, v_mlp_w1, v_mlp_w2, v_gdn_w_in, v_gdn_conv_w, v_gdn_a_log, v_gdn_dt_bias, v_gdn_o_norm_g, v_gdn_w_out, v_s5_w_in, v_s5_lam_re, v_s5_lam_im, v_s5_log_dt, v_s5_b_re, v_s5_b_im, v_s5_c_re, v_s5_c_im, v_s5_d, v_s5_w_out, v_m2_w_in, v_m2_conv_w, v_m2_conv_b, v_m2_dt_bias, v_m2_a_log, v_m2_d, v_m2_norm_g, v_m2_w_out, v_final_norm_g):
    given = dict(locals())
    w = {n: given[n] for n in WEIGHTS}
    mom = {n: given["m_" + n] for n in WEIGHTS}
    var = {n: given["v_" + n] for n in WEIGHTS}
    big, small_cut = tuple(BIG), tuple(SMALL_CUT)
    small = tuple(n for n in WEIGHTS if n not in BIG)
    chip = 2 * lax.axis_index("x") + lax.axis_index("y")

    W = {n: [None] * w[n].shape[0] for n in big}

    def fetch(groups, gather, extra=()):
        own = [jnp.concatenate([w[n][l] for n, l in grp]).astype(BF16) for grp in groups] + list(extra)
        gathered = gather(own)
        for grp, mine, got in zip(groups, own, gathered):
            shapes = [w[n][l].shape for n, l in grp]
            per_chip = [_split_rows(jnp.where(chip == k, mine, got[k]), shapes) for k in range(N_CHIPS)]
            for i, (n, l) in enumerate(grp):
                m = jnp.concatenate([per_chip[k][i] for k in range(N_CHIPS)], axis=BIG[n] - 1)
                pad = {"gdn_w_in": GDN_IN_PAD - GDN_IN, "m2_w_in": M2_IN_PAD - M2_IN}.get(n, 0)
                W[n][l] = jnp.pad(m, ((0, 0), (0, pad))) if pad else m
        return [jnp.stack([jnp.where(chip == k, mine, got[k]) for k in range(N_CHIPS)])
                for mine, got in zip(own[len(groups):], gathered[len(groups):])]

    for part in (1, 2):
        fetch(WEIGHT_PARTS[part], functools.partial(_gather_shards_later, part=part))
    cut_all, = fetch(WEIGHT_PARTS[0], _gather_shards, extra=[_pack([w[n] for n in small_cut], LANES, 2 * SUBLANES, F32)])
    per_chip = [_unpack(cut_all[k], [w[n].shape for n in small_cut]) for k in range(N_CHIPS)]
    W.update({n: jnp.concatenate([per_chip[k][i] for k in range(N_CHIPS)], axis=SMALL_CUT[n]) for i, n in enumerate(small_cut)})
    W.update({n: w[n] for n in small if n not in SMALL_CUT})

    core = lax.axis_index("c").astype(jnp.int32)
    ids = jnp.stack([chip.astype(jnp.int32), core])
    shard_grads = {}

    place, comm_bufs, odd = {}, {}, {}
    for layer, groups in enumerate(LAYER_ITEMS):
        for j, grp in enumerate(groups):
            row = 0
            for n, l in grp:
                place[n, l] = (layer, j, row)
                row += w[n].shape[1]
            if grp[0][0] not in ODD_WIDTH:
                comm_bufs[layer, j] = lax.empty((N_CHIPS, row, w[grp[0][0]].shape[2]), BF16)

    def dw(item, name, a, b):
        n = item[0]
        layer, j, row0 = place[item]
        if n in ODD_WIDTH:
            full = _mm(name, a, b, "tn", (BF16,))
            odd[layer, j] = jnp.stack([_cut(full[:, :ODD_WIDTH[n]], 1, k) for k in range(N_CHIPS)])
        else:
            comm_bufs[layer, j] = _dw_into(name, a, b, comm_bufs[layer, j], row0, BIG[n] - 1)

    def reduce_layer(i):
        groups = LAYER_ITEMS[i]
        gps = [comm_bufs[i, j] if (i, j) in comm_bufs else odd[i, j] for j in range(len(groups))]
        pairs = [_pair_sum(f"pair_sum{i}_{j}", gp, got, core.reshape(1))
                 for j, (gp, got) in enumerate(zip(gps, _pair_exchange(f"pair_exchange{i}", gps)))]
        sums = [_chip_sum(f"chip_sum{i}_{j}", t, got, ids) for j, (t, got) in enumerate(zip(pairs, _chip_exchange_later(pairs, i)))]
        for grp, g_shard in zip(groups, _swap_halves(f"swap_halves{i}", sums)):
            shard_grads.update(zip(grp, _split_rows(g_shard, [w[n][l].shape for n, l in grp])))

    loss, grad_x, G = _local_step(x[0], loss_target[0], W, dw, reduce_layer)
    grads = {n: jnp.stack([shard_grads[n, l] for l in range(w[n].shape[0])]) for n in big}
    sg = _pack([G[n] for n in small] + [loss], LANES, ADAM_ROWS, F32)
    sg_sum = _sum_pieces("sum_small_grads", _gather_small("gather_small_grads", sg).reshape(N_DEV, *sg.shape))
    *small_sums, loss = _unpack(sg_sum, [G[n].shape for n in small] + [(1, 1)])
    loss = loss[0, 0]
    for n, g in zip(small, small_sums):
        if n in SMALL_CUT:
            width = g.shape[SMALL_CUT[n]] // N_CHIPS
            g = lax.dynamic_slice_in_dim(g, chip * width, width, axis=SMALL_CUT[n])
        grads[n] = g.reshape(w[n].shape)

    delta, new_m, new_v = {}, {}, {}
    for n in big:
        if n in ROWS_MINOR:
            as2d = lambda a: jnp.swapaxes(a, -1, -2).reshape(-1, a.shape[-2])
            back = lambda o: jnp.swapaxes(o.reshape(w[n].shape[0], w[n].shape[2], w[n].shape[1]), -1, -2)
        else:
            as2d = lambda a: a.reshape(-1, a.shape[-1])
            back = lambda o: o.reshape(w[n].shape)
        outs = _adamw("adamw_" + n, as2d(w[n]), as2d(grads[n]), as2d(mom[n]), as2d(var[n]))
        delta[n], new_m[n], new_v[n] = (back(o) for o in outs)
    packs = [_pack([t[n] for n in small], LANES, ADAM_ROWS, F32) for t in (w, grads, mom, var)]
    outs = _adamw("adamw_small", *packs)
    for t, o in zip((delta, new_m, new_v), outs):
        t.update(zip(small, _unpack(o, [w[n].shape for n in small])))

    return (loss, grad_x[None], *[grads[n] for n in WEIGHTS], *[delta[n] for n in WEIGHTS], *[new_m[n] for n in WEIGHTS],
            *[new_v[n] for n in WEIGHTS])
```

```python
import functools
import math

import numpy as np
import jax
import jax.numpy as jnp
from jax import lax
from jax.experimental import pallas as pl
from jax.experimental.pallas import tpu as pltpu
from jax.experimental.pallas import tpu_sc as plsc

F32 = jnp.float32
BF16 = jnp.bfloat16

D_MODEL = 1024
D_FF = 4096
DEPTH = 4
CHUNK = 64
RMS_EPS = 1e-6
CONV_W = 4
GDN_HEADS = 8
GDN_DK = 128
GDN_IN = 4112
GDN_IN_PAD = 4224
S5_GROUPS = 64
S5_STATE = 64
S5_GROUP = 16
S5_BLOCKS = 8
M2_INNER = 2048
M2_HEADS = 32
M2_GROUPS = 8
M2_STATE = 128
M2_CONV_CH = 4096
M2_IN = 6176
M2_IN_PAD = 6272
ADAM_LR, ADAM_B1, ADAM_B2, ADAM_EPS, ADAM_WD, ADAM_STEP = 0.001, 0.9, 0.999, 1e-08, 0.01, 10

VMEM_LIMIT_BYTES = 56 * 1024 * 1024
SUBLANES = 8
LANES = 128


def _params(*sem):
    return pltpu.CompilerParams(dimension_semantics=tuple(sem) if sem else None, vmem_limit_bytes=VMEM_LIMIT_BYTES)


NN, NT, TN = ((1,), (0,)), ((1,), (1,)), ((0,), (0,))
_DOT_TRANSPOSES = {NN: ((NT, "gb"), (TN, "ag")), NT: ((NN, "gb"), (TN, "ga")), TN: ((NT, "bg"), (NN, "ag"))}


def _dg(a, b, dims):
    if a.ndim == 3:
        dn = (((dims[0][0] + 1,), (dims[1][0] + 1,)), ((0,), (0,)))
    else:
        dn = (dims, ((), ()))
    return lax.dot_general(a, b, dn, preferred_element_type=F32)


def _mxu(a, b, dims):
    return _dg(a.astype(BF16), b.astype(BF16), dims)


@functools.partial(jax.custom_vjp, nondiff_argnums=(2,))
def _dot(a, b, dims=NN):
    return _mxu(a, b, dims)


def _dot_fwd(a, b, dims):
    return _mxu(a, b, dims), (a, b)


def _dot_bwd(dims, res, g):
    ops = dict(a=res[0], b=res[1], g=g)
    (da_dims, da_ops), (db_dims, db_ops) = _DOT_TRANSPOSES[dims]
    return (_mxu(ops[da_ops[0]], ops[da_ops[1]], da_dims).astype(res[0].dtype),
            _mxu(ops[db_ops[0]], ops[db_ops[1]], db_dims).astype(res[1].dtype))


_dot.defvjp(_dot_fwd, _dot_bwd)


def _nt(a, b):
    return _dot(a, b, NT)


def _tn(a, b):
    return _dot(a, b, TN)


def _split3(x):
    x1 = x.astype(BF16)
    r = x - x1.astype(F32)
    x2 = r.astype(BF16)
    return x1, x2, (r - x2.astype(F32)).astype(BF16)


def _sel_mxu(x, sel, dims, x_first):
    f = (lambda p: _dg(p, sel.astype(BF16), dims)) if x_first else (lambda p: _dg(sel.astype(BF16), p, dims))
    x1, x2, x3 = _split3(x)
    return f(x1) + (f(x2) + f(x3))


@jax.custom_vjp
def _pick(x, sel):
    return _sel_mxu(x, sel, NN, True)


def _pick_fwd(x, sel):
    return _sel_mxu(x, sel, NN, True), sel


def _pick_bwd(sel, g):
    return _sel_mxu(g, sel, NT, True), jnp.zeros_like(sel)


_pick.defvjp(_pick_fwd, _pick_bwd)


@jax.custom_vjp
def _accum(sel, x):
    return _sel_mxu(x, sel, NN, False)


def _accum_fwd(sel, x):
    return _sel_mxu(x, sel, NN, False), sel


def _accum_bwd(sel, g):
    return jnp.zeros_like(sel), _sel_mxu(g, sel, TN, False)


_accum.defvjp(_accum_fwd, _accum_bwd)


def _dot3(a, b, dims=NN):
    ah, bh = a.astype(BF16), b.astype(BF16)
    al, bl = (a - ah.astype(F32)).astype(BF16), (b - bh.astype(F32)).astype(BF16)
    return _dg(ah, bh, dims) + (_dg(ah, bl, dims) + _dg(al, bh, dims))


def _neumann(x, r, dims):
    r = r + _dot3(x, r, dims)
    for _ in range(5):
        x = _dot3(x, x)
        r = r + _dot3(x, r, dims)
    return r


@jax.custom_vjp
def _unit_lower_solve(a, rhs):
    return _neumann(-a, rhs, NN)


def _unit_lower_solve_fwd(a, rhs):
    sol = _neumann(-a, rhs, NN)
    return sol, (a, sol)


def _unit_lower_solve_bwd(res, ct):
    a, sol = res
    d_rhs = _neumann(-a, ct, TN)
    return -_dot3(d_rhs, sol, NT), d_rhs


_unit_lower_solve.defvjp(_unit_lower_solve_fwd, _unit_lower_solve_bwd)


@jax.custom_vjp
def _unit_lower_solved(a, rhs, sol):
    return sol


def _unit_lower_solved_fwd(a, rhs, sol):
    return sol, (a, sol)


def _unit_lower_solved_bwd(res, ct):
    da, d_rhs = _unit_lower_solve_bwd(res, ct)
    return da, d_rhs, jnp.zeros_like(ct)


_unit_lower_solved.defvjp(_unit_lower_solved_fwd, _unit_lower_solved_bwd)


def _sigmoid(x):
    return 1.0 / (1.0 + jnp.exp(-x))


def _softplus(x):
    return jnp.maximum(x, 0.0) + jnp.log(1.0 + jnp.exp(-jnp.abs(x)))


def _iota2(shape, axis):
    return lax.broadcasted_iota(jnp.int32, shape, axis)


def _tile(n, cands):
    for c in cands:
        if n % c == 0:
            return c
    return n


MM_TILE_BYTES = 9 * 1024 * 1024


def _mm(name, a, b, mode, out_dtypes, epi=None, extras=(), tn=None):
    if mode == "nn":
        (M, K), N = a.shape, b.shape[1]
    elif mode == "nt":
        (M, K), N = a.shape, b.shape[0]
    else:
        (K, M), N = a.shape, b.shape[1]
    tn = tn or (1024 if N % 1024 == 0 and N >= 2048 else _tile(N, (512, 384, 896, 256, 128)))
    out_bytes = tn * (sum(jnp.dtype(d).itemsize for d in out_dtypes) + sum(e.dtype.itemsize for e in extras))
    fits = lambda t: t * K * a.dtype.itemsize <= MM_TILE_BYTES and t * out_bytes <= MM_TILE_BYTES
    tm = next(t for t in (2048, 1024, 512, 256, 128) if M % t == 0 and (fits(t) or t == 128))
    if mode == "nn":
        a_spec, b_spec = pl.BlockSpec((tm, K), lambda i, j: (i, 0)), pl.BlockSpec((K, tn), lambda i, j: (0, j))
        dims = NN
    elif mode == "nt":
        a_spec, b_spec = pl.BlockSpec((tm, K), lambda i, j: (i, 0)), pl.BlockSpec((tn, K), lambda i, j: (j, 0))
        dims = NT
    else:
        a_spec, b_spec = pl.BlockSpec((K, tm), lambda i, j: (0, i)), pl.BlockSpec((K, tn), lambda i, j: (0, j))
        dims = TN
    n_ex = len(extras)

    def body(a_ref, b_ref, *rest):
        acc = _mxu(a_ref[...], b_ref[...], dims)
        res = epi(acc, *[e[...] for e in rest[:n_ex]]) if epi is not None else (acc,)
        for o_ref, r in zip(rest[n_ex:], res):
            o_ref[...] = r.astype(o_ref.dtype)

    tile = pl.BlockSpec((tm, tn), lambda i, j: (i, j))
    out = pl.pallas_call(
        body, name=name, grid=(M // tm, N // tn),
        in_specs=[a_spec, b_spec] + [tile] * n_ex,
        out_specs=[tile] * len(out_dtypes),
        out_shape=[jax.ShapeDtypeStruct((M, N), d) for d in out_dtypes],
        compiler_params=_params("parallel", "parallel"),
    )(a, b, *extras)
    return out if len(out_dtypes) > 1 else out[0]


def _dw_into(name, a, b, buf, row0, cut_axis):
    (K, M), N = a.shape, b.shape[1]
    ms, ns = (M, N // N_CHIPS) if cut_axis == 1 else (M // N_CHIPS, N)
    assert buf.shape[2] == ns, (buf.shape, ns)
    tm = next(t for t in (1024, 512, 256, 128) if ms % t == 0 and row0 % t == 0)
    tn = _tile(ns, (512, 256, 128))
    rb, cb = ms // tm, ns // tn
    if cut_axis == 1:
        where = lambda i, j: (j // cb, row0 // tm + i, j % cb)
    else:
        where = lambda i, j: (i // rb, row0 // tm + i % rb, j)

    def body(a_ref, b_ref, buf_ref, o_ref):
        o_ref[...] = _mxu(a_ref[...], b_ref[...], TN).astype(o_ref.dtype)

    return pl.pallas_call(
        body, name=name, grid=(M // tm, N // tn),
        in_specs=[pl.BlockSpec((K, tm), lambda i, j: (0, i)), pl.BlockSpec((K, tn), lambda i, j: (0, j)),
                  pl.BlockSpec(memory_space=pl.ANY)],
        out_specs=pl.BlockSpec((None, tm, tn), where), out_shape=jax.ShapeDtypeStruct(buf.shape, buf.dtype),
        input_output_aliases={2: 0}, compiler_params=_params("parallel", "parallel"),
    )(a, b, buf)


def _rms_fwd(name, h, g):
    L, D = h.shape
    tr = _tile(L, (256, 128))

    def body(h_ref, g_ref, o_ref):
        x = h_ref[...]
        r = lax.rsqrt(jnp.mean(x * x, axis=-1, keepdims=True) + RMS_EPS)
        o_ref[...] = (x * r * g_ref[...]).astype(o_ref.dtype)

    return pl.pallas_call(
        body, name=name, grid=(L // tr,),
        in_specs=[pl.BlockSpec((tr, D), lambda i: (i, 0)), pl.BlockSpec((1, D), lambda i: (0, 0))],
        out_specs=pl.BlockSpec((tr, D), lambda i: (i, 0)),
        out_shape=jax.ShapeDtypeStruct((L, D), BF16),
        compiler_params=_params("parallel"),
    )(h, g.reshape(1, D))


def _rms_bwd(name, h, g, dhn, dres):
    L, D = h.shape
    tr = _tile(L, (256, 128))

    def body(h_ref, g_ref, dhn_ref, dres_ref, dh_ref, dg_ref):
        x = h_ref[...]
        r = lax.rsqrt(jnp.mean(x * x, axis=-1, keepdims=True) + RMS_EPS)
        xh = x * r
        dy = dhn_ref[...]
        dxh = dy * g_ref[...]
        dh_ref[...] = dres_ref[...] + r * (dxh - xh * jnp.mean(dxh * xh, axis=-1, keepdims=True))

        @pl.when(pl.program_id(0) == 0)
        def _():
            dg_ref[...] = jnp.zeros_like(dg_ref)

        dg_ref[...] += jnp.sum(dy * xh, axis=0, keepdims=True)

    row = pl.BlockSpec((tr, D), lambda i: (i, 0))
    vec = pl.BlockSpec((1, D), lambda i: (0, 0))
    return pl.pallas_call(
        body, name=name, grid=(L // tr,),
        in_specs=[row, vec, row, row], out_specs=[row, vec],
        out_shape=[jax.ShapeDtypeStruct((L, D), F32), jax.ShapeDtypeStruct((1, D), F32)],
        compiler_params=_params("arbitrary"),
    )(h, g.reshape(1, D), dhn, dres)


def _loss_head(h, g, target):
    L, D = h.shape
    tr = _tile(L, (256, 128))

    def body(h_ref, g_ref, t_ref, loss_ref, dh_ref, dg_ref):
        x = h_ref[...]
        r = lax.rsqrt(jnp.mean(x * x, axis=-1, keepdims=True) + RMS_EPS)
        xh = x * r
        err = xh * g_ref[...] - t_ref[...]
        dy = err * (1.0 / D)
        dxh = dy * g_ref[...]
        dh_ref[...] = r * (dxh - xh * jnp.mean(dxh * xh, axis=-1, keepdims=True))

        @pl.when(pl.program_id(0) == 0)
        def _():
            dg_ref[...] = jnp.zeros_like(dg_ref)
            loss_ref[...] = jnp.zeros_like(loss_ref)

        dg_ref[...] += jnp.sum(dy * xh, axis=0, keepdims=True)
        loss_ref[...] += (0.5 / D) * jnp.sum(jnp.sum(err * err, axis=-1, keepdims=True), axis=0, keepdims=True)

    row = pl.BlockSpec((tr, D), lambda i: (i, 0))
    vec = pl.BlockSpec((1, D), lambda i: (0, 0))
    return pl.pallas_call(
        body, name="loss_head", grid=(L // tr,),
        in_specs=[row, vec, row], out_specs=[pl.BlockSpec((1, 1), lambda i: (0, 0)), row, vec],
        out_shape=[jax.ShapeDtypeStruct((1, 1), F32), jax.ShapeDtypeStruct((L, D), F32), jax.ShapeDtypeStruct((1, D), F32)],
        compiler_params=_params("arbitrary"),
    )(h, g.reshape(1, D), target)


def _glu_fwd(h, ag):
    L, D = h.shape
    tr = _tile(L, (256, 128))

    def body(h_ref, v_ref, g_ref, o_ref):
        o_ref[...] = h_ref[...] + v_ref[...] * _sigmoid(g_ref[...])

    return pl.pallas_call(
        body, name="s5_glu_fwd", grid=(L // tr,),
        in_specs=[pl.BlockSpec((tr, D), lambda i: (i, 0)), pl.BlockSpec((tr, D), lambda i: (i, 0)),
                  pl.BlockSpec((tr, D), lambda i: (i, 1))],
        out_specs=pl.BlockSpec((tr, D), lambda i: (i, 0)),
        out_shape=jax.ShapeDtypeStruct((L, D), F32),
        compiler_params=_params("parallel"),
    )(h, ag, ag)


def _glu_bwd(dh, ag):
    L, D = dh.shape
    tr = _tile(L, (256, 128))

    def body(dh_ref, v_ref, g_ref, dv_ref, dg_ref):
        s = _sigmoid(g_ref[...])
        d = dh_ref[...]
        dv_ref[...] = d * s
        dg_ref[...] = d * v_ref[...] * s * (1.0 - s)

    dv, dg = pl.pallas_call(
        body, name="s5_glu_bwd", grid=(L // tr,),
        in_specs=[pl.BlockSpec((tr, D), lambda i: (i, 0)), pl.BlockSpec((tr, D), lambda i: (i, 0)),
                  pl.BlockSpec((tr, D), lambda i: (i, 1))],
        out_specs=[pl.BlockSpec((tr, D), lambda i: (i, 0))] * 2,
        out_shape=[jax.ShapeDtypeStruct((L, D), F32)] * 2,
        compiler_params=_params("parallel"),
    )(dh, ag, ag)
    return jnp.concatenate([dv, dg], axis=1).astype(BF16)


CONV_ROWS = 128
CONV_COLS = 512


def _shift_rows(cat, s):
    if s == 0:
        return cat[SUBLANES:, :]
    return pltpu.roll(cat, s, axis=0)[SUBLANES:, :]


def _conv_fwd(name, p, col0, w, b):
    L = p.shape[0]
    C = w.shape[1]
    tc = _tile(C, (CONV_COLS, 256))
    cb0 = col0 // tc
    nr = L // CONV_ROWS

    def body(x_ref, w_ref, b_ref, o_ref):
        def step(r, carry):
            r0 = pl.multiple_of(r * CONV_ROWS, CONV_ROWS)
            cur = x_ref[pl.ds(r0, CONV_ROWS), :]
            p0 = pl.multiple_of(jnp.maximum(r0 - SUBLANES, 0), SUBLANES)
            prev = jnp.where(r > 0, x_ref[pl.ds(p0, SUBLANES), :], 0.0)
            cat = jnp.concatenate([prev, cur], axis=0)
            acc = b_ref[...] + w_ref[3:4, :] * cur
            for k in range(CONV_W - 1):
                acc = acc + w_ref[k:k + 1, :] * _shift_rows(cat, CONV_W - 1 - k)
            o_ref[pl.ds(r0, CONV_ROWS), :] = acc * _sigmoid(acc)
            return carry

        lax.fori_loop(0, nr, step, 0)

    return pl.pallas_call(
        body, name=name, grid=(C // tc,),
        in_specs=[pl.BlockSpec((L, tc), lambda j: (0, cb0 + j)), pl.BlockSpec((CONV_W, tc), lambda j: (0, j)),
                  pl.BlockSpec((1, tc), lambda j: (0, j))],
        out_specs=pl.BlockSpec((L, tc), lambda j: (0, j)),
        out_shape=jax.ShapeDtypeStruct((L, C), F32),
        compiler_params=_params("parallel"),
    )(p, w, b)


def _conv_bwd(name, p, col0, w, b, dout):
    L = p.shape[0]
    C = w.shape[1]
    tc = _tile(C, (CONV_COLS, 256))
    cb0 = col0 // tc
    nr = L // CONV_ROWS

    def body(x_ref, w_ref, b_ref, do_ref, dx_ref, dw_ref, db_ref, dpre_ref):
        def step1(r, carry):
            dw0, dw1, dw2, dw3, dbb = carry
            r0 = pl.multiple_of(r * CONV_ROWS, CONV_ROWS)
            cur = x_ref[pl.ds(r0, CONV_ROWS), :]
            p0 = pl.multiple_of(jnp.maximum(r0 - SUBLANES, 0), SUBLANES)
            prev = jnp.where(r > 0, x_ref[pl.ds(p0, SUBLANES), :], 0.0)
            cat = jnp.concatenate([prev, cur], axis=0)
            sh = [_shift_rows(cat, CONV_W - 1 - k) for k in range(CONV_W - 1)] + [cur]
            acc = b_ref[...] + w_ref[3:4, :] * cur
            for k in range(CONV_W - 1):
                acc = acc + w_ref[k:k + 1, :] * sh[k]
            sg = _sigmoid(acc)
            dpre = do_ref[pl.ds(r0, CONV_ROWS), :] * (sg + acc * sg * (1.0 - sg))
            dpre_ref[pl.ds(r0, CONV_ROWS), :] = dpre
            dws = [d + jnp.sum(dpre * s, axis=0, keepdims=True) for d, s in zip((dw0, dw1, dw2, dw3), sh)]
            return (*dws, dbb + jnp.sum(dpre, axis=0, keepdims=True))

        z = jnp.zeros((1, tc), F32)
        dw0, dw1, dw2, dw3, dbb = lax.fori_loop(0, nr, step1, (z, z, z, z, z))
        dw_ref[...] = jnp.concatenate([dw0, dw1, dw2, dw3, z, z, z, z], axis=0)
        db_ref[...] = dbb

        def step2(r, carry):
            r0 = pl.multiple_of(r * CONV_ROWS, CONV_ROWS)
            cur = dpre_ref[pl.ds(r0, CONV_ROWS), :]
            n0 = pl.multiple_of(jnp.minimum(r0 + CONV_ROWS, L - SUBLANES), SUBLANES)
            nxt = jnp.where(r < nr - 1, dpre_ref[pl.ds(n0, SUBLANES), :], 0.0)
            cat = jnp.concatenate([cur, nxt], axis=0)
            acc = w_ref[3:4, :] * cur
            for k in range(CONV_W - 1):
                s = CONV_W - 1 - k
                acc = acc + w_ref[k:k + 1, :] * pltpu.roll(cat, CONV_ROWS + SUBLANES - s, axis=0)[:CONV_ROWS, :]
            dx_ref[pl.ds(r0, CONV_ROWS), :] = acc
            return carry

        lax.fori_loop(0, nr, step2, 0)

    dx, dw, db = pl.pallas_call(
        body, name=name, grid=(C // tc,),
        in_specs=[pl.BlockSpec((L, tc), lambda j: (0, cb0 + j)), pl.BlockSpec((CONV_W, tc), lambda j: (0, j)),
                  pl.BlockSpec((1, tc), lambda j: (0, j)), pl.BlockSpec((L, tc), lambda j: (0, j))],
        out_specs=[pl.BlockSpec((L, tc), lambda j: (0, j)), pl.BlockSpec((SUBLANES, tc), lambda j: (0, j)),
                   pl.BlockSpec((1, tc), lambda j: (0, j))],
        out_shape=[jax.ShapeDtypeStruct((L, C), F32), jax.ShapeDtypeStruct((SUBLANES, C), F32),
                   jax.ShapeDtypeStruct((1, C), F32)],
        scratch_shapes=[pltpu.VMEM((L, tc), F32)],
        compiler_params=_params("parallel"),
    )(p, w, b, dout)
    return dx, dw[:CONV_W], db


def _chunk_consts():
    r, c = _iota2((CHUNK, CHUNK), 0), _iota2((CHUNK, CHUNK), 1)
    causal = r >= c
    return causal, r > c, (r == c).astype(F32), causal.astype(F32), jnp.ones((CHUNK, CHUNK), F32)


def _by_lanes(t):
    return jnp.concatenate([t[i] for i in range(t.shape[0])], axis=1)


def _by_batch(t, w):
    return jnp.concatenate([t[None, :, i * w:(i + 1) * w] for i in range(t.shape[1] // w)], axis=0)


def _diag_lanes():
    return (_iota2((CHUNK, LANES), 0) == _iota2((CHUNK, LANES), 1)).astype(F32)


def _gdn_chunk(q, k, v, ab, gate, S, alog, dtb, og, ea, eb, sol=None):
    causal, strict, _, tril, ones = _chunk_consts()
    logits = _by_batch(_pick(ab, jnp.concatenate([_by_lanes(ea), _by_lanes(eb)], axis=1)), LANES)
    H = q.shape[0]
    g = -jnp.exp(alog) * _softplus(logits[:H] + dtb)
    beta = _sigmoid(logits[H:])
    qn = q * lax.rsqrt(jnp.sum(q * q, axis=-1, keepdims=True) + 1e-6) * (GDN_DK ** -0.5)
    kn = k * lax.rsqrt(jnp.sum(k * k, axis=-1, keepdims=True) + 1e-6)
    g_l = _by_lanes(g)
    gc = _by_batch(_accum(tril, g_l), LANES)
    glast = _by_batch(_accum(ones, g_l), LANES)
    gcol = gc[:, :, :CHUNK]
    grow = _by_batch(_accum(ones, _by_lanes(gc * _diag_lanes())), LANES)[:, :, :CHUNK]
    decay = jnp.exp(jnp.where(causal, gcol - grow, -jnp.inf))
    a = jnp.where(strict, beta[:, :, :CHUNK] * _nt(kn, kn) * decay, 0.0)
    eg = jnp.exp(gc)
    rhs = jnp.concatenate([v * beta, kn * (beta * eg)], axis=2)
    sol = _unit_lower_solve(a, rhs) if sol is None else _unit_lower_solved(a, rhs, sol)
    u, w = sol[:, :, :GDN_DK], sol[:, :, GDN_DK:]
    qk = _nt(qn, kn) * decay
    v_new = u - _dot(w, S)
    o = _dot(qn * eg, S) + _dot(qk, v_new)
    cd = jnp.exp(glast)
    s_new = jnp.concatenate([cd, cd], axis=1) * S + _tn(kn * jnp.exp(glast - gc), v_new)
    on = o * lax.rsqrt(jnp.mean(o * o, axis=-1, keepdims=True) + RMS_EPS) * og
    return on * (gate * _sigmoid(gate)), s_new, sol


GDN_HB = 8


def _gdn_specs(nc, rev):
    cm = (lambda c: nc - 1 - c) if rev else (lambda c: c)
    blk = lambda off: pl.BlockSpec((CHUNK, GDN_HB * GDN_DK), lambda c, h: (cm(c), off // GDN_HB + h))
    ab = pl.BlockSpec((CHUNK, LANES), lambda c, h: (cm(c), (GDN_IN_PAD - LANES) // LANES))
    hv = pl.BlockSpec((GDN_HB, 1, LANES), lambda c, h: (h, 0, 0))
    og = pl.BlockSpec((1, LANES), lambda c, h: (0, 0))
    em = pl.BlockSpec((GDN_HB, LANES, LANES), lambda c, h: (h, 0, 0))
    st = pl.BlockSpec((None, GDN_HB, GDN_DK, GDN_DK), lambda c, h: (cm(c), h, 0, 0))
    sl = pl.BlockSpec((None, GDN_HB, CHUNK, 2 * GDN_DK), lambda c, h: (cm(c), h, 0, 0))
    return blk, ab, hv, og, em, st, sl


def _gdn_fwd(qc, kc, vc, p, alog_e, dtb_e, og, ea, eb):
    L = qc.shape[0]
    nc = L // CHUNK
    blk, ab, hv, ogs, em, st, sl = _gdn_specs(nc, False)

    def body(q_ref, k_ref, v_ref, gate_ref, ab_ref, al_ref, dt_ref, og_ref, ea_ref, eb_ref, y_ref, sp_ref, sol_ref, s_scr):
        c, h = pl.program_id(0), pl.program_id(1)
        lanes = [slice(i * GDN_DK, (i + 1) * GDN_DK) for i in range(GDN_HB)]
        heads = pl.ds(h * GDN_HB, GDN_HB)
        stack = lambda ref: jnp.concatenate([ref[:, ls][None] for ls in lanes], axis=0)

        @pl.when(c == 0)
        def _():
            s_scr[heads] = jnp.zeros((GDN_HB, GDN_DK, GDN_DK), F32)

        S = s_scr[heads]
        sp_ref[...] = S
        y, s_new, sol = _gdn_chunk(stack(q_ref), stack(k_ref), stack(v_ref), ab_ref[...], stack(gate_ref), S,
                                   al_ref[...], dt_ref[...], og_ref[...], ea_ref[...], eb_ref[...])
        for i, ls in enumerate(lanes):
            y_ref[:, ls] = y[i]
        s_scr[heads] = s_new
        sol_ref[...] = sol

    return pl.pallas_call(
        body, name="gdn_fwd", grid=(nc, GDN_HEADS // GDN_HB),
        in_specs=[blk(0), blk(0), blk(0), blk(3 * GDN_HEADS), ab, hv, hv, ogs, em, em],
        out_specs=[blk(0), st, sl],
        out_shape=[jax.ShapeDtypeStruct((L, D_MODEL), F32), jax.ShapeDtypeStruct((nc, GDN_HEADS, GDN_DK, GDN_DK), F32),
                   jax.ShapeDtypeStruct((nc, GDN_HEADS, CHUNK, 2 * GDN_DK), F32)],
        scratch_shapes=[pltpu.VMEM((GDN_HEADS, GDN_DK, GDN_DK), F32)],
        compiler_params=_params("arbitrary", "arbitrary"),
    )(qc, kc, vc, p, p, alog_e, dtb_e, og, ea, eb)


def _gdn_bwd(qc, kc, vc, p, alog_e, dtb_e, og, ea, eb, sprev, sol, dy):
    L = qc.shape[0]
    nc = L // CHUNK
    blk, ab, hv, ogs, em, st, sl = _gdn_specs(nc, True)

    def body(q_ref, k_ref, v_ref, gate_ref, ab_ref, al_ref, dt_ref, og_ref, ea_ref, eb_ref, sp_ref, sol_ref, dy_ref,
             dq_ref, dk_ref, dv_ref, dgate_ref, dab_ref, dpar_ref, ds_scr):
        c, h = pl.program_id(0), pl.program_id(1)
        lanes = [slice(i * GDN_DK, (i + 1) * GDN_DK) for i in range(GDN_HB)]
        heads = pl.ds(h * GDN_HB, GDN_HB)
        stack = lambda ref: jnp.concatenate([ref[:, ls][None] for ls in lanes], axis=0)

        @pl.when(c == 0)
        def _():
            ds_scr[heads] = jnp.zeros((GDN_HB, GDN_DK, GDN_DK), F32)
            dpar_ref[heads] = jnp.zeros((GDN_HB, SUBLANES, LANES), F32)

        @pl.when(h == 0)
        def _():
            dab_ref[...] = jnp.zeros_like(dab_ref)

        ea_m, eb_m, sol_m = ea_ref[...], eb_ref[...], sol_ref[...]
        f = lambda q, k, v, a_b, gate, S, al, dt, o_g: _gdn_chunk(q, k, v, a_b, gate, S, al, dt, o_g, ea_m, eb_m, sol_m)[:2]
        _, vjp = jax.vjp(f, stack(q_ref), stack(k_ref), stack(v_ref), ab_ref[...], stack(gate_ref), sp_ref[...],
                         al_ref[...], dt_ref[...], og_ref[...])
        dq, dk, dv, dab, dgate, ds, dal, ddt, dog = vjp((stack(dy_ref), ds_scr[heads]))
        for i, ls in enumerate(lanes):
            dq_ref[:, ls] = dq[i]
            dk_ref[:, ls] = dk[i]
            dv_ref[:, ls] = dv[i]
            dgate_ref[:, ls] = dgate[i]
        ds_scr[heads] = ds
        dab_ref[...] += dab
        first = _iota2((GDN_HB, 1, LANES), 0) == 0
        dpar_ref[heads] += jnp.concatenate([dal, ddt, jnp.where(first, dog[None], 0.0),
                                            jnp.zeros((GDN_HB, SUBLANES - 3, LANES), F32)], axis=1)

    return pl.pallas_call(
        body, name="gdn_bwd", grid=(nc, GDN_HEADS // GDN_HB),
        in_specs=[blk(0), blk(0), blk(0), blk(3 * GDN_HEADS), ab, hv, hv, ogs, em, em, st, sl, blk(0)],
        out_specs=[blk(0), blk(0), blk(0), blk(0), pl.BlockSpec((CHUNK, LANES), lambda c, h: (nc - 1 - c, 0)),
                   pl.BlockSpec((GDN_HEADS, SUBLANES, LANES), lambda c, h: (0, 0, 0))],
        out_shape=[jax.ShapeDtypeStruct((L, D_MODEL), F32)] * 4
        + [jax.ShapeDtypeStruct((L, LANES), F32), jax.ShapeDtypeStruct((GDN_HEADS, SUBLANES, LANES), F32)],
        scratch_shapes=[pltpu.VMEM((GDN_HEADS, GDN_DK, GDN_DK), F32)],
        compiler_params=_params("arbitrary", "arbitrary"),
    )(qc, kc, vc, p, p, alog_e, dtb_e, og, ea, eb, sprev, sol, dy)


def _gdn_selectors():
    rows = np.arange(LANES)[None, :, None]
    heads = np.arange(GDN_HEADS)[:, None, None]
    ea = np.broadcast_to(rows == heads, (GDN_HEADS, LANES, LANES)).astype(np.float32)
    eb = np.broadcast_to(rows == heads + GDN_HEADS, (GDN_HEADS, LANES, LANES)).astype(np.float32)
    return jnp.asarray(ea), jnp.asarray(eb)


M2_GW = M2_INNER // M2_GROUPS
M2_HPG = M2_HEADS // M2_GROUPS
M2_HD = M2_INNER // M2_HEADS


def _m2_chunk(x, bm, cm, z, dtr, st, dtb, alog, dsk, ng, e, ecol):
    G = x.shape[0]
    causal, _, _, tril, ones = _chunk_consts()
    dt_n = _softplus(dtr + dtb)
    da_n = dt_n * (-jnp.exp(alog))
    cum_n = _accum(tril, da_n)
    tot_n = _accum(ones, da_n)
    wide = _pick(jnp.concatenate([dt_n, cum_n, tot_n], axis=0), e)
    dt_w, cum_w, tot_w = (_by_batch(wide[i * CHUNK:(i + 1) * CHUNK], M2_GW) for i in range(3))
    xdt = x * dt_w
    cb = _nt(cm, bm)
    heads = lambda t: jnp.concatenate([t[i:i + 1] for i in range(G) for _ in range(M2_HPG)], axis=0)
    colb = _by_batch(_pick(cum_n, ecol), LANES)
    rowb = _by_batch(_accum(ones, _by_lanes(colb * _diag_lanes())), LANES)
    lmat = jnp.exp(jnp.where(causal, colb[:, :, :CHUNK] - rowb[:, :, :CHUNK], -jnp.inf))
    yr = _dot(heads(cb) * lmat, heads(xdt))
    head = _iota2((CHUNK, M2_GW), 1) // M2_HD
    ydiag = jnp.concatenate([sum(jnp.where(head == r, yr[i * M2_HPG + r], 0.0) for r in range(M2_HPG))[None] for i in range(G)], axis=0)
    st_new = _tn(bm, xdt * jnp.exp(tot_w - cum_w))
    cd = jnp.exp(tot_w)
    s_new = jnp.concatenate([cd, cd], axis=1) * st + st_new
    y = ydiag + _dot(cm, st) * jnp.exp(cum_w) + dsk * x
    y = y * (z * _sigmoid(z))
    yn = y * lax.rsqrt(jnp.mean(y * y, axis=-1, keepdims=True) + RMS_EPS) * ng
    return yn, s_new


M2_GB = 8


def _m2_specs(nc, rev):
    cm = (lambda c: nc - 1 - c) if rev else (lambda c: c)
    wide = lambda off: pl.BlockSpec((CHUNK, M2_GB * M2_GW), lambda c, g: (cm(c), off // M2_GB + g))
    nar = lambda off: pl.BlockSpec((CHUNK, M2_GB * LANES), lambda c, g: (cm(c), off // M2_GB + g))
    dts = pl.BlockSpec((CHUNK, LANES), lambda c, g: (cm(c), (M2_IN_PAD - LANES) // LANES))
    v128 = pl.BlockSpec((1, LANES), lambda c, g: (0, 0))
    v256 = pl.BlockSpec((1, M2_GB * M2_GW), lambda c, g: (0, g))
    es = pl.BlockSpec((LANES, M2_GB * M2_GW), lambda c, g: (0, g))
    ecs = pl.BlockSpec((LANES, M2_GB * M2_HPG * LANES), lambda c, g: (0, g))
    st = pl.BlockSpec((None, M2_GB, M2_STATE, M2_GW), lambda c, g: (cm(c), g, 0, 0))
    return wide, nar, dts, v128, v256, es, ecs, st


def _m2_fwd(xbc, p, dtb, alog, dsk, ng, e, ecol):
    L = xbc.shape[0]
    nc = L // CHUNK
    wide, nar, dts, v128, v256, es, ecs, st = _m2_specs(nc, False)

    def body(x_ref, b_ref, c_ref, z_ref, dt_ref, dtb_ref, al_ref, dsk_ref, ng_ref, e_ref, ec_ref, y_ref, sp_ref, s_scr):
        c, g = pl.program_id(0), pl.program_id(1)
        wide_l = [slice(i * M2_GW, (i + 1) * M2_GW) for i in range(M2_GB)]
        nar_l = [slice(i * LANES, (i + 1) * LANES) for i in range(M2_GB)]
        groups = pl.ds(g * M2_GB, M2_GB)
        wide_s = lambda ref: jnp.concatenate([ref[:, ls][None] for ls in wide_l], axis=0)
        nar_s = lambda ref: jnp.concatenate([ref[:, ls][None] for ls in nar_l], axis=0)

        @pl.when(c == 0)
        def _():
            s_scr[groups] = jnp.zeros((M2_GB, M2_STATE, M2_GW), F32)

        S = s_scr[groups]
        sp_ref[...] = S
        y, s_new = _m2_chunk(wide_s(x_ref), nar_s(b_ref), nar_s(c_ref), wide_s(z_ref), dt_ref[...], S, dtb_ref[...], al_ref[...],
                             wide_s(dsk_ref), wide_s(ng_ref), e_ref[...], ec_ref[...])
        for i, ls in enumerate(wide_l):
            y_ref[:, ls] = y[i]
        s_scr[groups] = s_new

    return pl.pallas_call(
        body, name="m2_fwd", grid=(nc, M2_GROUPS // M2_GB),
        in_specs=[wide(0), nar(2 * M2_GROUPS), nar(3 * M2_GROUPS), wide(0), dts, v128, v128, v256, v256, es, ecs],
        out_specs=[wide(0), st],
        out_shape=[jax.ShapeDtypeStruct((L, M2_INNER), F32), jax.ShapeDtypeStruct((nc, M2_GROUPS, M2_STATE, M2_GW), F32)],
        scratch_shapes=[pltpu.VMEM((M2_GROUPS, M2_STATE, M2_GW), F32)],
        compiler_params=_params("arbitrary", "arbitrary"),
    )(xbc, xbc, xbc, p, p, dtb, alog, dsk, ng, e, ecol)


def _m2_bwd(xbc, p, dtb, alog, dsk, ng, e, ecol, sprev, dy):
    L = xbc.shape[0]
    nc = L // CHUNK
    wide, nar, dts, v128, v256, es, ecs, st = _m2_specs(nc, True)

    def body(x_ref, b_ref, c_ref, z_ref, dt_ref, dtb_ref, al_ref, dsk_ref, ng_ref, e_ref, ec_ref, sp_ref, dy_ref,
             dx_ref, db_ref, dc_ref, dz_ref, ddt_ref, dnar_ref, dwide_ref, ds_scr):
        c, g = pl.program_id(0), pl.program_id(1)
        wide_l = [slice(i * M2_GW, (i + 1) * M2_GW) for i in range(M2_GB)]
        nar_l = [slice(i * LANES, (i + 1) * LANES) for i in range(M2_GB)]
        groups = pl.ds(g * M2_GB, M2_GB)
        wide_s = lambda ref: jnp.concatenate([ref[:, ls][None] for ls in wide_l], axis=0)
        nar_s = lambda ref: jnp.concatenate([ref[:, ls][None] for ls in nar_l], axis=0)

        @pl.when(jnp.logical_and(c == 0, g == 0))
        def _():
            dnar_ref[...] = jnp.zeros_like(dnar_ref)

        @pl.when(c == 0)
        def _():
            ds_scr[groups] = jnp.zeros((M2_GB, M2_STATE, M2_GW), F32)
            dwide_ref[groups] = jnp.zeros((M2_GB, SUBLANES, M2_GW), F32)

        @pl.when(g == 0)
        def _():
            ddt_ref[...] = jnp.zeros_like(ddt_ref)

        e_m, ec_m = e_ref[...], ec_ref[...]
        f = lambda x, bm, cm, z, dtr, S, dtb, al, dsk, ng: _m2_chunk(x, bm, cm, z, dtr, S, dtb, al, dsk, ng, e_m, ec_m)
        _, vjp = jax.vjp(f, wide_s(x_ref), nar_s(b_ref), nar_s(c_ref), wide_s(z_ref), dt_ref[...], sp_ref[...], dtb_ref[...],
                         al_ref[...], wide_s(dsk_ref), wide_s(ng_ref))
        dx, db, dc, dz, ddt, ds, ddtb, dal, ddsk, dng = vjp((wide_s(dy_ref), ds_scr[groups]))
        for i in range(M2_GB):
            dx_ref[:, wide_l[i]] = dx[i]
            db_ref[:, nar_l[i]] = db[i]
            dc_ref[:, nar_l[i]] = dc[i]
            dz_ref[:, wide_l[i]] = dz[i]
        ds_scr[groups] = ds
        ddt_ref[...] += ddt
        dnar_ref[...] += jnp.concatenate([ddtb, dal, jnp.zeros((SUBLANES - 2, LANES), F32)], axis=0)
        dwide_ref[groups] += jnp.concatenate([ddsk, dng, jnp.zeros((M2_GB, SUBLANES - 2, M2_GW), F32)], axis=1)

    return pl.pallas_call(
        body, name="m2_bwd", grid=(nc, M2_GROUPS // M2_GB),
        in_specs=[wide(0), nar(2 * M2_GROUPS), nar(3 * M2_GROUPS), wide(0), dts, v128, v128, v256, v256, es, ecs, st, wide(0)],
        out_specs=[wide(0), nar(0), nar(0), wide(0), pl.BlockSpec((CHUNK, LANES), lambda c, g: (nc - 1 - c, 0)),
                   pl.BlockSpec((SUBLANES, LANES), lambda c, g: (0, 0)),
                   pl.BlockSpec((M2_GROUPS, SUBLANES, M2_GW), lambda c, g: (0, 0, 0))],
        out_shape=[jax.ShapeDtypeStruct((L, M2_INNER), F32), jax.ShapeDtypeStruct((L, M2_GROUPS * M2_STATE), F32),
                   jax.ShapeDtypeStruct((L, M2_GROUPS * M2_STATE), F32), jax.ShapeDtypeStruct((L, M2_INNER), F32),
                   jax.ShapeDtypeStruct((L, LANES), F32), jax.ShapeDtypeStruct((SUBLANES, LANES), F32),
                   jax.ShapeDtypeStruct((M2_GROUPS, SUBLANES, M2_GW), F32)],
        scratch_shapes=[pltpu.VMEM((M2_GROUPS, M2_STATE, M2_GW), F32)],
        compiler_params=_params("arbitrary", "arbitrary"),
    )(xbc, xbc, xbc, p, p, dtb, alog, dsk, ng, e, ecol, sprev, dy)


def _m2_selectors():
    e = np.zeros((LANES, M2_INNER), np.float32)
    ecol = np.zeros((LANES, M2_HEADS * LANES), np.float32)
    for h in range(M2_HEADS):
        e[h, M2_HD * h:M2_HD * (h + 1)] = 1.0
        ecol[h, LANES * h:LANES * (h + 1)] = 1.0
    return jnp.asarray(e), jnp.asarray(ecol)


S5_NS = S5_GROUPS * S5_STATE // S5_BLOCKS
S5_ROWS = 256
GELU_C = math.sqrt(2.0 / math.pi)


def _gelu(x):
    return 0.5 * x * (1.0 + jnp.tanh(GELU_C * (x + 0.044715 * x * x * x)))


def _gelu_grad(x):
    t = jnp.tanh(GELU_C * (x + 0.044715 * x * x * x))
    return 0.5 * (1.0 + t) + 0.5 * x * (1.0 - t * t) * GELU_C * (1.0 + 3.0 * 0.044715 * x * x)


def _s5_scan(re_ref, im_ref, pw_re, pw_im, nrows, reverse, states=None):
    n = re_ref.shape[1]
    row = _iota2((SUBLANES, n), 0)
    steps = []
    for d in (1, 2, 4):
        keep = (row < SUBLANES - d) if reverse else (row >= d)
        steps.append(((SUBLANES - d) if reverse else d, jnp.where(keep, pw_re[d - 1:d, :], 0.0), jnp.where(keep, pw_im[d - 1:d, :], 0.0)))
    if reverse:
        cw_re = jnp.concatenate([pw_re[SUBLANES - 1 - k:SUBLANES - k, :] for k in range(SUBLANES)], axis=0)
        cw_im = jnp.concatenate([pw_im[SUBLANES - 1 - k:SUBLANES - k, :] for k in range(SUBLANES)], axis=0)
    else:
        cw_re, cw_im = pw_re, pw_im
    edge = 0 if reverse else SUBLANES - 1
    ngroups = nrows // SUBLANES

    def step(i, carry):
        cr, ci, ar, ai = carry
        gi = (ngroups - 1 - i) if reverse else i
        r0 = pl.multiple_of(gi * SUBLANES, SUBLANES)
        xr, xi = re_ref[pl.ds(r0, SUBLANES), :], im_ref[pl.ds(r0, SUBLANES), :]
        for shift, pr, pi in steps:
            sr, si = pltpu.roll(xr, shift, axis=0), pltpu.roll(xi, shift, axis=0)
            xr, xi = xr + (pr * sr - pi * si), xi + (pr * si + pi * sr)
        xr, xi = xr + (cw_re * cr - cw_im * ci), xi + (cw_re * ci + cw_im * cr)
        re_ref[pl.ds(r0, SUBLANES), :] = xr
        im_ref[pl.ds(r0, SUBLANES), :] = xi
        if states is not None:
            p0 = pl.multiple_of(jnp.maximum(r0 - SUBLANES, 0), SUBLANES)
            live = jnp.where(gi > 0, 1.0, 0.0)
            prev = [jnp.where(row >= 1, pltpu.roll(ref[pl.ds(r0, SUBLANES), :], 1, axis=0),
                              live * pltpu.roll(ref[pl.ds(p0, SUBLANES), :], 1, axis=0)) for ref in states]
            ar, ai = ar + (prev[0] * xr + prev[1] * xi), ai + (prev[0] * xi - prev[1] * xr)
        return (jnp.sum(jnp.where(row == edge, xr, 0.0), axis=0, keepdims=True),
                jnp.sum(jnp.where(row == edge, xi, 0.0), axis=0, keepdims=True), ar, ai)

    z = jnp.zeros((1, n), F32)
    za = jnp.zeros((SUBLANES, n) if states is not None else (1, n), F32)
    _, _, ar, ai = lax.fori_loop(0, ngroups, step, (z, z, za, za))
    return jnp.sum(ar, axis=0, keepdims=True), jnp.sum(ai, axis=0, keepdims=True)


def _s5_project_in(u_ref, bm_ref, re_ref, im_ref, L):
    def step(i, carry):
        r0 = pl.multiple_of(i * S5_ROWS, S5_ROWS)
        bu = _dot(u_ref[pl.ds(r0, S5_ROWS), :], bm_ref[...])
        re_ref[pl.ds(r0, S5_ROWS), :] = bu[:, :S5_NS]
        im_ref[pl.ds(r0, S5_ROWS), :] = bu[:, S5_NS:]
        return carry

    lax.fori_loop(0, L // S5_ROWS, step, 0)


def _s5_specs(L):
    col = pl.BlockSpec((L, LANES), lambda j: (0, j))
    bm = pl.BlockSpec((None, LANES, 2 * S5_NS), lambda j: (j, 0, 0))
    cm = pl.BlockSpec((None, 2 * S5_NS, LANES), lambda j: (j, 0, 0))
    pw = pl.BlockSpec((None, SUBLANES, S5_NS), lambda j: (j, 0, 0))
    vec = pl.BlockSpec((1, LANES), lambda j: (0, j))
    return col, bm, cm, pw, vec


def _s5_fwd(u, bmat, cmat, pw_re, pw_im, dsk):
    L = u.shape[0]
    col, bm, cm, pw, vec = _s5_specs(L)

    def body(u_ref, bm_ref, cm_ref, pr_ref, pi_ref, d_ref, y_ref, re_scr, im_scr):
        _s5_project_in(u_ref, bm_ref, re_scr, im_scr, L)
        _s5_scan(re_scr, im_scr, pr_ref[...], pi_ref[...], L, False)

        def step(i, carry):
            r0 = pl.multiple_of(i * S5_ROWS, S5_ROWS)
            rows = pl.ds(r0, S5_ROWS)
            y = _dot(re_scr[rows, :], cm_ref[:S5_NS, :]) + _dot(im_scr[rows, :], cm_ref[S5_NS:, :]) + d_ref[...] * u_ref[rows, :]
            y_ref[rows, :] = _gelu(y)
            return carry

        lax.fori_loop(0, L // S5_ROWS, step, 0)

    return pl.pallas_call(
        body, name="s5_fwd", grid=(S5_BLOCKS,),
        in_specs=[col, bm, cm, pw, pw, vec], out_specs=col,
        out_shape=jax.ShapeDtypeStruct((L, D_MODEL), F32),
        scratch_shapes=[pltpu.VMEM((L, S5_NS), F32)] * 2,
        compiler_params=_params("parallel"),
    )(u, bmat, cmat, pw_re, pw_im, dsk)


def _s5_bwd(u, bmat, cmat, pw_re, pw_im, dsk, dyg):
    L = u.shape[0]
    col, bm, cm, pw, vec = _s5_specs(L)

    def body(u_ref, bm_ref, cm_ref, pr_ref, pi_ref, d_ref, dy_ref, du_ref, dbm_ref, dcm_ref, dlam_ref, dd_ref,
             re_scr, im_scr, gr_scr, gi_scr, dyp_scr):
        _s5_project_in(u_ref, bm_ref, re_scr, im_scr, L)
        _s5_scan(re_scr, im_scr, pr_ref[...], pi_ref[...], L, False)

        def step(i, carry):
            dcr, dci, dd = carry
            r0 = pl.multiple_of(i * S5_ROWS, S5_ROWS)
            rows = pl.ds(r0, S5_ROWS)
            sr, si, uu = re_scr[rows, :], im_scr[rows, :], u_ref[rows, :]
            y = _dot(sr, cm_ref[:S5_NS, :]) + _dot(si, cm_ref[S5_NS:, :]) + d_ref[...] * uu
            dyp = dy_ref[rows, :] * _gelu_grad(y)
            dyp_scr[rows, :] = dyp
            gr_scr[rows, :] = _nt(dyp, cm_ref[:S5_NS, :])
            gi_scr[rows, :] = _nt(dyp, cm_ref[S5_NS:, :])
            return dcr + _tn(sr, dyp), dci + _tn(si, dyp), dd + jnp.sum(dyp * uu, axis=0, keepdims=True)

        zc = jnp.zeros((S5_NS, LANES), F32)
        dcr, dci, dd = lax.fori_loop(0, L // S5_ROWS, step, (zc, zc, jnp.zeros((1, LANES), F32)))
        dcm_ref[:S5_NS, :] = dcr
        dcm_ref[S5_NS:, :] = dci
        dd_ref[...] = dd

        ar, ai = _s5_scan(gr_scr, gi_scr, pr_ref[...], -pi_ref[...], L, True, states=(re_scr, im_scr))
        dlam_ref[...] = jnp.concatenate([ar, ai, jnp.zeros((SUBLANES - 2, S5_NS), F32)], axis=0)

        def in_step(i, carry):
            dbr, dbi = carry
            r0 = pl.multiple_of(i * S5_ROWS, S5_ROWS)
            rows = pl.ds(r0, S5_ROWS)
            gr, gi, uu = gr_scr[rows, :], gi_scr[rows, :], u_ref[rows, :]
            du_ref[rows, :] = dyp_scr[rows, :] * d_ref[...] + _nt(gr, bm_ref[:, :S5_NS]) + _nt(gi, bm_ref[:, S5_NS:])
            return dbr + _tn(uu, gr), dbi + _tn(uu, gi)

        zb = jnp.zeros((LANES, S5_NS), F32)
        dbr, dbi = lax.fori_loop(0, L // S5_ROWS, in_step, (zb, zb))
        dbm_ref[:, :S5_NS] = dbr
        dbm_ref[:, S5_NS:] = dbi

    return pl.pallas_call(
        body, name="s5_bwd", grid=(S5_BLOCKS,),
        in_specs=[col, bm, cm, pw, pw, vec, col], out_specs=[col, bm, cm, pw, vec],
        out_shape=[jax.ShapeDtypeStruct((L, D_MODEL), F32), jax.ShapeDtypeStruct((S5_BLOCKS, LANES, 2 * S5_NS), F32),
                   jax.ShapeDtypeStruct((S5_BLOCKS, 2 * S5_NS, LANES), F32),
                   jax.ShapeDtypeStruct((S5_BLOCKS, SUBLANES, S5_NS), F32), jax.ShapeDtypeStruct((1, D_MODEL), F32)],
        scratch_shapes=[pltpu.VMEM((L, S5_NS), F32)] * 4 + [pltpu.VMEM((L, LANES), F32)],
        compiler_params=_params("parallel"),
    )(u, bmat, cmat, pw_re, pw_im, dsk, dyg)


def _s5_discretize(lam_re, lam_im, log_dt, b_re, b_im, e16):
    dt = jnp.exp(log_dt)
    zr, zi = lam_re * dt, lam_im * dt
    mag = jnp.exp(zr)
    lbr, lbi = mag * jnp.cos(zi), mag * jnp.sin(zi)
    den = lam_re * lam_re + lam_im * lam_im
    nr, ni = lbr - 1.0, lbi
    cr = (nr * lam_re + ni * lam_im) / den
    ci = (ni * lam_re - nr * lam_im) / den
    crw, ciw = _pick(cr, e16), _pick(ci, e16)
    return lbr, lbi, crw * b_re - ciw * b_im, crw * b_im + ciw * b_re


def _s5_params_fwd(lam_re, lam_im, log_dt, b_re, b_im, e16):
    def body(lr, li, ld, br, bi, e, o1, o2, o3, o4):
        for o, val in zip((o1, o2, o3, o4), _s5_discretize(lr[...], li[...], ld[...], br[...], bi[...], e[...])):
            o[...] = val

    g, p, n = S5_GROUPS, S5_STATE, S5_STATE * S5_GROUP
    return pl.pallas_call(
        body, name="s5_params_fwd",
        out_shape=[jax.ShapeDtypeStruct((g, p), F32)] * 2 + [jax.ShapeDtypeStruct((g, n), F32)] * 2,
        compiler_params=_params(),
    )(lam_re, lam_im, log_dt, b_re, b_im, e16)


def _s5_params_bwd(lam_re, lam_im, log_dt, b_re, b_im, e16, cts):
    def body(lr, li, ld, br, bi, e, c1, c2, c3, c4, o1, o2, o3, o4, o5):
        e_m = e[...]
        f = lambda a, b, c, d, g: _s5_discretize(a, b, c, d, g, e_m)
        _, vjp = jax.vjp(f, lr[...], li[...], ld[...], br[...], bi[...])
        for o, val in zip((o1, o2, o3, o4, o5), vjp((c1[...], c2[...], c3[...], c4[...]))):
            o[...] = val

    g, p, n = S5_GROUPS, S5_STATE, S5_STATE * S5_GROUP
    return pl.pallas_call(
        body, name="s5_params_bwd",
        out_shape=[jax.ShapeDtypeStruct((g, p), F32)] * 2 + [jax.ShapeDtypeStruct((g, 1), F32)]
        + [jax.ShapeDtypeStruct((g, n), F32)] * 2,
        compiler_params=_params(),
    )(lam_re, lam_im, log_dt, b_re, b_im, e16, *cts)


def _add_residual(acc, h):
    return (acc + h,)


def _mlp_fwd(i, h, g, w1, w2):
    hn = _rms_fwd(f"mlp{i}_norm", h, g)
    r = _mm(f"mlp{i}_up", hn, w1, "nn", (BF16,), epi=lambda acc: (jnp.square(jnp.maximum(acc, 0.0)),))
    return _mm(f"mlp{i}_down", r, w2, "nn", (F32,), epi=_add_residual, extras=(h,)), (h, hn, r)


def _mlp_bwd(i, dh_out, saved, g, w1, w2, dw):
    h, hn, r = saved
    dw(("mlp_w2", i), f"mlp{i}_dw2", r, dh_out)
    da = _mm(f"mlp{i}_da", dh_out, w2, "nt", (BF16,), epi=lambda acc, rr: (acc * (2.0 * jnp.sqrt(rr.astype(F32))),), extras=(r,))
    dw(("mlp_w1", i), f"mlp{i}_dw1", hn, da)
    dhn = _mm(f"mlp{i}_dhn", da, w1, "nt", (F32,))
    dh, dg = _rms_bwd(f"mlp{i}_dnorm", h, g, dhn, dh_out)
    return dh, dg[0]


def _lanes(v, n):
    return jnp.broadcast_to(v.reshape(n, 1, 1), (n, 1, LANES))


def _gdn_fwd_layer(i, h, g, w_in, conv_w, a_log, dt_bias, o_g, w_out):
    hn = _rms_fwd(f"gdn{i}_norm", h, g)
    p = _mm(f"gdn{i}_in", hn, w_in, "nn", (F32,))
    zb = jnp.zeros((1, D_MODEL), F32)
    qkv = [_conv_fwd(f"gdn{i}_conv{t}", p, t * D_MODEL, conv_w[:, t * D_MODEL:(t + 1) * D_MODEL], zb) for t in range(3)]
    ea, eb = _gdn_selectors()
    y, sprev, sol = _gdn_fwd(*qkv, p, _lanes(a_log, GDN_HEADS), _lanes(dt_bias, GDN_HEADS), o_g.reshape(1, LANES), ea, eb)
    return _mm(f"gdn{i}_out", y, w_out, "nn", (F32,), epi=_add_residual, extras=(h,)), (h, hn, p, qkv, y, sprev, sol)


def _gdn_bwd_layer(i, dh_out, saved, g, w_in, conv_w, a_log, dt_bias, o_g, w_out, dw):
    h, hn, p, qkv, y, sprev, sol = saved
    dy = _mm(f"gdn{i}_dy", dh_out, w_out, "nt", (F32,))
    dw(("gdn_w_out", MIXER_INDEX[i]), f"gdn{i}_dwout", y, dh_out)
    ea, eb = _gdn_selectors()
    dq, dk, dv, dgate, dab, dpar = _gdn_bwd(*qkv, p, _lanes(a_log, GDN_HEADS), _lanes(dt_bias, GDN_HEADS),
                                            o_g.reshape(1, LANES), ea, eb, sprev, sol, dy)
    zb = jnp.zeros((1, D_MODEL), F32)
    dpre, dcw = [], []
    for t, d in enumerate((dq, dk, dv)):
        dx, dwc, _ = _conv_bwd(f"gdn{i}_dconv{t}", p, t * D_MODEL, conv_w[:, t * D_MODEL:(t + 1) * D_MODEL], zb, d)
        dpre.append(dx)
        dcw.append(dwc)
    dp = jnp.concatenate(dpre + [dgate, dab], axis=1).astype(BF16)
    dw(("gdn_w_in", MIXER_INDEX[i]), f"gdn{i}_dwin", hn, dp)
    dhn = _mm(f"gdn{i}_dhn", dp, w_in, "nt", (F32,))
    dh, dg = _rms_bwd(f"gdn{i}_dnorm", h, g, dhn, dh_out)
    grads = dict(conv_w=jnp.concatenate(dcw, axis=1), a_log=jnp.sum(dpar[:, 0, :], axis=-1),
                 dt_bias=jnp.sum(dpar[:, 1, :], axis=-1), o_norm_g=jnp.sum(dpar[:, 2, :], axis=0))
    return dh, dg[0], grads


def _m2_vectors(dt_bias, a_log, d_skip, norm_g):
    pad = lambda v: jnp.pad(v, (0, LANES - M2_HEADS)).reshape(1, LANES)
    return pad(dt_bias), pad(a_log), jnp.repeat(d_skip, M2_HD).reshape(1, M2_INNER), norm_g.reshape(1, M2_INNER)


def _m2_fwd_layer(h, g, w_in, conv_w, conv_b, dt_bias, a_log, d_skip, norm_g, w_out):
    hn = _rms_fwd("m2_norm", h, g)
    p = _mm("m2_in", hn, w_in, "nn", (F32,))
    xbc = _conv_fwd("m2_conv", p, M2_INNER, conv_w, conv_b.reshape(1, M2_CONV_CH))
    e, ecol = _m2_selectors()
    y, sprev = _m2_fwd(xbc, p, *_m2_vectors(dt_bias, a_log, d_skip, norm_g), e, ecol)
    return _mm("m2_out", y, w_out, "nn", (F32,), epi=_add_residual, extras=(h,)), (h, hn, p, xbc, y, sprev)


def _m2_bwd_layer(dh_out, saved, g, w_in, conv_w, conv_b, dt_bias, a_log, d_skip, norm_g, w_out, dw):
    h, hn, p, xbc, y, sprev = saved
    dy = _mm("m2_dy", dh_out, w_out, "nt", (F32,))
    dw(("m2_w_out", 0), "m2_dwout", y, dh_out)
    e, ecol = _m2_selectors()
    dx, db, dc, dz, ddt, dnar, dwide = _m2_bwd(xbc, p, *_m2_vectors(dt_bias, a_log, d_skip, norm_g), e, ecol, sprev, dy)
    dxbc, dcw, dcb = _conv_bwd("m2_dconv", p, M2_INNER, conv_w, conv_b.reshape(1, M2_CONV_CH),
                               jnp.concatenate([dx, db, dc], axis=1))
    dp = jnp.concatenate([dz, dxbc, ddt], axis=1).astype(BF16)
    dw(("m2_w_in", 0), "m2_dwin", hn, dp)
    dhn = _mm("m2_dhn", dp, w_in, "nt", (F32,))
    dh, dg = _rms_bwd("m2_dnorm", h, g, dhn, dh_out)
    grads = dict(conv_w=dcw, conv_b=dcb[0], dt_bias=dnar[0, :M2_HEADS], a_log=dnar[1, :M2_HEADS],
                 d=jnp.sum(dwide[:, 0, :].reshape(M2_HEADS, M2_HD), axis=-1), norm_g=dwide[:, 1, :].reshape(M2_INNER))
    return dh, dg[0], grads


def _s5_selector():
    e16 = np.zeros((S5_STATE, S5_STATE * S5_GROUP), np.float32)
    for p in range(S5_STATE):
        e16[p, p * S5_GROUP:(p + 1) * S5_GROUP] = 1.0
    return jnp.asarray(e16)


def _s5_operands(lbr, lbi, bbr, bbi, c_re, c_im):
    eye = jnp.eye(S5_BLOCKS, dtype=F32)
    gpb = S5_GROUPS // S5_BLOCKS
    bd = lambda t: jnp.einsum("jgpk,gh->jgkhp", t.reshape(S5_BLOCKS, gpb, S5_STATE, S5_GROUP), eye).reshape(S5_BLOCKS, LANES, S5_NS)
    cd = lambda t: jnp.einsum("jgkp,gh->jgphk", t.reshape(S5_BLOCKS, gpb, S5_GROUP, S5_STATE), eye).reshape(S5_BLOCKS, S5_NS, LANES)
    bmat = jnp.concatenate([bd(bbr), bd(bbi)], axis=2).astype(BF16)
    cmat = jnp.concatenate([cd(c_re), -cd(c_im)], axis=1).astype(BF16)
    ar, ai = lbr.reshape(S5_BLOCKS, S5_NS), lbi.reshape(S5_BLOCKS, S5_NS)
    pr, pi = [ar], [ai]
    for _ in range(SUBLANES - 1):
        pr, pi = pr + [pr[-1] * ar - pi[-1] * ai], pi + [pr[-1] * ai + pi[-1] * ar]
    return bmat, cmat, jnp.stack(pr, axis=1), jnp.stack(pi, axis=1)


def _s5_fwd_layer(h, g, w_in, lam_re, lam_im, log_dt, b_re, b_im, c_re, c_im, d_skip, w_out):
    hn = _rms_fwd("s5_norm", h, g)
    u = _mm("s5_in", hn, w_in, "nn", (F32,))
    n = S5_STATE * S5_GROUP
    lbr, lbi, bbr, bbi = _s5_params_fwd(lam_re, lam_im, log_dt.reshape(S5_GROUPS, 1), b_re.reshape(S5_GROUPS, n),
                                        b_im.reshape(S5_GROUPS, n), _s5_selector())
    ops = _s5_operands(lbr, lbi, bbr, bbi, c_re, c_im)
    yg = _s5_fwd(u, *ops, d_skip.reshape(1, D_MODEL))
    ag = _mm("s5_out", yg, w_out, "nn", (F32,))
    return _glu_fwd(h, ag), (h, hn, u, ops, yg, ag)


def _s5_bwd_layer(dh_out, saved, g, w_in, lam_re, lam_im, log_dt, b_re, b_im, c_re, c_im, d_skip, w_out, dw):
    h, hn, u, ops, yg, ag = saved
    dag = _glu_bwd(dh_out, ag)
    dw(("s5_w_out", 0), "s5_dwout", yg, dag)
    dyg = _mm("s5_dyg", dag, w_out, "nt", (F32,))
    du, dbmat, dcmat, dlam, ddsk = _s5_bwd(u, *ops, d_skip.reshape(1, D_MODEL), dyg)
    eye = jnp.eye(S5_BLOCKS, dtype=F32)
    gpb = S5_GROUPS // S5_BLOCKS
    n = S5_STATE * S5_GROUP
    ub = lambda t: jnp.einsum("jgkhp,gh->jgpk", t.reshape(S5_BLOCKS, gpb, S5_GROUP, gpb, S5_STATE), eye).reshape(S5_GROUPS, n)
    uc = lambda t: jnp.einsum("jgphk,gh->jgkp", t.reshape(S5_BLOCKS, gpb, S5_STATE, gpb, S5_GROUP), eye).reshape(c_re.shape)
    cts = (dlam[:, 0, :].reshape(S5_GROUPS, S5_STATE), dlam[:, 1, :].reshape(S5_GROUPS, S5_STATE),
           ub(dbmat[:, :, :S5_NS]), ub(dbmat[:, :, S5_NS:]))
    dlr, dli, dld, dbr, dbi = _s5_params_bwd(lam_re, lam_im, log_dt.reshape(S5_GROUPS, 1), b_re.reshape(S5_GROUPS, n),
                                             b_im.reshape(S5_GROUPS, n), _s5_selector(), cts)
    dw(("s5_w_in", 0), "s5_dwin", hn, du)
    dhn = _mm("s5_dhn", du, w_in, "nt", (F32,))
    dh, dg = _rms_bwd("s5_dnorm", h, g, dhn, dh_out)
    grads = dict(lam_re=dlr, lam_im=dli, log_dt=dld[:, 0], b_re=dbr.reshape(b_re.shape), b_im=dbi.reshape(b_im.shape),
                 c_re=uc(dcmat[:, :S5_NS, :]), c_im=-uc(dcmat[:, S5_NS:, :]), d=ddsk[0])
    return dh, dg[0], grads


MIXER_OF_LAYER = ("gdn", "s5", "m2", "gdn")
MIXER_INDEX = (0, 0, 0, 1)


def _mixer_args(W, i):
    kind, j = MIXER_OF_LAYER[i], MIXER_INDEX[i]
    if kind == "gdn":
        return tuple(W["gdn_" + k][j] for k in ("w_in", "conv_w", "a_log", "dt_bias", "o_norm_g", "w_out"))
    if kind == "s5":
        return tuple(W["s5_" + k][j] for k in ("w_in", "lam_re", "lam_im", "log_dt", "b_re", "b_im", "c_re", "c_im", "d", "w_out"))
    return tuple(W["m2_" + k][j] for k in ("w_in", "conv_w", "conv_b", "dt_bias", "a_log", "d", "norm_g", "w_out"))


def _local_step(x, target, W, dw, on_layer_done):
    h = x
    saved = []
    for i in range(DEPTH):
        kind = MIXER_OF_LAYER[i]
        args = _mixer_args(W, i)
        if kind == "gdn":
            h, sm = _gdn_fwd_layer(i, h, W["norm_mix_g"][i], *args)
        elif kind == "s5":
            h, sm = _s5_fwd_layer(h, W["norm_mix_g"][i], *args)
        else:
            h, sm = _m2_fwd_layer(h, W["norm_mix_g"][i], *args)
        h, sp = _mlp_fwd(i, h, W["norm_mlp_g"][i], W["mlp_w1"][i], W["mlp_w2"][i])
        saved.append((sm, sp))
    loss, dh, dgf = _loss_head(h, W["final_norm_g"], target)
    G = {"final_norm_g": dgf[0], "norm_mix_g": [None] * DEPTH, "norm_mlp_g": [None] * DEPTH}
    mix = {}
    for i in reversed(range(DEPTH)):
        kind = MIXER_OF_LAYER[i]
        sm, sp = saved[i]
        dh, G["norm_mlp_g"][i] = _mlp_bwd(i, dh, sp, W["norm_mlp_g"][i], W["mlp_w1"][i], W["mlp_w2"][i], dw)
        args = _mixer_args(W, i)
        if kind == "gdn":
            dh, G["norm_mix_g"][i], gm = _gdn_bwd_layer(i, dh, sm, W["norm_mix_g"][i], *args, dw)
        elif kind == "s5":
            dh, G["norm_mix_g"][i], gm = _s5_bwd_layer(dh, sm, W["norm_mix_g"][i], *args, dw)
        else:
            dh, G["norm_mix_g"][i], gm = _m2_bwd_layer(dh, sm, W["norm_mix_g"][i], *args, dw)
        j = MIXER_INDEX[i]
        on_layer_done(i)
        for k, v in gm.items():
            mix.setdefault(kind + "_" + k, {})[j] = v
    for k, d in mix.items():
        G[k] = [d[j] for j in sorted(d)]
    return loss, dh, {k: jnp.stack(v) if isinstance(v, list) else v for k, v in G.items()}


ADAM_ROWS = 128
ADAM_COLS = 128


def _adamw(name, w, g, m, v):
    R, C = w.shape
    if R % ADAM_ROWS == 0:
        grid, blk = (R // ADAM_ROWS,), pl.BlockSpec((ADAM_ROWS, C), lambda i: (i, 0))
    else:
        grid, blk = (C // ADAM_COLS,), pl.BlockSpec((R, ADAM_COLS), lambda j: (0, j))

    def body(w_ref, g_ref, m_ref, v_ref, d_ref, mo_ref, vo_ref):
        gg = g_ref[...]
        mn = ADAM_B1 * m_ref[...] + (1.0 - ADAM_B1) * gg
        vn = ADAM_B2 * v_ref[...] + (1.0 - ADAM_B2) * (gg * gg)
        m_hat = mn / (1.0 - ADAM_B1 ** ADAM_STEP)
        v_hat = vn / (1.0 - ADAM_B2 ** ADAM_STEP)
        d_ref[...] = -ADAM_LR * (m_hat / (jnp.sqrt(v_hat) + ADAM_EPS) + ADAM_WD * w_ref[...])
        mo_ref[...] = mn
        vo_ref[...] = vn

    return pl.pallas_call(
        body, name=name, grid=grid, in_specs=[blk] * 4, out_specs=[blk] * 3,
        out_shape=[jax.ShapeDtypeStruct((R, C), F32)] * 3, compiler_params=_params("parallel"),
    )(w, g, m, v)


MESH = pl.DeviceIdType.MESH
ANY = pl.BlockSpec(memory_space=pl.ANY)
N_CHIPS = 4
N_DEV = 8


def _position():
    return lax.axis_index("x"), lax.axis_index("y"), lax.axis_index("c")


GATHER_IDS = {1: 1, 2: 2}
EXCHANGE_IDS = {0: 4, 1: 5, 2: 6, 3: 7}


LINK_SLOWDOWN = 40


def _link_cost(link_bytes):
    return pl.CostEstimate(flops=0, transcendentals=0, bytes_accessed=LINK_SLOWDOWN * link_bytes)


def _gather_body(w_refs, out_refs, send_sems, recv_sems):
    x, y, c = _position()
    sibling = (x, y, 1 - c)
    chips = [(1 - x, y), (x, 1 - y), (1 - x, 1 - y)]
    firsts, passes = [], []
    for t, (w_ref, out_ref) in enumerate(zip(w_refs, out_refs)):
        half = w_ref.shape[0] // 2

        def piece(cx, cy, hc, out_ref=out_ref, half=half):
            return out_ref.at[2 * cx + cy, pl.ds(hc * half, half), :]

        def copy(k, src, dst, to, t=t):
            return pltpu.make_async_remote_copy(src_ref=src, dst_ref=dst, send_sem=send_sems.at[6 * t + k],
                                                recv_sem=recv_sems.at[6 * t + k], device_id=to, device_id_type=MESH)

        first = [copy(j, w_ref.at[pl.ds(c * half, half), :], piece(x, y, c), (*chip, c)) for j, chip in enumerate(chips)]
        for cp in first:
            cp.start()
        firsts.append((first, piece, copy))
    for first, piece, copy in firsts:
        passed = [copy(3 + j, piece(*chip, c), piece(*chip, c), sibling) for j, chip in enumerate(chips)]
        for j, chip in enumerate(chips):
            copy(j, piece(*chip, c), piece(*chip, c), sibling).wait_recv()
            passed[j].start()
        passes.append(passed)
    for (first, piece, copy), passed in zip(firsts, passes):
        for j, chip in enumerate(chips):
            copy(3 + j, piece(*chip, 1 - c), piece(*chip, 1 - c), sibling).wait_recv()
        for cp in first + passed:
            cp.wait_send()


def _gather_shards(wps):
    n = len(wps)

    def body(*refs):
        _gather_body(refs[:n], refs[n:2 * n], *refs[2 * n:])

    return pl.pallas_call(
        body, name="gather_shards", in_specs=[ANY] * n, out_specs=[ANY] * n,
        out_shape=[jax.ShapeDtypeStruct((N_CHIPS, *wp.shape), wp.dtype) for wp in wps],
        scratch_shapes=[pltpu.SemaphoreType.DMA((6 * n,)), pltpu.SemaphoreType.DMA((6 * n,))],
    )(*wps)


def _gather_shards_later(wps, part):
    n = len(wps)
    w_refs = [jax.new_ref(wp, memory_space=pltpu.MemorySpace.HBM) for wp in wps]
    out_refs = [jax.empty_ref(jax.ShapeDtypeStruct((N_CHIPS, *wp.shape), wp.dtype), memory_space=pltpu.MemorySpace.HBM)
                for wp in wps]

    @pl.kernel(mesh=plsc.ScalarSubcoreMesh(axis_name="sequencer", num_cores=1), name=f"gather_shards_later{part}",
               scratch_types=(pltpu.SemaphoreType.DMA((6 * n,)), pltpu.SemaphoreType.DMA((6 * n,))),
               cost_estimate=_link_cost(3 * sum(wp.size * wp.dtype.itemsize for wp in wps)),
               compiler_params=pltpu.CompilerParams(collective_id=GATHER_IDS[part]))
    def launch(send_sems, recv_sems):
        x, y, c = _position()
        barrier = pltpu.get_barrier_semaphore()
        for peer in [(x, y, 1 - c), (1 - x, y, c), (x, 1 - y, c), (1 - x, 1 - y, c)]:
            pl.semaphore_signal(barrier, inc=1, device_id=peer, device_id_type=MESH)
        pl.semaphore_wait(barrier, 4)
        _gather_body(w_refs, out_refs, send_sems, recv_sems)

    launch()
    return [r[...] for r in out_refs]


def _pair_exchange(name, gps):
    n = len(gps)

    def body(*refs):
        g_refs, out_refs, (send_sems, recv_sems) = refs[:n], refs[n:2 * n], refs[2 * n:]
        x, y, c = _position()
        copies = []
        for t, (g_ref, out_ref) in enumerate(zip(g_refs, out_refs)):
            half = g_ref.shape[1] // 2
            copies += [pltpu.make_async_remote_copy(
                src_ref=g_ref.at[k, pl.ds((1 - c) * half, half), :], dst_ref=out_ref.at[k], send_sem=send_sems.at[N_CHIPS * t + k],
                recv_sem=recv_sems.at[N_CHIPS * t + k], device_id=(x, y, 1 - c), device_id_type=MESH) for k in range(N_CHIPS)]
        for cp in copies:
            cp.start()
        for cp in copies:
            cp.wait()

    return pl.pallas_call(
        body, name=name, in_specs=[ANY] * n, out_specs=[ANY] * n,
        out_shape=[jax.ShapeDtypeStruct((N_CHIPS, gp.shape[1] // 2, gp.shape[2]), gp.dtype) for gp in gps],
        scratch_shapes=[pltpu.SemaphoreType.DMA((N_CHIPS * n,)), pltpu.SemaphoreType.DMA((N_CHIPS * n,))],
    )(*gps)


SUM_ROWS = (1280, 1152, 1024, 512, 256, 128)


def _pair_sum(name, gp, got, core):
    n, R, C = gp.shape
    half = R // 2
    tr = _tile(half, SUM_ROWS)
    nb = half // tr

    def body(core_ref, g_ref, r_ref, o_ref):
        o_ref[...] = (g_ref[...].astype(F32) + r_ref[...].astype(F32)).astype(o_ref.dtype)

    return pl.pallas_call(
        body, name=name,
        grid_spec=pltpu.PrefetchScalarGridSpec(
            num_scalar_prefetch=1, grid=(n, nb),
            in_specs=[pl.BlockSpec((None, tr, C), lambda k, i, core_ref: (k, core_ref[0] * nb + i, 0)),
                      pl.BlockSpec((None, tr, C), lambda k, i, core_ref: (k, i, 0))],
            out_specs=pl.BlockSpec((None, tr, C), lambda k, i, core_ref: (k, i, 0))),
        out_shape=jax.ShapeDtypeStruct((n, half, C), gp.dtype), compiler_params=_params("parallel", "parallel"),
    )(core, gp, got)


def _chip_exchange_body(t_refs, out_refs, send_sems, recv_sems):
    x, y, c = _position()
    chips = [(1 - x, y), (x, 1 - y), (1 - x, 1 - y)]
    copies, waits = [], []
    for t, (t_ref, out_ref) in enumerate(zip(t_refs, out_refs)):
        for j, (cx, cy) in enumerate(chips):
            sems = dict(send_sem=send_sems.at[3 * t + j], recv_sem=recv_sems.at[3 * t + j], device_id=(cx, cy, c),
                        device_id_type=MESH)
            copies.append(pltpu.make_async_remote_copy(src_ref=t_ref.at[2 * cx + cy], dst_ref=out_ref.at[2 * x + y], **sems))
            waits.append(pltpu.make_async_remote_copy(src_ref=t_ref.at[2 * cx + cy], dst_ref=out_ref.at[2 * cx + cy], **sems))
    for cp in copies:
        cp.start()
    for cp in waits:
        cp.wait_recv()
    for cp in copies:
        cp.wait_send()


def _chip_exchange_later(ts, layer):
    n = len(ts)
    t_refs = [jax.new_ref(t, memory_space=pltpu.MemorySpace.HBM) for t in ts]
    out_refs = [jax.empty_ref(jax.ShapeDtypeStruct(t.shape, t.dtype), memory_space=pltpu.MemorySpace.HBM) for t in ts]

    @pl.kernel(mesh=plsc.ScalarSubcoreMesh(axis_name="sequencer", num_cores=1), name=f"chip_exchange_later{layer}",
               scratch_types=(pltpu.SemaphoreType.DMA((3 * n,)), pltpu.SemaphoreType.DMA((3 * n,))),
               cost_estimate=_link_cost(3 * sum(t.size * t.dtype.itemsize for t in ts) // N_CHIPS),
               compiler_params=pltpu.CompilerParams(collective_id=EXCHANGE_IDS[layer]))
    def launch(send_sems, recv_sems):
        x, y, c = _position()
        barrier = pltpu.get_barrier_semaphore()
        for peer in [(1 - x, y, c), (x, 1 - y, c), (1 - x, 1 - y, c)]:
            pl.semaphore_signal(barrier, inc=1, device_id=peer, device_id_type=MESH)
        pl.semaphore_wait(barrier, 3)
        _chip_exchange_body(t_refs, out_refs, send_sems, recv_sems)

    launch()
    return [r[...] for r in out_refs]


def _chip_sum(name, t, got, ids):
    n, H, C = t.shape
    tr = _tile(H, SUM_ROWS)
    nb = H // tr

    def body(ids_ref, t_ref, r_ref, o_ref):
        own = t_ref[...].astype(F32)
        acc = jnp.where(ids_ref[0] == 0, own, r_ref[0].astype(F32))
        for k in range(1, n):
            acc = acc + jnp.where(ids_ref[0] == k, own, r_ref[k].astype(F32))
        o_ref[...] = acc

    return pl.pallas_call(
        body, name=name,
        grid_spec=pltpu.PrefetchScalarGridSpec(
            num_scalar_prefetch=1, grid=(nb,),
            in_specs=[pl.BlockSpec((None, tr, C), lambda i, ids_ref: (ids_ref[0], i, 0)),
                      pl.BlockSpec((n, tr, C), lambda i, ids_ref: (0, i, 0))],
            out_specs=pl.BlockSpec((tr, C), lambda i, ids_ref: (ids_ref[1] * nb + i, 0))),
        out_shape=jax.ShapeDtypeStruct((2 * H, C), F32), compiler_params=_params("parallel"),
    )(ids, t, got)


def _sum_pieces(name, pieces):
    n, R, C = pieces.shape
    tr = _tile(R, (256, 128, SUBLANES))

    def body(p_ref, o_ref):
        acc = p_ref[0].astype(F32)
        for s in range(1, n):
            acc = acc + p_ref[s].astype(F32)
        o_ref[...] = acc

    return pl.pallas_call(
        body, name=name, grid=(R // tr,),
        in_specs=[pl.BlockSpec((n, tr, C), lambda i: (0, i, 0))], out_specs=pl.BlockSpec((tr, C), lambda i: (i, 0)),
        out_shape=jax.ShapeDtypeStruct((R, C), F32), compiler_params=_params("parallel"),
    )(pieces)


def _swap_halves(name, ss):
    n = len(ss)

    def body(*refs):
        s_refs, out_refs, (send_sems, recv_sems) = refs[:n], refs[n:2 * n], refs[2 * n:]
        x, y, c = _position()
        copies, waits = [], []
        for t, (s_ref, out_ref) in enumerate(zip(s_refs, out_refs)):
            half = s_ref.shape[0] // 2
            sems = dict(send_sem=send_sems.at[t], recv_sem=recv_sems.at[t], device_id=(x, y, 1 - c), device_id_type=MESH)
            mine = s_ref.at[pl.ds(c * half, half), :]
            copies.append(pltpu.make_async_remote_copy(src_ref=mine, dst_ref=out_ref.at[pl.ds(c * half, half), :], **sems))
            waits.append(pltpu.make_async_remote_copy(src_ref=mine, dst_ref=out_ref.at[pl.ds((1 - c) * half, half), :], **sems))
        for cp in copies:
            cp.start()
        for cp in waits:
            cp.wait_recv()
        for cp in copies:
            cp.wait_send()

    return pl.pallas_call(
        body, name=name, in_specs=[ANY] * n, out_specs=[ANY] * n, input_output_aliases={i: i for i in range(n)},
        out_shape=[jax.ShapeDtypeStruct(s_.shape, s_.dtype) for s_ in ss],
        scratch_shapes=[pltpu.SemaphoreType.DMA((n,)), pltpu.SemaphoreType.DMA((n,))],
    )(*ss)


def _gather_small(name, blk):
    m_per, n = blk.shape

    def body(x_ref, out_ref, send_sems, recv_sems, local_sem):
        x, y, c = _position()
        me, sibling = (x, y, c), (x, y, 1 - c)
        chips = [(1 - x, y), (x, 1 - y), (1 - x, 1 - y)]

        def rows(px, py, pc):
            return out_ref.at[pl.ds((4 * px + 2 * py + pc) * m_per, m_per), :]

        def copy(k, block, to, src=None):
            return pltpu.make_async_remote_copy(src_ref=rows(*block) if src is None else src, dst_ref=rows(*block),
                                                send_sem=send_sems.at[k], recv_sem=recv_sems.at[k], device_id=to, device_id_type=MESH)

        mine = pltpu.make_async_copy(x_ref, rows(*me), local_sem)
        mine.start()
        first = [copy(0, me, sibling, src=x_ref)] + [copy(1 + j, me, (*chip, c), src=x_ref) for j, chip in enumerate(chips)]
        for cp in first:
            cp.start()
        passed = [copy(4 + j, (*chip, c), sibling) for j, chip in enumerate(chips)]
        for j, chip in enumerate(chips):
            copy(1 + j, (*chip, c), me).wait_recv()
            passed[j].start()
        copy(0, sibling, me).wait_recv()
        for j, chip in enumerate(chips):
            copy(4 + j, (*chip, 1 - c), me).wait_recv()
        for cp in first + passed:
            cp.wait_send()
        mine.wait()

    return pl.pallas_call(
        body, name=name, out_shape=jax.ShapeDtypeStruct((N_DEV * m_per, n), blk.dtype),
        in_specs=[pl.BlockSpec(memory_space=pltpu.VMEM)], out_specs=pl.BlockSpec(memory_space=pltpu.VMEM),
        scratch_shapes=[pltpu.SemaphoreType.DMA((7,)), pltpu.SemaphoreType.DMA((7,)), pltpu.SemaphoreType.DMA],
        compiler_params=pltpu.CompilerParams(vmem_limit_bytes=VMEM_LIMIT_BYTES),
    )(blk)


WEIGHTS = ("norm_mix_g", "norm_mlp_g", "mlp_w1", "mlp_w2", "gdn_w_in", "gdn_conv_w", "gdn_a_log", "gdn_dt_bias", "gdn_o_norm_g",
           "gdn_w_out", "s5_w_in", "s5_lam_re", "s5_lam_im", "s5_log_dt", "s5_b_re", "s5_b_im", "s5_c_re", "s5_c_im", "s5_d",
           "s5_w_out", "m2_w_in", "m2_conv_w", "m2_conv_b", "m2_dt_bias", "m2_a_log", "m2_d", "m2_norm_g", "m2_w_out",
           "final_norm_g")
BIG = {"mlp_w1": 2, "mlp_w2": 1, "gdn_w_in": 2, "gdn_w_out": 1, "s5_w_in": 1, "s5_w_out": 2, "m2_w_in": 2, "m2_w_out": 1}
SMALL_CUT = {"gdn_conv_w": 2, "m2_conv_w": 2, "m2_conv_b": 1, "m2_norm_g": 1}
ROWS_MINOR = ("m2_w_in",)
ODD_WIDTH = {"gdn_w_in": GDN_IN, "m2_w_in": M2_IN}
WEIGHT_PARTS = (
    ((("gdn_w_out", 0),), (("gdn_w_in", 0),)),
    ((("mlp_w1", 0), ("mlp_w2", 0), ("mlp_w1", 1), ("mlp_w2", 1), ("s5_w_in", 0)), (("s5_w_out", 0),)),
    ((("mlp_w1", 2), ("mlp_w2", 2), ("m2_w_out", 0), ("mlp_w1", 3), ("mlp_w2", 3), ("gdn_w_out", 1)), (("m2_w_in", 0),),
     (("gdn_w_in", 1),)),
)
LAYER_ITEMS = (
    ((("mlp_w1", 0), ("mlp_w2", 0), ("gdn_w_out", 0)), (("gdn_w_in", 0),)),
    ((("mlp_w1", 1), ("mlp_w2", 1), ("s5_w_in", 0)), (("s5_w_out", 0),)),
    ((("mlp_w1", 2), ("mlp_w2", 2), ("m2_w_out", 0)), (("m2_w_in", 0),)),
    ((("mlp_w1", 3), ("mlp_w2", 3), ("gdn_w_out", 1)), (("gdn_w_in", 1),)),
)


def _rows2d(a):
    return a.reshape(-1, a.shape[-1])


def _pack(arrays, cols, row_multiple, dtype):
    flat = jnp.concatenate([a.reshape(-1).astype(dtype) for a in arrays])
    n = -(-flat.shape[0] // (cols * row_multiple)) * cols * row_multiple
    return jnp.pad(flat, (0, n - flat.shape[0])).reshape(-1, cols)


def _unpack(packed, shapes):
    flat = packed.reshape(-1)
    out, off = [], 0
    for shp in shapes:
        n = math.prod(shp)
        out.append(flat[off:off + n].reshape(shp))
        off += n
    return out


def _split_rows(buf, shapes):
    out, off = [], 0
    for shp in shapes:
        rows = math.prod(shp[:-1])
        out.append(buf[off:off + rows].reshape(shp))
        off += rows
    return out


def _cut(a, axis, k):
    n = a.shape[axis] // N_CHIPS
    return lax.slice_in_dim(a, k * n, (k + 1) * n, axis=axis)


def kernel(x, norm_mix_g, norm_mlp_g, mlp_w1, mlp_w2, gdn_w_in, gdn_conv_w, gdn_a_log, gdn_dt_bias, gdn_o_norm_g, gdn_w_out, s5_w_in, s5_lam_re, s5_lam_im, s5_log_dt, s5_b_re, s5_b_im, s5_c_re, s5_c_im, s5_d, s5_w_out, m2_w_in, m2_conv_w, m2_conv_b, m2_dt_bias, m2_a_log, m2_d, m2_norm_g, m2_w_out, final_norm_g, loss_target, m_norm_mix_g, m_norm_mlp_g, m_mlp_w1, m_mlp_w2, m_gdn_w_in, m_gdn_conv_w, m_gdn_a_log, m_gdn_dt_bias, m_gdn_o_norm_g, m_gdn_w_out, m_s5_w_in, m_s5_lam_re, m_s5_lam_im, m_s5_log_dt, m_s5_b_re, m_s5_b_im, m_s5_c_re, m_s5_c_im, m_s5_d, m_s5_w_out, m_m2_w_in, m_m2_conv_w, m_m2_conv_b, m_m2_dt_bias, m_m2_a_log, m_m2_d, m_m2_norm_g, m_m2_w_out, m_final_norm_g, v_norm_mix_g, v_norm_mlp_g, v_mlp_w1, v_mlp_w2, v_gdn_w_in, v_gdn_conv_w, v_gdn_a_log, v_gdn_dt_bias, v_gdn_o_norm_g, v_gdn_w_out, v_s5_w_in, v_s5_lam_re, v_s5_lam_im, v_s5_log_dt, v_s5_b_re, v_s5_b_im, v_s5_c_re, v_s5_c_im, v_s5_d, v_s5_w_out, v_m2_w_in, v_m2_conv_w, v_m2_conv_b, v_m2_dt_bias, v_m2_a_log, v_m2_d, v_m2_norm_g, v_m2_w_out, v_final_norm_g):
    given = dict(locals())
    w = {n: given[n] for n in WEIGHTS}
    mom = {n: given["m_" + n] for n in WEIGHTS}
    var = {n: given["v_" + n] for n in WEIGHTS}
    big, small_cut = tuple(BIG), tuple(SMALL_CUT)
    small = tuple(n for n in WEIGHTS if n not in BIG)
    chip = 2 * lax.axis_index("x") + lax.axis_index("y")

    W = {n: [None] * w[n].shape[0] for n in big}

    def fetch(groups, gather):
        own = [jnp.concatenate([w[n][l] for n, l in grp]).astype(BF16) for grp in groups]
        for grp, mine, got in zip(groups, own, gather(own)):
            shapes = [w[n][l].shape for n, l in grp]
            per_chip = [_split_rows(jnp.where(chip == k, mine, got[k]), shapes) for k in range(N_CHIPS)]
            for i, (n, l) in enumerate(grp):
                m = jnp.concatenate([per_chip[k][i] for k in range(N_CHIPS)], axis=BIG[n] - 1)
                pad = {"gdn_w_in": GDN_IN_PAD - GDN_IN, "m2_w_in": M2_IN_PAD - M2_IN}.get(n, 0)
                W[n][l] = jnp.pad(m, ((0, 0), (0, pad))) if pad else m

    for part in (1, 2):
        fetch(WEIGHT_PARTS[part], functools.partial(_gather_shards_later, part=part))
    fetch(WEIGHT_PARTS[0], _gather_shards)
    cut_blk = _pack([w[n] for n in small_cut], LANES, SUBLANES, F32)
    cut_all = _gather_small("gather_small_params", cut_blk).reshape(N_DEV, *cut_blk.shape)
    per_chip = [_unpack(cut_all[2 * k], [w[n].shape for n in small_cut]) for k in range(N_CHIPS)]
    W.update({n: jnp.concatenate([per_chip[k][i] for k in range(N_CHIPS)], axis=SMALL_CUT[n]) for i, n in enumerate(small_cut)})
    W.update({n: w[n] for n in small if n not in SMALL_CUT})

    core = lax.axis_index("c").astype(jnp.int32)
    ids = jnp.stack([chip.astype(jnp.int32), core])
    shard_grads = {}

    place, comm_bufs, odd = {}, {}, {}
    for layer, groups in enumerate(LAYER_ITEMS):
        for j, grp in enumerate(groups):
            row = 0
            for n, l in grp:
                place[n, l] = (layer, j, row)
                row += w[n].shape[1]
            if grp[0][0] not in ODD_WIDTH:
                comm_bufs[layer, j] = lax.empty((N_CHIPS, row, w[grp[0][0]].shape[2]), BF16)

    def dw(item, name, a, b):
        n = item[0]
        layer, j, row0 = place[item]
        if n in ODD_WIDTH:
            full = _mm(name, a, b, "tn", (BF16,))
            odd[layer, j] = jnp.stack([_cut(full[:, :ODD_WIDTH[n]], 1, k) for k in range(N_CHIPS)])
        else:
            comm_bufs[layer, j] = _dw_into(name, a, b, comm_bufs[layer, j], row0, BIG[n] - 1)

    def reduce_layer(i):
        groups = LAYER_ITEMS[i]
        gps = [comm_bufs[i, j] if (i, j) in comm_bufs else odd[i, j] for j in range(len(groups))]
        pairs = [_pair_sum(f"pair_sum{i}_{j}", gp, got, core.reshape(1))
                 for j, (gp, got) in enumerate(zip(gps, _pair_exchange(f"pair_exchange{i}", gps)))]
        sums = [_chip_sum(f"chip_sum{i}_{j}", t, got, ids) for j, (t, got) in enumerate(zip(pairs, _chip_exchange_later(pairs, i)))]
        for grp, g_shard in zip(groups, _swap_halves(f"swap_halves{i}", sums)):
            shard_grads.update(zip(grp, _split_rows(g_shard, [w[n][l].shape for n, l in grp])))

    loss, grad_x, G = _local_step(x[0], loss_target[0], W, dw, reduce_layer)
    loss = lax.psum(loss[0, 0], ("x", "y", "c"))
    grads = {n: jnp.stack([shard_grads[n, l] for l in range(w[n].shape[0])]) for n in big}
    sg = _pack([G[n] for n in small], LANES, ADAM_ROWS, F32)
    sg_sum = _sum_pieces("sum_small_grads", _gather_small("gather_small_grads", sg).reshape(N_DEV, *sg.shape))
    for n, g in zip(small, _unpack(sg_sum, [G[n].shape for n in small])):
        if n in SMALL_CUT:
            width = g.shape[SMALL_CUT[n]] // N_CHIPS
            g = lax.dynamic_slice_in_dim(g, chip * width, width, axis=SMALL_CUT[n])
        grads[n] = g.reshape(w[n].shape)

    delta, new_m, new_v = {}, {}, {}
    for n in big:
        if n in ROWS_MINOR:
            as2d = lambda a: jnp.swapaxes(a, -1, -2).reshape(-1, a.shape[-2])
            back = lambda o: jnp.swapaxes(o.reshape(w[n].shape[0], w[n].shape[2], w[n].shape[1]), -1, -2)
        else:
            as2d = lambda a: a.reshape(-1, a.shape[-1])
            back = lambda o: o.reshape(w[n].shape)
        outs = _adamw("adamw_" + n, as2d(w[n]), as2d(grads[n]), as2d(mom[n]), as2d(var[n]))
        delta[n], new_m[n], new_v[n] = (back(o) for o in outs)
    packs = [_pack([t[n] for n in small], LANES, ADAM_ROWS, F32) for t in (w, grads, mom, var)]
    outs = _adamw("adamw_small", *packs)
    for t, o in zip((delta, new_m, new_v), outs):
        t.update(zip(small, _unpack(o, [w[n].shape for n in small])))

    return (loss, grad_x[None], *[grads[n] for n in WEIGHTS], *[delta[n] for n in WEIGHTS], *[new_m[n] for n in WEIGHTS],
            *[new_v[n] for n in WEIGHTS])
```

```python
import functools
import math

import numpy as np
import jax
import jax.numpy as jnp
from jax import lax
from jax.experimental import pallas as pl
from jax.experimental.pallas import tpu as pltpu
from jax.experimental.pallas import tpu_sc as plsc

F32 = jnp.float32
BF16 = jnp.bfloat16

D_MODEL = 1024
D_FF = 4096
DEPTH = 4
CHUNK = 64
RMS_EPS = 1e-6
CONV_W = 4
GDN_HEADS = 8
GDN_DK = 128
GDN_IN = 4112
GDN_IN_PAD = 4224
S5_GROUPS = 64
S5_STATE = 64
S5_GROUP = 16
S5_BLOCKS = 8
M2_INNER = 2048
M2_HEADS = 32
M2_GROUPS = 8
M2_STATE = 128
M2_CONV_CH = 4096
M2_IN = 6176
M2_IN_PAD = 6272
ADAM_LR, ADAM_B1, ADAM_B2, ADAM_EPS, ADAM_WD, ADAM_STEP = 0.001, 0.9, 0.999, 1e-08, 0.01, 10

VMEM_LIMIT_BYTES = 56 * 1024 * 1024
SUBLANES = 8
LANES = 128


def _params(*sem):
    return pltpu.CompilerParams(dimension_semantics=tuple(sem) if sem else None, vmem_limit_bytes=VMEM_LIMIT_BYTES)


NN, NT, TN = ((1,), (0,)), ((1,), (1,)), ((0,), (0,))
_DOT_TRANSPOSES = {NN: ((NT, "gb"), (TN, "ag")), NT: ((NN, "gb"), (TN, "ga")), TN: ((NT, "bg"), (NN, "ag"))}


def _dg(a, b, dims):
    if a.ndim == 3:
        dn = (((dims[0][0] + 1,), (dims[1][0] + 1,)), ((0,), (0,)))
    else:
        dn = (dims, ((), ()))
    return lax.dot_general(a, b, dn, preferred_element_type=F32)


def _mxu(a, b, dims):
    return _dg(a.astype(BF16), b.astype(BF16), dims)


@functools.partial(jax.custom_vjp, nondiff_argnums=(2,))
def _dot(a, b, dims=NN):
    return _mxu(a, b, dims)


def _dot_fwd(a, b, dims):
    return _mxu(a, b, dims), (a, b)


def _dot_bwd(dims, res, g):
    ops = dict(a=res[0], b=res[1], g=g)
    (da_dims, da_ops), (db_dims, db_ops) = _DOT_TRANSPOSES[dims]
    return (_mxu(ops[da_ops[0]], ops[da_ops[1]], da_dims).astype(res[0].dtype),
            _mxu(ops[db_ops[0]], ops[db_ops[1]], db_dims).astype(res[1].dtype))


_dot.defvjp(_dot_fwd, _dot_bwd)


def _nt(a, b):
    return _dot(a, b, NT)


def _tn(a, b):
    return _dot(a, b, TN)


def _split3(x):
    x1 = x.astype(BF16)
    r = x - x1.astype(F32)
    x2 = r.astype(BF16)
    return x1, x2, (r - x2.astype(F32)).astype(BF16)


def _sel_mxu(x, sel, dims, x_first):
    f = (lambda p: _dg(p, sel.astype(BF16), dims)) if x_first else (lambda p: _dg(sel.astype(BF16), p, dims))
    x1, x2, x3 = _split3(x)
    return f(x1) + (f(x2) + f(x3))


@jax.custom_vjp
def _pick(x, sel):
    return _sel_mxu(x, sel, NN, True)


def _pick_fwd(x, sel):
    return _sel_mxu(x, sel, NN, True), sel


def _pick_bwd(sel, g):
    return _sel_mxu(g, sel, NT, True), jnp.zeros_like(sel)


_pick.defvjp(_pick_fwd, _pick_bwd)


@jax.custom_vjp
def _accum(sel, x):
    return _sel_mxu(x, sel, NN, False)


def _accum_fwd(sel, x):
    return _sel_mxu(x, sel, NN, False), sel


def _accum_bwd(sel, g):
    return jnp.zeros_like(sel), _sel_mxu(g, sel, TN, False)


_accum.defvjp(_accum_fwd, _accum_bwd)


def _dot3(a, b, dims=NN):
    ah, bh = a.astype(BF16), b.astype(BF16)
    al, bl = (a - ah.astype(F32)).astype(BF16), (b - bh.astype(F32)).astype(BF16)
    return _dg(ah, bh, dims) + (_dg(ah, bl, dims) + _dg(al, bh, dims))


def _neumann(x, r, dims):
    r = r + _dot3(x, r, dims)
    for _ in range(5):
        x = _dot3(x, x)
        r = r + _dot3(x, r, dims)
    return r


@jax.custom_vjp
def _unit_lower_solve(a, rhs):
    return _neumann(-a, rhs, NN)


def _unit_lower_solve_fwd(a, rhs):
    sol = _neumann(-a, rhs, NN)
    return sol, (a, sol)


def _unit_lower_solve_bwd(res, ct):
    a, sol = res
    d_rhs = _neumann(-a, ct, TN)
    return -_dot3(d_rhs, sol, NT), d_rhs


_unit_lower_solve.defvjp(_unit_lower_solve_fwd, _unit_lower_solve_bwd)


@jax.custom_vjp
def _unit_lower_solved(a, rhs, sol):
    return sol


def _unit_lower_solved_fwd(a, rhs, sol):
    return sol, (a, sol)


def _unit_lower_solved_bwd(res, ct):
    da, d_rhs = _unit_lower_solve_bwd(res, ct)
    return da, d_rhs, jnp.zeros_like(ct)


_unit_lower_solved.defvjp(_unit_lower_solved_fwd, _unit_lower_solved_bwd)


def _sigmoid(x):
    return 1.0 / (1.0 + jnp.exp(-x))


def _softplus(x):
    return jnp.maximum(x, 0.0) + jnp.log(1.0 + jnp.exp(-jnp.abs(x)))


def _iota2(shape, axis):
    return lax.broadcasted_iota(jnp.int32, shape, axis)


def _tile(n, cands):
    for c in cands:
        if n % c == 0:
            return c
    return n


MM_TILE_BYTES = 9 * 1024 * 1024


def _mm(name, a, b, mode, out_dtypes, epi=None, extras=(), tn=None):
    if mode == "nn":
        (M, K), N = a.shape, b.shape[1]
    elif mode == "nt":
        (M, K), N = a.shape, b.shape[0]
    else:
        (K, M), N = a.shape, b.shape[1]
    tn = tn or _tile(N, (512, 384, 896, 256, 128))
    out_bytes = tn * (sum(jnp.dtype(d).itemsize for d in out_dtypes) + sum(e.dtype.itemsize for e in extras))
    fits = lambda t: t * K * a.dtype.itemsize <= MM_TILE_BYTES and t * out_bytes <= MM_TILE_BYTES
    tm = next(t for t in (2048, 1024, 512, 256, 128) if M % t == 0 and (fits(t) or t == 128))
    if mode == "nn":
        a_spec, b_spec = pl.BlockSpec((tm, K), lambda i, j: (i, 0)), pl.BlockSpec((K, tn), lambda i, j: (0, j))
        dims = NN
    elif mode == "nt":
        a_spec, b_spec = pl.BlockSpec((tm, K), lambda i, j: (i, 0)), pl.BlockSpec((tn, K), lambda i, j: (j, 0))
        dims = NT
    else:
        a_spec, b_spec = pl.BlockSpec((K, tm), lambda i, j: (0, i)), pl.BlockSpec((K, tn), lambda i, j: (0, j))
        dims = TN
    n_ex = len(extras)

    def body(a_ref, b_ref, *rest):
        acc = _mxu(a_ref[...], b_ref[...], dims)
        res = epi(acc, *[e[...] for e in rest[:n_ex]]) if epi is not None else (acc,)
        for o_ref, r in zip(rest[n_ex:], res):
            o_ref[...] = r.astype(o_ref.dtype)

    tile = pl.BlockSpec((tm, tn), lambda i, j: (i, j))
    out = pl.pallas_call(
        body, name=name, grid=(M // tm, N // tn),
        in_specs=[a_spec, b_spec] + [tile] * n_ex,
        out_specs=[tile] * len(out_dtypes),
        out_shape=[jax.ShapeDtypeStruct((M, N), d) for d in out_dtypes],
        compiler_params=_params("parallel", "parallel"),
    )(a, b, *extras)
    return out if len(out_dtypes) > 1 else out[0]


def _dw_into(name, a, b, buf, row0, cut_axis):
    (K, M), N = a.shape, b.shape[1]
    ms, ns = (M, N // N_CHIPS) if cut_axis == 1 else (M // N_CHIPS, N)
    assert buf.shape[2] == ns, (buf.shape, ns)
    tm = next(t for t in (1024, 512, 256, 128) if ms % t == 0 and row0 % t == 0)
    tn = _tile(ns, (512, 256, 128))
    rb, cb = ms // tm, ns // tn
    if cut_axis == 1:
        where = lambda i, j: (j // cb, row0 // tm + i, j % cb)
    else:
        where = lambda i, j: (i // rb, row0 // tm + i % rb, j)

    def body(a_ref, b_ref, buf_ref, o_ref):
        o_ref[...] = _mxu(a_ref[...], b_ref[...], TN).astype(o_ref.dtype)

    return pl.pallas_call(
        body, name=name, grid=(M // tm, N // tn),
        in_specs=[pl.BlockSpec((K, tm), lambda i, j: (0, i)), pl.BlockSpec((K, tn), lambda i, j: (0, j)),
                  pl.BlockSpec(memory_space=pl.ANY)],
        out_specs=pl.BlockSpec((None, tm, tn), where), out_shape=jax.ShapeDtypeStruct(buf.shape, buf.dtype),
        input_output_aliases={2: 0}, compiler_params=_params("parallel", "parallel"),
    )(a, b, buf)


def _rms_fwd(name, h, g):
    L, D = h.shape
    tr = _tile(L, (256, 128))

    def body(h_ref, g_ref, o_ref):
        x = h_ref[...]
        r = lax.rsqrt(jnp.mean(x * x, axis=-1, keepdims=True) + RMS_EPS)
        o_ref[...] = (x * r * g_ref[...]).astype(o_ref.dtype)

    return pl.pallas_call(
        body, name=name, grid=(L // tr,),
        in_specs=[pl.BlockSpec((tr, D), lambda i: (i, 0)), pl.BlockSpec((1, D), lambda i: (0, 0))],
        out_specs=pl.BlockSpec((tr, D), lambda i: (i, 0)),
        out_shape=jax.ShapeDtypeStruct((L, D), BF16),
        compiler_params=_params("parallel"),
    )(h, g.reshape(1, D))


def _rms_bwd(name, h, g, dhn, dres):
    L, D = h.shape
    tr = _tile(L, (256, 128))

    def body(h_ref, g_ref, dhn_ref, dres_ref, dh_ref, dg_ref):
        x = h_ref[...]
        r = lax.rsqrt(jnp.mean(x * x, axis=-1, keepdims=True) + RMS_EPS)
        xh = x * r
        dy = dhn_ref[...]
        dxh = dy * g_ref[...]
        dh_ref[...] = dres_ref[...] + r * (dxh - xh * jnp.mean(dxh * xh, axis=-1, keepdims=True))

        @pl.when(pl.program_id(0) == 0)
        def _():
            dg_ref[...] = jnp.zeros_like(dg_ref)

        dg_ref[...] += jnp.sum(dy * xh, axis=0, keepdims=True)

    row = pl.BlockSpec((tr, D), lambda i: (i, 0))
    vec = pl.BlockSpec((1, D), lambda i: (0, 0))
    return pl.pallas_call(
        body, name=name, grid=(L // tr,),
        in_specs=[row, vec, row, row], out_specs=[row, vec],
        out_shape=[jax.ShapeDtypeStruct((L, D), F32), jax.ShapeDtypeStruct((1, D), F32)],
        compiler_params=_params("arbitrary"),
    )(h, g.reshape(1, D), dhn, dres)


def _loss_head(h, g, target):
    L, D = h.shape
    tr = _tile(L, (256, 128))

    def body(h_ref, g_ref, t_ref, loss_ref, dh_ref, dg_ref):
        x = h_ref[...]
        r = lax.rsqrt(jnp.mean(x * x, axis=-1, keepdims=True) + RMS_EPS)
        xh = x * r
        err = xh * g_ref[...] - t_ref[...]
        dy = err * (1.0 / D)
        dxh = dy * g_ref[...]
        dh_ref[...] = r * (dxh - xh * jnp.mean(dxh * xh, axis=-1, keepdims=True))

        @pl.when(pl.program_id(0) == 0)
        def _():
            dg_ref[...] = jnp.zeros_like(dg_ref)
            loss_ref[...] = jnp.zeros_like(loss_ref)

        dg_ref[...] += jnp.sum(dy * xh, axis=0, keepdims=True)
        loss_ref[...] += (0.5 / D) * jnp.sum(jnp.sum(err * err, axis=-1, keepdims=True), axis=0, keepdims=True)

    row = pl.BlockSpec((tr, D), lambda i: (i, 0))
    vec = pl.BlockSpec((1, D), lambda i: (0, 0))
    return pl.pallas_call(
        body, name="loss_head", grid=(L // tr,),
        in_specs=[row, vec, row], out_specs=[pl.BlockSpec((1, 1), lambda i: (0, 0)), row, vec],
        out_shape=[jax.ShapeDtypeStruct((1, 1), F32), jax.ShapeDtypeStruct((L, D), F32), jax.ShapeDtypeStruct((1, D), F32)],
        compiler_params=_params("arbitrary"),
    )(h, g.reshape(1, D), target)


def _glu_fwd(h, ag):
    L, D = h.shape
    tr = _tile(L, (256, 128))

    def body(h_ref, v_ref, g_ref, o_ref):
        o_ref[...] = h_ref[...] + v_ref[...] * _sigmoid(g_ref[...])

    return pl.pallas_call(
        body, name="s5_glu_fwd", grid=(L // tr,),
        in_specs=[pl.BlockSpec((tr, D), lambda i: (i, 0)), pl.BlockSpec((tr, D), lambda i: (i, 0)),
                  pl.BlockSpec((tr, D), lambda i: (i, 1))],
        out_specs=pl.BlockSpec((tr, D), lambda i: (i, 0)),
        out_shape=jax.ShapeDtypeStruct((L, D), F32),
        compiler_params=_params("parallel"),
    )(h, ag, ag)


def _glu_bwd(dh, ag):
    L, D = dh.shape
    tr = _tile(L, (256, 128))

    def body(dh_ref, v_ref, g_ref, dv_ref, dg_ref):
        s = _sigmoid(g_ref[...])
        d = dh_ref[...]
        dv_ref[...] = d * s
        dg_ref[...] = d * v_ref[...] * s * (1.0 - s)

    dv, dg = pl.pallas_call(
        body, name="s5_glu_bwd", grid=(L // tr,),
        in_specs=[pl.BlockSpec((tr, D), lambda i: (i, 0)), pl.BlockSpec((tr, D), lambda i: (i, 0)),
                  pl.BlockSpec((tr, D), lambda i: (i, 1))],
        out_specs=[pl.BlockSpec((tr, D), lambda i: (i, 0))] * 2,
        out_shape=[jax.ShapeDtypeStruct((L, D), F32)] * 2,
        compiler_params=_params("parallel"),
    )(dh, ag, ag)
    return jnp.concatenate([dv, dg], axis=1).astype(BF16)


CONV_ROWS = 128
CONV_COLS = 512


def _shift_rows(cat, s):
    if s == 0:
        return cat[SUBLANES:, :]
    return pltpu.roll(cat, s, axis=0)[SUBLANES:, :]


def _conv_fwd(name, p, col0, w, b):
    L = p.shape[0]
    C = w.shape[1]
    tc = _tile(C, (CONV_COLS, 256))
    cb0 = col0 // tc
    nr = L // CONV_ROWS

    def body(x_ref, w_ref, b_ref, o_ref):
        def step(r, carry):
            r0 = pl.multiple_of(r * CONV_ROWS, CONV_ROWS)
            cur = x_ref[pl.ds(r0, CONV_ROWS), :]
            p0 = pl.multiple_of(jnp.maximum(r0 - SUBLANES, 0), SUBLANES)
            prev = jnp.where(r > 0, x_ref[pl.ds(p0, SUBLANES), :], 0.0)
            cat = jnp.concatenate([prev, cur], axis=0)
            acc = b_ref[...] + w_ref[3:4, :] * cur
            for k in range(CONV_W - 1):
                acc = acc + w_ref[k:k + 1, :] * _shift_rows(cat, CONV_W - 1 - k)
            o_ref[pl.ds(r0, CONV_ROWS), :] = acc * _sigmoid(acc)
            return carry

        lax.fori_loop(0, nr, step, 0)

    return pl.pallas_call(
        body, name=name, grid=(C // tc,),
        in_specs=[pl.BlockSpec((L, tc), lambda j: (0, cb0 + j)), pl.BlockSpec((CONV_W, tc), lambda j: (0, j)),
                  pl.BlockSpec((1, tc), lambda j: (0, j))],
        out_specs=pl.BlockSpec((L, tc), lambda j: (0, j)),
        out_shape=jax.ShapeDtypeStruct((L, C), F32),
        compiler_params=_params("parallel"),
    )(p, w, b)


def _conv_bwd(name, p, col0, w, b, dout):
    L = p.shape[0]
    C = w.shape[1]
    tc = _tile(C, (CONV_COLS, 256))
    cb0 = col0 // tc
    nr = L // CONV_ROWS

    def body(x_ref, w_ref, b_ref, do_ref, dx_ref, dw_ref, db_ref, dpre_ref):
        def step1(r, carry):
            dw0, dw1, dw2, dw3, dbb = carry
            r0 = pl.multiple_of(r * CONV_ROWS, CONV_ROWS)
            cur = x_ref[pl.ds(r0, CONV_ROWS), :]
            p0 = pl.multiple_of(jnp.maximum(r0 - SUBLANES, 0), SUBLANES)
            prev = jnp.where(r > 0, x_ref[pl.ds(p0, SUBLANES), :], 0.0)
            cat = jnp.concatenate([prev, cur], axis=0)
            sh = [_shift_rows(cat, CONV_W - 1 - k) for k in range(CONV_W - 1)] + [cur]
            acc = b_ref[...] + w_ref[3:4, :] * cur
            for k in range(CONV_W - 1):
                acc = acc + w_ref[k:k + 1, :] * sh[k]
            sg = _sigmoid(acc)
            dpre = do_ref[pl.ds(r0, CONV_ROWS), :] * (sg + acc * sg * (1.0 - sg))
            dpre_ref[pl.ds(r0, CONV_ROWS), :] = dpre
            dws = [d + jnp.sum(dpre * s, axis=0, keepdims=True) for d, s in zip((dw0, dw1, dw2, dw3), sh)]
            return (*dws, dbb + jnp.sum(dpre, axis=0, keepdims=True))

        z = jnp.zeros((1, tc), F32)
        dw0, dw1, dw2, dw3, dbb = lax.fori_loop(0, nr, step1, (z, z, z, z, z))
        dw_ref[...] = jnp.concatenate([dw0, dw1, dw2, dw3, z, z, z, z], axis=0)
        db_ref[...] = dbb

        def step2(r, carry):
            r0 = pl.multiple_of(r * CONV_ROWS, CONV_ROWS)
            cur = dpre_ref[pl.ds(r0, CONV_ROWS), :]
            n0 = pl.multiple_of(jnp.minimum(r0 + CONV_ROWS, L - SUBLANES), SUBLANES)
            nxt = jnp.where(r < nr - 1, dpre_ref[pl.ds(n0, SUBLANES), :], 0.0)
            cat = jnp.concatenate([cur, nxt], axis=0)
            acc = w_ref[3:4, :] * cur
            for k in range(CONV_W - 1):
                s = CONV_W - 1 - k
                acc = acc + w_ref[k:k + 1, :] * pltpu.roll(cat, CONV_ROWS + SUBLANES - s, axis=0)[:CONV_ROWS, :]
            dx_ref[pl.ds(r0, CONV_ROWS), :] = acc
            return carry

        lax.fori_loop(0, nr, step2, 0)

    dx, dw, db = pl.pallas_call(
        body, name=name, grid=(C // tc,),
        in_specs=[pl.BlockSpec((L, tc), lambda j: (0, cb0 + j)), pl.BlockSpec((CONV_W, tc), lambda j: (0, j)),
                  pl.BlockSpec((1, tc), lambda j: (0, j)), pl.BlockSpec((L, tc), lambda j: (0, j))],
        out_specs=[pl.BlockSpec((L, tc), lambda j: (0, j)), pl.BlockSpec((SUBLANES, tc), lambda j: (0, j)),
                   pl.BlockSpec((1, tc), lambda j: (0, j))],
        out_shape=[jax.ShapeDtypeStruct((L, C), F32), jax.ShapeDtypeStruct((SUBLANES, C), F32),
                   jax.ShapeDtypeStruct((1, C), F32)],
        scratch_shapes=[pltpu.VMEM((L, tc), F32)],
        compiler_params=_params("parallel"),
    )(p, w, b, dout)
    return dx, dw[:CONV_W], db


def _chunk_consts():
    r, c = _iota2((CHUNK, CHUNK), 0), _iota2((CHUNK, CHUNK), 1)
    causal = r >= c
    return causal, r > c, (r == c).astype(F32), causal.astype(F32), jnp.ones((CHUNK, CHUNK), F32)


def _by_lanes(t):
    return jnp.concatenate([t[i] for i in range(t.shape[0])], axis=1)


def _by_batch(t, w):
    return jnp.concatenate([t[None, :, i * w:(i + 1) * w] for i in range(t.shape[1] // w)], axis=0)


def _diag_lanes():
    return (_iota2((CHUNK, LANES), 0) == _iota2((CHUNK, LANES), 1)).astype(F32)


def _gdn_chunk(q, k, v, ab, gate, S, alog, dtb, og, ea, eb, sol=None):
    causal, strict, _, tril, ones = _chunk_consts()
    logits = _by_batch(_pick(ab, jnp.concatenate([_by_lanes(ea), _by_lanes(eb)], axis=1)), LANES)
    H = q.shape[0]
    g = -jnp.exp(alog) * _softplus(logits[:H] + dtb)
    beta = _sigmoid(logits[H:])
    qn = q * lax.rsqrt(jnp.sum(q * q, axis=-1, keepdims=True) + 1e-6) * (GDN_DK ** -0.5)
    kn = k * lax.rsqrt(jnp.sum(k * k, axis=-1, keepdims=True) + 1e-6)
    g_l = _by_lanes(g)
    gc = _by_batch(_accum(tril, g_l), LANES)
    glast = _by_batch(_accum(ones, g_l), LANES)
    gcol = gc[:, :, :CHUNK]
    grow = _by_batch(_accum(ones, _by_lanes(gc * _diag_lanes())), LANES)[:, :, :CHUNK]
    decay = jnp.exp(jnp.where(causal, gcol - grow, -jnp.inf))
    a = jnp.where(strict, beta[:, :, :CHUNK] * _nt(kn, kn) * decay, 0.0)
    eg = jnp.exp(gc)
    rhs = jnp.concatenate([v * beta, kn * (beta * eg)], axis=2)
    sol = _unit_lower_solve(a, rhs) if sol is None else _unit_lower_solved(a, rhs, sol)
    u, w = sol[:, :, :GDN_DK], sol[:, :, GDN_DK:]
    qk = _nt(qn, kn) * decay
    v_new = u - _dot(w, S)
    o = _dot(qn * eg, S) + _dot(qk, v_new)
    cd = jnp.exp(glast)
    s_new = jnp.concatenate([cd, cd], axis=1) * S + _tn(kn * jnp.exp(glast - gc), v_new)
    on = o * lax.rsqrt(jnp.mean(o * o, axis=-1, keepdims=True) + RMS_EPS) * og
    return on * (gate * _sigmoid(gate)), s_new, sol


GDN_HB = 8


def _gdn_specs(nc, rev):
    cm = (lambda c: nc - 1 - c) if rev else (lambda c: c)
    blk = lambda off: pl.BlockSpec((CHUNK, GDN_HB * GDN_DK), lambda c, h: (cm(c), off // GDN_HB + h))
    ab = pl.BlockSpec((CHUNK, LANES), lambda c, h: (cm(c), (GDN_IN_PAD - LANES) // LANES))
    hv = pl.BlockSpec((GDN_HB, 1, LANES), lambda c, h: (h, 0, 0))
    og = pl.BlockSpec((1, LANES), lambda c, h: (0, 0))
    em = pl.BlockSpec((GDN_HB, LANES, LANES), lambda c, h: (h, 0, 0))
    st = pl.BlockSpec((None, GDN_HB, GDN_DK, GDN_DK), lambda c, h: (cm(c), h, 0, 0))
    sl = pl.BlockSpec((None, GDN_HB, CHUNK, 2 * GDN_DK), lambda c, h: (cm(c), h, 0, 0))
    return blk, ab, hv, og, em, st, sl


def _gdn_fwd(qkv, p, alog_e, dtb_e, og, ea, eb):
    L = qkv.shape[0]
    nc = L // CHUNK
    blk, ab, hv, ogs, em, st, sl = _gdn_specs(nc, False)

    def body(q_ref, k_ref, v_ref, gate_ref, ab_ref, al_ref, dt_ref, og_ref, ea_ref, eb_ref, y_ref, sp_ref, sol_ref, s_scr):
        c, h = pl.program_id(0), pl.program_id(1)
        lanes = [slice(i * GDN_DK, (i + 1) * GDN_DK) for i in range(GDN_HB)]
        heads = pl.ds(h * GDN_HB, GDN_HB)
        stack = lambda ref: jnp.concatenate([ref[:, ls][None] for ls in lanes], axis=0)

        @pl.when(c == 0)
        def _():
            s_scr[heads] = jnp.zeros((GDN_HB, GDN_DK, GDN_DK), F32)

        S = s_scr[heads]
        sp_ref[...] = S
        y, s_new, sol = _gdn_chunk(stack(q_ref), stack(k_ref), stack(v_ref), ab_ref[...], stack(gate_ref), S,
                                   al_ref[...], dt_ref[...], og_ref[...], ea_ref[...], eb_ref[...])
        for i, ls in enumerate(lanes):
            y_ref[:, ls] = y[i]
        s_scr[heads] = s_new
        sol_ref[...] = sol

    return pl.pallas_call(
        body, name="gdn_fwd", grid=(nc, GDN_HEADS // GDN_HB),
        in_specs=[blk(0), blk(GDN_HEADS), blk(2 * GDN_HEADS), blk(3 * GDN_HEADS), ab, hv, hv, ogs, em, em],
        out_specs=[blk(0), st, sl],
        out_shape=[jax.ShapeDtypeStruct((L, D_MODEL), F32), jax.ShapeDtypeStruct((nc, GDN_HEADS, GDN_DK, GDN_DK), F32),
                   jax.ShapeDtypeStruct((nc, GDN_HEADS, CHUNK, 2 * GDN_DK), F32)],
        scratch_shapes=[pltpu.VMEM((GDN_HEADS, GDN_DK, GDN_DK), F32)],
        compiler_params=_params("arbitrary", "arbitrary"),
    )(qkv, qkv, qkv, p, p, alog_e, dtb_e, og, ea, eb)


def _gdn_bwd(qkv, p, alog_e, dtb_e, og, ea, eb, sprev, sol, dy):
    L = qkv.shape[0]
    nc = L // CHUNK
    blk, ab, hv, ogs, em, st, sl = _gdn_specs(nc, True)

    def body(q_ref, k_ref, v_ref, gate_ref, ab_ref, al_ref, dt_ref, og_ref, ea_ref, eb_ref, sp_ref, sol_ref, dy_ref,
             dqkv_ref, dgate_ref, dab_ref, dpar_ref, ds_scr):
        c, h = pl.program_id(0), pl.program_id(1)
        lanes = [slice(i * GDN_DK, (i + 1) * GDN_DK) for i in range(GDN_HB)]
        heads = pl.ds(h * GDN_HB, GDN_HB)
        stack = lambda ref: jnp.concatenate([ref[:, ls][None] for ls in lanes], axis=0)

        @pl.when(c == 0)
        def _():
            ds_scr[heads] = jnp.zeros((GDN_HB, GDN_DK, GDN_DK), F32)
            dpar_ref[heads] = jnp.zeros((GDN_HB, SUBLANES, LANES), F32)

        @pl.when(h == 0)
        def _():
            dab_ref[...] = jnp.zeros_like(dab_ref)

        ea_m, eb_m, sol_m = ea_ref[...], eb_ref[...], sol_ref[...]
        f = lambda q, k, v, a_b, gate, S, al, dt, o_g: _gdn_chunk(q, k, v, a_b, gate, S, al, dt, o_g, ea_m, eb_m, sol_m)[:2]
        _, vjp = jax.vjp(f, stack(q_ref), stack(k_ref), stack(v_ref), ab_ref[...], stack(gate_ref), sp_ref[...],
                         al_ref[...], dt_ref[...], og_ref[...])
        dq, dk, dv, dab, dgate, ds, dal, ddt, dog = vjp((stack(dy_ref), ds_scr[heads]))
        for i, ls in enumerate(lanes):
            for t, part in enumerate((dq, dk, dv)):
                dqkv_ref[:, t * D_MODEL + ls.start:t * D_MODEL + ls.stop] = part[i]
            dgate_ref[:, ls] = dgate[i]
        ds_scr[heads] = ds
        dab_ref[...] += dab
        first = _iota2((GDN_HB, 1, LANES), 0) == 0
        dpar_ref[heads] += jnp.concatenate([dal, ddt, jnp.where(first, dog[None], 0.0),
                                            jnp.zeros((GDN_HB, SUBLANES - 3, LANES), F32)], axis=1)

    return pl.pallas_call(
        body, name="gdn_bwd", grid=(nc, GDN_HEADS // GDN_HB),
        in_specs=[blk(0), blk(GDN_HEADS), blk(2 * GDN_HEADS), blk(3 * GDN_HEADS), ab, hv, hv, ogs, em, em, st, sl, blk(0)],
        out_specs=[pl.BlockSpec((CHUNK, 3 * D_MODEL), lambda c, h: (nc - 1 - c, 0)), blk(0),
                   pl.BlockSpec((CHUNK, LANES), lambda c, h: (nc - 1 - c, 0)),
                   pl.BlockSpec((GDN_HEADS, SUBLANES, LANES), lambda c, h: (0, 0, 0))],
        out_shape=[jax.ShapeDtypeStruct((L, 3 * D_MODEL), F32), jax.ShapeDtypeStruct((L, D_MODEL), F32),
                   jax.ShapeDtypeStruct((L, LANES), F32), jax.ShapeDtypeStruct((GDN_HEADS, SUBLANES, LANES), F32)],
        scratch_shapes=[pltpu.VMEM((GDN_HEADS, GDN_DK, GDN_DK), F32)],
        compiler_params=_params("arbitrary", "arbitrary"),
    )(qkv, qkv, qkv, p, p, alog_e, dtb_e, og, ea, eb, sprev, sol, dy)


def _gdn_selectors():
    rows = np.arange(LANES)[None, :, None]
    heads = np.arange(GDN_HEADS)[:, None, None]
    ea = np.broadcast_to(rows == heads, (GDN_HEADS, LANES, LANES)).astype(np.float32)
    eb = np.broadcast_to(rows == heads + GDN_HEADS, (GDN_HEADS, LANES, LANES)).astype(np.float32)
    return jnp.asarray(ea), jnp.asarray(eb)


M2_GW = M2_INNER // M2_GROUPS
M2_HPG = M2_HEADS // M2_GROUPS
M2_HD = M2_INNER // M2_HEADS


def _m2_chunk(x, bm, cm, z, dtr, st, dtb, alog, dsk, ng, e, ecol):
    G = x.shape[0]
    causal, _, _, tril, ones = _chunk_consts()
    dt_n = _softplus(dtr + dtb)
    da_n = dt_n * (-jnp.exp(alog))
    cum_n = _accum(tril, da_n)
    tot_n = _accum(ones, da_n)
    wide = _pick(jnp.concatenate([dt_n, cum_n, tot_n], axis=0), e)
    dt_w, cum_w, tot_w = (_by_batch(wide[i * CHUNK:(i + 1) * CHUNK], M2_GW) for i in range(3))
    xdt = x * dt_w
    cb = _nt(cm, bm)
    heads = lambda t: jnp.concatenate([t[i:i + 1] for i in range(G) for _ in range(M2_HPG)], axis=0)
    colb = _by_batch(_pick(cum_n, ecol), LANES)
    rowb = _by_batch(_accum(ones, _by_lanes(colb * _diag_lanes())), LANES)
    lmat = jnp.exp(jnp.where(causal, colb[:, :, :CHUNK] - rowb[:, :, :CHUNK], -jnp.inf))
    yr = _dot(heads(cb) * lmat, heads(xdt))
    head = _iota2((CHUNK, M2_GW), 1) // M2_HD
    ydiag = jnp.concatenate([sum(jnp.where(head == r, yr[i * M2_HPG + r], 0.0) for r in range(M2_HPG))[None] for i in range(G)], axis=0)
    st_new = _tn(bm, xdt * jnp.exp(tot_w - cum_w))
    cd = jnp.exp(tot_w)
    s_new = jnp.concatenate([cd, cd], axis=1) * st + st_new
    y = ydiag + _dot(cm, st) * jnp.exp(cum_w) + dsk * x
    y = y * (z * _sigmoid(z))
    yn = y * lax.rsqrt(jnp.mean(y * y, axis=-1, keepdims=True) + RMS_EPS) * ng
    return yn, s_new


M2_GB = 8


def _m2_specs(nc, rev):
    cm = (lambda c: nc - 1 - c) if rev else (lambda c: c)
    wide = lambda off: pl.BlockSpec((CHUNK, M2_GB * M2_GW), lambda c, g: (cm(c), off // M2_GB + g))
    nar = lambda off: pl.BlockSpec((CHUNK, M2_GB * LANES), lambda c, g: (cm(c), off // M2_GB + g))
    dts = pl.BlockSpec((CHUNK, LANES), lambda c, g: (cm(c), (M2_IN_PAD - LANES) // LANES))
    v128 = pl.BlockSpec((1, LANES), lambda c, g: (0, 0))
    v256 = pl.BlockSpec((1, M2_GB * M2_GW), lambda c, g: (0, g))
    es = pl.BlockSpec((LANES, M2_GB * M2_GW), lambda c, g: (0, g))
    ecs = pl.BlockSpec((LANES, M2_GB * M2_HPG * LANES), lambda c, g: (0, g))
    st = pl.BlockSpec((None, M2_GB, M2_STATE, M2_GW), lambda c, g: (cm(c), g, 0, 0))
    return wide, nar, dts, v128, v256, es, ecs, st


def _m2_fwd(xbc, p, dtb, alog, dsk, ng, e, ecol):
    L = xbc.shape[0]
    nc = L // CHUNK
    wide, nar, dts, v128, v256, es, ecs, st = _m2_specs(nc, False)

    def body(x_ref, b_ref, c_ref, z_ref, dt_ref, dtb_ref, al_ref, dsk_ref, ng_ref, e_ref, ec_ref, y_ref, sp_ref, s_scr):
        c, g = pl.program_id(0), pl.program_id(1)
        wide_l = [slice(i * M2_GW, (i + 1) * M2_GW) for i in range(M2_GB)]
        nar_l = [slice(i * LANES, (i + 1) * LANES) for i in range(M2_GB)]
        groups = pl.ds(g * M2_GB, M2_GB)
        wide_s = lambda ref: jnp.concatenate([ref[:, ls][None] for ls in wide_l], axis=0)
        nar_s = lambda ref: jnp.concatenate([ref[:, ls][None] for ls in nar_l], axis=0)

        @pl.when(c == 0)
        def _():
            s_scr[groups] = jnp.zeros((M2_GB, M2_STATE, M2_GW), F32)

        S = s_scr[groups]
        sp_ref[...] = S
        y, s_new = _m2_chunk(wide_s(x_ref), nar_s(b_ref), nar_s(c_ref), wide_s(z_ref), dt_ref[...], S, dtb_ref[...], al_ref[...],
                             wide_s(dsk_ref), wide_s(ng_ref), e_ref[...], ec_ref[...])
        for i, ls in enumerate(wide_l):
            y_ref[:, ls] = y[i]
        s_scr[groups] = s_new

    return pl.pallas_call(
        body, name="m2_fwd", grid=(nc, M2_GROUPS // M2_GB),
        in_specs=[wide(0), nar(2 * M2_GROUPS), nar(3 * M2_GROUPS), wide(0), dts, v128, v128, v256, v256, es, ecs],
        out_specs=[wide(0), st],
        out_shape=[jax.ShapeDtypeStruct((L, M2_INNER), F32), jax.ShapeDtypeStruct((nc, M2_GROUPS, M2_STATE, M2_GW), F32)],
        scratch_shapes=[pltpu.VMEM((M2_GROUPS, M2_STATE, M2_GW), F32)],
        compiler_params=_params("arbitrary", "arbitrary"),
    )(xbc, xbc, xbc, p, p, dtb, alog, dsk, ng, e, ecol)


def _m2_bwd(xbc, p, dtb, alog, dsk, ng, e, ecol, sprev, dy):
    L = xbc.shape[0]
    nc = L // CHUNK
    wide, nar, dts, v128, v256, es, ecs, st = _m2_specs(nc, True)

    def body(x_ref, b_ref, c_ref, z_ref, dt_ref, dtb_ref, al_ref, dsk_ref, ng_ref, e_ref, ec_ref, sp_ref, dy_ref,
             dx_ref, db_ref, dc_ref, dz_ref, ddt_ref, dnar_ref, dwide_ref, ds_scr):
        c, g = pl.program_id(0), pl.program_id(1)
        wide_l = [slice(i * M2_GW, (i + 1) * M2_GW) for i in range(M2_GB)]
        nar_l = [slice(i * LANES, (i + 1) * LANES) for i in range(M2_GB)]
        groups = pl.ds(g * M2_GB, M2_GB)
        wide_s = lambda ref: jnp.concatenate([ref[:, ls][None] for ls in wide_l], axis=0)
        nar_s = lambda ref: jnp.concatenate([ref[:, ls][None] for ls in nar_l], axis=0)

        @pl.when(jnp.logical_and(c == 0, g == 0))
        def _():
            dnar_ref[...] = jnp.zeros_like(dnar_ref)

        @pl.when(c == 0)
        def _():
            ds_scr[groups] = jnp.zeros((M2_GB, M2_STATE, M2_GW), F32)
            dwide_ref[groups] = jnp.zeros((M2_GB, SUBLANES, M2_GW), F32)

        @pl.when(g == 0)
        def _():
            ddt_ref[...] = jnp.zeros_like(ddt_ref)

        e_m, ec_m = e_ref[...], ec_ref[...]
        f = lambda x, bm, cm, z, dtr, S, dtb, al, dsk, ng: _m2_chunk(x, bm, cm, z, dtr, S, dtb, al, dsk, ng, e_m, ec_m)
        _, vjp = jax.vjp(f, wide_s(x_ref), nar_s(b_ref), nar_s(c_ref), wide_s(z_ref), dt_ref[...], sp_ref[...], dtb_ref[...],
                         al_ref[...], wide_s(dsk_ref), wide_s(ng_ref))
        dx, db, dc, dz, ddt, ds, ddtb, dal, ddsk, dng = vjp((wide_s(dy_ref), ds_scr[groups]))
        for i in range(M2_GB):
            dx_ref[:, wide_l[i]] = dx[i]
            db_ref[:, nar_l[i]] = db[i]
            dc_ref[:, nar_l[i]] = dc[i]
            dz_ref[:, wide_l[i]] = dz[i]
        ds_scr[groups] = ds
        ddt_ref[...] += ddt
        dnar_ref[...] += jnp.concatenate([ddtb, dal, jnp.zeros((SUBLANES - 2, LANES), F32)], axis=0)
        dwide_ref[groups] += jnp.concatenate([ddsk, dng, jnp.zeros((M2_GB, SUBLANES - 2, M2_GW), F32)], axis=1)

    return pl.pallas_call(
        body, name="m2_bwd", grid=(nc, M2_GROUPS // M2_GB),
        in_specs=[wide(0), nar(2 * M2_GROUPS), nar(3 * M2_GROUPS), wide(0), dts, v128, v128, v256, v256, es, ecs, st, wide(0)],
        out_specs=[wide(0), nar(0), nar(0), wide(0), pl.BlockSpec((CHUNK, LANES), lambda c, g: (nc - 1 - c, 0)),
                   pl.BlockSpec((SUBLANES, LANES), lambda c, g: (0, 0)),
                   pl.BlockSpec((M2_GROUPS, SUBLANES, M2_GW), lambda c, g: (0, 0, 0))],
        out_shape=[jax.ShapeDtypeStruct((L, M2_INNER), F32), jax.ShapeDtypeStruct((L, M2_GROUPS * M2_STATE), F32),
                   jax.ShapeDtypeStruct((L, M2_GROUPS * M2_STATE), F32), jax.ShapeDtypeStruct((L, M2_INNER), F32),
                   jax.ShapeDtypeStruct((L, LANES), F32), jax.ShapeDtypeStruct((SUBLANES, LANES), F32),
                   jax.ShapeDtypeStruct((M2_GROUPS, SUBLANES, M2_GW), F32)],
        scratch_shapes=[pltpu.VMEM((M2_GROUPS, M2_STATE, M2_GW), F32)],
        compiler_params=_params("arbitrary", "arbitrary"),
    )(xbc, xbc, xbc, p, p, dtb, alog, dsk, ng, e, ecol, sprev, dy)


def _m2_selectors():
    e = np.zeros((LANES, M2_INNER), np.float32)
    ecol = np.zeros((LANES, M2_HEADS * LANES), np.float32)
    for h in range(M2_HEADS):
        e[h, M2_HD * h:M2_HD * (h + 1)] = 1.0
        ecol[h, LANES * h:LANES * (h + 1)] = 1.0
    return jnp.asarray(e), jnp.asarray(ecol)


S5_NS = S5_GROUPS * S5_STATE // S5_BLOCKS
S5_ROWS = 256
GELU_C = math.sqrt(2.0 / math.pi)


def _gelu(x):
    return 0.5 * x * (1.0 + jnp.tanh(GELU_C * (x + 0.044715 * x * x * x)))


def _gelu_grad(x):
    t = jnp.tanh(GELU_C * (x + 0.044715 * x * x * x))
    return 0.5 * (1.0 + t) + 0.5 * x * (1.0 - t * t) * GELU_C * (1.0 + 3.0 * 0.044715 * x * x)


def _s5_scan(re_ref, im_ref, pw_re, pw_im, nrows, reverse, states=None):
    n = re_ref.shape[1]
    row = _iota2((SUBLANES, n), 0)
    steps = []
    for d in (1, 2, 4):
        keep = (row < SUBLANES - d) if reverse else (row >= d)
        steps.append(((SUBLANES - d) if reverse else d, jnp.where(keep, pw_re[d - 1:d, :], 0.0), jnp.where(keep, pw_im[d - 1:d, :], 0.0)))
    if reverse:
        cw_re = jnp.concatenate([pw_re[SUBLANES - 1 - k:SUBLANES - k, :] for k in range(SUBLANES)], axis=0)
        cw_im = jnp.concatenate([pw_im[SUBLANES - 1 - k:SUBLANES - k, :] for k in range(SUBLANES)], axis=0)
    else:
        cw_re, cw_im = pw_re, pw_im
    edge = 0 if reverse else SUBLANES - 1
    ngroups = nrows // SUBLANES

    def step(i, carry):
        cr, ci, ar, ai = carry
        gi = (ngroups - 1 - i) if reverse else i
        r0 = pl.multiple_of(gi * SUBLANES, SUBLANES)
        xr, xi = re_ref[pl.ds(r0, SUBLANES), :], im_ref[pl.ds(r0, SUBLANES), :]
        for shift, pr, pi in steps:
            sr, si = pltpu.roll(xr, shift, axis=0), pltpu.roll(xi, shift, axis=0)
            xr, xi = xr + (pr * sr - pi * si), xi + (pr * si + pi * sr)
        xr, xi = xr + (cw_re * cr - cw_im * ci), xi + (cw_re * ci + cw_im * cr)
        re_ref[pl.ds(r0, SUBLANES), :] = xr
        im_ref[pl.ds(r0, SUBLANES), :] = xi
        if states is not None:
            p0 = pl.multiple_of(jnp.maximum(r0 - SUBLANES, 0), SUBLANES)
            live = jnp.where(gi > 0, 1.0, 0.0)
            prev = [jnp.where(row >= 1, pltpu.roll(ref[pl.ds(r0, SUBLANES), :], 1, axis=0),
                              live * pltpu.roll(ref[pl.ds(p0, SUBLANES), :], 1, axis=0)) for ref in states]
            ar, ai = ar + (prev[0] * xr + prev[1] * xi), ai + (prev[0] * xi - prev[1] * xr)
        return (jnp.sum(jnp.where(row == edge, xr, 0.0), axis=0, keepdims=True),
                jnp.sum(jnp.where(row == edge, xi, 0.0), axis=0, keepdims=True), ar, ai)

    z = jnp.zeros((1, n), F32)
    za = jnp.zeros((SUBLANES, n) if states is not None else (1, n), F32)
    _, _, ar, ai = lax.fori_loop(0, ngroups, step, (z, z, za, za))
    return jnp.sum(ar, axis=0, keepdims=True), jnp.sum(ai, axis=0, keepdims=True)


def _s5_project_in(u_ref, bm_ref, re_ref, im_ref, L):
    def step(i, carry):
        r0 = pl.multiple_of(i * S5_ROWS, S5_ROWS)
        bu = _dot(u_ref[pl.ds(r0, S5_ROWS), :], bm_ref[...])
        re_ref[pl.ds(r0, S5_ROWS), :] = bu[:, :S5_NS]
        im_ref[pl.ds(r0, S5_ROWS), :] = bu[:, S5_NS:]
        return carry

    lax.fori_loop(0, L // S5_ROWS, step, 0)


def _s5_specs(L):
    col = pl.BlockSpec((L, LANES), lambda j: (0, j))
    bm = pl.BlockSpec((None, LANES, 2 * S5_NS), lambda j: (j, 0, 0))
    cm = pl.BlockSpec((None, 2 * S5_NS, LANES), lambda j: (j, 0, 0))
    pw = pl.BlockSpec((None, SUBLANES, S5_NS), lambda j: (j, 0, 0))
    vec = pl.BlockSpec((1, LANES), lambda j: (0, j))
    return col, bm, cm, pw, vec


def _s5_fwd(u, bmat, cmat, pw_re, pw_im, dsk):
    L = u.shape[0]
    col, bm, cm, pw, vec = _s5_specs(L)

    def body(u_ref, bm_ref, cm_ref, pr_ref, pi_ref, d_ref, y_ref, re_scr, im_scr):
        _s5_project_in(u_ref, bm_ref, re_scr, im_scr, L)
        _s5_scan(re_scr, im_scr, pr_ref[...], pi_ref[...], L, False)

        def step(i, carry):
            r0 = pl.multiple_of(i * S5_ROWS, S5_ROWS)
            rows = pl.ds(r0, S5_ROWS)
            y = _dot(re_scr[rows, :], cm_ref[:S5_NS, :]) + _dot(im_scr[rows, :], cm_ref[S5_NS:, :]) + d_ref[...] * u_ref[rows, :]
            y_ref[rows, :] = _gelu(y)
            return carry

        lax.fori_loop(0, L // S5_ROWS, step, 0)

    return pl.pallas_call(
        body, name="s5_fwd", grid=(S5_BLOCKS,),
        in_specs=[col, bm, cm, pw, pw, vec], out_specs=col,
        out_shape=jax.ShapeDtypeStruct((L, D_MODEL), F32),
        scratch_shapes=[pltpu.VMEM((L, S5_NS), F32)] * 2,
        compiler_params=_params("parallel"),
    )(u, bmat, cmat, pw_re, pw_im, dsk)


def _s5_bwd(u, bmat, cmat, pw_re, pw_im, dsk, dyg):
    L = u.shape[0]
    col, bm, cm, pw, vec = _s5_specs(L)

    def body(u_ref, bm_ref, cm_ref, pr_ref, pi_ref, d_ref, dy_ref, du_ref, dbm_ref, dcm_ref, dlam_ref, dd_ref,
             re_scr, im_scr, gr_scr, gi_scr, dyp_scr):
        _s5_project_in(u_ref, bm_ref, re_scr, im_scr, L)
        _s5_scan(re_scr, im_scr, pr_ref[...], pi_ref[...], L, False)

        def step(i, carry):
            dcr, dci, dd = carry
            r0 = pl.multiple_of(i * S5_ROWS, S5_ROWS)
            rows = pl.ds(r0, S5_ROWS)
            sr, si, uu = re_scr[rows, :], im_scr[rows, :], u_ref[rows, :]
            y = _dot(sr, cm_ref[:S5_NS, :]) + _dot(si, cm_ref[S5_NS:, :]) + d_ref[...] * uu
            dyp = dy_ref[rows, :] * _gelu_grad(y)
            dyp_scr[rows, :] = dyp
            gr_scr[rows, :] = _nt(dyp, cm_ref[:S5_NS, :])
            gi_scr[rows, :] = _nt(dyp, cm_ref[S5_NS:, :])
            return dcr + _tn(sr, dyp), dci + _tn(si, dyp), dd + jnp.sum(dyp * uu, axis=0, keepdims=True)

        zc = jnp.zeros((S5_NS, LANES), F32)
        dcr, dci, dd = lax.fori_loop(0, L // S5_ROWS, step, (zc, zc, jnp.zeros((1, LANES), F32)))
        dcm_ref[:S5_NS, :] = dcr
        dcm_ref[S5_NS:, :] = dci
        dd_ref[...] = dd

        ar, ai = _s5_scan(gr_scr, gi_scr, pr_ref[...], -pi_ref[...], L, True, states=(re_scr, im_scr))
        dlam_ref[...] = jnp.concatenate([ar, ai, jnp.zeros((SUBLANES - 2, S5_NS), F32)], axis=0)

        def in_step(i, carry):
            dbr, dbi = carry
            r0 = pl.multiple_of(i * S5_ROWS, S5_ROWS)
            rows = pl.ds(r0, S5_ROWS)
            gr, gi, uu = gr_scr[rows, :], gi_scr[rows, :], u_ref[rows, :]
            du_ref[rows, :] = dyp_scr[rows, :] * d_ref[...] + _nt(gr, bm_ref[:, :S5_NS]) + _nt(gi, bm_ref[:, S5_NS:])
            return dbr + _tn(uu, gr), dbi + _tn(uu, gi)

        zb = jnp.zeros((LANES, S5_NS), F32)
        dbr, dbi = lax.fori_loop(0, L // S5_ROWS, in_step, (zb, zb))
        dbm_ref[:, :S5_NS] = dbr
        dbm_ref[:, S5_NS:] = dbi

    return pl.pallas_call(
        body, name="s5_bwd", grid=(S5_BLOCKS,),
        in_specs=[col, bm, cm, pw, pw, vec, col], out_specs=[col, bm, cm, pw, vec],
        out_shape=[jax.ShapeDtypeStruct((L, D_MODEL), F32), jax.ShapeDtypeStruct((S5_BLOCKS, LANES, 2 * S5_NS), F32),
                   jax.ShapeDtypeStruct((S5_BLOCKS, 2 * S5_NS, LANES), F32),
                   jax.ShapeDtypeStruct((S5_BLOCKS, SUBLANES, S5_NS), F32), jax.ShapeDtypeStruct((1, D_MODEL), F32)],
        scratch_shapes=[pltpu.VMEM((L, S5_NS), F32)] * 4 + [pltpu.VMEM((L, LANES), F32)],
        compiler_params=_params("parallel"),
    )(u, bmat, cmat, pw_re, pw_im, dsk, dyg)


def _s5_discretize(lam_re, lam_im, log_dt, b_re, b_im, e16):
    dt = jnp.exp(log_dt)
    zr, zi = lam_re * dt, lam_im * dt
    mag = jnp.exp(zr)
    lbr, lbi = mag * jnp.cos(zi), mag * jnp.sin(zi)
    den = lam_re * lam_re + lam_im * lam_im
    nr, ni = lbr - 1.0, lbi
    cr = (nr * lam_re + ni * lam_im) / den
    ci = (ni * lam_re - nr * lam_im) / den
    crw, ciw = _pick(cr, e16), _pick(ci, e16)
    return lbr, lbi, crw * b_re - ciw * b_im, crw * b_im + ciw * b_re


def _s5_params_fwd(lam_re, lam_im, log_dt, b_re, b_im, e16):
    def body(lr, li, ld, br, bi, e, o1, o2, o3, o4):
        for o, val in zip((o1, o2, o3, o4), _s5_discretize(lr[...], li[...], ld[...], br[...], bi[...], e[...])):
            o[...] = val

    g, p, n = S5_GROUPS, S5_STATE, S5_STATE * S5_GROUP
    return pl.pallas_call(
        body, name="s5_params_fwd",
        out_shape=[jax.ShapeDtypeStruct((g, p), F32)] * 2 + [jax.ShapeDtypeStruct((g, n), F32)] * 2,
        compiler_params=_params(),
    )(lam_re, lam_im, log_dt, b_re, b_im, e16)


def _s5_params_bwd(lam_re, lam_im, log_dt, b_re, b_im, e16, cts):
    def body(lr, li, ld, br, bi, e, c1, c2, c3, c4, o1, o2, o3, o4, o5):
        e_m = e[...]
        f = lambda a, b, c, d, g: _s5_discretize(a, b, c, d, g, e_m)
        _, vjp = jax.vjp(f, lr[...], li[...], ld[...], br[...], bi[...])
        for o, val in zip((o1, o2, o3, o4, o5), vjp((c1[...], c2[...], c3[...], c4[...]))):
            o[...] = val

    g, p, n = S5_GROUPS, S5_STATE, S5_STATE * S5_GROUP
    return pl.pallas_call(
        body, name="s5_params_bwd",
        out_shape=[jax.ShapeDtypeStruct((g, p), F32)] * 2 + [jax.ShapeDtypeStruct((g, 1), F32)]
        + [jax.ShapeDtypeStruct((g, n), F32)] * 2,
        compiler_params=_params(),
    )(lam_re, lam_im, log_dt, b_re, b_im, e16, *cts)


def _add_residual(acc, h):
    return (acc + h,)


def _mlp_fwd(i, h, g, w1, w2):
    hn = _rms_fwd(f"mlp{i}_norm", h, g)
    r = _mm(f"mlp{i}_up", hn, w1, "nn", (BF16,), epi=lambda acc: (jnp.square(jnp.maximum(acc, 0.0)),))
    return _mm(f"mlp{i}_down", r, w2, "nn", (F32,), epi=_add_residual, extras=(h,)), (h, hn, r)


def _mlp_bwd(i, dh_out, saved, g, w1, w2, dw):
    h, hn, r = saved
    dw(("mlp_w2", i), f"mlp{i}_dw2", r, dh_out)
    da = _mm(f"mlp{i}_da", dh_out, w2, "nt", (BF16,), epi=lambda acc, rr: (acc * (2.0 * jnp.sqrt(rr.astype(F32))),), extras=(r,))
    dw(("mlp_w1", i), f"mlp{i}_dw1", hn, da)
    dhn = _mm(f"mlp{i}_dhn", da, w1, "nt", (F32,))
    dh, dg = _rms_bwd(f"mlp{i}_dnorm", h, g, dhn, dh_out)
    return dh, dg[0]


def _lanes(v, n):
    return jnp.broadcast_to(v.reshape(n, 1, 1), (n, 1, LANES))


def _gdn_fwd_layer(i, h, g, w_in, conv_w, a_log, dt_bias, o_g, w_out):
    hn = _rms_fwd(f"gdn{i}_norm", h, g)
    p = _mm(f"gdn{i}_in", hn, w_in, "nn", (F32,))
    qkv = _conv_fwd(f"gdn{i}_conv", p, 0, conv_w, jnp.zeros((1, 3 * D_MODEL), F32))
    ea, eb = _gdn_selectors()
    y, sprev, sol = _gdn_fwd(qkv, p, _lanes(a_log, GDN_HEADS), _lanes(dt_bias, GDN_HEADS), o_g.reshape(1, LANES), ea, eb)
    return _mm(f"gdn{i}_out", y, w_out, "nn", (F32,), epi=_add_residual, extras=(h,)), (h, hn, p, qkv, y, sprev, sol)


def _gdn_bwd_layer(i, dh_out, saved, g, w_in, conv_w, a_log, dt_bias, o_g, w_out, dw):
    h, hn, p, qkv, y, sprev, sol = saved
    dy = _mm(f"gdn{i}_dy", dh_out, w_out, "nt", (F32,))
    dw(("gdn_w_out", MIXER_INDEX[i]), f"gdn{i}_dwout", y, dh_out)
    ea, eb = _gdn_selectors()
    dqkv, dgate, dab, dpar = _gdn_bwd(qkv, p, _lanes(a_log, GDN_HEADS), _lanes(dt_bias, GDN_HEADS),
                                      o_g.reshape(1, LANES), ea, eb, sprev, sol, dy)
    dpre, dcw, _ = _conv_bwd(f"gdn{i}_dconv", p, 0, conv_w, jnp.zeros((1, 3 * D_MODEL), F32), dqkv)
    dp = jnp.concatenate([dpre, dgate, dab], axis=1).astype(BF16)
    dw(("gdn_w_in", MIXER_INDEX[i]), f"gdn{i}_dwin", hn, dp)
    dhn = _mm(f"gdn{i}_dhn", dp, w_in, "nt", (F32,))
    dh, dg = _rms_bwd(f"gdn{i}_dnorm", h, g, dhn, dh_out)
    grads = dict(conv_w=dcw, a_log=jnp.sum(dpar[:, 0, :], axis=-1),
                 dt_bias=jnp.sum(dpar[:, 1, :], axis=-1), o_norm_g=jnp.sum(dpar[:, 2, :], axis=0))
    return dh, dg[0], grads


def _m2_vectors(dt_bias, a_log, d_skip, norm_g):
    pad = lambda v: jnp.pad(v, (0, LANES - M2_HEADS)).reshape(1, LANES)
    return pad(dt_bias), pad(a_log), jnp.repeat(d_skip, M2_HD).reshape(1, M2_INNER), norm_g.reshape(1, M2_INNER)


def _m2_fwd_layer(h, g, w_in, conv_w, conv_b, dt_bias, a_log, d_skip, norm_g, w_out):
    hn = _rms_fwd("m2_norm", h, g)
    p = _mm("m2_in", hn, w_in, "nn", (F32,))
    xbc = _conv_fwd("m2_conv", p, M2_INNER, conv_w, conv_b.reshape(1, M2_CONV_CH))
    e, ecol = _m2_selectors()
    y, sprev = _m2_fwd(xbc, p, *_m2_vectors(dt_bias, a_log, d_skip, norm_g), e, ecol)
    return _mm("m2_out", y, w_out, "nn", (F32,), epi=_add_residual, extras=(h,)), (h, hn, p, xbc, y, sprev)


def _m2_bwd_layer(dh_out, saved, g, w_in, conv_w, conv_b, dt_bias, a_log, d_skip, norm_g, w_out, dw):
    h, hn, p, xbc, y, sprev = saved
    dy = _mm("m2_dy", dh_out, w_out, "nt", (F32,))
    dw(("m2_w_out", 0), "m2_dwout", y, dh_out)
    e, ecol = _m2_selectors()
    dx, db, dc, dz, ddt, dnar, dwide = _m2_bwd(xbc, p, *_m2_vectors(dt_bias, a_log, d_skip, norm_g), e, ecol, sprev, dy)
    dxbc, dcw, dcb = _conv_bwd("m2_dconv", p, M2_INNER, conv_w, conv_b.reshape(1, M2_CONV_CH),
                               jnp.concatenate([dx, db, dc], axis=1))
    dp = jnp.concatenate([dz, dxbc, ddt], axis=1).astype(BF16)
    dw(("m2_w_in", 0), "m2_dwin", hn, dp)
    dhn = _mm("m2_dhn", dp, w_in, "nt", (F32,))
    dh, dg = _rms_bwd("m2_dnorm", h, g, dhn, dh_out)
    grads = dict(conv_w=dcw, conv_b=dcb[0], dt_bias=dnar[0, :M2_HEADS], a_log=dnar[1, :M2_HEADS],
                 d=jnp.sum(dwide[:, 0, :].reshape(M2_HEADS, M2_HD), axis=-1), norm_g=dwide[:, 1, :].reshape(M2_INNER))
    return dh, dg[0], grads


def _s5_selector():
    e16 = np.zeros((S5_STATE, S5_STATE * S5_GROUP), np.float32)
    for p in range(S5_STATE):
        e16[p, p * S5_GROUP:(p + 1) * S5_GROUP] = 1.0
    return jnp.asarray(e16)


def _s5_operands(lbr, lbi, bbr, bbi, c_re, c_im):
    eye = jnp.eye(S5_BLOCKS, dtype=F32)
    gpb = S5_GROUPS // S5_BLOCKS
    bd = lambda t: jnp.einsum("jgpk,gh->jgkhp", t.reshape(S5_BLOCKS, gpb, S5_STATE, S5_GROUP), eye).reshape(S5_BLOCKS, LANES, S5_NS)
    cd = lambda t: jnp.einsum("jgkp,gh->jgphk", t.reshape(S5_BLOCKS, gpb, S5_GROUP, S5_STATE), eye).reshape(S5_BLOCKS, S5_NS, LANES)
    bmat = jnp.concatenate([bd(bbr), bd(bbi)], axis=2).astype(BF16)
    cmat = jnp.concatenate([cd(c_re), -cd(c_im)], axis=1).astype(BF16)
    ar, ai = lbr.reshape(S5_BLOCKS, S5_NS), lbi.reshape(S5_BLOCKS, S5_NS)
    pr, pi = [ar], [ai]
    for _ in range(SUBLANES - 1):
        pr, pi = pr + [pr[-1] * ar - pi[-1] * ai], pi + [pr[-1] * ai + pi[-1] * ar]
    return bmat, cmat, jnp.stack(pr, axis=1), jnp.stack(pi, axis=1)


def _s5_fwd_layer(h, g, w_in, lam_re, lam_im, log_dt, b_re, b_im, c_re, c_im, d_skip, w_out):
    hn = _rms_fwd("s5_norm", h, g)
    u = _mm("s5_in", hn, w_in, "nn", (F32,))
    n = S5_STATE * S5_GROUP
    lbr, lbi, bbr, bbi = _s5_params_fwd(lam_re, lam_im, log_dt.reshape(S5_GROUPS, 1), b_re.reshape(S5_GROUPS, n),
                                        b_im.reshape(S5_GROUPS, n), _s5_selector())
    ops = _s5_operands(lbr, lbi, bbr, bbi, c_re, c_im)
    yg = _s5_fwd(u, *ops, d_skip.reshape(1, D_MODEL))
    ag = _mm("s5_out", yg, w_out, "nn", (F32,))
    return _glu_fwd(h, ag), (h, hn, u, ops, yg, ag)


def _s5_bwd_layer(dh_out, saved, g, w_in, lam_re, lam_im, log_dt, b_re, b_im, c_re, c_im, d_skip, w_out, dw):
    h, hn, u, ops, yg, ag = saved
    dag = _glu_bwd(dh_out, ag)
    dw(("s5_w_out", 0), "s5_dwout", yg, dag)
    dyg = _mm("s5_dyg", dag, w_out, "nt", (F32,))
    du, dbmat, dcmat, dlam, ddsk = _s5_bwd(u, *ops, d_skip.reshape(1, D_MODEL), dyg)
    eye = jnp.eye(S5_BLOCKS, dtype=F32)
    gpb = S5_GROUPS // S5_BLOCKS
    n = S5_STATE * S5_GROUP
    ub = lambda t: jnp.einsum("jgkhp,gh->jgpk", t.reshape(S5_BLOCKS, gpb, S5_GROUP, gpb, S5_STATE), eye).reshape(S5_GROUPS, n)
    uc = lambda t: jnp.einsum("jgphk,gh->jgkp", t.reshape(S5_BLOCKS, gpb, S5_STATE, gpb, S5_GROUP), eye).reshape(c_re.shape)
    cts = (dlam[:, 0, :].reshape(S5_GROUPS, S5_STATE), dlam[:, 1, :].reshape(S5_GROUPS, S5_STATE),
           ub(dbmat[:, :, :S5_NS]), ub(dbmat[:, :, S5_NS:]))
    dlr, dli, dld, dbr, dbi = _s5_params_bwd(lam_re, lam_im, log_dt.reshape(S5_GROUPS, 1), b_re.reshape(S5_GROUPS, n),
                                             b_im.reshape(S5_GROUPS, n), _s5_selector(), cts)
    dw(("s5_w_in", 0), "s5_dwin", hn, du)
    dhn = _mm("s5_dhn", du, w_in, "nt", (F32,))
    dh, dg = _rms_bwd("s5_dnorm", h, g, dhn, dh_out)
    grads = dict(lam_re=dlr, lam_im=dli, log_dt=dld[:, 0], b_re=dbr.reshape(b_re.shape), b_im=dbi.reshape(b_im.shape),
                 c_re=uc(dcmat[:, :S5_NS, :]), c_im=-uc(dcmat[:, S5_NS:, :]), d=ddsk[0])
    return dh, dg[0], grads


MIXER_OF_LAYER = ("gdn", "s5", "m2", "gdn")
MIXER_INDEX = (0, 0, 0, 1)


def _mixer_args(W, i):
    kind, j = MIXER_OF_LAYER[i], MIXER_INDEX[i]
    if kind == "gdn":
        return tuple(W["gdn_" + k][j] for k in ("w_in", "conv_w", "a_log", "dt_bias", "o_norm_g", "w_out"))
    if kind == "s5":
        return tuple(W["s5_" + k][j] for k in ("w_in", "lam_re", "lam_im", "log_dt", "b_re", "b_im", "c_re", "c_im", "d", "w_out"))
    return tuple(W["m2_" + k][j] for k in ("w_in", "conv_w", "conv_b", "dt_bias", "a_log", "d", "norm_g", "w_out"))


def _local_step(x, target, W, dw, on_layer_done):
    h = x
    saved = []
    for i in range(DEPTH):
        kind = MIXER_OF_LAYER[i]
        args = _mixer_args(W, i)
        if kind == "gdn":
            h, sm = _gdn_fwd_layer(i, h, W["norm_mix_g"][i], *args)
        elif kind == "s5":
            h, sm = _s5_fwd_layer(h, W["norm_mix_g"][i], *args)
        else:
            h, sm = _m2_fwd_layer(h, W["norm_mix_g"][i], *args)
        h, sp = _mlp_fwd(i, h, W["norm_mlp_g"][i], W["mlp_w1"][i], W["mlp_w2"][i])
        saved.append((sm, sp))
    loss, dh, dgf = _loss_head(h, W["final_norm_g"], target)
    G = {"final_norm_g": dgf[0], "norm_mix_g": [None] * DEPTH, "norm_mlp_g": [None] * DEPTH}
    mix = {}
    for i in reversed(range(DEPTH)):
        kind = MIXER_OF_LAYER[i]
        sm, sp = saved[i]
        dh, G["norm_mlp_g"][i] = _mlp_bwd(i, dh, sp, W["norm_mlp_g"][i], W["mlp_w1"][i], W["mlp_w2"][i], dw)
        args = _mixer_args(W, i)
        if kind == "gdn":
            dh, G["norm_mix_g"][i], gm = _gdn_bwd_layer(i, dh, sm, W["norm_mix_g"][i], *args, dw)
        elif kind == "s5":
            dh, G["norm_mix_g"][i], gm = _s5_bwd_layer(dh, sm, W["norm_mix_g"][i], *args, dw)
        else:
            dh, G["norm_mix_g"][i], gm = _m2_bwd_layer(dh, sm, W["norm_mix_g"][i], *args, dw)
        j = MIXER_INDEX[i]
        on_layer_done(i)
        for k, v in gm.items():
            mix.setdefault(kind + "_" + k, {})[j] = v
    for k, d in mix.items():
        G[k] = [d[j] for j in sorted(d)]
    return loss, dh, {k: jnp.stack(v) if isinstance(v, list) else v for k, v in G.items()}


ADAM_ROWS = 128
ADAM_COLS = 128


def _adamw(name, w, g, m, v):
    R, C = w.shape
    if R % ADAM_ROWS == 0:
        grid, blk = (R // ADAM_ROWS,), pl.BlockSpec((ADAM_ROWS, C), lambda i: (i, 0))
    else:
        grid, blk = (C // ADAM_COLS,), pl.BlockSpec((R, ADAM_COLS), lambda j: (0, j))

    def body(w_ref, g_ref, m_ref, v_ref, d_ref, mo_ref, vo_ref):
        gg = g_ref[...]
        mn = ADAM_B1 * m_ref[...] + (1.0 - ADAM_B1) * gg
        vn = ADAM_B2 * v_ref[...] + (1.0 - ADAM_B2) * (gg * gg)
        m_hat = mn / (1.0 - ADAM_B1 ** ADAM_STEP)
        v_hat = vn / (1.0 - ADAM_B2 ** ADAM_STEP)
        d_ref[...] = -ADAM_LR * (m_hat / (jnp.sqrt(v_hat) + ADAM_EPS) + ADAM_WD * w_ref[...])
        mo_ref[...] = mn
        vo_ref[...] = vn

    return pl.pallas_call(
        body, name=name, grid=grid, in_specs=[blk] * 4, out_specs=[blk] * 3,
        out_shape=[jax.ShapeDtypeStruct((R, C), F32)] * 3, compiler_params=_params("parallel"),
    )(w, g, m, v)


MESH = pl.DeviceIdType.MESH
ANY = pl.BlockSpec(memory_space=pl.ANY)
N_CHIPS = 4
N_DEV = 8


def _position():
    return lax.axis_index("x"), lax.axis_index("y"), lax.axis_index("c")


GATHER_IDS = {1: 1, 2: 2}
EXCHANGE_IDS = {0: 4, 1: 5, 2: 6, 3: 7}


LINK_SLOWDOWN = 40


def _link_cost(link_bytes):
    return pl.CostEstimate(flops=0, transcendentals=0, bytes_accessed=LINK_SLOWDOWN * link_bytes)


def _gather_body(w_refs, out_refs, send_sems, recv_sems):
    x, y, c = _position()
    sibling = (x, y, 1 - c)
    chips = [(1 - x, y), (x, 1 - y), (1 - x, 1 - y)]
    firsts, passes = [], []
    for t, (w_ref, out_ref) in enumerate(zip(w_refs, out_refs)):
        half = w_ref.shape[0] // 2

        def piece(cx, cy, hc, out_ref=out_ref, half=half):
            return out_ref.at[2 * cx + cy, pl.ds(hc * half, half), :]

        def copy(k, src, dst, to, t=t):
            return pltpu.make_async_remote_copy(src_ref=src, dst_ref=dst, send_sem=send_sems.at[6 * t + k],
                                                recv_sem=recv_sems.at[6 * t + k], device_id=to, device_id_type=MESH)

        first = [copy(j, w_ref.at[pl.ds(c * half, half), :], piece(x, y, c), (*chip, c)) for j, chip in enumerate(chips)]
        for cp in first:
            cp.start()
        firsts.append((first, piece, copy))
    for first, piece, copy in firsts:
        passed = [copy(3 + j, piece(*chip, c), piece(*chip, c), sibling) for j, chip in enumerate(chips)]
        for j, chip in enumerate(chips):
            copy(j, piece(*chip, c), piece(*chip, c), sibling).wait_recv()
            passed[j].start()
        passes.append(passed)
    for (first, piece, copy), passed in zip(firsts, passes):
        for j, chip in enumerate(chips):
            copy(3 + j, piece(*chip, 1 - c), piece(*chip, 1 - c), sibling).wait_recv()
        for cp in first + passed:
            cp.wait_send()


def _gather_shards(wps):
    n = len(wps)

    def body(*refs):
        _gather_body(refs[:n], refs[n:2 * n], *refs[2 * n:])

    return pl.pallas_call(
        body, name="gather_shards", in_specs=[ANY] * n, out_specs=[ANY] * n,
        out_shape=[jax.ShapeDtypeStruct((N_CHIPS, *wp.shape), wp.dtype) for wp in wps],
        scratch_shapes=[pltpu.SemaphoreType.DMA((6 * n,)), pltpu.SemaphoreType.DMA((6 * n,))],
    )(*wps)


def _gather_shards_later(wps, part):
    n = len(wps)
    w_refs = [jax.new_ref(wp, memory_space=pltpu.MemorySpace.HBM) for wp in wps]
    out_refs = [jax.empty_ref(jax.ShapeDtypeStruct((N_CHIPS, *wp.shape), wp.dtype), memory_space=pltpu.MemorySpace.HBM)
                for wp in wps]

    @pl.kernel(mesh=plsc.ScalarSubcoreMesh(axis_name="sequencer", num_cores=1), name=f"gather_shards_later{part}",
               scratch_types=(pltpu.SemaphoreType.DMA((6 * n,)), pltpu.SemaphoreType.DMA((6 * n,))),
               cost_estimate=_link_cost(3 * sum(wp.size * wp.dtype.itemsize for wp in wps)),
               compiler_params=pltpu.CompilerParams(collective_id=GATHER_IDS[part]))
    def launch(send_sems, recv_sems):
        x, y, c = _position()
        barrier = pltpu.get_barrier_semaphore()
        for peer in [(x, y, 1 - c), (1 - x, y, c), (x, 1 - y, c), (1 - x, 1 - y, c)]:
            pl.semaphore_signal(barrier, inc=1, device_id=peer, device_id_type=MESH)
        pl.semaphore_wait(barrier, 4)
        _gather_body(w_refs, out_refs, send_sems, recv_sems)

    launch()
    return [r[...] for r in out_refs]


def _pair_exchange(name, gps):
    n = len(gps)

    def body(*refs):
        g_refs, out_refs, (send_sems, recv_sems) = refs[:n], refs[n:2 * n], refs[2 * n:]
        x, y, c = _position()
        copies = []
        for t, (g_ref, out_ref) in enumerate(zip(g_refs, out_refs)):
            half = g_ref.shape[1] // 2
            copies += [pltpu.make_async_remote_copy(
                src_ref=g_ref.at[k, pl.ds((1 - c) * half, half), :], dst_ref=out_ref.at[k], send_sem=send_sems.at[N_CHIPS * t + k],
                recv_sem=recv_sems.at[N_CHIPS * t + k], device_id=(x, y, 1 - c), device_id_type=MESH) for k in range(N_CHIPS)]
        for cp in copies:
            cp.start()
        for cp in copies:
            cp.wait()

    return pl.pallas_call(
        body, name=name, in_specs=[ANY] * n, out_specs=[ANY] * n,
        out_shape=[jax.ShapeDtypeStruct((N_CHIPS, gp.shape[1] // 2, gp.shape[2]), gp.dtype) for gp in gps],
        scratch_shapes=[pltpu.SemaphoreType.DMA((N_CHIPS * n,)), pltpu.SemaphoreType.DMA((N_CHIPS * n,))],
    )(*gps)


SUM_ROWS = (1280, 1152, 1024, 512, 256, 128)


def _pair_sum(name, gp, got, core):
    n, R, C = gp.shape
    half = R // 2
    tr = _tile(half, SUM_ROWS)
    nb = half // tr

    def body(core_ref, g_ref, r_ref, o_ref):
        o_ref[...] = (g_ref[...].astype(F32) + r_ref[...].astype(F32)).astype(o_ref.dtype)

    return pl.pallas_call(
        body, name=name,
        grid_spec=pltpu.PrefetchScalarGridSpec(
            num_scalar_prefetch=1, grid=(n, nb),
            in_specs=[pl.BlockSpec((None, tr, C), lambda k, i, core_ref: (k, core_ref[0] * nb + i, 0)),
                      pl.BlockSpec((None, tr, C), lambda k, i, core_ref: (k, i, 0))],
            out_specs=pl.BlockSpec((None, tr, C), lambda k, i, core_ref: (k, i, 0))),
        out_shape=jax.ShapeDtypeStruct((n, half, C), gp.dtype), compiler_params=_params("parallel", "parallel"),
    )(core, gp, got)


def _chip_exchange_body(t_refs, out_refs, send_sems, recv_sems):
    x, y, c = _position()
    chips = [(1 - x, y), (x, 1 - y), (1 - x, 1 - y)]
    copies, waits = [], []
    for t, (t_ref, out_ref) in enumerate(zip(t_refs, out_refs)):
        for j, (cx, cy) in enumerate(chips):
            sems = dict(send_sem=send_sems.at[3 * t + j], recv_sem=recv_sems.at[3 * t + j], device_id=(cx, cy, c),
                        device_id_type=MESH)
            copies.append(pltpu.make_async_remote_copy(src_ref=t_ref.at[2 * cx + cy], dst_ref=out_ref.at[2 * x + y], **sems))
            waits.append(pltpu.make_async_remote_copy(src_ref=t_ref.at[2 * cx + cy], dst_ref=out_ref.at[2 * cx + cy], **sems))
    for cp in copies:
        cp.start()
    for cp in waits:
        cp.wait_recv()
    for cp in copies:
        cp.wait_send()


def _chip_exchange_later(ts, layer):
    n = len(ts)
    t_refs = [jax.new_ref(t, memory_space=pltpu.MemorySpace.HBM) for t in ts]
    out_refs = [jax.empty_ref(jax.ShapeDtypeStruct(t.shape, t.dtype), memory_space=pltpu.MemorySpace.HBM) for t in ts]

    @pl.kernel(mesh=plsc.ScalarSubcoreMesh(axis_name="sequencer", num_cores=1), name=f"chip_exchange_later{layer}",
               scratch_types=(pltpu.SemaphoreType.DMA((3 * n,)), pltpu.SemaphoreType.DMA((3 * n,))),
               cost_estimate=_link_cost(3 * sum(t.size * t.dtype.itemsize for t in ts) // N_CHIPS),
               compiler_params=pltpu.CompilerParams(collective_id=EXCHANGE_IDS[layer]))
    def launch(send_sems, recv_sems):
        x, y, c = _position()
        barrier = pltpu.get_barrier_semaphore()
        for peer in [(1 - x, y, c), (x, 1 - y, c), (1 - x, 1 - y, c)]:
            pl.semaphore_signal(barrier, inc=1, device_id=peer, device_id_type=MESH)
        pl.semaphore_wait(barrier, 3)
        _chip_exchange_body(t_refs, out_refs, send_sems, recv_sems)

    launch()
    return [r[...] for r in out_refs]


def _chip_sum(name, t, got, ids):
    n, H, C = t.shape
    tr = _tile(H, SUM_ROWS)
    nb = H // tr

    def body(ids_ref, t_ref, r_ref, o_ref):
        own = t_ref[...].astype(F32)
        acc = jnp.where(ids_ref[0] == 0, own, r_ref[0].astype(F32))
        for k in range(1, n):
            acc = acc + jnp.where(ids_ref[0] == k, own, r_ref[k].astype(F32))
        o_ref[...] = acc

    return pl.pallas_call(
        body, name=name,
        grid_spec=pltpu.PrefetchScalarGridSpec(
            num_scalar_prefetch=1, grid=(nb,),
            in_specs=[pl.BlockSpec((None, tr, C), lambda i, ids_ref: (ids_ref[0], i, 0)),
                      pl.BlockSpec((n, tr, C), lambda i, ids_ref: (0, i, 0))],
            out_specs=pl.BlockSpec((tr, C), lambda i, ids_ref: (ids_ref[1] * nb + i, 0))),
        out_shape=jax.ShapeDtypeStruct((2 * H, C), F32), compiler_params=_params("parallel"),
    )(ids, t, got)


def _sum_pieces(name, pieces):
    n, R, C = pieces.shape
    tr = _tile(R, (256, 128, SUBLANES))

    def body(p_ref, o_ref):
        acc = p_ref[0].astype(F32)
        for s in range(1, n):
            acc = acc + p_ref[s].astype(F32)
        o_ref[...] = acc

    return pl.pallas_call(
        body, name=name, grid=(R // tr,),
        in_specs=[pl.BlockSpec((n, tr, C), lambda i: (0, i, 0))], out_specs=pl.BlockSpec((tr, C), lambda i: (i, 0)),
        out_shape=jax.ShapeDtypeStruct((R, C), F32), compiler_params=_params("parallel"),
    )(pieces)


def _swap_halves(name, ss):
    n = len(ss)

    def body(*refs):
        s_refs, out_refs, (send_sems, recv_sems) = refs[:n], refs[n:2 * n], refs[2 * n:]
        x, y, c = _position()
        copies, waits = [], []
        for t, (s_ref, out_ref) in enumerate(zip(s_refs, out_refs)):
            half = s_ref.shape[0] // 2
            sems = dict(send_sem=send_sems.at[t], recv_sem=recv_sems.at[t], device_id=(x, y, 1 - c), device_id_type=MESH)
            mine = s_ref.at[pl.ds(c * half, half), :]
            copies.append(pltpu.make_async_remote_copy(src_ref=mine, dst_ref=out_ref.at[pl.ds(c * half, half), :], **sems))
            waits.append(pltpu.make_async_remote_copy(src_ref=mine, dst_ref=out_ref.at[pl.ds((1 - c) * half, half), :], **sems))
        for cp in copies:
            cp.start()
        for cp in waits:
            cp.wait_recv()
        for cp in copies:
            cp.wait_send()

    return pl.pallas_call(
        body, name=name, in_specs=[ANY] * n, out_specs=[ANY] * n, input_output_aliases={i: i for i in range(n)},
        out_shape=[jax.ShapeDtypeStruct(s_.shape, s_.dtype) for s_ in ss],
        scratch_shapes=[pltpu.SemaphoreType.DMA((n,)), pltpu.SemaphoreType.DMA((n,))],
    )(*ss)


def _gather_small(name, blk):
    m_per, n = blk.shape

    def body(x_ref, out_ref, send_sems, recv_sems, local_sem):
        x, y, c = _position()
        me, sibling = (x, y, c), (x, y, 1 - c)
        chips = [(1 - x, y), (x, 1 - y), (1 - x, 1 - y)]

        def rows(px, py, pc):
            return out_ref.at[pl.ds((4 * px + 2 * py + pc) * m_per, m_per), :]

        def copy(k, block, to, src=None):
            return pltpu.make_async_remote_copy(src_ref=rows(*block) if src is None else src, dst_ref=rows(*block),
                                                send_sem=send_sems.at[k], recv_sem=recv_sems.at[k], device_id=to, device_id_type=MESH)

        mine = pltpu.make_async_copy(x_ref, rows(*me), local_sem)
        mine.start()
        first = [copy(0, me, sibling, src=x_ref)] + [copy(1 + j, me, (*chip, c), src=x_ref) for j, chip in enumerate(chips)]
        for cp in first:
            cp.start()
        passed = [copy(4 + j, (*chip, c), sibling) for j, chip in enumerate(chips)]
        for j, chip in enumerate(chips):
            copy(1 + j, (*chip, c), me).wait_recv()
            passed[j].start()
        copy(0, sibling, me).wait_recv()
        for j, chip in enumerate(chips):
            copy(4 + j, (*chip, 1 - c), me).wait_recv()
        for cp in first + passed:
            cp.wait_send()
        mine.wait()

    return pl.pallas_call(
        body, name=name, out_shape=jax.ShapeDtypeStruct((N_DEV * m_per, n), blk.dtype),
        in_specs=[pl.BlockSpec(memory_space=pltpu.VMEM)], out_specs=pl.BlockSpec(memory_space=pltpu.VMEM),
        scratch_shapes=[pltpu.SemaphoreType.DMA((7,)), pltpu.SemaphoreType.DMA((7,)), pltpu.SemaphoreType.DMA],
        compiler_params=pltpu.CompilerParams(vmem_limit_bytes=VMEM_LIMIT_BYTES),
    )(blk)


WEIGHTS = ("norm_mix_g", "norm_mlp_g", "mlp_w1", "mlp_w2", "gdn_w_in", "gdn_conv_w", "gdn_a_log", "gdn_dt_bias", "gdn_o_norm_g",
           "gdn_w_out", "s5_w_in", "s5_lam_re", "s5_lam_im", "s5_log_dt", "s5_b_re", "s5_b_im", "s5_c_re", "s5_c_im", "s5_d",
           "s5_w_out", "m2_w_in", "m2_conv_w", "m2_conv_b", "m2_dt_bias", "m2_a_log", "m2_d", "m2_norm_g", "m2_w_out",
           "final_norm_g")
BIG = {"mlp_w1": 2, "mlp_w2": 1, "gdn_w_in": 2, "gdn_w_out": 1, "s5_w_in": 1, "s5_w_out": 2, "m2_w_in": 2, "m2_w_out": 1}
SMALL_CUT = {"gdn_conv_w": 2, "m2_conv_w": 2, "m2_conv_b": 1, "m2_norm_g": 1}
ROWS_MINOR = ("m2_w_in",)
ODD_WIDTH = {"gdn_w_in": GDN_IN, "m2_w_in": M2_IN}
WEIGHT_PARTS = (
    ((("gdn_w_out", 0),), (("gdn_w_in", 0),)),
    ((("mlp_w1", 0), ("mlp_w2", 0), ("mlp_w1", 1), ("mlp_w2", 1), ("s5_w_in", 0)), (("s5_w_out", 0),)),
    ((("mlp_w1", 2), ("mlp_w2", 2), ("m2_w_out", 0), ("mlp_w1", 3), ("mlp_w2", 3), ("gdn_w_out", 1)), (("m2_w_in", 0),),
     (("gdn_w_in", 1),)),
)
LAYER_ITEMS = (
    ((("mlp_w1", 0), ("mlp_w2", 0), ("gdn_w_out", 0)), (("gdn_w_in", 0),)),
    ((("mlp_w1", 1), ("mlp_w2", 1), ("s5_w_in", 0)), (("s5_w_out", 0),)),
    ((("mlp_w1", 2), ("mlp_w2", 2), ("m2_w_out", 0)), (("m2_w_in", 0),)),
    ((("mlp_w1", 3), ("mlp_w2", 3), ("gdn_w_out", 1)), (("gdn_w_in", 1),)),
)


def _rows2d(a):
    return a.reshape(-1, a.shape[-1])


def _pack(arrays, cols, row_multiple, dtype):
    flat = jnp.concatenate([a.reshape(-1).astype(dtype) for a in arrays])
    n = -(-flat.shape[0] // (cols * row_multiple)) * cols * row_multiple
    return jnp.pad(flat, (0, n - flat.shape[0])).reshape(-1, cols)


def _unpack(packed, shapes):
    flat = packed.reshape(-1)
    out, off = [], 0
    for shp in shapes:
        n = math.prod(shp)
        out.append(flat[off:off + n].reshape(shp))
        off += n
    return out


def _split_rows(buf, shapes):
    out, off = [], 0
    for shp in shapes:
        rows = math.prod(shp[:-1])
        out.append(buf[off:off + rows].reshape(shp))
        off += rows
    return out


def _cut(a, axis, k):
    n = a.shape[axis] // N_CHIPS
    return lax.slice_in_dim(a, k * n, (k + 1) * n, axis=axis)


def kernel(x, norm_mix_g, norm_mlp_g, mlp_w1, mlp_w2, gdn_w_in, gdn_conv_w, gdn_a_log, gdn_dt_bias, gdn_o_norm_g, gdn_w_out, s5_w_in, s5_lam_re, s5_lam_im, s5_log_dt, s5_b_re, s5_b_im, s5_c_re, s5_c_im, s5_d, s5_w_out, m2_w_in, m2_conv_w, m2_conv_b, m2_dt_bias, m2_a_log, m2_d, m2_norm_g, m2_w_out, final_norm_g, loss_target, m_norm_mix_g, m_norm_mlp_g, m_mlp_w1, m_mlp_w2, m_gdn_w_in, m_gdn_conv_w, m_gdn_a_log, m_gdn_dt_bias, m_gdn_o_norm_g, m_gdn_w_out, m_s5_w_in, m_s5_lam_re, m_s5_lam_im, m_s5_log_dt, m_s5_b_re, m_s5_b_im, m_s5_c_re, m_s5_c_im, m_s5_d, m_s5_w_out, m_m2_w_in, m_m2_conv_w, m_m2_conv_b, m_m2_dt_bias, m_m2_a_log, m_m2_d, m_m2_norm_g, m_m2_w_out, m_final_norm_g, v_norm_mix_g, v_norm_mlp_g, v_mlp_w1, v_mlp_w2, v_gdn_w_in, v_gdn_conv_w, v_gdn_a_log, v_gdn_dt_bias, v_gdn_o_norm_g, v_gdn_w_out, v_s5_w_in, v_s5_lam_re, v_s5_lam_im, v_s5_log_dt, v_s5_b_re, v_s5_b_im, v_s5_c_re, v_s5_c_im, v_s5_d, v_s5_w_out, v_m2_w_in, v_m2_conv_w, v_m2_conv_b, v_m2_dt_bias, v_m2_a_log, v_m2_d, v_m2_norm_g, v_m2_w_out, v_final_norm_g):
    given = dict(locals())
    w = {n: given[n] for n in WEIGHTS}
    mom = {n: given["m_" + n] for n in WEIGHTS}
    var = {n: given["v_" + n] for n in WEIGHTS}
    big, small_cut = tuple(BIG), tuple(SMALL_CUT)
    small = tuple(n for n in WEIGHTS if n not in BIG)
    chip = 2 * lax.axis_index("x") + lax.axis_index("y")

    W = {n: [None] * w[n].shape[0] for n in big}

    def fetch(groups, gather):
        own = [jnp.concatenate([w[n][l] for n, l in grp]).astype(BF16) for grp in groups]
        for grp, mine, got in zip(groups, own, gather(own)):
            shapes = [w[n][l].shape for n, l in grp]
            per_chip = [_split_rows(jnp.where(chip == k, mine, got[k]), shapes) for k in range(N_CHIPS)]
            for i, (n, l) in enumerate(grp):
                m = jnp.concatenate([per_chip[k][i] for k in range(N_CHIPS)], axis=BIG[n] - 1)
                pad = {"gdn_w_in": GDN_IN_PAD - GDN_IN, "m2_w_in": M2_IN_PAD - M2_IN}.get(n, 0)
                W[n][l] = jnp.pad(m, ((0, 0), (0, pad))) if pad else m

    for part in (1, 2):
        fetch(WEIGHT_PARTS[part], functools.partial(_gather_shards_later, part=part))
    fetch(WEIGHT_PARTS[0], _gather_shards)
    cut_blk = _pack([w[n] for n in small_cut], LANES, SUBLANES, F32)
    cut_all = _gather_small("gather_small_params", cut_blk).reshape(N_DEV, *cut_blk.shape)
    per_chip = [_unpack(cut_all[2 * k], [w[n].shape for n in small_cut]) for k in range(N_CHIPS)]
    W.update({n: jnp.concatenate([per_chip[k][i] for k in range(N_CHIPS)], axis=SMALL_CUT[n]) for i, n in enumerate(small_cut)})
    W.update({n: w[n] for n in small if n not in SMALL_CUT})

    core = lax.axis_index("c").astype(jnp.int32)
    ids = jnp.stack([chip.astype(jnp.int32), core])
    shard_grads = {}

    place, comm_bufs, odd = {}, {}, {}
    for layer, groups in enumerate(LAYER_ITEMS):
        for j, grp in enumerate(groups):
            row = 0
            for n, l in grp:
                place[n, l] = (layer, j, row)
                row += w[n].shape[1]
            if grp[0][0] not in ODD_WIDTH:
                comm_bufs[layer, j] = lax.empty((N_CHIPS, row, w[grp[0][0]].shape[2]), BF16)

    def dw(item, name, a, b):
        n = item[0]
        layer, j, row0 = place[item]
        if n in ODD_WIDTH:
            full = _mm(name, a, b, "tn", (BF16,))
            odd[layer, j] = jnp.stack([_cut(full[:, :ODD_WIDTH[n]], 1, k) for k in range(N_CHIPS)])
        else:
            comm_bufs[layer, j] = _dw_into(name, a, b, comm_bufs[layer, j], row0, BIG[n] - 1)

    def reduce_layer(i):
        groups = LAYER_ITEMS[i]
        gps = [comm_bufs[i, j] if (i, j) in comm_bufs else odd[i, j] for j in range(len(groups))]
        pairs = [_pair_sum(f"pair_sum{i}_{j}", gp, got, core.reshape(1))
                 for j, (gp, got) in enumerate(zip(gps, _pair_exchange(f"pair_exchange{i}", gps)))]
        sums = [_chip_sum(f"chip_sum{i}_{j}", t, got, ids) for j, (t, got) in enumerate(zip(pairs, _chip_exchange_later(pairs, i)))]
        for grp, g_shard in zip(groups, _swap_halves(f"swap_halves{i}", sums)):
            shard_grads.update(zip(grp, _split_rows(g_shard, [w[n][l].shape for n, l in grp])))

    loss, grad_x, G = _local_step(x[0], loss_target[0], W, dw, reduce_layer)
    loss = lax.psum(loss[0, 0], ("x", "y", "c"))
    grads = {n: jnp.stack([shard_grads[n, l] for l in range(w[n].shape[0])]) for n in big}
    sg = _pack([G[n] for n in small], LANES, ADAM_ROWS, F32)
    sg_sum = _sum_pieces("sum_small_grads", _gather_small("gather_small_grads", sg).reshape(N_DEV, *sg.shape))
    for n, g in zip(small, _unpack(sg_sum, [G[n].shape for n in small])):
        if n in SMALL_CUT:
            width = g.shape[SMALL_CUT[n]] // N_CHIPS
            g = lax.dynamic_slice_in_dim(g, chip * width, width, axis=SMALL_CUT[n])
        grads[n] = g.reshape(w[n].shape)

    delta, new_m, new_v = {}, {}, {}
    for n in big:
        if n in ROWS_MINOR:
            as2d = lambda a: jnp.swapaxes(a, -1, -2).reshape(-1, a.shape[-2])
            back = lambda o: jnp.swapaxes(o.reshape(w[n].shape[0], w[n].shape[2], w[n].shape[1]), -1, -2)
        else:
            as2d = lambda a: a.reshape(-1, a.shape[-1])
            back = lambda o: o.reshape(w[n].shape)
        outs = _adamw("adamw_" + n, as2d(w[n]), as2d(grads[n]), as2d(mom[n]), as2d(var[n]))
        delta[n], new_m[n], new_v[n] = (back(o) for o in outs)
    packs = [_pack([t[n] for n in small], LANES, ADAM_ROWS, F32) for t in (w, grads, mom, var)]
    outs = _adamw("adamw_small", *packs)
    for t, o in zip((delta, new_m, new_v), outs):
        t.update(zip(small, _unpack(o, [w[n].shape for n in small])))

    return (loss, grad_x[None], *[grads[n] for n in WEIGHTS], *[delta[n] for n in WEIGHTS], *[new_m[n] for n in WEIGHTS],
            *[new_v[n] for n in WEIGHTS])
```
